```python
import math
import jax, jax.numpy as jnp
from jax import lax
import numpy as np

D_MODEL = 1024
BATCH = 8
SEQ = 8192
DEPTH = 1

GRID_W = 64
CTX_LEN = 256

D_INNER = 2 * D_MODEL
HEAD_DIM = 64
N_HEADS = D_INNER // HEAD_DIM
N_GROUPS = 8
HPG = N_HEADS // N_GROUPS
D_STATE = 128
SSM_CONV = 4
SSM_PAD = (2, 1)
CHUNK = 128

D_CONF = D_MODEL
CONF_KERNEL = 31
CONF_PAD = (CONF_KERNEL // 2, CONF_KERNEL // 2)

EPS = 1e-6

GN = N_GROUPS * D_STATE
X_END = D_INNER
B_END = X_END + GN
C_END = B_END + GN
DT_END = C_END + 2 * N_HEADS
Z_END = DT_END + D_INNER
GLU_END = Z_END + 2 * D_CONF
CG_END = GLU_END + D_CONF
IN_COLS = CG_END + 2 * D_MODEL

kernel_name = 'hybrid_ssd_conformer_prefix_block'


def rms_norm(x, w):
    xf = x.astype(jnp.float32)
    y = xf * lax.rsqrt(jnp.mean(xf * xf, axis=-1, keepdims=True) + EPS)
    return y.astype(x.dtype) * w


def group_rms_norm(x, w):
    shp = x.shape
    xg = x.reshape(*shp[:-1], N_GROUPS, shp[-1] // N_GROUPS).astype(jnp.float32)
    y = xg * lax.rsqrt(jnp.mean(xg * xg, axis=-1, keepdims=True) + EPS)
    return y.reshape(shp).astype(x.dtype) * w


def layer_norm(x, w, b):
    xf = x.astype(jnp.float32)
    mu = jnp.mean(xf, axis=-1, keepdims=True)
    var = jnp.mean(jnp.square(xf - mu), axis=-1, keepdims=True)
    return ((xf - mu) * lax.rsqrt(var + EPS)).astype(x.dtype) * w + b


def modulate(h, shift, scale):
    return h * (1 + scale) + shift


def depthwise_conv(u, w, b, pad):
    out = lax.conv_general_dilated(u, w[:, None, :], (1,), [pad],
                                   dimension_numbers=('NWC', 'WIO', 'NWC'),
                                   feature_group_count=u.shape[-1])
    return out + b


def rev(t):
    return jnp.flip(t, axis=1)


def to_chunks(t):
    return t.reshape(t.shape[0], t.shape[1] // CHUNK, CHUNK, *t.shape[2:])


def ssm_dt(dt_raw, dt_bias):
    b, L = dt_raw.shape[:2]
    return jax.nn.softplus(dt_raw.astype(jnp.float32).reshape(b, L, 2, N_GROUPS, HPG)
                           + dt_bias.astype(jnp.float32).reshape(2, N_GROUPS, HPG))


def ssm_decay(a_log):
    return -jnp.exp(a_log.astype(jnp.float32)).reshape(2, N_GROUPS, HPG)


def chunk_states(xs, dt, a, bm, h0):
    la = jnp.cumsum(dt * a, axis=2)
    w_end = jnp.exp(la[:, :, -1:] - la) * dt
    contrib = jnp.einsum('bcsgn,bcsgrp->bcgrpn', bm, xs * w_end[..., None])
    chunk_decay = jnp.exp(la[:, :, -1])

    def step(h, inp):
        s, d = inp
        return h * d[..., None, None] + s, h

    h_last, h_prev = lax.scan(step, h0, (jnp.moveaxis(contrib, 1, 0), jnp.moveaxis(chunk_decay, 1, 0)))
    return la, jnp.moveaxis(h_prev, 0, 1), h_last


def ssd_scan(xs, dt, a, bm, cm, h0):
    xs_c, dt_c, bm_c, cm_c = [to_chunks(t.astype(jnp.float32)) for t in (xs, dt, bm, cm)]
    la, h_prev, h_last = chunk_states(xs_c, dt_c, a, bm_c, h0)
    idx = jnp.arange(CHUNK)
    order = (idx[:, None] >= idx[None, :])[None, None, :, :, None, None]
    seg = la[:, :, :, None] - la[:, :, None, :]
    decay = jnp.exp(jnp.where(order, seg, -jnp.inf))
    scores = jnp.einsum('bclgn,bcsgn->bclsg', cm_c, bm_c)
    mix = scores[..., None] * decay * dt_c[:, :, None]
    y_diag = jnp.einsum('bclsgr,bcsgrp->bclgrp', mix, xs_c)
    y_off = jnp.einsum('bclgn,bcgrpn->bclgrp', cm_c, h_prev) * jnp.exp(la)[..., None]
    return (y_diag + y_off).reshape(xs.shape), h_last


def ssd_final_state(xs, dt, a, bm, h0):
    xs_c, dt_c, bm_c = [to_chunks(t.astype(jnp.float32)) for t in (xs, dt, bm)]
    _, _, h_last = chunk_states(xs_c, dt_c, a, bm_c, h0)
    return h_last


def context_states(h_ctx, w_in, ssm_conv_w, ssm_conv_b, dt_bias, a_log, h0):
    b, L, _ = h_ctx.shape
    xb = jax.nn.silu(depthwise_conv(h_ctx @ w_in[:, :B_END], ssm_conv_w[:, :B_END], ssm_conv_b[:B_END], SSM_PAD))
    xs = xb[..., :X_END].reshape(b, L, N_GROUPS, HPG, HEAD_DIM)
    bm = xb[..., X_END:B_END].reshape(b, L, N_GROUPS, D_STATE)
    dt = ssm_dt(h_ctx @ w_in[:, C_END:DT_END], dt_bias)
    a = ssm_decay(a_log)
    h_f = ssd_final_state(xs, dt[:, :, 0], a[0], bm, h0)
    h_b = ssd_final_state(rev(xs), rev(dt[:, :, 1]), a[1], rev(bm), h0)
    return h_f, h_b


def mixer(h, p, h0_f, h0_b, rows):
    b, L, _ = h.shape
    proj = h @ p['w_in']
    xbc = jax.nn.silu(depthwise_conv(proj[..., :C_END], p['ssm_conv_w'], p['ssm_conv_b'], SSM_PAD))
    xs = xbc[..., :X_END].reshape(b, L, N_GROUPS, HPG, HEAD_DIM)
    bm = xbc[..., X_END:B_END].reshape(b, L, N_GROUPS, D_STATE)
    cm = xbc[..., B_END:C_END].reshape(b, L, N_GROUPS, D_STATE)
    dt = ssm_dt(proj[..., C_END:DT_END], p['dt_bias'])
    a = ssm_decay(p['a_log'])
    y_f, h_f = ssd_scan(xs, dt[:, :, 0], a[0], bm, cm, h0_f)
    y_b, h_b = ssd_scan(rev(xs), rev(dt[:, :, 1]), a[1], rev(bm), rev(cm), h0_b)
    y = (y_f + rev(y_b)).astype(h.dtype) + p['d_skip'].reshape(N_GROUPS, HPG, 1) * xs
    y = y.reshape(b, L, D_INNER) * jax.nn.silu(proj[..., DT_END:Z_END])
    branch_ssm = group_rms_norm(y, p['ssm_norm_w']) @ p['w_out_ssm']
    glu = proj[..., Z_END:GLU_END]
    u = glu[..., :D_CONF] * jax.nn.sigmoid(glu[..., D_CONF:])
    if rows is not None:
        u = u.reshape(b * rows, GRID_W, D_CONF)
    u = depthwise_conv(u, p['conf_conv_w'], p['conf_conv_b'], CONF_PAD).reshape(b, L, D_CONF)
    u = jax.nn.silu(layer_norm(u, p['conf_ln_w'], p['conf_ln_b'])) * jax.nn.silu(proj[..., GLU_END:CG_END])
    branch_conf = u @ p['w_out_conf']
    g = jax.nn.sigmoid(proj[..., CG_END:])
    merged = g[..., :D_MODEL] * branch_ssm + g[..., D_MODEL:] * branch_conf
    return merged @ p['w_out'], h_f, h_b


def _fwd_setup_inputs(seed: int = 0) -> dict:
    key = jax.random.key(seed)
    ks = jax.random.split(key, 24)
    f32 = jnp.float32

    def nrm(k, shape, s):
        return jax.random.normal(k, shape, f32) * s

    dt0 = jnp.exp(jax.random.uniform(ks[10], (DEPTH, 2, N_HEADS), f32, math.log(1e-3), math.log(1e-1)))
    return {
        'x': nrm(ks[0], (BATCH, SEQ, D_MODEL), 1.0),
        'c': nrm(ks[1], (BATCH, D_MODEL), 1.0),
        'ctx': nrm(ks[2], (BATCH, CTX_LEN, D_MODEL), 1.0),
        'c_ctx': nrm(ks[3], (D_MODEL,), 1.0),
        'w_mod': nrm(ks[4], (DEPTH, D_MODEL, 3 * D_MODEL), D_MODEL ** -0.5),
        'b_mod': nrm(ks[5], (DEPTH, 3 * D_MODEL), 0.01),
        'norm_w': 1.0 + nrm(ks[6], (DEPTH, D_MODEL), 0.01),
        'w_in': nrm(ks[7], (DEPTH, D_MODEL, IN_COLS), D_MODEL ** -0.5),
        'ssm_conv_w': nrm(ks[8], (DEPTH, SSM_CONV, C_END), SSM_CONV ** -0.5),
        'ssm_conv_b': nrm(ks[9], (DEPTH, C_END), 0.01),
        'dt_bias': dt0 + jnp.log(-jnp.expm1(-dt0)),
        'a_log': jnp.log(jax.random.uniform(ks[11], (DEPTH, 2, N_HEADS), f32, 1.0, 16.0)),
        'd_skip': 1.0 + nrm(ks[12], (DEPTH, N_HEADS), 0.01),
        'ssm_norm_w': 1.0 + nrm(ks[13], (DEPTH, D_INNER), 0.01),
        'w_out_ssm': nrm(ks[14], (DEPTH, D_INNER, D_MODEL), D_INNER ** -0.5),
        'conf_conv_w': nrm(ks[15], (DEPTH, CONF_KERNEL, D_CONF), CONF_KERNEL ** -0.5),
        'conf_conv_b': nrm(ks[16], (DEPTH, D_CONF), 0.01),
        'conf_ln_w': 1.0 + nrm(ks[17], (DEPTH, D_CONF), 0.01),
        'conf_ln_b': nrm(ks[18], (DEPTH, D_CONF), 0.01),
        'w_out_conf': nrm(ks[19], (DEPTH, D_CONF, D_MODEL), D_CONF ** -0.5),
        'w_out': nrm(ks[20], (DEPTH, D_MODEL, D_MODEL), D_MODEL ** -0.5),
        'final_norm_w': 1.0 + nrm(ks[21], (D_MODEL,), 0.01),
    }


def _fwd_reference(x, c, ctx, c_ctx, w_mod, b_mod, norm_w, w_in, ssm_conv_w, ssm_conv_b, dt_bias, a_log,
              d_skip, ssm_norm_w, w_out_ssm, conf_conv_w, conf_conv_b, conf_ln_w, conf_ln_b,
              w_out_conf, w_out, final_norm_w):
    rows = x.shape[1] // GRID_W
    h0 = jnp.zeros((ctx.shape[0], N_GROUPS, HPG, HEAD_DIM, D_STATE), jnp.float32)
    for i in range(DEPTH):
        p = {'w_in': w_in[i], 'ssm_conv_w': ssm_conv_w[i], 'ssm_conv_b': ssm_conv_b[i],
             'dt_bias': dt_bias[i], 'a_log': a_log[i], 'd_skip': d_skip[i], 'ssm_norm_w': ssm_norm_w[i],
             'w_out_ssm': w_out_ssm[i], 'conf_conv_w': conf_conv_w[i], 'conf_conv_b': conf_conv_b[i],
             'conf_ln_w': conf_ln_w[i], 'conf_ln_b': conf_ln_b[i], 'w_out_conf': w_out_conf[i],
             'w_out': w_out[i]}
        mod_x = jax.nn.silu(c) @ w_mod[i] + b_mod[i]
        mod_c = jax.nn.silu(c_ctx) @ w_mod[i] + b_mod[i]
        shift_x, scale_x, gate_x = jnp.split(mod_x[:, None, :], 3, axis=-1)
        shift_c, scale_c, gate_c = jnp.split(mod_c, 3)
        h_ctx = modulate(rms_norm(ctx, norm_w[i]), shift_c, scale_c)
        if i < DEPTH - 1:
            ctx_out, h_f, h_b = mixer(h_ctx, p, h0, h0, None)
        else:
            h_f, h_b = context_states(h_ctx, p['w_in'], p['ssm_conv_w'], p['ssm_conv_b'],
                                      p['dt_bias'], p['a_log'], h0)
        h = modulate(rms_norm(x, norm_w[i]), shift_x, scale_x)
        out, _, _ = mixer(h, p, h_f, h_b, rows)
        x = x + gate_x * out
        if i < DEPTH - 1:
            ctx = ctx + gate_c * ctx_out
    return rms_norm(x, final_norm_w)


import jax as _jax
import jax.numpy as _jnp

TWIN_FORMAT = 'train_step'
FWD_PARAMS = ['x', 'c', 'ctx', 'c_ctx', 'w_mod', 'b_mod', 'norm_w', 'w_in', 'ssm_conv_w', 'ssm_conv_b', 'dt_bias', 'a_log', 'd_skip', 'ssm_norm_w', 'w_out_ssm', 'conf_conv_w', 'conf_conv_b', 'conf_ln_w', 'conf_ln_b', 'w_out_conf', 'w_out', 'final_norm_w']
TWIN_WEIGHTS = ['c_ctx', 'w_mod', 'b_mod', 'norm_w', 'w_in', 'ssm_conv_w', 'ssm_conv_b', 'dt_bias', 'a_log', 'd_skip', 'ssm_norm_w', 'w_out_ssm', 'conf_conv_w', 'conf_conv_b', 'conf_ln_w', 'conf_ln_b', 'w_out_conf', 'w_out', 'final_norm_w']
TWIN_DIFF_INPUT = 'x'
TWIN_INPUTS = ['x', 'c', 'ctx', 'c_ctx', 'w_mod', 'b_mod', 'norm_w', 'w_in', 'ssm_conv_w', 'ssm_conv_b', 'dt_bias', 'a_log', 'd_skip', 'ssm_norm_w', 'w_out_ssm', 'conf_conv_w', 'conf_conv_b', 'conf_ln_w', 'conf_ln_b', 'w_out_conf', 'w_out', 'final_norm_w', 'loss_target', 'm_c_ctx', 'm_w_mod', 'm_b_mod', 'm_norm_w', 'm_w_in', 'm_ssm_conv_w', 'm_ssm_conv_b', 'm_dt_bias', 'm_a_log', 'm_d_skip', 'm_ssm_norm_w', 'm_w_out_ssm', 'm_conf_conv_w', 'm_conf_conv_b', 'm_conf_ln_w', 'm_conf_ln_b', 'm_w_out_conf', 'm_w_out', 'm_final_norm_w', 'v_c_ctx', 'v_w_mod', 'v_b_mod', 'v_norm_w', 'v_w_in', 'v_ssm_conv_w', 'v_ssm_conv_b', 'v_dt_bias', 'v_a_log', 'v_d_skip', 'v_ssm_norm_w', 'v_w_out_ssm', 'v_conf_conv_w', 'v_conf_conv_b', 'v_conf_ln_w', 'v_conf_ln_b', 'v_w_out_conf', 'v_w_out', 'v_final_norm_w']
TWIN_OUTPUTS = ['loss', 'grad_x', 'grad_c_ctx', 'grad_w_mod', 'grad_b_mod', 'grad_norm_w', 'grad_w_in', 'grad_ssm_conv_w', 'grad_ssm_conv_b', 'grad_dt_bias', 'grad_a_log', 'grad_d_skip', 'grad_ssm_norm_w', 'grad_w_out_ssm', 'grad_conf_conv_w', 'grad_conf_conv_b', 'grad_conf_ln_w', 'grad_conf_ln_b', 'grad_w_out_conf', 'grad_w_out', 'grad_final_norm_w', 'delta_c_ctx', 'delta_w_mod', 'delta_b_mod', 'delta_norm_w', 'delta_w_in', 'delta_ssm_conv_w', 'delta_ssm_conv_b', 'delta_dt_bias', 'delta_a_log', 'delta_d_skip', 'delta_ssm_norm_w', 'delta_w_out_ssm', 'delta_conf_conv_w', 'delta_conf_conv_b', 'delta_conf_ln_w', 'delta_conf_ln_b', 'delta_w_out_conf', 'delta_w_out', 'delta_final_norm_w', 'new_m_c_ctx', 'new_m_w_mod', 'new_m_b_mod', 'new_m_norm_w', 'new_m_w_in', 'new_m_ssm_conv_w', 'new_m_ssm_conv_b', 'new_m_dt_bias', 'new_m_a_log', 'new_m_d_skip', 'new_m_ssm_norm_w', 'new_m_w_out_ssm', 'new_m_conf_conv_w', 'new_m_conf_conv_b', 'new_m_conf_ln_w', 'new_m_conf_ln_b', 'new_m_w_out_conf', 'new_m_w_out', 'new_m_final_norm_w', 'new_v_c_ctx', 'new_v_w_mod', 'new_v_b_mod', 'new_v_norm_w', 'new_v_w_in', 'new_v_ssm_conv_w', 'new_v_ssm_conv_b', 'new_v_dt_bias', 'new_v_a_log', 'new_v_d_skip', 'new_v_ssm_norm_w', 'new_v_w_out_ssm', 'new_v_conf_conv_w', 'new_v_conf_conv_b', 'new_v_conf_ln_w', 'new_v_conf_ln_b', 'new_v_w_out_conf', 'new_v_w_out', 'new_v_final_norm_w']
TWIN_LEAF_KINDS = {'loss': 'loss', 'grad_x': 'grad_x', 'grad_c_ctx': 'grad_w', 'grad_w_mod': 'grad_w', 'grad_b_mod': 'grad_w', 'grad_norm_w': 'grad_w', 'grad_w_in': 'grad_w', 'grad_ssm_conv_w': 'grad_w', 'grad_ssm_conv_b': 'grad_w', 'grad_dt_bias': 'grad_w', 'grad_a_log': 'grad_w', 'grad_d_skip': 'grad_w', 'grad_ssm_norm_w': 'grad_w', 'grad_w_out_ssm': 'grad_w', 'grad_conf_conv_w': 'grad_w', 'grad_conf_conv_b': 'grad_w', 'grad_conf_ln_w': 'grad_w', 'grad_conf_ln_b': 'grad_w', 'grad_w_out_conf': 'grad_w', 'grad_w_out': 'grad_w', 'grad_final_norm_w': 'grad_w', 'delta_c_ctx': 'delta_w', 'delta_w_mod': 'delta_w', 'delta_b_mod': 'delta_w', 'delta_norm_w': 'delta_w', 'delta_w_in': 'delta_w', 'delta_ssm_conv_w': 'delta_w', 'delta_ssm_conv_b': 'delta_w', 'delta_dt_bias': 'delta_w', 'delta_a_log': 'delta_w', 'delta_d_skip': 'delta_w', 'delta_ssm_norm_w': 'delta_w', 'delta_w_out_ssm': 'delta_w', 'delta_conf_conv_w': 'delta_w', 'delta_conf_conv_b': 'delta_w', 'delta_conf_ln_w': 'delta_w', 'delta_conf_ln_b': 'delta_w', 'delta_w_out_conf': 'delta_w', 'delta_w_out': 'delta_w', 'delta_final_norm_w': 'delta_w', 'new_m_c_ctx': 'new_m', 'new_m_w_mod': 'new_m', 'new_m_b_mod': 'new_m', 'new_m_norm_w': 'new_m', 'new_m_w_in': 'new_m', 'new_m_ssm_conv_w': 'new_m', 'new_m_ssm_conv_b': 'new_m', 'new_m_dt_bias': 'new_m', 'new_m_a_log': 'new_m', 'new_m_d_skip': 'new_m', 'new_m_ssm_norm_w': 'new_m', 'new_m_w_out_ssm': 'new_m', 'new_m_conf_conv_w': 'new_m', 'new_m_conf_conv_b': 'new_m', 'new_m_conf_ln_w': 'new_m', 'new_m_conf_ln_b': 'new_m', 'new_m_w_out_conf': 'new_m', 'new_m_w_out': 'new_m', 'new_m_final_norm_w': 'new_m', 'new_v_c_ctx': 'new_v', 'new_v_w_mod': 'new_v', 'new_v_b_mod': 'new_v', 'new_v_norm_w': 'new_v', 'new_v_w_in': 'new_v', 'new_v_ssm_conv_w': 'new_v', 'new_v_ssm_conv_b': 'new_v', 'new_v_dt_bias': 'new_v', 'new_v_a_log': 'new_v', 'new_v_d_skip': 'new_v', 'new_v_ssm_norm_w': 'new_v', 'new_v_w_out_ssm': 'new_v', 'new_v_conf_conv_w': 'new_v', 'new_v_conf_conv_b': 'new_v', 'new_v_conf_ln_w': 'new_v', 'new_v_conf_ln_b': 'new_v', 'new_v_w_out_conf': 'new_v', 'new_v_w_out': 'new_v', 'new_v_final_norm_w': 'new_v'}


def _forward(args):
    return _fwd_reference(*[args[k] for k in FWD_PARAMS])


def _output_shape():
    def fwd():
        inp = _fwd_setup_inputs(0)
        return _fwd_reference(*[inp[k] for k in FWD_PARAMS])
    out = _jax.eval_shape(fwd)
    return out.shape, out.dtype

N_MICROBATCH = 1
ADAM_LR = 0.001
ADAM_B1 = 0.9
ADAM_B2 = 0.999
ADAM_EPS = 1e-08
ADAM_WD = 0.01
ADAM_STEP = 10
PER_EXAMPLE_BATCH_AXIS = {'x': 0, 'c': 0, 'ctx': 0, 'loss_target': 0}
SHARED_INPUTS = []
_WEIGHT_DTYPES = {'c_ctx': _jnp.float32, 'w_mod': _jnp.float32, 'b_mod': _jnp.float32, 'norm_w': _jnp.float32, 'w_in': _jnp.float32, 'ssm_conv_w': _jnp.float32, 'ssm_conv_b': _jnp.float32, 'dt_bias': _jnp.float32, 'a_log': _jnp.float32, 'd_skip': _jnp.float32, 'ssm_norm_w': _jnp.float32, 'w_out_ssm': _jnp.float32, 'conf_conv_w': _jnp.float32, 'conf_conv_b': _jnp.float32, 'conf_ln_w': _jnp.float32, 'conf_ln_b': _jnp.float32, 'w_out_conf': _jnp.float32, 'w_out': _jnp.float32, 'final_norm_w': _jnp.float32}
MOMENT_SCALE = {'c_ctx': 1.276687e-02, 'w_mod': 8.056746e-02, 'b_mod': 1.350982e-01, 'norm_w': 1.232741e-01, 'w_in': 4.290441e-02, 'ssm_conv_w': 4.721872e-02, 'ssm_conv_b': 4.647768e-02, 'dt_bias': 1.041627e-01, 'a_log': 2.480180e-01, 'd_skip': 2.194659e-01, 'ssm_norm_w': 6.509175e-02, 'w_out_ssm': 7.984895e-02, 'conf_conv_w': 4.099794e-02, 'conf_conv_b': 6.086420e-02, 'conf_ln_w': 4.468915e-02, 'conf_ln_b': 4.080860e-02, 'w_out_conf': 3.979671e-02, 'w_out': 8.963504e-02, 'final_norm_w': 6.410683e+01}


def _to_microbatches(a, axis):
    t = _jnp.moveaxis(a, axis, 0)
    t = t.reshape((N_MICROBATCH, t.shape[0] // N_MICROBATCH) + t.shape[1:])
    return _jnp.moveaxis(t, 1, axis + 1)


def setup_inputs(seed: int = 0) -> dict:
    inp = _fwd_setup_inputs(seed)
    key = _jax.random.fold_in(_jax.random.key(seed), 7919)
    shape, _ = _output_shape()
    out = dict(inp)
    out["loss_target"] = _jax.random.normal(_jax.random.fold_in(key, 0), shape, _jnp.float32)
    for i, name in enumerate(TWIN_WEIGHTS):
        w = inp[name].astype(_jnp.float32)
        if MOMENT_SCALE is None:
            s = _jnp.sqrt(_jnp.mean(_jnp.square(w)) + 1e-30)
        else:
            s = MOMENT_SCALE[name]
        km, kv = _jax.random.split(_jax.random.fold_in(key, i + 1))
        out[name] = w
        out["m_" + name] = s * _jax.random.normal(km, w.shape, _jnp.float32)
        out["v_" + name] = (s * s) * _jax.random.uniform(kv, w.shape, _jnp.float32, 0.5, 1.5)
    if N_MICROBATCH > 1:
        for name, axis in PER_EXAMPLE_BATCH_AXIS.items():
            out[name] = _to_microbatches(out[name], axis)
    return {'x': out['x'], 'c': out['c'], 'ctx': out['ctx'], 'c_ctx': out['c_ctx'], 'w_mod': out['w_mod'], 'b_mod': out['b_mod'], 'norm_w': out['norm_w'], 'w_in': out['w_in'], 'ssm_conv_w': out['ssm_conv_w'], 'ssm_conv_b': out['ssm_conv_b'], 'dt_bias': out['dt_bias'], 'a_log': out['a_log'], 'd_skip': out['d_skip'], 'ssm_norm_w': out['ssm_norm_w'], 'w_out_ssm': out['w_out_ssm'], 'conf_conv_w': out['conf_conv_w'], 'conf_conv_b': out['conf_conv_b'], 'conf_ln_w': out['conf_ln_w'], 'conf_ln_b': out['conf_ln_b'], 'w_out_conf': out['w_out_conf'], 'w_out': out['w_out'], 'final_norm_w': out['final_norm_w'], 'loss_target': out['loss_target'], 'm_c_ctx': out['m_c_ctx'], 'm_w_mod': out['m_w_mod'], 'm_b_mod': out['m_b_mod'], 'm_norm_w': out['m_norm_w'], 'm_w_in': out['m_w_in'], 'm_ssm_conv_w': out['m_ssm_conv_w'], 'm_ssm_conv_b': out['m_ssm_conv_b'], 'm_dt_bias': out['m_dt_bias'], 'm_a_log': out['m_a_log'], 'm_d_skip': out['m_d_skip'], 'm_ssm_norm_w': out['m_ssm_norm_w'], 'm_w_out_ssm': out['m_w_out_ssm'], 'm_conf_conv_w': out['m_conf_conv_w'], 'm_conf_conv_b': out['m_conf_conv_b'], 'm_conf_ln_w': out['m_conf_ln_w'], 'm_conf_ln_b': out['m_conf_ln_b'], 'm_w_out_conf': out['m_w_out_conf'], 'm_w_out': out['m_w_out'], 'm_final_norm_w': out['m_final_norm_w'], 'v_c_ctx': out['v_c_ctx'], 'v_w_mod': out['v_w_mod'], 'v_b_mod': out['v_b_mod'], 'v_norm_w': out['v_norm_w'], 'v_w_in': out['v_w_in'], 'v_ssm_conv_w': out['v_ssm_conv_w'], 'v_ssm_conv_b': out['v_ssm_conv_b'], 'v_dt_bias': out['v_dt_bias'], 'v_a_log': out['v_a_log'], 'v_d_skip': out['v_d_skip'], 'v_ssm_norm_w': out['v_ssm_norm_w'], 'v_w_out_ssm': out['v_w_out_ssm'], 'v_conf_conv_w': out['v_conf_conv_w'], 'v_conf_conv_b': out['v_conf_conv_b'], 'v_conf_ln_w': out['v_conf_ln_w'], 'v_conf_ln_b': out['v_conf_ln_b'], 'v_w_out_conf': out['v_w_out_conf'], 'v_w_out': out['v_w_out'], 'v_final_norm_w': out['v_final_norm_w']}


def _loss(weights, diff, rest, loss_target):
    with _jax.named_scope("forward"):
        args = {**rest, TWIN_DIFF_INPUT: diff, **{k: w.astype(_WEIGHT_DTYPES[k]) for k, w in weights.items()}}
        y = _forward(args)
    with _jax.named_scope("loss_head"):
        err = _jnp.square(y.astype(_jnp.float32) - loss_target)
        return 0.5 * _jnp.sum(_jnp.mean(err, axis=-1)) if err.ndim else 0.5 * err


def _adamw(w, g, m, v):
    m = ADAM_B1 * m + (1.0 - ADAM_B1) * g
    v = ADAM_B2 * v + (1.0 - ADAM_B2) * _jnp.square(g)
    m_hat = m / (1.0 - ADAM_B1 ** ADAM_STEP)
    v_hat = v / (1.0 - ADAM_B2 ** ADAM_STEP)
    delta = -ADAM_LR * (m_hat / (_jnp.sqrt(v_hat) + ADAM_EPS) + ADAM_WD * w)
    return delta, m, v


def reference(x, c, ctx, c_ctx, w_mod, b_mod, norm_w, w_in, ssm_conv_w, ssm_conv_b, dt_bias, a_log, d_skip, ssm_norm_w, w_out_ssm, conf_conv_w, conf_conv_b, conf_ln_w, conf_ln_b, w_out_conf, w_out, final_norm_w, loss_target, m_c_ctx, m_w_mod, m_b_mod, m_norm_w, m_w_in, m_ssm_conv_w, m_ssm_conv_b, m_dt_bias, m_a_log, m_d_skip, m_ssm_norm_w, m_w_out_ssm, m_conf_conv_w, m_conf_conv_b, m_conf_ln_w, m_conf_ln_b, m_w_out_conf, m_w_out, m_final_norm_w, v_c_ctx, v_w_mod, v_b_mod, v_norm_w, v_w_in, v_ssm_conv_w, v_ssm_conv_b, v_dt_bias, v_a_log, v_d_skip, v_ssm_norm_w, v_w_out_ssm, v_conf_conv_w, v_conf_conv_b, v_conf_ln_w, v_conf_ln_b, v_w_out_conf, v_w_out, v_final_norm_w):
    given = dict(x=x, c=c, ctx=ctx, c_ctx=c_ctx, w_mod=w_mod, b_mod=b_mod, norm_w=norm_w, w_in=w_in, ssm_conv_w=ssm_conv_w, ssm_conv_b=ssm_conv_b, dt_bias=dt_bias, a_log=a_log, d_skip=d_skip, ssm_norm_w=ssm_norm_w, w_out_ssm=w_out_ssm, conf_conv_w=conf_conv_w, conf_conv_b=conf_conv_b, conf_ln_w=conf_ln_w, conf_ln_b=conf_ln_b, w_out_conf=w_out_conf, w_out=w_out, final_norm_w=final_norm_w, loss_target=loss_target, m_c_ctx=m_c_ctx, m_w_mod=m_w_mod, m_b_mod=m_b_mod, m_norm_w=m_norm_w, m_w_in=m_w_in, m_ssm_conv_w=m_ssm_conv_w, m_ssm_conv_b=m_ssm_conv_b, m_dt_bias=m_dt_bias, m_a_log=m_a_log, m_d_skip=m_d_skip, m_ssm_norm_w=m_ssm_norm_w, m_w_out_ssm=m_w_out_ssm, m_conf_conv_w=m_conf_conv_w, m_conf_conv_b=m_conf_conv_b, m_conf_ln_w=m_conf_ln_w, m_conf_ln_b=m_conf_ln_b, m_w_out_conf=m_w_out_conf, m_w_out=m_w_out, m_final_norm_w=m_final_norm_w, v_c_ctx=v_c_ctx, v_w_mod=v_w_mod, v_b_mod=v_b_mod, v_norm_w=v_norm_w, v_w_in=v_w_in, v_ssm_conv_w=v_ssm_conv_w, v_ssm_conv_b=v_ssm_conv_b, v_dt_bias=v_dt_bias, v_a_log=v_a_log, v_d_skip=v_d_skip, v_ssm_norm_w=v_ssm_norm_w, v_w_out_ssm=v_w_out_ssm, v_conf_conv_w=v_conf_conv_w, v_conf_conv_b=v_conf_conv_b, v_conf_ln_w=v_conf_ln_w, v_conf_ln_b=v_conf_ln_b, v_w_out_conf=v_w_out_conf, v_w_out=v_w_out, v_final_norm_w=v_final_norm_w)
    weights = {n: given[n] for n in TWIN_WEIGHTS}
    shared = {n: given[n] for n in SHARED_INPUTS}
    per_example = {n: given[n] for n in ['x', 'c', 'ctx']}
    grad_fn = _jax.value_and_grad(_loss, argnums=(0, 1))

    def one_microbatch(ex, loss_target):
        ex = dict(ex)
        diff = ex.pop(TWIN_DIFF_INPUT)
        return grad_fn(weights, diff, {**shared, **ex}, loss_target)

    if N_MICROBATCH == 1:
        loss, (grad_w, grad_x) = one_microbatch(per_example, given["loss_target"])
    else:
        def body(carry, xs):
            loss_sum, grad_sum = carry
            l_k, (gw_k, gx_k) = one_microbatch(xs[0], xs[1])
            with _jax.named_scope("update"):
                return (loss_sum + l_k, _jax.tree.map(_jnp.add, grad_sum, gw_k)), gx_k

        init = (_jnp.zeros((), _jnp.float32), _jax.tree.map(_jnp.zeros_like, weights))
        (loss, grad_w), grad_x = _jax.lax.scan(body, init, (per_example, given["loss_target"]))
    with _jax.named_scope("update"):
        delta_w, new_m, new_v = {}, {}, {}
        for n in TWIN_WEIGHTS:
            delta_w[n], new_m[n], new_v[n] = _adamw(weights[n], grad_w[n], given["m_" + n], given["v_" + n])
    return (loss, grad_x, *[grad_w[n] for n in TWIN_WEIGHTS], *[delta_w[n] for n in TWIN_WEIGHTS],
            *[new_m[n] for n in TWIN_WEIGHTS], *[new_v[n] for n in TWIN_WEIGHTS])
```

```python
import functools

import jax
import jax.numpy as jnp
from jax import lax
from jax.experimental import pallas as pl
from jax.experimental.pallas import tpu as pltpu

F32, BF16 = jnp.float32, jnp.bfloat16

D = 1024
DI = 2048
NH = 32
HP = 64
NG = 8
HPG = 4
NS = 128
Q = 128
GW = 64
CK = 31
SK = 4
CTX = 256
EPS = 1e-6
XBC = DI + 2 * NG * NS
SSDW = XBC + 128
RESTW = 7168
T = 256
VMEM_LIMIT = 56 * 1024 * 1024

ADAM_LR, ADAM_B1, ADAM_B2, ADAM_EPS, ADAM_WD, ADAM_STEP = 0.001, 0.9, 0.999, 1e-08, 0.01, 10


def _cp(sem):
    return pltpu.CompilerParams(dimension_semantics=sem, vmem_limit_bytes=VMEM_LIMIT)


def _sig(x):
    return jax.nn.sigmoid(x)


def _silu(x):
    return x * _sig(x)


def _dsilu(x):
    s = _sig(x)
    return s * (1.0 + x * (1.0 - s))


def _dot(a, b):
    return jnp.dot(a, b, preferred_element_type=F32)


def _dot_nt(a, b):
    return lax.dot_general(a, b, (((1,), (1,)), ((), ())), preferred_element_type=F32)


def _split3(x):
    h = x.astype(BF16)
    r = x - h.astype(F32)
    m = r.astype(BF16)
    l = (r - m.astype(F32)).astype(BF16)
    return h, m, l


def _dot3_l(sel, x):
    h, m, l = _split3(x)
    return _dot(sel, h) + _dot(sel, m) + _dot(sel, l)


def _dot3_r(x, sel):
    h, m, l = _split3(x)
    return _dot(h, sel) + _dot(m, sel) + _dot(l, sel)


def _iota(shape, dim):
    return lax.broadcasted_iota(jnp.int32, shape, dim)


def _mm(a, b, dims, m, n, k, bm, bn, bk, out_dtype, name, a_off=0, b_off=0):
    nk = k // bk
    assert m % bm == 0 and n % bn == 0 and k % bk == 0

    def kern(a_ref, b_ref, o_ref, acc):
        kk = pl.program_id(2)

        @pl.when(kk == 0)
        def _():
            acc[...] = jnp.zeros_like(acc)

        av = a_ref[...].astype(BF16)
        bv = b_ref[...].astype(BF16)
        if dims == "nn":
            acc[...] += _dot(av, bv)
        elif dims == "nt":
            acc[...] += _dot_nt(av, bv)
        else:
            acc[...] += lax.dot_general(av, bv, (((0,), (0,)), ((), ())), preferred_element_type=F32)

        @pl.when(kk == nk - 1)
        def _():
            o_ref[...] = acc[...].astype(out_dtype)

    if dims == "nn":
        a_spec = pl.BlockSpec((bm, bk), lambda i, j, kk: (i + a_off, kk))
        b_spec = pl.BlockSpec((bk, bn), lambda i, j, kk: (kk, j))
    elif dims == "nt":
        a_spec = pl.BlockSpec((bm, bk), lambda i, j, kk: (i + a_off, kk))
        b_spec = pl.BlockSpec((bn, bk), lambda i, j, kk: (j, kk))
    else:
        a_spec = pl.BlockSpec((bk, bm), lambda i, j, kk: (kk + a_off, i))
        b_spec = pl.BlockSpec((bk, bn), lambda i, j, kk: (kk + b_off, j))
    return pl.pallas_call(
        kern, name=name,
        grid=(m // bm, n // bn, nk),
        in_specs=[a_spec, b_spec],
        out_specs=pl.BlockSpec((bm, bn), lambda i, j, kk: (i, j)),
        out_shape=jax.ShapeDtypeStruct((m, n), out_dtype),
        scratch_shapes=[pltpu.VMEM((bm, bn), F32)],
        compiler_params=_cp(("parallel", "parallel", "arbitrary")),
    )(a, b)


def _mod_fwd(cc, w_mod, b_mod):
    def kern(cc_ref, w_ref, b_ref, o_ref):
        s = _silu(cc_ref[...]).astype(BF16)
        o_ref[...] = _dot(s, w_ref[...]) + b_ref[...]

    return pl.pallas_call(
        kern, name="mod_fwd", grid=(3,),
        in_specs=[pl.BlockSpec((8, D), lambda j: (0, 0)), pl.BlockSpec((D, D), lambda j: (0, j)),
                  pl.BlockSpec((1, D), lambda j: (0, j))],
        out_specs=pl.BlockSpec((8, D), lambda j: (0, j)),
        out_shape=jax.ShapeDtypeStruct((8, 3 * D), F32),
        compiler_params=_cp(("parallel",)),
    )(cc, w_mod, b_mod)


def _mod_bwd(dmod, cc, cct, w_mod):
    def kern(dm_ref, cc_ref, cct_ref, w_ref, gw_ref, gb_ref, gc_ref):
        kk = pl.program_id(0)
        dm = dm_ref[...]
        sct = _silu(cct_ref[...])
        gw_ref[...] = sct[:, 0:1] * dm[0:1, :] + sct[:, 1:2] * dm[1:2, :]
        gb_ref[...] = jnp.broadcast_to(dm[0:1, :] + dm[1:2, :], dm.shape)

        @pl.when(kk == 0)
        def _():
            gc_ref[...] = jnp.zeros_like(gc_ref)

        gc_ref[...] += _dot_nt(dm.astype(BF16), w_ref[...])

        @pl.when(kk == 2)
        def _():
            gc_ref[...] = gc_ref[...] * _dsilu(cc_ref[...])

    return pl.pallas_call(
        kern, name="mod_bwd", grid=(3,),
        in_specs=[pl.BlockSpec((8, D), lambda j: (0, j)), pl.BlockSpec((8, D), lambda j: (0, 0)),
                  pl.BlockSpec((D, 8), lambda j: (0, 0)), pl.BlockSpec((D, D), lambda j: (0, j))],
        out_specs=[pl.BlockSpec((D, D), lambda j: (0, j)), pl.BlockSpec((8, D), lambda j: (0, j)),
                   pl.BlockSpec((8, D), lambda j: (0, 0))],
        out_shape=[jax.ShapeDtypeStruct((D, 3 * D), F32), jax.ShapeDtypeStruct((8, 3 * D), F32),
                   jax.ShapeDtypeStruct((8, D), F32)],
        compiler_params=_cp(("arbitrary",)),
    )(dmod, cc, cct, w_mod)


def _norm_fwd(xe, mod, nw, nct):
    lext = xe.shape[0]

    def kern(x_ref, mod_ref, nw_ref, h_ref):
        is_ctx = pl.program_id(0) < nct
        x = x_ref[...]
        r = lax.rsqrt(jnp.mean(x * x, axis=-1, keepdims=True) + EPS)
        xn = x * r * nw_ref[...]
        shift = jnp.where(is_ctx, mod_ref[1:2, 0:D], mod_ref[0:1, 0:D])
        scale = jnp.where(is_ctx, mod_ref[1:2, D:2 * D], mod_ref[0:1, D:2 * D])
        h_ref[...] = (xn * (1.0 + scale) + shift).astype(BF16)

    return pl.pallas_call(
        kern, name="norm_fwd", grid=(lext // T,),
        in_specs=[pl.BlockSpec((T, D), lambda i: (i, 0)), pl.BlockSpec((8, 3 * D), lambda i: (0, 0)),
                  pl.BlockSpec((1, D), lambda i: (0, 0))],
        out_specs=pl.BlockSpec((T, D), lambda i: (i, 0)),
        out_shape=jax.ShapeDtypeStruct((lext, D), BF16),
        compiler_params=_cp(("parallel",)),
    )(xe, mod, nw)


def _norm_bwd(dha, dhb, xe, dx2, mod, nw, nct):
    lext = xe.shape[0]
    ntl = lext // T

    def kern(dha_ref, dhb_ref, x_ref, dx2_ref, mod_ref, nw_ref, gx_ref, gnw_ref, dss_ref):
        i = pl.program_id(0)
        is_ctx = i < nct

        @pl.when(i == 0)
        def _():
            gnw_ref[...] = jnp.zeros_like(gnw_ref)
            dss_ref[...] = jnp.zeros_like(dss_ref)

        x = x_ref[...]
        dh_ = dha_ref[...] + jnp.where(is_ctx, 0.0, dhb_ref[...])
        nw_ = nw_ref[...]
        r = lax.rsqrt(jnp.mean(x * x, axis=-1, keepdims=True) + EPS)
        xn = x * r
        scale = jnp.where(is_ctx, mod_ref[1:2, D:2 * D], mod_ref[0:1, D:2 * D])
        dsh = jnp.sum(dh_, axis=0, keepdims=True)
        dsc = jnp.sum(dh_ * (xn * nw_), axis=0, keepdims=True)
        row = jnp.concatenate([dsh, dsc], axis=1)
        rid = _iota((8, 2 * D), 0)
        dss_ref[...] += jnp.where(rid == jnp.where(is_ctx, 1, 0), row, 0.0)
        dxnw = dh_ * (1.0 + scale)
        gnw_ref[...] += jnp.broadcast_to(jnp.sum(dxnw * xn, axis=0, keepdims=True), (8, D))
        dxn = dxnw * nw_
        dx = r * (dxn - xn * jnp.mean(dxn * xn, axis=-1, keepdims=True))
        gx_ref[...] = dx2_ref[...] + dx

    return pl.pallas_call(
        kern, name="norm_bwd", grid=(ntl,),
        in_specs=[pl.BlockSpec((T, D), lambda i: (i, 0)), pl.BlockSpec((T, D), lambda i: (jnp.maximum(i - nct, 0), 0)),
                  pl.BlockSpec((T, D), lambda i: (i, 0)),
                  pl.BlockSpec((T, D), lambda i: (jnp.maximum(i - nct, 0), 0)),
                  pl.BlockSpec((8, 3 * D), lambda i: (0, 0)), pl.BlockSpec((1, D), lambda i: (0, 0))],
        out_specs=[pl.BlockSpec((T, D), lambda i: (jnp.maximum(i - nct, 0), 0)),
                   pl.BlockSpec((8, D), lambda i: (0, 0)), pl.BlockSpec((8, 2 * D), lambda i: (0, 0))],
        out_shape=[jax.ShapeDtypeStruct((lext - nct * T, D), F32), jax.ShapeDtypeStruct((8, D), F32),
                   jax.ShapeDtypeStruct((8, 2 * D), F32)],
        compiler_params=_cp(("arbitrary",)),
    )(dha, dhb, xe, dx2, mod, nw)


CB = 512


def _halo_specs(width_blk, col_off_blocks, ntl):
    t8 = T // 8
    main = pl.BlockSpec((T, width_blk), lambda j, i: (i, j + col_off_blocks))
    prev = pl.BlockSpec((8, width_blk), lambda j, i: (jnp.maximum(i * t8 - 1, 0), j + col_off_blocks))
    nxt = pl.BlockSpec((8, width_blk), lambda j, i: (jnp.minimum((i + 1) * t8, ntl * t8 - 1), j + col_off_blocks))
    return main, prev, nxt


def _seq_edges(i, nct, ntl):
    starts = jnp.logical_or(i == 0, i == nct)
    ends = jnp.logical_or(i == nct - 1, i == ntl - 1)
    return starts, ends


def _shifted(ext, off):
    n = ext.shape[0]
    return pltpu.roll(ext, (-off) % n, axis=0)[8:8 + T]


def _conv_fwd(proj_ssd, cw, cb, nct):
    lext = proj_ssd.shape[0]
    ntl = lext // T

    def kern(u_ref, up_ref, un_ref, w_ref, b_ref, o_ref):
        i = pl.program_id(1)
        starts, ends = _seq_edges(i, nct, ntl)
        up = jnp.where(starts, 0.0, up_ref[...])
        un = jnp.where(ends, 0.0, un_ref[...])
        ext = jnp.concatenate([up, u_ref[...], un], axis=0)
        w = w_ref[...]
        pre = b_ref[...] + w[0:1] * _shifted(ext, -2) + w[1:2] * _shifted(ext, -1) \
            + w[2:3] * u_ref[...] + w[3:4] * _shifted(ext, 1)
        o_ref[...] = _silu(pre)

    main, prev, nxt = _halo_specs(CB, 0, ntl)
    return pl.pallas_call(
        kern, name="conv_fwd", grid=(XBC // CB, ntl),
        in_specs=[main, prev, nxt, pl.BlockSpec((8, CB), lambda j, i: (0, j)), pl.BlockSpec((1, CB), lambda j, i: (0, j))],
        out_specs=pl.BlockSpec((T, CB), lambda j, i: (i, j)),
        out_shape=jax.ShapeDtypeStruct((lext, XBC), F32),
        compiler_params=_cp(("parallel", "parallel")),
    )(proj_ssd, proj_ssd, proj_ssd, cw, cb)


def _conv_bwd(dpost, proj_ssd, cw, cb, col_off, width, nct, name):
    lext = proj_ssd.shape[0]
    ntl = lext // T
    cob = col_off // CB

    def kern(u_ref, up_ref, un_ref, d_ref, dp_ref, dn_ref, w_ref, b_ref, du_ref, gw_ref, gb_ref):
        i = pl.program_id(1)

        @pl.when(i == 0)
        def _():
            gw_ref[...] = jnp.zeros_like(gw_ref)
            gb_ref[...] = jnp.zeros_like(gb_ref)

        starts, ends = _seq_edges(i, nct, ntl)
        ext = jnp.concatenate([jnp.where(starts, 0.0, up_ref[...]), u_ref[...], jnp.where(ends, 0.0, un_ref[...])], axis=0)
        dext = jnp.concatenate([jnp.where(starts, 0.0, dp_ref[...]), d_ref[...], jnp.where(ends, 0.0, dn_ref[...])], axis=0)
        w = w_ref[...]
        n = ext.shape[0]
        pre = b_ref[...] + w[0:1] * pltpu.roll(ext, 2, axis=0) + w[1:2] * pltpu.roll(ext, 1, axis=0) \
            + w[2:3] * ext + w[3:4] * pltpu.roll(ext, n - 1, axis=0)
        dpre = dext * _dsilu(pre)
        dm = dpre[8:8 + T]
        du = w[0:1] * _shifted(dpre, 2) + w[1:2] * _shifted(dpre, 1) + w[2:3] * dm + w[3:4] * _shifted(dpre, -1)
        du_ref[...] = du.astype(BF16)
        g0 = jnp.sum(dm * _shifted(ext, -2), axis=0, keepdims=True)
        g1 = jnp.sum(dm * _shifted(ext, -1), axis=0, keepdims=True)
        g2 = jnp.sum(dm * u_ref[...], axis=0, keepdims=True)
        g3 = jnp.sum(dm * _shifted(ext, 1), axis=0, keepdims=True)
        rid = _iota((8, CB), 0)
        gw_ref[...] += jnp.where(rid == 0, g0, jnp.where(rid == 1, g1, jnp.where(rid == 2, g2, jnp.where(rid == 3, g3, 0.0))))
        gb_ref[...] += jnp.broadcast_to(jnp.sum(dm, axis=0, keepdims=True), (8, CB))

    main, prev, nxt = _halo_specs(CB, cob, ntl)
    dmain, dprev, dnxt = _halo_specs(CB, 0, ntl)
    return pl.pallas_call(
        kern, name=name, grid=(width // CB, ntl),
        in_specs=[main, prev, nxt, dmain, dprev, dnxt,
                  pl.BlockSpec((8, CB), lambda j, i: (0, j + cob)), pl.BlockSpec((1, CB), lambda j, i: (0, j + cob))],
        out_specs=[pl.BlockSpec((T, CB), lambda j, i: (i, j)), pl.BlockSpec((8, CB), lambda j, i: (0, j)),
                   pl.BlockSpec((8, CB), lambda j, i: (0, j))],
        out_shape=[jax.ShapeDtypeStruct((lext, width), BF16), jax.ShapeDtypeStruct((8, width), F32),
                   jax.ShapeDtypeStruct((8, width), F32)],
        compiler_params=_cp(("parallel", "arbitrary")),
    )(proj_ssd, proj_ssd, proj_ssd, dpost, dpost, dpost, cw, cb)


def _tri(lower):
    r, c = _iota((Q, Q), 0), _iota((Q, Q), 1)
    return jnp.where((c <= r) if lower else (c >= r), 1.0, 0.0).astype(BF16)


def _is_bdir_lane(shape):
    ln = _iota(shape, len(shape) - 1)
    return jnp.logical_and(((ln >> 2) & 1) == 1, ln < 64)


def _dt_fwd(proj_ssd, dtb, av):
    lext = proj_ssd.shape[0]

    def kern(p_ref, b_ref, a_ref, dtg_ref, lag_ref, dtt_ref, lat_ref):
        lane = _iota((T, 128), 1)
        raw = p_ref[...] + b_ref[...]
        dt = jnp.where(lane < 64, jnp.maximum(raw, 0.0) + jnp.log1p(jnp.exp(-jnp.abs(raw))), 0.0)
        dta = dt * a_ref[...]
        tl, tu = _tri(True), _tri(False)
        isb = _is_bdir_lane((Q, 128))
        las = []
        for qq in range(T // Q):
            blk = dta[qq * Q:(qq + 1) * Q]
            las.append(jnp.where(isb, _dot3_l(tu, blk), _dot3_l(tl, blk)))
        la = jnp.concatenate(las, axis=0)
        for g in range(NG):
            sh = (128 - 8 * g) % 128
            dtg_ref[g] = jnp.where(lane < 8, pltpu.roll(dt, sh, axis=1) if sh else dt, 0.0)
            lag_ref[g] = jnp.where(lane < 8, pltpu.roll(la, sh, axis=1) if sh else la, 0.0)
        dtt_ref[...] = dt.T[0:64]
        lat_ref[...] = la.T[0:64]

    return pl.pallas_call(
        kern, name="dt_fwd", grid=(lext // T,),
        in_specs=[pl.BlockSpec((T, 128), lambda i: (i, XBC // 128)), pl.BlockSpec((1, 128), lambda i: (0, 0)),
                  pl.BlockSpec((1, 128), lambda i: (0, 0))],
        out_specs=[pl.BlockSpec((NG, T, 128), lambda i: (0, i, 0)), pl.BlockSpec((NG, T, 128), lambda i: (0, i, 0)),
                   pl.BlockSpec((64, T), lambda i: (0, i)), pl.BlockSpec((64, T), lambda i: (0, i))],
        out_shape=[jax.ShapeDtypeStruct((NG, lext, 128), F32), jax.ShapeDtypeStruct((NG, lext, 128), F32),
                   jax.ShapeDtypeStruct((64, lext), F32), jax.ShapeDtypeStruct((64, lext), F32)],
        compiler_params=_cp(("parallel",)),
    )(proj_ssd, dtb, av)


def _dt_bwd(ddtg, proj_ssd, dtb):
    lext = proj_ssd.shape[0]

    def kern(d_ref, p_ref, b_ref, o_ref, gb_ref):
        @pl.when(pl.program_id(0) == 0)
        def _():
            gb_ref[...] = jnp.zeros_like(gb_ref)

        acc = d_ref[0]
        for g in range(1, NG):
            acc = acc + pltpu.roll(d_ref[g], 8 * g, axis=1)
        draw = acc * _sig(p_ref[...] + b_ref[...])
        o_ref[...] = draw.astype(BF16)
        gb_ref[...] += jnp.broadcast_to(jnp.sum(draw, axis=0, keepdims=True), (8, 128))

    return pl.pallas_call(
        kern, name="dt_bwd", grid=(lext // T,),
        in_specs=[pl.BlockSpec((NG, T, 128), lambda i: (0, i, 0)), pl.BlockSpec((T, 128), lambda i: (i, XBC // 128)),
                  pl.BlockSpec((1, 128), lambda i: (0, 0))],
        out_specs=[pl.BlockSpec((T, 128), lambda i: (i, 0)), pl.BlockSpec((8, 128), lambda i: (0, 0))],
        out_shape=[jax.ShapeDtypeStruct((lext, 128), BF16), jax.ShapeDtypeStruct((8, 128), F32)],
        compiler_params=_cp(("arbitrary",)),
    )(ddtg, proj_ssd, dtb)


def _expand_sel(d):
    r, c = _iota((128, 256), 0), _iota((128, 256), 1)
    return jnp.where(r == 4 * d + (c >> 6), 1.0, 0.0).astype(BF16)


def _reduce_sel(d):
    r, c = _iota((256, 128), 0), _iota((256, 128), 1)
    return jnp.where(c == 4 * d + (r >> 6), 1.0, 0.0).astype(BF16)


def _chunk_of_bwd_dir(j, ncc, nc):
    return jnp.where(j < ncc, ncc - 1 - j, nc + ncc - 1 - j)


def _dir_terms(la, dt, d):
    lane = _iota(la.shape, 1)
    mine = jnp.logical_and(lane >= 4 * d, lane < 4 * d + 4)
    la = jnp.where(mine, la, 0.0)
    tot = la[Q - 1:Q] if d == 0 else la[0:1]
    wnd = jnp.exp(tot - la)
    return tot, wnd * jnp.where(mine, dt, 0.0), wnd


def _ssd_state(xbc, dtg, lag, ncc):
    lext = xbc.shape[0]
    nc = lext // Q

    def kern(xf_ref, bf_ref, dtf_ref, laf_ref, xb_ref, bb_ref, dtb_ref, lab_ref, hf_ref, hb_ref, sf, sb):
        @pl.when(pl.program_id(1) == 0)
        def _():
            sf[...] = jnp.zeros_like(sf)
            sb[...] = jnp.zeros_like(sb)

        for d, (x_ref, b_ref, dt_ref, la_ref, h_ref, s) in enumerate(
                ((xf_ref, bf_ref, dtf_ref, laf_ref, hf_ref, sf), (xb_ref, bb_ref, dtb_ref, lab_ref, hb_ref, sb))):
            h_ref[...] = s[...]
            tot, w_end, _ = _dir_terms(la_ref[...], dt_ref[...], d)
            ex = _expand_sel(d)
            wexp = _dot3_r(w_end, ex)
            dexp = _dot3_r(jnp.broadcast_to(jnp.exp(tot), (8, 128)), ex)[0:1]
            xw = (x_ref[...] * wexp).astype(BF16)
            s[...] = s[...] * dexp + _dot(b_ref[...].T.astype(BF16), xw)

    cb = functools.partial(_chunk_of_bwd_dir, ncc=ncc, nc=nc)
    return pl.pallas_call(
        kern, name="ssd_state", grid=(NG, nc),
        in_specs=[pl.BlockSpec((Q, 256), lambda g, j: (j, g)), pl.BlockSpec((Q, 128), lambda g, j: (j, 16 + g)),
                  pl.BlockSpec((None, Q, 128), lambda g, j: (g, j, 0)), pl.BlockSpec((None, Q, 128), lambda g, j: (g, j, 0)),
                  pl.BlockSpec((Q, 256), lambda g, j: (cb(j), g)), pl.BlockSpec((Q, 128), lambda g, j: (cb(j), 16 + g)),
                  pl.BlockSpec((None, Q, 128), lambda g, j: (g, cb(j), 0)), pl.BlockSpec((None, Q, 128), lambda g, j: (g, cb(j), 0))],
        out_specs=[pl.BlockSpec((None, 128, 256), lambda g, j: (j, 0, g)),
                   pl.BlockSpec((None, 128, 256), lambda g, j: (cb(j), 0, g))],
        out_shape=[jax.ShapeDtypeStruct((nc, 128, DI), F32), jax.ShapeDtypeStruct((nc, 128, DI), F32)],
        scratch_shapes=[pltpu.VMEM((128, 256), F32), pltpu.VMEM((128, 256), F32)],
        compiler_params=_cp(("parallel", "arbitrary")),
    )(xbc, xbc, dtg, lag, xbc, xbc, dtg, lag)


def _ssd_out(xbc, dtg, lag, dtt, lat, htf, htb, ncc):
    lext = xbc.shape[0]
    nc = lext // Q
    ncx = nc - ncc

    def kern(x_ref, b_ref, c_ref, dtg_ref, lag_ref, dtt_ref, lat_ref, hf_ref, hb_ref, y_ref):
        x = x_ref[...]
        cm = c_ref[...]
        xb_ = x.astype(BF16)
        s_ = _dot_nt(cm.astype(BF16), b_ref[...].astype(BF16))
        li, si = _iota((Q, Q), 0), _iota((Q, Q), 1)
        lane = _iota((Q, 256), 1)
        la, dtt_, lat_ = lag_ref[...], dtt_ref[...], lat_ref[...]
        elam = jnp.exp(la)
        y = jnp.zeros((Q, 256), F32)
        for d, h_ref in enumerate((hf_ref, hb_ref)):
            rhs = jnp.concatenate([xb_, h_ref[...].astype(BF16)], axis=0)
            mask = (li >= si) if d == 0 else (li <= si)
            for r in range(HPG):
                j = 4 * d + r
                lm = jnp.where(mask, jnp.exp(la[:, j:j + 1] - lat_[j:j + 1, :]), 0.0)
                w = s_ * lm * dtt_[j:j + 1, :]
                lhs = jnp.concatenate([w, cm * elam[:, j:j + 1]], axis=1).astype(BF16)
                y = y + jnp.where((lane >> 6) == r, _dot(lhs, rhs), 0.0)
        y_ref[...] = y

    return pl.pallas_call(
        kern, name="ssd_out", grid=(ncx, NG),
        in_specs=[pl.BlockSpec((Q, 256), lambda c, g: (c + ncc, g)), pl.BlockSpec((Q, 128), lambda c, g: (c + ncc, 16 + g)),
                  pl.BlockSpec((Q, 128), lambda c, g: (c + ncc, 24 + g)),
                  pl.BlockSpec((None, Q, 128), lambda c, g: (g, c + ncc, 0)), pl.BlockSpec((None, Q, 128), lambda c, g: (g, c + ncc, 0)),
                  pl.BlockSpec((8, Q), lambda c, g: (g, c + ncc)), pl.BlockSpec((8, Q), lambda c, g: (g, c + ncc)),
                  pl.BlockSpec((None, 128, 256), lambda c, g: (c + ncc, 0, g)), pl.BlockSpec((None, 128, 256), lambda c, g: (c + ncc, 0, g))],
        out_specs=pl.BlockSpec((Q, 256), lambda c, g: (c, g)),
        out_shape=jax.ShapeDtypeStruct((ncx * Q, DI), F32),
        compiler_params=_cp(("parallel", "parallel")),
    )(xbc, xbc, xbc, dtg, lag, dtt, lat, htf, htb)


def _ssd_bwd_state(xbc, dy, lag, ncc):
    lext = xbc.shape[0]
    nc = lext // Q

    def kern(cf_ref, dyf_ref, laf_ref, cb_ref, dyb_ref, lab_ref, df_ref, db_ref, sf, sb):
        @pl.when(pl.program_id(1) == 0)
        def _():
            sf[...] = jnp.zeros_like(sf)
            sb[...] = jnp.zeros_like(sb)

        for d, (c_ref, dy_ref, la_ref, o_ref, s) in enumerate(
                ((cf_ref, dyf_ref, laf_ref, df_ref, sf), (cb_ref, dyb_ref, lab_ref, db_ref, sb))):
            o_ref[...] = s[...]
            la = la_ref[...]
            tot = la[Q - 1:Q] if d == 0 else la[0:1]
            ex = _expand_sel(d)
            eexp = _dot3_r(jnp.exp(la), ex)
            dexp = _dot3_r(jnp.broadcast_to(jnp.exp(tot), (8, 128)), ex)[0:1]
            dye = (dy_ref[...] * eexp).astype(BF16)
            s[...] = s[...] * dexp + _dot(c_ref[...].T.astype(BF16), dye)

    cf = lambda j: nc - 1 - j
    cb = lambda j: _chunk_of_bwd_dir(nc - 1 - j, ncc, nc)
    return pl.pallas_call(
        kern, name="ssd_bwd_state", grid=(NG, nc),
        in_specs=[pl.BlockSpec((Q, 128), lambda g, j: (cf(j), 24 + g)), pl.BlockSpec((Q, 256), lambda g, j: (cf(j), g)),
                  pl.BlockSpec((None, Q, 128), lambda g, j: (g, cf(j), 0)),
                  pl.BlockSpec((Q, 128), lambda g, j: (cb(j), 24 + g)), pl.BlockSpec((Q, 256), lambda g, j: (cb(j), g)),
                  pl.BlockSpec((None, Q, 128), lambda g, j: (g, cb(j), 0))],
        out_specs=[pl.BlockSpec((None, 128, 256), lambda g, j: (cf(j), 0, g)),
                   pl.BlockSpec((None, 128, 256), lambda g, j: (cb(j), 0, g))],
        out_shape=[jax.ShapeDtypeStruct((nc, 128, DI), F32), jax.ShapeDtypeStruct((nc, 128, DI), F32)],
        scratch_shapes=[pltpu.VMEM((128, 256), F32), pltpu.VMEM((128, 256), F32)],
        compiler_params=_cp(("parallel", "arbitrary")),
    )(xbc, dy, lag, xbc, dy, lag)


def _ssd_bwd_out(xbc, dy, dxskip, dtg, lag, dtt, lat, htf, htb, dhf, dhb, a_rows):
    lext = xbc.shape[0]
    nc = lext // Q

    def kern(x_ref, b_ref, c_ref, dy_ref, sk_ref, dtg_ref, lag_ref, dtt_ref, lat_ref, hf_ref, hb_ref, df_ref, db_ref,
             a_ref, dx_ref, dbo_ref, dco_ref, ddt_ref, ga_ref):
        g = pl.program_id(0)

        @pl.when(pl.program_id(1) == 0)
        def _():
            ga_ref[...] = jnp.zeros_like(ga_ref)

        x, bm, cm, dy_ = x_ref[...], b_ref[...], c_ref[...], dy_ref[...]
        xb_, bb_, cb_, dyb_ = x.astype(BF16), bm.astype(BF16), cm.astype(BF16), dy_.astype(BF16)
        st = _dot_nt(bb_, cb_)
        si, li = _iota((Q, Q), 0), _iota((Q, Q), 1)
        lane = _iota((Q, 256), 1)
        lane128 = _iota((Q, 128), 1)
        row128 = _iota((Q, 128), 0)
        sub = _iota((128, Q), 0)
        la, dt, dtt_, lat_ = lag_ref[...], dtg_ref[...], dtt_ref[...], lat_ref[...]
        elam = jnp.exp(la)
        dst = jnp.zeros((Q, Q), F32)
        dxa = jnp.zeros((Q, 256), F32)
        dba = jnp.zeros((Q, 128), F32)
        dca = jnp.zeros((Q, 128), F32)
        dlam = jnp.zeros((Q, 128), F32)
        ddir = jnp.zeros((Q, 128), F32)
        rows = jnp.zeros((128, Q), F32)
        for d, (h_ref, dh_ref) in enumerate(((hf_ref, df_ref), (hb_ref, db_ref))):
            ht, dht = h_ref[...], dh_ref[...]
            htb_, dhtb_ = ht.astype(BF16), dht.astype(BF16)
            tot, w_end, wnd = _dir_terms(la, dt, d)
            ex, rs = _expand_sel(d), _reduce_sel(d)
            elx = _dot3_r(elam, ex)
            wex = _dot3_r(w_end, ex)
            dye = dy_ * elx
            ch = _dot(cb_, htb_)
            bd = _dot(bb_, dhtb_)
            dca = dca + _dot_nt(dye.astype(BF16), htb_)
            dba = dba + _dot_nt((x * wex).astype(BF16), dhtb_)
            dlam = dlam + _dot3_r(dye * ch, rs)
            xbd = _dot3_r(x * bd, rs)
            e_ = w_end * xbd
            dlam = dlam - e_
            ddir = ddir + wnd * xbd
            hh = _dot3_r(jnp.broadcast_to(jnp.sum(dht * ht, axis=0, keepdims=True), (8, 256)), rs)[0:1]
            tot_term = jnp.sum(e_, axis=0, keepdims=True) + jnp.exp(tot) * hh
            dlam = dlam + jnp.where(row128 == (Q - 1 if d == 0 else 0), tot_term, 0.0)
            rhs = jnp.concatenate([dyb_, dhtb_], axis=0)
            maskt = (li >= si) if d == 0 else (li <= si)
            for r in range(HPG):
                j = 4 * d + r
                dc = dt[:, j:j + 1]
                lmt = jnp.where(maskt, jnp.exp(lat_[j:j + 1, :] - la[:, j:j + 1]), 0.0)
                wt = st * lmt * dc
                lhs = jnp.concatenate([wt, bm * w_end[:, j:j + 1]], axis=1).astype(BF16)
                hm = (lane >> 6) == r
                dxa = dxa + jnp.where(hm, _dot(lhs, rhs), 0.0)
                dwt = _dot_nt(jnp.where(hm, x, 0.0).astype(BF16), dyb_)
                dl = dwt * lmt
                gpt = dl * st
                cs = jnp.sum(gpt, axis=1, keepdims=True)
                ddir = ddir + jnp.where(lane128 == j, cs, 0.0)
                dlam = dlam - jnp.where(lane128 == j, cs * dc, 0.0)
                rows = rows + jnp.where(sub == j, jnp.sum(gpt * dc, axis=0, keepdims=True), 0.0)
                dst = dst + dl * dc
        dlam = dlam + rows.T
        dba = dba + _dot(dst.astype(BF16), cb_)
        dca = dca + _dot(dst.T.astype(BF16), bb_)
        isb = jnp.logical_and(lane128 >= 4, lane128 < 8)
        ddel = jnp.where(isb, _dot3_l(_tri(True), dlam), _dot3_l(_tri(False), dlam))
        a_l = a_ref[pl.ds(g, 1), :]
        ddt_ref[...] = ddir + a_l * ddel
        ga_ref[...] += jnp.broadcast_to(a_l * jnp.sum(dt * ddel, axis=0, keepdims=True), (8, 128))
        dx_ref[...] = dxa + sk_ref[...]
        dbo_ref[...] = dba
        dco_ref[...] = dca

    st3 = pl.BlockSpec((None, 128, 256), lambda g, c: (c, 0, g))
    sm = pl.BlockSpec((None, Q, 128), lambda g, c: (g, c, 0))
    smt = pl.BlockSpec((8, Q), lambda g, c: (g, c))
    return pl.pallas_call(
        kern, name="ssd_bwd_out", grid=(NG, nc),
        in_specs=[pl.BlockSpec((Q, 256), lambda g, c: (c, g)), pl.BlockSpec((Q, 128), lambda g, c: (c, 16 + g)),
                  pl.BlockSpec((Q, 128), lambda g, c: (c, 24 + g)), pl.BlockSpec((Q, 256), lambda g, c: (c, g)),
                  pl.BlockSpec((Q, 256), lambda g, c: (c, g)), sm, sm, smt, smt, st3, st3, st3, st3,
                  pl.BlockSpec((8, 128), lambda g, c: (0, 0))],
        out_specs=[pl.BlockSpec((Q, 256), lambda g, c: (c, g)), pl.BlockSpec((Q, 128), lambda g, c: (c, g)),
                   pl.BlockSpec((Q, 128), lambda g, c: (c, g)), sm, pl.BlockSpec((None, 8, 128), lambda g, c: (g, 0, 0))],
        out_shape=[jax.ShapeDtypeStruct((lext, DI), F32), jax.ShapeDtypeStruct((lext, NG * NS), F32),
                   jax.ShapeDtypeStruct((lext, NG * NS), F32), jax.ShapeDtypeStruct((NG, lext, 128), F32),
                   jax.ShapeDtypeStruct((NG, 8, 128), F32)],
        compiler_params=_cp(("parallel", "arbitrary")),
    )(xbc, xbc, xbc, dy, dxskip, dtg, lag, dtt, lat, htf, htb, dhf, dhb, a_rows)


def _post_fwd(yssm, xbc, proj_rest, dsk, gnw, nct):
    l = yssm.shape[0]

    def kern(y_ref, x_ref, z_ref, dsk_ref, w_ref, o_ref):
        y = y_ref[...] + dsk_ref[...] * x_ref[...]
        yz = y * _silu(z_ref[...])
        for g in range(NG):
            sl = slice(256 * g, 256 * (g + 1))
            v = yz[:, sl]
            r = lax.rsqrt(jnp.mean(v * v, axis=-1, keepdims=True) + EPS)
            o_ref[:, sl] = (v * r * w_ref[:, sl]).astype(BF16)

    return pl.pallas_call(
        kern, name="post_fwd", grid=(l // T,),
        in_specs=[pl.BlockSpec((T, DI), lambda i: (i, 0)), pl.BlockSpec((T, DI), lambda i: (i + nct, 0)),
                  pl.BlockSpec((T, DI), lambda i: (i, 0)), pl.BlockSpec((1, DI), lambda i: (0, 0)),
                  pl.BlockSpec((1, DI), lambda i: (0, 0))],
        out_specs=pl.BlockSpec((T, DI), lambda i: (i, 0)),
        out_shape=jax.ShapeDtypeStruct((l, DI), BF16),
        compiler_params=_cp(("parallel",)),
    )(yssm, xbc, proj_rest, dsk, gnw)


def _post_bwd(dgn, yssm, xbc, proj_rest, dsk, gnw, nct):
    l = yssm.shape[0]
    lext = xbc.shape[0]
    xi = lambda i: (jnp.maximum(i - nct, 0), 0)

    def kern(dg_ref, y_ref, x_ref, z_ref, dsk_ref, w_ref, dy_ref, sk_ref, dz_ref, gw_ref, gd_ref):
        i = pl.program_id(0)

        @pl.when(i == 0)
        def _():
            gw_ref[...] = jnp.zeros_like(gw_ref)
            gd_ref[...] = jnp.zeros_like(gd_ref)

        @pl.when(i < nct)
        def _():
            dy_ref[...] = jnp.zeros_like(dy_ref)
            sk_ref[...] = jnp.zeros_like(sk_ref)

        @pl.when(i >= nct)
        def _():
            xs = x_ref[...]
            z = z_ref[...]
            y = y_ref[...] + dsk_ref[...] * xs
            sz = _silu(z)
            yz = y * sz
            dgn_ = dg_ref[...]
            dyz_parts = []
            gws = []
            for g in range(NG):
                sl = slice(256 * g, 256 * (g + 1))
                v = yz[:, sl]
                r = lax.rsqrt(jnp.mean(v * v, axis=-1, keepdims=True) + EPS)
                vn = v * r
                dn = dgn_[:, sl] * w_ref[:, sl]
                gws.append(jnp.sum(dgn_[:, sl] * vn, axis=0, keepdims=True))
                dyz_parts.append(r * (dn - vn * jnp.mean(dn * vn, axis=-1, keepdims=True)))
            dyz = jnp.concatenate(dyz_parts, axis=1)
            gw_ref[...] += jnp.broadcast_to(jnp.concatenate(gws, axis=1), (8, DI))
            dy = dyz * sz
            dz_ref[...] = (dyz * y * _dsilu(z)).astype(BF16)
            gd_ref[...] += jnp.broadcast_to(jnp.sum(dy * xs, axis=0, keepdims=True), (8, DI))
            dy_ref[...] = dy
            sk_ref[...] = dy * dsk_ref[...]

    return pl.pallas_call(
        kern, name="post_bwd", grid=(lext // T,),
        in_specs=[pl.BlockSpec((T, DI), xi), pl.BlockSpec((T, DI), xi), pl.BlockSpec((T, DI), lambda i: (i, 0)),
                  pl.BlockSpec((T, DI), xi), pl.BlockSpec((1, DI), lambda i: (0, 0)), pl.BlockSpec((1, DI), lambda i: (0, 0))],
        out_specs=[pl.BlockSpec((T, DI), lambda i: (i, 0)), pl.BlockSpec((T, DI), lambda i: (i, 0)),
                   pl.BlockSpec((T, DI), xi), pl.BlockSpec((8, DI), lambda i: (0, 0)), pl.BlockSpec((8, DI), lambda i: (0, 0))],
        out_shape=[jax.ShapeDtypeStruct((lext, DI), F32), jax.ShapeDtypeStruct((lext, DI), F32),
                   jax.ShapeDtypeStruct((l, DI), BF16), jax.ShapeDtypeStruct((8, DI), F32), jax.ShapeDtypeStruct((8, DI), F32)],
        compiler_params=_cp(("arbitrary",)),
    )(dgn, yssm, xbc, proj_rest, dsk, gnw)


def _row_conv(a, w, transpose):
    pos = _iota((T, D), 0) & (GW - 1)
    acc = jnp.zeros((T, D), F32)
    for k in range(CK):
        off = (k - 15) if not transpose else (15 - k)
        if off == 0:
            acc = acc + w[k:k + 1] * a
        else:
            sh = pltpu.roll(a, (-off) % T, axis=0)
            ok = jnp.logical_and(pos + off >= 0, pos + off < GW)
            acc = acc + w[k:k + 1] * jnp.where(ok, sh, 0.0)
    return acc


def _conf_stats(ga, gb, cw, cb):
    a = ga * _sig(gb)
    cv = _row_conv(a, cw, False) + cb
    mu = jnp.mean(cv, axis=-1, keepdims=True)
    xc = cv - mu
    rs = lax.rsqrt(jnp.mean(xc * xc, axis=-1, keepdims=True) + EPS)
    return a, xc * rs, rs


def _conf_fwd(proj_rest, cw, cb, lw, lb):
    l = proj_rest.shape[0]

    def kern(ga_ref, gb_ref, cg_ref, cw_ref, cb_ref, lw_ref, lb_ref, o_ref):
        _, xh, _ = _conf_stats(ga_ref[...], gb_ref[...], cw_ref[...], cb_ref[...])
        ln = xh * lw_ref[...] + lb_ref[...]
        o_ref[...] = (_silu(ln) * _silu(cg_ref[...])).astype(BF16)

    vec = pl.BlockSpec((1, D), lambda i: (0, 0))
    return pl.pallas_call(
        kern, name="conf_fwd", grid=(l // T,),
        in_specs=[pl.BlockSpec((T, D), lambda i: (i, 2)), pl.BlockSpec((T, D), lambda i: (i, 3)),
                  pl.BlockSpec((T, D), lambda i: (i, 4)), pl.BlockSpec((32, D), lambda i: (0, 0)), vec, vec, vec],
        out_specs=pl.BlockSpec((T, D), lambda i: (i, 0)),
        out_shape=jax.ShapeDtypeStruct((l, D), BF16),
        compiler_params=_cp(("parallel",)),
    )(proj_rest, proj_rest, proj_rest, cw, cb, lw, lb)


def _conf_bwd(duc, proj_rest, cw, cb, lw, lb):
    l = proj_rest.shape[0]

    def kern(du_ref, ga_ref, gb_ref, cg_ref, cw_ref, cb_ref, lw_ref, lb_ref, o_ref, gcw_ref, gv_ref):
        @pl.when(pl.program_id(0) == 0)
        def _():
            gcw_ref[...] = jnp.zeros_like(gcw_ref)
            gv_ref[...] = jnp.zeros_like(gv_ref)

        ga, gb, cg, cw = ga_ref[...], gb_ref[...], cg_ref[...], cw_ref[...]
        a, xh, rs = _conf_stats(ga, gb, cw, cb_ref[...])
        ln = xh * lw_ref[...] + lb_ref[...]
        du = du_ref[...]
        o_ref[:, 2 * D:3 * D] = (du * _silu(ln) * _dsilu(cg)).astype(BF16)
        dln = du * _silu(cg) * _dsilu(ln)
        g_lw = jnp.sum(dln * xh, axis=0, keepdims=True)
        g_lb = jnp.sum(dln, axis=0, keepdims=True)
        dxh = dln * lw_ref[...]
        dcv = rs * (dxh - jnp.mean(dxh, axis=-1, keepdims=True) - xh * jnp.mean(dxh * xh, axis=-1, keepdims=True))
        g_cb = jnp.sum(dcv, axis=0, keepdims=True)
        rid = _iota((8, D), 0)
        gv_ref[...] += jnp.where(rid == 0, g_cb, jnp.where(rid == 1, g_lw, jnp.where(rid == 2, g_lb, 0.0)))
        da = _row_conv(dcv, cw, True)
        sg = _sig(gb)
        o_ref[:, 0:D] = (da * sg).astype(BF16)
        o_ref[:, D:2 * D] = (da * ga * sg * (1.0 - sg)).astype(BF16)
        pos = _iota((T, D), 0) & (GW - 1)
        for k in range(CK):
            off = k - 15
            if off == 0:
                sh = a
            else:
                ok = jnp.logical_and(pos + off >= 0, pos + off < GW)
                sh = jnp.where(ok, pltpu.roll(a, (-off) % T, axis=0), 0.0)
            gcw_ref[k:k + 1, :] += jnp.sum(dcv * sh, axis=0, keepdims=True)

    vec = pl.BlockSpec((1, D), lambda i: (0, 0))
    return pl.pallas_call(
        kern, name="conf_bwd", grid=(l // T,),
        in_specs=[pl.BlockSpec((T, D), lambda i: (i, 0)), pl.BlockSpec((T, D), lambda i: (i, 2)),
                  pl.BlockSpec((T, D), lambda i: (i, 3)), pl.BlockSpec((T, D), lambda i: (i, 4)),
                  pl.BlockSpec((32, D), lambda i: (0, 0)), vec, vec, vec],
        out_specs=[pl.BlockSpec((T, 3 * D), lambda i: (i, 0)), pl.BlockSpec((32, D), lambda i: (0, 0)),
                   pl.BlockSpec((8, D), lambda i: (0, 0))],
        out_shape=[jax.ShapeDtypeStruct((l, 3 * D), BF16), jax.ShapeDtypeStruct((32, D), F32),
                   jax.ShapeDtypeStruct((8, D), F32)],
        compiler_params=_cp(("arbitrary",)),
    )(duc, proj_rest, proj_rest, proj_rest, cw, cb, lw, lb)


def _merge_fwd(bs, bc, proj_rest):
    l = bs.shape[0]

    def kern(bs_ref, bc_ref, g1_ref, g2_ref, o_ref):
        o_ref[...] = (_sig(g1_ref[...]) * bs_ref[...] + _sig(g2_ref[...]) * bc_ref[...]).astype(BF16)

    blk = pl.BlockSpec((T, D), lambda i: (i, 0))
    return pl.pallas_call(
        kern, name="merge_fwd", grid=(l // T,),
        in_specs=[blk, blk, pl.BlockSpec((T, D), lambda i: (i, 5)), pl.BlockSpec((T, D), lambda i: (i, 6))],
        out_specs=blk, out_shape=jax.ShapeDtypeStruct((l, D), BF16),
        compiler_params=_cp(("parallel",)),
    )(bs, bc, proj_rest, proj_rest)


def _merge_bwd(dm, bs, bc, proj_rest):
    l = bs.shape[0]

    def kern(dm_ref, bs_ref, bc_ref, g1_ref, g2_ref, dbs_ref, dbc_ref, dg_ref):
        dm_ = dm_ref[...]
        s1, s2 = _sig(g1_ref[...]), _sig(g2_ref[...])
        dbs_ref[...] = (dm_ * s1).astype(BF16)
        dbc_ref[...] = (dm_ * s2).astype(BF16)
        dg_ref[:, 0:D] = (dm_ * bs_ref[...] * s1 * (1.0 - s1)).astype(BF16)
        dg_ref[:, D:2 * D] = (dm_ * bc_ref[...] * s2 * (1.0 - s2)).astype(BF16)

    blk = pl.BlockSpec((T, D), lambda i: (i, 0))
    return pl.pallas_call(
        kern, name="merge_bwd", grid=(l // T,),
        in_specs=[blk, blk, blk, pl.BlockSpec((T, D), lambda i: (i, 5)), pl.BlockSpec((T, D), lambda i: (i, 6))],
        out_specs=[blk, blk, pl.BlockSpec((T, 2 * D), lambda i: (i, 0))],
        out_shape=[jax.ShapeDtypeStruct((l, D), BF16), jax.ShapeDtypeStruct((l, D), BF16),
                   jax.ShapeDtypeStruct((l, 2 * D), BF16)],
        compiler_params=_cp(("parallel",)),
    )(dm, bs, bc, proj_rest, proj_rest)


def _final(x, out, tgt, mod, fw):
    l = x.shape[0]

    def kern(x_ref, o_ref, t_ref, mod_ref, fw_ref, ls_ref, dx2_ref, do_ref, gv_ref):
        @pl.when(pl.program_id(0) == 0)
        def _():
            ls_ref[...] = jnp.zeros_like(ls_ref)
            gv_ref[...] = jnp.zeros_like(gv_ref)

        gate = mod_ref[0:1, 2 * D:3 * D]
        o = o_ref[...]
        x2 = x_ref[...] + gate * o
        r = lax.rsqrt(jnp.mean(x2 * x2, axis=-1, keepdims=True) + EPS)
        yn = x2 * r
        fw_ = fw_ref[...]
        e = yn * fw_ - t_ref[...]
        ls_ref[...] += jnp.full((8, 128), 1.0, F32) * (0.5 / D) * jnp.sum(e * e)
        dy = e * (1.0 / D)
        g_fw = jnp.sum(dy * yn, axis=0, keepdims=True)
        dyn = dy * fw_
        dx2 = r * (dyn - yn * jnp.mean(dyn * yn, axis=-1, keepdims=True))
        g_gate = jnp.sum(dx2 * o, axis=0, keepdims=True)
        rid = _iota((8, D), 0)
        gv_ref[...] += jnp.where(rid == 0, g_fw, jnp.where(rid == 1, g_gate, 0.0))
        dx2_ref[...] = dx2
        do_ref[...] = (dx2 * gate).astype(BF16)

    blk = pl.BlockSpec((T, D), lambda i: (i, 0))
    return pl.pallas_call(
        kern, name="final", grid=(l // T,),
        in_specs=[blk, blk, blk, pl.BlockSpec((8, 3 * D), lambda i: (0, 0)), pl.BlockSpec((1, D), lambda i: (0, 0))],
        out_specs=[pl.BlockSpec((8, 128), lambda i: (0, 0)), blk, blk, pl.BlockSpec((8, D), lambda i: (0, 0))],
        out_shape=[jax.ShapeDtypeStruct((8, 128), F32), jax.ShapeDtypeStruct((l, D), F32),
                   jax.ShapeDtypeStruct((l, D), BF16), jax.ShapeDtypeStruct((8, D), F32)],
        compiler_params=_cp(("arbitrary",)),
    )(x, out, tgt, mod, fw)


def _perm_dt_cols(w):
    s = w.shape[:-1]
    return w.reshape(*s, 2, NG, HPG).swapaxes(-3, -2).reshape(*s, 64)


def _unperm_dt_cols(w):
    s = w.shape[:-1]
    return w.reshape(*s, NG, 2, HPG).swapaxes(-3, -2).reshape(*s, 64)


def _pad_lanes(v, width):
    return jnp.pad(v, ((0, 0), (0, width - v.shape[1])))


def _local_step(x, c, ctx, tgt, w):
    l = x.shape[0]
    nct = CTX // T
    ncc = CTX // Q
    lext = l + CTX

    w_mod = w["w_mod"].astype(BF16)
    w_in = w["w_in"].astype(BF16)
    w_ssd = jnp.concatenate([w_in[:, :XBC], _perm_dt_cols(w_in[:, XBC:XBC + 64]), jnp.zeros((D, 64), BF16)], axis=1)
    w_rest = w_in[:, XBC + 64:]
    w_os, w_oc, w_o = w["w_out_ssm"].astype(BF16), w["w_out_conf"].astype(BF16), w["w_out"].astype(BF16)
    cw8 = jnp.pad(w["ssm_conv_w"], ((0, 4), (0, 0)))
    cb_s = w["ssm_conv_b"].reshape(1, XBC)
    dtb = _pad_lanes(_perm_dt_cols(w["dt_bias"].reshape(1, 64)), 128)
    a_all = -jnp.exp(w["a_log"].reshape(1, 64))
    a_perm = _pad_lanes(_perm_dt_cols(a_all), 128)
    a_rows = _pad_lanes(_perm_dt_cols(a_all).reshape(NG, 8), 128)
    dsk = jnp.repeat(w["d_skip"].reshape(NH), HP).reshape(1, DI)
    gnw = w["ssm_norm_w"].reshape(1, DI)
    ccw = jnp.pad(w["conf_conv_w"], ((0, 1), (0, 0)))
    ccb, clw, clb = w["conf_conv_b"].reshape(1, D), w["conf_ln_w"].reshape(1, D), w["conf_ln_b"].reshape(1, D)
    nw = w["norm_w"].reshape(1, D)
    fw = w["final_norm_w"].reshape(1, D)
    cc = jnp.concatenate([c.reshape(1, D), w["c_ctx"].reshape(1, D), jnp.zeros((6, D), F32)], axis=0)

    mod = _mod_fwd(cc, w_mod, w["b_mod"].reshape(1, 3 * D))
    xe = jnp.concatenate([ctx, x], axis=0)
    h = _norm_fwd(xe, mod, nw, nct)
    proj_ssd = _mm(h, w_ssd, "nn", lext, SSDW, D, T, SSDW // 3, D, F32, "proj_ssd")
    proj_rest = _mm(h, w_rest, "nn", l, RESTW, D, T, 1024, D, F32, "proj_rest", a_off=nct)
    xbc = _conv_fwd(proj_ssd, cw8, cb_s, nct)
    dtg, lag, dtt, lat = _dt_fwd(proj_ssd, dtb, a_perm)
    htf, htb = _ssd_state(xbc, dtg, lag, ncc)
    yssm = _ssd_out(xbc, dtg, lag, dtt, lat, htf, htb, ncc)
    gn = _post_fwd(yssm, xbc, proj_rest, dsk, gnw, nct)
    bs = _mm(gn, w_os, "nn", l, D, DI, T, D, DI, F32, "out_ssm")
    uc = _conf_fwd(proj_rest, ccw, ccb, clw, clb)
    bc = _mm(uc, w_oc, "nn", l, D, D, T, D, D, F32, "out_conf")
    merged = _merge_fwd(bs, bc, proj_rest)
    out = _mm(merged, w_o, "nn", l, D, D, T, D, D, F32, "out_proj")
    lsum, dx2, dout, gv_fin = _final(x, out, tgt, mod, fw)

    g = {}
    g["final_norm_w"] = gv_fin[0]
    dmerged = _mm(dout, w_o, "nt", l, D, D, T, D, D, F32, "d_merged")
    g["w_out"] = _mm(merged, dout, "tn", D, D, l, D, D, 512, F32, "g_w_out")
    dbs, dbc, dg = _merge_bwd(dmerged, bs, bc, proj_rest)
    dgn = _mm(dbs, w_os, "nt", l, DI, D, T, DI, D, F32, "d_gn")
    g["w_out_ssm"] = _mm(gn, dbs, "tn", DI, D, l, DI, D, 512, F32, "g_w_out_ssm")
    duc = _mm(dbc, w_oc, "nt", l, D, D, T, D, D, F32, "d_uc")
    g["w_out_conf"] = _mm(uc, dbc, "tn", D, D, l, D, D, 512, F32, "g_w_out_conf")
    dconf, gcw, gv_conf = _conf_bwd(duc, proj_rest, ccw, ccb, clw, clb)
    g["conf_conv_w"] = gcw[:CK]
    g["conf_conv_b"], g["conf_ln_w"], g["conf_ln_b"] = gv_conf[0], gv_conf[1], gv_conf[2]
    dy, dxskip, dz, ggnw, gdsk = _post_bwd(dgn, yssm, xbc, proj_rest, dsk, gnw, nct)
    g["ssm_norm_w"] = ggnw[0]
    g["d_skip"] = gdsk[0].reshape(NH, HP).sum(axis=1)
    dhf, dhb = _ssd_bwd_state(xbc, dy, lag, ncc)
    dxs, dbm, dcm, ddtg, galog = _ssd_bwd_out(xbc, dy, dxskip, dtg, lag, dtt, lat, htf, htb, dhf, dhb, a_rows)
    g["a_log"] = _unperm_dt_cols(galog[:, 0, 0:8].reshape(1, 64)).reshape(2, NH)
    dus, gws, gbs = [], [], []
    for dpost, off, width, nm in ((dxs, 0, DI, "conv_bwd_x"), (dbm, DI, NG * NS, "conv_bwd_b"), (dcm, DI + NG * NS, NG * NS, "conv_bwd_c")):
        du_, gw_, gb_ = _conv_bwd(dpost, proj_ssd, cw8, cb_s, off, width, nct, nm)
        dus.append(du_)
        gws.append(gw_[:SK])
        gbs.append(gb_[0])
    g["ssm_conv_w"] = jnp.concatenate(gws, axis=1)
    g["ssm_conv_b"] = jnp.concatenate(gbs, axis=0)
    ddt_raw, gdtb = _dt_bwd(ddtg, proj_ssd, dtb)
    g["dt_bias"] = _unperm_dt_cols(gdtb[0:1, 0:64]).reshape(2, NH)
    dproj_ssd = jnp.concatenate(dus + [ddt_raw], axis=1)
    dproj_rest = jnp.concatenate([dz, dconf, dg], axis=1)
    gw_ssd = _mm(h, dproj_ssd, "tn", D, SSDW, lext, D, SSDW // 3, 256, F32, "g_w_ssd")
    gw_rest = _mm(h, dproj_rest, "tn", D, RESTW, l, D, 1024, 256, F32, "g_w_rest", a_off=CTX // 256)
    g["w_in"] = jnp.concatenate([gw_ssd[:, :XBC], _unperm_dt_cols(gw_ssd[:, XBC:XBC + 64]), gw_rest], axis=1)
    dh_a = _mm(dproj_ssd, w_ssd, "nt", lext, D, SSDW, T, D, SSDW // 3, F32, "dh_ssd")
    dh_b = _mm(dproj_rest, w_rest, "nt", l, D, RESTW, T, D, 1024, F32, "dh_rest")
    grad_x, gnw_in, dss = _norm_bwd(dh_a, dh_b, xe, dx2, mod, nw, nct)
    g["norm_w"] = gnw_in[0]
    dmod = jnp.concatenate([jnp.concatenate([dss[0:1], gv_fin[1:2]], axis=1),
                            jnp.concatenate([dss[1:2], jnp.zeros((1, D), F32)], axis=1),
                            jnp.zeros((6, 3 * D), F32)], axis=0)
    gwm, gbm, gcc = _mod_bwd(dmod, cc, cc.T, w_mod)
    g["w_mod"], g["b_mod"], g["c_ctx"] = gwm, gbm[0], gcc[1]
    return lsum[0, 0], grad_x, g


NSHARD = 4
R_MOD, R_IN, R_OS, R_OC, R_O, R_SC, R_CC = 768, 2832, 512, 256, 256, 4, 8
O_MOD = 0
O_IN = O_MOD + R_MOD
O_OS = O_IN + R_IN
O_OC = O_OS + R_OS
O_O = O_OC + R_OC
O_SC = O_O + R_O
O_CC = O_SC + R_SC
PROWS = 4640
HALF = PROWS // 2
RB = 464
SROWS = 16
SHARDED = ("w_mod", "w_in", "w_out_ssm", "w_out_conf", "w_out", "ssm_conv_w", "conf_conv_w")
SMALL = (("b_mod", 3 * D), ("norm_w", D), ("ssm_conv_b", XBC), ("dt_bias", 64), ("a_log", 64), ("d_skip", NH),
         ("ssm_norm_w", DI), ("conf_conv_b", D), ("conf_ln_w", D), ("conf_ln_b", D), ("final_norm_w", D), ("c_ctx", D))
SMALL_OFF = {"b_mod": 0, "norm_w": 3 * D, "ssm_conv_b": 4 * D, "dt_bias": 8 * D, "a_log": 8 * D + 64, "d_skip": 8 * D + 128,
             "ssm_norm_w": 9 * D, "conf_conv_b": 11 * D, "conf_ln_w": 12 * D, "conf_ln_b": 13 * D, "final_norm_w": 14 * D,
             "c_ctx": 15 * D}


def _pack_shard(s):
    cc = jnp.pad(s["conf_conv_w"].reshape(1, CK * 256), ((0, 0), (0, R_CC * D - CK * 256))).reshape(R_CC, D)
    return jnp.concatenate([s["w_mod"].reshape(R_MOD, D), s["w_in"].reshape(R_IN, D), s["w_out_ssm"], s["w_out_conf"],
                            s["w_out"], s["ssm_conv_w"], cc, jnp.zeros((PROWS - O_CC - R_CC, D), F32)], axis=0)


def _unpack_shard(p):
    return {"w_mod": p[O_MOD:O_IN].reshape(1, D, R_MOD), "w_in": p[O_IN:O_OS].reshape(1, D, R_IN),
            "w_out_ssm": p[O_OS:O_OC][None], "w_out_conf": p[O_OC:O_O][None], "w_out": p[O_O:O_SC][None],
            "ssm_conv_w": p[O_SC:O_CC][None],
            "conf_conv_w": p[O_CC:O_CC + R_CC].reshape(R_CC * D)[:CK * 256].reshape(1, CK, 256)}


def _pack_full(g):
    def cols(a, n):
        return a.reshape(a.shape[0], NSHARD, n).transpose(1, 0, 2)
    cc = jnp.pad(cols(g["conf_conv_w"], 256).reshape(NSHARD, CK * 256), ((0, 0), (0, R_CC * D - CK * 256)))
    return jnp.concatenate([cols(g["w_mod"], R_MOD).reshape(NSHARD, R_MOD, D), cols(g["w_in"], R_IN).reshape(NSHARD, R_IN, D),
                            g["w_out_ssm"].reshape(NSHARD, R_OS, D), g["w_out_conf"].reshape(NSHARD, R_OC, D),
                            g["w_out"].reshape(NSHARD, R_O, D), cols(g["ssm_conv_w"], D), cc.reshape(NSHARD, R_CC, D),
                            jnp.zeros((NSHARD, PROWS - O_CC - R_CC, D), F32)], axis=1)


def _unpack_gathered(gm, gs):
    def cols(a, r, n):
        return a.reshape(NSHARD, r, n).transpose(1, 0, 2).reshape(r, NSHARD * n)
    return {"w_mod": cols(gm[:, O_MOD:O_IN], D, R_MOD), "w_in": cols(gm[:, O_IN:O_OS], D, R_IN),
            "w_out_ssm": gm[:, O_OS:O_OC].reshape(DI, D), "w_out_conf": gm[:, O_OC:O_O].reshape(D, D),
            "w_out": gm[:, O_O:O_SC].reshape(D, D), "ssm_conv_w": cols(gs[:, 0:R_SC], SK, D),
            "conf_conv_w": cols(gs[:, R_SC:R_SC + R_CC].reshape(NSHARD, R_CC * D)[:, :CK * 256], CK, 256)}


def _pack_small(d):
    flat = jnp.zeros((SROWS * D,), F32)
    for name, n in SMALL:
        flat = lax.dynamic_update_slice(flat, d[name].reshape(n).astype(F32), (SMALL_OFF[name],))
    return flat.reshape(SROWS, D)


def _unpack_small(p, shapes):
    flat = p.reshape(SROWS * D)
    return {name: flat[SMALL_OFF[name]:SMALL_OFF[name] + n].reshape(shapes[name]) for name, n in SMALL}


MESH_ID = pl.DeviceIdType.MESH
ANY = pl.BlockSpec(memory_space=pl.ANY)


def _place():
    x, y, c = lax.axis_index("x"), lax.axis_index("y"), lax.axis_index("c")
    return x, y, c, [(1 - x, y), (x, 1 - y), (1 - x, 1 - y)]


def _rcopy(src, dst, send, recv, dev):
    return pltpu.make_async_remote_copy(src_ref=src, dst_ref=dst, send_sem=send, recv_sem=recv,
                                        device_id=dev, device_id_type=MESH_ID)


def _gather_weights(mats, small):
    def kern(m_ref, s_ref, gm_ref, gs_ref, send, recv, lsem):
        x, y, c, chips = _place()
        me = 2 * x + y
        sib = (x, y, 1 - c)
        mine = pl.ds(pl.multiple_of(c * HALF, 16), HALF)
        other = pl.ds(pl.multiple_of((1 - c) * HALF, 16), HALF)
        own_m = pltpu.make_async_copy(m_ref, gm_ref.at[me], lsem.at[0])
        own_s = pltpu.make_async_copy(s_ref, gs_ref.at[me], lsem.at[1])
        own_m.start()
        own_s.start()
        first = []
        for k, (px, py) in enumerate(chips):
            first.append(_rcopy(m_ref.at[mine], gm_ref.at[me, mine], send.at[k], recv.at[k], (px, py, c)))
            first.append(_rcopy(s_ref, gs_ref.at[me], send.at[3 + k], recv.at[3 + k], (px, py, c)))
        for cp in first:
            cp.start()
        passed = []
        for k, (px, py) in enumerate(chips):
            s = 2 * px + py
            _rcopy(m_ref.at[mine], gm_ref.at[s, mine], send.at[k], recv.at[k], sib).wait_recv()
            f = _rcopy(gm_ref.at[s, mine], gm_ref.at[s, mine], send.at[6 + k], recv.at[6 + k], sib)
            f.start()
            passed.append(f)
        for k, (px, py) in enumerate(chips):
            s = 2 * px + py
            _rcopy(s_ref, gs_ref.at[s], send.at[3 + k], recv.at[3 + k], sib).wait_recv()
            _rcopy(gm_ref.at[s, other], gm_ref.at[s, other], send.at[6 + k], recv.at[6 + k], sib).wait_recv()
        for cp in first + passed:
            cp.wait_send()
        own_m.wait()
        own_s.wait()

    return pl.pallas_call(
        kern, name="gather_weights", in_specs=[ANY, ANY], out_specs=[ANY, ANY],
        out_shape=[jax.ShapeDtypeStruct((NSHARD, PROWS, D), BF16), jax.ShapeDtypeStruct((NSHARD, SROWS, D), F32)],
        scratch_shapes=[pltpu.SemaphoreType.DMA((9,)), pltpu.SemaphoreType.DMA((9,)), pltpu.SemaphoreType.DMA((2,))],
    )(mats, small)


def _swap_halves(g):
    def kern(g_ref, o_ref, send, recv):
        x, y, c, _ = _place()
        other = pl.ds(pl.multiple_of((1 - c) * HALF, 8), HALF)
        cps = [_rcopy(g_ref.at[s, other], o_ref.at[s], send.at[s], recv.at[s], (x, y, 1 - c)) for s in range(NSHARD)]
        for cp in cps:
            cp.start()
        for cp in cps:
            cp.wait()

    return pl.pallas_call(
        kern, name="swap_halves", in_specs=[ANY], out_specs=ANY,
        out_shape=jax.ShapeDtypeStruct((NSHARD, HALF, D), F32),
        scratch_shapes=[pltpu.SemaphoreType.DMA((NSHARD,)), pltpu.SemaphoreType.DMA((NSHARD,))],
    )(g)


def _add_halves(cidx, g, ra):
    nb = HALF // RB

    def kern(c_ref, g_ref, a_ref, o_ref):
        o_ref[...] = g_ref[...] + a_ref[...]

    return pl.pallas_call(
        kern, name="add_halves",
        grid_spec=pltpu.PrefetchScalarGridSpec(
            num_scalar_prefetch=1, grid=(NSHARD, nb),
            in_specs=[pl.BlockSpec((None, RB, D), lambda s, i, c: (s, c[0] * nb + i, 0)),
                      pl.BlockSpec((None, RB, D), lambda s, i, c: (s, i, 0))],
            out_specs=pl.BlockSpec((None, RB, D), lambda s, i, c: (s, i, 0))),
        out_shape=jax.ShapeDtypeStruct((NSHARD, HALF, D), F32),
        compiler_params=_cp(("parallel", "parallel")),
    )(cidx, g, ra)


def _exchange_chips(p):
    def kern(p_ref, o_ref, send, recv):
        x, y, c, chips = _place()
        cps = [_rcopy(p_ref.at[2 * px + py], o_ref.at[k], send.at[k], recv.at[k], (px, py, c))
               for k, (px, py) in enumerate(chips)]
        for cp in cps:
            cp.start()
        for cp in cps:
            cp.wait()

    return pl.pallas_call(
        kern, name="exchange_chips", in_specs=[ANY], out_specs=ANY,
        out_shape=jax.ShapeDtypeStruct((3, HALF, D), F32),
        scratch_shapes=[pltpu.SemaphoreType.DMA((3,)), pltpu.SemaphoreType.DMA((3,))],
    )(p)


def _add_chips(midx, p, rb):
    def kern(m_ref, p_ref, r0_ref, r1_ref, r2_ref, o_ref):
        o_ref[...] = ((p_ref[...] + r0_ref[...]) + r1_ref[...]) + r2_ref[...]

    return pl.pallas_call(
        kern, name="add_chips",
        grid_spec=pltpu.PrefetchScalarGridSpec(
            num_scalar_prefetch=1, grid=(HALF // RB,),
            in_specs=[pl.BlockSpec((None, RB, D), lambda i, m: (m[0], i, 0))]
            + [pl.BlockSpec((None, RB, D), functools.partial(lambda i, m, k: (k, i, 0), k=k)) for k in range(3)],
            out_specs=pl.BlockSpec((RB, D), lambda i, m: (i, 0))),
        out_shape=jax.ShapeDtypeStruct((HALF, D), F32),
        compiler_params=_cp(("parallel",)),
    )(midx, p, rb, rb, rb)


def _share_halves(r):
    def kern(r_ref, o_ref, send, recv, lsem):
        x, y, c, _ = _place()
        mine = pl.ds(pl.multiple_of(c * HALF, 8), HALF)
        own = pltpu.make_async_copy(r_ref, o_ref.at[mine], lsem)
        own.start()
        cp = _rcopy(r_ref, o_ref.at[mine], send, recv, (x, y, 1 - c))
        cp.start()
        cp.wait()
        own.wait()

    return pl.pallas_call(
        kern, name="share_halves", in_specs=[ANY], out_specs=ANY,
        out_shape=jax.ShapeDtypeStruct((PROWS, D), F32),
        scratch_shapes=[pltpu.SemaphoreType.DMA, pltpu.SemaphoreType.DMA, pltpu.SemaphoreType.DMA],
    )(r)


def _reduce_small(s):
    def kern(s_ref, o_ref, buf, send, recv):
        x, y, c, _ = _place()
        me = 4 * x + 2 * y + c
        buf[me] = s_ref[...]
        cps = []
        for r in range(1, 8):
            peer = (1 - x if r & 4 else x, 1 - y if r & 2 else y, 1 - c if r & 1 else c)
            cps.append(_rcopy(s_ref, buf.at[me], send.at[r - 1], recv.at[r - 1], peer))
        for cp in cps:
            cp.start()
        for cp in cps:
            cp.wait()
        acc = buf[0]
        for i in range(1, 8):
            acc = acc + buf[i]
        o_ref[...] = acc

    return pl.pallas_call(
        kern, name="reduce_small",
        in_specs=[pl.BlockSpec(memory_space=pltpu.VMEM)], out_specs=pl.BlockSpec(memory_space=pltpu.VMEM),
        out_shape=jax.ShapeDtypeStruct((SROWS, D), F32),
        scratch_shapes=[pltpu.VMEM((8, SROWS, D), F32), pltpu.SemaphoreType.DMA((7,)), pltpu.SemaphoreType.DMA((7,))],
    )(s)


def _adamw(g, w, m, v, rb, name):
    rows = g.shape[0]

    def kern(g_ref, w_ref, m_ref, v_ref, d_ref, nm_ref, nv_ref):
        g_ = g_ref[...]
        m_ = ADAM_B1 * m_ref[...] + (1.0 - ADAM_B1) * g_
        v_ = ADAM_B2 * v_ref[...] + (1.0 - ADAM_B2) * jnp.square(g_)
        m_hat = m_ / (1.0 - ADAM_B1 ** ADAM_STEP)
        v_hat = v_ / (1.0 - ADAM_B2 ** ADAM_STEP)
        d_ref[...] = -ADAM_LR * (m_hat / (jnp.sqrt(v_hat) + ADAM_EPS) + ADAM_WD * w_ref[...])
        nm_ref[...] = m_
        nv_ref[...] = v_

    blk = pl.BlockSpec((rb, D), lambda i: (i, 0))
    return pl.pallas_call(
        kern, name=name, grid=(rows // rb,), in_specs=[blk] * 4, out_specs=[blk] * 3,
        out_shape=[jax.ShapeDtypeStruct((rows, D), F32)] * 3,
        compiler_params=_cp(("parallel",)),
    )(g, w, m, v)


WEIGHTS = ("c_ctx", "w_mod", "b_mod", "norm_w", "w_in", "ssm_conv_w", "ssm_conv_b", "dt_bias", "a_log", "d_skip",
           "ssm_norm_w", "w_out_ssm", "conf_conv_w", "conf_conv_b", "conf_ln_w", "conf_ln_b", "w_out_conf", "w_out",
           "final_norm_w")


def kernel(x, c, ctx, c_ctx, w_mod, b_mod, norm_w, w_in, ssm_conv_w, ssm_conv_b, dt_bias, a_log, d_skip, ssm_norm_w, w_out_ssm, conf_conv_w, conf_conv_b, conf_ln_w, conf_ln_b, w_out_conf, w_out, final_norm_w, loss_target, m_c_ctx, m_w_mod, m_b_mod, m_norm_w, m_w_in, m_ssm_conv_w, m_ssm_conv_b, m_dt_bias, m_a_log, m_d_skip, m_ssm_norm_w, m_w_out_ssm, m_conf_conv_w, m_conf_conv_b, m_conf_ln_w, m_conf_ln_b, m_w_out_conf, m_w_out, m_final_norm_w, v_c_ctx, v_w_mod, v_b_mod, v_norm_w, v_w_in, v_ssm_conv_w, v_ssm_conv_b, v_dt_bias, v_a_log, v_d_skip, v_ssm_norm_w, v_w_out_ssm, v_conf_conv_w, v_conf_conv_b, v_conf_ln_w, v_conf_ln_b, v_w_out_conf, v_w_out, v_final_norm_w):
    wv = (c_ctx, w_mod, b_mod, norm_w, w_in, ssm_conv_w, ssm_conv_b, dt_bias, a_log, d_skip, ssm_norm_w, w_out_ssm,
          conf_conv_w, conf_conv_b, conf_ln_w, conf_ln_b, w_out_conf, w_out, final_norm_w)
    mv = (m_c_ctx, m_w_mod, m_b_mod, m_norm_w, m_w_in, m_ssm_conv_w, m_ssm_conv_b, m_dt_bias, m_a_log, m_d_skip,
          m_ssm_norm_w, m_w_out_ssm, m_conf_conv_w, m_conf_conv_b, m_conf_ln_w, m_conf_ln_b, m_w_out_conf, m_w_out,
          m_final_norm_w)
    vv = (v_c_ctx, v_w_mod, v_b_mod, v_norm_w, v_w_in, v_ssm_conv_w, v_ssm_conv_b, v_dt_bias, v_a_log, v_d_skip,
          v_ssm_norm_w, v_w_out_ssm, v_conf_conv_w, v_conf_conv_b, v_conf_ln_w, v_conf_ln_b, v_w_out_conf, v_w_out,
          v_final_norm_w)
    shapes = {n: a.shape for n, a in zip(WEIGHTS, wv)}

    def squeeze(d):
        return {n: (a if n in ("c_ctx", "final_norm_w") else a[0]) for n, a in d.items()}

    w, m, v = (squeeze(dict(zip(WEIGHTS, t))) for t in (wv, mv, vv))

    pw, pm, pv = _pack_shard(w), _pack_shard(m), _pack_shard(v)
    gm, gs = _gather_weights(pw.astype(BF16), pw[O_SC:PROWS])
    full = dict(w)
    full.update(_unpack_gathered(gm, gs))

    lsum, grad_x, g = _local_step(x[0], c, ctx[0], loss_target[0], full)
    loss = lax.psum(lsum, ("x", "y", "c"))

    cidx = lax.axis_index("c").astype(jnp.int32).reshape(1)
    midx = (2 * lax.axis_index("x") + lax.axis_index("y")).astype(jnp.int32).reshape(1)
    gp = _pack_full(g)
    part = _add_halves(cidx, gp, _swap_halves(gp))
    red = _add_chips(midx, part, _exchange_chips(part))
    g_sh = _share_halves(red)
    g_sm = _reduce_small(_pack_small(g))

    d_sh, nm_sh, nv_sh = _adamw(g_sh, pw, pm, pv, RB, "adamw_sharded")
    d_sm, nm_sm, nv_sm = _adamw(g_sm, _pack_small(w), _pack_small(m), _pack_small(v), SROWS, "adamw_small")

    outs = []
    for p_sh, p_sm in ((g_sh, g_sm), (d_sh, d_sm), (nm_sh, nm_sm), (nv_sh, nv_sm)):
        d = _unpack_shard(p_sh)
        d.update(_unpack_small(p_sm, shapes))
        outs.extend(d[n] for n in WEIGHTS)
    return (loss, grad_x[None], *outs)
```

```python
import functools

import jax
import jax.numpy as jnp
from jax import lax
from jax.experimental import pallas as pl
from jax.experimental.pallas import tpu as pltpu

F32, BF16 = jnp.float32, jnp.bfloat16

D = 1024
DI = 2048
NH = 32
HP = 64
NG = 8
HPG = 4
NS = 128
Q = 128
GW = 64
CK = 31
SK = 4
CTX = 256
EPS = 1e-6
XBC = DI + 2 * NG * NS
SSDW = XBC + 128
RESTW = 7168
T = 256
VMEM_LIMIT = 56 * 1024 * 1024

ADAM_LR, ADAM_B1, ADAM_B2, ADAM_EPS, ADAM_WD, ADAM_STEP = 0.001, 0.9, 0.999, 1e-08, 0.01, 10


def _cp(sem):
    return pltpu.CompilerParams(dimension_semantics=sem, vmem_limit_bytes=VMEM_LIMIT)


def _sig(x):
    return jax.nn.sigmoid(x)


def _silu(x):
    return x * _sig(x)


def _dsilu(x):
    s = _sig(x)
    return s * (1.0 + x * (1.0 - s))


def _dot(a, b):
    return jnp.dot(a, b, preferred_element_type=F32)


def _dot_nt(a, b):
    return lax.dot_general(a, b, (((1,), (1,)), ((), ())), preferred_element_type=F32)


def _split3(x):
    h = x.astype(BF16)
    r = x - h.astype(F32)
    m = r.astype(BF16)
    l = (r - m.astype(F32)).astype(BF16)
    return h, m, l


def _dot3_l(sel, x):
    h, m, l = _split3(x)
    return _dot(sel, h) + _dot(sel, m) + _dot(sel, l)


def _dot3_r(x, sel):
    h, m, l = _split3(x)
    return _dot(h, sel) + _dot(m, sel) + _dot(l, sel)


def _split2(x):
    h = x.astype(BF16)
    return h, (x - h.astype(F32)).astype(BF16)


def _dot2_l(sel, x):
    h, l = _split2(x)
    return _dot(sel, h) + _dot(sel, l)


def _dot2_r(x, sel):
    h, l = _split2(x)
    return _dot(h, sel) + _dot(l, sel)


def _iota(shape, dim):
    return lax.broadcasted_iota(jnp.int32, shape, dim)


def _mm(a, b, dims, m, n, k, bm, bn, bk, out_dtype, name):
    nk = k // bk
    assert m % bm == 0 and n % bn == 0 and k % bk == 0, (name, m, n, k, bm, bn, bk)

    def prod(a_ref, b_ref):
        av = a_ref[...].astype(BF16)
        bv = b_ref[...].astype(BF16)
        if dims == "nn":
            return _dot(av, bv)
        if dims == "nt":
            return _dot_nt(av, bv)
        return lax.dot_general(av, bv, (((0,), (0,)), ((), ())), preferred_element_type=F32)

    def kern_one(a_ref, b_ref, o_ref):
        o_ref[...] = prod(a_ref, b_ref).astype(out_dtype)

    def kern_acc(a_ref, b_ref, o_ref, acc):
        kk = pl.program_id(2)

        @pl.when(kk == 0)
        def _():
            acc[...] = jnp.zeros_like(acc)

        acc[...] += prod(a_ref, b_ref)

        @pl.when(kk == nk - 1)
        def _():
            o_ref[...] = acc[...].astype(out_dtype)

    if dims == "nn":
        a_spec = pl.BlockSpec((bm, bk), lambda j, i, kk: (i, kk))
        b_spec = pl.BlockSpec((bk, bn), lambda j, i, kk: (kk, j))
    elif dims == "nt":
        a_spec = pl.BlockSpec((bm, bk), lambda j, i, kk: (i, kk))
        b_spec = pl.BlockSpec((bn, bk), lambda j, i, kk: (j, kk))
    else:
        a_spec = pl.BlockSpec((bk, bm), lambda j, i, kk: (kk, i))
        b_spec = pl.BlockSpec((bk, bn), lambda j, i, kk: (kk, j))
    return pl.pallas_call(
        kern_one if nk == 1 else kern_acc, name=name,
        grid=(n // bn, m // bm, nk),
        in_specs=[a_spec, b_spec],
        out_specs=pl.BlockSpec((bm, bn), lambda j, i, kk: (i, j)),
        out_shape=jax.ShapeDtypeStruct((m, n), out_dtype),
        scratch_shapes=[] if nk == 1 else [pltpu.VMEM((bm, bn), F32)],
        compiler_params=_cp(("parallel", "parallel", "arbitrary")),
    )(a, b)


def _mod_fwd(cc, w_mod, b_mod):
    def kern(cc_ref, w_ref, b_ref, o_ref):
        s = _silu(cc_ref[...]).astype(BF16)
        o_ref[...] = _dot(s, w_ref[...]) + b_ref[...]

    return pl.pallas_call(
        kern, name="mod_fwd", grid=(3,),
        in_specs=[pl.BlockSpec((8, D), lambda j: (0, 0)), pl.BlockSpec((D, D), lambda j: (0, j)),
                  pl.BlockSpec((1, D), lambda j: (0, j))],
        out_specs=pl.BlockSpec((8, D), lambda j: (0, j)),
        out_shape=jax.ShapeDtypeStruct((8, 3 * D), F32),
        compiler_params=_cp(("parallel",)),
    )(cc, w_mod, b_mod)


def _mod_bwd(dmod, cc, cct, w_mod):
    def kern(dm_ref, cc_ref, cct_ref, w_ref, gw_ref, gb_ref, gc_ref):
        kk = pl.program_id(0)
        dm = dm_ref[...]
        sct = _silu(cct_ref[...])
        gw_ref[...] = sct[:, 0:1] * dm[0:1, :] + sct[:, 1:2] * dm[1:2, :]
        gb_ref[...] = jnp.broadcast_to(dm[0:1, :] + dm[1:2, :], dm.shape)

        @pl.when(kk == 0)
        def _():
            gc_ref[...] = jnp.zeros_like(gc_ref)

        gc_ref[...] += _dot_nt(dm.astype(BF16), w_ref[...])

        @pl.when(kk == 2)
        def _():
            gc_ref[...] = gc_ref[...] * _dsilu(cc_ref[...])

    return pl.pallas_call(
        kern, name="mod_bwd", grid=(3,),
        in_specs=[pl.BlockSpec((8, D), lambda j: (0, j)), pl.BlockSpec((8, D), lambda j: (0, 0)),
                  pl.BlockSpec((D, 8), lambda j: (0, 0)), pl.BlockSpec((D, D), lambda j: (0, j))],
        out_specs=[pl.BlockSpec((D, D), lambda j: (0, j)), pl.BlockSpec((8, D), lambda j: (0, j)),
                   pl.BlockSpec((8, D), lambda j: (0, 0))],
        out_shape=[jax.ShapeDtypeStruct((D, 3 * D), F32), jax.ShapeDtypeStruct((8, 3 * D), F32),
                   jax.ShapeDtypeStruct((8, D), F32)],
        compiler_params=_cp(("arbitrary",)),
    )(dmod, cc, cct, w_mod)


def _ext_specs(nct):
    return (pl.BlockSpec((T, D), lambda i: (jnp.minimum(i, nct - 1), 0)),
            pl.BlockSpec((T, D), lambda i: (jnp.maximum(i - nct, 0), 0)))


def _norm_fwd(ctx, xl, mod, nw, nct):
    lext = ctx.shape[0] + xl.shape[0]

    def kern(c_ref, x_ref, mod_ref, nw_ref, h_ref):
        is_ctx = pl.program_id(0) < nct
        x = jnp.where(is_ctx, c_ref[...], x_ref[...])
        r = lax.rsqrt(jnp.mean(x * x, axis=-1, keepdims=True) + EPS)
        xn = x * r * nw_ref[...]
        shift = jnp.where(is_ctx, mod_ref[1:2, 0:D], mod_ref[0:1, 0:D])
        scale = jnp.where(is_ctx, mod_ref[1:2, D:2 * D], mod_ref[0:1, D:2 * D])
        h_ref[...] = (xn * (1.0 + scale) + shift).astype(BF16)

    return pl.pallas_call(
        kern, name="norm_fwd", grid=(lext // T,),
        in_specs=[*_ext_specs(nct), pl.BlockSpec((8, 3 * D), lambda i: (0, 0)),
                  pl.BlockSpec((1, D), lambda i: (0, 0))],
        out_specs=pl.BlockSpec((T, D), lambda i: (i, 0)),
        out_shape=jax.ShapeDtypeStruct((lext, D), BF16),
        compiler_params=_cp(("parallel",)),
    )(ctx, xl, mod, nw)


def _norm_bwd(dha, dhb, ctx, xl, dx2, mod, nw, nct):
    lext = ctx.shape[0] + xl.shape[0]
    ntl = lext // T

    def kern(dha_ref, dhb_ref, c_ref, x_ref, dx2_ref, mod_ref, nw_ref, gx_ref, gnw_ref, dss_ref):
        i = pl.program_id(0)
        is_ctx = i < nct

        @pl.when(i == 0)
        def _():
            gnw_ref[...] = jnp.zeros_like(gnw_ref)
            dss_ref[...] = jnp.zeros_like(dss_ref)

        x = jnp.where(is_ctx, c_ref[...], x_ref[...])
        dh_ = dha_ref[...] + jnp.where(is_ctx, 0.0, dhb_ref[...])
        nw_ = nw_ref[...]
        r = lax.rsqrt(jnp.mean(x * x, axis=-1, keepdims=True) + EPS)
        xn = x * r
        scale = jnp.where(is_ctx, mod_ref[1:2, D:2 * D], mod_ref[0:1, D:2 * D])
        dsh = jnp.sum(dh_, axis=0, keepdims=True)
        dsc = jnp.sum(dh_ * (xn * nw_), axis=0, keepdims=True)
        row = jnp.concatenate([dsh, dsc], axis=1)
        rid = _iota((8, 2 * D), 0)
        dss_ref[...] += jnp.where(rid == jnp.where(is_ctx, 1, 0), row, 0.0)
        dxnw = dh_ * (1.0 + scale)
        gnw_ref[...] += jnp.broadcast_to(jnp.sum(dxnw * xn, axis=0, keepdims=True), (8, D))
        dxn = dxnw * nw_
        dx = r * (dxn - xn * jnp.mean(dxn * xn, axis=-1, keepdims=True))
        gx_ref[...] = dx2_ref[...] + dx

    return pl.pallas_call(
        kern, name="norm_bwd", grid=(ntl,),
        in_specs=[pl.BlockSpec((T, D), lambda i: (i, 0)), pl.BlockSpec((T, D), lambda i: (jnp.maximum(i - nct, 0), 0)),
                  *_ext_specs(nct),
                  pl.BlockSpec((T, D), lambda i: (jnp.maximum(i - nct, 0), 0)),
                  pl.BlockSpec((8, 3 * D), lambda i: (0, 0)), pl.BlockSpec((1, D), lambda i: (0, 0))],
        out_specs=[pl.BlockSpec((T, D), lambda i: (jnp.maximum(i - nct, 0), 0)),
                   pl.BlockSpec((8, D), lambda i: (0, 0)), pl.BlockSpec((8, 2 * D), lambda i: (0, 0))],
        out_shape=[jax.ShapeDtypeStruct((lext - nct * T, D), F32), jax.ShapeDtypeStruct((8, D), F32),
                   jax.ShapeDtypeStruct((8, 2 * D), F32)],
        compiler_params=_cp(("arbitrary",)),
    )(dha, dhb, ctx, xl, dx2, mod, nw)


CB = 512


def _halo_specs(width_blk, col_off_blocks, ntl):
    t8 = T // 8
    main = pl.BlockSpec((T, width_blk), lambda j, i: (i, j + col_off_blocks))
    prev = pl.BlockSpec((8, width_blk), lambda j, i: (jnp.maximum(i * t8 - 1, 0), j + col_off_blocks))
    nxt = pl.BlockSpec((8, width_blk), lambda j, i: (jnp.minimum((i + 1) * t8, ntl * t8 - 1), j + col_off_blocks))
    return main, prev, nxt


def _seq_edges(i, nct, ntl):
    starts = jnp.logical_or(i == 0, i == nct)
    ends = jnp.logical_or(i == nct - 1, i == ntl - 1)
    return starts, ends


def _shifted(ext, off):
    n = ext.shape[0]
    return pltpu.roll(ext, (-off) % n, axis=0)[8:8 + T]


def _conv_fwd(proj_ssd, cw, cb, nct):
    lext = proj_ssd.shape[0]
    ntl = lext // T

    def kern(u_ref, up_ref, un_ref, w_ref, b_ref, o_ref):
        i = pl.program_id(1)
        starts, ends = _seq_edges(i, nct, ntl)
        up = jnp.where(starts, 0.0, up_ref[...])
        un = jnp.where(ends, 0.0, un_ref[...])
        ext = jnp.concatenate([up, u_ref[...], un], axis=0)
        w = w_ref[...]
        pre = b_ref[...] + w[0:1] * _shifted(ext, -2) + w[1:2] * _shifted(ext, -1) \
            + w[2:3] * u_ref[...] + w[3:4] * _shifted(ext, 1)
        o_ref[...] = _silu(pre)

    main, prev, nxt = _halo_specs(CB, 0, ntl)
    return pl.pallas_call(
        kern, name="conv_fwd", grid=(XBC // CB, ntl),
        in_specs=[main, prev, nxt, pl.BlockSpec((8, CB), lambda j, i: (0, j)), pl.BlockSpec((1, CB), lambda j, i: (0, j))],
        out_specs=pl.BlockSpec((T, CB), lambda j, i: (i, j)),
        out_shape=jax.ShapeDtypeStruct((lext, XBC), F32),
        compiler_params=_cp(("parallel", "parallel")),
    )(proj_ssd, proj_ssd, proj_ssd, cw, cb)


def _conv_bwd(dpost, proj_ssd, cw, cb, col_off, width, nct, name):
    lext = proj_ssd.shape[0]
    ntl = lext // T
    cob = col_off // CB

    def kern(u_ref, up_ref, un_ref, d_ref, dp_ref, dn_ref, w_ref, b_ref, du_ref, gw_ref, gb_ref):
        i = pl.program_id(1)

        @pl.when(i == 0)
        def _():
            gw_ref[...] = jnp.zeros_like(gw_ref)
            gb_ref[...] = jnp.zeros_like(gb_ref)

        starts, ends = _seq_edges(i, nct, ntl)
        ext = jnp.concatenate([jnp.where(starts, 0.0, up_ref[...]), u_ref[...], jnp.where(ends, 0.0, un_ref[...])], axis=0)
        dext = jnp.concatenate([jnp.where(starts, 0.0, dp_ref[...]), d_ref[...], jnp.where(ends, 0.0, dn_ref[...])], axis=0)
        w = w_ref[...]
        n = ext.shape[0]
        pre = b_ref[...] + w[0:1] * pltpu.roll(ext, 2, axis=0) + w[1:2] * pltpu.roll(ext, 1, axis=0) \
            + w[2:3] * ext + w[3:4] * pltpu.roll(ext, n - 1, axis=0)
        dpre = dext * _dsilu(pre)
        dm = dpre[8:8 + T]
        du = w[0:1] * _shifted(dpre, 2) + w[1:2] * _shifted(dpre, 1) + w[2:3] * dm + w[3:4] * _shifted(dpre, -1)
        du_ref[...] = du.astype(BF16)
        g0 = jnp.sum(dm * _shifted(ext, -2), axis=0, keepdims=True)
        g1 = jnp.sum(dm * _shifted(ext, -1), axis=0, keepdims=True)
        g2 = jnp.sum(dm * u_ref[...], axis=0, keepdims=True)
        g3 = jnp.sum(dm * _shifted(ext, 1), axis=0, keepdims=True)
        rid = _iota((8, CB), 0)
        gw_ref[...] += jnp.where(rid == 0, g0, jnp.where(rid == 1, g1, jnp.where(rid == 2, g2, jnp.where(rid == 3, g3, 0.0))))
        gb_ref[...] += jnp.broadcast_to(jnp.sum(dm, axis=0, keepdims=True), (8, CB))

    main, prev, nxt = _halo_specs(CB, cob, ntl)
    dmain, dprev, dnxt = _halo_specs(CB, 0, ntl)
    return pl.pallas_call(
        kern, name=name, grid=(width // CB, ntl),
        in_specs=[main, prev, nxt, dmain, dprev, dnxt,
                  pl.BlockSpec((8, CB), lambda j, i: (0, j + cob)), pl.BlockSpec((1, CB), lambda j, i: (0, j + cob))],
        out_specs=[pl.BlockSpec((T, CB), lambda j, i: (i, j)), pl.BlockSpec((8, CB), lambda j, i: (0, j)),
                   pl.BlockSpec((8, CB), lambda j, i: (0, j))],
        out_shape=[jax.ShapeDtypeStruct((lext, width), BF16), jax.ShapeDtypeStruct((8, width), F32),
                   jax.ShapeDtypeStruct((8, width), F32)],
        compiler_params=_cp(("parallel", "arbitrary")),
    )(proj_ssd, proj_ssd, proj_ssd, dpost, dpost, dpost, cw, cb)


def _tri(lower):
    r, c = _iota((Q, Q), 0), _iota((Q, Q), 1)
    return jnp.where((c <= r) if lower else (c >= r), 1.0, 0.0).astype(BF16)


def _is_bdir_lane(shape):
    ln = _iota(shape, len(shape) - 1)
    return jnp.logical_and(((ln >> 2) & 1) == 1, ln < 64)


def _dt_fwd(proj_ssd, dtb, av):
    lext = proj_ssd.shape[0]

    def kern(p_ref, b_ref, a_ref, dtg_ref, lag_ref, dtt_ref, lat_ref):
        lane = _iota((T, 128), 1)
        raw = p_ref[...] + b_ref[...]
        dt = jnp.where(lane < 64, jnp.maximum(raw, 0.0) + jnp.log1p(jnp.exp(-jnp.abs(raw))), 0.0)
        dta = dt * a_ref[...]
        tl, tu = _tri(True), _tri(False)
        isb = _is_bdir_lane((Q, 128))
        las = []
        for qq in range(T // Q):
            blk = dta[qq * Q:(qq + 1) * Q]
            las.append(jnp.where(isb, _dot3_l(tu, blk), _dot3_l(tl, blk)))
        la = jnp.concatenate(las, axis=0)
        for g in range(NG):
            sh = (128 - 8 * g) % 128
            dtg_ref[g] = jnp.where(lane < 8, pltpu.roll(dt, sh, axis=1) if sh else dt, 0.0)
            lag_ref[g] = jnp.where(lane < 8, pltpu.roll(la, sh, axis=1) if sh else la, 0.0)
        dtt_ref[...] = dt.T[0:64]
        lat_ref[...] = la.T[0:64]

    return pl.pallas_call(
        kern, name="dt_fwd", grid=(lext // T,),
        in_specs=[pl.BlockSpec((T, 128), lambda i: (i, XBC // 128)), pl.BlockSpec((1, 128), lambda i: (0, 0)),
                  pl.BlockSpec((1, 128), lambda i: (0, 0))],
        out_specs=[pl.BlockSpec((NG, T, 128), lambda i: (0, i, 0)), pl.BlockSpec((NG, T, 128), lambda i: (0, i, 0)),
                   pl.BlockSpec((64, T), lambda i: (0, i)), pl.BlockSpec((64, T), lambda i: (0, i))],
        out_shape=[jax.ShapeDtypeStruct((NG, lext, 128), F32), jax.ShapeDtypeStruct((NG, lext, 128), F32),
                   jax.ShapeDtypeStruct((64, lext), F32), jax.ShapeDtypeStruct((64, lext), F32)],
        compiler_params=_cp(("parallel",)),
    )(proj_ssd, dtb, av)


def _dt_bwd(ddtg, proj_ssd, dtb):
    lext = proj_ssd.shape[0]

    def kern(d_ref, p_ref, b_ref, o_ref, gb_ref):
        @pl.when(pl.program_id(0) == 0)
        def _():
            gb_ref[...] = jnp.zeros_like(gb_ref)

        acc = d_ref[0]
        for g in range(1, NG):
            acc = acc + pltpu.roll(d_ref[g], 8 * g, axis=1)
        draw = acc * _sig(p_ref[...] + b_ref[...])
        o_ref[...] = draw.astype(BF16)
        gb_ref[...] += jnp.broadcast_to(jnp.sum(draw, axis=0, keepdims=True), (8, 128))

    return pl.pallas_call(
        kern, name="dt_bwd", grid=(lext // T,),
        in_specs=[pl.BlockSpec((NG, T, 128), lambda i: (0, i, 0)), pl.BlockSpec((T, 128), lambda i: (i, XBC // 128)),
                  pl.BlockSpec((1, 128), lambda i: (0, 0))],
        out_specs=[pl.BlockSpec((T, 128), lambda i: (i, 0)), pl.BlockSpec((8, 128), lambda i: (0, 0))],
        out_shape=[jax.ShapeDtypeStruct((lext, 128), BF16), jax.ShapeDtypeStruct((8, 128), F32)],
        compiler_params=_cp(("arbitrary",)),
    )(ddtg, proj_ssd, dtb)


def _expand_sel(d):
    r, c = _iota((128, 256), 0), _iota((128, 256), 1)
    return jnp.where(r == 4 * d + (c >> 6), 1.0, 0.0).astype(BF16)


def _reduce_sel(d):
    r, c = _iota((256, 128), 0), _iota((256, 128), 1)
    return jnp.where(c == 4 * d + (r >> 6), 1.0, 0.0).astype(BF16)


def _chunk_of_bwd_dir(j, ncc, nc):
    return jnp.where(j < ncc, ncc - 1 - j, nc + ncc - 1 - j)


def _dir_terms(la, dt, d):
    lane = _iota(la.shape, 1)
    mine = jnp.logical_and(lane >= 4 * d, lane < 4 * d + 4)
    la = jnp.where(mine, la, 0.0)
    tot = la[Q - 1:Q] if d == 0 else la[0:1]
    wnd = jnp.exp(tot - la)
    return tot, wnd * jnp.where(mine, dt, 0.0), wnd


def _ssd_state(xbc, dtg, lag, ncc):
    lext = xbc.shape[0]
    nc = lext // Q

    def kern(xf_ref, bf_ref, dtf_ref, laf_ref, xb_ref, bb_ref, dtb_ref, lab_ref, hf_ref, hb_ref, sf, sb):
        @pl.when(pl.program_id(0) == 0)
        def _():
            sf[...] = jnp.zeros_like(sf)
            sb[...] = jnp.zeros_like(sb)

        for d, (x_ref, b_ref, dt_ref, la_ref, h_ref, s) in enumerate(
                ((xf_ref, bf_ref, dtf_ref, laf_ref, hf_ref, sf), (xb_ref, bb_ref, dtb_ref, lab_ref, hb_ref, sb))):
            h_ref[...] = s[...]
            ex = _expand_sel(d)
            for g in range(NG):
                cols = slice(256 * g, 256 * (g + 1))
                tot, w_end, _ = _dir_terms(la_ref[g], dt_ref[g], d)
                wexp = _dot2_r(w_end, ex)
                dexp = _dot2_r(jnp.broadcast_to(jnp.exp(tot), (8, 128)), ex)[0:1]
                xw = (x_ref[:, cols] * wexp).astype(BF16)
                s[:, cols] = s[:, cols] * dexp + _dot(b_ref[:, 128 * g:128 * (g + 1)].T.astype(BF16), xw)

    cb = functools.partial(_chunk_of_bwd_dir, ncc=ncc, nc=nc)
    sm = lambda f: pl.BlockSpec((NG, Q, 128), lambda j: (0, f(j), 0))
    one = lambda j: j
    return pl.pallas_call(
        kern, name="ssd_state", grid=(nc,),
        in_specs=[pl.BlockSpec((Q, DI), lambda j: (j, 0)), pl.BlockSpec((Q, NG * NS), lambda j: (j, 2)), sm(one), sm(one),
                  pl.BlockSpec((Q, DI), lambda j: (cb(j), 0)), pl.BlockSpec((Q, NG * NS), lambda j: (cb(j), 2)), sm(cb), sm(cb)],
        out_specs=[pl.BlockSpec((None, 128, DI), lambda j: (j, 0, 0)),
                   pl.BlockSpec((None, 128, DI), lambda j: (cb(j), 0, 0))],
        out_shape=[jax.ShapeDtypeStruct((nc, 128, DI), F32), jax.ShapeDtypeStruct((nc, 128, DI), F32)],
        scratch_shapes=[pltpu.VMEM((128, DI), F32), pltpu.VMEM((128, DI), F32)],
        compiler_params=_cp(("arbitrary",)),
    )(xbc, xbc, dtg, lag, xbc, xbc, dtg, lag)


def _ssd_out(xbc, dtg, lag, dtt, lat, htf, htb, ncc):
    lext = xbc.shape[0]
    nc = lext // Q
    ncx = nc - ncc

    def kern(x_ref, b_ref, c_ref, dtg_ref, lag_ref, dtt_ref, lat_ref, hf_ref, hb_ref, y_ref):
        x = x_ref[...]
        cm = c_ref[...]
        xb_ = x.astype(BF16)
        s_ = _dot_nt(cm.astype(BF16), b_ref[...].astype(BF16))
        li, si = _iota((Q, Q), 0), _iota((Q, Q), 1)
        lane = _iota((Q, 256), 1)
        la, dtt_, lat_ = lag_ref[...], dtt_ref[...], lat_ref[...]
        elam = jnp.exp(la)
        y = jnp.zeros((Q, 256), F32)
        for d, h_ref in enumerate((hf_ref, hb_ref)):
            rhs = jnp.concatenate([xb_, h_ref[...].astype(BF16)], axis=0)
            mask = (li >= si) if d == 0 else (li <= si)
            for r in range(HPG):
                j = 4 * d + r
                lm = jnp.where(mask, jnp.exp(la[:, j:j + 1] - lat_[j:j + 1, :]), 0.0)
                w = s_ * lm * dtt_[j:j + 1, :]
                lhs = jnp.concatenate([w, cm * elam[:, j:j + 1]], axis=1).astype(BF16)
                y = y + jnp.where((lane >> 6) == r, _dot(lhs, rhs), 0.0)
        y_ref[...] = y

    return pl.pallas_call(
        kern, name="ssd_out", grid=(ncx, NG),
        in_specs=[pl.BlockSpec((Q, 256), lambda c, g: (c + ncc, g)), pl.BlockSpec((Q, 128), lambda c, g: (c + ncc, 16 + g)),
                  pl.BlockSpec((Q, 128), lambda c, g: (c + ncc, 24 + g)),
                  pl.BlockSpec((None, Q, 128), lambda c, g: (g, c + ncc, 0)), pl.BlockSpec((None, Q, 128), lambda c, g: (g, c + ncc, 0)),
                  pl.BlockSpec((8, Q), lambda c, g: (g, c + ncc)), pl.BlockSpec((8, Q), lambda c, g: (g, c + ncc)),
                  pl.BlockSpec((None, 128, 256), lambda c, g: (c + ncc, 0, g)), pl.BlockSpec((None, 128, 256), lambda c, g: (c + ncc, 0, g))],
        out_specs=pl.BlockSpec((Q, 256), lambda c, g: (c, g)),
        out_shape=jax.ShapeDtypeStruct((ncx * Q, DI), F32),
        compiler_params=_cp(("parallel", "parallel")),
    )(xbc, xbc, xbc, dtg, lag, dtt, lat, htf, htb)


def _ssd_bwd_state(xbc, dy, lag, ncc):
    lext = xbc.shape[0]
    nc = lext // Q

    def kern(cf_ref, dyf_ref, laf_ref, cb_ref, dyb_ref, lab_ref, df_ref, db_ref, sf, sb):
        @pl.when(pl.program_id(0) == 0)
        def _():
            sf[...] = jnp.zeros_like(sf)
            sb[...] = jnp.zeros_like(sb)

        for d, (c_ref, dy_ref, la_ref, o_ref, s) in enumerate(
                ((cf_ref, dyf_ref, laf_ref, df_ref, sf), (cb_ref, dyb_ref, lab_ref, db_ref, sb))):
            o_ref[...] = s[...]
            ex = _expand_sel(d)
            for g in range(NG):
                cols = slice(256 * g, 256 * (g + 1))
                la = la_ref[g]
                tot = la[Q - 1:Q] if d == 0 else la[0:1]
                eexp = _dot2_r(jnp.exp(la), ex)
                dexp = _dot2_r(jnp.broadcast_to(jnp.exp(tot), (8, 128)), ex)[0:1]
                dye = (dy_ref[:, cols] * eexp).astype(BF16)
                s[:, cols] = s[:, cols] * dexp + _dot(c_ref[:, 128 * g:128 * (g + 1)].T.astype(BF16), dye)

    cf = lambda j: nc - 1 - j
    cb = lambda j: _chunk_of_bwd_dir(nc - 1 - j, ncc, nc)
    sm = lambda f: pl.BlockSpec((NG, Q, 128), lambda j: (0, f(j), 0))
    return pl.pallas_call(
        kern, name="ssd_bwd_state", grid=(nc,),
        in_specs=[pl.BlockSpec((Q, NG * NS), lambda j: (cf(j), 3)), pl.BlockSpec((Q, DI), lambda j: (cf(j), 0)), sm(cf),
                  pl.BlockSpec((Q, NG * NS), lambda j: (cb(j), 3)), pl.BlockSpec((Q, DI), lambda j: (cb(j), 0)), sm(cb)],
        out_specs=[pl.BlockSpec((None, 128, DI), lambda j: (cf(j), 0, 0)),
                   pl.BlockSpec((None, 128, DI), lambda j: (cb(j), 0, 0))],
        out_shape=[jax.ShapeDtypeStruct((nc, 128, DI), F32), jax.ShapeDtypeStruct((nc, 128, DI), F32)],
        scratch_shapes=[pltpu.VMEM((128, DI), F32), pltpu.VMEM((128, DI), F32)],
        compiler_params=_cp(("arbitrary",)),
    )(xbc, dy, lag, xbc, dy, lag)


def _ssd_bwd_out(xbc, dy, dxskip, dtg, lag, dtt, lat, htf, htb, dhf, dhb, a_rows):
    lext = xbc.shape[0]
    nc = lext // Q

    def kern(x_ref, b_ref, c_ref, dy_ref, sk_ref, dtg_ref, lag_ref, dtt_ref, lat_ref, hf_ref, hb_ref, df_ref, db_ref,
             a_ref, dx_ref, dbo_ref, dco_ref, ddt_ref, ga_ref):
        g = pl.program_id(0)

        @pl.when(pl.program_id(1) == 0)
        def _():
            ga_ref[...] = jnp.zeros_like(ga_ref)

        x, bm, cm, dy_ = x_ref[...], b_ref[...], c_ref[...], dy_ref[...]
        xb_, bb_, cb_, dyb_ = x.astype(BF16), bm.astype(BF16), cm.astype(BF16), dy_.astype(BF16)
        st = _dot_nt(bb_, cb_)
        si, li = _iota((Q, Q), 0), _iota((Q, Q), 1)
        lane = _iota((Q, 256), 1)
        lane128 = _iota((Q, 128), 1)
        row128 = _iota((Q, 128), 0)
        sub = _iota((128, Q), 0)
        la, dt, dtt_, lat_ = lag_ref[...], dtg_ref[...], dtt_ref[...], lat_ref[...]
        elam = jnp.exp(la)
        dst = jnp.zeros((Q, Q), F32)
        dxa = jnp.zeros((Q, 256), F32)
        dba = jnp.zeros((Q, 128), F32)
        dca = jnp.zeros((Q, 128), F32)
        dlam = jnp.zeros((Q, 128), F32)
        ddir = jnp.zeros((Q, 128), F32)
        rows = jnp.zeros((128, Q), F32)
        for d, (h_ref, dh_ref) in enumerate(((hf_ref, df_ref), (hb_ref, db_ref))):
            ht, dht = h_ref[...], dh_ref[...]
            htb_, dhtb_ = ht.astype(BF16), dht.astype(BF16)
            tot, w_end, wnd = _dir_terms(la, dt, d)
            ex, rs = _expand_sel(d), _reduce_sel(d)
            elx = _dot2_r(elam, ex)
            wex = _dot2_r(w_end, ex)
            dye = dy_ * elx
            ch = _dot(cb_, htb_)
            bd = _dot(bb_, dhtb_)
            dca = dca + _dot_nt(dye.astype(BF16), htb_)
            dba = dba + _dot_nt((x * wex).astype(BF16), dhtb_)
            dlam = dlam + _dot2_r(dye * ch, rs)
            xbd = _dot2_r(x * bd, rs)
            e_ = w_end * xbd
            dlam = dlam - e_
            ddir = ddir + wnd * xbd
            hh = _dot2_r(jnp.broadcast_to(jnp.sum(dht * ht, axis=0, keepdims=True), (8, 256)), rs)[0:1]
            tot_term = jnp.sum(e_, axis=0, keepdims=True) + jnp.exp(tot) * hh
            dlam = dlam + jnp.where(row128 == (Q - 1 if d == 0 else 0), tot_term, 0.0)
            rhs = jnp.concatenate([dyb_, dhtb_], axis=0)
            maskt = (li >= si) if d == 0 else (li <= si)
            for r in range(HPG):
                j = 4 * d + r
                dc = dt[:, j:j + 1]
                lmt = jnp.where(maskt, jnp.exp(lat_[j:j + 1, :] - la[:, j:j + 1]), 0.0)
                wt = st * lmt * dc
                lhs = jnp.concatenate([wt, bm * w_end[:, j:j + 1]], axis=1).astype(BF16)
                hm = (lane >> 6) == r
                dxa = dxa + jnp.where(hm, _dot(lhs, rhs), 0.0)
                dwt = _dot_nt(jnp.where(hm, x, 0.0).astype(BF16), dyb_)
                dl = dwt * lmt
                gpt = dl * st
                cs = jnp.sum(gpt, axis=1, keepdims=True)
                ddir = ddir + jnp.where(lane128 == j, cs, 0.0)
                dlam = dlam - jnp.where(lane128 == j, cs * dc, 0.0)
                rows = rows + jnp.where(sub == j, jnp.sum(gpt * dc, axis=0, keepdims=True), 0.0)
                dst = dst + dl * dc
        dlam = dlam + rows.T
        dba = dba + _dot(dst.astype(BF16), cb_)
        dca = dca + _dot(dst.T.astype(BF16), bb_)
        isb = jnp.logical_and(lane128 >= 4, lane128 < 8)
        ddel = jnp.where(isb, _dot2_l(_tri(True), dlam), _dot2_l(_tri(False), dlam))
        a_l = a_ref[pl.ds(g, 1), :]
        ddt_ref[...] = ddir + a_l * ddel
        ga_ref[...] += jnp.broadcast_to(a_l * jnp.sum(dt * ddel, axis=0, keepdims=True), (8, 128))
        dx_ref[...] = dxa + sk_ref[...]
        dbo_ref[...] = dba
        dco_ref[...] = dca

    st3 = pl.BlockSpec((None, 128, 256), lambda g, c: (c, 0, g))
    sm = pl.BlockSpec((None, Q, 128), lambda g, c: (g, c, 0))
    smt = pl.BlockSpec((8, Q), lambda g, c: (g, c))
    return pl.pallas_call(
        kern, name="ssd_bwd_out", grid=(NG, nc),
        in_specs=[pl.BlockSpec((Q, 256), lambda g, c: (c, g)), pl.BlockSpec((Q, 128), lambda g, c: (c, 16 + g)),
                  pl.BlockSpec((Q, 128), lambda g, c: (c, 24 + g)), pl.BlockSpec((Q, 256), lambda g, c: (c, g)),
                  pl.BlockSpec((Q, 256), lambda g, c: (c, g)), sm, sm, smt, smt, st3, st3, st3, st3,
                  pl.BlockSpec((8, 128), lambda g, c: (0, 0))],
        out_specs=[pl.BlockSpec((Q, 256), lambda g, c: (c, g)), pl.BlockSpec((Q, 128), lambda g, c: (c, g)),
                   pl.BlockSpec((Q, 128), lambda g, c: (c, g)), sm, pl.BlockSpec((None, 8, 128), lambda g, c: (g, 0, 0))],
        out_shape=[jax.ShapeDtypeStruct((lext, DI), F32), jax.ShapeDtypeStruct((lext, NG * NS), F32),
                   jax.ShapeDtypeStruct((lext, NG * NS), F32), jax.ShapeDtypeStruct((NG, lext, 128), F32),
                   jax.ShapeDtypeStruct((NG, 8, 128), F32)],
        compiler_params=_cp(("parallel", "arbitrary")),
    )(xbc, xbc, xbc, dy, dxskip, dtg, lag, dtt, lat, htf, htb, dhf, dhb, a_rows)


def _post_fwd(yssm, xbc, proj_rest, dsk, gnw, nct):
    l = yssm.shape[0]

    def kern(y_ref, x_ref, z_ref, dsk_ref, w_ref, o_ref):
        y = y_ref[...] + dsk_ref[...] * x_ref[...]
        yz = y * _silu(z_ref[...])
        for g in range(NG):
            sl = slice(256 * g, 256 * (g + 1))
            v = yz[:, sl]
            r = lax.rsqrt(jnp.mean(v * v, axis=-1, keepdims=True) + EPS)
            o_ref[:, sl] = (v * r * w_ref[:, sl]).astype(BF16)

    return pl.pallas_call(
        kern, name="post_fwd", grid=(l // T,),
        in_specs=[pl.BlockSpec((T, DI), lambda i: (i, 0)), pl.BlockSpec((T, DI), lambda i: (i + nct, 0)),
                  pl.BlockSpec((T, DI), lambda i: (i, 0)), pl.BlockSpec((1, DI), lambda i: (0, 0)),
                  pl.BlockSpec((1, DI), lambda i: (0, 0))],
        out_specs=pl.BlockSpec((T, DI), lambda i: (i, 0)),
        out_shape=jax.ShapeDtypeStruct((l, DI), BF16),
        compiler_params=_cp(("parallel",)),
    )(yssm, xbc, proj_rest, dsk, gnw)


def _post_bwd(dgn, yssm, xbc, proj_rest, dsk, gnw, dpr, nct):
    l = yssm.shape[0]
    lext = xbc.shape[0]
    xi = lambda i: (jnp.maximum(i - nct, 0), 0)

    def kern(dg_ref, y_ref, x_ref, z_ref, dsk_ref, w_ref, dpr_ref, dy_ref, sk_ref, dz_ref, gw_ref, gd_ref):
        i = pl.program_id(0)

        @pl.when(i == 0)
        def _():
            gw_ref[...] = jnp.zeros_like(gw_ref)
            gd_ref[...] = jnp.zeros_like(gd_ref)

        @pl.when(i < nct)
        def _():
            dy_ref[...] = jnp.zeros_like(dy_ref)
            sk_ref[...] = jnp.zeros_like(sk_ref)

        @pl.when(i >= nct)
        def _():
            xs = x_ref[...]
            z = z_ref[...]
            y = y_ref[...] + dsk_ref[...] * xs
            sz = _silu(z)
            yz = y * sz
            dgn_ = dg_ref[...]
            dyz_parts = []
            gws = []
            for g in range(NG):
                sl = slice(256 * g, 256 * (g + 1))
                v = yz[:, sl]
                r = lax.rsqrt(jnp.mean(v * v, axis=-1, keepdims=True) + EPS)
                vn = v * r
                dn = dgn_[:, sl] * w_ref[:, sl]
                gws.append(jnp.sum(dgn_[:, sl] * vn, axis=0, keepdims=True))
                dyz_parts.append(r * (dn - vn * jnp.mean(dn * vn, axis=-1, keepdims=True)))
            dyz = jnp.concatenate(dyz_parts, axis=1)
            gw_ref[...] += jnp.broadcast_to(jnp.concatenate(gws, axis=1), (8, DI))
            dy = dyz * sz
            dz_ref[...] = (dyz * y * _dsilu(z)).astype(BF16)
            gd_ref[...] += jnp.broadcast_to(jnp.sum(dy * xs, axis=0, keepdims=True), (8, DI))
            dy_ref[...] = dy
            sk_ref[...] = dy * dsk_ref[...]

    return pl.pallas_call(
        kern, name="post_bwd", grid=(lext // T,),
        in_specs=[pl.BlockSpec((T, DI), xi), pl.BlockSpec((T, DI), xi), pl.BlockSpec((T, DI), lambda i: (i, 0)),
                  pl.BlockSpec((T, DI), xi), pl.BlockSpec((1, DI), lambda i: (0, 0)), pl.BlockSpec((1, DI), lambda i: (0, 0)),
                  pl.BlockSpec(memory_space=pl.ANY)],
        out_specs=[pl.BlockSpec((T, DI), lambda i: (i, 0)), pl.BlockSpec((T, DI), lambda i: (i, 0)),
                   pl.BlockSpec((T, DI), xi), pl.BlockSpec((8, DI), lambda i: (0, 0)), pl.BlockSpec((8, DI), lambda i: (0, 0))],
        out_shape=[jax.ShapeDtypeStruct((lext, DI), F32), jax.ShapeDtypeStruct((lext, DI), F32),
                   jax.ShapeDtypeStruct((l, RESTW), BF16), jax.ShapeDtypeStruct((8, DI), F32), jax.ShapeDtypeStruct((8, DI), F32)],
        input_output_aliases={6: 2},
        compiler_params=_cp(("arbitrary",)),
    )(dgn, yssm, xbc, proj_rest, dsk, gnw, dpr)


C_G1, C_G2, C_GA, C_GB, C_CG = 2, 3, 4, 5, 6
PITCH = GW + 16
NROW = T // GW


def _pad_rows(a):
    z = jnp.zeros((PITCH - GW, D), F32)
    parts = []
    for r in range(NROW):
        parts += [a[GW * r:GW * (r + 1)], z]
    return jnp.concatenate(parts, axis=0)


def _unpad_rows(p):
    return jnp.concatenate([p[PITCH * r:PITCH * r + GW] for r in range(NROW)], axis=0)


def _row_conv(p, w, transpose):
    n = p.shape[0]
    acc = w[15:16] * p
    for k in range(CK):
        off = (k - 15) if not transpose else (15 - k)
        if off != 0:
            acc = acc + w[k:k + 1] * pltpu.roll(p, (-off) % n, axis=0)
    return acc


def _ln_stats(cv):
    mu = jnp.mean(cv, axis=-1, keepdims=True)
    xc = cv - mu
    rs = lax.rsqrt(jnp.mean(xc * xc, axis=-1, keepdims=True) + EPS)
    return xc * rs, rs


def _conf_fwd(proj_rest, cw, cb, lw, lb):
    l = proj_rest.shape[0]

    def kern(ga_ref, gb_ref, cg_ref, cw_ref, cb_ref, lw_ref, lb_ref, o_ref, cv_ref):
        a = ga_ref[...] * _sig(gb_ref[...])
        cv = _unpad_rows(_row_conv(_pad_rows(a), cw_ref[...], False)) + cb_ref[...]
        cv_ref[...] = cv
        xh, _ = _ln_stats(cv)
        ln = xh * lw_ref[...] + lb_ref[...]
        o_ref[...] = (_silu(ln) * _silu(cg_ref[...])).astype(BF16)

    vec = pl.BlockSpec((1, D), lambda i: (0, 0))
    blk = pl.BlockSpec((T, D), lambda i: (i, 0))
    return pl.pallas_call(
        kern, name="conf_fwd", grid=(l // T,),
        in_specs=[pl.BlockSpec((T, D), lambda i: (i, C_GA)), pl.BlockSpec((T, D), lambda i: (i, C_GB)),
                  pl.BlockSpec((T, D), lambda i: (i, C_CG)), pl.BlockSpec((32, D), lambda i: (0, 0)), vec, vec, vec],
        out_specs=[blk, blk],
        out_shape=[jax.ShapeDtypeStruct((l, D), BF16), jax.ShapeDtypeStruct((l, D), F32)],
        compiler_params=_cp(("parallel",)),
    )(proj_rest, proj_rest, proj_rest, cw, cb, lw, lb)


def _conf_bwd(duc, cv, proj_rest, cw, lw, lb, dpr):
    l = proj_rest.shape[0]

    def kern(du_ref, cv_ref, ga_ref, gb_ref, cg_ref, cw_ref, lw_ref, lb_ref, dpr_ref, o_ref, gcw_ref, gv_ref, sc):
        i, j = pl.program_id(0), pl.program_id(1)

        @pl.when(jnp.logical_and(i == 0, j == 0))
        def _():
            gcw_ref[...] = jnp.zeros_like(gcw_ref)
            gv_ref[...] = jnp.zeros_like(gv_ref)

        @pl.when(j == 0)
        def _():
            ga, gb, cg, cw = ga_ref[...], gb_ref[...], cg_ref[...], cw_ref[...]
            sg = _sig(gb)
            xh, rs = _ln_stats(cv_ref[...])
            ln = xh * lw_ref[...] + lb_ref[...]
            du = du_ref[...]
            sc[:, 2 * D:3 * D] = (du * _silu(ln) * _dsilu(cg)).astype(BF16)
            dln = du * _silu(cg) * _dsilu(ln)
            g_lw = jnp.sum(dln * xh, axis=0, keepdims=True)
            g_lb = jnp.sum(dln, axis=0, keepdims=True)
            dxh = dln * lw_ref[...]
            dcv = rs * (dxh - jnp.mean(dxh, axis=-1, keepdims=True) - xh * jnp.mean(dxh * xh, axis=-1, keepdims=True))
            g_cb = jnp.sum(dcv, axis=0, keepdims=True)
            rid = _iota((8, D), 0)
            gv_ref[...] += jnp.where(rid == 0, g_cb, jnp.where(rid == 1, g_lw, jnp.where(rid == 2, g_lb, 0.0)))
            dcvp = _pad_rows(dcv)
            da = _unpad_rows(_row_conv(dcvp, cw, True))
            sc[:, 0:D] = (da * sg).astype(BF16)
            sc[:, D:2 * D] = (da * ga * sg * (1.0 - sg)).astype(BF16)
            ap = _pad_rows(ga * sg)
            n = ap.shape[0]
            for k in range(CK):
                sh = ap if k == 15 else pltpu.roll(ap, (15 - k) % n, axis=0)
                gcw_ref[k:k + 1, :] += jnp.sum(dcvp * sh, axis=0, keepdims=True)

        o_ref[...] = sc[:, pl.ds(pl.multiple_of(j * D, 128), D)]

    vec = pl.BlockSpec((1, D), lambda i, j: (0, 0))
    col = lambda c: pl.BlockSpec((T, D), lambda i, j: (i, c))
    return pl.pallas_call(
        kern, name="conf_bwd", grid=(l // T, 3),
        in_specs=[col(0), col(0), col(C_GA), col(C_GB), col(C_CG), pl.BlockSpec((32, D), lambda i, j: (0, 0)), vec, vec,
                  pl.BlockSpec(memory_space=pl.ANY)],
        out_specs=[pl.BlockSpec((T, D), lambda i, j: (i, C_GA + j)), pl.BlockSpec((32, D), lambda i, j: (0, 0)),
                   pl.BlockSpec((8, D), lambda i, j: (0, 0))],
        out_shape=[jax.ShapeDtypeStruct((l, RESTW), BF16), jax.ShapeDtypeStruct((32, D), F32),
                   jax.ShapeDtypeStruct((8, D), F32)],
        scratch_shapes=[pltpu.VMEM((T, 3 * D), BF16)],
        input_output_aliases={8: 0},
        compiler_params=_cp(("arbitrary", "arbitrary")),
    )(duc, cv, proj_rest, proj_rest, proj_rest, cw, lw, lb, dpr)


def _merge_fwd(bs, bc, proj_rest):
    l = bs.shape[0]

    def kern(bs_ref, bc_ref, g1_ref, g2_ref, o_ref):
        o_ref[...] = (_sig(g1_ref[...]) * bs_ref[...] + _sig(g2_ref[...]) * bc_ref[...]).astype(BF16)

    blk = pl.BlockSpec((T, D), lambda i: (i, 0))
    return pl.pallas_call(
        kern, name="merge_fwd", grid=(l // T,),
        in_specs=[blk, blk, pl.BlockSpec((T, D), lambda i: (i, C_G1)), pl.BlockSpec((T, D), lambda i: (i, C_G2))],
        out_specs=blk, out_shape=jax.ShapeDtypeStruct((l, D), BF16),
        compiler_params=_cp(("parallel",)),
    )(bs, bc, proj_rest, proj_rest)


def _merge_bwd(dm, bs, bc, proj_rest):
    l = bs.shape[0]

    def kern(dm_ref, bs_ref, bc_ref, g1_ref, g2_ref, dbs_ref, dbc_ref, dg_ref):
        dm_ = dm_ref[...]
        s1, s2 = _sig(g1_ref[...]), _sig(g2_ref[...])
        dbs_ref[...] = (dm_ * s1).astype(BF16)
        dbc_ref[...] = (dm_ * s2).astype(BF16)
        dg_ref[:, 0:D] = (dm_ * bs_ref[...] * s1 * (1.0 - s1)).astype(BF16)
        dg_ref[:, D:2 * D] = (dm_ * bc_ref[...] * s2 * (1.0 - s2)).astype(BF16)

    blk = pl.BlockSpec((T, D), lambda i: (i, 0))
    return pl.pallas_call(
        kern, name="merge_bwd", grid=(l // T,),
        in_specs=[blk, blk, blk, pl.BlockSpec((T, D), lambda i: (i, C_G1)), pl.BlockSpec((T, D), lambda i: (i, C_G2))],
        out_specs=[blk, blk, pl.BlockSpec((T, 2 * D), lambda i: (i, 1))],
        out_shape=[jax.ShapeDtypeStruct((l, D), BF16), jax.ShapeDtypeStruct((l, D), BF16),
                   jax.ShapeDtypeStruct((l, RESTW), BF16)],
        compiler_params=_cp(("parallel",)),
    )(dm, bs, bc, proj_rest, proj_rest)


def _final(x, out, tgt, mod, fw):
    l = x.shape[0]

    def kern(x_ref, o_ref, t_ref, mod_ref, fw_ref, ls_ref, dx2_ref, do_ref, gv_ref):
        @pl.when(pl.program_id(0) == 0)
        def _():
            ls_ref[...] = jnp.zeros_like(ls_ref)
            gv_ref[...] = jnp.zeros_like(gv_ref)

        gate = mod_ref[0:1, 2 * D:3 * D]
        o = o_ref[...]
        x2 = x_ref[...] + gate * o
        r = lax.rsqrt(jnp.mean(x2 * x2, axis=-1, keepdims=True) + EPS)
        yn = x2 * r
        fw_ = fw_ref[...]
        e = yn * fw_ - t_ref[...]
        ls_ref[...] += jnp.full((8, 128), 1.0, F32) * (0.5 / D) * jnp.sum(e * e)
        dy = e * (1.0 / D)
        g_fw = jnp.sum(dy * yn, axis=0, keepdims=True)
        dyn = dy * fw_
        dx2 = r * (dyn - yn * jnp.mean(dyn * yn, axis=-1, keepdims=True))
        g_gate = jnp.sum(dx2 * o, axis=0, keepdims=True)
        rid = _iota((8, D), 0)
        gv_ref[...] += jnp.where(rid == 0, g_fw, jnp.where(rid == 1, g_gate, 0.0))
        dx2_ref[...] = dx2
        do_ref[...] = (dx2 * gate).astype(BF16)

    blk = pl.BlockSpec((T, D), lambda i: (i, 0))
    return pl.pallas_call(
        kern, name="final", grid=(l // T,),
        in_specs=[blk, blk, blk, pl.BlockSpec((8, 3 * D), lambda i: (0, 0)), pl.BlockSpec((1, D), lambda i: (0, 0))],
        out_specs=[pl.BlockSpec((8, 128), lambda i: (0, 0)), blk, blk, pl.BlockSpec((8, D), lambda i: (0, 0))],
        out_shape=[jax.ShapeDtypeStruct((8, 128), F32), jax.ShapeDtypeStruct((l, D), F32),
                   jax.ShapeDtypeStruct((l, D), BF16), jax.ShapeDtypeStruct((8, D), F32)],
        compiler_params=_cp(("arbitrary",)),
    )(x, out, tgt, mod, fw)


def _perm_dt_cols(w):
    s = w.shape[:-1]
    return w.reshape(*s, 2, NG, HPG).swapaxes(-3, -2).reshape(*s, 64)


def _unperm_dt_cols(w):
    s = w.shape[:-1]
    return w.reshape(*s, NG, 2, HPG).swapaxes(-3, -2).reshape(*s, 64)


def _pad_lanes(v, width):
    return jnp.pad(v, ((0, 0), (0, width - v.shape[1])))


def _local_step(x, c, ctx, tgt, w):
    l = x.shape[0]
    nct = CTX // T
    ncc = CTX // Q
    lext = l + CTX

    w_mod = w["w_mod"].astype(BF16)
    w_in = w["w_in"].astype(BF16)
    w_ssd = jnp.concatenate([w_in[:, :XBC], _perm_dt_cols(w_in[:, XBC:XBC + 64]), jnp.zeros((D, 64), BF16)], axis=1)
    wr = w_in[:, XBC + 64:]
    w_rest = jnp.concatenate([wr[:, :DI], wr[:, DI + 3 * D:], wr[:, DI:DI + 3 * D]], axis=1)
    w_os, w_oc, w_o = w["w_out_ssm"].astype(BF16), w["w_out_conf"].astype(BF16), w["w_out"].astype(BF16)
    cw8 = jnp.pad(w["ssm_conv_w"], ((0, 4), (0, 0)))
    cb_s = w["ssm_conv_b"].reshape(1, XBC)
    dtb = _pad_lanes(_perm_dt_cols(w["dt_bias"].reshape(1, 64)), 128)
    a_all = -jnp.exp(w["a_log"].reshape(1, 64))
    a_perm = _pad_lanes(_perm_dt_cols(a_all), 128)
    a_rows = _pad_lanes(_perm_dt_cols(a_all).reshape(NG, 8), 128)
    dsk = jnp.repeat(w["d_skip"].reshape(NH), HP).reshape(1, DI)
    gnw = w["ssm_norm_w"].reshape(1, DI)
    ccw = jnp.pad(w["conf_conv_w"], ((0, 1), (0, 0)))
    ccb, clw, clb = w["conf_conv_b"].reshape(1, D), w["conf_ln_w"].reshape(1, D), w["conf_ln_b"].reshape(1, D)
    nw = w["norm_w"].reshape(1, D)
    fw = w["final_norm_w"].reshape(1, D)
    cc = jnp.concatenate([c.reshape(1, D), w["c_ctx"].reshape(1, D), jnp.zeros((6, D), F32)], axis=0)

    bx = 512
    be = 768 if lext % 768 == 0 else 256
    tk = min(1024, l)
    mod = _mod_fwd(cc, w_mod, w["b_mod"].reshape(1, 3 * D))
    h = _norm_fwd(ctx, x, mod, nw, nct)
    hx = h[CTX:]
    proj_ssd = _mm(h, w_ssd, "nn", lext, SSDW, D, be, SSDW // 3, D, F32, "proj_ssd")
    proj_rest = _mm(hx, w_rest, "nn", l, RESTW, D, bx, 1024, D, F32, "proj_rest")
    xbc = _conv_fwd(proj_ssd, cw8, cb_s, nct)
    dtg, lag, dtt, lat = _dt_fwd(proj_ssd, dtb, a_perm)
    htf, htb = _ssd_state(xbc, dtg, lag, ncc)
    yssm = _ssd_out(xbc, dtg, lag, dtt, lat, htf, htb, ncc)
    gn = _post_fwd(yssm, xbc, proj_rest, dsk, gnw, nct)
    bs = _mm(gn, w_os, "nn", l, D, DI, bx, D, DI, F32, "out_ssm")
    uc, cv = _conf_fwd(proj_rest, ccw, ccb, clw, clb)
    bc = _mm(uc, w_oc, "nn", l, D, D, bx, D, D, F32, "out_conf")
    merged = _merge_fwd(bs, bc, proj_rest)
    out = _mm(merged, w_o, "nn", l, D, D, bx, D, D, F32, "out_proj")
    lsum, dx2, dout, gv_fin = _final(x, out, tgt, mod, fw)

    g = {}
    g["final_norm_w"] = gv_fin[0]
    dmerged = _mm(dout, w_o, "nt", l, D, D, bx, D, D, F32, "d_merged")
    g["w_out"] = _mm(merged, dout, "tn", D, D, l, D, D, tk, F32, "g_w_out")
    dbs, dbc, dpr = _merge_bwd(dmerged, bs, bc, proj_rest)
    dgn = _mm(dbs, w_os, "nt", l, DI, D, bx, DI, D, F32, "d_gn")
    g["w_out_ssm"] = _mm(gn, dbs, "tn", DI, D, l, DI, D, tk, F32, "g_w_out_ssm")
    duc = _mm(dbc, w_oc, "nt", l, D, D, bx, D, D, F32, "d_uc")
    g["w_out_conf"] = _mm(uc, dbc, "tn", D, D, l, D, D, tk, F32, "g_w_out_conf")
    dpr, gcw, gv_conf = _conf_bwd(duc, cv, proj_rest, ccw, clw, clb, dpr)
    g["conf_conv_w"] = gcw[:CK]
    g["conf_conv_b"], g["conf_ln_w"], g["conf_ln_b"] = gv_conf[0], gv_conf[1], gv_conf[2]
    dy, dxskip, dproj_rest, ggnw, gdsk = _post_bwd(dgn, yssm, xbc, proj_rest, dsk, gnw, dpr, nct)
    g["ssm_norm_w"] = ggnw[0]
    g["d_skip"] = gdsk[0].reshape(NH, HP).sum(axis=1)
    dhf, dhb = _ssd_bwd_state(xbc, dy, lag, ncc)
    dxs, dbm, dcm, ddtg, galog = _ssd_bwd_out(xbc, dy, dxskip, dtg, lag, dtt, lat, htf, htb, dhf, dhb, a_rows)
    g["a_log"] = _unperm_dt_cols(galog[:, 0, 0:8].reshape(1, 64)).reshape(2, NH)
    dus, gws, gbs = [], [], []
    for dpost, off, width, nm in ((dxs, 0, DI, "conv_bwd_x"), (dbm, DI, NG * NS, "conv_bwd_b"), (dcm, DI + NG * NS, NG * NS, "conv_bwd_c")):
        du_, gw_, gb_ = _conv_bwd(dpost, proj_ssd, cw8, cb_s, off, width, nct, nm)
        dus.append(du_)
        gws.append(gw_[:SK])
        gbs.append(gb_[0])
    g["ssm_conv_w"] = jnp.concatenate(gws, axis=1)
    g["ssm_conv_b"] = jnp.concatenate(gbs, axis=0)
    ddt_raw, gdtb = _dt_bwd(ddtg, proj_ssd, dtb)
    g["dt_bias"] = _unperm_dt_cols(gdtb[0:1, 0:64]).reshape(2, NH)
    dproj_ssd = jnp.concatenate(dus + [ddt_raw], axis=1)
    gw_ssd = _mm(h, dproj_ssd, "tn", D, SSDW, lext, D, SSDW // 3, be, F32, "g_w_ssd")
    gw_rest = _mm(hx, dproj_rest, "tn", D, RESTW, l, D, 1024, tk, F32, "g_w_rest")
    g["w_in"] = jnp.concatenate([gw_ssd[:, :XBC], _unperm_dt_cols(gw_ssd[:, XBC:XBC + 64]), gw_rest[:, :DI],
                                 gw_rest[:, 2 * DI:], gw_rest[:, DI:2 * DI]], axis=1)
    dh_a = _mm(dproj_ssd, w_ssd, "nt", lext, D, SSDW, T, D, SSDW, F32, "dh_ssd")
    dh_b = _mm(dproj_rest, w_rest, "nt", l, D, RESTW, T, D, RESTW, F32, "dh_rest")
    grad_x, gnw_in, dss = _norm_bwd(dh_a, dh_b, ctx, x, dx2, mod, nw, nct)
    g["norm_w"] = gnw_in[0]
    dmod = jnp.concatenate([jnp.concatenate([dss[0:1], gv_fin[1:2]], axis=1),
                            jnp.concatenate([dss[1:2], jnp.zeros((1, D), F32)], axis=1),
                            jnp.zeros((6, 3 * D), F32)], axis=0)
    gwm, gbm, gcc = _mod_bwd(dmod, cc, cc.T, w_mod)
    g["w_mod"], g["b_mod"], g["c_ctx"] = gwm, gbm[0], gcc[1]
    return lsum[0, 0], grad_x, g


NSHARD = 4
R_MOD, R_IN, R_OS, R_OC, R_O, R_SC, R_CC = 768, 2832, 512, 256, 256, 8, 8
O_MOD = 0
O_IN = O_MOD + R_MOD
O_OS = O_IN + R_IN
O_OC = O_OS + R_OS
O_O = O_OC + R_OC
O_SC = O_O + R_O
O_CC = O_SC + R_SC
PROWS = 4640
HALF = PROWS // 2
RB = 464
SROWS = 16
SHARDED = ("w_mod", "w_in", "w_out_ssm", "w_out_conf", "w_out", "ssm_conv_w", "conf_conv_w")
SMALL = (("b_mod", 3 * D), ("norm_w", D), ("ssm_conv_b", XBC), ("dt_bias", 64), ("a_log", 64), ("d_skip", NH),
         ("ssm_norm_w", DI), ("conf_conv_b", D), ("conf_ln_w", D), ("conf_ln_b", D), ("final_norm_w", D), ("c_ctx", D))
SMALL_OFF = {"b_mod": 0, "norm_w": 3 * D, "ssm_conv_b": 4 * D, "dt_bias": 8 * D, "a_log": 8 * D + 64, "d_skip": 8 * D + 128,
             "ssm_norm_w": 9 * D, "conf_conv_b": 11 * D, "conf_ln_w": 12 * D, "conf_ln_b": 13 * D, "final_norm_w": 14 * D,
             "c_ctx": 15 * D}


def _pack_shard(s):
    cc = jnp.pad(s["conf_conv_w"].reshape(1, CK * 256), ((0, 0), (0, R_CC * D - CK * 256))).reshape(R_CC, D)
    assert O_CC + R_CC == PROWS
    return jnp.concatenate([s["w_mod"].reshape(R_MOD, D), s["w_in"].reshape(R_IN, D), s["w_out_ssm"], s["w_out_conf"],
                            s["w_out"], jnp.pad(s["ssm_conv_w"], ((0, R_SC - SK), (0, 0))), cc], axis=0)


def _unpack_shard(p):
    return {"w_mod": p[O_MOD:O_IN].reshape(1, D, R_MOD), "w_in": p[O_IN:O_OS].reshape(1, D, R_IN),
            "w_out_ssm": p[O_OS:O_OC][None], "w_out_conf": p[O_OC:O_O][None], "w_out": p[O_O:O_SC][None],
            "ssm_conv_w": p[O_SC:O_SC + SK][None],
            "conf_conv_w": p[O_CC:O_CC + R_CC].reshape(R_CC * D)[:CK * 256].reshape(1, CK, 256)}


def _pack_full(g):
    def cols(a, n):
        return a.reshape(a.shape[0], NSHARD, n).transpose(1, 0, 2)
    cc = jnp.pad(cols(g["conf_conv_w"], 256).reshape(NSHARD, CK * 256), ((0, 0), (0, R_CC * D - CK * 256)))
    return jnp.concatenate([cols(g["w_mod"], R_MOD).reshape(NSHARD, R_MOD, D), cols(g["w_in"], R_IN).reshape(NSHARD, R_IN, D),
                            g["w_out_ssm"].reshape(NSHARD, R_OS, D), g["w_out_conf"].reshape(NSHARD, R_OC, D),
                            g["w_out"].reshape(NSHARD, R_O, D),
                            jnp.pad(cols(g["ssm_conv_w"], D), ((0, 0), (0, R_SC - SK), (0, 0))),
                            cc.reshape(NSHARD, R_CC, D)], axis=1)


def _unpack_gathered(gm, gs):
    def cols(a, r, n):
        return a.reshape(NSHARD, r, n).transpose(1, 0, 2).reshape(r, NSHARD * n)
    return {"w_mod": cols(gm[:, O_MOD:O_IN], D, R_MOD), "w_in": cols(gm[:, O_IN:O_OS], D, R_IN),
            "w_out_ssm": gm[:, O_OS:O_OC].reshape(DI, D), "w_out_conf": gm[:, O_OC:O_O].reshape(D, D),
            "w_out": gm[:, O_O:O_SC].reshape(D, D), "ssm_conv_w": cols(gs[:, 0:SK], SK, D),
            "conf_conv_w": cols(gs[:, R_SC:R_SC + R_CC].reshape(NSHARD, R_CC * D)[:, :CK * 256], CK, 256)}


def _pack_small(d):
    flat = jnp.zeros((SROWS * D,), F32)
    for name, n in SMALL:
        flat = lax.dynamic_update_slice(flat, d[name].reshape(n).astype(F32), (SMALL_OFF[name],))
    return flat.reshape(SROWS, D)


def _unpack_small(p, shapes):
    flat = p.reshape(SROWS * D)
    return {name: flat[SMALL_OFF[name]:SMALL_OFF[name] + n].reshape(shapes[name]) for name, n in SMALL}


MESH_ID = pl.DeviceIdType.MESH
ANY = pl.BlockSpec(memory_space=pl.ANY)


def _place():
    x, y, c = lax.axis_index("x"), lax.axis_index("y"), lax.axis_index("c")
    return x, y, c, [(1 - x, y), (x, 1 - y), (1 - x, 1 - y)]


def _rcopy(src, dst, send, recv, dev):
    return pltpu.make_async_remote_copy(src_ref=src, dst_ref=dst, send_sem=send, recv_sem=recv,
                                        device_id=dev, device_id_type=MESH_ID)


def _gather_weights(mats, small):
    def kern(m_ref, s_ref, gm_ref, gs_ref, send, recv, lsem):
        x, y, c, chips = _place()
        me = 2 * x + y
        sib = (x, y, 1 - c)
        mine = pl.ds(pl.multiple_of(c * HALF, 16), HALF)
        other = pl.ds(pl.multiple_of((1 - c) * HALF, 16), HALF)
        own_m = pltpu.make_async_copy(m_ref, gm_ref.at[me], lsem.at[0])
        own_s = pltpu.make_async_copy(s_ref, gs_ref.at[me], lsem.at[1])
        own_m.start()
        own_s.start()
        first = []
        for k, (px, py) in enumerate(chips):
            first.append(_rcopy(m_ref.at[mine], gm_ref.at[me, mine], send.at[k], recv.at[k], (px, py, c)))
            first.append(_rcopy(s_ref, gs_ref.at[me], send.at[3 + k], recv.at[3 + k], (px, py, c)))
        for cp in first:
            cp.start()
        passed = []
        for k, (px, py) in enumerate(chips):
            s = 2 * px + py
            _rcopy(m_ref.at[mine], gm_ref.at[s, mine], send.at[k], recv.at[k], sib).wait_recv()
            f = _rcopy(gm_ref.at[s, mine], gm_ref.at[s, mine], send.at[6 + k], recv.at[6 + k], sib)
            f.start()
            passed.append(f)
        for k, (px, py) in enumerate(chips):
            s = 2 * px + py
            _rcopy(s_ref, gs_ref.at[s], send.at[3 + k], recv.at[3 + k], sib).wait_recv()
            _rcopy(gm_ref.at[s, other], gm_ref.at[s, other], send.at[6 + k], recv.at[6 + k], sib).wait_recv()
        for cp in first + passed:
            cp.wait_send()
        own_m.wait()
        own_s.wait()

    return pl.pallas_call(
        kern, name="gather_weights", in_specs=[ANY, ANY], out_specs=[ANY, ANY],
        out_shape=[jax.ShapeDtypeStruct((NSHARD, PROWS, D), BF16), jax.ShapeDtypeStruct((NSHARD, SROWS, D), F32)],
        scratch_shapes=[pltpu.SemaphoreType.DMA((9,)), pltpu.SemaphoreType.DMA((9,)), pltpu.SemaphoreType.DMA((2,))],
    )(mats, small)


def _swap_halves(g):
    def kern(g_ref, o_ref, send, recv):
        x, y, c, _ = _place()
        other = pl.ds(pl.multiple_of((1 - c) * HALF, 8), HALF)
        cps = [_rcopy(g_ref.at[s, other], o_ref.at[s], send.at[s], recv.at[s], (x, y, 1 - c)) for s in range(NSHARD)]
        for cp in cps:
            cp.start()
        for cp in cps:
            cp.wait()

    return pl.pallas_call(
        kern, name="swap_halves", in_specs=[ANY], out_specs=ANY,
        out_shape=jax.ShapeDtypeStruct((NSHARD, HALF, D), F32),
        scratch_shapes=[pltpu.SemaphoreType.DMA((NSHARD,)), pltpu.SemaphoreType.DMA((NSHARD,))],
    )(g)


def _add_halves(cidx, g, ra):
    nb = HALF // RB

    def kern(c_ref, g_ref, a_ref, o_ref):
        o_ref[...] = (g_ref[...] + a_ref[...]).astype(BF16)

    return pl.pallas_call(
        kern, name="add_halves",
        grid_spec=pltpu.PrefetchScalarGridSpec(
            num_scalar_prefetch=1, grid=(NSHARD, nb),
            in_specs=[pl.BlockSpec((None, RB, D), lambda s, i, c: (s, c[0] * nb + i, 0)),
                      pl.BlockSpec((None, RB, D), lambda s, i, c: (s, i, 0))],
            out_specs=pl.BlockSpec((None, RB, D), lambda s, i, c: (s, i, 0))),
        out_shape=jax.ShapeDtypeStruct((NSHARD, HALF, D), BF16),
        compiler_params=_cp(("parallel", "parallel")),
    )(cidx, g, ra)


def _exchange_chips(p):
    def kern(p_ref, o_ref, send, recv):
        x, y, c, chips = _place()
        cps = [_rcopy(p_ref.at[2 * px + py], o_ref.at[k], send.at[k], recv.at[k], (px, py, c))
               for k, (px, py) in enumerate(chips)]
        for cp in cps:
            cp.start()
        for cp in cps:
            cp.wait()

    return pl.pallas_call(
        kern, name="exchange_chips", in_specs=[ANY], out_specs=ANY,
        out_shape=jax.ShapeDtypeStruct((3, HALF, D), p.dtype),
        scratch_shapes=[pltpu.SemaphoreType.DMA((3,)), pltpu.SemaphoreType.DMA((3,))],
    )(p)


def _add_chips(mc, g, ra, rb):
    nb = HALF // RB

    def kern(m_ref, g_ref, a_ref, r0_ref, r1_ref, r2_ref, o_ref):
        own = g_ref[...] + a_ref[...]
        o_ref[...] = ((own + r0_ref[...].astype(F32)) + r1_ref[...].astype(F32)) + r2_ref[...].astype(F32)

    return pl.pallas_call(
        kern, name="add_chips",
        grid_spec=pltpu.PrefetchScalarGridSpec(
            num_scalar_prefetch=1, grid=(nb,),
            in_specs=[pl.BlockSpec((None, RB, D), lambda i, m: (m[0], m[1] * nb + i, 0)),
                      pl.BlockSpec((None, RB, D), lambda i, m: (m[0], i, 0))]
            + [pl.BlockSpec((None, RB, D), functools.partial(lambda i, m, k: (k, i, 0), k=k)) for k in range(3)],
            out_specs=pl.BlockSpec((RB, D), lambda i, m: (i, 0))),
        out_shape=jax.ShapeDtypeStruct((HALF, D), F32),
        compiler_params=_cp(("parallel",)),
    )(mc, g, ra, rb, rb, rb)


def _share_halves(r):
    def kern(r_ref, o_ref, send, recv, lsem):
        x, y, c, _ = _place()
        mine = pl.ds(pl.multiple_of(c * HALF, 8), HALF)
        own = pltpu.make_async_copy(r_ref, o_ref.at[mine], lsem)
        own.start()
        cp = _rcopy(r_ref, o_ref.at[mine], send, recv, (x, y, 1 - c))
        cp.start()
        cp.wait()
        own.wait()

    return pl.pallas_call(
        kern, name="share_halves", in_specs=[ANY], out_specs=ANY,
        out_shape=jax.ShapeDtypeStruct((PROWS, D), F32),
        scratch_shapes=[pltpu.SemaphoreType.DMA, pltpu.SemaphoreType.DMA, pltpu.SemaphoreType.DMA],
    )(r)


def _reduce_small(s):
    def kern(s_ref, o_ref, buf, send, recv):
        x, y, c, _ = _place()
        me = 4 * x + 2 * y + c
        buf[me] = s_ref[...]
        cps = []
        for r in range(1, 8):
            peer = (1 - x if r & 4 else x, 1 - y if r & 2 else y, 1 - c if r & 1 else c)
            cps.append(_rcopy(s_ref, buf.at[me], send.at[r - 1], recv.at[r - 1], peer))
        for cp in cps:
            cp.start()
        for cp in cps:
            cp.wait()
        acc = buf[0]
        for i in range(1, 8):
            acc = acc + buf[i]
        o_ref[...] = acc

    return pl.pallas_call(
        kern, name="reduce_small",
        in_specs=[pl.BlockSpec(memory_space=pltpu.VMEM)], out_specs=pl.BlockSpec(memory_space=pltpu.VMEM),
        out_shape=jax.ShapeDtypeStruct((SROWS, D), F32),
        scratch_shapes=[pltpu.VMEM((8, SROWS, D), F32), pltpu.SemaphoreType.DMA((7,)), pltpu.SemaphoreType.DMA((7,))],
    )(s)


def _adamw(g, w, m, v, rb, name):
    rows = g.shape[0]

    def kern(g_ref, w_ref, m_ref, v_ref, d_ref, nm_ref, nv_ref):
        g_ = g_ref[...]
        m_ = ADAM_B1 * m_ref[...] + (1.0 - ADAM_B1) * g_
        v_ = ADAM_B2 * v_ref[...] + (1.0 - ADAM_B2) * jnp.square(g_)
        m_hat = m_ / (1.0 - ADAM_B1 ** ADAM_STEP)
        v_hat = v_ / (1.0 - ADAM_B2 ** ADAM_STEP)
        d_ref[...] = -ADAM_LR * (m_hat / (jnp.sqrt(v_hat) + ADAM_EPS) + ADAM_WD * w_ref[...])
        nm_ref[...] = m_
        nv_ref[...] = v_

    blk = pl.BlockSpec((rb, D), lambda i: (i, 0))
    return pl.pallas_call(
        kern, name=name, grid=(rows // rb,), in_specs=[blk] * 4, out_specs=[blk] * 3,
        out_shape=[jax.ShapeDtypeStruct((rows, D), F32)] * 3,
        compiler_params=_cp(("parallel",)),
    )(g, w, m, v)


WEIGHTS = ("c_ctx", "w_mod", "b_mod", "norm_w", "w_in", "ssm_conv_w", "ssm_conv_b", "dt_bias", "a_log", "d_skip",
           "ssm_norm_w", "w_out_ssm", "conf_conv_w", "conf_conv_b", "conf_ln_w", "conf_ln_b", "w_out_conf", "w_out",
           "final_norm_w")


def kernel(x, c, ctx, c_ctx, w_mod, b_mod, norm_w, w_in, ssm_conv_w, ssm_conv_b, dt_bias, a_log, d_skip, ssm_norm_w, w_out_ssm, conf_conv_w, conf_conv_b, conf_ln_w, conf_ln_b, w_out_conf, w_out, final_norm_w, loss_target, m_c_ctx, m_w_mod, m_b_mod, m_norm_w, m_w_in, m_ssm_conv_w, m_ssm_conv_b, m_dt_bias, m_a_log, m_d_skip, m_ssm_norm_w, m_w_out_ssm, m_conf_conv_w, m_conf_conv_b, m_conf_ln_w, m_conf_ln_b, m_w_out_conf, m_w_out, m_final_norm_w, v_c_ctx, v_w_mod, v_b_mod, v_norm_w, v_w_in, v_ssm_conv_w, v_ssm_conv_b, v_dt_bias, v_a_log, v_d_skip, v_ssm_norm_w, v_w_out_ssm, v_conf_conv_w, v_conf_conv_b, v_conf_ln_w, v_conf_ln_b, v_w_out_conf, v_w_out, v_final_norm_w):
    wv = (c_ctx, w_mod, b_mod, norm_w, w_in, ssm_conv_w, ssm_conv_b, dt_bias, a_log, d_skip, ssm_norm_w, w_out_ssm,
          conf_conv_w, conf_conv_b, conf_ln_w, conf_ln_b, w_out_conf, w_out, final_norm_w)
    mv = (m_c_ctx, m_w_mod, m_b_mod, m_norm_w, m_w_in, m_ssm_conv_w, m_ssm_conv_b, m_dt_bias, m_a_log, m_d_skip,
          m_ssm_norm_w, m_w_out_ssm, m_conf_conv_w, m_conf_conv_b, m_conf_ln_w, m_conf_ln_b, m_w_out_conf, m_w_out,
          m_final_norm_w)
    vv = (v_c_ctx, v_w_mod, v_b_mod, v_norm_w, v_w_in, v_ssm_conv_w, v_ssm_conv_b, v_dt_bias, v_a_log, v_d_skip,
          v_ssm_norm_w, v_w_out_ssm, v_conf_conv_w, v_conf_conv_b, v_conf_ln_w, v_conf_ln_b, v_w_out_conf, v_w_out,
          v_final_norm_w)
    shapes = {n: a.shape for n, a in zip(WEIGHTS, wv)}

    def squeeze(d):
        return {n: (a if n in ("c_ctx", "final_norm_w") else a[0]) for n, a in d.items()}

    w, m, v = (squeeze(dict(zip(WEIGHTS, t))) for t in (wv, mv, vv))

    pw, pm, pv = _pack_shard(w), _pack_shard(m), _pack_shard(v)
    gm, gs = _gather_weights(pw.astype(BF16), pw[O_SC:PROWS])
    full = dict(w)
    full.update(_unpack_gathered(gm, gs))

    lsum, grad_x, g = _local_step(x[0], c, ctx[0], loss_target[0], full)
    loss = lax.psum(lsum, ("x", "y", "c"))

    cidx = lax.axis_index("c").astype(jnp.int32).reshape(1)
    mc = jnp.stack([2 * lax.axis_index("x") + lax.axis_index("y"), lax.axis_index("c")]).astype(jnp.int32)
    gp = _pack_full(g)
    sib = _swap_halves(gp)
    part = _add_halves(cidx, gp, sib)
    red = _add_chips(mc, gp, sib, _exchange_chips(part))
    g_sh = _share_halves(red)
    g_sm = _reduce_small(_pack_small(g))

    d_sh, nm_sh, nv_sh = _adamw(g_sh, pw, pm, pv, RB, "adamw_sharded")
    d_sm, nm_sm, nv_sm = _adamw(g_sm, _pack_small(w), _pack_small(m), _pack_small(v), SROWS, "adamw_small")

    outs = []
    for p_sh, p_sm in ((g_sh, g_sm), (d_sh, d_sm), (nm_sh, nm_sm), (nv_sh, nv_sm)):
        d = _unpack_shard(p_sh)
        d.update(_unpack_small(p_sm, shapes))
        outs.extend(d[n] for n in WEIGHTS)
    return (loss, grad_x[None], *outs)
```

```python
import functools

import jax
import jax.numpy as jnp
from jax import lax
from jax.experimental import pallas as pl
from jax.experimental.pallas import tpu as pltpu

F32, BF16 = jnp.float32, jnp.bfloat16

D = 1024
DI = 2048
NH = 32
HP = 64
NG = 8
HPG = 4
NS = 128
Q = 128
GW = 64
CK = 31
SK = 4
CTX = 256
EPS = 1e-6
XBC = DI + 2 * NG * NS
SSDW = XBC + 128
RESTW = 7168
T = 256
VMEM_LIMIT = 56 * 1024 * 1024

ADAM_LR, ADAM_B1, ADAM_B2, ADAM_EPS, ADAM_WD, ADAM_STEP = 0.001, 0.9, 0.999, 1e-08, 0.01, 10


def _cp(sem):
    return pltpu.CompilerParams(dimension_semantics=sem, vmem_limit_bytes=VMEM_LIMIT)


def _sig(x):
    return jax.nn.sigmoid(x)


def _silu(x):
    return x * _sig(x)


def _dsilu(x):
    s = _sig(x)
    return s * (1.0 + x * (1.0 - s))


def _dot(a, b):
    return jnp.dot(a, b, preferred_element_type=F32)


def _dot_nt(a, b):
    return lax.dot_general(a, b, (((1,), (1,)), ((), ())), preferred_element_type=F32)


def _split3(x):
    h = x.astype(BF16)
    r = x - h.astype(F32)
    m = r.astype(BF16)
    l = (r - m.astype(F32)).astype(BF16)
    return h, m, l


def _dot3_l(sel, x):
    h, m, l = _split3(x)
    return _dot(sel, h) + _dot(sel, m) + _dot(sel, l)


def _dot3_r(x, sel):
    h, m, l = _split3(x)
    return _dot(h, sel) + _dot(m, sel) + _dot(l, sel)


def _split2(x):
    h = x.astype(BF16)
    return h, (x - h.astype(F32)).astype(BF16)


def _dot2_l(sel, x):
    h, l = _split2(x)
    return _dot(sel, h) + _dot(sel, l)


def _dot2_r(x, sel):
    h, l = _split2(x)
    return _dot(h, sel) + _dot(l, sel)


def _iota(shape, dim):
    return lax.broadcasted_iota(jnp.int32, shape, dim)


def _mm(a, b, dims, m, n, k, bm, bn, bk, out_dtype, name):
    nk = k // bk
    assert m % bm == 0 and n % bn == 0 and k % bk == 0, (name, m, n, k, bm, bn, bk)

    def prod(a_ref, b_ref):
        av = a_ref[...].astype(BF16)
        bv = b_ref[...].astype(BF16)
        if dims == "nn":
            return _dot(av, bv)
        if dims == "nt":
            return _dot_nt(av, bv)
        return lax.dot_general(av, bv, (((0,), (0,)), ((), ())), preferred_element_type=F32)

    def kern_one(a_ref, b_ref, o_ref):
        o_ref[...] = prod(a_ref, b_ref).astype(out_dtype)

    def kern_acc(a_ref, b_ref, o_ref, acc):
        kk = pl.program_id(2)

        @pl.when(kk == 0)
        def _():
            acc[...] = jnp.zeros_like(acc)

        acc[...] += prod(a_ref, b_ref)

        @pl.when(kk == nk - 1)
        def _():
            o_ref[...] = acc[...].astype(out_dtype)

    if dims == "nn":
        a_spec = pl.BlockSpec((bm, bk), lambda j, i, kk: (i, kk))
        b_spec = pl.BlockSpec((bk, bn), lambda j, i, kk: (kk, j))
    elif dims == "nt":
        a_spec = pl.BlockSpec((bm, bk), lambda j, i, kk: (i, kk))
        b_spec = pl.BlockSpec((bn, bk), lambda j, i, kk: (j, kk))
    else:
        a_spec = pl.BlockSpec((bk, bm), lambda j, i, kk: (kk, i))
        b_spec = pl.BlockSpec((bk, bn), lambda j, i, kk: (kk, j))
    return pl.pallas_call(
        kern_one if nk == 1 else kern_acc, name=name,
        grid=(n // bn, m // bm, nk),
        in_specs=[a_spec, b_spec],
        out_specs=pl.BlockSpec((bm, bn), lambda j, i, kk: (i, j)),
        out_shape=jax.ShapeDtypeStruct((m, n), out_dtype),
        scratch_shapes=[] if nk == 1 else [pltpu.VMEM((bm, bn), F32)],
        compiler_params=_cp(("parallel", "parallel", "arbitrary")),
    )(a, b)


def _mod_fwd(cc, w_mod, b_mod):
    def kern(cc_ref, w_ref, b_ref, o_ref):
        s = _silu(cc_ref[...]).astype(BF16)
        o_ref[...] = _dot(s, w_ref[...]) + b_ref[...]

    return pl.pallas_call(
        kern, name="mod_fwd", grid=(3,),
        in_specs=[pl.BlockSpec((8, D), lambda j: (0, 0)), pl.BlockSpec((D, D), lambda j: (0, j)),
                  pl.BlockSpec((1, D), lambda j: (0, j))],
        out_specs=pl.BlockSpec((8, D), lambda j: (0, j)),
        out_shape=jax.ShapeDtypeStruct((8, 3 * D), F32),
        compiler_params=_cp(("parallel",)),
    )(cc, w_mod, b_mod)


def _mod_bwd(dmod, cc, cct, w_mod):
    def kern(dm_ref, cc_ref, cct_ref, w_ref, gw_ref, gb_ref, gc_ref):
        kk = pl.program_id(0)
        dm = dm_ref[...]
        sct = _silu(cct_ref[...])
        gw_ref[...] = sct[:, 0:1] * dm[0:1, :] + sct[:, 1:2] * dm[1:2, :]
        gb_ref[...] = jnp.broadcast_to(dm[0:1, :] + dm[1:2, :], dm.shape)

        @pl.when(kk == 0)
        def _():
            gc_ref[...] = jnp.zeros_like(gc_ref)

        gc_ref[...] += _dot_nt(dm.astype(BF16), w_ref[...])

        @pl.when(kk == 2)
        def _():
            gc_ref[...] = gc_ref[...] * _dsilu(cc_ref[...])

    return pl.pallas_call(
        kern, name="mod_bwd", grid=(3,),
        in_specs=[pl.BlockSpec((8, D), lambda j: (0, j)), pl.BlockSpec((8, D), lambda j: (0, 0)),
                  pl.BlockSpec((D, 8), lambda j: (0, 0)), pl.BlockSpec((D, D), lambda j: (0, j))],
        out_specs=[pl.BlockSpec((D, D), lambda j: (0, j)), pl.BlockSpec((8, D), lambda j: (0, j)),
                   pl.BlockSpec((8, D), lambda j: (0, 0))],
        out_shape=[jax.ShapeDtypeStruct((D, 3 * D), F32), jax.ShapeDtypeStruct((8, 3 * D), F32),
                   jax.ShapeDtypeStruct((8, D), F32)],
        compiler_params=_cp(("arbitrary",)),
    )(dmod, cc, cct, w_mod)


def _ext_specs(nct):
    return (pl.BlockSpec((T, D), lambda i: (jnp.minimum(i, nct - 1), 0)),
            pl.BlockSpec((T, D), lambda i: (jnp.maximum(i - nct, 0), 0)))


def _norm_fwd(ctx, xl, mod, nw, nct):
    lext = ctx.shape[0] + xl.shape[0]

    def kern(c_ref, x_ref, mod_ref, nw_ref, h_ref):
        is_ctx = pl.program_id(0) < nct
        x = jnp.where(is_ctx, c_ref[...], x_ref[...])
        r = lax.rsqrt(jnp.mean(x * x, axis=-1, keepdims=True) + EPS)
        xn = x * r * nw_ref[...]
        shift = jnp.where(is_ctx, mod_ref[1:2, 0:D], mod_ref[0:1, 0:D])
        scale = jnp.where(is_ctx, mod_ref[1:2, D:2 * D], mod_ref[0:1, D:2 * D])
        h_ref[...] = (xn * (1.0 + scale) + shift).astype(BF16)

    return pl.pallas_call(
        kern, name="norm_fwd", grid=(lext // T,),
        in_specs=[*_ext_specs(nct), pl.BlockSpec((8, 3 * D), lambda i: (0, 0)),
                  pl.BlockSpec((1, D), lambda i: (0, 0))],
        out_specs=pl.BlockSpec((T, D), lambda i: (i, 0)),
        out_shape=jax.ShapeDtypeStruct((lext, D), BF16),
        compiler_params=_cp(("parallel",)),
    )(ctx, xl, mod, nw)


def _norm_bwd(dha, dhb, ctx, xl, dx2, mod, nw, nct):
    lext = ctx.shape[0] + xl.shape[0]
    ntl = lext // T

    def kern(dha_ref, dhb_ref, c_ref, x_ref, dx2_ref, mod_ref, nw_ref, gx_ref, gnw_ref, dss_ref):
        i = pl.program_id(0)
        is_ctx = i < nct

        @pl.when(i == 0)
        def _():
            gnw_ref[...] = jnp.zeros_like(gnw_ref)
            dss_ref[...] = jnp.zeros_like(dss_ref)

        x = jnp.where(is_ctx, c_ref[...], x_ref[...])
        dh_ = dha_ref[...] + jnp.where(is_ctx, 0.0, dhb_ref[...])
        nw_ = nw_ref[...]
        r = lax.rsqrt(jnp.mean(x * x, axis=-1, keepdims=True) + EPS)
        xn = x * r
        scale = jnp.where(is_ctx, mod_ref[1:2, D:2 * D], mod_ref[0:1, D:2 * D])
        dsh = jnp.sum(dh_, axis=0, keepdims=True)
        dsc = jnp.sum(dh_ * (xn * nw_), axis=0, keepdims=True)
        row = jnp.concatenate([dsh, dsc], axis=1)
        rid = _iota((8, 2 * D), 0)
        dss_ref[...] += jnp.where(rid == jnp.where(is_ctx, 1, 0), row, 0.0)
        dxnw = dh_ * (1.0 + scale)
        gnw_ref[...] += jnp.broadcast_to(jnp.sum(dxnw * xn, axis=0, keepdims=True), (8, D))
        dxn = dxnw * nw_
        dx = r * (dxn - xn * jnp.mean(dxn * xn, axis=-1, keepdims=True))
        gx_ref[...] = dx2_ref[...] + dx

    return pl.pallas_call(
        kern, name="norm_bwd", grid=(ntl,),
        in_specs=[pl.BlockSpec((T, D), lambda i: (i, 0)), pl.BlockSpec((T, D), lambda i: (jnp.maximum(i - nct, 0), 0)),
                  *_ext_specs(nct),
                  pl.BlockSpec((T, D), lambda i: (jnp.maximum(i - nct, 0), 0)),
                  pl.BlockSpec((8, 3 * D), lambda i: (0, 0)), pl.BlockSpec((1, D), lambda i: (0, 0))],
        out_specs=[pl.BlockSpec((T, D), lambda i: (jnp.maximum(i - nct, 0), 0)),
                   pl.BlockSpec((8, D), lambda i: (0, 0)), pl.BlockSpec((8, 2 * D), lambda i: (0, 0))],
        out_shape=[jax.ShapeDtypeStruct((lext - nct * T, D), F32), jax.ShapeDtypeStruct((8, D), F32),
                   jax.ShapeDtypeStruct((8, 2 * D), F32)],
        compiler_params=_cp(("arbitrary",)),
    )(dha, dhb, ctx, xl, dx2, mod, nw)


CB = 1024


def _halo_specs(width_blk, col_off_blocks, ntl):
    t8 = T // 8
    main = pl.BlockSpec((T, width_blk), lambda j, i: (i, j + col_off_blocks))
    prev = pl.BlockSpec((8, width_blk), lambda j, i: (jnp.maximum(i * t8 - 1, 0), j + col_off_blocks))
    nxt = pl.BlockSpec((8, width_blk), lambda j, i: (jnp.minimum((i + 1) * t8, ntl * t8 - 1), j + col_off_blocks))
    return main, prev, nxt


def _seq_edges(i, nct, ntl):
    starts = jnp.logical_or(i == 0, i == nct)
    ends = jnp.logical_or(i == nct - 1, i == ntl - 1)
    return starts, ends


def _shifted(ext, off):
    n = ext.shape[0]
    return pltpu.roll(ext, (-off) % n, axis=0)[8:8 + T]


def _conv_fwd(proj_ssd, cw, cb, nct):
    lext = proj_ssd.shape[0]
    ntl = lext // T

    def kern(u_ref, up_ref, un_ref, w_ref, b_ref, o_ref):
        i = pl.program_id(1)
        starts, ends = _seq_edges(i, nct, ntl)
        up = jnp.where(starts, 0.0, up_ref[...])
        un = jnp.where(ends, 0.0, un_ref[...])
        ext = jnp.concatenate([up, u_ref[...], un], axis=0)
        w = w_ref[...]
        pre = b_ref[...] + w[0:1] * _shifted(ext, -2) + w[1:2] * _shifted(ext, -1) \
            + w[2:3] * u_ref[...] + w[3:4] * _shifted(ext, 1)
        o_ref[...] = _silu(pre)

    main, prev, nxt = _halo_specs(CB, 0, ntl)
    return pl.pallas_call(
        kern, name="conv_fwd", grid=(XBC // CB, ntl),
        in_specs=[main, prev, nxt, pl.BlockSpec((8, CB), lambda j, i: (0, j)), pl.BlockSpec((1, CB), lambda j, i: (0, j))],
        out_specs=pl.BlockSpec((T, CB), lambda j, i: (i, j)),
        out_shape=jax.ShapeDtypeStruct((lext, XBC), F32),
        compiler_params=_cp(("parallel", "parallel")),
    )(proj_ssd, proj_ssd, proj_ssd, cw, cb)


def _conv_bwd(dpost, proj_ssd, cw, cb, col_off, width, nct, name):
    lext = proj_ssd.shape[0]
    ntl = lext // T
    cob = col_off // CB

    def kern(u_ref, up_ref, un_ref, d_ref, dp_ref, dn_ref, w_ref, b_ref, du_ref, gw_ref, gb_ref):
        i = pl.program_id(1)

        @pl.when(i == 0)
        def _():
            gw_ref[...] = jnp.zeros_like(gw_ref)
            gb_ref[...] = jnp.zeros_like(gb_ref)

        starts, ends = _seq_edges(i, nct, ntl)
        ext = jnp.concatenate([jnp.where(starts, 0.0, up_ref[...]), u_ref[...], jnp.where(ends, 0.0, un_ref[...])], axis=0)
        dext = jnp.concatenate([jnp.where(starts, 0.0, dp_ref[...]), d_ref[...], jnp.where(ends, 0.0, dn_ref[...])], axis=0)
        w = w_ref[...]
        n = ext.shape[0]
        pre = b_ref[...] + w[0:1] * pltpu.roll(ext, 2, axis=0) + w[1:2] * pltpu.roll(ext, 1, axis=0) \
            + w[2:3] * ext + w[3:4] * pltpu.roll(ext, n - 1, axis=0)
        dpre = dext * _dsilu(pre)
        dm = dpre[8:8 + T]
        du = w[0:1] * _shifted(dpre, 2) + w[1:2] * _shifted(dpre, 1) + w[2:3] * dm + w[3:4] * _shifted(dpre, -1)
        du_ref[...] = du.astype(BF16)
        g0 = jnp.sum(dm * _shifted(ext, -2), axis=0, keepdims=True)
        g1 = jnp.sum(dm * _shifted(ext, -1), axis=0, keepdims=True)
        g2 = jnp.sum(dm * u_ref[...], axis=0, keepdims=True)
        g3 = jnp.sum(dm * _shifted(ext, 1), axis=0, keepdims=True)
        rid = _iota((8, CB), 0)
        gw_ref[...] += jnp.where(rid == 0, g0, jnp.where(rid == 1, g1, jnp.where(rid == 2, g2, jnp.where(rid == 3, g3, 0.0))))
        gb_ref[...] += jnp.broadcast_to(jnp.sum(dm, axis=0, keepdims=True), (8, CB))

    main, prev, nxt = _halo_specs(CB, cob, ntl)
    dmain, dprev, dnxt = _halo_specs(CB, 0, ntl)
    return pl.pallas_call(
        kern, name=name, grid=(width // CB, ntl),
        in_specs=[main, prev, nxt, dmain, dprev, dnxt,
                  pl.BlockSpec((8, CB), lambda j, i: (0, j + cob)), pl.BlockSpec((1, CB), lambda j, i: (0, j + cob))],
        out_specs=[pl.BlockSpec((T, CB), lambda j, i: (i, j)), pl.BlockSpec((8, CB), lambda j, i: (0, j)),
                   pl.BlockSpec((8, CB), lambda j, i: (0, j))],
        out_shape=[jax.ShapeDtypeStruct((lext, width), BF16), jax.ShapeDtypeStruct((8, width), F32),
                   jax.ShapeDtypeStruct((8, width), F32)],
        compiler_params=_cp(("parallel", "arbitrary")),
    )(proj_ssd, proj_ssd, proj_ssd, dpost, dpost, dpost, cw, cb)


def _tri(lower):
    r, c = _iota((Q, Q), 0), _iota((Q, Q), 1)
    return jnp.where((c <= r) if lower else (c >= r), 1.0, 0.0).astype(BF16)


def _is_bdir_lane(shape):
    ln = _iota(shape, len(shape) - 1)
    return jnp.logical_and(((ln >> 2) & 1) == 1, ln < 64)


def _dt_fwd(proj_ssd, dtb, av):
    lext = proj_ssd.shape[0]

    def kern(p_ref, b_ref, a_ref, dtg_ref, lag_ref, dtt_ref, lat_ref):
        lane = _iota((T, 128), 1)
        raw = p_ref[...] + b_ref[...]
        dt = jnp.where(lane < 64, jnp.maximum(raw, 0.0) + jnp.log1p(jnp.exp(-jnp.abs(raw))), 0.0)
        dta = dt * a_ref[...]
        tl, tu = _tri(True), _tri(False)
        isb = _is_bdir_lane((Q, 128))
        las = []
        for qq in range(T // Q):
            blk = dta[qq * Q:(qq + 1) * Q]
            las.append(jnp.where(isb, _dot3_l(tu, blk), _dot3_l(tl, blk)))
        la = jnp.concatenate(las, axis=0)
        for g in range(NG):
            sh = (128 - 8 * g) % 128
            dtg_ref[g] = jnp.where(lane < 8, pltpu.roll(dt, sh, axis=1) if sh else dt, 0.0)
            lag_ref[g] = jnp.where(lane < 8, pltpu.roll(la, sh, axis=1) if sh else la, 0.0)
        dtt_ref[...] = dt.T[0:64]
        lat_ref[...] = la.T[0:64]

    return pl.pallas_call(
        kern, name="dt_fwd", grid=(lext // T,),
        in_specs=[pl.BlockSpec((T, 128), lambda i: (i, XBC // 128)), pl.BlockSpec((1, 128), lambda i: (0, 0)),
                  pl.BlockSpec((1, 128), lambda i: (0, 0))],
        out_specs=[pl.BlockSpec((NG, T, 128), lambda i: (0, i, 0)), pl.BlockSpec((NG, T, 128), lambda i: (0, i, 0)),
                   pl.BlockSpec((64, T), lambda i: (0, i)), pl.BlockSpec((64, T), lambda i: (0, i))],
        out_shape=[jax.ShapeDtypeStruct((NG, lext, 128), F32), jax.ShapeDtypeStruct((NG, lext, 128), F32),
                   jax.ShapeDtypeStruct((64, lext), F32), jax.ShapeDtypeStruct((64, lext), F32)],
        compiler_params=_cp(("parallel",)),
    )(proj_ssd, dtb, av)


def _dt_bwd(ddtg, proj_ssd, dtb):
    lext = proj_ssd.shape[0]

    def kern(d_ref, p_ref, b_ref, o_ref, gb_ref):
        @pl.when(pl.program_id(0) == 0)
        def _():
            gb_ref[...] = jnp.zeros_like(gb_ref)

        acc = d_ref[0]
        for g in range(1, NG):
            acc = acc + pltpu.roll(d_ref[g], 8 * g, axis=1)
        draw = acc * _sig(p_ref[...] + b_ref[...])
        o_ref[...] = draw.astype(BF16)
        gb_ref[...] += jnp.broadcast_to(jnp.sum(draw, axis=0, keepdims=True), (8, 128))

    return pl.pallas_call(
        kern, name="dt_bwd", grid=(lext // T,),
        in_specs=[pl.BlockSpec((NG, T, 128), lambda i: (0, i, 0)), pl.BlockSpec((T, 128), lambda i: (i, XBC // 128)),
                  pl.BlockSpec((1, 128), lambda i: (0, 0))],
        out_specs=[pl.BlockSpec((T, 128), lambda i: (i, 0)), pl.BlockSpec((8, 128), lambda i: (0, 0))],
        out_shape=[jax.ShapeDtypeStruct((lext, 128), BF16), jax.ShapeDtypeStruct((8, 128), F32)],
        compiler_params=_cp(("arbitrary",)),
    )(ddtg, proj_ssd, dtb)


def _expand_sel(d):
    r, c = _iota((128, 256), 0), _iota((128, 256), 1)
    return jnp.where(r == 4 * d + (c >> 6), 1.0, 0.0).astype(BF16)


def _reduce_sel(d):
    r, c = _iota((256, 128), 0), _iota((256, 128), 1)
    return jnp.where(c == 4 * d + (r >> 6), 1.0, 0.0).astype(BF16)


def _chunk_of_bwd_dir(j, ncc, nc):
    return jnp.where(j < ncc, ncc - 1 - j, nc + ncc - 1 - j)


def _dir_terms(la, dt, d):
    lane = _iota(la.shape, 1)
    mine = jnp.logical_and(lane >= 4 * d, lane < 4 * d + 4)
    la = jnp.where(mine, la, 0.0)
    tot = la[Q - 1:Q] if d == 0 else la[0:1]
    wnd = jnp.exp(tot - la)
    return tot, wnd * jnp.where(mine, dt, 0.0), wnd


def _ssd_state(xbc, dtg, lag, ncc):
    lext = xbc.shape[0]
    nc = lext // Q

    def kern(xf_ref, bf_ref, dtf_ref, laf_ref, xb_ref, bb_ref, dtb_ref, lab_ref, hf_ref, hb_ref, sf, sb):
        @pl.when(pl.program_id(0) == 0)
        def _():
            sf[...] = jnp.zeros_like(sf)
            sb[...] = jnp.zeros_like(sb)

        for d, (x_ref, b_ref, dt_ref, la_ref, h_ref, s) in enumerate(
                ((xf_ref, bf_ref, dtf_ref, laf_ref, hf_ref, sf), (xb_ref, bb_ref, dtb_ref, lab_ref, hb_ref, sb))):
            h_ref[...] = s[...]
            ex = _expand_sel(d)
            for g in range(NG):
                cols = slice(256 * g, 256 * (g + 1))
                tot, w_end, _ = _dir_terms(la_ref[g], dt_ref[g], d)
                wexp = _dot2_r(w_end, ex)
                dexp = _dot2_r(jnp.broadcast_to(jnp.exp(tot), (8, 128)), ex)[0:1]
                xw = (x_ref[:, cols] * wexp).astype(BF16)
                s[:, cols] = s[:, cols] * dexp + _dot(b_ref[:, 128 * g:128 * (g + 1)].T.astype(BF16), xw)

    cb = functools.partial(_chunk_of_bwd_dir, ncc=ncc, nc=nc)
    sm = lambda f: pl.BlockSpec((NG, Q, 128), lambda j: (0, f(j), 0))
    one = lambda j: j
    return pl.pallas_call(
        kern, name="ssd_state", grid=(nc,),
        in_specs=[pl.BlockSpec((Q, DI), lambda j: (j, 0)), pl.BlockSpec((Q, NG * NS), lambda j: (j, 2)), sm(one), sm(one),
                  pl.BlockSpec((Q, DI), lambda j: (cb(j), 0)), pl.BlockSpec((Q, NG * NS), lambda j: (cb(j), 2)), sm(cb), sm(cb)],
        out_specs=[pl.BlockSpec((None, 128, DI), lambda j: (j, 0, 0)),
                   pl.BlockSpec((None, 128, DI), lambda j: (cb(j), 0, 0))],
        out_shape=[jax.ShapeDtypeStruct((nc, 128, DI), F32), jax.ShapeDtypeStruct((nc, 128, DI), F32)],
        scratch_shapes=[pltpu.VMEM((128, DI), F32), pltpu.VMEM((128, DI), F32)],
        compiler_params=_cp(("arbitrary",)),
    )(xbc, xbc, dtg, lag, xbc, xbc, dtg, lag)


def _ssd_out(xbc, dtg, lag, dtt, lat, htf, htb, ncc):
    lext = xbc.shape[0]
    nc = lext // Q
    ncx = nc - ncc

    gps = 4
    li, si = (lambda: _iota((Q, Q), 0)), (lambda: _iota((Q, Q), 1))

    def kern(x_ref, b_ref, c_ref, dtg_ref, lag_ref, dtt_ref, lat_ref, hf_ref, hb_ref, y_ref):
        lane = _iota((Q, 256), 1)
        masks = (li() >= si(), li() <= si())
        for gg in range(gps):
            cols = slice(256 * gg, 256 * (gg + 1))
            cm = c_ref[:, 128 * gg:128 * (gg + 1)]
            xb_ = x_ref[:, cols].astype(BF16)
            s_ = _dot_nt(cm.astype(BF16), b_ref[:, 128 * gg:128 * (gg + 1)].astype(BF16))
            la, dtt_, lat_ = lag_ref[gg], dtt_ref[8 * gg:8 * (gg + 1)], lat_ref[8 * gg:8 * (gg + 1)]
            elam = jnp.exp(la)
            y = jnp.zeros((Q, 256), F32)
            for d, h_ref in enumerate((hf_ref, hb_ref)):
                rhs = jnp.concatenate([xb_, h_ref[:, cols].astype(BF16)], axis=0)
                for r in range(HPG):
                    j = 4 * d + r
                    lm = jnp.where(masks[d], jnp.exp(la[:, j:j + 1] - lat_[j:j + 1, :]), 0.0)
                    w = s_ * lm * dtt_[j:j + 1, :]
                    lhs = jnp.concatenate([w, cm * elam[:, j:j + 1]], axis=1).astype(BF16)
                    y = y + jnp.where((lane >> 6) == r, _dot(lhs, rhs), 0.0)
            y_ref[:, cols] = y

    nb = NG // gps
    sm = pl.BlockSpec((gps, Q, 128), lambda c, g: (g, c + ncc, 0))
    smt = pl.BlockSpec((8 * gps, Q), lambda c, g: (g, c + ncc))
    st3 = pl.BlockSpec((None, 128, 256 * gps), lambda c, g: (c + ncc, 0, g))
    return pl.pallas_call(
        kern, name="ssd_out", grid=(ncx, nb),
        in_specs=[pl.BlockSpec((Q, 256 * gps), lambda c, g: (c + ncc, g)),
                  pl.BlockSpec((Q, 128 * gps), lambda c, g: (c + ncc, 2 * nb + g)),
                  pl.BlockSpec((Q, 128 * gps), lambda c, g: (c + ncc, 3 * nb + g)), sm, sm, smt, smt, st3, st3],
        out_specs=pl.BlockSpec((Q, 256 * gps), lambda c, g: (c, g)),
        out_shape=jax.ShapeDtypeStruct((ncx * Q, DI), F32),
        compiler_params=_cp(("parallel", "parallel")),
    )(xbc, xbc, xbc, dtg, lag, dtt, lat, htf, htb)


def _ssd_bwd_state(xbc, dy, lag, ncc):
    lext = xbc.shape[0]
    nc = lext // Q

    def kern(cf_ref, dyf_ref, laf_ref, cb_ref, dyb_ref, lab_ref, df_ref, db_ref, sf, sb):
        @pl.when(pl.program_id(0) == 0)
        def _():
            sf[...] = jnp.zeros_like(sf)
            sb[...] = jnp.zeros_like(sb)

        for d, (c_ref, dy_ref, la_ref, o_ref, s) in enumerate(
                ((cf_ref, dyf_ref, laf_ref, df_ref, sf), (cb_ref, dyb_ref, lab_ref, db_ref, sb))):
            o_ref[...] = s[...]
            ex = _expand_sel(d)
            for g in range(NG):
                cols = slice(256 * g, 256 * (g + 1))
                la = la_ref[g]
                tot = la[Q - 1:Q] if d == 0 else la[0:1]
                eexp = _dot2_r(jnp.exp(la), ex)
                dexp = _dot2_r(jnp.broadcast_to(jnp.exp(tot), (8, 128)), ex)[0:1]
                dye = (dy_ref[:, cols] * eexp).astype(BF16)
                s[:, cols] = s[:, cols] * dexp + _dot(c_ref[:, 128 * g:128 * (g + 1)].T.astype(BF16), dye)

    cf = lambda j: nc - 1 - j
    cb = lambda j: _chunk_of_bwd_dir(nc - 1 - j, ncc, nc)
    sm = lambda f: pl.BlockSpec((NG, Q, 128), lambda j: (0, f(j), 0))
    return pl.pallas_call(
        kern, name="ssd_bwd_state", grid=(nc,),
        in_specs=[pl.BlockSpec((Q, NG * NS), lambda j: (cf(j), 3)), pl.BlockSpec((Q, DI), lambda j: (cf(j), 0)), sm(cf),
                  pl.BlockSpec((Q, NG * NS), lambda j: (cb(j), 3)), pl.BlockSpec((Q, DI), lambda j: (cb(j), 0)), sm(cb)],
        out_specs=[pl.BlockSpec((None, 128, DI), lambda j: (cf(j), 0, 0)),
                   pl.BlockSpec((None, 128, DI), lambda j: (cb(j), 0, 0))],
        out_shape=[jax.ShapeDtypeStruct((nc, 128, DI), F32), jax.ShapeDtypeStruct((nc, 128, DI), F32)],
        scratch_shapes=[pltpu.VMEM((128, DI), F32), pltpu.VMEM((128, DI), F32)],
        compiler_params=_cp(("arbitrary",)),
    )(xbc, dy, lag, xbc, dy, lag)


def _ssd_bwd_out(xbc, dy, dsk, dtg, lag, dtt, lat, htf, htb, dhf, dhb, a_rows):
    lext = xbc.shape[0]
    nc = lext // Q

    def kern(x_ref, b_ref, c_ref, dy_ref, sk_ref, dtg_ref, lag_ref, dtt_ref, lat_ref, hf_ref, hb_ref, df_ref, db_ref,
             a_ref, dx_ref, dbo_ref, dco_ref, ddt_ref, ga_ref):
        g = pl.program_id(0)

        @pl.when(pl.program_id(1) == 0)
        def _():
            ga_ref[...] = jnp.zeros_like(ga_ref)

        x, bm, cm, dy_ = x_ref[...], b_ref[...], c_ref[...], dy_ref[...]
        xb_, bb_, cb_, dyb_ = x.astype(BF16), bm.astype(BF16), cm.astype(BF16), dy_.astype(BF16)
        st = _dot_nt(bb_, cb_)
        si, li = _iota((Q, Q), 0), _iota((Q, Q), 1)
        lane = _iota((Q, 256), 1)
        lane128 = _iota((Q, 128), 1)
        row128 = _iota((Q, 128), 0)
        sub = _iota((128, Q), 0)
        la, dt, dtt_, lat_ = lag_ref[...], dtg_ref[...], dtt_ref[...], lat_ref[...]
        elam = jnp.exp(la)
        dst = jnp.zeros((Q, Q), F32)
        dxa = jnp.zeros((Q, 256), F32)
        dba = jnp.zeros((Q, 128), F32)
        dca = jnp.zeros((Q, 128), F32)
        dlam = jnp.zeros((Q, 128), F32)
        ddir = jnp.zeros((Q, 128), F32)
        rows = jnp.zeros((128, Q), F32)
        for d, (h_ref, dh_ref) in enumerate(((hf_ref, df_ref), (hb_ref, db_ref))):
            ht, dht = h_ref[...], dh_ref[...]
            htb_, dhtb_ = ht.astype(BF16), dht.astype(BF16)
            tot, w_end, wnd = _dir_terms(la, dt, d)
            ex, rs = _expand_sel(d), _reduce_sel(d)
            elx = _dot2_r(elam, ex)
            wex = _dot2_r(w_end, ex)
            dye = dy_ * elx
            ch = _dot(cb_, htb_)
            bd = _dot(bb_, dhtb_)
            dca = dca + _dot_nt(dye.astype(BF16), htb_)
            dba = dba + _dot_nt((x * wex).astype(BF16), dhtb_)
            dlam = dlam + _dot2_r(dye * ch, rs)
            xbd = _dot2_r(x * bd, rs)
            e_ = w_end * xbd
            dlam = dlam - e_
            ddir = ddir + wnd * xbd
            hh = _dot2_r(jnp.broadcast_to(jnp.sum(dht * ht, axis=0, keepdims=True), (8, 256)), rs)[0:1]
            tot_term = jnp.sum(e_, axis=0, keepdims=True) + jnp.exp(tot) * hh
            dlam = dlam + jnp.where(row128 == (Q - 1 if d == 0 else 0), tot_term, 0.0)
            rhs = jnp.concatenate([dyb_, dhtb_], axis=0)
            maskt = (li >= si) if d == 0 else (li <= si)
            for r in range(HPG):
                j = 4 * d + r
                dc = dt[:, j:j + 1]
                lmt = jnp.where(maskt, jnp.exp(lat_[j:j + 1, :] - la[:, j:j + 1]), 0.0)
                wt = st * lmt * dc
                lhs = jnp.concatenate([wt, bm * w_end[:, j:j + 1]], axis=1).astype(BF16)
                hm = (lane >> 6) == r
                dxa = dxa + jnp.where(hm, _dot(lhs, rhs), 0.0)
                dwt = _dot_nt(jnp.where(hm, x, 0.0).astype(BF16), dyb_)
                dl = dwt * lmt
                gpt = dl * st
                cs = jnp.sum(gpt, axis=1, keepdims=True)
                ddir = ddir + jnp.where(lane128 == j, cs, 0.0)
                dlam = dlam - jnp.where(lane128 == j, cs * dc, 0.0)
                rows = rows + jnp.where(sub == j, jnp.sum(gpt * dc, axis=0, keepdims=True), 0.0)
                dst = dst + dl * dc
        dlam = dlam + rows.T
        dba = dba + _dot(dst.astype(BF16), cb_)
        dca = dca + _dot(dst.T.astype(BF16), bb_)
        isb = jnp.logical_and(lane128 >= 4, lane128 < 8)
        ddel = jnp.where(isb, _dot2_l(_tri(True), dlam), _dot2_l(_tri(False), dlam))
        a_l = a_ref[pl.ds(g, 1), :]
        ddt_ref[...] = ddir + a_l * ddel
        ga_ref[...] += jnp.broadcast_to(a_l * jnp.sum(dt * ddel, axis=0, keepdims=True), (8, 128))
        dx_ref[...] = dxa + dy_ * sk_ref[...]
        dbo_ref[...] = dba
        dco_ref[...] = dca

    st3 = pl.BlockSpec((None, 128, 256), lambda g, c: (c, 0, g))
    sm = pl.BlockSpec((None, Q, 128), lambda g, c: (g, c, 0))
    smt = pl.BlockSpec((8, Q), lambda g, c: (g, c))
    return pl.pallas_call(
        kern, name="ssd_bwd_out", grid=(NG, nc),
        in_specs=[pl.BlockSpec((Q, 256), lambda g, c: (c, g)), pl.BlockSpec((Q, 128), lambda g, c: (c, 16 + g)),
                  pl.BlockSpec((Q, 128), lambda g, c: (c, 24 + g)), pl.BlockSpec((Q, 256), lambda g, c: (c, g)),
                  pl.BlockSpec((1, 256), lambda g, c: (0, g)), sm, sm, smt, smt, st3, st3, st3, st3,
                  pl.BlockSpec((8, 128), lambda g, c: (0, 0))],
        out_specs=[pl.BlockSpec((Q, 256), lambda g, c: (c, g)), pl.BlockSpec((Q, 128), lambda g, c: (c, g)),
                   pl.BlockSpec((Q, 128), lambda g, c: (c, g)), sm, pl.BlockSpec((None, 8, 128), lambda g, c: (g, 0, 0))],
        out_shape=[jax.ShapeDtypeStruct((lext, DI), F32), jax.ShapeDtypeStruct((lext, NG * NS), F32),
                   jax.ShapeDtypeStruct((lext, NG * NS), F32), jax.ShapeDtypeStruct((NG, lext, 128), F32),
                   jax.ShapeDtypeStruct((NG, 8, 128), F32)],
        compiler_params=_cp(("parallel", "arbitrary")),
    )(xbc, xbc, xbc, dy, dsk, dtg, lag, dtt, lat, htf, htb, dhf, dhb, a_rows)


def _post_fwd(yssm, xbc, proj_rest, dsk, gnw, nct):
    l = yssm.shape[0]

    def kern(y_ref, x_ref, z_ref, dsk_ref, w_ref, o_ref):
        y = y_ref[...] + dsk_ref[...] * x_ref[...]
        yz = y * _silu(z_ref[...])
        for g in range(NG):
            sl = slice(256 * g, 256 * (g + 1))
            v = yz[:, sl]
            r = lax.rsqrt(jnp.mean(v * v, axis=-1, keepdims=True) + EPS)
            o_ref[:, sl] = (v * r * w_ref[:, sl]).astype(BF16)

    return pl.pallas_call(
        kern, name="post_fwd", grid=(l // T,),
        in_specs=[pl.BlockSpec((T, DI), lambda i: (i, 0)), pl.BlockSpec((T, DI), lambda i: (i + nct, 0)),
                  pl.BlockSpec((T, DI), lambda i: (i, 0)), pl.BlockSpec((1, DI), lambda i: (0, 0)),
                  pl.BlockSpec((1, DI), lambda i: (0, 0))],
        out_specs=pl.BlockSpec((T, DI), lambda i: (i, 0)),
        out_shape=jax.ShapeDtypeStruct((l, DI), BF16),
        compiler_params=_cp(("parallel",)),
    )(yssm, xbc, proj_rest, dsk, gnw)


def _post_bwd(dgn, yssm, xbc, proj_rest, dsk, gnw, dpr, nct):
    l = yssm.shape[0]
    lext = xbc.shape[0]
    xi = lambda i: (jnp.maximum(i - nct, 0), 0)

    def kern(dg_ref, y_ref, x_ref, z_ref, dsk_ref, w_ref, dpr_ref, dy_ref, dz_ref, gw_ref, gd_ref):
        i = pl.program_id(0)

        @pl.when(i == 0)
        def _():
            gw_ref[...] = jnp.zeros_like(gw_ref)
            gd_ref[...] = jnp.zeros_like(gd_ref)

        @pl.when(i < nct)
        def _():
            dy_ref[...] = jnp.zeros_like(dy_ref)

        @pl.when(i >= nct)
        def _():
            xs = x_ref[...]
            z = z_ref[...]
            y = y_ref[...] + dsk_ref[...] * xs
            sz = _silu(z)
            yz = y * sz
            dgn_ = dg_ref[...]
            dyz_parts = []
            gws = []
            for g in range(NG):
                sl = slice(256 * g, 256 * (g + 1))
                v = yz[:, sl]
                r = lax.rsqrt(jnp.mean(v * v, axis=-1, keepdims=True) + EPS)
                vn = v * r
                dn = dgn_[:, sl] * w_ref[:, sl]
                gws.append(jnp.sum(dgn_[:, sl] * vn, axis=0, keepdims=True))
                dyz_parts.append(r * (dn - vn * jnp.mean(dn * vn, axis=-1, keepdims=True)))
            dyz = jnp.concatenate(dyz_parts, axis=1)
            gw_ref[...] += jnp.broadcast_to(jnp.concatenate(gws, axis=1), (8, DI))
            dy = dyz * sz
            dz_ref[...] = (dyz * y * _dsilu(z)).astype(BF16)
            gd_ref[...] += jnp.broadcast_to(jnp.sum(dy * xs, axis=0, keepdims=True), (8, DI))
            dy_ref[...] = dy

    return pl.pallas_call(
        kern, name="post_bwd", grid=(lext // T,),
        in_specs=[pl.BlockSpec((T, DI), xi), pl.BlockSpec((T, DI), xi), pl.BlockSpec((T, DI), lambda i: (i, 0)),
                  pl.BlockSpec((T, DI), xi), pl.BlockSpec((1, DI), lambda i: (0, 0)), pl.BlockSpec((1, DI), lambda i: (0, 0)),
                  pl.BlockSpec(memory_space=pl.ANY)],
        out_specs=[pl.BlockSpec((T, DI), lambda i: (i, 0)),
                   pl.BlockSpec((T, DI), xi), pl.BlockSpec((8, DI), lambda i: (0, 0)), pl.BlockSpec((8, DI), lambda i: (0, 0))],
        out_shape=[jax.ShapeDtypeStruct((lext, DI), F32),
                   jax.ShapeDtypeStruct((l, RESTW), BF16), jax.ShapeDtypeStruct((8, DI), F32), jax.ShapeDtypeStruct((8, DI), F32)],
        input_output_aliases={6: 1},
        compiler_params=_cp(("arbitrary",)),
    )(dgn, yssm, xbc, proj_rest, dsk, gnw, dpr)


C_G1, C_G2, C_GA, C_GB, C_CG = 2, 3, 4, 5, 6
PITCH = GW + 16
NROW = T // GW


def _pad_rows(a):
    z = jnp.zeros((PITCH - GW, D), F32)
    parts = []
    for r in range(NROW):
        parts += [a[GW * r:GW * (r + 1)], z]
    return jnp.concatenate(parts, axis=0)


def _unpad_rows(p):
    return jnp.concatenate([p[PITCH * r:PITCH * r + GW] for r in range(NROW)], axis=0)


def _row_conv(p, w, transpose):
    n = p.shape[0]
    acc = w[15:16] * p
    for k in range(CK):
        off = (k - 15) if not transpose else (15 - k)
        if off != 0:
            acc = acc + w[k:k + 1] * pltpu.roll(p, (-off) % n, axis=0)
    return acc


def _ln_stats(cv):
    mu = jnp.mean(cv, axis=-1, keepdims=True)
    xc = cv - mu
    rs = lax.rsqrt(jnp.mean(xc * xc, axis=-1, keepdims=True) + EPS)
    return xc * rs, rs


def _conf_fwd(proj_rest, cw, cb, lw, lb):
    l = proj_rest.shape[0]

    def kern(ga_ref, gb_ref, cg_ref, cw_ref, cb_ref, lw_ref, lb_ref, o_ref, cv_ref):
        a = ga_ref[...] * _sig(gb_ref[...])
        cv = _unpad_rows(_row_conv(_pad_rows(a), cw_ref[...], False)) + cb_ref[...]
        cv_ref[...] = cv
        xh, _ = _ln_stats(cv)
        ln = xh * lw_ref[...] + lb_ref[...]
        o_ref[...] = (_silu(ln) * _silu(cg_ref[...])).astype(BF16)

    vec = pl.BlockSpec((1, D), lambda i: (0, 0))
    blk = pl.BlockSpec((T, D), lambda i: (i, 0))
    return pl.pallas_call(
        kern, name="conf_fwd", grid=(l // T,),
        in_specs=[pl.BlockSpec((T, D), lambda i: (i, C_GA)), pl.BlockSpec((T, D), lambda i: (i, C_GB)),
                  pl.BlockSpec((T, D), lambda i: (i, C_CG)), pl.BlockSpec((32, D), lambda i: (0, 0)), vec, vec, vec],
        out_specs=[blk, blk],
        out_shape=[jax.ShapeDtypeStruct((l, D), BF16), jax.ShapeDtypeStruct((l, D), F32)],
        compiler_params=_cp(("parallel",)),
    )(proj_rest, proj_rest, proj_rest, cw, cb, lw, lb)


def _conf_bwd(duc, cv, proj_rest, cw, lw, lb, dpr):
    l = proj_rest.shape[0]

    def kern(du_ref, cv_ref, ga_ref, gb_ref, cg_ref, cw_ref, lw_ref, lb_ref, dpr_ref, o_ref, gcw_ref, gv_ref, sc):
        i, j = pl.program_id(0), pl.program_id(1)

        @pl.when(jnp.logical_and(i == 0, j == 0))
        def _():
            gcw_ref[...] = jnp.zeros_like(gcw_ref)
            gv_ref[...] = jnp.zeros_like(gv_ref)

        @pl.when(j == 0)
        def _():
            ga, gb, cg, cw = ga_ref[...], gb_ref[...], cg_ref[...], cw_ref[...]
            sg = _sig(gb)
            xh, rs = _ln_stats(cv_ref[...])
            ln = xh * lw_ref[...] + lb_ref[...]
            du = du_ref[...]
            sc[:, 2 * D:3 * D] = (du * _silu(ln) * _dsilu(cg)).astype(BF16)
            dln = du * _silu(cg) * _dsilu(ln)
            g_lw = jnp.sum(dln * xh, axis=0, keepdims=True)
            g_lb = jnp.sum(dln, axis=0, keepdims=True)
            dxh = dln * lw_ref[...]
            dcv = rs * (dxh - jnp.mean(dxh, axis=-1, keepdims=True) - xh * jnp.mean(dxh * xh, axis=-1, keepdims=True))
            g_cb = jnp.sum(dcv, axis=0, keepdims=True)
            rid = _iota((8, D), 0)
            gv_ref[...] += jnp.where(rid == 0, g_cb, jnp.where(rid == 1, g_lw, jnp.where(rid == 2, g_lb, 0.0)))
            dcvp = _pad_rows(dcv)
            da = _unpad_rows(_row_conv(dcvp, cw, True))
            sc[:, 0:D] = (da * sg).astype(BF16)
            sc[:, D:2 * D] = (da * ga * sg * (1.0 - sg)).astype(BF16)
            ap = _pad_rows(ga * sg)
            n = ap.shape[0]
            for k in range(CK):
                sh = ap if k == 15 else pltpu.roll(ap, (15 - k) % n, axis=0)
                gcw_ref[k:k + 1, :] += jnp.sum(dcvp * sh, axis=0, keepdims=True)

        o_ref[...] = sc[:, pl.ds(pl.multiple_of(j * D, 128), D)]

    vec = pl.BlockSpec((1, D), lambda i, j: (0, 0))
    col = lambda c: pl.BlockSpec((T, D), lambda i, j: (i, c))
    return pl.pallas_call(
        kern, name="conf_bwd", grid=(l // T, 3),
        in_specs=[col(0), col(0), col(C_GA), col(C_GB), col(C_CG), pl.BlockSpec((32, D), lambda i, j: (0, 0)), vec, vec,
                  pl.BlockSpec(memory_space=pl.ANY)],
        out_specs=[pl.BlockSpec((T, D), lambda i, j: (i, C_GA + j)), pl.BlockSpec((32, D), lambda i, j: (0, 0)),
                   pl.BlockSpec((8, D), lambda i, j: (0, 0))],
        out_shape=[jax.ShapeDtypeStruct((l, RESTW), BF16), jax.ShapeDtypeStruct((32, D), F32),
                   jax.ShapeDtypeStruct((8, D), F32)],
        scratch_shapes=[pltpu.VMEM((T, 3 * D), BF16)],
        input_output_aliases={8: 0},
        compiler_params=_cp(("arbitrary", "arbitrary")),
    )(duc, cv, proj_rest, proj_rest, proj_rest, cw, lw, lb, dpr)


def _merge_fwd(bs, bc, proj_rest):
    l = bs.shape[0]

    def kern(bs_ref, bc_ref, g1_ref, g2_ref, o_ref):
        o_ref[...] = (_sig(g1_ref[...]) * bs_ref[...] + _sig(g2_ref[...]) * bc_ref[...]).astype(BF16)

    blk = pl.BlockSpec((T, D), lambda i: (i, 0))
    return pl.pallas_call(
        kern, name="merge_fwd", grid=(l // T,),
        in_specs=[blk, blk, pl.BlockSpec((T, D), lambda i: (i, C_G1)), pl.BlockSpec((T, D), lambda i: (i, C_G2))],
        out_specs=blk, out_shape=jax.ShapeDtypeStruct((l, D), BF16),
        compiler_params=_cp(("parallel",)),
    )(bs, bc, proj_rest, proj_rest)


def _merge_bwd(dm, bs, bc, proj_rest):
    l = bs.shape[0]

    def kern(dm_ref, bs_ref, bc_ref, g1_ref, g2_ref, dbs_ref, dbc_ref, dg_ref):
        dm_ = dm_ref[...]
        s1, s2 = _sig(g1_ref[...]), _sig(g2_ref[...])
        dbs_ref[...] = (dm_ * s1).astype(BF16)
        dbc_ref[...] = (dm_ * s2).astype(BF16)
        dg_ref[:, 0:D] = (dm_ * bs_ref[...] * s1 * (1.0 - s1)).astype(BF16)
        dg_ref[:, D:2 * D] = (dm_ * bc_ref[...] * s2 * (1.0 - s2)).astype(BF16)

    blk = pl.BlockSpec((T, D), lambda i: (i, 0))
    return pl.pallas_call(
        kern, name="merge_bwd", grid=(l // T,),
        in_specs=[blk, blk, blk, pl.BlockSpec((T, D), lambda i: (i, C_G1)), pl.BlockSpec((T, D), lambda i: (i, C_G2))],
        out_specs=[blk, blk, pl.BlockSpec((T, 2 * D), lambda i: (i, 1))],
        out_shape=[jax.ShapeDtypeStruct((l, D), BF16), jax.ShapeDtypeStruct((l, D), BF16),
                   jax.ShapeDtypeStruct((l, RESTW), BF16)],
        compiler_params=_cp(("parallel",)),
    )(dm, bs, bc, proj_rest, proj_rest)


def _final(x, out, tgt, mod, fw):
    l = x.shape[0]

    def kern(x_ref, o_ref, t_ref, mod_ref, fw_ref, ls_ref, dx2_ref, do_ref, gv_ref):
        @pl.when(pl.program_id(0) == 0)
        def _():
            ls_ref[...] = jnp.zeros_like(ls_ref)
            gv_ref[...] = jnp.zeros_like(gv_ref)

        gate = mod_ref[0:1, 2 * D:3 * D]
        o = o_ref[...]
        x2 = x_ref[...] + gate * o
        r = lax.rsqrt(jnp.mean(x2 * x2, axis=-1, keepdims=True) + EPS)
        yn = x2 * r
        fw_ = fw_ref[...]
        e = yn * fw_ - t_ref[...]
        ls_ref[...] += jnp.full((8, 128), 1.0, F32) * (0.5 / D) * jnp.sum(e * e)
        dy = e * (1.0 / D)
        g_fw = jnp.sum(dy * yn, axis=0, keepdims=True)
        dyn = dy * fw_
        dx2 = r * (dyn - yn * jnp.mean(dyn * yn, axis=-1, keepdims=True))
        g_gate = jnp.sum(dx2 * o, axis=0, keepdims=True)
        rid = _iota((8, D), 0)
        gv_ref[...] += jnp.where(rid == 0, g_fw, jnp.where(rid == 1, g_gate, 0.0))
        dx2_ref[...] = dx2
        do_ref[...] = (dx2 * gate).astype(BF16)

    blk = pl.BlockSpec((T, D), lambda i: (i, 0))
    return pl.pallas_call(
        kern, name="final", grid=(l // T,),
        in_specs=[blk, blk, blk, pl.BlockSpec((8, 3 * D), lambda i: (0, 0)), pl.BlockSpec((1, D), lambda i: (0, 0))],
        out_specs=[pl.BlockSpec((8, 128), lambda i: (0, 0)), blk, blk, pl.BlockSpec((8, D), lambda i: (0, 0))],
        out_shape=[jax.ShapeDtypeStruct((8, 128), F32), jax.ShapeDtypeStruct((l, D), F32),
                   jax.ShapeDtypeStruct((l, D), BF16), jax.ShapeDtypeStruct((8, D), F32)],
        compiler_params=_cp(("arbitrary",)),
    )(x, out, tgt, mod, fw)


def _perm_dt_cols(w):
    s = w.shape[:-1]
    return w.reshape(*s, 2, NG, HPG).swapaxes(-3, -2).reshape(*s, 64)


def _unperm_dt_cols(w):
    s = w.shape[:-1]
    return w.reshape(*s, NG, 2, HPG).swapaxes(-3, -2).reshape(*s, 64)


def _pad_lanes(v, width):
    return jnp.pad(v, ((0, 0), (0, width - v.shape[1])))


def _local_step(x, c, ctx, tgt, w):
    l = x.shape[0]
    nct = CTX // T
    ncc = CTX // Q
    lext = l + CTX

    w_mod = w["w_mod"].astype(BF16)
    w_in = w["w_in"].astype(BF16)
    w_ssd = jnp.concatenate([w_in[:, :XBC], _perm_dt_cols(w_in[:, XBC:XBC + 64]), jnp.zeros((D, 64), BF16)], axis=1)
    wr = w_in[:, XBC + 64:]
    w_rest = jnp.concatenate([wr[:, :DI], wr[:, DI + 3 * D:], wr[:, DI:DI + 3 * D]], axis=1)
    w_os, w_oc, w_o = w["w_out_ssm"].astype(BF16), w["w_out_conf"].astype(BF16), w["w_out"].astype(BF16)
    cw8 = jnp.pad(w["ssm_conv_w"], ((0, 4), (0, 0)))
    cb_s = w["ssm_conv_b"].reshape(1, XBC)
    dtb = _pad_lanes(_perm_dt_cols(w["dt_bias"].reshape(1, 64)), 128)
    a_all = -jnp.exp(w["a_log"].reshape(1, 64))
    a_perm = _pad_lanes(_perm_dt_cols(a_all), 128)
    a_rows = _pad_lanes(_perm_dt_cols(a_all).reshape(NG, 8), 128)
    dsk = jnp.repeat(w["d_skip"].reshape(NH), HP).reshape(1, DI)
    gnw = w["ssm_norm_w"].reshape(1, DI)
    ccw = jnp.pad(w["conf_conv_w"], ((0, 1), (0, 0)))
    ccb, clw, clb = w["conf_conv_b"].reshape(1, D), w["conf_ln_w"].reshape(1, D), w["conf_ln_b"].reshape(1, D)
    nw = w["norm_w"].reshape(1, D)
    fw = w["final_norm_w"].reshape(1, D)
    cc = jnp.concatenate([c.reshape(1, D), w["c_ctx"].reshape(1, D), jnp.zeros((6, D), F32)], axis=0)

    bx = 512
    be = 768 if lext % 768 == 0 else 256
    tk = min(1024, l)
    mod = _mod_fwd(cc, w_mod, w["b_mod"].reshape(1, 3 * D))
    h = _norm_fwd(ctx, x, mod, nw, nct)
    hx = h[CTX:]
    proj_ssd = _mm(h, w_ssd, "nn", lext, SSDW, D, be, SSDW // 3, D, F32, "proj_ssd")
    proj_rest = _mm(hx, w_rest, "nn", l, RESTW, D, bx, 1024, D, F32, "proj_rest")
    xbc = _conv_fwd(proj_ssd, cw8, cb_s, nct)
    dtg, lag, dtt, lat = _dt_fwd(proj_ssd, dtb, a_perm)
    htf, htb = _ssd_state(xbc, dtg, lag, ncc)
    yssm = _ssd_out(xbc, dtg, lag, dtt, lat, htf, htb, ncc)
    gn = _post_fwd(yssm, xbc, proj_rest, dsk, gnw, nct)
    bs = _mm(gn, w_os, "nn", l, D, DI, bx, D, DI, F32, "out_ssm")
    uc, cv = _conf_fwd(proj_rest, ccw, ccb, clw, clb)
    bc = _mm(uc, w_oc, "nn", l, D, D, bx, D, D, F32, "out_conf")
    merged = _merge_fwd(bs, bc, proj_rest)
    out = _mm(merged, w_o, "nn", l, D, D, bx, D, D, F32, "out_proj")
    lsum, dx2, dout, gv_fin = _final(x, out, tgt, mod, fw)

    g = {}
    g["final_norm_w"] = gv_fin[0]
    dmerged = _mm(dout, w_o, "nt", l, D, D, bx, D, D, F32, "d_merged")
    g["w_out"] = _mm(merged, dout, "tn", D, D, l, D, D, tk, F32, "g_w_out")
    dbs, dbc, dpr = _merge_bwd(dmerged, bs, bc, proj_rest)
    dgn = _mm(dbs, w_os, "nt", l, DI, D, bx, DI, D, F32, "d_gn")
    g["w_out_ssm"] = _mm(gn, dbs, "tn", DI, D, l, DI, D, tk, F32, "g_w_out_ssm")
    duc = _mm(dbc, w_oc, "nt", l, D, D, bx, D, D, F32, "d_uc")
    g["w_out_conf"] = _mm(uc, dbc, "tn", D, D, l, D, D, tk, F32, "g_w_out_conf")
    dpr, gcw, gv_conf = _conf_bwd(duc, cv, proj_rest, ccw, clw, clb, dpr)
    g["conf_conv_w"] = gcw[:CK]
    g["conf_conv_b"], g["conf_ln_w"], g["conf_ln_b"] = gv_conf[0], gv_conf[1], gv_conf[2]
    dy, dproj_rest, ggnw, gdsk = _post_bwd(dgn, yssm, xbc, proj_rest, dsk, gnw, dpr, nct)
    g["ssm_norm_w"] = ggnw[0]
    g["d_skip"] = gdsk[0].reshape(NH, HP).sum(axis=1)
    dhf, dhb = _ssd_bwd_state(xbc, dy, lag, ncc)
    dxs, dbm, dcm, ddtg, galog = _ssd_bwd_out(xbc, dy, dsk, dtg, lag, dtt, lat, htf, htb, dhf, dhb, a_rows)
    g["a_log"] = _unperm_dt_cols(galog[:, 0, 0:8].reshape(1, 64)).reshape(2, NH)
    dus, gws, gbs = [], [], []
    for dpost, off, width, nm in ((dxs, 0, DI, "conv_bwd_x"), (dbm, DI, NG * NS, "conv_bwd_b"), (dcm, DI + NG * NS, NG * NS, "conv_bwd_c")):
        du_, gw_, gb_ = _conv_bwd(dpost, proj_ssd, cw8, cb_s, off, width, nct, nm)
        dus.append(du_)
        gws.append(gw_[:SK])
        gbs.append(gb_[0])
    g["ssm_conv_w"] = jnp.concatenate(gws, axis=1)
    g["ssm_conv_b"] = jnp.concatenate(gbs, axis=0)
    ddt_raw, gdtb = _dt_bwd(ddtg, proj_ssd, dtb)
    g["dt_bias"] = _unperm_dt_cols(gdtb[0:1, 0:64]).reshape(2, NH)
    dproj_ssd = jnp.concatenate(dus + [ddt_raw], axis=1)
    gw_ssd = _mm(h, dproj_ssd, "tn", D, SSDW, lext, D, SSDW // 3, be, F32, "g_w_ssd")
    gw_rest = _mm(hx, dproj_rest, "tn", D, RESTW, l, D, 1024, tk, F32, "g_w_rest")
    g["w_in"] = jnp.concatenate([gw_ssd[:, :XBC], _unperm_dt_cols(gw_ssd[:, XBC:XBC + 64]), gw_rest[:, :DI],
                                 gw_rest[:, 2 * DI:], gw_rest[:, DI:2 * DI]], axis=1)
    dh_a = _mm(dproj_ssd, w_ssd, "nt", lext, D, SSDW, T, D, SSDW, F32, "dh_ssd")
    dh_b = _mm(dproj_rest, w_rest, "nt", l, D, RESTW, T, D, RESTW, F32, "dh_rest")
    grad_x, gnw_in, dss = _norm_bwd(dh_a, dh_b, ctx, x, dx2, mod, nw, nct)
    g["norm_w"] = gnw_in[0]
    dmod = jnp.concatenate([jnp.concatenate([dss[0:1], gv_fin[1:2]], axis=1),
                            jnp.concatenate([dss[1:2], jnp.zeros((1, D), F32)], axis=1),
                            jnp.zeros((6, 3 * D), F32)], axis=0)
    gwm, gbm, gcc = _mod_bwd(dmod, cc, cc.T, w_mod)
    g["w_mod"], g["b_mod"], g["c_ctx"] = gwm, gbm[0], gcc[1]
    return lsum[0, 0], grad_x, g


NSHARD = 4
R_MOD, R_IN, R_OS, R_OC, R_O, R_SC, R_CC = 768, 2832, 512, 256, 256, 8, 8
O_MOD = 0
O_IN = O_MOD + R_MOD
O_OS = O_IN + R_IN
O_OC = O_OS + R_OS
O_O = O_OC + R_OC
O_SC = O_O + R_O
O_CC = O_SC + R_SC
PROWS = 4640
HALF = PROWS // 2
RB = 464
SROWS = 16
SHARDED = ("w_mod", "w_in", "w_out_ssm", "w_out_conf", "w_out", "ssm_conv_w", "conf_conv_w")
SMALL = (("b_mod", 3 * D), ("norm_w", D), ("ssm_conv_b", XBC), ("dt_bias", 64), ("a_log", 64), ("d_skip", NH),
         ("ssm_norm_w", DI), ("conf_conv_b", D), ("conf_ln_w", D), ("conf_ln_b", D), ("final_norm_w", D), ("c_ctx", D))
SMALL_OFF = {"b_mod": 0, "norm_w": 3 * D, "ssm_conv_b": 4 * D, "dt_bias": 8 * D, "a_log": 8 * D + 64, "d_skip": 8 * D + 128,
             "ssm_norm_w": 9 * D, "conf_conv_b": 11 * D, "conf_ln_w": 12 * D, "conf_ln_b": 13 * D, "final_norm_w": 14 * D,
             "c_ctx": 15 * D}


def _pack_shard(s):
    assert O_CC + R_CC == PROWS
    return jnp.concatenate([s["w_mod"].reshape(R_MOD, D), s["w_in"].reshape(R_IN, D), _pack_rest(s)], axis=0)


def _pack_rest(s):
    cc = jnp.pad(s["conf_conv_w"].reshape(1, CK * 256), ((0, 0), (0, R_CC * D - CK * 256))).reshape(R_CC, D)
    return jnp.concatenate([s["w_out_ssm"], s["w_out_conf"], s["w_out"],
                            jnp.pad(s["ssm_conv_w"], ((0, R_SC - SK), (0, 0))), cc], axis=0)


def _unpack_rest(p):
    o = lambda r: r - O_OS
    return {"w_out_ssm": p[o(O_OS):o(O_OC)][None], "w_out_conf": p[o(O_OC):o(O_O)][None], "w_out": p[o(O_O):o(O_SC)][None],
            "ssm_conv_w": p[o(O_SC):o(O_SC) + SK][None],
            "conf_conv_w": p[o(O_CC):o(O_CC) + R_CC].reshape(R_CC * D)[:CK * 256].reshape(1, CK, 256)}


def _pack_full(g):
    def cols(a, n):
        return a.reshape(a.shape[0], NSHARD, n).transpose(1, 0, 2)
    cc = jnp.pad(cols(g["conf_conv_w"], 256).reshape(NSHARD, CK * 256), ((0, 0), (0, R_CC * D - CK * 256)))
    return jnp.concatenate([cols(g["w_mod"], R_MOD).reshape(NSHARD, R_MOD, D), cols(g["w_in"], R_IN).reshape(NSHARD, R_IN, D),
                            g["w_out_ssm"].reshape(NSHARD, R_OS, D), g["w_out_conf"].reshape(NSHARD, R_OC, D),
                            g["w_out"].reshape(NSHARD, R_O, D),
                            jnp.pad(cols(g["ssm_conv_w"], D), ((0, 0), (0, R_SC - SK), (0, 0))),
                            cc.reshape(NSHARD, R_CC, D)], axis=1)


def _unpack_gathered(gm, gs):
    def cols(a, r, n):
        return a.reshape(NSHARD, r, n).transpose(1, 0, 2).reshape(r, NSHARD * n)
    return {"w_mod": cols(gm[:, O_MOD:O_IN], D, R_MOD), "w_in": cols(gm[:, O_IN:O_OS], D, R_IN),
            "w_out_ssm": gm[:, O_OS:O_OC].reshape(DI, D), "w_out_conf": gm[:, O_OC:O_O].reshape(D, D),
            "w_out": gm[:, O_O:O_SC].reshape(D, D), "ssm_conv_w": cols(gs[:, 0:SK], SK, D),
            "conf_conv_w": cols(gs[:, R_SC:R_SC + R_CC].reshape(NSHARD, R_CC * D)[:, :CK * 256], CK, 256)}


def _pack_small(d):
    flat = jnp.zeros((SROWS * D,), F32)
    for name, n in SMALL:
        flat = lax.dynamic_update_slice(flat, d[name].reshape(n).astype(F32), (SMALL_OFF[name],))
    return flat.reshape(SROWS, D)


def _unpack_small(p, shapes):
    flat = p.reshape(SROWS * D)
    return {name: flat[SMALL_OFF[name]:SMALL_OFF[name] + n].reshape(shapes[name]) for name, n in SMALL}


MESH_ID = pl.DeviceIdType.MESH
ANY = pl.BlockSpec(memory_space=pl.ANY)


def _place():
    x, y, c = lax.axis_index("x"), lax.axis_index("y"), lax.axis_index("c")
    return x, y, c, [(1 - x, y), (x, 1 - y), (1 - x, 1 - y)]


def _rcopy(src, dst, send, recv, dev):
    return pltpu.make_async_remote_copy(src_ref=src, dst_ref=dst, send_sem=send, recv_sem=recv,
                                        device_id=dev, device_id_type=MESH_ID)


def _gather_weights(mats, small):
    def kern(m_ref, s_ref, gm_ref, gs_ref, send, recv):
        x, y, c, chips = _place()
        me = 2 * x + y
        sib = (x, y, 1 - c)
        mine = pl.ds(pl.multiple_of(c * HALF, 16), HALF)
        other = pl.ds(pl.multiple_of((1 - c) * HALF, 16), HALF)
        first = []
        for k, (px, py) in enumerate(chips):
            first.append(_rcopy(m_ref.at[mine], gm_ref.at[me, mine], send.at[k], recv.at[k], (px, py, c)))
            first.append(_rcopy(s_ref, gs_ref.at[me], send.at[3 + k], recv.at[3 + k], (px, py, c)))
        for cp in first:
            cp.start()
        passed = []
        for k, (px, py) in enumerate(chips):
            s = 2 * px + py
            _rcopy(m_ref.at[mine], gm_ref.at[s, mine], send.at[k], recv.at[k], sib).wait_recv()
            f = _rcopy(gm_ref.at[s, mine], gm_ref.at[s, mine], send.at[6 + k], recv.at[6 + k], sib)
            f.start()
            passed.append(f)
        for k, (px, py) in enumerate(chips):
            s = 2 * px + py
            _rcopy(s_ref, gs_ref.at[s], send.at[3 + k], recv.at[3 + k], sib).wait_recv()
            _rcopy(gm_ref.at[s, other], gm_ref.at[s, other], send.at[6 + k], recv.at[6 + k], sib).wait_recv()
        for cp in first + passed:
            cp.wait_send()

    return pl.pallas_call(
        kern, name="gather_weights", in_specs=[ANY, ANY], out_specs=[ANY, ANY],
        out_shape=[jax.ShapeDtypeStruct((NSHARD, PROWS, D), BF16), jax.ShapeDtypeStruct((NSHARD, SROWS, D), F32)],
        scratch_shapes=[pltpu.SemaphoreType.DMA((9,)), pltpu.SemaphoreType.DMA((9,))],
    )(mats, small)


def _swap_halves(g):
    def kern(g_ref, o_ref, send, recv):
        x, y, c, _ = _place()
        other = pl.ds(pl.multiple_of((1 - c) * HALF, 8), HALF)
        cps = [_rcopy(g_ref.at[s, other], o_ref.at[s], send.at[s], recv.at[s], (x, y, 1 - c)) for s in range(NSHARD)]
        for cp in cps:
            cp.start()
        for cp in cps:
            cp.wait()

    return pl.pallas_call(
        kern, name="swap_halves", in_specs=[ANY], out_specs=ANY,
        out_shape=jax.ShapeDtypeStruct((NSHARD, HALF, D), F32),
        scratch_shapes=[pltpu.SemaphoreType.DMA((NSHARD,)), pltpu.SemaphoreType.DMA((NSHARD,))],
    )(g)


def _add_halves(cidx, g, ra):
    nb = HALF // RB

    def kern(c_ref, g_ref, a_ref, o_ref):
        o_ref[...] = (g_ref[...] + a_ref[...]).astype(BF16)

    return pl.pallas_call(
        kern, name="add_halves",
        grid_spec=pltpu.PrefetchScalarGridSpec(
            num_scalar_prefetch=1, grid=(NSHARD, nb),
            in_specs=[pl.BlockSpec((None, RB, D), lambda s, i, c: (s, c[0] * nb + i, 0)),
                      pl.BlockSpec((None, RB, D), lambda s, i, c: (s, i, 0))],
            out_specs=pl.BlockSpec((None, RB, D), lambda s, i, c: (s, i, 0))),
        out_shape=jax.ShapeDtypeStruct((NSHARD, HALF, D), BF16),
        compiler_params=_cp(("parallel", "parallel")),
    )(cidx, g, ra)


def _exchange_chips(p):
    def kern(p_ref, o_ref, send, recv):
        x, y, c, chips = _place()
        cps = [_rcopy(p_ref.at[2 * px + py], o_ref.at[k], send.at[k], recv.at[k], (px, py, c))
               for k, (px, py) in enumerate(chips)]
        for cp in cps:
            cp.start()
        for cp in cps:
            cp.wait()

    return pl.pallas_call(
        kern, name="exchange_chips", in_specs=[ANY], out_specs=ANY,
        out_shape=jax.ShapeDtypeStruct((3, HALF, D), p.dtype),
        scratch_shapes=[pltpu.SemaphoreType.DMA((3,)), pltpu.SemaphoreType.DMA((3,))],
    )(p)


def _add_chips(mc, g, ra, rb):
    nb = HALF // RB

    def kern(m_ref, g_ref, a_ref, r0_ref, r1_ref, r2_ref, o_ref):
        own = g_ref[...] + a_ref[...]
        o_ref[...] = ((own + r0_ref[...].astype(F32)) + r1_ref[...].astype(F32)) + r2_ref[...].astype(F32)

    return pl.pallas_call(
        kern, name="add_chips",
        grid_spec=pltpu.PrefetchScalarGridSpec(
            num_scalar_prefetch=1, grid=(nb,),
            in_specs=[pl.BlockSpec((None, RB, D), lambda i, m: (m[0], m[1] * nb + i, 0)),
                      pl.BlockSpec((None, RB, D), lambda i, m: (m[0], i, 0))]
            + [pl.BlockSpec((None, RB, D), functools.partial(lambda i, m, k: (k, i, 0), k=k)) for k in range(3)],
            out_specs=pl.BlockSpec((RB, D), lambda i, m: (i, 0))),
        out_shape=jax.ShapeDtypeStruct((HALF, D), F32),
        compiler_params=_cp(("parallel",)),
    )(mc, g, ra, rb, rb, rb)


def _share_halves(r):
    def kern(r_ref, o_ref, send, recv):
        x, y, c, _ = _place()
        cp = _rcopy(r_ref, o_ref, send, recv, (x, y, 1 - c))
        cp.start()
        cp.wait()

    return pl.pallas_call(
        kern, name="share_halves", in_specs=[ANY], out_specs=ANY,
        out_shape=jax.ShapeDtypeStruct((HALF, D), F32),
        scratch_shapes=[pltpu.SemaphoreType.DMA, pltpu.SemaphoreType.DMA],
    )(r)


def _reduce_small(s):
    def kern(s_ref, o_ref, buf, send, recv):
        x, y, c, _ = _place()
        me = 4 * x + 2 * y + c
        buf[me] = s_ref[...]
        cps = []
        for r in range(1, 8):
            peer = (1 - x if r & 4 else x, 1 - y if r & 2 else y, 1 - c if r & 1 else c)
            cps.append(_rcopy(s_ref, buf.at[me], send.at[r - 1], recv.at[r - 1], peer))
        for cp in cps:
            cp.start()
        for cp in cps:
            cp.wait()
        acc = buf[0]
        for i in range(1, 8):
            acc = acc + buf[i]
        o_ref[...] = acc

    return pl.pallas_call(
        kern, name="reduce_small",
        in_specs=[pl.BlockSpec(memory_space=pltpu.VMEM)], out_specs=pl.BlockSpec(memory_space=pltpu.VMEM),
        out_shape=jax.ShapeDtypeStruct((SROWS, D), F32),
        scratch_shapes=[pltpu.VMEM((8, SROWS, D), F32), pltpu.SemaphoreType.DMA((7,)), pltpu.SemaphoreType.DMA((7,))],
    )(s)


def _adamw(g, w, m, v, rb, name):
    rows, cols = g.shape

    def kern(g_ref, w_ref, m_ref, v_ref, d_ref, nm_ref, nv_ref):
        g_ = g_ref[...]
        m_ = ADAM_B1 * m_ref[...] + (1.0 - ADAM_B1) * g_
        v_ = ADAM_B2 * v_ref[...] + (1.0 - ADAM_B2) * jnp.square(g_)
        m_hat = m_ / (1.0 - ADAM_B1 ** ADAM_STEP)
        v_hat = v_ / (1.0 - ADAM_B2 ** ADAM_STEP)
        d_ref[...] = -ADAM_LR * (m_hat / (jnp.sqrt(v_hat) + ADAM_EPS) + ADAM_WD * w_ref[...])
        nm_ref[...] = m_
        nv_ref[...] = v_

    assert rows % rb == 0
    blk = pl.BlockSpec((rb, cols), lambda i: (i, 0))
    return pl.pallas_call(
        kern, name=name, grid=(rows // rb,), in_specs=[blk] * 4, out_specs=[blk] * 3,
        out_shape=[jax.ShapeDtypeStruct((rows, cols), F32)] * 3,
        compiler_params=_cp(("parallel",)),
    )(g, w, m, v)


WEIGHTS = ("c_ctx", "w_mod", "b_mod", "norm_w", "w_in", "ssm_conv_w", "ssm_conv_b", "dt_bias", "a_log", "d_skip",
           "ssm_norm_w", "w_out_ssm", "conf_conv_w", "conf_conv_b", "conf_ln_w", "conf_ln_b", "w_out_conf", "w_out",
           "final_norm_w")


def kernel(x, c, ctx, c_ctx, w_mod, b_mod, norm_w, w_in, ssm_conv_w, ssm_conv_b, dt_bias, a_log, d_skip, ssm_norm_w, w_out_ssm, conf_conv_w, conf_conv_b, conf_ln_w, conf_ln_b, w_out_conf, w_out, final_norm_w, loss_target, m_c_ctx, m_w_mod, m_b_mod, m_norm_w, m_w_in, m_ssm_conv_w, m_ssm_conv_b, m_dt_bias, m_a_log, m_d_skip, m_ssm_norm_w, m_w_out_ssm, m_conf_conv_w, m_conf_conv_b, m_conf_ln_w, m_conf_ln_b, m_w_out_conf, m_w_out, m_final_norm_w, v_c_ctx, v_w_mod, v_b_mod, v_norm_w, v_w_in, v_ssm_conv_w, v_ssm_conv_b, v_dt_bias, v_a_log, v_d_skip, v_ssm_norm_w, v_w_out_ssm, v_conf_conv_w, v_conf_conv_b, v_conf_ln_w, v_conf_ln_b, v_w_out_conf, v_w_out, v_final_norm_w):
    wv = (c_ctx, w_mod, b_mod, norm_w, w_in, ssm_conv_w, ssm_conv_b, dt_bias, a_log, d_skip, ssm_norm_w, w_out_ssm,
          conf_conv_w, conf_conv_b, conf_ln_w, conf_ln_b, w_out_conf, w_out, final_norm_w)
    mv = (m_c_ctx, m_w_mod, m_b_mod, m_norm_w, m_w_in, m_ssm_conv_w, m_ssm_conv_b, m_dt_bias, m_a_log, m_d_skip,
          m_ssm_norm_w, m_w_out_ssm, m_conf_conv_w, m_conf_conv_b, m_conf_ln_w, m_conf_ln_b, m_w_out_conf, m_w_out,
          m_final_norm_w)
    vv = (v_c_ctx, v_w_mod, v_b_mod, v_norm_w, v_w_in, v_ssm_conv_w, v_ssm_conv_b, v_dt_bias, v_a_log, v_d_skip,
          v_ssm_norm_w, v_w_out_ssm, v_conf_conv_w, v_conf_conv_b, v_conf_ln_w, v_conf_ln_b, v_w_out_conf, v_w_out,
          v_final_norm_w)
    shapes = {n: a.shape for n, a in zip(WEIGHTS, wv)}

    def squeeze(d):
        return {n: (a if n in ("c_ctx", "final_norm_w") else a[0]) for n, a in d.items()}

    w, m, v = (squeeze(dict(zip(WEIGHTS, t))) for t in (wv, mv, vv))

    my_chip = 2 * lax.axis_index("x") + lax.axis_index("y")
    my_core = lax.axis_index("c")

    pw = _pack_shard(w)
    pwb, psm = pw.astype(BF16), pw[O_SC:PROWS]
    gm, gs = _gather_weights(pwb, psm)
    gm = lax.dynamic_update_slice(gm, pwb[None], (my_chip, 0, 0))
    gs = lax.dynamic_update_slice(gs, psm[None], (my_chip, 0, 0))
    full = dict(w)
    full.update(_unpack_gathered(gm, gs))

    lsum, grad_x, g = _local_step(x[0], c, ctx[0], loss_target[0], full)
    loss = lax.psum(lsum, ("x", "y", "c"))

    cidx = my_core.astype(jnp.int32).reshape(1)
    mc = jnp.stack([my_chip, my_core]).astype(jnp.int32)
    gp = _pack_full(g)
    sib = _swap_halves(gp)
    part = _add_halves(cidx, gp, sib)
    red = _add_chips(mc, gp, sib, _exchange_chips(part))
    got = _share_halves(red)
    g_sh = jnp.concatenate([jnp.where(my_core == 0, red, got), jnp.where(my_core == 0, got, red)], axis=0)
    g_sm = _reduce_small(_pack_small(g))

    gr = {"w_mod": g_sh[O_MOD:O_IN].reshape(D, R_MOD), "w_in": g_sh[O_IN:O_OS].reshape(D, R_IN), "rest": g_sh[O_OS:]}
    wr, mr, vr = ({"w_mod": t["w_mod"], "w_in": t["w_in"], "rest": _pack_rest(t)} for t in (w, m, v))
    res = {k: _adamw(gr[k], wr[k], mr[k], vr[k], rb, "adamw_" + k)
           for k, rb in (("w_in", 128), ("w_mod", 512), ("rest", (PROWS - O_OS) // 2))}
    res_sm = _adamw(g_sm, _pack_small(w), _pack_small(m), _pack_small(v), SROWS, "adamw_small")

    outs = []
    for i in range(4):
        pick = (lambda k: gr[k]) if i == 0 else (lambda k: res[k][i - 1])
        d = {"w_mod": pick("w_mod")[None], "w_in": pick("w_in")[None]}
        d.update(_unpack_rest(pick("rest")))
        d.update(_unpack_small(g_sm if i == 0 else res_sm[i - 1], shapes))
        outs.extend(d[n] for n in WEIGHTS)
    return (loss, grad_x[None], *outs)
```

```python
import functools

import jax
import jax.numpy as jnp
from jax import lax
from jax.experimental import pallas as pl
from jax.experimental.pallas import tpu as pltpu

F32, BF16 = jnp.float32, jnp.bfloat16

D = 1024
DI = 2048
NH = 32
HP = 64
NG = 8
HPG = 4
NS = 128
Q = 128
GW = 64
CK = 31
SK = 4
CTX = 256
EPS = 1e-6
XBC = DI + 2 * NG * NS
SSDW = XBC + 128
RESTW = 7168
T = 256
VMEM_LIMIT = 56 * 1024 * 1024

ADAM_LR, ADAM_B1, ADAM_B2, ADAM_EPS, ADAM_WD, ADAM_STEP = 0.001, 0.9, 0.999, 1e-08, 0.01, 10


def _cp(sem):
    return pltpu.CompilerParams(dimension_semantics=sem, vmem_limit_bytes=VMEM_LIMIT)


def _sig(x):
    return jax.nn.sigmoid(x)


def _silu(x):
    return x * _sig(x)


def _dsilu(x):
    s = _sig(x)
    return s * (1.0 + x * (1.0 - s))


def _dot(a, b):
    return jnp.dot(a, b, preferred_element_type=F32)


def _dot_nt(a, b):
    return lax.dot_general(a, b, (((1,), (1,)), ((), ())), preferred_element_type=F32)


def _split3(x):
    h = x.astype(BF16)
    r = x - h.astype(F32)
    m = r.astype(BF16)
    l = (r - m.astype(F32)).astype(BF16)
    return h, m, l


def _dot3_l(sel, x):
    h, m, l = _split3(x)
    return _dot(sel, h) + _dot(sel, m) + _dot(sel, l)


def _dot3_r(x, sel):
    h, m, l = _split3(x)
    return _dot(h, sel) + _dot(m, sel) + _dot(l, sel)


def _split2(x):
    h = x.astype(BF16)
    return h, (x - h.astype(F32)).astype(BF16)


def _dot2_l(sel, x):
    h, l = _split2(x)
    return _dot(sel, h) + _dot(sel, l)


def _dot2_r(x, sel):
    h, l = _split2(x)
    return _dot(h, sel) + _dot(l, sel)


def _iota(shape, dim):
    return lax.broadcasted_iota(jnp.int32, shape, dim)


def _mm(a, b, dims, m, n, k, bm, bn, bk, out_dtype, name):
    nk = k // bk
    assert m % bm == 0 and n % bn == 0 and k % bk == 0, (name, m, n, k, bm, bn, bk)

    def prod(a_ref, b_ref):
        av = a_ref[...].astype(BF16)
        bv = b_ref[...].astype(BF16)
        if dims == "nn":
            return _dot(av, bv)
        if dims == "nt":
            return _dot_nt(av, bv)
        return lax.dot_general(av, bv, (((0,), (0,)), ((), ())), preferred_element_type=F32)

    def kern_one(a_ref, b_ref, o_ref):
        o_ref[...] = prod(a_ref, b_ref).astype(out_dtype)

    def kern_acc(a_ref, b_ref, o_ref, acc):
        kk = pl.program_id(2)

        @pl.when(kk == 0)
        def _():
            acc[...] = jnp.zeros_like(acc)

        acc[...] += prod(a_ref, b_ref)

        @pl.when(kk == nk - 1)
        def _():
            o_ref[...] = acc[...].astype(out_dtype)

    if dims == "nn":
        a_spec = pl.BlockSpec((bm, bk), lambda j, i, kk: (i, kk))
        b_spec = pl.BlockSpec((bk, bn), lambda j, i, kk: (kk, j))
    elif dims == "nt":
        a_spec = pl.BlockSpec((bm, bk), lambda j, i, kk: (i, kk))
        b_spec = pl.BlockSpec((bn, bk), lambda j, i, kk: (j, kk))
    else:
        a_spec = pl.BlockSpec((bk, bm), lambda j, i, kk: (kk, i))
        b_spec = pl.BlockSpec((bk, bn), lambda j, i, kk: (kk, j))
    return pl.pallas_call(
        kern_one if nk == 1 else kern_acc, name=name,
        grid=(n // bn, m // bm, nk),
        in_specs=[a_spec, b_spec],
        out_specs=pl.BlockSpec((bm, bn), lambda j, i, kk: (i, j)),
        out_shape=jax.ShapeDtypeStruct((m, n), out_dtype),
        scratch_shapes=[] if nk == 1 else [pltpu.VMEM((bm, bn), F32)],
        compiler_params=_cp(("parallel", "parallel", "arbitrary")),
    )(a, b)


def _mod_fwd(cc, w_mod, b_mod):
    def kern(cc_ref, w_ref, b_ref, o_ref):
        s = _silu(cc_ref[...]).astype(BF16)
        o_ref[...] = _dot(s, w_ref[...]) + b_ref[...]

    return pl.pallas_call(
        kern, name="mod_fwd", grid=(3,),
        in_specs=[pl.BlockSpec((8, D), lambda j: (0, 0)), pl.BlockSpec((D, D), lambda j: (0, j)),
                  pl.BlockSpec((1, D), lambda j: (0, j))],
        out_specs=pl.BlockSpec((8, D), lambda j: (0, j)),
        out_shape=jax.ShapeDtypeStruct((8, 3 * D), F32),
        compiler_params=_cp(("parallel",)),
    )(cc, w_mod, b_mod)


def _mod_bwd(dmod, cc, cct, w_mod):
    def kern(dm_ref, cc_ref, cct_ref, w_ref, gw_ref, gb_ref, gc_ref):
        kk = pl.program_id(0)
        dm = dm_ref[...]
        sct = _silu(cct_ref[...])
        gw_ref[...] = sct[:, 0:1] * dm[0:1, :] + sct[:, 1:2] * dm[1:2, :]
        gb_ref[...] = jnp.broadcast_to(dm[0:1, :] + dm[1:2, :], dm.shape)

        @pl.when(kk == 0)
        def _():
            gc_ref[...] = jnp.zeros_like(gc_ref)

        gc_ref[...] += _dot_nt(dm.astype(BF16), w_ref[...])

        @pl.when(kk == 2)
        def _():
            gc_ref[...] = gc_ref[...] * _dsilu(cc_ref[...])

    return pl.pallas_call(
        kern, name="mod_bwd", grid=(3,),
        in_specs=[pl.BlockSpec((8, D), lambda j: (0, j)), pl.BlockSpec((8, D), lambda j: (0, 0)),
                  pl.BlockSpec((D, 8), lambda j: (0, 0)), pl.BlockSpec((D, D), lambda j: (0, j))],
        out_specs=[pl.BlockSpec((D, D), lambda j: (0, j)), pl.BlockSpec((8, D), lambda j: (0, j)),
                   pl.BlockSpec((8, D), lambda j: (0, 0))],
        out_shape=[jax.ShapeDtypeStruct((D, 3 * D), F32), jax.ShapeDtypeStruct((8, 3 * D), F32),
                   jax.ShapeDtypeStruct((8, D), F32)],
        compiler_params=_cp(("arbitrary",)),
    )(dmod, cc, cct, w_mod)


def _ext_specs(nct):
    return (pl.BlockSpec((T, D), lambda i: (jnp.minimum(i, nct - 1), 0)),
            pl.BlockSpec((T, D), lambda i: (jnp.maximum(i - nct, 0), 0)))


def _norm_fwd(ctx, xl, mod, nw, nct):
    lext = ctx.shape[0] + xl.shape[0]

    def kern(c_ref, x_ref, mod_ref, nw_ref, h_ref):
        is_ctx = pl.program_id(0) < nct
        x = jnp.where(is_ctx, c_ref[...], x_ref[...])
        r = lax.rsqrt(jnp.mean(x * x, axis=-1, keepdims=True) + EPS)
        xn = x * r * nw_ref[...]
        shift = jnp.where(is_ctx, mod_ref[1:2, 0:D], mod_ref[0:1, 0:D])
        scale = jnp.where(is_ctx, mod_ref[1:2, D:2 * D], mod_ref[0:1, D:2 * D])
        h_ref[...] = (xn * (1.0 + scale) + shift).astype(BF16)

    return pl.pallas_call(
        kern, name="norm_fwd", grid=(lext // T,),
        in_specs=[*_ext_specs(nct), pl.BlockSpec((8, 3 * D), lambda i: (0, 0)),
                  pl.BlockSpec((1, D), lambda i: (0, 0))],
        out_specs=pl.BlockSpec((T, D), lambda i: (i, 0)),
        out_shape=jax.ShapeDtypeStruct((lext, D), BF16),
        compiler_params=_cp(("parallel",)),
    )(ctx, xl, mod, nw)


def _norm_bwd(dha, dhb, ctx, xl, dx2, mod, nw, nct):
    lext = ctx.shape[0] + xl.shape[0]
    ntl = lext // T

    def kern(dha_ref, dhb_ref, c_ref, x_ref, dx2_ref, mod_ref, nw_ref, gx_ref, gnw_ref, dss_ref):
        i = pl.program_id(0)
        is_ctx = i < nct

        @pl.when(i == 0)
        def _():
            gnw_ref[...] = jnp.zeros_like(gnw_ref)
            dss_ref[...] = jnp.zeros_like(dss_ref)

        x = jnp.where(is_ctx, c_ref[...], x_ref[...])
        dh_ = dha_ref[...] + jnp.where(is_ctx, 0.0, dhb_ref[...])
        nw_ = nw_ref[...]
        r = lax.rsqrt(jnp.mean(x * x, axis=-1, keepdims=True) + EPS)
        xn = x * r
        scale = jnp.where(is_ctx, mod_ref[1:2, D:2 * D], mod_ref[0:1, D:2 * D])
        dsh = jnp.sum(dh_, axis=0, keepdims=True)
        dsc = jnp.sum(dh_ * (xn * nw_), axis=0, keepdims=True)
        row = jnp.concatenate([dsh, dsc], axis=1)
        rid = _iota((8, 2 * D), 0)
        dss_ref[...] += jnp.where(rid == jnp.where(is_ctx, 1, 0), row, 0.0)
        dxnw = dh_ * (1.0 + scale)
        gnw_ref[...] += jnp.broadcast_to(jnp.sum(dxnw * xn, axis=0, keepdims=True), (8, D))
        dxn = dxnw * nw_
        dx = r * (dxn - xn * jnp.mean(dxn * xn, axis=-1, keepdims=True))
        gx_ref[...] = dx2_ref[...] + dx

    return pl.pallas_call(
        kern, name="norm_bwd", grid=(ntl,),
        in_specs=[pl.BlockSpec((T, D), lambda i: (i, 0)), pl.BlockSpec((T, D), lambda i: (jnp.maximum(i - nct, 0), 0)),
                  *_ext_specs(nct),
                  pl.BlockSpec((T, D), lambda i: (jnp.maximum(i - nct, 0), 0)),
                  pl.BlockSpec((8, 3 * D), lambda i: (0, 0)), pl.BlockSpec((1, D), lambda i: (0, 0))],
        out_specs=[pl.BlockSpec((T, D), lambda i: (jnp.maximum(i - nct, 0), 0)),
                   pl.BlockSpec((8, D), lambda i: (0, 0)), pl.BlockSpec((8, 2 * D), lambda i: (0, 0))],
        out_shape=[jax.ShapeDtypeStruct((lext - nct * T, D), F32), jax.ShapeDtypeStruct((8, D), F32),
                   jax.ShapeDtypeStruct((8, 2 * D), F32)],
        compiler_params=_cp(("arbitrary",)),
    )(dha, dhb, ctx, xl, dx2, mod, nw)


CB = 1024


def _halo_specs(width_blk, col_off_blocks, ntl):
    t8 = T // 8
    main = pl.BlockSpec((T, width_blk), lambda j, i: (i, j + col_off_blocks))
    prev = pl.BlockSpec((8, width_blk), lambda j, i: (jnp.maximum(i * t8 - 1, 0), j + col_off_blocks))
    nxt = pl.BlockSpec((8, width_blk), lambda j, i: (jnp.minimum((i + 1) * t8, ntl * t8 - 1), j + col_off_blocks))
    return main, prev, nxt


def _seq_edges(i, nct, ntl):
    starts = jnp.logical_or(i == 0, i == nct)
    ends = jnp.logical_or(i == nct - 1, i == ntl - 1)
    return starts, ends


def _shifted(ext, off):
    n = ext.shape[0]
    return pltpu.roll(ext, (-off) % n, axis=0)[8:8 + T]


def _conv_fwd(proj_ssd, cw, cb, nct):
    lext = proj_ssd.shape[0]
    ntl = lext // T

    def kern(u_ref, up_ref, un_ref, w_ref, b_ref, o_ref):
        i = pl.program_id(1)
        starts, ends = _seq_edges(i, nct, ntl)
        up = jnp.where(starts, 0.0, up_ref[...])
        un = jnp.where(ends, 0.0, un_ref[...])
        ext = jnp.concatenate([up, u_ref[...], un], axis=0)
        w = w_ref[...]
        pre = b_ref[...] + w[0:1] * _shifted(ext, -2) + w[1:2] * _shifted(ext, -1) \
            + w[2:3] * u_ref[...] + w[3:4] * _shifted(ext, 1)
        o_ref[...] = _silu(pre)

    main, prev, nxt = _halo_specs(CB, 0, ntl)
    return pl.pallas_call(
        kern, name="conv_fwd", grid=(XBC // CB, ntl),
        in_specs=[main, prev, nxt, pl.BlockSpec((8, CB), lambda j, i: (0, j)), pl.BlockSpec((1, CB), lambda j, i: (0, j))],
        out_specs=pl.BlockSpec((T, CB), lambda j, i: (i, j)),
        out_shape=jax.ShapeDtypeStruct((lext, XBC), F32),
        compiler_params=_cp(("parallel", "parallel")),
    )(proj_ssd, proj_ssd, proj_ssd, cw, cb)


def _conv_bwd(dpost, proj_ssd, cw, cb, col_off, width, nct, name):
    lext = proj_ssd.shape[0]
    ntl = lext // T
    cob = col_off // CB

    def kern(u_ref, up_ref, un_ref, d_ref, dp_ref, dn_ref, w_ref, b_ref, du_ref, gw_ref, gb_ref):
        i = pl.program_id(1)

        @pl.when(i == 0)
        def _():
            gw_ref[...] = jnp.zeros_like(gw_ref)
            gb_ref[...] = jnp.zeros_like(gb_ref)

        starts, ends = _seq_edges(i, nct, ntl)
        ext = jnp.concatenate([jnp.where(starts, 0.0, up_ref[...]), u_ref[...], jnp.where(ends, 0.0, un_ref[...])], axis=0)
        dext = jnp.concatenate([jnp.where(starts, 0.0, dp_ref[...]), d_ref[...], jnp.where(ends, 0.0, dn_ref[...])], axis=0)
        w = w_ref[...]
        n = ext.shape[0]
        pre = b_ref[...] + w[0:1] * pltpu.roll(ext, 2, axis=0) + w[1:2] * pltpu.roll(ext, 1, axis=0) \
            + w[2:3] * ext + w[3:4] * pltpu.roll(ext, n - 1, axis=0)
        dpre = dext * _dsilu(pre)
        dm = dpre[8:8 + T]
        du = w[0:1] * _shifted(dpre, 2) + w[1:2] * _shifted(dpre, 1) + w[2:3] * dm + w[3:4] * _shifted(dpre, -1)
        du_ref[...] = du.astype(BF16)
        g0 = jnp.sum(dm * _shifted(ext, -2), axis=0, keepdims=True)
        g1 = jnp.sum(dm * _shifted(ext, -1), axis=0, keepdims=True)
        g2 = jnp.sum(dm * u_ref[...], axis=0, keepdims=True)
        g3 = jnp.sum(dm * _shifted(ext, 1), axis=0, keepdims=True)
        rid = _iota((8, CB), 0)
        gw_ref[...] += jnp.where(rid == 0, g0, jnp.where(rid == 1, g1, jnp.where(rid == 2, g2, jnp.where(rid == 3, g3, 0.0))))
        gb_ref[...] += jnp.broadcast_to(jnp.sum(dm, axis=0, keepdims=True), (8, CB))

    main, prev, nxt = _halo_specs(CB, cob, ntl)
    dmain, dprev, dnxt = _halo_specs(CB, 0, ntl)
    return pl.pallas_call(
        kern, name=name, grid=(width // CB, ntl),
        in_specs=[main, prev, nxt, dmain, dprev, dnxt,
                  pl.BlockSpec((8, CB), lambda j, i: (0, j + cob)), pl.BlockSpec((1, CB), lambda j, i: (0, j + cob))],
        out_specs=[pl.BlockSpec((T, CB), lambda j, i: (i, j)), pl.BlockSpec((8, CB), lambda j, i: (0, j)),
                   pl.BlockSpec((8, CB), lambda j, i: (0, j))],
        out_shape=[jax.ShapeDtypeStruct((lext, width), BF16), jax.ShapeDtypeStruct((8, width), F32),
                   jax.ShapeDtypeStruct((8, width), F32)],
        compiler_params=_cp(("parallel", "arbitrary")),
    )(proj_ssd, proj_ssd, proj_ssd, dpost, dpost, dpost, cw, cb)


def _tri(lower):
    r, c = _iota((Q, Q), 0), _iota((Q, Q), 1)
    return jnp.where((c <= r) if lower else (c >= r), 1.0, 0.0).astype(BF16)


def _is_bdir_lane(shape):
    ln = _iota(shape, len(shape) - 1)
    return jnp.logical_and(((ln >> 2) & 1) == 1, ln < 64)


def _dt_fwd(proj_ssd, dtb, av):
    lext = proj_ssd.shape[0]

    def kern(p_ref, b_ref, a_ref, dtg_ref, lag_ref, dtt_ref, lat_ref):
        lane = _iota((T, 128), 1)
        raw = p_ref[...] + b_ref[...]
        dt = jnp.where(lane < 64, jnp.maximum(raw, 0.0) + jnp.log1p(jnp.exp(-jnp.abs(raw))), 0.0)
        dta = dt * a_ref[...]
        tl, tu = _tri(True), _tri(False)
        isb = _is_bdir_lane((Q, 128))
        las = []
        for qq in range(T // Q):
            blk = dta[qq * Q:(qq + 1) * Q]
            las.append(jnp.where(isb, _dot3_l(tu, blk), _dot3_l(tl, blk)))
        la = jnp.concatenate(las, axis=0)
        for g in range(NG):
            sh = (128 - 8 * g) % 128
            dtg_ref[g] = jnp.where(lane < 8, pltpu.roll(dt, sh, axis=1) if sh else dt, 0.0)
            lag_ref[g] = jnp.where(lane < 8, pltpu.roll(la, sh, axis=1) if sh else la, 0.0)
        dtt_ref[...] = dt.T[0:64]
        lat_ref[...] = la.T[0:64]

    return pl.pallas_call(
        kern, name="dt_fwd", grid=(lext // T,),
        in_specs=[pl.BlockSpec((T, 128), lambda i: (i, XBC // 128)), pl.BlockSpec((1, 128), lambda i: (0, 0)),
                  pl.BlockSpec((1, 128), lambda i: (0, 0))],
        out_specs=[pl.BlockSpec((NG, T, 128), lambda i: (0, i, 0)), pl.BlockSpec((NG, T, 128), lambda i: (0, i, 0)),
                   pl.BlockSpec((64, T), lambda i: (0, i)), pl.BlockSpec((64, T), lambda i: (0, i))],
        out_shape=[jax.ShapeDtypeStruct((NG, lext, 128), F32), jax.ShapeDtypeStruct((NG, lext, 128), F32),
                   jax.ShapeDtypeStruct((64, lext), F32), jax.ShapeDtypeStruct((64, lext), F32)],
        compiler_params=_cp(("parallel",)),
    )(proj_ssd, dtb, av)


def _dt_bwd(ddtg, proj_ssd, dtb):
    lext = proj_ssd.shape[0]

    def kern(d_ref, p_ref, b_ref, o_ref, gb_ref):
        @pl.when(pl.program_id(0) == 0)
        def _():
            gb_ref[...] = jnp.zeros_like(gb_ref)

        acc = d_ref[0]
        for g in range(1, NG):
            acc = acc + pltpu.roll(d_ref[g], 8 * g, axis=1)
        draw = acc * _sig(p_ref[...] + b_ref[...])
        o_ref[...] = draw.astype(BF16)
        gb_ref[...] += jnp.broadcast_to(jnp.sum(draw, axis=0, keepdims=True), (8, 128))

    return pl.pallas_call(
        kern, name="dt_bwd", grid=(lext // T,),
        in_specs=[pl.BlockSpec((NG, T, 128), lambda i: (0, i, 0)), pl.BlockSpec((T, 128), lambda i: (i, XBC // 128)),
                  pl.BlockSpec((1, 128), lambda i: (0, 0))],
        out_specs=[pl.BlockSpec((T, 128), lambda i: (i, 0)), pl.BlockSpec((8, 128), lambda i: (0, 0))],
        out_shape=[jax.ShapeDtypeStruct((lext, 128), BF16), jax.ShapeDtypeStruct((8, 128), F32)],
        compiler_params=_cp(("arbitrary",)),
    )(ddtg, proj_ssd, dtb)


def _expand_sel(d):
    r, c = _iota((128, 256), 0), _iota((128, 256), 1)
    return jnp.where(r == 4 * d + (c >> 6), 1.0, 0.0).astype(BF16)


def _reduce_sel(d):
    r, c = _iota((256, 128), 0), _iota((256, 128), 1)
    return jnp.where(c == 4 * d + (r >> 6), 1.0, 0.0).astype(BF16)


def _chunk_of_bwd_dir(j, ncc, nc):
    return jnp.where(j < ncc, ncc - 1 - j, nc + ncc - 1 - j)


def _dir_terms(la, dt, d):
    lane = _iota(la.shape, 1)
    mine = jnp.logical_and(lane >= 4 * d, lane < 4 * d + 4)
    la = jnp.where(mine, la, 0.0)
    tot = la[Q - 1:Q] if d == 0 else la[0:1]
    wnd = jnp.exp(tot - la)
    return tot, wnd * jnp.where(mine, dt, 0.0), wnd


def _ssd_state(xbc, dtg, lag, ncc):
    lext = xbc.shape[0]
    nc = lext // Q

    def kern(xf_ref, bf_ref, dtf_ref, laf_ref, xb_ref, bb_ref, dtb_ref, lab_ref, hf_ref, hb_ref, sf, sb):
        @pl.when(pl.program_id(0) == 0)
        def _():
            sf[...] = jnp.zeros_like(sf)
            sb[...] = jnp.zeros_like(sb)

        for d, (x_ref, b_ref, dt_ref, la_ref, h_ref, s) in enumerate(
                ((xf_ref, bf_ref, dtf_ref, laf_ref, hf_ref, sf), (xb_ref, bb_ref, dtb_ref, lab_ref, hb_ref, sb))):
            h_ref[...] = s[...]
            ex = _expand_sel(d)
            for g in range(NG):
                cols = slice(256 * g, 256 * (g + 1))
                tot, w_end, _ = _dir_terms(la_ref[g], dt_ref[g], d)
                wexp = _dot2_r(w_end, ex)
                dexp = _dot2_r(jnp.broadcast_to(jnp.exp(tot), (8, 128)), ex)[0:1]
                xw = (x_ref[:, cols] * wexp).astype(BF16)
                s[:, cols] = s[:, cols] * dexp + _dot(b_ref[:, 128 * g:128 * (g + 1)].T.astype(BF16), xw)

    cb = functools.partial(_chunk_of_bwd_dir, ncc=ncc, nc=nc)
    sm = lambda f: pl.BlockSpec((NG, Q, 128), lambda j: (0, f(j), 0))
    one = lambda j: j
    return pl.pallas_call(
        kern, name="ssd_state", grid=(nc,),
        in_specs=[pl.BlockSpec((Q, DI), lambda j: (j, 0)), pl.BlockSpec((Q, NG * NS), lambda j: (j, 2)), sm(one), sm(one),
                  pl.BlockSpec((Q, DI), lambda j: (cb(j), 0)), pl.BlockSpec((Q, NG * NS), lambda j: (cb(j), 2)), sm(cb), sm(cb)],
        out_specs=[pl.BlockSpec((None, 128, DI), lambda j: (j, 0, 0)),
                   pl.BlockSpec((None, 128, DI), lambda j: (cb(j), 0, 0))],
        out_shape=[jax.ShapeDtypeStruct((nc, 128, DI), F32), jax.ShapeDtypeStruct((nc, 128, DI), F32)],
        scratch_shapes=[pltpu.VMEM((128, DI), F32), pltpu.VMEM((128, DI), F32)],
        compiler_params=_cp(("arbitrary",)),
    )(xbc, xbc, dtg, lag, xbc, xbc, dtg, lag)


def _ssd_out(xbc, dtg, lag, dtt, lat, htf, htb, ncc):
    lext = xbc.shape[0]
    nc = lext // Q
    ncx = nc - ncc

    gps = 4
    li, si = (lambda: _iota((Q, Q), 0)), (lambda: _iota((Q, Q), 1))

    def kern(x_ref, b_ref, c_ref, dtg_ref, lag_ref, dtt_ref, lat_ref, hf_ref, hb_ref, y_ref):
        lane = _iota((Q, 256), 1)
        masks = (li() >= si(), li() <= si())
        for gg in range(gps):
            cols = slice(256 * gg, 256 * (gg + 1))
            cm = c_ref[:, 128 * gg:128 * (gg + 1)]
            xb_ = x_ref[:, cols].astype(BF16)
            s_ = _dot_nt(cm.astype(BF16), b_ref[:, 128 * gg:128 * (gg + 1)].astype(BF16))
            la, dtt_, lat_ = lag_ref[gg], dtt_ref[8 * gg:8 * (gg + 1)], lat_ref[8 * gg:8 * (gg + 1)]
            elam = jnp.exp(la)
            y = jnp.zeros((Q, 256), F32)
            for d, h_ref in enumerate((hf_ref, hb_ref)):
                rhs = jnp.concatenate([xb_, h_ref[:, cols].astype(BF16)], axis=0)
                for r in range(HPG):
                    j = 4 * d + r
                    lm = jnp.where(masks[d], jnp.exp(la[:, j:j + 1] - lat_[j:j + 1, :]), 0.0)
                    w = s_ * lm * dtt_[j:j + 1, :]
                    lhs = jnp.concatenate([w, cm * elam[:, j:j + 1]], axis=1).astype(BF16)
                    y = y + jnp.where((lane >> 6) == r, _dot(lhs, rhs), 0.0)
            y_ref[:, cols] = y

    nb = NG // gps
    sm = pl.BlockSpec((gps, Q, 128), lambda c, g: (g, c + ncc, 0))
    smt = pl.BlockSpec((8 * gps, Q), lambda c, g: (g, c + ncc))
    st3 = pl.BlockSpec((None, 128, 256 * gps), lambda c, g: (c + ncc, 0, g))
    return pl.pallas_call(
        kern, name="ssd_out", grid=(ncx, nb),
        in_specs=[pl.BlockSpec((Q, 256 * gps), lambda c, g: (c + ncc, g)),
                  pl.BlockSpec((Q, 128 * gps), lambda c, g: (c + ncc, 2 * nb + g)),
                  pl.BlockSpec((Q, 128 * gps), lambda c, g: (c + ncc, 3 * nb + g)), sm, sm, smt, smt, st3, st3],
        out_specs=pl.BlockSpec((Q, 256 * gps), lambda c, g: (c, g)),
        out_shape=jax.ShapeDtypeStruct((ncx * Q, DI), F32),
        compiler_params=_cp(("parallel", "parallel")),
    )(xbc, xbc, xbc, dtg, lag, dtt, lat, htf, htb)


def _ssd_bwd_state(xbc, dy, lag, ncc):
    lext = xbc.shape[0]
    nc = lext // Q

    def kern(cf_ref, dyf_ref, laf_ref, cb_ref, dyb_ref, lab_ref, df_ref, db_ref, sf, sb):
        @pl.when(pl.program_id(0) == 0)
        def _():
            sf[...] = jnp.zeros_like(sf)
            sb[...] = jnp.zeros_like(sb)

        for d, (c_ref, dy_ref, la_ref, o_ref, s) in enumerate(
                ((cf_ref, dyf_ref, laf_ref, df_ref, sf), (cb_ref, dyb_ref, lab_ref, db_ref, sb))):
            o_ref[...] = s[...]
            ex = _expand_sel(d)
            for g in range(NG):
                cols = slice(256 * g, 256 * (g + 1))
                la = la_ref[g]
                tot = la[Q - 1:Q] if d == 0 else la[0:1]
                eexp = _dot2_r(jnp.exp(la), ex)
                dexp = _dot2_r(jnp.broadcast_to(jnp.exp(tot), (8, 128)), ex)[0:1]
                dye = (dy_ref[:, cols] * eexp).astype(BF16)
                s[:, cols] = s[:, cols] * dexp + _dot(c_ref[:, 128 * g:128 * (g + 1)].T.astype(BF16), dye)

    cf = lambda j: nc - 1 - j
    cb = lambda j: _chunk_of_bwd_dir(nc - 1 - j, ncc, nc)
    sm = lambda f: pl.BlockSpec((NG, Q, 128), lambda j: (0, f(j), 0))
    return pl.pallas_call(
        kern, name="ssd_bwd_state", grid=(nc,),
        in_specs=[pl.BlockSpec((Q, NG * NS), lambda j: (cf(j), 3)), pl.BlockSpec((Q, DI), lambda j: (cf(j), 0)), sm(cf),
                  pl.BlockSpec((Q, NG * NS), lambda j: (cb(j), 3)), pl.BlockSpec((Q, DI), lambda j: (cb(j), 0)), sm(cb)],
        out_specs=[pl.BlockSpec((None, 128, DI), lambda j: (cf(j), 0, 0)),
                   pl.BlockSpec((None, 128, DI), lambda j: (cb(j), 0, 0))],
        out_shape=[jax.ShapeDtypeStruct((nc, 128, DI), F32), jax.ShapeDtypeStruct((nc, 128, DI), F32)],
        scratch_shapes=[pltpu.VMEM((128, DI), F32), pltpu.VMEM((128, DI), F32)],
        compiler_params=_cp(("arbitrary",)),
    )(xbc, dy, lag, xbc, dy, lag)


def _ssd_bwd_out(xbc, dy, dsk, dtg, lag, dtt, lat, htf, htb, dhf, dhb, a_rows):
    lext = xbc.shape[0]
    nc = lext // Q

    gps = 4

    def kern(x_ref, b_ref, c_ref, dy_ref, sk_ref, dtg_ref, lag_ref, dtt_ref, lat_ref, hf_ref, hb_ref, df_ref, db_ref,
             a_ref, dx_ref, dbo_ref, dco_ref, ddt_ref, ga_ref):
        @pl.when(pl.program_id(1) == 0)
        def _():
            ga_ref[...] = jnp.zeros_like(ga_ref)

        for gg in range(gps):
            one_group(gg, x_ref, b_ref, c_ref, dy_ref, sk_ref, dtg_ref, lag_ref, dtt_ref, lat_ref, hf_ref, hb_ref, df_ref,
                      db_ref, a_ref, dx_ref, dbo_ref, dco_ref, ddt_ref, ga_ref)

    def one_group(gg, x_ref, b_ref, c_ref, dy_ref, sk_ref, dtg_ref, lag_ref, dtt_ref, lat_ref, hf_ref, hb_ref, df_ref,
                  db_ref, a_ref, dx_ref, dbo_ref, dco_ref, ddt_ref, ga_ref):
        g = pl.program_id(0) * gps + gg
        cols, cols128 = slice(256 * gg, 256 * (gg + 1)), slice(128 * gg, 128 * (gg + 1))
        x, bm, cm, dy_ = x_ref[:, cols], b_ref[:, cols128], c_ref[:, cols128], dy_ref[:, cols]
        xb_, bb_, cb_, dyb_ = x.astype(BF16), bm.astype(BF16), cm.astype(BF16), dy_.astype(BF16)
        st = _dot_nt(bb_, cb_)
        si, li = _iota((Q, Q), 0), _iota((Q, Q), 1)
        lane = _iota((Q, 256), 1)
        lane128 = _iota((Q, 128), 1)
        row128 = _iota((Q, 128), 0)
        sub = _iota((128, Q), 0)
        la, dt = lag_ref[gg], dtg_ref[gg]
        dtt_, lat_ = dtt_ref[8 * gg:8 * (gg + 1)], lat_ref[8 * gg:8 * (gg + 1)]
        elam = jnp.exp(la)
        dst = jnp.zeros((Q, Q), F32)
        dxa = jnp.zeros((Q, 256), F32)
        dba = jnp.zeros((Q, 128), F32)
        dca = jnp.zeros((Q, 128), F32)
        dlam = jnp.zeros((Q, 128), F32)
        ddir = jnp.zeros((Q, 128), F32)
        rows = jnp.zeros((128, Q), F32)
        for d, (h_ref, dh_ref) in enumerate(((hf_ref, df_ref), (hb_ref, db_ref))):
            ht, dht = h_ref[:, cols], dh_ref[:, cols]
            htb_, dhtb_ = ht.astype(BF16), dht.astype(BF16)
            tot, w_end, wnd = _dir_terms(la, dt, d)
            ex, rs = _expand_sel(d), _reduce_sel(d)
            elx = _dot2_r(elam, ex)
            wex = _dot2_r(w_end, ex)
            dye = dy_ * elx
            ch = _dot(cb_, htb_)
            bd = _dot(bb_, dhtb_)
            dca = dca + _dot_nt(dye.astype(BF16), htb_)
            dba = dba + _dot_nt((x * wex).astype(BF16), dhtb_)
            dlam = dlam + _dot2_r(dye * ch, rs)
            xbd = _dot2_r(x * bd, rs)
            e_ = w_end * xbd
            dlam = dlam - e_
            ddir = ddir + wnd * xbd
            hh = _dot2_r(jnp.broadcast_to(jnp.sum(dht * ht, axis=0, keepdims=True), (8, 256)), rs)[0:1]
            tot_term = jnp.sum(e_, axis=0, keepdims=True) + jnp.exp(tot) * hh
            dlam = dlam + jnp.where(row128 == (Q - 1 if d == 0 else 0), tot_term, 0.0)
            rhs = jnp.concatenate([dyb_, dhtb_], axis=0)
            maskt = (li >= si) if d == 0 else (li <= si)
            for r in range(HPG):
                j = 4 * d + r
                dc = dt[:, j:j + 1]
                lmt = jnp.where(maskt, jnp.exp(lat_[j:j + 1, :] - la[:, j:j + 1]), 0.0)
                wt = st * lmt * dc
                lhs = jnp.concatenate([wt, bm * w_end[:, j:j + 1]], axis=1).astype(BF16)
                hm = (lane >> 6) == r
                dxa = dxa + jnp.where(hm, _dot(lhs, rhs), 0.0)
                dwt = _dot_nt(jnp.where(hm, x, 0.0).astype(BF16), dyb_)
                dl = dwt * lmt
                gpt = dl * st
                cs = jnp.sum(gpt, axis=1, keepdims=True)
                ddir = ddir + jnp.where(lane128 == j, cs, 0.0)
                dlam = dlam - jnp.where(lane128 == j, cs * dc, 0.0)
                rows = rows + jnp.where(sub == j, jnp.sum(gpt * dc, axis=0, keepdims=True), 0.0)
                dst = dst + dl * dc
        dlam = dlam + rows.T
        dba = dba + _dot(dst.astype(BF16), cb_)
        dca = dca + _dot(dst.T.astype(BF16), bb_)
        isb = jnp.logical_and(lane128 >= 4, lane128 < 8)
        ddel = jnp.where(isb, _dot2_l(_tri(True), dlam), _dot2_l(_tri(False), dlam))
        a_l = a_ref[pl.ds(g, 1), :]
        ddt_ref[gg] = ddir + a_l * ddel
        ga_ref[gg] += jnp.broadcast_to(a_l * jnp.sum(dt * ddel, axis=0, keepdims=True), (8, 128))
        dx_ref[:, cols] = dxa + dy_ * sk_ref[:, cols]
        dbo_ref[:, cols128] = dba
        dco_ref[:, cols128] = dca

    nb = NG // gps
    st3 = pl.BlockSpec((None, 128, 256 * gps), lambda g, c: (c, 0, g))
    sm = pl.BlockSpec((gps, Q, 128), lambda g, c: (g, c, 0))
    smt = pl.BlockSpec((8 * gps, Q), lambda g, c: (g, c))
    wide = pl.BlockSpec((Q, 256 * gps), lambda g, c: (c, g))
    return pl.pallas_call(
        kern, name="ssd_bwd_out", grid=(nb, nc),
        in_specs=[wide, pl.BlockSpec((Q, 128 * gps), lambda g, c: (c, 2 * nb + g)),
                  pl.BlockSpec((Q, 128 * gps), lambda g, c: (c, 3 * nb + g)), wide,
                  pl.BlockSpec((1, 256 * gps), lambda g, c: (0, g)), sm, sm, smt, smt, st3, st3, st3, st3,
                  pl.BlockSpec((8, 128), lambda g, c: (0, 0))],
        out_specs=[wide, pl.BlockSpec((Q, 128 * gps), lambda g, c: (c, g)),
                   pl.BlockSpec((Q, 128 * gps), lambda g, c: (c, g)), sm, pl.BlockSpec((gps, 8, 128), lambda g, c: (g, 0, 0))],
        out_shape=[jax.ShapeDtypeStruct((lext, DI), F32), jax.ShapeDtypeStruct((lext, NG * NS), F32),
                   jax.ShapeDtypeStruct((lext, NG * NS), F32), jax.ShapeDtypeStruct((NG, lext, 128), F32),
                   jax.ShapeDtypeStruct((NG, 8, 128), F32)],
        compiler_params=_cp(("parallel", "arbitrary")),
    )(xbc, xbc, xbc, dy, dsk, dtg, lag, dtt, lat, htf, htb, dhf, dhb, a_rows)


def _post_fwd(yssm, xbc, proj_rest, dsk, gnw, nct):
    l = yssm.shape[0]

    def kern(y_ref, x_ref, z_ref, dsk_ref, w_ref, o_ref):
        y = y_ref[...] + dsk_ref[...] * x_ref[...]
        yz = y * _silu(z_ref[...])
        for g in range(NG):
            sl = slice(256 * g, 256 * (g + 1))
            v = yz[:, sl]
            r = lax.rsqrt(jnp.mean(v * v, axis=-1, keepdims=True) + EPS)
            o_ref[:, sl] = (v * r * w_ref[:, sl]).astype(BF16)

    return pl.pallas_call(
        kern, name="post_fwd", grid=(l // T,),
        in_specs=[pl.BlockSpec((T, DI), lambda i: (i, 0)), pl.BlockSpec((T, DI), lambda i: (i + nct, 0)),
                  pl.BlockSpec((T, DI), lambda i: (i, 0)), pl.BlockSpec((1, DI), lambda i: (0, 0)),
                  pl.BlockSpec((1, DI), lambda i: (0, 0))],
        out_specs=pl.BlockSpec((T, DI), lambda i: (i, 0)),
        out_shape=jax.ShapeDtypeStruct((l, DI), BF16),
        compiler_params=_cp(("parallel",)),
    )(yssm, xbc, proj_rest, dsk, gnw)


def _post_bwd(dgn, yssm, xbc, proj_rest, dsk, gnw, dpr, nct):
    l = yssm.shape[0]
    lext = xbc.shape[0]
    xi = lambda i: (jnp.maximum(i - nct, 0), 0)

    def kern(dg_ref, y_ref, x_ref, z_ref, dsk_ref, w_ref, dpr_ref, dy_ref, dz_ref, gw_ref, gd_ref):
        i = pl.program_id(0)

        @pl.when(i == 0)
        def _():
            gw_ref[...] = jnp.zeros_like(gw_ref)
            gd_ref[...] = jnp.zeros_like(gd_ref)

        @pl.when(i < nct)
        def _():
            dy_ref[...] = jnp.zeros_like(dy_ref)

        @pl.when(i >= nct)
        def _():
            xs = x_ref[...]
            z = z_ref[...]
            y = y_ref[...] + dsk_ref[...] * xs
            sz = _silu(z)
            yz = y * sz
            dgn_ = dg_ref[...]
            dyz_parts = []
            gws = []
            for g in range(NG):
                sl = slice(256 * g, 256 * (g + 1))
                v = yz[:, sl]
                r = lax.rsqrt(jnp.mean(v * v, axis=-1, keepdims=True) + EPS)
                vn = v * r
                dn = dgn_[:, sl] * w_ref[:, sl]
                gws.append(jnp.sum(dgn_[:, sl] * vn, axis=0, keepdims=True))
                dyz_parts.append(r * (dn - vn * jnp.mean(dn * vn, axis=-1, keepdims=True)))
            dyz = jnp.concatenate(dyz_parts, axis=1)
            gw_ref[...] += jnp.broadcast_to(jnp.concatenate(gws, axis=1), (8, DI))
            dy = dyz * sz
            dz_ref[...] = (dyz * y * _dsilu(z)).astype(BF16)
            gd_ref[...] += jnp.broadcast_to(jnp.sum(dy * xs, axis=0, keepdims=True), (8, DI))
            dy_ref[...] = dy

    return pl.pallas_call(
        kern, name="post_bwd", grid=(lext // T,),
        in_specs=[pl.BlockSpec((T, DI), xi), pl.BlockSpec((T, DI), xi), pl.BlockSpec((T, DI), lambda i: (i, 0)),
                  pl.BlockSpec((T, DI), xi), pl.BlockSpec((1, DI), lambda i: (0, 0)), pl.BlockSpec((1, DI), lambda i: (0, 0)),
                  pl.BlockSpec(memory_space=pl.ANY)],
        out_specs=[pl.BlockSpec((T, DI), lambda i: (i, 0)),
                   pl.BlockSpec((T, DI), xi), pl.BlockSpec((8, DI), lambda i: (0, 0)), pl.BlockSpec((8, DI), lambda i: (0, 0))],
        out_shape=[jax.ShapeDtypeStruct((lext, DI), F32),
                   jax.ShapeDtypeStruct((l, RESTW), BF16), jax.ShapeDtypeStruct((8, DI), F32), jax.ShapeDtypeStruct((8, DI), F32)],
        input_output_aliases={6: 1},
        compiler_params=_cp(("arbitrary",)),
    )(dgn, yssm, xbc, proj_rest, dsk, gnw, dpr)


C_G1, C_G2, C_GA, C_GB, C_CG = 2, 3, 4, 5, 6
PITCH = GW + 16
NROW = T // GW


GAP = PITCH - GW
PADR = GAP + NROW * PITCH
NSTRIP = D // 128


def _fill_padded(pad, val):
    z = jnp.zeros((GAP, D), F32)
    pad[0:GAP] = z
    for r in range(NROW):
        pad[GAP + PITCH * r:GAP + PITCH * r + GW] = val[GW * r:GW * (r + 1)]
        pad[PITCH * (r + 1):PITCH * (r + 1) + GAP] = z


def _row_conv(out_ref, pad, w_ref, transpose):
    def strip(s, carry):
        ln = pl.ds(pl.multiple_of(s * 128, 128), 128)
        for r in range(NROW):
            base = GAP + PITCH * r
            acc = jnp.zeros((GW, 128), F32)
            for k in range(CK):
                off = (k - 15) if not transpose else (15 - k)
                acc = acc + w_ref[pl.ds(k, 1), ln] * pad[pl.ds(base + off, GW), ln]
            out_ref[pl.ds(GW * r, GW), ln] = acc
        return carry

    lax.fori_loop(0, NSTRIP, strip, 0)


def _row_conv_wgrad(gcw_ref, padd, pada):
    def strip(s, carry):
        ln = pl.ds(pl.multiple_of(s * 128, 128), 128)
        accs = [jnp.zeros((8, 128), F32) for _ in range(CK)]
        for r in range(NROW):
            base = GAP + PITCH * r
            d = padd[pl.ds(base, GW), ln]
            for k in range(CK):
                p = d * pada[pl.ds(base + k - 15, GW), ln]
                part = p[0:8]
                for q in range(1, GW // 8):
                    part = part + p[8 * q:8 * (q + 1)]
                accs[k] = accs[k] + part
        rid = _iota((32, 128), 0)
        g = jnp.zeros((32, 128), F32)
        for k in range(CK):
            g = jnp.where(rid == k, jnp.sum(accs[k], axis=0, keepdims=True), g)
        gcw_ref[:, ln] += g
        return carry

    lax.fori_loop(0, NSTRIP, strip, 0)


def _ln_stats(cv):
    mu = jnp.mean(cv, axis=-1, keepdims=True)
    xc = cv - mu
    rs = lax.rsqrt(jnp.mean(xc * xc, axis=-1, keepdims=True) + EPS)
    return xc * rs, rs


def _conf_fwd(proj_rest, cw, cb, lw, lb):
    l = proj_rest.shape[0]

    def kern(ga_ref, gb_ref, cg_ref, cw_ref, cb_ref, lw_ref, lb_ref, o_ref, cv_ref, pad):
        _fill_padded(pad, ga_ref[...] * _sig(gb_ref[...]))
        _row_conv(cv_ref, pad, cw_ref, False)
        cv = cv_ref[...] + cb_ref[...]
        cv_ref[...] = cv
        xh, _ = _ln_stats(cv)
        ln = xh * lw_ref[...] + lb_ref[...]
        o_ref[...] = (_silu(ln) * _silu(cg_ref[...])).astype(BF16)

    vec = pl.BlockSpec((1, D), lambda i: (0, 0))
    blk = pl.BlockSpec((T, D), lambda i: (i, 0))
    return pl.pallas_call(
        kern, name="conf_fwd", grid=(l // T,),
        in_specs=[pl.BlockSpec((T, D), lambda i: (i, C_GA)), pl.BlockSpec((T, D), lambda i: (i, C_GB)),
                  pl.BlockSpec((T, D), lambda i: (i, C_CG)), pl.BlockSpec((32, D), lambda i: (0, 0)), vec, vec, vec],
        out_specs=[blk, blk],
        out_shape=[jax.ShapeDtypeStruct((l, D), BF16), jax.ShapeDtypeStruct((l, D), F32)],
        scratch_shapes=[pltpu.VMEM((PADR, D), F32)],
        compiler_params=_cp(("parallel",)),
    )(proj_rest, proj_rest, proj_rest, cw, cb, lw, lb)


def _conf_bwd(duc, cv, proj_rest, cw, lw, lb, dpr):
    l = proj_rest.shape[0]

    def kern(du_ref, cv_ref, ga_ref, gb_ref, cg_ref, cw_ref, lw_ref, lb_ref, dpr_ref, o_ref, gcw_ref, gv_ref, sc,
             pada, padd, da_ref):
        i, j = pl.program_id(0), pl.program_id(1)

        @pl.when(jnp.logical_and(i == 0, j == 0))
        def _():
            gcw_ref[...] = jnp.zeros_like(gcw_ref)
            gv_ref[...] = jnp.zeros_like(gv_ref)

        @pl.when(j == 0)
        def _():
            ga, gb, cg = ga_ref[...], gb_ref[...], cg_ref[...]
            sg = _sig(gb)
            xh, rs = _ln_stats(cv_ref[...])
            ln = xh * lw_ref[...] + lb_ref[...]
            du = du_ref[...]
            sc[:, 2 * D:3 * D] = (du * _silu(ln) * _dsilu(cg)).astype(BF16)
            dln = du * _silu(cg) * _dsilu(ln)
            g_lw = jnp.sum(dln * xh, axis=0, keepdims=True)
            g_lb = jnp.sum(dln, axis=0, keepdims=True)
            dxh = dln * lw_ref[...]
            dcv = rs * (dxh - jnp.mean(dxh, axis=-1, keepdims=True) - xh * jnp.mean(dxh * xh, axis=-1, keepdims=True))
            g_cb = jnp.sum(dcv, axis=0, keepdims=True)
            rid = _iota((8, D), 0)
            gv_ref[...] += jnp.where(rid == 0, g_cb, jnp.where(rid == 1, g_lw, jnp.where(rid == 2, g_lb, 0.0)))
            _fill_padded(padd, dcv)
            _fill_padded(pada, ga * sg)
            _row_conv(da_ref, padd, cw_ref, True)
            _row_conv_wgrad(gcw_ref, padd, pada)
            da = da_ref[...]
            sc[:, 0:D] = (da * sg).astype(BF16)
            sc[:, D:2 * D] = (da * ga * sg * (1.0 - sg)).astype(BF16)

        o_ref[...] = sc[:, pl.ds(pl.multiple_of(j * D, 128), D)]

    vec = pl.BlockSpec((1, D), lambda i, j: (0, 0))
    col = lambda c: pl.BlockSpec((T, D), lambda i, j: (i, c))
    return pl.pallas_call(
        kern, name="conf_bwd", grid=(l // T, 3),
        in_specs=[col(0), col(0), col(C_GA), col(C_GB), col(C_CG), pl.BlockSpec((32, D), lambda i, j: (0, 0)), vec, vec,
                  pl.BlockSpec(memory_space=pl.ANY)],
        out_specs=[pl.BlockSpec((T, D), lambda i, j: (i, C_GA + j)), pl.BlockSpec((32, D), lambda i, j: (0, 0)),
                   pl.BlockSpec((8, D), lambda i, j: (0, 0))],
        out_shape=[jax.ShapeDtypeStruct((l, RESTW), BF16), jax.ShapeDtypeStruct((32, D), F32),
                   jax.ShapeDtypeStruct((8, D), F32)],
        scratch_shapes=[pltpu.VMEM((T, 3 * D), BF16), pltpu.VMEM((PADR, D), F32), pltpu.VMEM((PADR, D), F32),
                        pltpu.VMEM((T, D), F32)],
        input_output_aliases={8: 0},
        compiler_params=_cp(("arbitrary", "arbitrary")),
    )(duc, cv, proj_rest, proj_rest, proj_rest, cw, lw, lb, dpr)


def _merge_fwd(bs, bc, proj_rest):
    l = bs.shape[0]

    def kern(bs_ref, bc_ref, g1_ref, g2_ref, o_ref):
        o_ref[...] = (_sig(g1_ref[...]) * bs_ref[...] + _sig(g2_ref[...]) * bc_ref[...]).astype(BF16)

    blk = pl.BlockSpec((T, D), lambda i: (i, 0))
    return pl.pallas_call(
        kern, name="merge_fwd", grid=(l // T,),
        in_specs=[blk, blk, pl.BlockSpec((T, D), lambda i: (i, C_G1)), pl.BlockSpec((T, D), lambda i: (i, C_G2))],
        out_specs=blk, out_shape=jax.ShapeDtypeStruct((l, D), BF16),
        compiler_params=_cp(("parallel",)),
    )(bs, bc, proj_rest, proj_rest)


def _merge_bwd(dm, bs, bc, proj_rest):
    l = bs.shape[0]

    def kern(dm_ref, bs_ref, bc_ref, g1_ref, g2_ref, dbs_ref, dbc_ref, dg_ref):
        dm_ = dm_ref[...]
        s1, s2 = _sig(g1_ref[...]), _sig(g2_ref[...])
        dbs_ref[...] = (dm_ * s1).astype(BF16)
        dbc_ref[...] = (dm_ * s2).astype(BF16)
        dg_ref[:, 0:D] = (dm_ * bs_ref[...] * s1 * (1.0 - s1)).astype(BF16)
        dg_ref[:, D:2 * D] = (dm_ * bc_ref[...] * s2 * (1.0 - s2)).astype(BF16)

    blk = pl.BlockSpec((T, D), lambda i: (i, 0))
    return pl.pallas_call(
        kern, name="merge_bwd", grid=(l // T,),
        in_specs=[blk, blk, blk, pl.BlockSpec((T, D), lambda i: (i, C_G1)), pl.BlockSpec((T, D), lambda i: (i, C_G2))],
        out_specs=[blk, blk, pl.BlockSpec((T, 2 * D), lambda i: (i, 1))],
        out_shape=[jax.ShapeDtypeStruct((l, D), BF16), jax.ShapeDtypeStruct((l, D), BF16),
                   jax.ShapeDtypeStruct((l, RESTW), BF16)],
        compiler_params=_cp(("parallel",)),
    )(dm, bs, bc, proj_rest, proj_rest)


def _final(x, out, tgt, mod, fw):
    l = x.shape[0]

    def kern(x_ref, o_ref, t_ref, mod_ref, fw_ref, ls_ref, dx2_ref, do_ref, gv_ref):
        @pl.when(pl.program_id(0) == 0)
        def _():
            ls_ref[...] = jnp.zeros_like(ls_ref)
            gv_ref[...] = jnp.zeros_like(gv_ref)

        gate = mod_ref[0:1, 2 * D:3 * D]
        o = o_ref[...]
        x2 = x_ref[...] + gate * o
        r = lax.rsqrt(jnp.mean(x2 * x2, axis=-1, keepdims=True) + EPS)
        yn = x2 * r
        fw_ = fw_ref[...]
        e = yn * fw_ - t_ref[...]
        ls_ref[...] += jnp.full((8, 128), 1.0, F32) * (0.5 / D) * jnp.sum(e * e)
        dy = e * (1.0 / D)
        g_fw = jnp.sum(dy * yn, axis=0, keepdims=True)
        dyn = dy * fw_
        dx2 = r * (dyn - yn * jnp.mean(dyn * yn, axis=-1, keepdims=True))
        g_gate = jnp.sum(dx2 * o, axis=0, keepdims=True)
        rid = _iota((8, D), 0)
        gv_ref[...] += jnp.where(rid == 0, g_fw, jnp.where(rid == 1, g_gate, 0.0))
        dx2_ref[...] = dx2
        do_ref[...] = (dx2 * gate).astype(BF16)

    blk = pl.BlockSpec((T, D), lambda i: (i, 0))
    return pl.pallas_call(
        kern, name="final", grid=(l // T,),
        in_specs=[blk, blk, blk, pl.BlockSpec((8, 3 * D), lambda i: (0, 0)), pl.BlockSpec((1, D), lambda i: (0, 0))],
        out_specs=[pl.BlockSpec((8, 128), lambda i: (0, 0)), blk, blk, pl.BlockSpec((8, D), lambda i: (0, 0))],
        out_shape=[jax.ShapeDtypeStruct((8, 128), F32), jax.ShapeDtypeStruct((l, D), F32),
                   jax.ShapeDtypeStruct((l, D), BF16), jax.ShapeDtypeStruct((8, D), F32)],
        compiler_params=_cp(("arbitrary",)),
    )(x, out, tgt, mod, fw)


def _perm_dt_cols(w):
    s = w.shape[:-1]
    return w.reshape(*s, 2, NG, HPG).swapaxes(-3, -2).reshape(*s, 64)


def _unperm_dt_cols(w):
    s = w.shape[:-1]
    return w.reshape(*s, NG, 2, HPG).swapaxes(-3, -2).reshape(*s, 64)


def _pad_lanes(v, width):
    return jnp.pad(v, ((0, 0), (0, width - v.shape[1])))


def _local_step(x, c, ctx, tgt, w):
    l = x.shape[0]
    nct = CTX // T
    ncc = CTX // Q
    lext = l + CTX

    w_mod = w["w_mod"].astype(BF16)
    w_in = w["w_in"].astype(BF16)
    w_ssd = jnp.concatenate([w_in[:, :XBC], _perm_dt_cols(w_in[:, XBC:XBC + 64]), jnp.zeros((D, 64), BF16)], axis=1)
    wr = w_in[:, XBC + 64:]
    w_rest = jnp.concatenate([wr[:, :DI], wr[:, DI + 3 * D:], wr[:, DI:DI + 3 * D]], axis=1)
    w_os, w_oc, w_o = w["w_out_ssm"].astype(BF16), w["w_out_conf"].astype(BF16), w["w_out"].astype(BF16)
    cw8 = jnp.pad(w["ssm_conv_w"], ((0, 4), (0, 0)))
    cb_s = w["ssm_conv_b"].reshape(1, XBC)
    dtb = _pad_lanes(_perm_dt_cols(w["dt_bias"].reshape(1, 64)), 128)
    a_all = -jnp.exp(w["a_log"].reshape(1, 64))
    a_perm = _pad_lanes(_perm_dt_cols(a_all), 128)
    a_rows = _pad_lanes(_perm_dt_cols(a_all).reshape(NG, 8), 128)
    dsk = jnp.repeat(w["d_skip"].reshape(NH), HP).reshape(1, DI)
    gnw = w["ssm_norm_w"].reshape(1, DI)
    ccw = jnp.pad(w["conf_conv_w"], ((0, 1), (0, 0)))
    ccb, clw, clb = w["conf_conv_b"].reshape(1, D), w["conf_ln_w"].reshape(1, D), w["conf_ln_b"].reshape(1, D)
    nw = w["norm_w"].reshape(1, D)
    fw = w["final_norm_w"].reshape(1, D)
    cc = jnp.concatenate([c.reshape(1, D), w["c_ctx"].reshape(1, D), jnp.zeros((6, D), F32)], axis=0)

    bx = 512
    be = 768 if lext % 768 == 0 else 256
    tk = min(1024, l)
    mod = _mod_fwd(cc, w_mod, w["b_mod"].reshape(1, 3 * D))
    h = _norm_fwd(ctx, x, mod, nw, nct)
    hx = h[CTX:]
    proj_ssd = _mm(h, w_ssd, "nn", lext, SSDW, D, be, SSDW // 3, D, F32, "proj_ssd")
    proj_rest = _mm(hx, w_rest, "nn", l, RESTW, D, bx, 1024, D, F32, "proj_rest")
    xbc = _conv_fwd(proj_ssd, cw8, cb_s, nct)
    dtg, lag, dtt, lat = _dt_fwd(proj_ssd, dtb, a_perm)
    htf, htb = _ssd_state(xbc, dtg, lag, ncc)
    yssm = _ssd_out(xbc, dtg, lag, dtt, lat, htf, htb, ncc)
    gn = _post_fwd(yssm, xbc, proj_rest, dsk, gnw, nct)
    bs = _mm(gn, w_os, "nn", l, D, DI, bx, D, DI, F32, "out_ssm")
    uc, cv = _conf_fwd(proj_rest, ccw, ccb, clw, clb)
    bc = _mm(uc, w_oc, "nn", l, D, D, bx, D, D, F32, "out_conf")
    merged = _merge_fwd(bs, bc, proj_rest)
    out = _mm(merged, w_o, "nn", l, D, D, bx, D, D, F32, "out_proj")
    lsum, dx2, dout, gv_fin = _final(x, out, tgt, mod, fw)

    g = {}
    g["final_norm_w"] = gv_fin[0]
    dmerged = _mm(dout, w_o, "nt", l, D, D, bx, D, D, F32, "d_merged")
    g["w_out"] = _mm(merged, dout, "tn", D, D, l, D, D, tk, F32, "g_w_out")
    dbs, dbc, dpr = _merge_bwd(dmerged, bs, bc, proj_rest)
    dgn = _mm(dbs, w_os, "nt", l, DI, D, bx, DI, D, F32, "d_gn")
    g["w_out_ssm"] = _mm(gn, dbs, "tn", DI, D, l, DI, D, tk, F32, "g_w_out_ssm")
    duc = _mm(dbc, w_oc, "nt", l, D, D, bx, D, D, F32, "d_uc")
    g["w_out_conf"] = _mm(uc, dbc, "tn", D, D, l, D, D, tk, F32, "g_w_out_conf")
    dpr, gcw, gv_conf = _conf_bwd(duc, cv, proj_rest, ccw, clw, clb, dpr)
    g["conf_conv_w"] = gcw[:CK]
    g["conf_conv_b"], g["conf_ln_w"], g["conf_ln_b"] = gv_conf[0], gv_conf[1], gv_conf[2]
    dy, dproj_rest, ggnw, gdsk = _post_bwd(dgn, yssm, xbc, proj_rest, dsk, gnw, dpr, nct)
    g["ssm_norm_w"] = ggnw[0]
    g["d_skip"] = gdsk[0].reshape(NH, HP).sum(axis=1)
    dhf, dhb = _ssd_bwd_state(xbc, dy, lag, ncc)
    dxs, dbm, dcm, ddtg, galog = _ssd_bwd_out(xbc, dy, dsk, dtg, lag, dtt, lat, htf, htb, dhf, dhb, a_rows)
    g["a_log"] = _unperm_dt_cols(galog[:, 0, 0:8].reshape(1, 64)).reshape(2, NH)
    dus, gws, gbs = [], [], []
    for dpost, off, width, nm in ((dxs, 0, DI, "conv_bwd_x"), (dbm, DI, NG * NS, "conv_bwd_b"), (dcm, DI + NG * NS, NG * NS, "conv_bwd_c")):
        du_, gw_, gb_ = _conv_bwd(dpost, proj_ssd, cw8, cb_s, off, width, nct, nm)
        dus.append(du_)
        gws.append(gw_[:SK])
        gbs.append(gb_[0])
    g["ssm_conv_w"] = jnp.concatenate(gws, axis=1)
    g["ssm_conv_b"] = jnp.concatenate(gbs, axis=0)
    ddt_raw, gdtb = _dt_bwd(ddtg, proj_ssd, dtb)
    g["dt_bias"] = _unperm_dt_cols(gdtb[0:1, 0:64]).reshape(2, NH)
    dproj_ssd = jnp.concatenate(dus + [ddt_raw], axis=1)
    gw_ssd = _mm(h, dproj_ssd, "tn", D, SSDW, lext, D, SSDW // 3, be, F32, "g_w_ssd")
    gw_rest = _mm(hx, dproj_rest, "tn", D, RESTW, l, D, 1024, tk, F32, "g_w_rest")
    g["w_in"] = jnp.concatenate([gw_ssd[:, :XBC], _unperm_dt_cols(gw_ssd[:, XBC:XBC + 64]), gw_rest[:, :DI],
                                 gw_rest[:, 2 * DI:], gw_rest[:, DI:2 * DI]], axis=1)
    dh_a = _mm(dproj_ssd, w_ssd, "nt", lext, D, SSDW, T, D, SSDW, F32, "dh_ssd")
    dh_b = _mm(dproj_rest, w_rest, "nt", l, D, RESTW, T, D, RESTW, F32, "dh_rest")
    grad_x, gnw_in, dss = _norm_bwd(dh_a, dh_b, ctx, x, dx2, mod, nw, nct)
    g["norm_w"] = gnw_in[0]
    dmod = jnp.concatenate([jnp.concatenate([dss[0:1], gv_fin[1:2]], axis=1),
                            jnp.concatenate([dss[1:2], jnp.zeros((1, D), F32)], axis=1),
                            jnp.zeros((6, 3 * D), F32)], axis=0)
    gwm, gbm, gcc = _mod_bwd(dmod, cc, cc.T, w_mod)
    g["w_mod"], g["b_mod"], g["c_ctx"] = gwm, gbm[0], gcc[1]
    return lsum[0, 0], grad_x, g


NSHARD = 4
R_MOD, R_IN, R_OS, R_OC, R_O, R_SC, R_CC = 768, 2832, 512, 256, 256, 8, 8
O_MOD = 0
O_IN = O_MOD + R_MOD
O_OS = O_IN + R_IN
O_OC = O_OS + R_OS
O_O = O_OC + R_OC
O_SC = O_O + R_O
O_CC = O_SC + R_SC
PROWS = 4640
HALF = PROWS // 2
RB = 464
SROWS = 16
SHARDED = ("w_mod", "w_in", "w_out_ssm", "w_out_conf", "w_out", "ssm_conv_w", "conf_conv_w")
SMALL = (("b_mod", 3 * D), ("norm_w", D), ("ssm_conv_b", XBC), ("dt_bias", 64), ("a_log", 64), ("d_skip", NH),
         ("ssm_norm_w", DI), ("conf_conv_b", D), ("conf_ln_w", D), ("conf_ln_b", D), ("final_norm_w", D), ("c_ctx", D))
SMALL_OFF = {"b_mod": 0, "norm_w": 3 * D, "ssm_conv_b": 4 * D, "dt_bias": 8 * D, "a_log": 8 * D + 64, "d_skip": 8 * D + 128,
             "ssm_norm_w": 9 * D, "conf_conv_b": 11 * D, "conf_ln_w": 12 * D, "conf_ln_b": 13 * D, "final_norm_w": 14 * D,
             "c_ctx": 15 * D}


def _pack_shard(s):
    assert O_CC + R_CC == PROWS
    return jnp.concatenate([s["w_mod"].reshape(R_MOD, D), s["w_in"].reshape(R_IN, D), _pack_rest(s)], axis=0)


def _pack_rest(s):
    cc = jnp.pad(s["conf_conv_w"].reshape(1, CK * 256), ((0, 0), (0, R_CC * D - CK * 256))).reshape(R_CC, D)
    return jnp.concatenate([s["w_out_ssm"], s["w_out_conf"], s["w_out"],
                            jnp.pad(s["ssm_conv_w"], ((0, R_SC - SK), (0, 0))), cc], axis=0)


def _unpack_rest(p):
    o = lambda r: r - O_OS
    return {"w_out_ssm": p[o(O_OS):o(O_OC)][None], "w_out_conf": p[o(O_OC):o(O_O)][None], "w_out": p[o(O_O):o(O_SC)][None],
            "ssm_conv_w": p[o(O_SC):o(O_SC) + SK][None],
            "conf_conv_w": p[o(O_CC):o(O_CC) + R_CC].reshape(R_CC * D)[:CK * 256].reshape(1, CK, 256)}


def _pack_full(g):
    def cols(a, n):
        return a.reshape(a.shape[0], NSHARD, n).transpose(1, 0, 2)
    cc = jnp.pad(cols(g["conf_conv_w"], 256).reshape(NSHARD, CK * 256), ((0, 0), (0, R_CC * D - CK * 256)))
    return jnp.concatenate([cols(g["w_mod"], R_MOD).reshape(NSHARD, R_MOD, D), cols(g["w_in"], R_IN).reshape(NSHARD, R_IN, D),
                            g["w_out_ssm"].reshape(NSHARD, R_OS, D), g["w_out_conf"].reshape(NSHARD, R_OC, D),
                            g["w_out"].reshape(NSHARD, R_O, D),
                            jnp.pad(cols(g["ssm_conv_w"], D), ((0, 0), (0, R_SC - SK), (0, 0))),
                            cc.reshape(NSHARD, R_CC, D)], axis=1)


def _unpack_gathered(gm, gs):
    def cols(a, r, n):
        return a.reshape(NSHARD, r, n).transpose(1, 0, 2).reshape(r, NSHARD * n)
    return {"w_mod": cols(gm[:, O_MOD:O_IN], D, R_MOD), "w_in": cols(gm[:, O_IN:O_OS], D, R_IN),
            "w_out_ssm": gm[:, O_OS:O_OC].reshape(DI, D), "w_out_conf": gm[:, O_OC:O_O].reshape(D, D),
            "w_out": gm[:, O_O:O_SC].reshape(D, D), "ssm_conv_w": cols(gs[:, 0:SK], SK, D),
            "conf_conv_w": cols(gs[:, R_SC:R_SC + R_CC].reshape(NSHARD, R_CC * D)[:, :CK * 256], CK, 256)}


def _pack_small(d):
    flat = jnp.zeros((SROWS * D,), F32)
    for name, n in SMALL:
        flat = lax.dynamic_update_slice(flat, d[name].reshape(n).astype(F32), (SMALL_OFF[name],))
    return flat.reshape(SROWS, D)


def _unpack_small(p, shapes):
    flat = p.reshape(SROWS * D)
    return {name: flat[SMALL_OFF[name]:SMALL_OFF[name] + n].reshape(shapes[name]) for name, n in SMALL}


MESH_ID = pl.DeviceIdType.MESH
ANY = pl.BlockSpec(memory_space=pl.ANY)


def _place():
    x, y, c = lax.axis_index("x"), lax.axis_index("y"), lax.axis_index("c")
    return x, y, c, [(1 - x, y), (x, 1 - y), (1 - x, 1 - y)]


def _rcopy(src, dst, send, recv, dev):
    return pltpu.make_async_remote_copy(src_ref=src, dst_ref=dst, send_sem=send, recv_sem=recv,
                                        device_id=dev, device_id_type=MESH_ID)


def _gather_weights(mats, small):
    def kern(m_ref, s_ref, gm_ref, gs_ref, send, recv):
        x, y, c, chips = _place()
        me = 2 * x + y
        sib = (x, y, 1 - c)
        mine = pl.ds(pl.multiple_of(c * HALF, 16), HALF)
        other = pl.ds(pl.multiple_of((1 - c) * HALF, 16), HALF)
        first = []
        for k, (px, py) in enumerate(chips):
            first.append(_rcopy(m_ref.at[mine], gm_ref.at[me, mine], send.at[k], recv.at[k], (px, py, c)))
            first.append(_rcopy(s_ref, gs_ref.at[me], send.at[3 + k], recv.at[3 + k], (px, py, c)))
        for cp in first:
            cp.start()
        passed = []
        for k, (px, py) in enumerate(chips):
            s = 2 * px + py
            _rcopy(m_ref.at[mine], gm_ref.at[s, mine], send.at[k], recv.at[k], sib).wait_recv()
            f = _rcopy(gm_ref.at[s, mine], gm_ref.at[s, mine], send.at[6 + k], recv.at[6 + k], sib)
            f.start()
            passed.append(f)
        for k, (px, py) in enumerate(chips):
            s = 2 * px + py
            _rcopy(s_ref, gs_ref.at[s], send.at[3 + k], recv.at[3 + k], sib).wait_recv()
            _rcopy(gm_ref.at[s, other], gm_ref.at[s, other], send.at[6 + k], recv.at[6 + k], sib).wait_recv()
        for cp in first + passed:
            cp.wait_send()

    return pl.pallas_call(
        kern, name="gather_weights", in_specs=[ANY, ANY], out_specs=[ANY, ANY],
        out_shape=[jax.ShapeDtypeStruct((NSHARD, PROWS, D), BF16), jax.ShapeDtypeStruct((NSHARD, SROWS, D), F32)],
        scratch_shapes=[pltpu.SemaphoreType.DMA((9,)), pltpu.SemaphoreType.DMA((9,))],
    )(mats, small)


def _swap_halves(g):
    def kern(g_ref, o_ref, send, recv):
        x, y, c, _ = _place()
        other = pl.ds(pl.multiple_of((1 - c) * HALF, 8), HALF)
        cps = [_rcopy(g_ref.at[s, other], o_ref.at[s], send.at[s], recv.at[s], (x, y, 1 - c)) for s in range(NSHARD)]
        for cp in cps:
            cp.start()
        for cp in cps:
            cp.wait()

    return pl.pallas_call(
        kern, name="swap_halves", in_specs=[ANY], out_specs=ANY,
        out_shape=jax.ShapeDtypeStruct((NSHARD, HALF, D), F32),
        scratch_shapes=[pltpu.SemaphoreType.DMA((NSHARD,)), pltpu.SemaphoreType.DMA((NSHARD,))],
    )(g)


def _add_halves(cidx, g, ra):
    nb = HALF // RB

    def kern(c_ref, g_ref, a_ref, o_ref):
        o_ref[...] = (g_ref[...] + a_ref[...]).astype(BF16)

    return pl.pallas_call(
        kern, name="add_halves",
        grid_spec=pltpu.PrefetchScalarGridSpec(
            num_scalar_prefetch=1, grid=(NSHARD, nb),
            in_specs=[pl.BlockSpec((None, RB, D), lambda s, i, c: (s, c[0] * nb + i, 0)),
                      pl.BlockSpec((None, RB, D), lambda s, i, c: (s, i, 0))],
            out_specs=pl.BlockSpec((None, RB, D), lambda s, i, c: (s, i, 0))),
        out_shape=jax.ShapeDtypeStruct((NSHARD, HALF, D), BF16),
        compiler_params=_cp(("parallel", "parallel")),
    )(cidx, g, ra)


def _exchange_chips(p):
    def kern(p_ref, o_ref, send, recv):
        x, y, c, chips = _place()
        cps = [_rcopy(p_ref.at[2 * px + py], o_ref.at[k], send.at[k], recv.at[k], (px, py, c))
               for k, (px, py) in enumerate(chips)]
        for cp in cps:
            cp.start()
        for cp in cps:
            cp.wait()

    return pl.pallas_call(
        kern, name="exchange_chips", in_specs=[ANY], out_specs=ANY,
        out_shape=jax.ShapeDtypeStruct((3, HALF, D), p.dtype),
        scratch_shapes=[pltpu.SemaphoreType.DMA((3,)), pltpu.SemaphoreType.DMA((3,))],
    )(p)


def _add_chips(mc, g, ra, rb):
    nb = HALF // RB

    def kern(m_ref, g_ref, a_ref, r0_ref, r1_ref, r2_ref, o_ref):
        own = g_ref[...] + a_ref[...]
        o_ref[...] = ((own + r0_ref[...].astype(F32)) + r1_ref[...].astype(F32)) + r2_ref[...].astype(F32)

    return pl.pallas_call(
        kern, name="add_chips",
        grid_spec=pltpu.PrefetchScalarGridSpec(
            num_scalar_prefetch=1, grid=(nb,),
            in_specs=[pl.BlockSpec((None, RB, D), lambda i, m: (m[0], m[1] * nb + i, 0)),
                      pl.BlockSpec((None, RB, D), lambda i, m: (m[0], i, 0))]
            + [pl.BlockSpec((None, RB, D), functools.partial(lambda i, m, k: (k, i, 0), k=k)) for k in range(3)],
            out_specs=pl.BlockSpec((RB, D), lambda i, m: (i, 0))),
        out_shape=jax.ShapeDtypeStruct((HALF, D), F32),
        compiler_params=_cp(("parallel",)),
    )(mc, g, ra, rb, rb, rb)


def _share_halves(r):
    def kern(r_ref, o_ref, send, recv):
        x, y, c, _ = _place()
        cp = _rcopy(r_ref, o_ref, send, recv, (x, y, 1 - c))
        cp.start()
        cp.wait()

    return pl.pallas_call(
        kern, name="share_halves", in_specs=[ANY], out_specs=ANY,
        out_shape=jax.ShapeDtypeStruct((HALF, D), F32),
        scratch_shapes=[pltpu.SemaphoreType.DMA, pltpu.SemaphoreType.DMA],
    )(r)


def _reduce_small(s):
    def kern(s_ref, o_ref, buf, send, recv):
        x, y, c, _ = _place()
        me = 4 * x + 2 * y + c
        buf[me] = s_ref[...]
        cps = []
        for r in range(1, 8):
            peer = (1 - x if r & 4 else x, 1 - y if r & 2 else y, 1 - c if r & 1 else c)
            cps.append(_rcopy(s_ref, buf.at[me], send.at[r - 1], recv.at[r - 1], peer))
        for cp in cps:
            cp.start()
        for cp in cps:
            cp.wait()
        acc = buf[0]
        for i in range(1, 8):
            acc = acc + buf[i]
        o_ref[...] = acc

    return pl.pallas_call(
        kern, name="reduce_small",
        in_specs=[pl.BlockSpec(memory_space=pltpu.VMEM)], out_specs=pl.BlockSpec(memory_space=pltpu.VMEM),
        out_shape=jax.ShapeDtypeStruct((SROWS, D), F32),
        scratch_shapes=[pltpu.VMEM((8, SROWS, D), F32), pltpu.SemaphoreType.DMA((7,)), pltpu.SemaphoreType.DMA((7,))],
    )(s)


def _adamw(g, w, m, v, rb, name):
    rows, cols = g.shape

    def kern(g_ref, w_ref, m_ref, v_ref, d_ref, nm_ref, nv_ref):
        g_ = g_ref[...]
        m_ = ADAM_B1 * m_ref[...] + (1.0 - ADAM_B1) * g_
        v_ = ADAM_B2 * v_ref[...] + (1.0 - ADAM_B2) * jnp.square(g_)
        m_hat = m_ / (1.0 - ADAM_B1 ** ADAM_STEP)
        v_hat = v_ / (1.0 - ADAM_B2 ** ADAM_STEP)
        d_ref[...] = -ADAM_LR * (m_hat / (jnp.sqrt(v_hat) + ADAM_EPS) + ADAM_WD * w_ref[...])
        nm_ref[...] = m_
        nv_ref[...] = v_

    assert rows % rb == 0
    blk = pl.BlockSpec((rb, cols), lambda i: (i, 0))
    return pl.pallas_call(
        kern, name=name, grid=(rows // rb,), in_specs=[blk] * 4, out_specs=[blk] * 3,
        out_shape=[jax.ShapeDtypeStruct((rows, cols), F32)] * 3,
        compiler_params=_cp(("parallel",)),
    )(g, w, m, v)


WEIGHTS = ("c_ctx", "w_mod", "b_mod", "norm_w", "w_in", "ssm_conv_w", "ssm_conv_b", "dt_bias", "a_log", "d_skip",
           "ssm_norm_w", "w_out_ssm", "conf_conv_w", "conf_conv_b", "conf_ln_w", "conf_ln_b", "w_out_conf", "w_out",
           "final_norm_w")


def kernel(x, c, ctx, c_ctx, w_mod, b_mod, norm_w, w_in, ssm_conv_w, ssm_conv_b, dt_bias, a_log, d_skip, ssm_norm_w, w_out_ssm, conf_conv_w, conf_conv_b, conf_ln_w, conf_ln_b, w_out_conf, w_out, final_norm_w, loss_target, m_c_ctx, m_w_mod, m_b_mod, m_norm_w, m_w_in, m_ssm_conv_w, m_ssm_conv_b, m_dt_bias, m_a_log, m_d_skip, m_ssm_norm_w, m_w_out_ssm, m_conf_conv_w, m_conf_conv_b, m_conf_ln_w, m_conf_ln_b, m_w_out_conf, m_w_out, m_final_norm_w, v_c_ctx, v_w_mod, v_b_mod, v_norm_w, v_w_in, v_ssm_conv_w, v_ssm_conv_b, v_dt_bias, v_a_log, v_d_skip, v_ssm_norm_w, v_w_out_ssm, v_conf_conv_w, v_conf_conv_b, v_conf_ln_w, v_conf_ln_b, v_w_out_conf, v_w_out, v_final_norm_w):
    wv = (c_ctx, w_mod, b_mod, norm_w, w_in, ssm_conv_w, ssm_conv_b, dt_bias, a_log, d_skip, ssm_norm_w, w_out_ssm,
          conf_conv_w, conf_conv_b, conf_ln_w, conf_ln_b, w_out_conf, w_out, final_norm_w)
    mv = (m_c_ctx, m_w_mod, m_b_mod, m_norm_w, m_w_in, m_ssm_conv_w, m_ssm_conv_b, m_dt_bias, m_a_log, m_d_skip,
          m_ssm_norm_w, m_w_out_ssm, m_conf_conv_w, m_conf_conv_b, m_conf_ln_w, m_conf_ln_b, m_w_out_conf, m_w_out,
          m_final_norm_w)
    vv = (v_c_ctx, v_w_mod, v_b_mod, v_norm_w, v_w_in, v_ssm_conv_w, v_ssm_conv_b, v_dt_bias, v_a_log, v_d_skip,
          v_ssm_norm_w, v_w_out_ssm, v_conf_conv_w, v_conf_conv_b, v_conf_ln_w, v_conf_ln_b, v_w_out_conf, v_w_out,
          v_final_norm_w)
    shapes = {n: a.shape for n, a in zip(WEIGHTS, wv)}

    def squeeze(d):
        return {n: (a if n in ("c_ctx", "final_norm_w") else a[0]) for n, a in d.items()}

    w, m, v = (squeeze(dict(zip(WEIGHTS, t))) for t in (wv, mv, vv))

    my_chip = 2 * lax.axis_index("x") + lax.axis_index("y")
    my_core = lax.axis_index("c")

    pw = _pack_shard(w)
    pwb, psm = pw.astype(BF16), pw[O_SC:PROWS]
    gm, gs = _gather_weights(pwb, psm)
    gm = lax.dynamic_update_slice(gm, pwb[None], (my_chip, 0, 0))
    gs = lax.dynamic_update_slice(gs, psm[None], (my_chip, 0, 0))
    full = dict(w)
    full.update(_unpack_gathered(gm, gs))

    lsum, grad_x, g = _local_step(x[0], c, ctx[0], loss_target[0], full)
    loss = lax.psum(lsum, ("x", "y", "c"))

    cidx = my_core.astype(jnp.int32).reshape(1)
    mc = jnp.stack([my_chip, my_core]).astype(jnp.int32)
    gp = _pack_full(g)
    sib = _swap_halves(gp)
    part = _add_halves(cidx, gp, sib)
    red = _add_chips(mc, gp, sib, _exchange_chips(part))
    got = _share_halves(red)
    g_sh = jnp.concatenate([jnp.where(my_core == 0, red, got), jnp.where(my_core == 0, got, red)], axis=0)
    g_sm = _reduce_small(_pack_small(g))

    gr = {"w_mod": g_sh[O_MOD:O_IN].reshape(D, R_MOD), "w_in": g_sh[O_IN:O_OS].reshape(D, R_IN), "rest": g_sh[O_OS:]}
    wr, mr, vr = ({"w_mod": t["w_mod"], "w_in": t["w_in"], "rest": _pack_rest(t)} for t in (w, m, v))
    res = {k: _adamw(gr[k], wr[k], mr[k], vr[k], rb, "adamw_" + k)
           for k, rb in (("w_in", 128), ("w_mod", 512), ("rest", (PROWS - O_OS) // 2))}
    res_sm = _adamw(g_sm, _pack_small(w), _pack_small(m), _pack_small(v), SROWS, "adamw_small")

    outs = []
    for i in range(4):
        pick = (lambda k: gr[k]) if i == 0 else (lambda k: res[k][i - 1])
        d = {"w_mod": pick("w_mod")[None], "w_in": pick("w_in")[None]}
        d.update(_unpack_rest(pick("rest")))
        d.update(_unpack_small(g_sm if i == 0 else res_sm[i - 1], shapes))
        outs.extend(d[n] for n in WEIGHTS)
    return (loss, grad_x[None], *outs)
```

```python
import functools

import jax
import jax.numpy as jnp
from jax import lax
from jax.experimental import pallas as pl
from jax.experimental.pallas import tpu as pltpu

F32, BF16 = jnp.float32, jnp.bfloat16

D = 1024
DI = 2048
NH = 32
HP = 64
NG = 8
HPG = 4
NS = 128
Q = 128
GW = 64
CK = 31
SK = 4
CTX = 256
EPS = 1e-6
XBC = DI + 2 * NG * NS
SSDW = XBC + 128
RESTW = 7168
T = 256
VMEM_LIMIT = 56 * 1024 * 1024

ADAM_LR, ADAM_B1, ADAM_B2, ADAM_EPS, ADAM_WD, ADAM_STEP = 0.001, 0.9, 0.999, 1e-08, 0.01, 10


def _cp(sem):
    return pltpu.CompilerParams(dimension_semantics=sem, vmem_limit_bytes=VMEM_LIMIT)


def _sig(x):
    return jax.nn.sigmoid(x)


def _silu(x):
    return x * _sig(x)


def _dsilu(x):
    s = _sig(x)
    return s * (1.0 + x * (1.0 - s))


def _dot(a, b):
    return jnp.dot(a, b, preferred_element_type=F32)


def _dot_nt(a, b):
    return lax.dot_general(a, b, (((1,), (1,)), ((), ())), preferred_element_type=F32)


def _split3(x):
    h = x.astype(BF16)
    r = x - h.astype(F32)
    m = r.astype(BF16)
    l = (r - m.astype(F32)).astype(BF16)
    return h, m, l


def _dot3_l(sel, x):
    h, m, l = _split3(x)
    return _dot(sel, h) + _dot(sel, m) + _dot(sel, l)


def _dot3_r(x, sel):
    h, m, l = _split3(x)
    return _dot(h, sel) + _dot(m, sel) + _dot(l, sel)


def _split2(x):
    h = x.astype(BF16)
    return h, (x - h.astype(F32)).astype(BF16)


def _dot2_l(sel, x):
    h, l = _split2(x)
    return _dot(sel, h) + _dot(sel, l)


def _dot2_r(x, sel):
    h, l = _split2(x)
    return _dot(h, sel) + _dot(l, sel)


def _iota(shape, dim):
    return lax.broadcasted_iota(jnp.int32, shape, dim)


def _mm(a, b, dims, m, n, k, bm, bn, bk, out_dtype, name):
    nk = k // bk
    assert m % bm == 0 and n % bn == 0 and k % bk == 0, (name, m, n, k, bm, bn, bk)

    def prod(a_ref, b_ref):
        av = a_ref[...].astype(BF16)
        bv = b_ref[...].astype(BF16)
        if dims == "nn":
            return _dot(av, bv)
        if dims == "nt":
            return _dot_nt(av, bv)
        return lax.dot_general(av, bv, (((0,), (0,)), ((), ())), preferred_element_type=F32)

    def kern_one(a_ref, b_ref, o_ref):
        o_ref[...] = prod(a_ref, b_ref).astype(out_dtype)

    def kern_acc(a_ref, b_ref, o_ref, acc):
        kk = pl.program_id(2)

        @pl.when(kk == 0)
        def _():
            acc[...] = jnp.zeros_like(acc)

        acc[...] += prod(a_ref, b_ref)

        @pl.when(kk == nk - 1)
        def _():
            o_ref[...] = acc[...].astype(out_dtype)

    if dims == "nn":
        a_spec = pl.BlockSpec((bm, bk), lambda j, i, kk: (i, kk))
        b_spec = pl.BlockSpec((bk, bn), lambda j, i, kk: (kk, j))
    elif dims == "nt":
        a_spec = pl.BlockSpec((bm, bk), lambda j, i, kk: (i, kk))
        b_spec = pl.BlockSpec((bn, bk), lambda j, i, kk: (j, kk))
    else:
        a_spec = pl.BlockSpec((bk, bm), lambda j, i, kk: (kk, i))
        b_spec = pl.BlockSpec((bk, bn), lambda j, i, kk: (kk, j))
    return pl.pallas_call(
        kern_one if nk == 1 else kern_acc, name=name,
        grid=(n // bn, m // bm, nk),
        in_specs=[a_spec, b_spec],
        out_specs=pl.BlockSpec((bm, bn), lambda j, i, kk: (i, j)),
        out_shape=jax.ShapeDtypeStruct((m, n), out_dtype),
        scratch_shapes=[] if nk == 1 else [pltpu.VMEM((bm, bn), F32)],
        compiler_params=_cp(("parallel", "parallel", "arbitrary")),
    )(a, b)


def _mod_fwd(cc, w_mod, b_mod):
    def kern(cc_ref, w_ref, b_ref, o_ref):
        s = _silu(cc_ref[...]).astype(BF16)
        o_ref[...] = _dot(s, w_ref[...]) + b_ref[...]

    return pl.pallas_call(
        kern, name="mod_fwd", grid=(3,),
        in_specs=[pl.BlockSpec((8, D), lambda j: (0, 0)), pl.BlockSpec((D, D), lambda j: (0, j)),
                  pl.BlockSpec((1, D), lambda j: (0, j))],
        out_specs=pl.BlockSpec((8, D), lambda j: (0, j)),
        out_shape=jax.ShapeDtypeStruct((8, 3 * D), F32),
        compiler_params=_cp(("parallel",)),
    )(cc, w_mod, b_mod)


def _mod_bwd(dmod, cc, cct, w_mod):
    def kern(dm_ref, cc_ref, cct_ref, w_ref, gw_ref, gb_ref, gc_ref):
        kk = pl.program_id(0)
        dm = dm_ref[...]
        sct = _silu(cct_ref[...])
        gw_ref[...] = sct[:, 0:1] * dm[0:1, :] + sct[:, 1:2] * dm[1:2, :]
        gb_ref[...] = jnp.broadcast_to(dm[0:1, :] + dm[1:2, :], dm.shape)

        @pl.when(kk == 0)
        def _():
            gc_ref[...] = jnp.zeros_like(gc_ref)

        gc_ref[...] += _dot_nt(dm.astype(BF16), w_ref[...])

        @pl.when(kk == 2)
        def _():
            gc_ref[...] = gc_ref[...] * _dsilu(cc_ref[...])

    return pl.pallas_call(
        kern, name="mod_bwd", grid=(3,),
        in_specs=[pl.BlockSpec((8, D), lambda j: (0, j)), pl.BlockSpec((8, D), lambda j: (0, 0)),
                  pl.BlockSpec((D, 8), lambda j: (0, 0)), pl.BlockSpec((D, D), lambda j: (0, j))],
        out_specs=[pl.BlockSpec((D, D), lambda j: (0, j)), pl.BlockSpec((8, D), lambda j: (0, j)),
                   pl.BlockSpec((8, D), lambda j: (0, 0))],
        out_shape=[jax.ShapeDtypeStruct((D, 3 * D), F32), jax.ShapeDtypeStruct((8, 3 * D), F32),
                   jax.ShapeDtypeStruct((8, D), F32)],
        compiler_params=_cp(("arbitrary",)),
    )(dmod, cc, cct, w_mod)


def _ext_specs(nct):
    return (pl.BlockSpec((T, D), lambda i: (jnp.minimum(i, nct - 1), 0)),
            pl.BlockSpec((T, D), lambda i: (jnp.maximum(i - nct, 0), 0)))


def _norm_fwd(ctx, xl, mod, nw, nct):
    lext = ctx.shape[0] + xl.shape[0]

    def kern(c_ref, x_ref, mod_ref, nw_ref, h_ref):
        is_ctx = pl.program_id(0) < nct
        x = jnp.where(is_ctx, c_ref[...], x_ref[...])
        r = lax.rsqrt(jnp.mean(x * x, axis=-1, keepdims=True) + EPS)
        xn = x * r * nw_ref[...]
        shift = jnp.where(is_ctx, mod_ref[1:2, 0:D], mod_ref[0:1, 0:D])
        scale = jnp.where(is_ctx, mod_ref[1:2, D:2 * D], mod_ref[0:1, D:2 * D])
        h_ref[...] = (xn * (1.0 + scale) + shift).astype(BF16)

    return pl.pallas_call(
        kern, name="norm_fwd", grid=(lext // T,),
        in_specs=[*_ext_specs(nct), pl.BlockSpec((8, 3 * D), lambda i: (0, 0)),
                  pl.BlockSpec((1, D), lambda i: (0, 0))],
        out_specs=pl.BlockSpec((T, D), lambda i: (i, 0)),
        out_shape=jax.ShapeDtypeStruct((lext, D), BF16),
        compiler_params=_cp(("parallel",)),
    )(ctx, xl, mod, nw)


def _norm_bwd(dha, dhb, ctx, xl, dx2, mod, nw, nct):
    lext = ctx.shape[0] + xl.shape[0]
    ntl = lext // T

    def kern(dha_ref, dhb_ref, c_ref, x_ref, dx2_ref, mod_ref, nw_ref, gx_ref, gnw_ref, dss_ref):
        i = pl.program_id(0)
        is_ctx = i < nct

        @pl.when(i == 0)
        def _():
            gnw_ref[...] = jnp.zeros_like(gnw_ref)
            dss_ref[...] = jnp.zeros_like(dss_ref)

        x = jnp.where(is_ctx, c_ref[...], x_ref[...])
        dh_ = dha_ref[...] + jnp.where(is_ctx, 0.0, dhb_ref[...])
        nw_ = nw_ref[...]
        r = lax.rsqrt(jnp.mean(x * x, axis=-1, keepdims=True) + EPS)
        xn = x * r
        scale = jnp.where(is_ctx, mod_ref[1:2, D:2 * D], mod_ref[0:1, D:2 * D])
        dsh = jnp.sum(dh_, axis=0, keepdims=True)
        dsc = jnp.sum(dh_ * (xn * nw_), axis=0, keepdims=True)
        row = jnp.concatenate([dsh, dsc], axis=1)
        rid = _iota((8, 2 * D), 0)
        dss_ref[...] += jnp.where(rid == jnp.where(is_ctx, 1, 0), row, 0.0)
        dxnw = dh_ * (1.0 + scale)
        gnw_ref[...] += jnp.broadcast_to(jnp.sum(dxnw * xn, axis=0, keepdims=True), (8, D))
        dxn = dxnw * nw_
        dx = r * (dxn - xn * jnp.mean(dxn * xn, axis=-1, keepdims=True))
        gx_ref[...] = dx2_ref[...] + dx

    return pl.pallas_call(
        kern, name="norm_bwd", grid=(ntl,),
        in_specs=[pl.BlockSpec((T, D), lambda i: (i, 0)), pl.BlockSpec((T, D), lambda i: (jnp.maximum(i - nct, 0), 0)),
                  *_ext_specs(nct),
                  pl.BlockSpec((T, D), lambda i: (jnp.maximum(i - nct, 0), 0)),
                  pl.BlockSpec((8, 3 * D), lambda i: (0, 0)), pl.BlockSpec((1, D), lambda i: (0, 0))],
        out_specs=[pl.BlockSpec((T, D), lambda i: (jnp.maximum(i - nct, 0), 0)),
                   pl.BlockSpec((8, D), lambda i: (0, 0)), pl.BlockSpec((8, 2 * D), lambda i: (0, 0))],
        out_shape=[jax.ShapeDtypeStruct((lext - nct * T, D), F32), jax.ShapeDtypeStruct((8, D), F32),
                   jax.ShapeDtypeStruct((8, 2 * D), F32)],
        compiler_params=_cp(("arbitrary",)),
    )(dha, dhb, ctx, xl, dx2, mod, nw)


CB = 1024


def _halo_specs(width_blk, col_off_blocks, ntl):
    t8 = T // 8
    main = pl.BlockSpec((T, width_blk), lambda j, i: (i, j + col_off_blocks))
    prev = pl.BlockSpec((8, width_blk), lambda j, i: (jnp.maximum(i * t8 - 1, 0), j + col_off_blocks))
    nxt = pl.BlockSpec((8, width_blk), lambda j, i: (jnp.minimum((i + 1) * t8, ntl * t8 - 1), j + col_off_blocks))
    return main, prev, nxt


def _seq_edges(i, nct, ntl):
    starts = jnp.logical_or(i == 0, i == nct)
    ends = jnp.logical_or(i == nct - 1, i == ntl - 1)
    return starts, ends


def _shifted(ext, off):
    n = ext.shape[0]
    return pltpu.roll(ext, (-off) % n, axis=0)[8:8 + T]


def _conv_fwd(proj_ssd, cw, cb, nct):
    lext = proj_ssd.shape[0]
    ntl = lext // T

    def kern(u_ref, up_ref, un_ref, w_ref, b_ref, o_ref):
        i = pl.program_id(1)
        starts, ends = _seq_edges(i, nct, ntl)
        up = jnp.where(starts, 0.0, up_ref[...])
        un = jnp.where(ends, 0.0, un_ref[...])
        ext = jnp.concatenate([up, u_ref[...], un], axis=0)
        w = w_ref[...]
        pre = b_ref[...] + w[0:1] * _shifted(ext, -2) + w[1:2] * _shifted(ext, -1) \
            + w[2:3] * u_ref[...] + w[3:4] * _shifted(ext, 1)
        o_ref[...] = _silu(pre)

    main, prev, nxt = _halo_specs(CB, 0, ntl)
    return pl.pallas_call(
        kern, name="conv_fwd", grid=(XBC // CB, ntl),
        in_specs=[main, prev, nxt, pl.BlockSpec((8, CB), lambda j, i: (0, j)), pl.BlockSpec((1, CB), lambda j, i: (0, j))],
        out_specs=pl.BlockSpec((T, CB), lambda j, i: (i, j)),
        out_shape=jax.ShapeDtypeStruct((lext, XBC), F32),
        compiler_params=_cp(("parallel", "parallel")),
    )(proj_ssd, proj_ssd, proj_ssd, cw, cb)


def _conv_bwd(dpost, proj_ssd, cw, cb, col_off, width, nct, name):
    lext = proj_ssd.shape[0]
    ntl = lext // T
    cob = col_off // CB

    def kern(u_ref, up_ref, un_ref, d_ref, dp_ref, dn_ref, w_ref, b_ref, du_ref, gw_ref, gb_ref):
        i = pl.program_id(1)

        @pl.when(i == 0)
        def _():
            gw_ref[...] = jnp.zeros_like(gw_ref)
            gb_ref[...] = jnp.zeros_like(gb_ref)

        starts, ends = _seq_edges(i, nct, ntl)
        ext = jnp.concatenate([jnp.where(starts, 0.0, up_ref[...]), u_ref[...], jnp.where(ends, 0.0, un_ref[...])], axis=0)
        dext = jnp.concatenate([jnp.where(starts, 0.0, dp_ref[...]), d_ref[...], jnp.where(ends, 0.0, dn_ref[...])], axis=0)
        w = w_ref[...]
        n = ext.shape[0]
        pre = b_ref[...] + w[0:1] * pltpu.roll(ext, 2, axis=0) + w[1:2] * pltpu.roll(ext, 1, axis=0) \
            + w[2:3] * ext + w[3:4] * pltpu.roll(ext, n - 1, axis=0)
        dpre = dext * _dsilu(pre)
        dm = dpre[8:8 + T]
        du = w[0:1] * _shifted(dpre, 2) + w[1:2] * _shifted(dpre, 1) + w[2:3] * dm + w[3:4] * _shifted(dpre, -1)
        du_ref[...] = du.astype(BF16)
        g0 = jnp.sum(dm * _shifted(ext, -2), axis=0, keepdims=True)
        g1 = jnp.sum(dm * _shifted(ext, -1), axis=0, keepdims=True)
        g2 = jnp.sum(dm * u_ref[...], axis=0, keepdims=True)
        g3 = jnp.sum(dm * _shifted(ext, 1), axis=0, keepdims=True)
        rid = _iota((8, CB), 0)
        gw_ref[...] += jnp.where(rid == 0, g0, jnp.where(rid == 1, g1, jnp.where(rid == 2, g2, jnp.where(rid == 3, g3, 0.0))))
        gb_ref[...] += jnp.broadcast_to(jnp.sum(dm, axis=0, keepdims=True), (8, CB))

    main, prev, nxt = _halo_specs(CB, cob, ntl)
    dmain, dprev, dnxt = _halo_specs(CB, 0, ntl)
    return pl.pallas_call(
        kern, name=name, grid=(width // CB, ntl),
        in_specs=[main, prev, nxt, dmain, dprev, dnxt,
                  pl.BlockSpec((8, CB), lambda j, i: (0, j + cob)), pl.BlockSpec((1, CB), lambda j, i: (0, j + cob))],
        out_specs=[pl.BlockSpec((T, CB), lambda j, i: (i, j)), pl.BlockSpec((8, CB), lambda j, i: (0, j)),
                   pl.BlockSpec((8, CB), lambda j, i: (0, j))],
        out_shape=[jax.ShapeDtypeStruct((lext, width), BF16), jax.ShapeDtypeStruct((8, width), F32),
                   jax.ShapeDtypeStruct((8, width), F32)],
        compiler_params=_cp(("parallel", "arbitrary")),
    )(proj_ssd, proj_ssd, proj_ssd, dpost, dpost, dpost, cw, cb)


def _tri(lower):
    r, c = _iota((Q, Q), 0), _iota((Q, Q), 1)
    return jnp.where((c <= r) if lower else (c >= r), 1.0, 0.0).astype(BF16)


def _is_bdir_lane(shape):
    ln = _iota(shape, len(shape) - 1)
    return jnp.logical_and(((ln >> 2) & 1) == 1, ln < 64)


def _dt_fwd(proj_ssd, dtb, av):
    lext = proj_ssd.shape[0]

    def kern(p_ref, b_ref, a_ref, dtg_ref, lag_ref, dtt_ref, lat_ref):
        lane = _iota((T, 128), 1)
        raw = p_ref[...] + b_ref[...]
        dt = jnp.where(lane < 64, jnp.maximum(raw, 0.0) + jnp.log1p(jnp.exp(-jnp.abs(raw))), 0.0)
        dta = dt * a_ref[...]
        tl, tu = _tri(True), _tri(False)
        isb = _is_bdir_lane((Q, 128))
        las = []
        for qq in range(T // Q):
            blk = dta[qq * Q:(qq + 1) * Q]
            las.append(jnp.where(isb, _dot3_l(tu, blk), _dot3_l(tl, blk)))
        la = jnp.concatenate(las, axis=0)
        for g in range(NG):
            sh = (128 - 8 * g) % 128
            dtg_ref[g] = jnp.where(lane < 8, pltpu.roll(dt, sh, axis=1) if sh else dt, 0.0)
            lag_ref[g] = jnp.where(lane < 8, pltpu.roll(la, sh, axis=1) if sh else la, 0.0)
        dtt_ref[...] = dt.T[0:64]
        lat_ref[...] = la.T[0:64]

    return pl.pallas_call(
        kern, name="dt_fwd", grid=(lext // T,),
        in_specs=[pl.BlockSpec((T, 128), lambda i: (i, XBC // 128)), pl.BlockSpec((1, 128), lambda i: (0, 0)),
                  pl.BlockSpec((1, 128), lambda i: (0, 0))],
        out_specs=[pl.BlockSpec((NG, T, 128), lambda i: (0, i, 0)), pl.BlockSpec((NG, T, 128), lambda i: (0, i, 0)),
                   pl.BlockSpec((64, T), lambda i: (0, i)), pl.BlockSpec((64, T), lambda i: (0, i))],
        out_shape=[jax.ShapeDtypeStruct((NG, lext, 128), F32), jax.ShapeDtypeStruct((NG, lext, 128), F32),
                   jax.ShapeDtypeStruct((64, lext), F32), jax.ShapeDtypeStruct((64, lext), F32)],
        compiler_params=_cp(("parallel",)),
    )(proj_ssd, dtb, av)


def _dt_bwd(ddtg, proj_ssd, dtb):
    lext = proj_ssd.shape[0]

    def kern(d_ref, p_ref, b_ref, o_ref, gb_ref):
        @pl.when(pl.program_id(0) == 0)
        def _():
            gb_ref[...] = jnp.zeros_like(gb_ref)

        acc = d_ref[0]
        for g in range(1, NG):
            acc = acc + pltpu.roll(d_ref[g], 8 * g, axis=1)
        draw = acc * _sig(p_ref[...] + b_ref[...])
        o_ref[...] = draw.astype(BF16)
        gb_ref[...] += jnp.broadcast_to(jnp.sum(draw, axis=0, keepdims=True), (8, 128))

    return pl.pallas_call(
        kern, name="dt_bwd", grid=(lext // T,),
        in_specs=[pl.BlockSpec((NG, T, 128), lambda i: (0, i, 0)), pl.BlockSpec((T, 128), lambda i: (i, XBC // 128)),
                  pl.BlockSpec((1, 128), lambda i: (0, 0))],
        out_specs=[pl.BlockSpec((T, 128), lambda i: (i, 0)), pl.BlockSpec((8, 128), lambda i: (0, 0))],
        out_shape=[jax.ShapeDtypeStruct((lext, 128), BF16), jax.ShapeDtypeStruct((8, 128), F32)],
        compiler_params=_cp(("arbitrary",)),
    )(ddtg, proj_ssd, dtb)


def _expand_sel(d):
    r, c = _iota((128, 256), 0), _iota((128, 256), 1)
    return jnp.where(r == 4 * d + (c >> 6), 1.0, 0.0).astype(BF16)


def _reduce_sel(d):
    r, c = _iota((256, 128), 0), _iota((256, 128), 1)
    return jnp.where(c == 4 * d + (r >> 6), 1.0, 0.0).astype(BF16)


def _chunk_of_bwd_dir(j, ncc, nc):
    return jnp.where(j < ncc, ncc - 1 - j, nc + ncc - 1 - j)


def _dir_terms(la, dt, d):
    lane = _iota(la.shape, 1)
    mine = jnp.logical_and(lane >= 4 * d, lane < 4 * d + 4)
    la = jnp.where(mine, la, 0.0)
    tot = la[Q - 1:Q] if d == 0 else la[0:1]
    wnd = jnp.exp(tot - la)
    return tot, wnd * jnp.where(mine, dt, 0.0), wnd


def _ssd_state(xbc, dtg, lag, ncc):
    lext = xbc.shape[0]
    nc = lext // Q

    def kern(xf_ref, bf_ref, dtf_ref, laf_ref, xb_ref, bb_ref, dtb_ref, lab_ref, hf_ref, hb_ref, sf, sb):
        @pl.when(pl.program_id(0) == 0)
        def _():
            sf[...] = jnp.zeros_like(sf)
            sb[...] = jnp.zeros_like(sb)

        for d, (x_ref, b_ref, dt_ref, la_ref, h_ref, s) in enumerate(
                ((xf_ref, bf_ref, dtf_ref, laf_ref, hf_ref, sf), (xb_ref, bb_ref, dtb_ref, lab_ref, hb_ref, sb))):
            h_ref[...] = s[...]
            ex = _expand_sel(d)
            for g in range(NG):
                cols = slice(256 * g, 256 * (g + 1))
                tot, w_end, _ = _dir_terms(la_ref[g], dt_ref[g], d)
                wexp = _dot2_r(w_end, ex)
                dexp = _dot2_r(jnp.broadcast_to(jnp.exp(tot), (8, 128)), ex)[0:1]
                xw = (x_ref[:, cols] * wexp).astype(BF16)
                s[:, cols] = s[:, cols] * dexp + _dot(b_ref[:, 128 * g:128 * (g + 1)].T.astype(BF16), xw)

    cb = functools.partial(_chunk_of_bwd_dir, ncc=ncc, nc=nc)
    sm = lambda f: pl.BlockSpec((NG, Q, 128), lambda j: (0, f(j), 0))
    one = lambda j: j
    return pl.pallas_call(
        kern, name="ssd_state", grid=(nc,),
        in_specs=[pl.BlockSpec((Q, DI), lambda j: (j, 0)), pl.BlockSpec((Q, NG * NS), lambda j: (j, 2)), sm(one), sm(one),
                  pl.BlockSpec((Q, DI), lambda j: (cb(j), 0)), pl.BlockSpec((Q, NG * NS), lambda j: (cb(j), 2)), sm(cb), sm(cb)],
        out_specs=[pl.BlockSpec((None, 128, DI), lambda j: (j, 0, 0)),
                   pl.BlockSpec((None, 128, DI), lambda j: (cb(j), 0, 0))],
        out_shape=[jax.ShapeDtypeStruct((nc, 128, DI), F32), jax.ShapeDtypeStruct((nc, 128, DI), F32)],
        scratch_shapes=[pltpu.VMEM((128, DI), F32), pltpu.VMEM((128, DI), F32)],
        compiler_params=_cp(("arbitrary",)),
    )(xbc, xbc, dtg, lag, xbc, xbc, dtg, lag)


def _ssd_out(xbc, dtg, lag, dtt, lat, htf, htb, ncc):
    lext = xbc.shape[0]
    nc = lext // Q
    ncx = nc - ncc

    gps = 4
    li, si = (lambda: _iota((Q, Q), 0)), (lambda: _iota((Q, Q), 1))

    def kern(x_ref, b_ref, c_ref, dtg_ref, lag_ref, dtt_ref, lat_ref, hf_ref, hb_ref, y_ref):
        lane = _iota((Q, 256), 1)
        masks = (li() >= si(), li() <= si())
        for gg in range(gps):
            cols = slice(256 * gg, 256 * (gg + 1))
            cm = c_ref[:, 128 * gg:128 * (gg + 1)]
            xb_ = x_ref[:, cols].astype(BF16)
            s_ = _dot_nt(cm.astype(BF16), b_ref[:, 128 * gg:128 * (gg + 1)].astype(BF16))
            la, dtt_, lat_ = lag_ref[gg], dtt_ref[8 * gg:8 * (gg + 1)], lat_ref[8 * gg:8 * (gg + 1)]
            elam = jnp.exp(la)
            y = jnp.zeros((Q, 256), F32)
            for d, h_ref in enumerate((hf_ref, hb_ref)):
                rhs = jnp.concatenate([xb_, h_ref[:, cols].astype(BF16)], axis=0)
                for r in range(HPG):
                    j = 4 * d + r
                    lm = jnp.where(masks[d], jnp.exp(la[:, j:j + 1] - lat_[j:j + 1, :]), 0.0)
                    w = s_ * lm * dtt_[j:j + 1, :]
                    lhs = jnp.concatenate([w, cm * elam[:, j:j + 1]], axis=1).astype(BF16)
                    y = y + jnp.where((lane >> 6) == r, _dot(lhs, rhs), 0.0)
            y_ref[:, cols] = y

    nb = NG // gps
    sm = pl.BlockSpec((gps, Q, 128), lambda c, g: (g, c + ncc, 0))
    smt = pl.BlockSpec((8 * gps, Q), lambda c, g: (g, c + ncc))
    st3 = pl.BlockSpec((None, 128, 256 * gps), lambda c, g: (c + ncc, 0, g))
    return pl.pallas_call(
        kern, name="ssd_out", grid=(ncx, nb),
        in_specs=[pl.BlockSpec((Q, 256 * gps), lambda c, g: (c + ncc, g)),
                  pl.BlockSpec((Q, 128 * gps), lambda c, g: (c + ncc, 2 * nb + g)),
                  pl.BlockSpec((Q, 128 * gps), lambda c, g: (c + ncc, 3 * nb + g)), sm, sm, smt, smt, st3, st3],
        out_specs=pl.BlockSpec((Q, 256 * gps), lambda c, g: (c, g)),
        out_shape=jax.ShapeDtypeStruct((ncx * Q, DI), F32),
        compiler_params=_cp(("parallel", "parallel")),
    )(xbc, xbc, xbc, dtg, lag, dtt, lat, htf, htb)


def _ssd_bwd_state(xbc, dy, lag, ncc):
    lext = xbc.shape[0]
    nc = lext // Q

    def kern(cf_ref, dyf_ref, laf_ref, cb_ref, dyb_ref, lab_ref, df_ref, db_ref, sf, sb):
        @pl.when(pl.program_id(0) == 0)
        def _():
            sf[...] = jnp.zeros_like(sf)
            sb[...] = jnp.zeros_like(sb)

        for d, (c_ref, dy_ref, la_ref, o_ref, s) in enumerate(
                ((cf_ref, dyf_ref, laf_ref, df_ref, sf), (cb_ref, dyb_ref, lab_ref, db_ref, sb))):
            o_ref[...] = s[...]
            ex = _expand_sel(d)
            for g in range(NG):
                cols = slice(256 * g, 256 * (g + 1))
                la = la_ref[g]
                tot = la[Q - 1:Q] if d == 0 else la[0:1]
                eexp = _dot2_r(jnp.exp(la), ex)
                dexp = _dot2_r(jnp.broadcast_to(jnp.exp(tot), (8, 128)), ex)[0:1]
                dye = (dy_ref[:, cols] * eexp).astype(BF16)
                s[:, cols] = s[:, cols] * dexp + _dot(c_ref[:, 128 * g:128 * (g + 1)].T.astype(BF16), dye)

    cf = lambda j: nc - 1 - j
    cb = lambda j: _chunk_of_bwd_dir(nc - 1 - j, ncc, nc)
    sm = lambda f: pl.BlockSpec((NG, Q, 128), lambda j: (0, f(j), 0))
    return pl.pallas_call(
        kern, name="ssd_bwd_state", grid=(nc,),
        in_specs=[pl.BlockSpec((Q, NG * NS), lambda j: (cf(j), 3)), pl.BlockSpec((Q, DI), lambda j: (cf(j), 0)), sm(cf),
                  pl.BlockSpec((Q, NG * NS), lambda j: (cb(j), 3)), pl.BlockSpec((Q, DI), lambda j: (cb(j), 0)), sm(cb)],
        out_specs=[pl.BlockSpec((None, 128, DI), lambda j: (cf(j), 0, 0)),
                   pl.BlockSpec((None, 128, DI), lambda j: (cb(j), 0, 0))],
        out_shape=[jax.ShapeDtypeStruct((nc, 128, DI), F32), jax.ShapeDtypeStruct((nc, 128, DI), F32)],
        scratch_shapes=[pltpu.VMEM((128, DI), F32), pltpu.VMEM((128, DI), F32)],
        compiler_params=_cp(("arbitrary",)),
    )(xbc, dy, lag, xbc, dy, lag)


def _ssd_bwd_out(xbc, dy, dsk, dtg, lag, dtt, lat, htf, htb, dhf, dhb, a_rows):
    lext = xbc.shape[0]
    nc = lext // Q

    gps = 1

    def kern(x_ref, b_ref, c_ref, dy_ref, sk_ref, dtg_ref, lag_ref, dtt_ref, lat_ref, hf_ref, hb_ref, df_ref, db_ref,
             a_ref, dx_ref, dbo_ref, dco_ref, ddt_ref, ga_ref):
        @pl.when(pl.program_id(1) == 0)
        def _():
            ga_ref[...] = jnp.zeros_like(ga_ref)

        for gg in range(gps):
            one_group(gg, x_ref, b_ref, c_ref, dy_ref, sk_ref, dtg_ref, lag_ref, dtt_ref, lat_ref, hf_ref, hb_ref, df_ref,
                      db_ref, a_ref, dx_ref, dbo_ref, dco_ref, ddt_ref, ga_ref)

    def one_group(gg, x_ref, b_ref, c_ref, dy_ref, sk_ref, dtg_ref, lag_ref, dtt_ref, lat_ref, hf_ref, hb_ref, df_ref,
                  db_ref, a_ref, dx_ref, dbo_ref, dco_ref, ddt_ref, ga_ref):
        g = pl.program_id(0) * gps + gg
        cols, cols128 = slice(256 * gg, 256 * (gg + 1)), slice(128 * gg, 128 * (gg + 1))
        x, bm, cm, dy_ = x_ref[:, cols], b_ref[:, cols128], c_ref[:, cols128], dy_ref[:, cols]
        xb_, bb_, cb_, dyb_ = x.astype(BF16), bm.astype(BF16), cm.astype(BF16), dy_.astype(BF16)
        st = _dot_nt(bb_, cb_)
        si, li = _iota((Q, Q), 0), _iota((Q, Q), 1)
        lane = _iota((Q, 256), 1)
        lane128 = _iota((Q, 128), 1)
        row128 = _iota((Q, 128), 0)
        sub = _iota((128, Q), 0)
        la, dt = lag_ref[gg], dtg_ref[gg]
        dtt_, lat_ = dtt_ref[8 * gg:8 * (gg + 1)], lat_ref[8 * gg:8 * (gg + 1)]
        elam = jnp.exp(la)
        dst = jnp.zeros((Q, Q), F32)
        dxa = jnp.zeros((Q, 256), F32)
        dba = jnp.zeros((Q, 128), F32)
        dca = jnp.zeros((Q, 128), F32)
        dlam = jnp.zeros((Q, 128), F32)
        ddir = jnp.zeros((Q, 128), F32)
        rows = jnp.zeros((128, Q), F32)
        for d, (h_ref, dh_ref) in enumerate(((hf_ref, df_ref), (hb_ref, db_ref))):
            ht, dht = h_ref[:, cols], dh_ref[:, cols]
            htb_, dhtb_ = ht.astype(BF16), dht.astype(BF16)
            tot, w_end, wnd = _dir_terms(la, dt, d)
            ex, rs = _expand_sel(d), _reduce_sel(d)
            elx = _dot2_r(elam, ex)
            wex = _dot2_r(w_end, ex)
            dye = dy_ * elx
            ch = _dot(cb_, htb_)
            bd = _dot(bb_, dhtb_)
            dca = dca + _dot_nt(dye.astype(BF16), htb_)
            dba = dba + _dot_nt((x * wex).astype(BF16), dhtb_)
            dlam = dlam + _dot2_r(dye * ch, rs)
            xbd = _dot2_r(x * bd, rs)
            e_ = w_end * xbd
            dlam = dlam - e_
            ddir = ddir + wnd * xbd
            hh = _dot2_r(jnp.broadcast_to(jnp.sum(dht * ht, axis=0, keepdims=True), (8, 256)), rs)[0:1]
            tot_term = jnp.sum(e_, axis=0, keepdims=True) + jnp.exp(tot) * hh
            dlam = dlam + jnp.where(row128 == (Q - 1 if d == 0 else 0), tot_term, 0.0)
            rhs = jnp.concatenate([dyb_, dhtb_], axis=0)
            maskt = (li >= si) if d == 0 else (li <= si)
            for r in range(HPG):
                j = 4 * d + r
                dc = dt[:, j:j + 1]
                lmt = jnp.where(maskt, jnp.exp(lat_[j:j + 1, :] - la[:, j:j + 1]), 0.0)
                wt = st * lmt * dc
                lhs = jnp.concatenate([wt, bm * w_end[:, j:j + 1]], axis=1).astype(BF16)
                hm = (lane >> 6) == r
                dxa = dxa + jnp.where(hm, _dot(lhs, rhs), 0.0)
                dwt = _dot_nt(jnp.where(hm, x, 0.0).astype(BF16), dyb_)
                dl = dwt * lmt
                gpt = dl * st
                cs = jnp.sum(gpt, axis=1, keepdims=True)
                ddir = ddir + jnp.where(lane128 == j, cs, 0.0)
                dlam = dlam - jnp.where(lane128 == j, cs * dc, 0.0)
                rows = rows + jnp.where(sub == j, jnp.sum(gpt * dc, axis=0, keepdims=True), 0.0)
                dst = dst + dl * dc
        dlam = dlam + rows.T
        dba = dba + _dot(dst.astype(BF16), cb_)
        dca = dca + _dot(dst.T.astype(BF16), bb_)
        isb = jnp.logical_and(lane128 >= 4, lane128 < 8)
        ddel = jnp.where(isb, _dot2_l(_tri(True), dlam), _dot2_l(_tri(False), dlam))
        a_l = a_ref[pl.ds(g, 1), :]
        ddt_ref[gg] = ddir + a_l * ddel
        ga_ref[gg] += jnp.broadcast_to(a_l * jnp.sum(dt * ddel, axis=0, keepdims=True), (8, 128))
        dx_ref[:, cols] = dxa + dy_ * sk_ref[:, cols]
        dbo_ref[:, cols128] = dba
        dco_ref[:, cols128] = dca

    nb = NG // gps
    st3 = pl.BlockSpec((None, 128, 256 * gps), lambda g, c: (c, 0, g))
    sm = pl.BlockSpec((gps, Q, 128), lambda g, c: (g, c, 0))
    smt = pl.BlockSpec((8 * gps, Q), lambda g, c: (g, c))
    wide = pl.BlockSpec((Q, 256 * gps), lambda g, c: (c, g))
    return pl.pallas_call(
        kern, name="ssd_bwd_out", grid=(nb, nc),
        in_specs=[wide, pl.BlockSpec((Q, 128 * gps), lambda g, c: (c, 2 * nb + g)),
                  pl.BlockSpec((Q, 128 * gps), lambda g, c: (c, 3 * nb + g)), wide,
                  pl.BlockSpec((1, 256 * gps), lambda g, c: (0, g)), sm, sm, smt, smt, st3, st3, st3, st3,
                  pl.BlockSpec((8, 128), lambda g, c: (0, 0))],
        out_specs=[wide, pl.BlockSpec((Q, 128 * gps), lambda g, c: (c, g)),
                   pl.BlockSpec((Q, 128 * gps), lambda g, c: (c, g)), sm, pl.BlockSpec((gps, 8, 128), lambda g, c: (g, 0, 0))],
        out_shape=[jax.ShapeDtypeStruct((lext, DI), F32), jax.ShapeDtypeStruct((lext, NG * NS), F32),
                   jax.ShapeDtypeStruct((lext, NG * NS), F32), jax.ShapeDtypeStruct((NG, lext, 128), F32),
                   jax.ShapeDtypeStruct((NG, 8, 128), F32)],
        compiler_params=_cp(("parallel", "arbitrary")),
    )(xbc, xbc, xbc, dy, dsk, dtg, lag, dtt, lat, htf, htb, dhf, dhb, a_rows)


def _post_fwd(yssm, xbc, proj_rest, dsk, gnw, nct):
    l = yssm.shape[0]

    def kern(y_ref, x_ref, z_ref, dsk_ref, w_ref, o_ref):
        y = y_ref[...] + dsk_ref[...] * x_ref[...]
        yz = y * _silu(z_ref[...])
        for g in range(NG):
            sl = slice(256 * g, 256 * (g + 1))
            v = yz[:, sl]
            r = lax.rsqrt(jnp.mean(v * v, axis=-1, keepdims=True) + EPS)
            o_ref[:, sl] = (v * r * w_ref[:, sl]).astype(BF16)

    return pl.pallas_call(
        kern, name="post_fwd", grid=(l // T,),
        in_specs=[pl.BlockSpec((T, DI), lambda i: (i, 0)), pl.BlockSpec((T, DI), lambda i: (i + nct, 0)),
                  pl.BlockSpec((T, DI), lambda i: (i, 0)), pl.BlockSpec((1, DI), lambda i: (0, 0)),
                  pl.BlockSpec((1, DI), lambda i: (0, 0))],
        out_specs=pl.BlockSpec((T, DI), lambda i: (i, 0)),
        out_shape=jax.ShapeDtypeStruct((l, DI), BF16),
        compiler_params=_cp(("parallel",)),
    )(yssm, xbc, proj_rest, dsk, gnw)


def _post_bwd(dgn, yssm, xbc, proj_rest, dsk, gnw, dpr, nct):
    l = yssm.shape[0]
    lext = xbc.shape[0]
    xi = lambda i: (jnp.maximum(i - nct, 0), 0)

    def kern(dg_ref, y_ref, x_ref, z_ref, dsk_ref, w_ref, dpr_ref, dy_ref, dz_ref, gw_ref, gd_ref):
        i = pl.program_id(0)

        @pl.when(i == 0)
        def _():
            gw_ref[...] = jnp.zeros_like(gw_ref)
            gd_ref[...] = jnp.zeros_like(gd_ref)

        @pl.when(i < nct)
        def _():
            dy_ref[...] = jnp.zeros_like(dy_ref)

        @pl.when(i >= nct)
        def _():
            xs = x_ref[...]
            z = z_ref[...]
            y = y_ref[...] + dsk_ref[...] * xs
            sz = _silu(z)
            yz = y * sz
            dgn_ = dg_ref[...]
            dyz_parts = []
            gws = []
            for g in range(NG):
                sl = slice(256 * g, 256 * (g + 1))
                v = yz[:, sl]
                r = lax.rsqrt(jnp.mean(v * v, axis=-1, keepdims=True) + EPS)
                vn = v * r
                dn = dgn_[:, sl] * w_ref[:, sl]
                gws.append(jnp.sum(dgn_[:, sl] * vn, axis=0, keepdims=True))
                dyz_parts.append(r * (dn - vn * jnp.mean(dn * vn, axis=-1, keepdims=True)))
            dyz = jnp.concatenate(dyz_parts, axis=1)
            gw_ref[...] += jnp.broadcast_to(jnp.concatenate(gws, axis=1), (8, DI))
            dy = dyz * sz
            dz_ref[...] = (dyz * y * _dsilu(z)).astype(BF16)
            gd_ref[...] += jnp.broadcast_to(jnp.sum(dy * xs, axis=0, keepdims=True), (8, DI))
            dy_ref[...] = dy

    return pl.pallas_call(
        kern, name="post_bwd", grid=(lext // T,),
        in_specs=[pl.BlockSpec((T, DI), xi), pl.BlockSpec((T, DI), xi), pl.BlockSpec((T, DI), lambda i: (i, 0)),
                  pl.BlockSpec((T, DI), xi), pl.BlockSpec((1, DI), lambda i: (0, 0)), pl.BlockSpec((1, DI), lambda i: (0, 0)),
                  pl.BlockSpec(memory_space=pl.ANY)],
        out_specs=[pl.BlockSpec((T, DI), lambda i: (i, 0)),
                   pl.BlockSpec((T, DI), xi), pl.BlockSpec((8, DI), lambda i: (0, 0)), pl.BlockSpec((8, DI), lambda i: (0, 0))],
        out_shape=[jax.ShapeDtypeStruct((lext, DI), F32),
                   jax.ShapeDtypeStruct((l, RESTW), BF16), jax.ShapeDtypeStruct((8, DI), F32), jax.ShapeDtypeStruct((8, DI), F32)],
        input_output_aliases={6: 1},
        compiler_params=_cp(("arbitrary",)),
    )(dgn, yssm, xbc, proj_rest, dsk, gnw, dpr)


C_G1, C_G2, C_GA, C_GB, C_CG = 2, 3, 4, 5, 6
PITCH = GW + 16
NROW = T // GW


GAP = PITCH - GW
PADR = GAP + NROW * PITCH
NSTRIP = D // 128


def _fill_padded(pad, val):
    z = jnp.zeros((GAP, D), F32)
    pad[0:GAP] = z
    for r in range(NROW):
        pad[GAP + PITCH * r:GAP + PITCH * r + GW] = val[GW * r:GW * (r + 1)]
        pad[PITCH * (r + 1):PITCH * (r + 1) + GAP] = z


def _row_conv(out_ref, pad, w_ref, transpose):
    def strip(s, carry):
        ln = pl.ds(pl.multiple_of(s * 128, 128), 128)
        for r in range(NROW):
            base = GAP + PITCH * r
            acc = jnp.zeros((GW, 128), F32)
            for k in range(CK):
                off = (k - 15) if not transpose else (15 - k)
                acc = acc + w_ref[pl.ds(k, 1), ln] * pad[pl.ds(base + off, GW), ln]
            out_ref[pl.ds(GW * r, GW), ln] = acc
        return carry

    lax.fori_loop(0, NSTRIP, strip, 0)


def _row_conv_wgrad(gcw_ref, padd, pada):
    def strip(s, carry):
        ln = pl.ds(pl.multiple_of(s * 128, 128), 128)
        accs = [jnp.zeros((8, 128), F32) for _ in range(CK)]
        for r in range(NROW):
            base = GAP + PITCH * r
            d = padd[pl.ds(base, GW), ln]
            for k in range(CK):
                p = d * pada[pl.ds(base + k - 15, GW), ln]
                part = p[0:8]
                for q in range(1, GW // 8):
                    part = part + p[8 * q:8 * (q + 1)]
                accs[k] = accs[k] + part
        rid = _iota((32, 128), 0)
        g = jnp.zeros((32, 128), F32)
        for k in range(CK):
            g = jnp.where(rid == k, jnp.sum(accs[k], axis=0, keepdims=True), g)
        gcw_ref[:, ln] += g
        return carry

    lax.fori_loop(0, NSTRIP, strip, 0)


def _ln_stats(cv):
    mu = jnp.mean(cv, axis=-1, keepdims=True)
    xc = cv - mu
    rs = lax.rsqrt(jnp.mean(xc * xc, axis=-1, keepdims=True) + EPS)
    return xc * rs, rs


def _conf_fwd(proj_rest, cw, cb, lw, lb):
    l = proj_rest.shape[0]

    def kern(ga_ref, gb_ref, cg_ref, cw_ref, cb_ref, lw_ref, lb_ref, o_ref, cv_ref):
        a = ga_ref[...] * _sig(gb_ref[...])
        z = jnp.zeros((GAP, D), F32)
        parts = []
        for r in range(NROW):
            parts += [a[GW * r:GW * (r + 1)], z]
        p = jnp.concatenate(parts, axis=0)
        n = p.shape[0]
        w = cw_ref[...]
        acc = w[15:16] * p
        for k in range(CK):
            if k != 15:
                acc = acc + w[k:k + 1] * pltpu.roll(p, (15 - k) % n, axis=0)
        cv = jnp.concatenate([acc[PITCH * r:PITCH * r + GW] for r in range(NROW)], axis=0) + cb_ref[...]
        cv_ref[...] = cv
        xh, _ = _ln_stats(cv)
        ln = xh * lw_ref[...] + lb_ref[...]
        o_ref[...] = (_silu(ln) * _silu(cg_ref[...])).astype(BF16)

    vec = pl.BlockSpec((1, D), lambda i: (0, 0))
    blk = pl.BlockSpec((T, D), lambda i: (i, 0))
    return pl.pallas_call(
        kern, name="conf_fwd", grid=(l // T,),
        in_specs=[pl.BlockSpec((T, D), lambda i: (i, C_GA)), pl.BlockSpec((T, D), lambda i: (i, C_GB)),
                  pl.BlockSpec((T, D), lambda i: (i, C_CG)), pl.BlockSpec((32, D), lambda i: (0, 0)), vec, vec, vec],
        out_specs=[blk, blk],
        out_shape=[jax.ShapeDtypeStruct((l, D), BF16), jax.ShapeDtypeStruct((l, D), F32)],
        compiler_params=_cp(("parallel",)),
    )(proj_rest, proj_rest, proj_rest, cw, cb, lw, lb)


def _conf_bwd(duc, cv, proj_rest, cw, lw, lb, dpr):
    l = proj_rest.shape[0]

    def kern(du_ref, cv_ref, ga_ref, gb_ref, cg_ref, cw_ref, lw_ref, lb_ref, dpr_ref, o_ref, gcw_ref, gv_ref, sc,
             pada, padd, da_ref):
        i, j = pl.program_id(0), pl.program_id(1)

        @pl.when(jnp.logical_and(i == 0, j == 0))
        def _():
            gcw_ref[...] = jnp.zeros_like(gcw_ref)
            gv_ref[...] = jnp.zeros_like(gv_ref)

        @pl.when(j == 0)
        def _():
            ga, gb, cg = ga_ref[...], gb_ref[...], cg_ref[...]
            sg = _sig(gb)
            xh, rs = _ln_stats(cv_ref[...])
            ln = xh * lw_ref[...] + lb_ref[...]
            du = du_ref[...]
            sc[:, 2 * D:3 * D] = (du * _silu(ln) * _dsilu(cg)).astype(BF16)
            dln = du * _silu(cg) * _dsilu(ln)
            g_lw = jnp.sum(dln * xh, axis=0, keepdims=True)
            g_lb = jnp.sum(dln, axis=0, keepdims=True)
            dxh = dln * lw_ref[...]
            dcv = rs * (dxh - jnp.mean(dxh, axis=-1, keepdims=True) - xh * jnp.mean(dxh * xh, axis=-1, keepdims=True))
            g_cb = jnp.sum(dcv, axis=0, keepdims=True)
            rid = _iota((8, D), 0)
            gv_ref[...] += jnp.where(rid == 0, g_cb, jnp.where(rid == 1, g_lw, jnp.where(rid == 2, g_lb, 0.0)))
            _fill_padded(padd, dcv)
            _fill_padded(pada, ga * sg)
            _row_conv(da_ref, padd, cw_ref, True)
            _row_conv_wgrad(gcw_ref, padd, pada)
            da = da_ref[...]
            sc[:, 0:D] = (da * sg).astype(BF16)
            sc[:, D:2 * D] = (da * ga * sg * (1.0 - sg)).astype(BF16)

        o_ref[...] = sc[:, pl.ds(pl.multiple_of(j * D, 128), D)]

    vec = pl.BlockSpec((1, D), lambda i, j: (0, 0))
    col = lambda c: pl.BlockSpec((T, D), lambda i, j: (i, c))
    return pl.pallas_call(
        kern, name="conf_bwd", grid=(l // T, 3),
        in_specs=[col(0), col(0), col(C_GA), col(C_GB), col(C_CG), pl.BlockSpec((32, D), lambda i, j: (0, 0)), vec, vec,
                  pl.BlockSpec(memory_space=pl.ANY)],
        out_specs=[pl.BlockSpec((T, D), lambda i, j: (i, C_GA + j)), pl.BlockSpec((32, D), lambda i, j: (0, 0)),
                   pl.BlockSpec((8, D), lambda i, j: (0, 0))],
        out_shape=[jax.ShapeDtypeStruct((l, RESTW), BF16), jax.ShapeDtypeStruct((32, D), F32),
                   jax.ShapeDtypeStruct((8, D), F32)],
        scratch_shapes=[pltpu.VMEM((T, 3 * D), BF16), pltpu.VMEM((PADR, D), F32), pltpu.VMEM((PADR, D), F32),
                        pltpu.VMEM((T, D), F32)],
        input_output_aliases={8: 0},
        compiler_params=_cp(("arbitrary", "arbitrary")),
    )(duc, cv, proj_rest, proj_rest, proj_rest, cw, lw, lb, dpr)


def _merge_fwd(bs, bc, proj_rest):
    l = bs.shape[0]

    def kern(bs_ref, bc_ref, g1_ref, g2_ref, o_ref):
        o_ref[...] = (_sig(g1_ref[...]) * bs_ref[...] + _sig(g2_ref[...]) * bc_ref[...]).astype(BF16)

    blk = pl.BlockSpec((T, D), lambda i: (i, 0))
    return pl.pallas_call(
        kern, name="merge_fwd", grid=(l // T,),
        in_specs=[blk, blk, pl.BlockSpec((T, D), lambda i: (i, C_G1)), pl.BlockSpec((T, D), lambda i: (i, C_G2))],
        out_specs=blk, out_shape=jax.ShapeDtypeStruct((l, D), BF16),
        compiler_params=_cp(("parallel",)),
    )(bs, bc, proj_rest, proj_rest)


def _merge_bwd(dm, bs, bc, proj_rest):
    l = bs.shape[0]

    def kern(dm_ref, bs_ref, bc_ref, g1_ref, g2_ref, dbs_ref, dbc_ref, dg_ref):
        dm_ = dm_ref[...]
        s1, s2 = _sig(g1_ref[...]), _sig(g2_ref[...])
        dbs_ref[...] = (dm_ * s1).astype(BF16)
        dbc_ref[...] = (dm_ * s2).astype(BF16)
        dg_ref[:, 0:D] = (dm_ * bs_ref[...] * s1 * (1.0 - s1)).astype(BF16)
        dg_ref[:, D:2 * D] = (dm_ * bc_ref[...] * s2 * (1.0 - s2)).astype(BF16)

    blk = pl.BlockSpec((T, D), lambda i: (i, 0))
    return pl.pallas_call(
        kern, name="merge_bwd", grid=(l // T,),
        in_specs=[blk, blk, blk, pl.BlockSpec((T, D), lambda i: (i, C_G1)), pl.BlockSpec((T, D), lambda i: (i, C_G2))],
        out_specs=[blk, blk, pl.BlockSpec((T, 2 * D), lambda i: (i, 1))],
        out_shape=[jax.ShapeDtypeStruct((l, D), BF16), jax.ShapeDtypeStruct((l, D), BF16),
                   jax.ShapeDtypeStruct((l, RESTW), BF16)],
        compiler_params=_cp(("parallel",)),
    )(dm, bs, bc, proj_rest, proj_rest)


def _final(x, out, tgt, mod, fw):
    l = x.shape[0]

    def kern(x_ref, o_ref, t_ref, mod_ref, fw_ref, ls_ref, dx2_ref, do_ref, gv_ref):
        @pl.when(pl.program_id(0) == 0)
        def _():
            ls_ref[...] = jnp.zeros_like(ls_ref)
            gv_ref[...] = jnp.zeros_like(gv_ref)

        gate = mod_ref[0:1, 2 * D:3 * D]
        o = o_ref[...]
        x2 = x_ref[...] + gate * o
        r = lax.rsqrt(jnp.mean(x2 * x2, axis=-1, keepdims=True) + EPS)
        yn = x2 * r
        fw_ = fw_ref[...]
        e = yn * fw_ - t_ref[...]
        ls_ref[...] += jnp.full((8, 128), 1.0, F32) * (0.5 / D) * jnp.sum(e * e)
        dy = e * (1.0 / D)
        g_fw = jnp.sum(dy * yn, axis=0, keepdims=True)
        dyn = dy * fw_
        dx2 = r * (dyn - yn * jnp.mean(dyn * yn, axis=-1, keepdims=True))
        g_gate = jnp.sum(dx2 * o, axis=0, keepdims=True)
        rid = _iota((8, D), 0)
        gv_ref[...] += jnp.where(rid == 0, g_fw, jnp.where(rid == 1, g_gate, 0.0))
        dx2_ref[...] = dx2
        do_ref[...] = (dx2 * gate).astype(BF16)

    blk = pl.BlockSpec((T, D), lambda i: (i, 0))
    return pl.pallas_call(
        kern, name="final", grid=(l // T,),
        in_specs=[blk, blk, blk, pl.BlockSpec((8, 3 * D), lambda i: (0, 0)), pl.BlockSpec((1, D), lambda i: (0, 0))],
        out_specs=[pl.BlockSpec((8, 128), lambda i: (0, 0)), blk, blk, pl.BlockSpec((8, D), lambda i: (0, 0))],
        out_shape=[jax.ShapeDtypeStruct((8, 128), F32), jax.ShapeDtypeStruct((l, D), F32),
                   jax.ShapeDtypeStruct((l, D), BF16), jax.ShapeDtypeStruct((8, D), F32)],
        compiler_params=_cp(("arbitrary",)),
    )(x, out, tgt, mod, fw)


def _perm_dt_cols(w):
    s = w.shape[:-1]
    return w.reshape(*s, 2, NG, HPG).swapaxes(-3, -2).reshape(*s, 64)


def _unperm_dt_cols(w):
    s = w.shape[:-1]
    return w.reshape(*s, NG, 2, HPG).swapaxes(-3, -2).reshape(*s, 64)


def _pad_lanes(v, width):
    return jnp.pad(v, ((0, 0), (0, width - v.shape[1])))


def _local_step(x, c, ctx, tgt, w):
    l = x.shape[0]
    nct = CTX // T
    ncc = CTX // Q
    lext = l + CTX

    w_mod = w["w_mod"].astype(BF16)
    w_in = w["w_in"].astype(BF16)
    w_ssd = jnp.concatenate([w_in[:, :XBC], _perm_dt_cols(w_in[:, XBC:XBC + 64]), jnp.zeros((D, 64), BF16)], axis=1)
    wr = w_in[:, XBC + 64:]
    w_rest = jnp.concatenate([wr[:, :DI], wr[:, DI + 3 * D:], wr[:, DI:DI + 3 * D]], axis=1)
    w_os, w_oc, w_o = w["w_out_ssm"].astype(BF16), w["w_out_conf"].astype(BF16), w["w_out"].astype(BF16)
    cw8 = jnp.pad(w["ssm_conv_w"], ((0, 4), (0, 0)))
    cb_s = w["ssm_conv_b"].reshape(1, XBC)
    dtb = _pad_lanes(_perm_dt_cols(w["dt_bias"].reshape(1, 64)), 128)
    a_all = -jnp.exp(w["a_log"].reshape(1, 64))
    a_perm = _pad_lanes(_perm_dt_cols(a_all), 128)
    a_rows = _pad_lanes(_perm_dt_cols(a_all).reshape(NG, 8), 128)
    dsk = jnp.repeat(w["d_skip"].reshape(NH), HP).reshape(1, DI)
    gnw = w["ssm_norm_w"].reshape(1, DI)
    ccw = jnp.pad(w["conf_conv_w"], ((0, 1), (0, 0)))
    ccb, clw, clb = w["conf_conv_b"].reshape(1, D), w["conf_ln_w"].reshape(1, D), w["conf_ln_b"].reshape(1, D)
    nw = w["norm_w"].reshape(1, D)
    fw = w["final_norm_w"].reshape(1, D)
    cc = jnp.concatenate([c.reshape(1, D), w["c_ctx"].reshape(1, D), jnp.zeros((6, D), F32)], axis=0)

    bx = 512
    be = 768 if lext % 768 == 0 else 256
    tk = min(1024, l)
    mod = _mod_fwd(cc, w_mod, w["b_mod"].reshape(1, 3 * D))
    h = _norm_fwd(ctx, x, mod, nw, nct)
    hx = h[CTX:]
    proj_ssd = _mm(h, w_ssd, "nn", lext, SSDW, D, be, SSDW // 3, D, F32, "proj_ssd")
    proj_rest = _mm(hx, w_rest, "nn", l, RESTW, D, bx, 1024, D, F32, "proj_rest")
    xbc = _conv_fwd(proj_ssd, cw8, cb_s, nct)
    dtg, lag, dtt, lat = _dt_fwd(proj_ssd, dtb, a_perm)
    htf, htb = _ssd_state(xbc, dtg, lag, ncc)
    yssm = _ssd_out(xbc, dtg, lag, dtt, lat, htf, htb, ncc)
    gn = _post_fwd(yssm, xbc, proj_rest, dsk, gnw, nct)
    bs = _mm(gn, w_os, "nn", l, D, DI, bx, D, DI, F32, "out_ssm")
    uc, cv = _conf_fwd(proj_rest, ccw, ccb, clw, clb)
    bc = _mm(uc, w_oc, "nn", l, D, D, bx, D, D, F32, "out_conf")
    merged = _merge_fwd(bs, bc, proj_rest)
    out = _mm(merged, w_o, "nn", l, D, D, bx, D, D, F32, "out_proj")
    lsum, dx2, dout, gv_fin = _final(x, out, tgt, mod, fw)

    g = {}
    g["final_norm_w"] = gv_fin[0]
    dmerged = _mm(dout, w_o, "nt", l, D, D, bx, D, D, F32, "d_merged")
    g["w_out"] = _mm(merged, dout, "tn", D, D, l, D, D, tk, F32, "g_w_out")
    dbs, dbc, dpr = _merge_bwd(dmerged, bs, bc, proj_rest)
    dgn = _mm(dbs, w_os, "nt", l, DI, D, bx, DI, D, F32, "d_gn")
    g["w_out_ssm"] = _mm(gn, dbs, "tn", DI, D, l, DI, D, tk, F32, "g_w_out_ssm")
    duc = _mm(dbc, w_oc, "nt", l, D, D, bx, D, D, F32, "d_uc")
    g["w_out_conf"] = _mm(uc, dbc, "tn", D, D, l, D, D, tk, F32, "g_w_out_conf")
    dpr, gcw, gv_conf = _conf_bwd(duc, cv, proj_rest, ccw, clw, clb, dpr)
    g["conf_conv_w"] = gcw[:CK]
    g["conf_conv_b"], g["conf_ln_w"], g["conf_ln_b"] = gv_conf[0], gv_conf[1], gv_conf[2]
    dy, dproj_rest, ggnw, gdsk = _post_bwd(dgn, yssm, xbc, proj_rest, dsk, gnw, dpr, nct)
    g["ssm_norm_w"] = ggnw[0]
    g["d_skip"] = gdsk[0].reshape(NH, HP).sum(axis=1)
    dhf, dhb = _ssd_bwd_state(xbc, dy, lag, ncc)
    dxs, dbm, dcm, ddtg, galog = _ssd_bwd_out(xbc, dy, dsk, dtg, lag, dtt, lat, htf, htb, dhf, dhb, a_rows)
    g["a_log"] = _unperm_dt_cols(galog[:, 0, 0:8].reshape(1, 64)).reshape(2, NH)
    dus, gws, gbs = [], [], []
    for dpost, off, width, nm in ((dxs, 0, DI, "conv_bwd_x"), (dbm, DI, NG * NS, "conv_bwd_b"), (dcm, DI + NG * NS, NG * NS, "conv_bwd_c")):
        du_, gw_, gb_ = _conv_bwd(dpost, proj_ssd, cw8, cb_s, off, width, nct, nm)
        dus.append(du_)
        gws.append(gw_[:SK])
        gbs.append(gb_[0])
    g["ssm_conv_w"] = jnp.concatenate(gws, axis=1)
    g["ssm_conv_b"] = jnp.concatenate(gbs, axis=0)
    ddt_raw, gdtb = _dt_bwd(ddtg, proj_ssd, dtb)
    g["dt_bias"] = _unperm_dt_cols(gdtb[0:1, 0:64]).reshape(2, NH)
    dproj_ssd = jnp.concatenate(dus + [ddt_raw], axis=1)
    gw_ssd = _mm(h, dproj_ssd, "tn", D, SSDW, lext, D, SSDW // 3, be, F32, "g_w_ssd")
    gw_rest = _mm(hx, dproj_rest, "tn", D, RESTW, l, D, 1024, tk, F32, "g_w_rest")
    g["w_in"] = jnp.concatenate([gw_ssd[:, :XBC], _unperm_dt_cols(gw_ssd[:, XBC:XBC + 64]), gw_rest[:, :DI],
                                 gw_rest[:, 2 * DI:], gw_rest[:, DI:2 * DI]], axis=1)
    dh_a = _mm(dproj_ssd, w_ssd, "nt", lext, D, SSDW, T, D, SSDW, F32, "dh_ssd")
    dh_b = _mm(dproj_rest, w_rest, "nt", l, D, RESTW, T, D, RESTW, F32, "dh_rest")
    grad_x, gnw_in, dss = _norm_bwd(dh_a, dh_b, ctx, x, dx2, mod, nw, nct)
    g["norm_w"] = gnw_in[0]
    dmod = jnp.concatenate([jnp.concatenate([dss[0:1], gv_fin[1:2]], axis=1),
                            jnp.concatenate([dss[1:2], jnp.zeros((1, D), F32)], axis=1),
                            jnp.zeros((6, 3 * D), F32)], axis=0)
    gwm, gbm, gcc = _mod_bwd(dmod, cc, cc.T, w_mod)
    g["w_mod"], g["b_mod"], g["c_ctx"] = gwm, gbm[0], gcc[1]
    return lsum[0, 0], grad_x, g


NSHARD = 4
R_MOD, R_IN, R_OS, R_OC, R_O, R_SC, R_CC = 768, 2832, 512, 256, 256, 8, 8
O_MOD = 0
O_OS = O_MOD + R_MOD
O_OC = O_OS + R_OS
O_O = O_OC + R_OC
O_SC = O_O + R_O
O_CC = O_SC + R_SC
PUSED = O_CC + R_CC
PROWS = 1824
HALF = PROWS // 2
RB = HALF // 3
WB = 128
SROWS = 16
SHARDED = ("w_mod", "w_in", "w_out_ssm", "w_out_conf", "w_out", "ssm_conv_w", "conf_conv_w")
SMALL = (("b_mod", 3 * D), ("norm_w", D), ("ssm_conv_b", XBC), ("dt_bias", 64), ("a_log", 64), ("d_skip", NH),
         ("ssm_norm_w", DI), ("conf_conv_b", D), ("conf_ln_w", D), ("conf_ln_b", D), ("final_norm_w", D), ("c_ctx", D))
SMALL_OFF = {"b_mod": 0, "norm_w": 3 * D, "ssm_conv_b": 4 * D, "dt_bias": 8 * D, "a_log": 8 * D + 64, "d_skip": 8 * D + 128,
             "ssm_norm_w": 9 * D, "conf_conv_b": 11 * D, "conf_ln_w": 12 * D, "conf_ln_b": 13 * D, "final_norm_w": 14 * D,
             "c_ctx": 15 * D}


def _pack_shard(s):
    return jnp.concatenate([s["w_mod"].reshape(R_MOD, D), _pack_rest(s), jnp.zeros((PROWS - PUSED, D), F32)], axis=0)


def _pack_rest(s):
    cc = jnp.pad(s["conf_conv_w"].reshape(1, CK * 256), ((0, 0), (0, R_CC * D - CK * 256))).reshape(R_CC, D)
    return jnp.concatenate([s["w_out_ssm"], s["w_out_conf"], s["w_out"],
                            jnp.pad(s["ssm_conv_w"], ((0, R_SC - SK), (0, 0))), cc], axis=0)


def _unpack_rest(p):
    o = lambda r: r - O_OS
    return {"w_out_ssm": p[o(O_OS):o(O_OC)][None], "w_out_conf": p[o(O_OC):o(O_O)][None], "w_out": p[o(O_O):o(O_SC)][None],
            "ssm_conv_w": p[o(O_SC):o(O_SC) + SK][None],
            "conf_conv_w": p[o(O_CC):o(O_CC) + R_CC].reshape(R_CC * D)[:CK * 256].reshape(1, CK, 256)}


def _shard_cols(a, n):
    return a.reshape(a.shape[0], NSHARD, n).transpose(1, 0, 2)


def _pack_full(g):
    cc = jnp.pad(_shard_cols(g["conf_conv_w"], 256).reshape(NSHARD, CK * 256), ((0, 0), (0, R_CC * D - CK * 256)))
    return jnp.concatenate([_shard_cols(g["w_mod"], R_MOD).reshape(NSHARD, R_MOD, D),
                            g["w_out_ssm"].reshape(NSHARD, R_OS, D), g["w_out_conf"].reshape(NSHARD, R_OC, D),
                            g["w_out"].reshape(NSHARD, R_O, D),
                            jnp.pad(_shard_cols(g["ssm_conv_w"], D), ((0, 0), (0, R_SC - SK), (0, 0))),
                            cc.reshape(NSHARD, R_CC, D), jnp.zeros((NSHARD, PROWS - PUSED, D), F32)], axis=1)


def _unpack_gathered(gm, gw, gs):
    def cols(a, r, n):
        return a.reshape(NSHARD, r, n).transpose(1, 0, 2).reshape(r, NSHARD * n)
    return {"w_mod": cols(gm[:, O_MOD:O_OS], D, R_MOD), "w_in": cols(gw, D, R_IN),
            "w_out_ssm": gm[:, O_OS:O_OC].reshape(DI, D), "w_out_conf": gm[:, O_OC:O_O].reshape(D, D),
            "w_out": gm[:, O_O:O_SC].reshape(D, D), "ssm_conv_w": cols(gs[:, 0:SK], SK, D),
            "conf_conv_w": cols(gs[:, R_SC:R_SC + R_CC].reshape(NSHARD, R_CC * D)[:, :CK * 256], CK, 256)}


def _pack_small(d):
    flat = jnp.zeros((SROWS * D,), F32)
    for name, n in SMALL:
        flat = lax.dynamic_update_slice(flat, d[name].reshape(n).astype(F32), (SMALL_OFF[name],))
    return flat.reshape(SROWS, D)


def _unpack_small(p, shapes):
    flat = p.reshape(SROWS * D)
    return {name: flat[SMALL_OFF[name]:SMALL_OFF[name] + n].reshape(shapes[name]) for name, n in SMALL}


MESH_ID = pl.DeviceIdType.MESH
ANY = pl.BlockSpec(memory_space=pl.ANY)


def _place():
    x, y, c = lax.axis_index("x"), lax.axis_index("y"), lax.axis_index("c")
    return x, y, c, [(1 - x, y), (x, 1 - y), (1 - x, 1 - y)]


def _rcopy(src, dst, send, recv, dev):
    return pltpu.make_async_remote_copy(src_ref=src, dst_ref=dst, send_sem=send, recv_sem=recv,
                                        device_id=dev, device_id_type=MESH_ID)


def _gather_weights(mats, small):
    n = len(mats)

    def kern(*refs):
        m_refs, s_ref, g_refs, gs_ref, (send, recv) = refs[:n], refs[n], refs[n + 1:2 * n + 1], refs[2 * n + 1], refs[2 * n + 2:]
        x, y, c, chips = _place()
        me = 2 * x + y
        sib = (x, y, 1 - c)
        first, passed = [], []
        for k, (px, py) in enumerate(chips):
            first.append(_rcopy(s_ref, gs_ref.at[me], send.at[k], recv.at[k], (px, py, c)))
            for a, (m_ref, g_ref) in enumerate(zip(m_refs, g_refs)):
                mine = _half_rows(c, m_ref.shape[0])
                first.append(_rcopy(m_ref.at[mine], g_ref.at[me, mine], send.at[3 + 6 * a + k], recv.at[3 + 6 * a + k], (px, py, c)))
        for cp in first:
            cp.start()
        for k, (px, py) in enumerate(chips):
            s = 2 * px + py
            for a, (m_ref, g_ref) in enumerate(zip(m_refs, g_refs)):
                mine = _half_rows(c, m_ref.shape[0])
                _rcopy(m_ref.at[mine], g_ref.at[s, mine], send.at[3 + 6 * a + k], recv.at[3 + 6 * a + k], sib).wait_recv()
                f = _rcopy(g_ref.at[s, mine], g_ref.at[s, mine], send.at[6 + 6 * a + k], recv.at[6 + 6 * a + k], sib)
                f.start()
                passed.append(f)
        for k, (px, py) in enumerate(chips):
            s = 2 * px + py
            _rcopy(s_ref, gs_ref.at[s], send.at[k], recv.at[k], sib).wait_recv()
            for a, g_ref in enumerate(g_refs):
                other = _half_rows(1 - c, g_ref.shape[1])
                _rcopy(g_ref.at[s, other], g_ref.at[s, other], send.at[6 + 6 * a + k], recv.at[6 + 6 * a + k], sib).wait_recv()
        for cp in first + passed:
            cp.wait_send()

    nsem = 3 + 6 * n
    return pl.pallas_call(
        kern, name="gather_weights", in_specs=[ANY] * (n + 1), out_specs=[ANY] * (n + 1),
        out_shape=[jax.ShapeDtypeStruct((NSHARD,) + m.shape, m.dtype) for m in mats]
        + [jax.ShapeDtypeStruct((NSHARD, SROWS, D), F32)],
        scratch_shapes=[pltpu.SemaphoreType.DMA((nsem,)), pltpu.SemaphoreType.DMA((nsem,))],
    )(*mats, small)


def _half_rows(c, rows):
    return pl.ds(pl.multiple_of(c * (rows // 2), 16), rows // 2)


def _swap_halves(gs):
    n = len(gs)

    def kern(*refs):
        g_refs, o_refs, (send, recv) = refs[:n], refs[n:2 * n], refs[2 * n:]
        x, y, c, _ = _place()
        cps = [_rcopy(g_ref.at[s, _half_rows(1 - c, g_ref.shape[1])], o_ref.at[s], send.at[NSHARD * a + s],
                      recv.at[NSHARD * a + s], (x, y, 1 - c))
               for a, (g_ref, o_ref) in enumerate(zip(g_refs, o_refs)) for s in range(NSHARD)]
        for cp in cps:
            cp.start()
        for cp in cps:
            cp.wait()

    return pl.pallas_call(
        kern, name="swap_halves", in_specs=[ANY] * n, out_specs=[ANY] * n,
        out_shape=[jax.ShapeDtypeStruct((NSHARD, g.shape[1] // 2, g.shape[2]), F32) for g in gs],
        scratch_shapes=[pltpu.SemaphoreType.DMA((NSHARD * n,)), pltpu.SemaphoreType.DMA((NSHARD * n,))],
    )(*gs)


def _add_halves(cidx, g, ra, rb, name):
    _, half, cols = ra.shape
    nb = half // rb

    def kern(c_ref, g_ref, a_ref, o_ref):
        o_ref[...] = (g_ref[...] + a_ref[...]).astype(BF16)

    return pl.pallas_call(
        kern, name=name,
        grid_spec=pltpu.PrefetchScalarGridSpec(
            num_scalar_prefetch=1, grid=(NSHARD, nb),
            in_specs=[pl.BlockSpec((None, rb, cols), lambda s, i, c: (s, c[0] * nb + i, 0)),
                      pl.BlockSpec((None, rb, cols), lambda s, i, c: (s, i, 0))],
            out_specs=pl.BlockSpec((None, rb, cols), lambda s, i, c: (s, i, 0))),
        out_shape=jax.ShapeDtypeStruct((NSHARD, half, cols), BF16),
        compiler_params=_cp(("parallel", "parallel")),
    )(cidx, g, ra)


def _exchange_chips(ps):
    n = len(ps)

    def kern(*refs):
        p_refs, o_refs, (send, recv) = refs[:n], refs[n:2 * n], refs[2 * n:]
        x, y, c, chips = _place()
        cps = [_rcopy(p_ref.at[2 * px + py], o_ref.at[k], send.at[3 * a + k], recv.at[3 * a + k], (px, py, c))
               for a, (p_ref, o_ref) in enumerate(zip(p_refs, o_refs)) for k, (px, py) in enumerate(chips)]
        for cp in cps:
            cp.start()
        for cp in cps:
            cp.wait()

    return pl.pallas_call(
        kern, name="exchange_chips", in_specs=[ANY] * n, out_specs=[ANY] * n,
        out_shape=[jax.ShapeDtypeStruct((3,) + p.shape[1:], p.dtype) for p in ps],
        scratch_shapes=[pltpu.SemaphoreType.DMA((3 * n,)), pltpu.SemaphoreType.DMA((3 * n,))],
    )(*ps)


def _add_chips(mc, g, ra, rx, rb, name):
    _, half, cols = ra.shape
    nb = half // rb

    def kern(m_ref, g_ref, a_ref, r0_ref, r1_ref, r2_ref, o_ref):
        own = g_ref[...] + a_ref[...]
        o_ref[...] = ((own + r0_ref[...].astype(F32)) + r1_ref[...].astype(F32)) + r2_ref[...].astype(F32)

    return pl.pallas_call(
        kern, name=name,
        grid_spec=pltpu.PrefetchScalarGridSpec(
            num_scalar_prefetch=1, grid=(nb,),
            in_specs=[pl.BlockSpec((None, rb, cols), lambda i, m: (m[0], m[1] * nb + i, 0)),
                      pl.BlockSpec((None, rb, cols), lambda i, m: (m[0], i, 0))]
            + [pl.BlockSpec((None, rb, cols), functools.partial(lambda i, m, k: (k, i, 0), k=k)) for k in range(3)],
            out_specs=pl.BlockSpec((rb, cols), lambda i, m: (i, 0))),
        out_shape=jax.ShapeDtypeStruct((half, cols), F32),
        compiler_params=_cp(("parallel",)),
    )(mc, g, ra, rx, rx, rx)


def _share_halves(rs):
    n = len(rs)

    def kern(*refs):
        r_refs, o_refs, (send, recv) = refs[:n], refs[n:2 * n], refs[2 * n:]
        x, y, c, _ = _place()
        cps = [_rcopy(r_ref, o_ref, send.at[a], recv.at[a], (x, y, 1 - c))
               for a, (r_ref, o_ref) in enumerate(zip(r_refs, o_refs))]
        for cp in cps:
            cp.start()
        for cp in cps:
            cp.wait()

    return pl.pallas_call(
        kern, name="share_halves", in_specs=[ANY] * n, out_specs=[ANY] * n,
        out_shape=[jax.ShapeDtypeStruct(r.shape, F32) for r in rs],
        scratch_shapes=[pltpu.SemaphoreType.DMA((n,)), pltpu.SemaphoreType.DMA((n,))],
    )(*rs)


def _reduce_small(s):
    def kern(s_ref, o_ref, buf, send, recv):
        x, y, c, _ = _place()
        me = 4 * x + 2 * y + c
        buf[me] = s_ref[...]
        cps = []
        for r in range(1, 8):
            peer = (1 - x if r & 4 else x, 1 - y if r & 2 else y, 1 - c if r & 1 else c)
            cps.append(_rcopy(s_ref, buf.at[me], send.at[r - 1], recv.at[r - 1], peer))
        for cp in cps:
            cp.start()
        for cp in cps:
            cp.wait()
        acc = buf[0]
        for i in range(1, 8):
            acc = acc + buf[i]
        o_ref[...] = acc

    return pl.pallas_call(
        kern, name="reduce_small",
        in_specs=[pl.BlockSpec(memory_space=pltpu.VMEM)], out_specs=pl.BlockSpec(memory_space=pltpu.VMEM),
        out_shape=jax.ShapeDtypeStruct((SROWS, D), F32),
        scratch_shapes=[pltpu.VMEM((8, SROWS, D), F32), pltpu.SemaphoreType.DMA((7,)), pltpu.SemaphoreType.DMA((7,))],
    )(s)


def _adamw(g, w, m, v, rb, name):
    rows, cols = g.shape

    def kern(g_ref, w_ref, m_ref, v_ref, d_ref, nm_ref, nv_ref):
        g_ = g_ref[...]
        m_ = ADAM_B1 * m_ref[...] + (1.0 - ADAM_B1) * g_
        v_ = ADAM_B2 * v_ref[...] + (1.0 - ADAM_B2) * jnp.square(g_)
        m_hat = m_ / (1.0 - ADAM_B1 ** ADAM_STEP)
        v_hat = v_ / (1.0 - ADAM_B2 ** ADAM_STEP)
        d_ref[...] = -ADAM_LR * (m_hat / (jnp.sqrt(v_hat) + ADAM_EPS) + ADAM_WD * w_ref[...])
        nm_ref[...] = m_
        nv_ref[...] = v_

    assert rows % rb == 0
    blk = pl.BlockSpec((rb, cols), lambda i: (i, 0))
    return pl.pallas_call(
        kern, name=name, grid=(rows // rb,), in_specs=[blk] * 4, out_specs=[blk] * 3,
        out_shape=[jax.ShapeDtypeStruct((rows, cols), F32)] * 3,
        compiler_params=_cp(("parallel",)),
    )(g, w, m, v)


WEIGHTS = ("c_ctx", "w_mod", "b_mod", "norm_w", "w_in", "ssm_conv_w", "ssm_conv_b", "dt_bias", "a_log", "d_skip",
           "ssm_norm_w", "w_out_ssm", "conf_conv_w", "conf_conv_b", "conf_ln_w", "conf_ln_b", "w_out_conf", "w_out",
           "final_norm_w")


def kernel(x, c, ctx, c_ctx, w_mod, b_mod, norm_w, w_in, ssm_conv_w, ssm_conv_b, dt_bias, a_log, d_skip, ssm_norm_w, w_out_ssm, conf_conv_w, conf_conv_b, conf_ln_w, conf_ln_b, w_out_conf, w_out, final_norm_w, loss_target, m_c_ctx, m_w_mod, m_b_mod, m_norm_w, m_w_in, m_ssm_conv_w, m_ssm_conv_b, m_dt_bias, m_a_log, m_d_skip, m_ssm_norm_w, m_w_out_ssm, m_conf_conv_w, m_conf_conv_b, m_conf_ln_w, m_conf_ln_b, m_w_out_conf, m_w_out, m_final_norm_w, v_c_ctx, v_w_mod, v_b_mod, v_norm_w, v_w_in, v_ssm_conv_w, v_ssm_conv_b, v_dt_bias, v_a_log, v_d_skip, v_ssm_norm_w, v_w_out_ssm, v_conf_conv_w, v_conf_conv_b, v_conf_ln_w, v_conf_ln_b, v_w_out_conf, v_w_out, v_final_norm_w):
    wv = (c_ctx, w_mod, b_mod, norm_w, w_in, ssm_conv_w, ssm_conv_b, dt_bias, a_log, d_skip, ssm_norm_w, w_out_ssm,
          conf_conv_w, conf_conv_b, conf_ln_w, conf_ln_b, w_out_conf, w_out, final_norm_w)
    mv = (m_c_ctx, m_w_mod, m_b_mod, m_norm_w, m_w_in, m_ssm_conv_w, m_ssm_conv_b, m_dt_bias, m_a_log, m_d_skip,
          m_ssm_norm_w, m_w_out_ssm, m_conf_conv_w, m_conf_conv_b, m_conf_ln_w, m_conf_ln_b, m_w_out_conf, m_w_out,
          m_final_norm_w)
    vv = (v_c_ctx, v_w_mod, v_b_mod, v_norm_w, v_w_in, v_ssm_conv_w, v_ssm_conv_b, v_dt_bias, v_a_log, v_d_skip,
          v_ssm_norm_w, v_w_out_ssm, v_conf_conv_w, v_conf_conv_b, v_conf_ln_w, v_conf_ln_b, v_w_out_conf, v_w_out,
          v_final_norm_w)
    shapes = {n: a.shape for n, a in zip(WEIGHTS, wv)}

    def squeeze(d):
        return {n: (a if n in ("c_ctx", "final_norm_w") else a[0]) for n, a in d.items()}

    w, m, v = (squeeze(dict(zip(WEIGHTS, t))) for t in (wv, mv, vv))

    my_chip = 2 * lax.axis_index("x") + lax.axis_index("y")
    my_core = lax.axis_index("c")

    pw = _pack_shard(w)
    pwb, wib, psm = pw.astype(BF16), w["w_in"].astype(BF16), pw[O_SC:O_SC + SROWS]
    gm, gw, gs = _gather_weights([pwb, wib], psm)
    gm = lax.dynamic_update_slice(gm, pwb[None], (my_chip, 0, 0))
    gw = lax.dynamic_update_slice(gw, wib[None], (my_chip, 0, 0))
    gs = lax.dynamic_update_slice(gs, psm[None], (my_chip, 0, 0))
    full = dict(w)
    full.update(_unpack_gathered(gm, gw, gs))

    lsum, grad_x, g = _local_step(x[0], c, ctx[0], loss_target[0], full)
    loss = lax.psum(lsum, ("x", "y", "c"))

    cidx = my_core.astype(jnp.int32).reshape(1)
    mc = jnp.stack([my_chip, my_core]).astype(jnp.int32)
    gsrc = [_pack_full(g), _shard_cols(g["w_in"], R_IN)]
    blocks = (RB, WB)
    sib = _swap_halves(gsrc)
    part = [_add_halves(cidx, a, b, rb, "add_halves_%d" % i) for i, (a, b, rb) in enumerate(zip(gsrc, sib, blocks))]
    far = _exchange_chips(part)
    red = [_add_chips(mc, a, b, f, rb, "add_chips_%d" % i) for i, (a, b, f, rb) in enumerate(zip(gsrc, sib, far, blocks))]
    got = _share_halves(red)
    g_pk, g_win = (jnp.concatenate([jnp.where(my_core == 0, r, o), jnp.where(my_core == 0, o, r)], axis=0)
                   for r, o in zip(red, got))
    g_sm = _reduce_small(_pack_small(g))

    gr = {"w_mod": g_pk[O_MOD:O_OS].reshape(D, R_MOD), "w_in": g_win, "rest": g_pk[O_OS:PUSED]}
    wr, mr, vr = ({"w_mod": t["w_mod"], "w_in": t["w_in"], "rest": _pack_rest(t)} for t in (w, m, v))
    res = {k: _adamw(gr[k], wr[k], mr[k], vr[k], rb, "adamw_" + k)
           for k, rb in (("w_in", WB), ("w_mod", 512), ("rest", (PUSED - O_OS) // 2))}
    res_sm = _adamw(g_sm, _pack_small(w), _pack_small(m), _pack_small(v), SROWS, "adamw_small")

    outs = []
    for i in range(4):
        pick = (lambda k: gr[k]) if i == 0 else (lambda k: res[k][i - 1])
        d = {"w_mod": pick("w_mod")[None], "w_in": pick("w_in")[None]}
        d.update(_unpack_rest(pick("rest")))
        d.update(_unpack_small(g_sm if i == 0 else res_sm[i - 1], shapes))
        outs.extend(d[n] for n in WEIGHTS)
    return (loss, grad_x[None], *outs)
```

```python
import functools

import jax
import jax.numpy as jnp
from jax import lax
from jax.experimental import pallas as pl
from jax.experimental.pallas import tpu as pltpu

F32, BF16 = jnp.float32, jnp.bfloat16

D = 1024
DI = 2048
NH = 32
HP = 64
NG = 8
HPG = 4
NS = 128
Q = 128
GW = 64
CK = 31
SK = 4
CTX = 256
EPS = 1e-6
XBC = DI + 2 * NG * NS
SSDW = XBC + 128
RESTW = 7168
T = 256
VMEM_LIMIT = 56 * 1024 * 1024

ADAM_LR, ADAM_B1, ADAM_B2, ADAM_EPS, ADAM_WD, ADAM_STEP = 0.001, 0.9, 0.999, 1e-08, 0.01, 10


def _cp(sem):
    return pltpu.CompilerParams(dimension_semantics=sem, vmem_limit_bytes=VMEM_LIMIT)


def _sig(x):
    return jax.nn.sigmoid(x)


def _silu(x):
    return x * _sig(x)


def _dsilu(x):
    s = _sig(x)
    return s * (1.0 + x * (1.0 - s))


def _dot(a, b):
    return jnp.dot(a, b, preferred_element_type=F32)


def _dot_nt(a, b):
    return lax.dot_general(a, b, (((1,), (1,)), ((), ())), preferred_element_type=F32)


def _split3(x):
    h = x.astype(BF16)
    r = x - h.astype(F32)
    m = r.astype(BF16)
    l = (r - m.astype(F32)).astype(BF16)
    return h, m, l


def _dot3_l(sel, x):
    h, m, l = _split3(x)
    return _dot(sel, h) + _dot(sel, m) + _dot(sel, l)


def _dot3_r(x, sel):
    h, m, l = _split3(x)
    return _dot(h, sel) + _dot(m, sel) + _dot(l, sel)


def _split2(x):
    h = x.astype(BF16)
    return h, (x - h.astype(F32)).astype(BF16)


def _dot2_l(sel, x):
    h, l = _split2(x)
    return _dot(sel, h) + _dot(sel, l)


def _dot2_r(x, sel):
    h, l = _split2(x)
    return _dot(h, sel) + _dot(l, sel)


def _iota(shape, dim):
    return lax.broadcasted_iota(jnp.int32, shape, dim)


def _mm(a, b, dims, m, n, k, bm, bn, bk, out_dtype, name):
    nk = k // bk
    assert m % bm == 0 and n % bn == 0 and k % bk == 0, (name, m, n, k, bm, bn, bk)

    def prod(a_ref, b_ref):
        av = a_ref[...].astype(BF16)
        bv = b_ref[...].astype(BF16)
        if dims == "nn":
            return _dot(av, bv)
        if dims == "nt":
            return _dot_nt(av, bv)
        return lax.dot_general(av, bv, (((0,), (0,)), ((), ())), preferred_element_type=F32)

    def kern_one(a_ref, b_ref, o_ref):
        o_ref[...] = prod(a_ref, b_ref).astype(out_dtype)

    def kern_acc(a_ref, b_ref, o_ref, acc):
        kk = pl.program_id(2)

        @pl.when(kk == 0)
        def _():
            acc[...] = jnp.zeros_like(acc)

        acc[...] += prod(a_ref, b_ref)

        @pl.when(kk == nk - 1)
        def _():
            o_ref[...] = acc[...].astype(out_dtype)

    if dims == "nn":
        a_spec = pl.BlockSpec((bm, bk), lambda j, i, kk: (i, kk))
        b_spec = pl.BlockSpec((bk, bn), lambda j, i, kk: (kk, j))
    elif dims == "nt":
        a_spec = pl.BlockSpec((bm, bk), lambda j, i, kk: (i, kk))
        b_spec = pl.BlockSpec((bn, bk), lambda j, i, kk: (j, kk))
    else:
        a_spec = pl.BlockSpec((bk, bm), lambda j, i, kk: (kk, i))
        b_spec = pl.BlockSpec((bk, bn), lambda j, i, kk: (kk, j))
    return pl.pallas_call(
        kern_one if nk == 1 else kern_acc, name=name,
        grid=(n // bn, m // bm, nk),
        in_specs=[a_spec, b_spec],
        out_specs=pl.BlockSpec((bm, bn), lambda j, i, kk: (i, j)),
        out_shape=jax.ShapeDtypeStruct((m, n), out_dtype),
        scratch_shapes=[] if nk == 1 else [pltpu.VMEM((bm, bn), F32)],
        compiler_params=_cp(("parallel", "parallel", "arbitrary")),
    )(a, b)


def _mod_fwd(cc, w_mod, b_mod):
    def kern(cc_ref, w_ref, b_ref, o_ref):
        s = _silu(cc_ref[...]).astype(BF16)
        o_ref[...] = _dot(s, w_ref[...]) + b_ref[...]

    return pl.pallas_call(
        kern, name="mod_fwd", grid=(3,),
        in_specs=[pl.BlockSpec((8, D), lambda j: (0, 0)), pl.BlockSpec((D, D), lambda j: (0, j)),
                  pl.BlockSpec((1, D), lambda j: (0, j))],
        out_specs=pl.BlockSpec((8, D), lambda j: (0, j)),
        out_shape=jax.ShapeDtypeStruct((8, 3 * D), F32),
        compiler_params=_cp(("parallel",)),
    )(cc, w_mod, b_mod)


def _mod_bwd(dmod, cc, cct, w_mod):
    def kern(dm_ref, cc_ref, cct_ref, w_ref, gw_ref, gb_ref, gc_ref):
        kk = pl.program_id(0)
        dm = dm_ref[...]
        sct = _silu(cct_ref[...])
        gw_ref[...] = sct[:, 0:1] * dm[0:1, :] + sct[:, 1:2] * dm[1:2, :]
        gb_ref[...] = jnp.broadcast_to(dm[0:1, :] + dm[1:2, :], dm.shape)

        @pl.when(kk == 0)
        def _():
            gc_ref[...] = jnp.zeros_like(gc_ref)

        gc_ref[...] += _dot_nt(dm.astype(BF16), w_ref[...])

        @pl.when(kk == 2)
        def _():
            gc_ref[...] = gc_ref[...] * _dsilu(cc_ref[...])

    return pl.pallas_call(
        kern, name="mod_bwd", grid=(3,),
        in_specs=[pl.BlockSpec((8, D), lambda j: (0, j)), pl.BlockSpec((8, D), lambda j: (0, 0)),
                  pl.BlockSpec((D, 8), lambda j: (0, 0)), pl.BlockSpec((D, D), lambda j: (0, j))],
        out_specs=[pl.BlockSpec((D, D), lambda j: (0, j)), pl.BlockSpec((8, D), lambda j: (0, j)),
                   pl.BlockSpec((8, D), lambda j: (0, 0))],
        out_shape=[jax.ShapeDtypeStruct((D, 3 * D), F32), jax.ShapeDtypeStruct((8, 3 * D), F32),
                   jax.ShapeDtypeStruct((8, D), F32)],
        compiler_params=_cp(("arbitrary",)),
    )(dmod, cc, cct, w_mod)


def _ext_specs(nct):
    return (pl.BlockSpec((T, D), lambda i: (jnp.minimum(i, nct - 1), 0)),
            pl.BlockSpec((T, D), lambda i: (jnp.maximum(i - nct, 0), 0)))


def _norm_fwd(ctx, xl, mod, nw, nct):
    lext = ctx.shape[0] + xl.shape[0]

    def kern(c_ref, x_ref, mod_ref, nw_ref, h_ref):
        is_ctx = pl.program_id(0) < nct
        x = jnp.where(is_ctx, c_ref[...], x_ref[...])
        r = lax.rsqrt(jnp.mean(x * x, axis=-1, keepdims=True) + EPS)
        xn = x * r * nw_ref[...]
        shift = jnp.where(is_ctx, mod_ref[1:2, 0:D], mod_ref[0:1, 0:D])
        scale = jnp.where(is_ctx, mod_ref[1:2, D:2 * D], mod_ref[0:1, D:2 * D])
        h_ref[...] = (xn * (1.0 + scale) + shift).astype(BF16)

    return pl.pallas_call(
        kern, name="norm_fwd", grid=(lext // T,),
        in_specs=[*_ext_specs(nct), pl.BlockSpec((8, 3 * D), lambda i: (0, 0)),
                  pl.BlockSpec((1, D), lambda i: (0, 0))],
        out_specs=pl.BlockSpec((T, D), lambda i: (i, 0)),
        out_shape=jax.ShapeDtypeStruct((lext, D), BF16),
        compiler_params=_cp(("parallel",)),
    )(ctx, xl, mod, nw)


def _norm_bwd(dha, dhb, ctx, xl, dx2, mod, nw, nct):
    lext = ctx.shape[0] + xl.shape[0]
    ntl = lext // T

    def kern(dha_ref, dhb_ref, c_ref, x_ref, dx2_ref, mod_ref, nw_ref, gx_ref, gnw_ref, dss_ref):
        i = pl.program_id(0)
        is_ctx = i < nct

        @pl.when(i == 0)
        def _():
            gnw_ref[...] = jnp.zeros_like(gnw_ref)
            dss_ref[...] = jnp.zeros_like(dss_ref)

        x = jnp.where(is_ctx, c_ref[...], x_ref[...])
        dh_ = dha_ref[...] + jnp.where(is_ctx, 0.0, dhb_ref[...])
        nw_ = nw_ref[...]
        r = lax.rsqrt(jnp.mean(x * x, axis=-1, keepdims=True) + EPS)
        xn = x * r
        scale = jnp.where(is_ctx, mod_ref[1:2, D:2 * D], mod_ref[0:1, D:2 * D])
        dsh = jnp.sum(dh_, axis=0, keepdims=True)
        dsc = jnp.sum(dh_ * (xn * nw_), axis=0, keepdims=True)
        row = jnp.concatenate([dsh, dsc], axis=1)
        rid = _iota((8, 2 * D), 0)
        dss_ref[...] += jnp.where(rid == jnp.where(is_ctx, 1, 0), row, 0.0)
        dxnw = dh_ * (1.0 + scale)
        gnw_ref[...] += jnp.broadcast_to(jnp.sum(dxnw * xn, axis=0, keepdims=True), (8, D))
        dxn = dxnw * nw_
        dx = r * (dxn - xn * jnp.mean(dxn * xn, axis=-1, keepdims=True))
        gx_ref[...] = dx2_ref[...] + dx

    return pl.pallas_call(
        kern, name="norm_bwd", grid=(ntl,),
        in_specs=[pl.BlockSpec((T, D), lambda i: (i, 0)), pl.BlockSpec((T, D), lambda i: (jnp.maximum(i - nct, 0), 0)),
                  *_ext_specs(nct),
                  pl.BlockSpec((T, D), lambda i: (jnp.maximum(i - nct, 0), 0)),
                  pl.BlockSpec((8, 3 * D), lambda i: (0, 0)), pl.BlockSpec((1, D), lambda i: (0, 0))],
        out_specs=[pl.BlockSpec((T, D), lambda i: (jnp.maximum(i - nct, 0), 0)),
                   pl.BlockSpec((8, D), lambda i: (0, 0)), pl.BlockSpec((8, 2 * D), lambda i: (0, 0))],
        out_shape=[jax.ShapeDtypeStruct((lext - nct * T, D), F32), jax.ShapeDtypeStruct((8, D), F32),
                   jax.ShapeDtypeStruct((8, 2 * D), F32)],
        compiler_params=_cp(("arbitrary",)),
    )(dha, dhb, ctx, xl, dx2, mod, nw)


CB = 1024


def _halo_specs(width_blk, col_off_blocks, ntl):
    t8 = T // 8
    main = pl.BlockSpec((T, width_blk), lambda j, i: (i, j + col_off_blocks))
    prev = pl.BlockSpec((8, width_blk), lambda j, i: (jnp.maximum(i * t8 - 1, 0), j + col_off_blocks))
    nxt = pl.BlockSpec((8, width_blk), lambda j, i: (jnp.minimum((i + 1) * t8, ntl * t8 - 1), j + col_off_blocks))
    return main, prev, nxt


def _seq_edges(i, nct, ntl):
    starts = jnp.logical_or(i == 0, i == nct)
    ends = jnp.logical_or(i == nct - 1, i == ntl - 1)
    return starts, ends


def _shifted(ext, off):
    n = ext.shape[0]
    return pltpu.roll(ext, (-off) % n, axis=0)[8:8 + T]


def _conv_fwd(proj_ssd, cw, cb, nct):
    lext = proj_ssd.shape[0]
    ntl = lext // T

    def kern(u_ref, up_ref, un_ref, w_ref, b_ref, o_ref):
        i = pl.program_id(1)
        starts, ends = _seq_edges(i, nct, ntl)
        up = jnp.where(starts, 0.0, up_ref[...])
        un = jnp.where(ends, 0.0, un_ref[...])
        ext = jnp.concatenate([up, u_ref[...], un], axis=0)
        w = w_ref[...]
        pre = b_ref[...] + w[0:1] * _shifted(ext, -2) + w[1:2] * _shifted(ext, -1) \
            + w[2:3] * u_ref[...] + w[3:4] * _shifted(ext, 1)
        o_ref[...] = _silu(pre)

    main, prev, nxt = _halo_specs(CB, 0, ntl)
    return pl.pallas_call(
        kern, name="conv_fwd", grid=(XBC // CB, ntl),
        in_specs=[main, prev, nxt, pl.BlockSpec((8, CB), lambda j, i: (0, j)), pl.BlockSpec((1, CB), lambda j, i: (0, j))],
        out_specs=pl.BlockSpec((T, CB), lambda j, i: (i, j)),
        out_shape=jax.ShapeDtypeStruct((lext, XBC), F32),
        compiler_params=_cp(("parallel", "parallel")),
    )(proj_ssd, proj_ssd, proj_ssd, cw, cb)


def _conv_bwd(dpost, proj_ssd, cw, cb, col_off, width, nct, name):
    lext = proj_ssd.shape[0]
    ntl = lext // T
    cob = col_off // CB

    def kern(u_ref, up_ref, un_ref, d_ref, dp_ref, dn_ref, w_ref, b_ref, du_ref, gw_ref, gb_ref):
        i = pl.program_id(1)

        @pl.when(i == 0)
        def _():
            gw_ref[...] = jnp.zeros_like(gw_ref)
            gb_ref[...] = jnp.zeros_like(gb_ref)

        starts, ends = _seq_edges(i, nct, ntl)
        ext = jnp.concatenate([jnp.where(starts, 0.0, up_ref[...]), u_ref[...], jnp.where(ends, 0.0, un_ref[...])], axis=0)
        dext = jnp.concatenate([jnp.where(starts, 0.0, dp_ref[...]), d_ref[...], jnp.where(ends, 0.0, dn_ref[...])], axis=0)
        w = w_ref[...]
        n = ext.shape[0]
        pre = b_ref[...] + w[0:1] * pltpu.roll(ext, 2, axis=0) + w[1:2] * pltpu.roll(ext, 1, axis=0) \
            + w[2:3] * ext + w[3:4] * pltpu.roll(ext, n - 1, axis=0)
        dpre = dext * _dsilu(pre)
        dm = dpre[8:8 + T]
        du = w[0:1] * _shifted(dpre, 2) + w[1:2] * _shifted(dpre, 1) + w[2:3] * dm + w[3:4] * _shifted(dpre, -1)
        du_ref[...] = du.astype(BF16)
        g0 = jnp.sum(dm * _shifted(ext, -2), axis=0, keepdims=True)
        g1 = jnp.sum(dm * _shifted(ext, -1), axis=0, keepdims=True)
        g2 = jnp.sum(dm * u_ref[...], axis=0, keepdims=True)
        g3 = jnp.sum(dm * _shifted(ext, 1), axis=0, keepdims=True)
        rid = _iota((8, CB), 0)
        gw_ref[...] += jnp.where(rid == 0, g0, jnp.where(rid == 1, g1, jnp.where(rid == 2, g2, jnp.where(rid == 3, g3, 0.0))))
        gb_ref[...] += jnp.broadcast_to(jnp.sum(dm, axis=0, keepdims=True), (8, CB))

    main, prev, nxt = _halo_specs(CB, cob, ntl)
    dmain, dprev, dnxt = _halo_specs(CB, 0, ntl)
    return pl.pallas_call(
        kern, name=name, grid=(width // CB, ntl),
        in_specs=[main, prev, nxt, dmain, dprev, dnxt,
                  pl.BlockSpec((8, CB), lambda j, i: (0, j + cob)), pl.BlockSpec((1, CB), lambda j, i: (0, j + cob))],
        out_specs=[pl.BlockSpec((T, CB), lambda j, i: (i, j)), pl.BlockSpec((8, CB), lambda j, i: (0, j)),
                   pl.BlockSpec((8, CB), lambda j, i: (0, j))],
        out_shape=[jax.ShapeDtypeStruct((lext, width), BF16), jax.ShapeDtypeStruct((8, width), F32),
                   jax.ShapeDtypeStruct((8, width), F32)],
        compiler_params=_cp(("parallel", "arbitrary")),
    )(proj_ssd, proj_ssd, proj_ssd, dpost, dpost, dpost, cw, cb)


def _tri(lower):
    r, c = _iota((Q, Q), 0), _iota((Q, Q), 1)
    return jnp.where((c <= r) if lower else (c >= r), 1.0, 0.0).astype(BF16)


def _is_bdir_lane(shape):
    ln = _iota(shape, len(shape) - 1)
    return jnp.logical_and(((ln >> 2) & 1) == 1, ln < 64)


def _dt_fwd(proj_ssd, dtb, av):
    lext = proj_ssd.shape[0]

    def kern(p_ref, b_ref, a_ref, dtg_ref, lag_ref, dtt_ref, lat_ref):
        lane = _iota((T, 128), 1)
        raw = p_ref[...] + b_ref[...]
        dt = jnp.where(lane < 64, jnp.maximum(raw, 0.0) + jnp.log1p(jnp.exp(-jnp.abs(raw))), 0.0)
        dta = dt * a_ref[...]
        tl, tu = _tri(True), _tri(False)
        isb = _is_bdir_lane((Q, 128))
        las = []
        for qq in range(T // Q):
            blk = dta[qq * Q:(qq + 1) * Q]
            las.append(jnp.where(isb, _dot3_l(tu, blk), _dot3_l(tl, blk)))
        la = jnp.concatenate(las, axis=0)
        for g in range(NG):
            sh = (128 - 8 * g) % 128
            dtg_ref[g] = jnp.where(lane < 8, pltpu.roll(dt, sh, axis=1) if sh else dt, 0.0)
            lag_ref[g] = jnp.where(lane < 8, pltpu.roll(la, sh, axis=1) if sh else la, 0.0)
        dtt_ref[...] = dt.T[0:64]
        lat_ref[...] = la.T[0:64]

    return pl.pallas_call(
        kern, name="dt_fwd", grid=(lext // T,),
        in_specs=[pl.BlockSpec((T, 128), lambda i: (i, XBC // 128)), pl.BlockSpec((1, 128), lambda i: (0, 0)),
                  pl.BlockSpec((1, 128), lambda i: (0, 0))],
        out_specs=[pl.BlockSpec((NG, T, 128), lambda i: (0, i, 0)), pl.BlockSpec((NG, T, 128), lambda i: (0, i, 0)),
                   pl.BlockSpec((64, T), lambda i: (0, i)), pl.BlockSpec((64, T), lambda i: (0, i))],
        out_shape=[jax.ShapeDtypeStruct((NG, lext, 128), F32), jax.ShapeDtypeStruct((NG, lext, 128), F32),
                   jax.ShapeDtypeStruct((64, lext), F32), jax.ShapeDtypeStruct((64, lext), F32)],
        compiler_params=_cp(("parallel",)),
    )(proj_ssd, dtb, av)


def _dt_bwd(ddtg, proj_ssd, dtb):
    lext = proj_ssd.shape[0]

    def kern(d_ref, p_ref, b_ref, o_ref, gb_ref):
        @pl.when(pl.program_id(0) == 0)
        def _():
            gb_ref[...] = jnp.zeros_like(gb_ref)

        acc = d_ref[0]
        for g in range(1, NG):
            acc = acc + pltpu.roll(d_ref[g], 8 * g, axis=1)
        draw = acc * _sig(p_ref[...] + b_ref[...])
        o_ref[...] = draw.astype(BF16)
        gb_ref[...] += jnp.broadcast_to(jnp.sum(draw, axis=0, keepdims=True), (8, 128))

    return pl.pallas_call(
        kern, name="dt_bwd", grid=(lext // T,),
        in_specs=[pl.BlockSpec((NG, T, 128), lambda i: (0, i, 0)), pl.BlockSpec((T, 128), lambda i: (i, XBC // 128)),
                  pl.BlockSpec((1, 128), lambda i: (0, 0))],
        out_specs=[pl.BlockSpec((T, 128), lambda i: (i, 0)), pl.BlockSpec((8, 128), lambda i: (0, 0))],
        out_shape=[jax.ShapeDtypeStruct((lext, 128), BF16), jax.ShapeDtypeStruct((8, 128), F32)],
        compiler_params=_cp(("arbitrary",)),
    )(ddtg, proj_ssd, dtb)


def _expand_sel(d):
    r, c = _iota((128, 256), 0), _iota((128, 256), 1)
    return jnp.where(r == 4 * d + (c >> 6), 1.0, 0.0).astype(BF16)


def _reduce_sel(d):
    r, c = _iota((256, 128), 0), _iota((256, 128), 1)
    return jnp.where(c == 4 * d + (r >> 6), 1.0, 0.0).astype(BF16)


def _chunk_of_bwd_dir(j, ncc, nc):
    return jnp.where(j < ncc, ncc - 1 - j, nc + ncc - 1 - j)


def _dir_terms(la, dt, d):
    lane = _iota(la.shape, 1)
    mine = jnp.logical_and(lane >= 4 * d, lane < 4 * d + 4)
    la = jnp.where(mine, la, 0.0)
    tot = la[Q - 1:Q] if d == 0 else la[0:1]
    wnd = jnp.exp(tot - la)
    return tot, wnd * jnp.where(mine, dt, 0.0), wnd


def _ssd_state(xbc, dtg, lag, ncc):
    lext = xbc.shape[0]
    nc = lext // Q

    def kern(xf_ref, bf_ref, dtf_ref, laf_ref, xb_ref, bb_ref, dtb_ref, lab_ref, hf_ref, hb_ref, sf, sb):
        @pl.when(pl.program_id(0) == 0)
        def _():
            sf[...] = jnp.zeros_like(sf)
            sb[...] = jnp.zeros_like(sb)

        for d, (x_ref, b_ref, dt_ref, la_ref, h_ref, s) in enumerate(
                ((xf_ref, bf_ref, dtf_ref, laf_ref, hf_ref, sf), (xb_ref, bb_ref, dtb_ref, lab_ref, hb_ref, sb))):
            h_ref[...] = s[...]
            ex = _expand_sel(d)
            for g in range(NG):
                cols = slice(256 * g, 256 * (g + 1))
                tot, w_end, _ = _dir_terms(la_ref[g], dt_ref[g], d)
                wexp = _dot2_r(w_end, ex)
                dexp = _dot2_r(jnp.broadcast_to(jnp.exp(tot), (8, 128)), ex)[0:1]
                xw = (x_ref[:, cols] * wexp).astype(BF16)
                s[:, cols] = s[:, cols] * dexp + _dot(b_ref[:, 128 * g:128 * (g + 1)].T.astype(BF16), xw)

    cb = functools.partial(_chunk_of_bwd_dir, ncc=ncc, nc=nc)
    sm = lambda f: pl.BlockSpec((NG, Q, 128), lambda j: (0, f(j), 0))
    one = lambda j: j
    return pl.pallas_call(
        kern, name="ssd_state", grid=(nc,),
        in_specs=[pl.BlockSpec((Q, DI), lambda j: (j, 0)), pl.BlockSpec((Q, NG * NS), lambda j: (j, 2)), sm(one), sm(one),
                  pl.BlockSpec((Q, DI), lambda j: (cb(j), 0)), pl.BlockSpec((Q, NG * NS), lambda j: (cb(j), 2)), sm(cb), sm(cb)],
        out_specs=[pl.BlockSpec((None, 128, DI), lambda j: (j, 0, 0)),
                   pl.BlockSpec((None, 128, DI), lambda j: (cb(j), 0, 0))],
        out_shape=[jax.ShapeDtypeStruct((nc, 128, DI), F32), jax.ShapeDtypeStruct((nc, 128, DI), F32)],
        scratch_shapes=[pltpu.VMEM((128, DI), F32), pltpu.VMEM((128, DI), F32)],
        compiler_params=_cp(("arbitrary",)),
    )(xbc, xbc, dtg, lag, xbc, xbc, dtg, lag)


def _ssd_out(xbc, dtg, lag, dtt, lat, htf, htb, ncc):
    lext = xbc.shape[0]
    nc = lext // Q
    ncx = nc - ncc

    gps = 4
    li, si = (lambda: _iota((Q, Q), 0)), (lambda: _iota((Q, Q), 1))

    def kern(x_ref, b_ref, c_ref, dtg_ref, lag_ref, dtt_ref, lat_ref, hf_ref, hb_ref, y_ref):
        lane = _iota((Q, 256), 1)
        masks = (li() >= si(), li() <= si())
        for gg in range(gps):
            cols = slice(256 * gg, 256 * (gg + 1))
            cm = c_ref[:, 128 * gg:128 * (gg + 1)]
            xb_ = x_ref[:, cols].astype(BF16)
            s_ = _dot_nt(cm.astype(BF16), b_ref[:, 128 * gg:128 * (gg + 1)].astype(BF16))
            la, dtt_, lat_ = lag_ref[gg], dtt_ref[8 * gg:8 * (gg + 1)], lat_ref[8 * gg:8 * (gg + 1)]
            elam = jnp.exp(la)
            y = jnp.zeros((Q, 256), F32)
            for d, h_ref in enumerate((hf_ref, hb_ref)):
                rhs = jnp.concatenate([xb_, h_ref[:, cols].astype(BF16)], axis=0)
                lhs = []
                for r in range(HPG):
                    j = 4 * d + r
                    lm = jnp.where(masks[d], jnp.exp(la[:, j:j + 1] - lat_[j:j + 1, :]), 0.0)
                    w = s_ * lm * dtt_[j:j + 1, :]
                    lhs.append(jnp.concatenate([w, cm * elam[:, j:j + 1]], axis=1).astype(BF16))
                yall = _dot(jnp.concatenate(lhs, axis=0), rhs)
                for r in range(HPG):
                    y = y + jnp.where((lane >> 6) == r, yall[Q * r:Q * (r + 1)], 0.0)
            y_ref[:, cols] = y

    nb = NG // gps
    sm = pl.BlockSpec((gps, Q, 128), lambda c, g: (g, c + ncc, 0))
    smt = pl.BlockSpec((8 * gps, Q), lambda c, g: (g, c + ncc))
    st3 = pl.BlockSpec((None, 128, 256 * gps), lambda c, g: (c + ncc, 0, g))
    return pl.pallas_call(
        kern, name="ssd_out", grid=(ncx, nb),
        in_specs=[pl.BlockSpec((Q, 256 * gps), lambda c, g: (c + ncc, g)),
                  pl.BlockSpec((Q, 128 * gps), lambda c, g: (c + ncc, 2 * nb + g)),
                  pl.BlockSpec((Q, 128 * gps), lambda c, g: (c + ncc, 3 * nb + g)), sm, sm, smt, smt, st3, st3],
        out_specs=pl.BlockSpec((Q, 256 * gps), lambda c, g: (c, g)),
        out_shape=jax.ShapeDtypeStruct((ncx * Q, DI), F32),
        compiler_params=_cp(("parallel", "parallel")),
    )(xbc, xbc, xbc, dtg, lag, dtt, lat, htf, htb)


def _ssd_bwd_state(xbc, dy, lag, ncc):
    lext = xbc.shape[0]
    nc = lext // Q

    def kern(cf_ref, dyf_ref, laf_ref, cb_ref, dyb_ref, lab_ref, df_ref, db_ref, sf, sb):
        @pl.when(pl.program_id(0) == 0)
        def _():
            sf[...] = jnp.zeros_like(sf)
            sb[...] = jnp.zeros_like(sb)

        for d, (c_ref, dy_ref, la_ref, o_ref, s) in enumerate(
                ((cf_ref, dyf_ref, laf_ref, df_ref, sf), (cb_ref, dyb_ref, lab_ref, db_ref, sb))):
            o_ref[...] = s[...]
            ex = _expand_sel(d)
            for g in range(NG):
                cols = slice(256 * g, 256 * (g + 1))
                la = la_ref[g]
                tot = la[Q - 1:Q] if d == 0 else la[0:1]
                eexp = _dot2_r(jnp.exp(la), ex)
                dexp = _dot2_r(jnp.broadcast_to(jnp.exp(tot), (8, 128)), ex)[0:1]
                dye = (dy_ref[:, cols] * eexp).astype(BF16)
                s[:, cols] = s[:, cols] * dexp + _dot(c_ref[:, 128 * g:128 * (g + 1)].T.astype(BF16), dye)

    cf = lambda j: nc - 1 - j
    cb = lambda j: _chunk_of_bwd_dir(nc - 1 - j, ncc, nc)
    sm = lambda f: pl.BlockSpec((NG, Q, 128), lambda j: (0, f(j), 0))
    return pl.pallas_call(
        kern, name="ssd_bwd_state", grid=(nc,),
        in_specs=[pl.BlockSpec((Q, NG * NS), lambda j: (cf(j), 3)), pl.BlockSpec((Q, DI), lambda j: (cf(j), 0)), sm(cf),
                  pl.BlockSpec((Q, NG * NS), lambda j: (cb(j), 3)), pl.BlockSpec((Q, DI), lambda j: (cb(j), 0)), sm(cb)],
        out_specs=[pl.BlockSpec((None, 128, DI), lambda j: (cf(j), 0, 0)),
                   pl.BlockSpec((None, 128, DI), lambda j: (cb(j), 0, 0))],
        out_shape=[jax.ShapeDtypeStruct((nc, 128, DI), F32), jax.ShapeDtypeStruct((nc, 128, DI), F32)],
        scratch_shapes=[pltpu.VMEM((128, DI), F32), pltpu.VMEM((128, DI), F32)],
        compiler_params=_cp(("arbitrary",)),
    )(xbc, dy, lag, xbc, dy, lag)


def _ssd_bwd_out(xbc, dy, dsk, dtg, lag, dtt, lat, htf, htb, dhf, dhb, a_rows):
    lext = xbc.shape[0]
    nc = lext // Q

    gps = 1

    def kern(x_ref, b_ref, c_ref, dy_ref, sk_ref, dtg_ref, lag_ref, dtt_ref, lat_ref, hf_ref, hb_ref, df_ref, db_ref,
             a_ref, dx_ref, dbo_ref, dco_ref, ddt_ref, ga_ref):
        @pl.when(pl.program_id(1) == 0)
        def _():
            ga_ref[...] = jnp.zeros_like(ga_ref)

        for gg in range(gps):
            one_group(gg, x_ref, b_ref, c_ref, dy_ref, sk_ref, dtg_ref, lag_ref, dtt_ref, lat_ref, hf_ref, hb_ref, df_ref,
                      db_ref, a_ref, dx_ref, dbo_ref, dco_ref, ddt_ref, ga_ref)

    def one_group(gg, x_ref, b_ref, c_ref, dy_ref, sk_ref, dtg_ref, lag_ref, dtt_ref, lat_ref, hf_ref, hb_ref, df_ref,
                  db_ref, a_ref, dx_ref, dbo_ref, dco_ref, ddt_ref, ga_ref):
        g = pl.program_id(0) * gps + gg
        cols, cols128 = slice(256 * gg, 256 * (gg + 1)), slice(128 * gg, 128 * (gg + 1))
        x, bm, cm, dy_ = x_ref[:, cols], b_ref[:, cols128], c_ref[:, cols128], dy_ref[:, cols]
        xb_, bb_, cb_, dyb_ = x.astype(BF16), bm.astype(BF16), cm.astype(BF16), dy_.astype(BF16)
        st = _dot_nt(bb_, cb_)
        si, li = _iota((Q, Q), 0), _iota((Q, Q), 1)
        lane = _iota((Q, 256), 1)
        lane128 = _iota((Q, 128), 1)
        row128 = _iota((Q, 128), 0)
        sub = _iota((128, Q), 0)
        la, dt = lag_ref[gg], dtg_ref[gg]
        dtt_, lat_ = dtt_ref[8 * gg:8 * (gg + 1)], lat_ref[8 * gg:8 * (gg + 1)]
        elam = jnp.exp(la)
        dst = jnp.zeros((Q, Q), F32)
        dxa = jnp.zeros((Q, 256), F32)
        dba = jnp.zeros((Q, 128), F32)
        dca = jnp.zeros((Q, 128), F32)
        dlam = jnp.zeros((Q, 128), F32)
        ddir = jnp.zeros((Q, 128), F32)
        rows = jnp.zeros((128, Q), F32)
        hms = [(lane >> 6) == r for r in range(HPG)]
        for d, (h_ref, dh_ref) in enumerate(((hf_ref, df_ref), (hb_ref, db_ref))):
            ht, dht = h_ref[:, cols], dh_ref[:, cols]
            htb_, dhtb_ = ht.astype(BF16), dht.astype(BF16)
            tot, w_end, wnd = _dir_terms(la, dt, d)
            ex, rs = _expand_sel(d), _reduce_sel(d)
            elx = _dot2_r(elam, ex)
            wex = _dot2_r(w_end, ex)
            dye = dy_ * elx
            ch = _dot(cb_, htb_)
            bd = _dot(bb_, dhtb_)
            dca = dca + _dot_nt(dye.astype(BF16), htb_)
            dba = dba + _dot_nt((x * wex).astype(BF16), dhtb_)
            dlam = dlam + _dot2_r(dye * ch, rs)
            xbd = _dot2_r(x * bd, rs)
            e_ = w_end * xbd
            dlam = dlam - e_
            ddir = ddir + wnd * xbd
            hh = _dot2_r(jnp.broadcast_to(jnp.sum(dht * ht, axis=0, keepdims=True), (8, 256)), rs)[0:1]
            tot_term = jnp.sum(e_, axis=0, keepdims=True) + jnp.exp(tot) * hh
            dlam = dlam + jnp.where(row128 == (Q - 1 if d == 0 else 0), tot_term, 0.0)
            rhs = jnp.concatenate([dyb_, dhtb_], axis=0)
            maskt = (li >= si) if d == 0 else (li <= si)
            for r in range(HPG):
                j = 4 * d + r
                dc = dt[:, j:j + 1]
                lmt = jnp.where(maskt, jnp.exp(lat_[j:j + 1, :] - la[:, j:j + 1]), 0.0)
                ldc = lmt * jnp.broadcast_to(dc, (Q, Q))
                lhs = jnp.concatenate([st * ldc, bm * w_end[:, j:j + 1]], axis=1).astype(BF16)
                dxa = dxa + jnp.where(hms[r], _dot(lhs, rhs), 0.0)
                dwt = _dot_nt(jnp.where(hms[r], x, 0.0).astype(BF16), dyb_)
                q = dwt * st
                cs = jnp.sum(q * lmt, axis=1, keepdims=True)
                ddir = ddir + jnp.where(lane128 == j, cs, 0.0)
                dlam = dlam - jnp.where(lane128 == j, cs * dc, 0.0)
                rows = rows + jnp.where(sub == j, jnp.sum(q * ldc, axis=0, keepdims=True), 0.0)
                dst = dst + dwt * ldc
        dlam = dlam + rows.T
        dba = dba + _dot(dst.astype(BF16), cb_)
        dca = dca + _dot(dst.T.astype(BF16), bb_)
        isb = jnp.logical_and(lane128 >= 4, lane128 < 8)
        ddel = jnp.where(isb, _dot2_l(_tri(True), dlam), _dot2_l(_tri(False), dlam))
        a_l = a_ref[pl.ds(g, 1), :]
        ddt_ref[gg] = ddir + a_l * ddel
        ga_ref[gg] += jnp.broadcast_to(a_l * jnp.sum(dt * ddel, axis=0, keepdims=True), (8, 128))
        dx_ref[:, cols] = dxa + dy_ * sk_ref[:, cols]
        dbo_ref[:, cols128] = dba
        dco_ref[:, cols128] = dca

    nb = NG // gps
    st3 = pl.BlockSpec((None, 128, 256 * gps), lambda g, c: (c, 0, g))
    sm = pl.BlockSpec((gps, Q, 128), lambda g, c: (g, c, 0))
    smt = pl.BlockSpec((8 * gps, Q), lambda g, c: (g, c))
    wide = pl.BlockSpec((Q, 256 * gps), lambda g, c: (c, g))
    return pl.pallas_call(
        kern, name="ssd_bwd_out", grid=(nb, nc),
        in_specs=[wide, pl.BlockSpec((Q, 128 * gps), lambda g, c: (c, 2 * nb + g)),
                  pl.BlockSpec((Q, 128 * gps), lambda g, c: (c, 3 * nb + g)), wide,
                  pl.BlockSpec((1, 256 * gps), lambda g, c: (0, g)), sm, sm, smt, smt, st3, st3, st3, st3,
                  pl.BlockSpec((8, 128), lambda g, c: (0, 0))],
        out_specs=[wide, pl.BlockSpec((Q, 128 * gps), lambda g, c: (c, g)),
                   pl.BlockSpec((Q, 128 * gps), lambda g, c: (c, g)), sm, pl.BlockSpec((gps, 8, 128), lambda g, c: (g, 0, 0))],
        out_shape=[jax.ShapeDtypeStruct((lext, DI), F32), jax.ShapeDtypeStruct((lext, NG * NS), F32),
                   jax.ShapeDtypeStruct((lext, NG * NS), F32), jax.ShapeDtypeStruct((NG, lext, 128), F32),
                   jax.ShapeDtypeStruct((NG, 8, 128), F32)],
        compiler_params=_cp(("parallel", "arbitrary")),
    )(xbc, xbc, xbc, dy, dsk, dtg, lag, dtt, lat, htf, htb, dhf, dhb, a_rows)


def _post_fwd(yssm, xbc, proj_rest, dsk, gnw, nct):
    l = yssm.shape[0]

    def kern(y_ref, x_ref, z_ref, dsk_ref, w_ref, o_ref):
        y = y_ref[...] + dsk_ref[...] * x_ref[...]
        yz = y * _silu(z_ref[...])
        for g in range(NG):
            sl = slice(256 * g, 256 * (g + 1))
            v = yz[:, sl]
            r = lax.rsqrt(jnp.mean(v * v, axis=-1, keepdims=True) + EPS)
            o_ref[:, sl] = (v * r * w_ref[:, sl]).astype(BF16)

    return pl.pallas_call(
        kern, name="post_fwd", grid=(l // T,),
        in_specs=[pl.BlockSpec((T, DI), lambda i: (i, 0)), pl.BlockSpec((T, DI), lambda i: (i + nct, 0)),
                  pl.BlockSpec((T, DI), lambda i: (i, 0)), pl.BlockSpec((1, DI), lambda i: (0, 0)),
                  pl.BlockSpec((1, DI), lambda i: (0, 0))],
        out_specs=pl.BlockSpec((T, DI), lambda i: (i, 0)),
        out_shape=jax.ShapeDtypeStruct((l, DI), BF16),
        compiler_params=_cp(("parallel",)),
    )(yssm, xbc, proj_rest, dsk, gnw)


def _post_bwd(dgn, yssm, xbc, proj_rest, dsk, gnw, dpr, nct):
    l = yssm.shape[0]
    lext = xbc.shape[0]
    xi = lambda i: (jnp.maximum(i - nct, 0), 0)

    def kern(dg_ref, y_ref, x_ref, z_ref, dsk_ref, w_ref, dpr_ref, dy_ref, dz_ref, gw_ref, gd_ref):
        i = pl.program_id(0)

        @pl.when(i == 0)
        def _():
            gw_ref[...] = jnp.zeros_like(gw_ref)
            gd_ref[...] = jnp.zeros_like(gd_ref)

        @pl.when(i < nct)
        def _():
            dy_ref[...] = jnp.zeros_like(dy_ref)

        @pl.when(i >= nct)
        def _():
            xs = x_ref[...]
            z = z_ref[...]
            y = y_ref[...] + dsk_ref[...] * xs
            sz = _silu(z)
            yz = y * sz
            dgn_ = dg_ref[...]
            dyz_parts = []
            gws = []
            for g in range(NG):
                sl = slice(256 * g, 256 * (g + 1))
                v = yz[:, sl]
                r = lax.rsqrt(jnp.mean(v * v, axis=-1, keepdims=True) + EPS)
                vn = v * r
                dn = dgn_[:, sl] * w_ref[:, sl]
                gws.append(jnp.sum(dgn_[:, sl] * vn, axis=0, keepdims=True))
                dyz_parts.append(r * (dn - vn * jnp.mean(dn * vn, axis=-1, keepdims=True)))
            dyz = jnp.concatenate(dyz_parts, axis=1)
            gw_ref[...] += jnp.broadcast_to(jnp.concatenate(gws, axis=1), (8, DI))
            dy = dyz * sz
            dz_ref[...] = (dyz * y * _dsilu(z)).astype(BF16)
            gd_ref[...] += jnp.broadcast_to(jnp.sum(dy * xs, axis=0, keepdims=True), (8, DI))
            dy_ref[...] = dy

    return pl.pallas_call(
        kern, name="post_bwd", grid=(lext // T,),
        in_specs=[pl.BlockSpec((T, DI), xi), pl.BlockSpec((T, DI), xi), pl.BlockSpec((T, DI), lambda i: (i, 0)),
                  pl.BlockSpec((T, DI), xi), pl.BlockSpec((1, DI), lambda i: (0, 0)), pl.BlockSpec((1, DI), lambda i: (0, 0)),
                  pl.BlockSpec(memory_space=pl.ANY)],
        out_specs=[pl.BlockSpec((T, DI), lambda i: (i, 0)),
                   pl.BlockSpec((T, DI), xi), pl.BlockSpec((8, DI), lambda i: (0, 0)), pl.BlockSpec((8, DI), lambda i: (0, 0))],
        out_shape=[jax.ShapeDtypeStruct((lext, DI), F32),
                   jax.ShapeDtypeStruct((l, RESTW), BF16), jax.ShapeDtypeStruct((8, DI), F32), jax.ShapeDtypeStruct((8, DI), F32)],
        input_output_aliases={6: 1},
        compiler_params=_cp(("arbitrary",)),
    )(dgn, yssm, xbc, proj_rest, dsk, gnw, dpr)


C_G1, C_G2, C_GA, C_GB, C_CG = 2, 3, 4, 5, 6
PITCH = GW + 16
NROW = T // GW


GAP = PITCH - GW
PADR = GAP + NROW * PITCH
NSTRIP = D // 128


def _fill_padded(pad8, val):
    z = jnp.zeros((GAP, D), F32)
    parts = [z]
    for r in range(NROW):
        parts += [val[GW * r:GW * (r + 1)], z]
    p = jnp.concatenate(parts, axis=0)
    pad8[0] = p
    for j in range(1, pad8.shape[0]):
        pad8[j] = pltpu.roll(p, PADR - j, axis=0)


def _tap(pad8, base, off, ln):
    return pad8[off % 8, pl.ds(base + off - off % 8, GW), ln]


def _row_conv(out_ref, pad8, w_ref, transpose):
    def strip(s, carry):
        ln = pl.ds(pl.multiple_of(s * 128, 128), 128)
        for r in range(NROW):
            base = GAP + PITCH * r
            acc = jnp.zeros((GW, 128), F32)
            for k in range(CK):
                off = (k - 15) if not transpose else (15 - k)
                acc = acc + w_ref[pl.ds(k, 1), ln] * _tap(pad8, base, off, ln)
            out_ref[pl.ds(GW * r, GW), ln] = acc
        return carry

    lax.fori_loop(0, NSTRIP, strip, 0)


def _row_conv_wgrad(gcw_ref, padd8, pada8):
    def strip(s, carry):
        ln = pl.ds(pl.multiple_of(s * 128, 128), 128)
        rid = _iota((32, 128), 0)
        g = jnp.zeros((32, 128), F32)
        for k0 in range(0, CK, 8):
            taps = range(k0, min(k0 + 8, CK))
            accs = {k: jnp.zeros((8, 128), F32) for k in taps}
            for r in range(NROW):
                base = GAP + PITCH * r
                d = _tap(padd8, base, 0, ln)
                for k in taps:
                    p = d * pada8[0, pl.ds(base + k - 15, GW), ln]
                    part = p[0:8]
                    for q in range(1, GW // 8):
                        part = part + p[8 * q:8 * (q + 1)]
                    accs[k] = accs[k] + part
            for k in taps:
                g = jnp.where(rid == k, jnp.sum(accs[k], axis=0, keepdims=True), g)
        gcw_ref[:, ln] += g
        return carry

    lax.fori_loop(0, NSTRIP, strip, 0)


def _ln_stats(cv):
    mu = jnp.mean(cv, axis=-1, keepdims=True)
    xc = cv - mu
    rs = lax.rsqrt(jnp.mean(xc * xc, axis=-1, keepdims=True) + EPS)
    return xc * rs, rs


def _conf_fwd(proj_rest, cw, cb, lw, lb):
    l = proj_rest.shape[0]

    def kern(ga_ref, gb_ref, cg_ref, cw_ref, cb_ref, lw_ref, lb_ref, o_ref, cv_ref, pad8):
        _fill_padded(pad8, ga_ref[...] * _sig(gb_ref[...]))
        _row_conv(cv_ref, pad8, cw_ref, False)
        cv = cv_ref[...] + cb_ref[...]
        cv_ref[...] = cv
        xh, _ = _ln_stats(cv)
        ln = xh * lw_ref[...] + lb_ref[...]
        o_ref[...] = (_silu(ln) * _silu(cg_ref[...])).astype(BF16)

    vec = pl.BlockSpec((1, D), lambda i: (0, 0))
    blk = pl.BlockSpec((T, D), lambda i: (i, 0))
    return pl.pallas_call(
        kern, name="conf_fwd", grid=(l // T,),
        in_specs=[pl.BlockSpec((T, D), lambda i: (i, C_GA)), pl.BlockSpec((T, D), lambda i: (i, C_GB)),
                  pl.BlockSpec((T, D), lambda i: (i, C_CG)), pl.BlockSpec((32, D), lambda i: (0, 0)), vec, vec, vec],
        out_specs=[blk, blk],
        out_shape=[jax.ShapeDtypeStruct((l, D), BF16), jax.ShapeDtypeStruct((l, D), F32)],
        scratch_shapes=[pltpu.VMEM((8, PADR, D), F32)],
        compiler_params=_cp(("parallel",)),
    )(proj_rest, proj_rest, proj_rest, cw, cb, lw, lb)


def _conf_bwd(duc, cv, proj_rest, cw, lw, lb, dpr):
    l = proj_rest.shape[0]

    def kern(du_ref, cv_ref, ga_ref, gb_ref, cg_ref, cw_ref, lw_ref, lb_ref, dpr_ref, o_ref, gcw_ref, gv_ref, sc,
             pada, padd, da_ref):
        i, j = pl.program_id(0), pl.program_id(1)

        @pl.when(jnp.logical_and(i == 0, j == 0))
        def _():
            gcw_ref[...] = jnp.zeros_like(gcw_ref)
            gv_ref[...] = jnp.zeros_like(gv_ref)

        @pl.when(j == 0)
        def _():
            ga, gb, cg = ga_ref[...], gb_ref[...], cg_ref[...]
            sg = _sig(gb)
            xh, rs = _ln_stats(cv_ref[...])
            ln = xh * lw_ref[...] + lb_ref[...]
            du = du_ref[...]
            sc[:, 2 * D:3 * D] = (du * _silu(ln) * _dsilu(cg)).astype(BF16)
            dln = du * _silu(cg) * _dsilu(ln)
            g_lw = jnp.sum(dln * xh, axis=0, keepdims=True)
            g_lb = jnp.sum(dln, axis=0, keepdims=True)
            dxh = dln * lw_ref[...]
            dcv = rs * (dxh - jnp.mean(dxh, axis=-1, keepdims=True) - xh * jnp.mean(dxh * xh, axis=-1, keepdims=True))
            g_cb = jnp.sum(dcv, axis=0, keepdims=True)
            rid = _iota((8, D), 0)
            gv_ref[...] += jnp.where(rid == 0, g_cb, jnp.where(rid == 1, g_lw, jnp.where(rid == 2, g_lb, 0.0)))
            _fill_padded(padd, dcv)
            _fill_padded(pada, ga * sg)
            _row_conv(da_ref, padd, cw_ref, True)
            _row_conv_wgrad(gcw_ref, padd, pada)
            da = da_ref[...]
            sc[:, 0:D] = (da * sg).astype(BF16)
            sc[:, D:2 * D] = (da * ga * sg * (1.0 - sg)).astype(BF16)

        o_ref[...] = sc[:, pl.ds(pl.multiple_of(j * D, 128), D)]

    vec = pl.BlockSpec((1, D), lambda i, j: (0, 0))
    col = lambda c: pl.BlockSpec((T, D), lambda i, j: (i, c))
    return pl.pallas_call(
        kern, name="conf_bwd", grid=(l // T, 3),
        in_specs=[col(0), col(0), col(C_GA), col(C_GB), col(C_CG), pl.BlockSpec((32, D), lambda i, j: (0, 0)), vec, vec,
                  pl.BlockSpec(memory_space=pl.ANY)],
        out_specs=[pl.BlockSpec((T, D), lambda i, j: (i, C_GA + j)), pl.BlockSpec((32, D), lambda i, j: (0, 0)),
                   pl.BlockSpec((8, D), lambda i, j: (0, 0))],
        out_shape=[jax.ShapeDtypeStruct((l, RESTW), BF16), jax.ShapeDtypeStruct((32, D), F32),
                   jax.ShapeDtypeStruct((8, D), F32)],
        scratch_shapes=[pltpu.VMEM((T, 3 * D), BF16), pltpu.VMEM((1, PADR, D), F32), pltpu.VMEM((8, PADR, D), F32),
                        pltpu.VMEM((T, D), F32)],
        input_output_aliases={8: 0},
        compiler_params=_cp(("arbitrary", "arbitrary")),
    )(duc, cv, proj_rest, proj_rest, proj_rest, cw, lw, lb, dpr)


def _merge_fwd(bs, bc, proj_rest):
    l = bs.shape[0]

    def kern(bs_ref, bc_ref, g1_ref, g2_ref, o_ref):
        o_ref[...] = (_sig(g1_ref[...]) * bs_ref[...] + _sig(g2_ref[...]) * bc_ref[...]).astype(BF16)

    blk = pl.BlockSpec((T, D), lambda i: (i, 0))
    return pl.pallas_call(
        kern, name="merge_fwd", grid=(l // T,),
        in_specs=[blk, blk, pl.BlockSpec((T, D), lambda i: (i, C_G1)), pl.BlockSpec((T, D), lambda i: (i, C_G2))],
        out_specs=blk, out_shape=jax.ShapeDtypeStruct((l, D), BF16),
        compiler_params=_cp(("parallel",)),
    )(bs, bc, proj_rest, proj_rest)


def _merge_bwd(dm, bs, bc, proj_rest):
    l = bs.shape[0]

    def kern(dm_ref, bs_ref, bc_ref, g1_ref, g2_ref, dbs_ref, dbc_ref, dg_ref):
        dm_ = dm_ref[...]
        s1, s2 = _sig(g1_ref[...]), _sig(g2_ref[...])
        dbs_ref[...] = (dm_ * s1).astype(BF16)
        dbc_ref[...] = (dm_ * s2).astype(BF16)
        dg_ref[:, 0:D] = (dm_ * bs_ref[...] * s1 * (1.0 - s1)).astype(BF16)
        dg_ref[:, D:2 * D] = (dm_ * bc_ref[...] * s2 * (1.0 - s2)).astype(BF16)

    blk = pl.BlockSpec((T, D), lambda i: (i, 0))
    return pl.pallas_call(
        kern, name="merge_bwd", grid=(l // T,),
        in_specs=[blk, blk, blk, pl.BlockSpec((T, D), lambda i: (i, C_G1)), pl.BlockSpec((T, D), lambda i: (i, C_G2))],
        out_specs=[blk, blk, pl.BlockSpec((T, 2 * D), lambda i: (i, 1))],
        out_shape=[jax.ShapeDtypeStruct((l, D), BF16), jax.ShapeDtypeStruct((l, D), BF16),
                   jax.ShapeDtypeStruct((l, RESTW), BF16)],
        compiler_params=_cp(("parallel",)),
    )(dm, bs, bc, proj_rest, proj_rest)


def _final(x, out, tgt, mod, fw):
    l = x.shape[0]

    def kern(x_ref, o_ref, t_ref, mod_ref, fw_ref, ls_ref, dx2_ref, do_ref, gv_ref):
        @pl.when(pl.program_id(0) == 0)
        def _():
            ls_ref[...] = jnp.zeros_like(ls_ref)
            gv_ref[...] = jnp.zeros_like(gv_ref)

        gate = mod_ref[0:1, 2 * D:3 * D]
        o = o_ref[...]
        x2 = x_ref[...] + gate * o
        r = lax.rsqrt(jnp.mean(x2 * x2, axis=-1, keepdims=True) + EPS)
        yn = x2 * r
        fw_ = fw_ref[...]
        e = yn * fw_ - t_ref[...]
        ls_ref[...] += jnp.full((8, 128), 1.0, F32) * (0.5 / D) * jnp.sum(e * e)
        dy = e * (1.0 / D)
        g_fw = jnp.sum(dy * yn, axis=0, keepdims=True)
        dyn = dy * fw_
        dx2 = r * (dyn - yn * jnp.mean(dyn * yn, axis=-1, keepdims=True))
        g_gate = jnp.sum(dx2 * o, axis=0, keepdims=True)
        rid = _iota((8, D), 0)
        gv_ref[...] += jnp.where(rid == 0, g_fw, jnp.where(rid == 1, g_gate, 0.0))
        dx2_ref[...] = dx2
        do_ref[...] = (dx2 * gate).astype(BF16)

    blk = pl.BlockSpec((T, D), lambda i: (i, 0))
    return pl.pallas_call(
        kern, name="final", grid=(l // T,),
        in_specs=[blk, blk, blk, pl.BlockSpec((8, 3 * D), lambda i: (0, 0)), pl.BlockSpec((1, D), lambda i: (0, 0))],
        out_specs=[pl.BlockSpec((8, 128), lambda i: (0, 0)), blk, blk, pl.BlockSpec((8, D), lambda i: (0, 0))],
        out_shape=[jax.ShapeDtypeStruct((8, 128), F32), jax.ShapeDtypeStruct((l, D), F32),
                   jax.ShapeDtypeStruct((l, D), BF16), jax.ShapeDtypeStruct((8, D), F32)],
        compiler_params=_cp(("arbitrary",)),
    )(x, out, tgt, mod, fw)


def _perm_dt_cols(w):
    s = w.shape[:-1]
    return w.reshape(*s, 2, NG, HPG).swapaxes(-3, -2).reshape(*s, 64)


def _unperm_dt_cols(w):
    s = w.shape[:-1]
    return w.reshape(*s, NG, 2, HPG).swapaxes(-3, -2).reshape(*s, 64)


def _pad_lanes(v, width):
    return jnp.pad(v, ((0, 0), (0, width - v.shape[1])))


def _local_step(x, c, ctx, tgt, w):
    l = x.shape[0]
    nct = CTX // T
    ncc = CTX // Q
    lext = l + CTX

    w_mod = w["w_mod"].astype(BF16)
    w_in = w["w_in"].astype(BF16)
    w_ssd = jnp.concatenate([w_in[:, :XBC], _perm_dt_cols(w_in[:, XBC:XBC + 64]), jnp.zeros((D, 64), BF16)], axis=1)
    wr = w_in[:, XBC + 64:]
    w_rest = jnp.concatenate([wr[:, :DI], wr[:, DI + 3 * D:], wr[:, DI:DI + 3 * D]], axis=1)
    w_os, w_oc, w_o = w["w_out_ssm"].astype(BF16), w["w_out_conf"].astype(BF16), w["w_out"].astype(BF16)
    cw8 = jnp.pad(w["ssm_conv_w"], ((0, 4), (0, 0)))
    cb_s = w["ssm_conv_b"].reshape(1, XBC)
    dtb = _pad_lanes(_perm_dt_cols(w["dt_bias"].reshape(1, 64)), 128)
    a_all = -jnp.exp(w["a_log"].reshape(1, 64))
    a_perm = _pad_lanes(_perm_dt_cols(a_all), 128)
    a_rows = _pad_lanes(_perm_dt_cols(a_all).reshape(NG, 8), 128)
    dsk = jnp.repeat(w["d_skip"].reshape(NH), HP).reshape(1, DI)
    gnw = w["ssm_norm_w"].reshape(1, DI)
    ccw = jnp.pad(w["conf_conv_w"], ((0, 1), (0, 0)))
    ccb, clw, clb = w["conf_conv_b"].reshape(1, D), w["conf_ln_w"].reshape(1, D), w["conf_ln_b"].reshape(1, D)
    nw = w["norm_w"].reshape(1, D)
    fw = w["final_norm_w"].reshape(1, D)
    cc = jnp.concatenate([c.reshape(1, D), w["c_ctx"].reshape(1, D), jnp.zeros((6, D), F32)], axis=0)

    bx = 512
    be = 768 if lext % 768 == 0 else 256
    tk = min(1024, l)
    mod = _mod_fwd(cc, w_mod, w["b_mod"].reshape(1, 3 * D))
    h = _norm_fwd(ctx, x, mod, nw, nct)
    hx = h[CTX:]
    proj_ssd = _mm(h, w_ssd, "nn", lext, SSDW, D, be, SSDW // 3, D, F32, "proj_ssd")
    proj_rest = _mm(hx, w_rest, "nn", l, RESTW, D, bx, 1024, D, F32, "proj_rest")
    xbc = _conv_fwd(proj_ssd, cw8, cb_s, nct)
    dtg, lag, dtt, lat = _dt_fwd(proj_ssd, dtb, a_perm)
    htf, htb = _ssd_state(xbc, dtg, lag, ncc)
    yssm = _ssd_out(xbc, dtg, lag, dtt, lat, htf, htb, ncc)
    gn = _post_fwd(yssm, xbc, proj_rest, dsk, gnw, nct)
    bs = _mm(gn, w_os, "nn", l, D, DI, bx, D, DI, F32, "out_ssm")
    uc, cv = _conf_fwd(proj_rest, ccw, ccb, clw, clb)
    bc = _mm(uc, w_oc, "nn", l, D, D, bx, D, D, F32, "out_conf")
    merged = _merge_fwd(bs, bc, proj_rest)
    out = _mm(merged, w_o, "nn", l, D, D, bx, D, D, F32, "out_proj")
    lsum, dx2, dout, gv_fin = _final(x, out, tgt, mod, fw)

    g = {}
    g["final_norm_w"] = gv_fin[0]
    dmerged = _mm(dout, w_o, "nt", l, D, D, bx, D, D, F32, "d_merged")
    g["w_out"] = _mm(merged, dout, "tn", D, D, l, D, D, tk, F32, "g_w_out")
    dbs, dbc, dpr = _merge_bwd(dmerged, bs, bc, proj_rest)
    dgn = _mm(dbs, w_os, "nt", l, DI, D, bx, DI, D, F32, "d_gn")
    g["w_out_ssm"] = _mm(gn, dbs, "tn", DI, D, l, DI, D, tk, F32, "g_w_out_ssm")
    duc = _mm(dbc, w_oc, "nt", l, D, D, bx, D, D, F32, "d_uc")
    g["w_out_conf"] = _mm(uc, dbc, "tn", D, D, l, D, D, tk, F32, "g_w_out_conf")
    dpr, gcw, gv_conf = _conf_bwd(duc, cv, proj_rest, ccw, clw, clb, dpr)
    g["conf_conv_w"] = gcw[:CK]
    g["conf_conv_b"], g["conf_ln_w"], g["conf_ln_b"] = gv_conf[0], gv_conf[1], gv_conf[2]
    dy, dproj_rest, ggnw, gdsk = _post_bwd(dgn, yssm, xbc, proj_rest, dsk, gnw, dpr, nct)
    g["ssm_norm_w"] = ggnw[0]
    g["d_skip"] = gdsk[0].reshape(NH, HP).sum(axis=1)
    dhf, dhb = _ssd_bwd_state(xbc, dy, lag, ncc)
    dxs, dbm, dcm, ddtg, galog = _ssd_bwd_out(xbc, dy, dsk, dtg, lag, dtt, lat, htf, htb, dhf, dhb, a_rows)
    g["a_log"] = _unperm_dt_cols(galog[:, 0, 0:8].reshape(1, 64)).reshape(2, NH)
    dus, gws, gbs = [], [], []
    for dpost, off, width, nm in ((dxs, 0, DI, "conv_bwd_x"), (dbm, DI, NG * NS, "conv_bwd_b"), (dcm, DI + NG * NS, NG * NS, "conv_bwd_c")):
        du_, gw_, gb_ = _conv_bwd(dpost, proj_ssd, cw8, cb_s, off, width, nct, nm)
        dus.append(du_)
        gws.append(gw_[:SK])
        gbs.append(gb_[0])
    g["ssm_conv_w"] = jnp.concatenate(gws, axis=1)
    g["ssm_conv_b"] = jnp.concatenate(gbs, axis=0)
    ddt_raw, gdtb = _dt_bwd(ddtg, proj_ssd, dtb)
    g["dt_bias"] = _unperm_dt_cols(gdtb[0:1, 0:64]).reshape(2, NH)
    dproj_ssd = jnp.concatenate(dus + [ddt_raw], axis=1)
    gw_ssd = _mm(h, dproj_ssd, "tn", D, SSDW, lext, D, SSDW // 3, be, F32, "g_w_ssd")
    gw_rest = _mm(hx, dproj_rest, "tn", D, RESTW, l, D, 1024, tk, F32, "g_w_rest")
    g["w_in"] = jnp.concatenate([gw_ssd[:, :XBC], _unperm_dt_cols(gw_ssd[:, XBC:XBC + 64]), gw_rest[:, :DI],
                                 gw_rest[:, 2 * DI:], gw_rest[:, DI:2 * DI]], axis=1)
    dh_a = _mm(dproj_ssd, w_ssd, "nt", lext, D, SSDW, T, D, SSDW, F32, "dh_ssd")
    dh_b = _mm(dproj_rest, w_rest, "nt", l, D, RESTW, T, D, RESTW, F32, "dh_rest")
    grad_x, gnw_in, dss = _norm_bwd(dh_a, dh_b, ctx, x, dx2, mod, nw, nct)
    g["norm_w"] = gnw_in[0]
    dmod = jnp.concatenate([jnp.concatenate([dss[0:1], gv_fin[1:2]], axis=1),
                            jnp.concatenate([dss[1:2], jnp.zeros((1, D), F32)], axis=1),
                            jnp.zeros((6, 3 * D), F32)], axis=0)
    gwm, gbm, gcc = _mod_bwd(dmod, cc, cc.T, w_mod)
    g["w_mod"], g["b_mod"], g["c_ctx"] = gwm, gbm[0], gcc[1]
    return lsum[0, 0], grad_x, g


NSHARD = 4
R_MOD, R_IN, R_OS, R_OC, R_O, R_SC, R_CC = 768, 2832, 512, 256, 256, 8, 8
O_MOD = 0
O_OS = O_MOD + R_MOD
O_OC = O_OS + R_OS
O_O = O_OC + R_OC
O_SC = O_O + R_O
O_CC = O_SC + R_SC
PUSED = O_CC + R_CC
PROWS = 1824
HALF = PROWS // 2
RB = HALF // 3
WB = 128
SROWS = 16
SHARDED = ("w_mod", "w_in", "w_out_ssm", "w_out_conf", "w_out", "ssm_conv_w", "conf_conv_w")
SMALL = (("b_mod", 3 * D), ("norm_w", D), ("ssm_conv_b", XBC), ("dt_bias", 64), ("a_log", 64), ("d_skip", NH),
         ("ssm_norm_w", DI), ("conf_conv_b", D), ("conf_ln_w", D), ("conf_ln_b", D), ("final_norm_w", D), ("c_ctx", D))
SMALL_OFF = {"b_mod": 0, "norm_w": 3 * D, "ssm_conv_b": 4 * D, "dt_bias": 8 * D, "a_log": 8 * D + 64, "d_skip": 8 * D + 128,
             "ssm_norm_w": 9 * D, "conf_conv_b": 11 * D, "conf_ln_w": 12 * D, "conf_ln_b": 13 * D, "final_norm_w": 14 * D,
             "c_ctx": 15 * D}


def _pack_shard(s):
    return jnp.concatenate([s["w_mod"].reshape(R_MOD, D), _pack_rest(s), jnp.zeros((PROWS - PUSED, D), F32)], axis=0)


def _pack_rest(s):
    cc = jnp.pad(s["conf_conv_w"].reshape(1, CK * 256), ((0, 0), (0, R_CC * D - CK * 256))).reshape(R_CC, D)
    return jnp.concatenate([s["w_out_ssm"], s["w_out_conf"], s["w_out"],
                            jnp.pad(s["ssm_conv_w"], ((0, R_SC - SK), (0, 0))), cc], axis=0)


def _unpack_rest(p):
    o = lambda r: r - O_OS
    return {"w_out_ssm": p[o(O_OS):o(O_OC)][None], "w_out_conf": p[o(O_OC):o(O_O)][None], "w_out": p[o(O_O):o(O_SC)][None],
            "ssm_conv_w": p[o(O_SC):o(O_SC) + SK][None],
            "conf_conv_w": p[o(O_CC):o(O_CC) + R_CC].reshape(R_CC * D)[:CK * 256].reshape(1, CK, 256)}


def _shard_cols(a, n):
    return a.reshape(a.shape[0], NSHARD, n).transpose(1, 0, 2)


def _pack_full(g):
    cc = jnp.pad(_shard_cols(g["conf_conv_w"], 256).reshape(NSHARD, CK * 256), ((0, 0), (0, R_CC * D - CK * 256)))
    return jnp.concatenate([_shard_cols(g["w_mod"], R_MOD).reshape(NSHARD, R_MOD, D),
                            g["w_out_ssm"].reshape(NSHARD, R_OS, D), g["w_out_conf"].reshape(NSHARD, R_OC, D),
                            g["w_out"].reshape(NSHARD, R_O, D),
                            jnp.pad(_shard_cols(g["ssm_conv_w"], D), ((0, 0), (0, R_SC - SK), (0, 0))),
                            cc.reshape(NSHARD, R_CC, D), jnp.zeros((NSHARD, PROWS - PUSED, D), F32)], axis=1)


def _unpack_gathered(gm, gw, gs):
    def cols(a, r, n):
        return a.reshape(NSHARD, r, n).transpose(1, 0, 2).reshape(r, NSHARD * n)
    return {"w_mod": cols(gm[:, O_MOD:O_OS], D, R_MOD), "w_in": cols(gw, D, R_IN),
            "w_out_ssm": gm[:, O_OS:O_OC].reshape(DI, D), "w_out_conf": gm[:, O_OC:O_O].reshape(D, D),
            "w_out": gm[:, O_O:O_SC].reshape(D, D), "ssm_conv_w": cols(gs[:, 0:SK], SK, D),
            "conf_conv_w": cols(gs[:, R_SC:R_SC + R_CC].reshape(NSHARD, R_CC * D)[:, :CK * 256], CK, 256)}


def _pack_small(d):
    flat = jnp.zeros((SROWS * D,), F32)
    for name, n in SMALL:
        flat = lax.dynamic_update_slice(flat, d[name].reshape(n).astype(F32), (SMALL_OFF[name],))
    return flat.reshape(SROWS, D)


def _unpack_small(p, shapes):
    flat = p.reshape(SROWS * D)
    return {name: flat[SMALL_OFF[name]:SMALL_OFF[name] + n].reshape(shapes[name]) for name, n in SMALL}


MESH_ID = pl.DeviceIdType.MESH
ANY = pl.BlockSpec(memory_space=pl.ANY)


def _place():
    x, y, c = lax.axis_index("x"), lax.axis_index("y"), lax.axis_index("c")
    return x, y, c, [(1 - x, y), (x, 1 - y), (1 - x, 1 - y)]


def _rcopy(src, dst, send, recv, dev):
    return pltpu.make_async_remote_copy(src_ref=src, dst_ref=dst, send_sem=send, recv_sem=recv,
                                        device_id=dev, device_id_type=MESH_ID)


def _gather_weights(mats, small):
    n = len(mats)

    def kern(*refs):
        m_refs, s_ref, g_refs, gs_ref, (send, recv) = refs[:n], refs[n], refs[n + 1:2 * n + 1], refs[2 * n + 1], refs[2 * n + 2:]
        x, y, c, chips = _place()
        me = 2 * x + y
        sib = (x, y, 1 - c)
        first, passed = [], []
        for k, (px, py) in enumerate(chips):
            first.append(_rcopy(s_ref, gs_ref.at[me], send.at[k], recv.at[k], (px, py, c)))
            for a, (m_ref, g_ref) in enumerate(zip(m_refs, g_refs)):
                mine = _half_rows(c, m_ref.shape[0])
                first.append(_rcopy(m_ref.at[mine], g_ref.at[me, mine], send.at[3 + 6 * a + k], recv.at[3 + 6 * a + k], (px, py, c)))
        for cp in first:
            cp.start()
        for k, (px, py) in enumerate(chips):
            s = 2 * px + py
            for a, (m_ref, g_ref) in enumerate(zip(m_refs, g_refs)):
                mine = _half_rows(c, m_ref.shape[0])
                _rcopy(m_ref.at[mine], g_ref.at[s, mine], send.at[3 + 6 * a + k], recv.at[3 + 6 * a + k], sib).wait_recv()
                f = _rcopy(g_ref.at[s, mine], g_ref.at[s, mine], send.at[6 + 6 * a + k], recv.at[6 + 6 * a + k], sib)
                f.start()
                passed.append(f)
        for k, (px, py) in enumerate(chips):
            s = 2 * px + py
            _rcopy(s_ref, gs_ref.at[s], send.at[k], recv.at[k], sib).wait_recv()
            for a, g_ref in enumerate(g_refs):
                other = _half_rows(1 - c, g_ref.shape[1])
                _rcopy(g_ref.at[s, other], g_ref.at[s, other], send.at[6 + 6 * a + k], recv.at[6 + 6 * a + k], sib).wait_recv()
        for cp in first + passed:
            cp.wait_send()

    nsem = 3 + 6 * n
    return pl.pallas_call(
        kern, name="gather_weights", in_specs=[ANY] * (n + 1), out_specs=[ANY] * (n + 1),
        out_shape=[jax.ShapeDtypeStruct((NSHARD,) + m.shape, m.dtype) for m in mats]
        + [jax.ShapeDtypeStruct((NSHARD, SROWS, D), F32)],
        scratch_shapes=[pltpu.SemaphoreType.DMA((nsem,)), pltpu.SemaphoreType.DMA((nsem,))],
    )(*mats, small)


def _half_rows(c, rows):
    return pl.ds(pl.multiple_of(c * (rows // 2), 16), rows // 2)


def _swap_halves(gs):
    n = len(gs)

    def kern(*refs):
        g_refs, o_refs, (send, recv) = refs[:n], refs[n:2 * n], refs[2 * n:]
        x, y, c, _ = _place()
        cps = [_rcopy(g_ref.at[s, _half_rows(1 - c, g_ref.shape[1])], o_ref.at[s], send.at[NSHARD * a + s],
                      recv.at[NSHARD * a + s], (x, y, 1 - c))
               for a, (g_ref, o_ref) in enumerate(zip(g_refs, o_refs)) for s in range(NSHARD)]
        for cp in cps:
            cp.start()
        for cp in cps:
            cp.wait()

    return pl.pallas_call(
        kern, name="swap_halves", in_specs=[ANY] * n, out_specs=[ANY] * n,
        out_shape=[jax.ShapeDtypeStruct((NSHARD, g.shape[1] // 2, g.shape[2]), F32) for g in gs],
        scratch_shapes=[pltpu.SemaphoreType.DMA((NSHARD * n,)), pltpu.SemaphoreType.DMA((NSHARD * n,))],
    )(*gs)


def _add_halves(cidx, g, ra, rb, name):
    _, half, cols = ra.shape
    nb = half // rb

    def kern(c_ref, g_ref, a_ref, o_ref):
        o_ref[...] = (g_ref[...] + a_ref[...]).astype(BF16)

    return pl.pallas_call(
        kern, name=name,
        grid_spec=pltpu.PrefetchScalarGridSpec(
            num_scalar_prefetch=1, grid=(NSHARD, nb),
            in_specs=[pl.BlockSpec((None, rb, cols), lambda s, i, c: (s, c[0] * nb + i, 0)),
                      pl.BlockSpec((None, rb, cols), lambda s, i, c: (s, i, 0))],
            out_specs=pl.BlockSpec((None, rb, cols), lambda s, i, c: (s, i, 0))),
        out_shape=jax.ShapeDtypeStruct((NSHARD, half, cols), BF16),
        compiler_params=_cp(("parallel", "parallel")),
    )(cidx, g, ra)


def _exchange_chips(ps):
    n = len(ps)

    def kern(*refs):
        p_refs, o_refs, (send, recv) = refs[:n], refs[n:2 * n], refs[2 * n:]
        x, y, c, chips = _place()
        cps = [_rcopy(p_ref.at[2 * px + py], o_ref.at[k], send.at[3 * a + k], recv.at[3 * a + k], (px, py, c))
               for a, (p_ref, o_ref) in enumerate(zip(p_refs, o_refs)) for k, (px, py) in enumerate(chips)]
        for cp in cps:
            cp.start()
        for cp in cps:
            cp.wait()

    return pl.pallas_call(
        kern, name="exchange_chips", in_specs=[ANY] * n, out_specs=[ANY] * n,
        out_shape=[jax.ShapeDtypeStruct((3,) + p.shape[1:], p.dtype) for p in ps],
        scratch_shapes=[pltpu.SemaphoreType.DMA((3 * n,)), pltpu.SemaphoreType.DMA((3 * n,))],
    )(*ps)


def _add_chips(mc, g, ra, rx, rb, name):
    _, half, cols = ra.shape
    nb = half // rb

    def kern(m_ref, g_ref, a_ref, r0_ref, r1_ref, r2_ref, o_ref):
        own = g_ref[...] + a_ref[...]
        o_ref[...] = ((own + r0_ref[...].astype(F32)) + r1_ref[...].astype(F32)) + r2_ref[...].astype(F32)

    return pl.pallas_call(
        kern, name=name,
        grid_spec=pltpu.PrefetchScalarGridSpec(
            num_scalar_prefetch=1, grid=(nb,),
            in_specs=[pl.BlockSpec((None, rb, cols), lambda i, m: (m[0], m[1] * nb + i, 0)),
                      pl.BlockSpec((None, rb, cols), lambda i, m: (m[0], i, 0))]
            + [pl.BlockSpec((None, rb, cols), functools.partial(lambda i, m, k: (k, i, 0), k=k)) for k in range(3)],
            out_specs=pl.BlockSpec((rb, cols), lambda i, m: (i, 0))),
        out_shape=jax.ShapeDtypeStruct((half, cols), F32),
        compiler_params=_cp(("parallel",)),
    )(mc, g, ra, rx, rx, rx)


def _share_halves(rs):
    n = len(rs)

    def kern(*refs):
        r_refs, o_refs, (send, recv) = refs[:n], refs[n:2 * n], refs[2 * n:]
        x, y, c, _ = _place()
        cps = [_rcopy(r_ref, o_ref, send.at[a], recv.at[a], (x, y, 1 - c))
               for a, (r_ref, o_ref) in enumerate(zip(r_refs, o_refs))]
        for cp in cps:
            cp.start()
        for cp in cps:
            cp.wait()

    return pl.pallas_call(
        kern, name="share_halves", in_specs=[ANY] * n, out_specs=[ANY] * n,
        out_shape=[jax.ShapeDtypeStruct(r.shape, F32) for r in rs],
        scratch_shapes=[pltpu.SemaphoreType.DMA((n,)), pltpu.SemaphoreType.DMA((n,))],
    )(*rs)


def _reduce_small(s):
    def kern(s_ref, o_ref, buf, send, recv):
        x, y, c, _ = _place()
        me = 4 * x + 2 * y + c
        buf[me] = s_ref[...]
        cps = []
        for r in range(1, 8):
            peer = (1 - x if r & 4 else x, 1 - y if r & 2 else y, 1 - c if r & 1 else c)
            cps.append(_rcopy(s_ref, buf.at[me], send.at[r - 1], recv.at[r - 1], peer))
        for cp in cps:
            cp.start()
        for cp in cps:
            cp.wait()
        acc = buf[0]
        for i in range(1, 8):
            acc = acc + buf[i]
        o_ref[...] = acc

    return pl.pallas_call(
        kern, name="reduce_small",
        in_specs=[pl.BlockSpec(memory_space=pltpu.VMEM)], out_specs=pl.BlockSpec(memory_space=pltpu.VMEM),
        out_shape=jax.ShapeDtypeStruct((SROWS, D), F32),
        scratch_shapes=[pltpu.VMEM((8, SROWS, D), F32), pltpu.SemaphoreType.DMA((7,)), pltpu.SemaphoreType.DMA((7,))],
    )(s)


def _adamw(g, w, m, v, rb, name):
    rows, cols = g.shape

    def kern(g_ref, w_ref, m_ref, v_ref, d_ref, nm_ref, nv_ref):
        g_ = g_ref[...]
        m_ = ADAM_B1 * m_ref[...] + (1.0 - ADAM_B1) * g_
        v_ = ADAM_B2 * v_ref[...] + (1.0 - ADAM_B2) * jnp.square(g_)
        m_hat = m_ / (1.0 - ADAM_B1 ** ADAM_STEP)
        v_hat = v_ / (1.0 - ADAM_B2 ** ADAM_STEP)
        d_ref[...] = -ADAM_LR * (m_hat / (jnp.sqrt(v_hat) + ADAM_EPS) + ADAM_WD * w_ref[...])
        nm_ref[...] = m_
        nv_ref[...] = v_

    assert rows % rb == 0
    blk = pl.BlockSpec((rb, cols), lambda i: (i, 0))
    return pl.pallas_call(
        kern, name=name, grid=(rows // rb,), in_specs=[blk] * 4, out_specs=[blk] * 3,
        out_shape=[jax.ShapeDtypeStruct((rows, cols), F32)] * 3,
        compiler_params=_cp(("parallel",)),
    )(g, w, m, v)


WEIGHTS = ("c_ctx", "w_mod", "b_mod", "norm_w", "w_in", "ssm_conv_w", "ssm_conv_b", "dt_bias", "a_log", "d_skip",
           "ssm_norm_w", "w_out_ssm", "conf_conv_w", "conf_conv_b", "conf_ln_w", "conf_ln_b", "w_out_conf", "w_out",
           "final_norm_w")


def kernel(x, c, ctx, c_ctx, w_mod, b_mod, norm_w, w_in, ssm_conv_w, ssm_conv_b, dt_bias, a_log, d_skip, ssm_norm_w, w_out_ssm, conf_conv_w, conf_conv_b, conf_ln_w, conf_ln_b, w_out_conf, w_out, final_norm_w, loss_target, m_c_ctx, m_w_mod, m_b_mod, m_norm_w, m_w_in, m_ssm_conv_w, m_ssm_conv_b, m_dt_bias, m_a_log, m_d_skip, m_ssm_norm_w, m_w_out_ssm, m_conf_conv_w, m_conf_conv_b, m_conf_ln_w, m_conf_ln_b, m_w_out_conf, m_w_out, m_final_norm_w, v_c_ctx, v_w_mod, v_b_mod, v_norm_w, v_w_in, v_ssm_conv_w, v_ssm_conv_b, v_dt_bias, v_a_log, v_d_skip, v_ssm_norm_w, v_w_out_ssm, v_conf_conv_w, v_conf_conv_b, v_conf_ln_w, v_conf_ln_b, v_w_out_conf, v_w_out, v_final_norm_w):
    wv = (c_ctx, w_mod, b_mod, norm_w, w_in, ssm_conv_w, ssm_conv_b, dt_bias, a_log, d_skip, ssm_norm_w, w_out_ssm,
          conf_conv_w, conf_conv_b, conf_ln_w, conf_ln_b, w_out_conf, w_out, final_norm_w)
    mv = (m_c_ctx, m_w_mod, m_b_mod, m_norm_w, m_w_in, m_ssm_conv_w, m_ssm_conv_b, m_dt_bias, m_a_log, m_d_skip,
          m_ssm_norm_w, m_w_out_ssm, m_conf_conv_w, m_conf_conv_b, m_conf_ln_w, m_conf_ln_b, m_w_out_conf, m_w_out,
          m_final_norm_w)
    vv = (v_c_ctx, v_w_mod, v_b_mod, v_norm_w, v_w_in, v_ssm_conv_w, v_ssm_conv_b, v_dt_bias, v_a_log, v_d_skip,
          v_ssm_norm_w, v_w_out_ssm, v_conf_conv_w, v_conf_conv_b, v_conf_ln_w, v_conf_ln_b, v_w_out_conf, v_w_out,
          v_final_norm_w)
    shapes = {n: a.shape for n, a in zip(WEIGHTS, wv)}

    def squeeze(d):
        return {n: (a if n in ("c_ctx", "final_norm_w") else a[0]) for n, a in d.items()}

    w, m, v = (squeeze(dict(zip(WEIGHTS, t))) for t in (wv, mv, vv))

    my_chip = 2 * lax.axis_index("x") + lax.axis_index("y")
    my_core = lax.axis_index("c")

    pw = _pack_shard(w)
    pwb, wib, psm = pw.astype(BF16), w["w_in"].astype(BF16), pw[O_SC:O_SC + SROWS]
    gm, gw, gs = _gather_weights([pwb, wib], psm)
    gm = lax.dynamic_update_slice(gm, pwb[None], (my_chip, 0, 0))
    gw = lax.dynamic_update_slice(gw, wib[None], (my_chip, 0, 0))
    gs = lax.dynamic_update_slice(gs, psm[None], (my_chip, 0, 0))
    full = dict(w)
    full.update(_unpack_gathered(gm, gw, gs))

    lsum, grad_x, g = _local_step(x[0], c, ctx[0], loss_target[0], full)
    loss = lax.psum(lsum, ("x", "y", "c"))

    cidx = my_core.astype(jnp.int32).reshape(1)
    mc = jnp.stack([my_chip, my_core]).astype(jnp.int32)
    gsrc = [_pack_full(g), _shard_cols(g["w_in"], R_IN)]
    blocks = (RB, WB)
    sib = _swap_halves(gsrc)
    part = [_add_halves(cidx, a, b, rb, "add_halves_%d" % i) for i, (a, b, rb) in enumerate(zip(gsrc, sib, blocks))]
    far = _exchange_chips(part)
    red = [_add_chips(mc, a, b, f, rb, "add_chips_%d" % i) for i, (a, b, f, rb) in enumerate(zip(gsrc, sib, far, blocks))]
    got = _share_halves(red)
    g_pk, g_win = (jnp.concatenate([jnp.where(my_core == 0, r, o), jnp.where(my_core == 0, o, r)], axis=0)
                   for r, o in zip(red, got))
    g_sm = _reduce_small(_pack_small(g))

    gr = {"w_mod": g_pk[O_MOD:O_OS].reshape(D, R_MOD), "w_in": g_win, "rest": g_pk[O_OS:PUSED]}
    wr, mr, vr = ({"w_mod": t["w_mod"], "w_in": t["w_in"], "rest": _pack_rest(t)} for t in (w, m, v))
    res = {k: _adamw(gr[k], wr[k], mr[k], vr[k], rb, "adamw_" + k)
           for k, rb in (("w_in", WB), ("w_mod", 512), ("rest", (PUSED - O_OS) // 2))}
    res_sm = _adamw(g_sm, _pack_small(w), _pack_small(m), _pack_small(v), SROWS, "adamw_small")

    outs = []
    for i in range(4):
        pick = (lambda k: gr[k]) if i == 0 else (lambda k: res[k][i - 1])
        d = {"w_mod": pick("w_mod")[None], "w_in": pick("w_in")[None]}
        d.update(_unpack_rest(pick("rest")))
        d.update(_unpack_small(g_sm if i == 0 else res_sm[i - 1], shapes))
        outs.extend(d[n] for n in WEIGHTS)
    return (loss, grad_x[None], *outs)
```

```python
import functools

import jax
import jax.numpy as jnp
from jax import lax
from jax.experimental import pallas as pl
from jax.experimental.pallas import tpu as pltpu

F32, BF16 = jnp.float32, jnp.bfloat16

D = 1024
DI = 2048
NH = 32
HP = 64
NG = 8
HPG = 4
NS = 128
Q = 128
GW = 64
CK = 31
SK = 4
CTX = 256
EPS = 1e-6
XBC = DI + 2 * NG * NS
SSDW = XBC + 128
RESTW = 7168
T = 256
VMEM_LIMIT = 56 * 1024 * 1024

ADAM_LR, ADAM_B1, ADAM_B2, ADAM_EPS, ADAM_WD, ADAM_STEP = 0.001, 0.9, 0.999, 1e-08, 0.01, 10


def _cp(sem):
    return pltpu.CompilerParams(dimension_semantics=sem, vmem_limit_bytes=VMEM_LIMIT)


def _sig(x):
    return jax.nn.sigmoid(x)


def _silu(x):
    return x * _sig(x)


def _dsilu(x):
    s = _sig(x)
    return s * (1.0 + x * (1.0 - s))


def _dot(a, b):
    return jnp.dot(a, b, preferred_element_type=F32)


def _dot_nt(a, b):
    return lax.dot_general(a, b, (((1,), (1,)), ((), ())), preferred_element_type=F32)


def _split3(x):
    h = x.astype(BF16)
    r = x - h.astype(F32)
    m = r.astype(BF16)
    l = (r - m.astype(F32)).astype(BF16)
    return h, m, l


def _dot3_l(sel, x):
    h, m, l = _split3(x)
    return _dot(sel, h) + _dot(sel, m) + _dot(sel, l)


def _dot3_r(x, sel):
    h, m, l = _split3(x)
    return _dot(h, sel) + _dot(m, sel) + _dot(l, sel)


def _split2(x):
    h = x.astype(BF16)
    return h, (x - h.astype(F32)).astype(BF16)


def _dot2_l(sel, x):
    h, l = _split2(x)
    return _dot(sel, h) + _dot(sel, l)


def _dot2_r(x, sel):
    h, l = _split2(x)
    return _dot(h, sel) + _dot(l, sel)


def _iota(shape, dim):
    return lax.broadcasted_iota(jnp.int32, shape, dim)


def _mm(a, b, dims, m, n, k, bm, bn, bk, out_dtype, name):
    nk = k // bk
    assert m % bm == 0 and n % bn == 0 and k % bk == 0, (name, m, n, k, bm, bn, bk)

    def prod(a_ref, b_ref):
        av = a_ref[...].astype(BF16)
        bv = b_ref[...].astype(BF16)
        if dims == "nn":
            return _dot(av, bv)
        if dims == "nt":
            return _dot_nt(av, bv)
        return lax.dot_general(av, bv, (((0,), (0,)), ((), ())), preferred_element_type=F32)

    def kern_one(a_ref, b_ref, o_ref):
        o_ref[...] = prod(a_ref, b_ref).astype(out_dtype)

    def kern_acc(a_ref, b_ref, o_ref, acc):
        kk = pl.program_id(2)

        @pl.when(kk == 0)
        def _():
            acc[...] = jnp.zeros_like(acc)

        acc[...] += prod(a_ref, b_ref)

        @pl.when(kk == nk - 1)
        def _():
            o_ref[...] = acc[...].astype(out_dtype)

    if dims == "nn":
        a_spec = pl.BlockSpec((bm, bk), lambda j, i, kk: (i, kk))
        b_spec = pl.BlockSpec((bk, bn), lambda j, i, kk: (kk, j))
    elif dims == "nt":
        a_spec = pl.BlockSpec((bm, bk), lambda j, i, kk: (i, kk))
        b_spec = pl.BlockSpec((bn, bk), lambda j, i, kk: (j, kk))
    else:
        a_spec = pl.BlockSpec((bk, bm), lambda j, i, kk: (kk, i))
        b_spec = pl.BlockSpec((bk, bn), lambda j, i, kk: (kk, j))
    return pl.pallas_call(
        kern_one if nk == 1 else kern_acc, name=name,
        grid=(n // bn, m // bm, nk),
        in_specs=[a_spec, b_spec],
        out_specs=pl.BlockSpec((bm, bn), lambda j, i, kk: (i, j)),
        out_shape=jax.ShapeDtypeStruct((m, n), out_dtype),
        scratch_shapes=[] if nk == 1 else [pltpu.VMEM((bm, bn), F32)],
        compiler_params=_cp(("parallel", "parallel", "arbitrary")),
    )(a, b)


def _mod_fwd(cc, w_mod, b_mod):
    def kern(cc_ref, w_ref, b_ref, o_ref):
        s = _silu(cc_ref[...]).astype(BF16)
        o_ref[...] = _dot(s, w_ref[...]) + b_ref[...]

    return pl.pallas_call(
        kern, name="mod_fwd", grid=(3,),
        in_specs=[pl.BlockSpec((8, D), lambda j: (0, 0)), pl.BlockSpec((D, D), lambda j: (0, j)),
                  pl.BlockSpec((1, D), lambda j: (0, j))],
        out_specs=pl.BlockSpec((8, D), lambda j: (0, j)),
        out_shape=jax.ShapeDtypeStruct((8, 3 * D), F32),
        compiler_params=_cp(("parallel",)),
    )(cc, w_mod, b_mod)


def _mod_bwd(dmod, cc, cct, w_mod):
    def kern(dm_ref, cc_ref, cct_ref, w_ref, gw_ref, gb_ref, gc_ref):
        kk = pl.program_id(0)
        dm = dm_ref[...]
        sct = _silu(cct_ref[...])
        gw_ref[...] = sct[:, 0:1] * dm[0:1, :] + sct[:, 1:2] * dm[1:2, :]
        gb_ref[...] = jnp.broadcast_to(dm[0:1, :] + dm[1:2, :], dm.shape)

        @pl.when(kk == 0)
        def _():
            gc_ref[...] = jnp.zeros_like(gc_ref)

        gc_ref[...] += _dot_nt(dm.astype(BF16), w_ref[...])

        @pl.when(kk == 2)
        def _():
            gc_ref[...] = gc_ref[...] * _dsilu(cc_ref[...])

    return pl.pallas_call(
        kern, name="mod_bwd", grid=(3,),
        in_specs=[pl.BlockSpec((8, D), lambda j: (0, j)), pl.BlockSpec((8, D), lambda j: (0, 0)),
                  pl.BlockSpec((D, 8), lambda j: (0, 0)), pl.BlockSpec((D, D), lambda j: (0, j))],
        out_specs=[pl.BlockSpec((D, D), lambda j: (0, j)), pl.BlockSpec((8, D), lambda j: (0, j)),
                   pl.BlockSpec((8, D), lambda j: (0, 0))],
        out_shape=[jax.ShapeDtypeStruct((D, 3 * D), F32), jax.ShapeDtypeStruct((8, 3 * D), F32),
                   jax.ShapeDtypeStruct((8, D), F32)],
        compiler_params=_cp(("arbitrary",)),
    )(dmod, cc, cct, w_mod)


def _ext_specs(nct):
    return (pl.BlockSpec((T, D), lambda i: (jnp.minimum(i, nct - 1), 0)),
            pl.BlockSpec((T, D), lambda i: (jnp.maximum(i - nct, 0), 0)))


def _norm_fwd(ctx, xl, mod, nw, nct):
    lext = ctx.shape[0] + xl.shape[0]

    def kern(c_ref, x_ref, mod_ref, nw_ref, h_ref):
        is_ctx = pl.program_id(0) < nct
        x = jnp.where(is_ctx, c_ref[...], x_ref[...])
        r = lax.rsqrt(jnp.mean(x * x, axis=-1, keepdims=True) + EPS)
        xn = x * r * nw_ref[...]
        shift = jnp.where(is_ctx, mod_ref[1:2, 0:D], mod_ref[0:1, 0:D])
        scale = jnp.where(is_ctx, mod_ref[1:2, D:2 * D], mod_ref[0:1, D:2 * D])
        h_ref[...] = (xn * (1.0 + scale) + shift).astype(BF16)

    return pl.pallas_call(
        kern, name="norm_fwd", grid=(lext // T,),
        in_specs=[*_ext_specs(nct), pl.BlockSpec((8, 3 * D), lambda i: (0, 0)),
                  pl.BlockSpec((1, D), lambda i: (0, 0))],
        out_specs=pl.BlockSpec((T, D), lambda i: (i, 0)),
        out_shape=jax.ShapeDtypeStruct((lext, D), BF16),
        compiler_params=_cp(("parallel",)),
    )(ctx, xl, mod, nw)


def _norm_bwd(dha, dhb, ctx, xl, dx2, mod, nw, nct):
    lext = ctx.shape[0] + xl.shape[0]
    ntl = lext // T

    def kern(dha_ref, dhb_ref, c_ref, x_ref, dx2_ref, mod_ref, nw_ref, gx_ref, gnw_ref, dss_ref):
        i = pl.program_id(0)
        is_ctx = i < nct

        @pl.when(i == 0)
        def _():
            gnw_ref[...] = jnp.zeros_like(gnw_ref)
            dss_ref[...] = jnp.zeros_like(dss_ref)

        x = jnp.where(is_ctx, c_ref[...], x_ref[...])
        dh_ = dha_ref[...] + jnp.where(is_ctx, 0.0, dhb_ref[...])
        nw_ = nw_ref[...]
        r = lax.rsqrt(jnp.mean(x * x, axis=-1, keepdims=True) + EPS)
        xn = x * r
        scale = jnp.where(is_ctx, mod_ref[1:2, D:2 * D], mod_ref[0:1, D:2 * D])
        dsh = jnp.sum(dh_, axis=0, keepdims=True)
        dsc = jnp.sum(dh_ * (xn * nw_), axis=0, keepdims=True)
        row = jnp.concatenate([dsh, dsc], axis=1)
        rid = _iota((8, 2 * D), 0)
        dss_ref[...] += jnp.where(rid == jnp.where(is_ctx, 1, 0), row, 0.0)
        dxnw = dh_ * (1.0 + scale)
        gnw_ref[...] += jnp.broadcast_to(jnp.sum(dxnw * xn, axis=0, keepdims=True), (8, D))
        dxn = dxnw * nw_
        dx = r * (dxn - xn * jnp.mean(dxn * xn, axis=-1, keepdims=True))
        gx_ref[...] = dx2_ref[...] + dx

    return pl.pallas_call(
        kern, name="norm_bwd", grid=(ntl,),
        in_specs=[pl.BlockSpec((T, D), lambda i: (i, 0)), pl.BlockSpec((T, D), lambda i: (jnp.maximum(i - nct, 0), 0)),
                  *_ext_specs(nct),
                  pl.BlockSpec((T, D), lambda i: (jnp.maximum(i - nct, 0), 0)),
                  pl.BlockSpec((8, 3 * D), lambda i: (0, 0)), pl.BlockSpec((1, D), lambda i: (0, 0))],
        out_specs=[pl.BlockSpec((T, D), lambda i: (jnp.maximum(i - nct, 0), 0)),
                   pl.BlockSpec((8, D), lambda i: (0, 0)), pl.BlockSpec((8, 2 * D), lambda i: (0, 0))],
        out_shape=[jax.ShapeDtypeStruct((lext - nct * T, D), F32), jax.ShapeDtypeStruct((8, D), F32),
                   jax.ShapeDtypeStruct((8, 2 * D), F32)],
        compiler_params=_cp(("arbitrary",)),
    )(dha, dhb, ctx, xl, dx2, mod, nw)


CB = 1024


def _halo_specs(width_blk, col_off_blocks, ntl):
    t8 = T // 8
    main = pl.BlockSpec((T, width_blk), lambda j, i: (i, j + col_off_blocks))
    prev = pl.BlockSpec((8, width_blk), lambda j, i: (jnp.maximum(i * t8 - 1, 0), j + col_off_blocks))
    nxt = pl.BlockSpec((8, width_blk), lambda j, i: (jnp.minimum((i + 1) * t8, ntl * t8 - 1), j + col_off_blocks))
    return main, prev, nxt


def _seq_edges(i, nct, ntl):
    starts = jnp.logical_or(i == 0, i == nct)
    ends = jnp.logical_or(i == nct - 1, i == ntl - 1)
    return starts, ends


def _shifted(ext, off):
    n = ext.shape[0]
    return pltpu.roll(ext, (-off) % n, axis=0)[8:8 + T]


def _conv_fwd(proj_ssd, cw, cb, nct):
    lext = proj_ssd.shape[0]
    ntl = lext // T

    def kern(u_ref, up_ref, un_ref, w_ref, b_ref, o_ref):
        i = pl.program_id(1)
        starts, ends = _seq_edges(i, nct, ntl)
        up = jnp.where(starts, 0.0, up_ref[...])
        un = jnp.where(ends, 0.0, un_ref[...])
        ext = jnp.concatenate([up, u_ref[...], un], axis=0)
        w = w_ref[...]
        pre = b_ref[...] + w[0:1] * _shifted(ext, -2) + w[1:2] * _shifted(ext, -1) \
            + w[2:3] * u_ref[...] + w[3:4] * _shifted(ext, 1)
        o_ref[...] = _silu(pre)

    main, prev, nxt = _halo_specs(CB, 0, ntl)
    return pl.pallas_call(
        kern, name="conv_fwd", grid=(XBC // CB, ntl),
        in_specs=[main, prev, nxt, pl.BlockSpec((8, CB), lambda j, i: (0, j)), pl.BlockSpec((1, CB), lambda j, i: (0, j))],
        out_specs=pl.BlockSpec((T, CB), lambda j, i: (i, j)),
        out_shape=jax.ShapeDtypeStruct((lext, XBC), F32),
        compiler_params=_cp(("parallel", "parallel")),
    )(proj_ssd, proj_ssd, proj_ssd, cw, cb)


def _conv_bwd(dpost, proj_ssd, cw, cb, col_off, width, nct, name):
    lext = proj_ssd.shape[0]
    ntl = lext // T
    cob = col_off // CB

    def kern(u_ref, up_ref, un_ref, d_ref, dp_ref, dn_ref, w_ref, b_ref, du_ref, gw_ref, gb_ref):
        i = pl.program_id(1)

        @pl.when(i == 0)
        def _():
            gw_ref[...] = jnp.zeros_like(gw_ref)
            gb_ref[...] = jnp.zeros_like(gb_ref)

        starts, ends = _seq_edges(i, nct, ntl)
        ext = jnp.concatenate([jnp.where(starts, 0.0, up_ref[...]), u_ref[...], jnp.where(ends, 0.0, un_ref[...])], axis=0)
        dext = jnp.concatenate([jnp.where(starts, 0.0, dp_ref[...]), d_ref[...], jnp.where(ends, 0.0, dn_ref[...])], axis=0)
        w = w_ref[...]
        n = ext.shape[0]
        pre = b_ref[...] + w[0:1] * pltpu.roll(ext, 2, axis=0) + w[1:2] * pltpu.roll(ext, 1, axis=0) \
            + w[2:3] * ext + w[3:4] * pltpu.roll(ext, n - 1, axis=0)
        dpre = dext * _dsilu(pre)
        dm = dpre[8:8 + T]
        du = w[0:1] * _shifted(dpre, 2) + w[1:2] * _shifted(dpre, 1) + w[2:3] * dm + w[3:4] * _shifted(dpre, -1)
        du_ref[...] = du.astype(BF16)
        g0 = jnp.sum(dm * _shifted(ext, -2), axis=0, keepdims=True)
        g1 = jnp.sum(dm * _shifted(ext, -1), axis=0, keepdims=True)
        g2 = jnp.sum(dm * u_ref[...], axis=0, keepdims=True)
        g3 = jnp.sum(dm * _shifted(ext, 1), axis=0, keepdims=True)
        rid = _iota((8, CB), 0)
        gw_ref[...] += jnp.where(rid == 0, g0, jnp.where(rid == 1, g1, jnp.where(rid == 2, g2, jnp.where(rid == 3, g3, 0.0))))
        gb_ref[...] += jnp.broadcast_to(jnp.sum(dm, axis=0, keepdims=True), (8, CB))

    main, prev, nxt = _halo_specs(CB, cob, ntl)
    dmain, dprev, dnxt = _halo_specs(CB, 0, ntl)
    return pl.pallas_call(
        kern, name=name, grid=(width // CB, ntl),
        in_specs=[main, prev, nxt, dmain, dprev, dnxt,
                  pl.BlockSpec((8, CB), lambda j, i: (0, j + cob)), pl.BlockSpec((1, CB), lambda j, i: (0, j + cob))],
        out_specs=[pl.BlockSpec((T, CB), lambda j, i: (i, j)), pl.BlockSpec((8, CB), lambda j, i: (0, j)),
                   pl.BlockSpec((8, CB), lambda j, i: (0, j))],
        out_shape=[jax.ShapeDtypeStruct((lext, width), BF16), jax.ShapeDtypeStruct((8, width), F32),
                   jax.ShapeDtypeStruct((8, width), F32)],
        compiler_params=_cp(("parallel", "arbitrary")),
    )(proj_ssd, proj_ssd, proj_ssd, dpost, dpost, dpost, cw, cb)


def _tri(lower):
    r, c = _iota((Q, Q), 0), _iota((Q, Q), 1)
    return jnp.where((c <= r) if lower else (c >= r), 1.0, 0.0).astype(BF16)


def _is_bdir_lane(shape):
    ln = _iota(shape, len(shape) - 1)
    return jnp.logical_and(((ln >> 2) & 1) == 1, ln < 64)


def _dt_fwd(proj_ssd, dtb, av):
    lext = proj_ssd.shape[0]

    def kern(p_ref, b_ref, a_ref, dtg_ref, lag_ref, dtt_ref, lat_ref):
        lane = _iota((T, 128), 1)
        raw = p_ref[...] + b_ref[...]
        dt = jnp.where(lane < 64, jnp.maximum(raw, 0.0) + jnp.log1p(jnp.exp(-jnp.abs(raw))), 0.0)
        dta = dt * a_ref[...]
        tl, tu = _tri(True), _tri(False)
        isb = _is_bdir_lane((Q, 128))
        las = []
        for qq in range(T // Q):
            blk = dta[qq * Q:(qq + 1) * Q]
            las.append(jnp.where(isb, _dot3_l(tu, blk), _dot3_l(tl, blk)))
        la = jnp.concatenate(las, axis=0)
        for g in range(NG):
            sh = (128 - 8 * g) % 128
            dtg_ref[g] = jnp.where(lane < 8, pltpu.roll(dt, sh, axis=1) if sh else dt, 0.0)
            lag_ref[g] = jnp.where(lane < 8, pltpu.roll(la, sh, axis=1) if sh else la, 0.0)
        dtt_ref[...] = dt.T[0:64]
        lat_ref[...] = la.T[0:64]

    return pl.pallas_call(
        kern, name="dt_fwd", grid=(lext // T,),
        in_specs=[pl.BlockSpec((T, 128), lambda i: (i, XBC // 128)), pl.BlockSpec((1, 128), lambda i: (0, 0)),
                  pl.BlockSpec((1, 128), lambda i: (0, 0))],
        out_specs=[pl.BlockSpec((NG, T, 128), lambda i: (0, i, 0)), pl.BlockSpec((NG, T, 128), lambda i: (0, i, 0)),
                   pl.BlockSpec((64, T), lambda i: (0, i)), pl.BlockSpec((64, T), lambda i: (0, i))],
        out_shape=[jax.ShapeDtypeStruct((NG, lext, 128), F32), jax.ShapeDtypeStruct((NG, lext, 128), F32),
                   jax.ShapeDtypeStruct((64, lext), F32), jax.ShapeDtypeStruct((64, lext), F32)],
        compiler_params=_cp(("parallel",)),
    )(proj_ssd, dtb, av)


def _dt_bwd(ddtg, proj_ssd, dtb):
    lext = proj_ssd.shape[0]

    def kern(d_ref, p_ref, b_ref, o_ref, gb_ref):
        @pl.when(pl.program_id(0) == 0)
        def _():
            gb_ref[...] = jnp.zeros_like(gb_ref)

        acc = d_ref[0]
        for g in range(1, NG):
            acc = acc + pltpu.roll(d_ref[g], 8 * g, axis=1)
        draw = acc * _sig(p_ref[...] + b_ref[...])
        o_ref[...] = draw.astype(BF16)
        gb_ref[...] += jnp.broadcast_to(jnp.sum(draw, axis=0, keepdims=True), (8, 128))

    return pl.pallas_call(
        kern, name="dt_bwd", grid=(lext // T,),
        in_specs=[pl.BlockSpec((NG, T, 128), lambda i: (0, i, 0)), pl.BlockSpec((T, 128), lambda i: (i, XBC // 128)),
                  pl.BlockSpec((1, 128), lambda i: (0, 0))],
        out_specs=[pl.BlockSpec((T, 128), lambda i: (i, 0)), pl.BlockSpec((8, 128), lambda i: (0, 0))],
        out_shape=[jax.ShapeDtypeStruct((lext, 128), BF16), jax.ShapeDtypeStruct((8, 128), F32)],
        compiler_params=_cp(("arbitrary",)),
    )(ddtg, proj_ssd, dtb)


def _expand_sel(d):
    r, c = _iota((128, 256), 0), _iota((128, 256), 1)
    return jnp.where(r == 4 * d + (c >> 6), 1.0, 0.0).astype(BF16)


def _reduce_sel(d):
    r, c = _iota((256, 128), 0), _iota((256, 128), 1)
    return jnp.where(c == 4 * d + (r >> 6), 1.0, 0.0).astype(BF16)


def _chunk_of_bwd_dir(j, ncc, nc):
    return jnp.where(j < ncc, ncc - 1 - j, nc + ncc - 1 - j)


def _dir_terms(la, dt, d):
    lane = _iota(la.shape, 1)
    mine = jnp.logical_and(lane >= 4 * d, lane < 4 * d + 4)
    la = jnp.where(mine, la, 0.0)
    tot = la[Q - 1:Q] if d == 0 else la[0:1]
    wnd = jnp.exp(tot - la)
    return tot, wnd * jnp.where(mine, dt, 0.0), wnd


def _ssd_state(xbc, dtg, lag, ncc):
    lext = xbc.shape[0]
    nc = lext // Q

    def kern(xf_ref, bf_ref, dtf_ref, laf_ref, xb_ref, bb_ref, dtb_ref, lab_ref, hf_ref, hb_ref, sf, sb):
        @pl.when(pl.program_id(0) == 0)
        def _():
            sf[...] = jnp.zeros_like(sf)
            sb[...] = jnp.zeros_like(sb)

        for d, (x_ref, b_ref, dt_ref, la_ref, h_ref, s) in enumerate(
                ((xf_ref, bf_ref, dtf_ref, laf_ref, hf_ref, sf), (xb_ref, bb_ref, dtb_ref, lab_ref, hb_ref, sb))):
            h_ref[...] = s[...]
            ex = _expand_sel(d)
            for g in range(NG):
                cols = slice(256 * g, 256 * (g + 1))
                tot, w_end, _ = _dir_terms(la_ref[g], dt_ref[g], d)
                wexp = _dot2_r(w_end, ex)
                dexp = _dot2_r(jnp.broadcast_to(jnp.exp(tot), (8, 128)), ex)[0:1]
                xw = (x_ref[:, cols] * wexp).astype(BF16)
                s[:, cols] = s[:, cols] * dexp + _dot(b_ref[:, 128 * g:128 * (g + 1)].T.astype(BF16), xw)

    cb = functools.partial(_chunk_of_bwd_dir, ncc=ncc, nc=nc)
    sm = lambda f: pl.BlockSpec((NG, Q, 128), lambda j: (0, f(j), 0))
    one = lambda j: j
    return pl.pallas_call(
        kern, name="ssd_state", grid=(nc,),
        in_specs=[pl.BlockSpec((Q, DI), lambda j: (j, 0)), pl.BlockSpec((Q, NG * NS), lambda j: (j, 2)), sm(one), sm(one),
                  pl.BlockSpec((Q, DI), lambda j: (cb(j), 0)), pl.BlockSpec((Q, NG * NS), lambda j: (cb(j), 2)), sm(cb), sm(cb)],
        out_specs=[pl.BlockSpec((None, 128, DI), lambda j: (j, 0, 0)),
                   pl.BlockSpec((None, 128, DI), lambda j: (cb(j), 0, 0))],
        out_shape=[jax.ShapeDtypeStruct((nc, 128, DI), F32), jax.ShapeDtypeStruct((nc, 128, DI), F32)],
        scratch_shapes=[pltpu.VMEM((128, DI), F32), pltpu.VMEM((128, DI), F32)],
        compiler_params=_cp(("arbitrary",)),
    )(xbc, xbc, dtg, lag, xbc, xbc, dtg, lag)


def _ssd_out(xbc, dtg, lag, dtt, lat, htf, htb, ncc):
    lext = xbc.shape[0]
    nc = lext // Q
    ncx = nc - ncc

    gps = 4
    li, si = (lambda: _iota((Q, Q), 0)), (lambda: _iota((Q, Q), 1))

    def kern(x_ref, b_ref, c_ref, dtg_ref, lag_ref, dtt_ref, lat_ref, hf_ref, hb_ref, y_ref):
        lane = _iota((Q, 256), 1)
        masks = (li() >= si(), li() <= si())
        for gg in range(gps):
            cols = slice(256 * gg, 256 * (gg + 1))
            cm = c_ref[:, 128 * gg:128 * (gg + 1)]
            xb_ = x_ref[:, cols].astype(BF16)
            s_ = _dot_nt(cm.astype(BF16), b_ref[:, 128 * gg:128 * (gg + 1)].astype(BF16))
            la, dtt_, lat_ = lag_ref[gg], dtt_ref[8 * gg:8 * (gg + 1)], lat_ref[8 * gg:8 * (gg + 1)]
            elam = jnp.exp(la)
            y = jnp.zeros((Q, 256), F32)
            for d, h_ref in enumerate((hf_ref, hb_ref)):
                rhs = jnp.concatenate([xb_, h_ref[:, cols].astype(BF16)], axis=0)
                lhs = []
                for r in range(HPG):
                    j = 4 * d + r
                    lm = jnp.where(masks[d], jnp.exp(la[:, j:j + 1] - lat_[j:j + 1, :]), 0.0)
                    w = s_ * lm * dtt_[j:j + 1, :]
                    lhs.append(jnp.concatenate([w, cm * elam[:, j:j + 1]], axis=1).astype(BF16))
                yall = _dot(jnp.concatenate(lhs, axis=0), rhs)
                for r in range(HPG):
                    y = y + jnp.where((lane >> 6) == r, yall[Q * r:Q * (r + 1)], 0.0)
            y_ref[:, cols] = y

    nb = NG // gps
    sm = pl.BlockSpec((gps, Q, 128), lambda c, g: (g, c + ncc, 0))
    smt = pl.BlockSpec((8 * gps, Q), lambda c, g: (g, c + ncc))
    st3 = pl.BlockSpec((None, 128, 256 * gps), lambda c, g: (c + ncc, 0, g))
    return pl.pallas_call(
        kern, name="ssd_out", grid=(ncx, nb),
        in_specs=[pl.BlockSpec((Q, 256 * gps), lambda c, g: (c + ncc, g)),
                  pl.BlockSpec((Q, 128 * gps), lambda c, g: (c + ncc, 2 * nb + g)),
                  pl.BlockSpec((Q, 128 * gps), lambda c, g: (c + ncc, 3 * nb + g)), sm, sm, smt, smt, st3, st3],
        out_specs=pl.BlockSpec((Q, 256 * gps), lambda c, g: (c, g)),
        out_shape=jax.ShapeDtypeStruct((ncx * Q, DI), F32),
        compiler_params=_cp(("parallel", "parallel")),
    )(xbc, xbc, xbc, dtg, lag, dtt, lat, htf, htb)


def _ssd_bwd_state(xbc, dy, lag, ncc):
    lext = xbc.shape[0]
    nc = lext // Q

    def kern(cf_ref, dyf_ref, laf_ref, cb_ref, dyb_ref, lab_ref, df_ref, db_ref, sf, sb):
        @pl.when(pl.program_id(0) == 0)
        def _():
            sf[...] = jnp.zeros_like(sf)
            sb[...] = jnp.zeros_like(sb)

        for d, (c_ref, dy_ref, la_ref, o_ref, s) in enumerate(
                ((cf_ref, dyf_ref, laf_ref, df_ref, sf), (cb_ref, dyb_ref, lab_ref, db_ref, sb))):
            o_ref[...] = s[...]
            ex = _expand_sel(d)
            for g in range(NG):
                cols = slice(256 * g, 256 * (g + 1))
                la = la_ref[g]
                tot = la[Q - 1:Q] if d == 0 else la[0:1]
                eexp = _dot2_r(jnp.exp(la), ex)
                dexp = _dot2_r(jnp.broadcast_to(jnp.exp(tot), (8, 128)), ex)[0:1]
                dye = (dy_ref[:, cols] * eexp).astype(BF16)
                s[:, cols] = s[:, cols] * dexp + _dot(c_ref[:, 128 * g:128 * (g + 1)].T.astype(BF16), dye)

    cf = lambda j: nc - 1 - j
    cb = lambda j: _chunk_of_bwd_dir(nc - 1 - j, ncc, nc)
    sm = lambda f: pl.BlockSpec((NG, Q, 128), lambda j: (0, f(j), 0))
    return pl.pallas_call(
        kern, name="ssd_bwd_state", grid=(nc,),
        in_specs=[pl.BlockSpec((Q, NG * NS), lambda j: (cf(j), 3)), pl.BlockSpec((Q, DI), lambda j: (cf(j), 0)), sm(cf),
                  pl.BlockSpec((Q, NG * NS), lambda j: (cb(j), 3)), pl.BlockSpec((Q, DI), lambda j: (cb(j), 0)), sm(cb)],
        out_specs=[pl.BlockSpec((None, 128, DI), lambda j: (cf(j), 0, 0)),
                   pl.BlockSpec((None, 128, DI), lambda j: (cb(j), 0, 0))],
        out_shape=[jax.ShapeDtypeStruct((nc, 128, DI), F32), jax.ShapeDtypeStruct((nc, 128, DI), F32)],
        scratch_shapes=[pltpu.VMEM((128, DI), F32), pltpu.VMEM((128, DI), F32)],
        compiler_params=_cp(("arbitrary",)),
    )(xbc, dy, lag, xbc, dy, lag)


def _ssd_bwd_out(xbc, dy, dsk, dtg, lag, dtt, lat, htf, htb, dhf, dhb, a_rows):
    lext = xbc.shape[0]
    nc = lext // Q

    gps = 1

    def kern(x_ref, b_ref, c_ref, dy_ref, sk_ref, dtg_ref, lag_ref, dtt_ref, lat_ref, hf_ref, hb_ref, df_ref, db_ref,
             a_ref, dx_ref, dbo_ref, dco_ref, ddt_ref, ga_ref):
        @pl.when(pl.program_id(1) == 0)
        def _():
            ga_ref[...] = jnp.zeros_like(ga_ref)

        for gg in range(gps):
            one_group(gg, x_ref, b_ref, c_ref, dy_ref, sk_ref, dtg_ref, lag_ref, dtt_ref, lat_ref, hf_ref, hb_ref, df_ref,
                      db_ref, a_ref, dx_ref, dbo_ref, dco_ref, ddt_ref, ga_ref)

    def one_group(gg, x_ref, b_ref, c_ref, dy_ref, sk_ref, dtg_ref, lag_ref, dtt_ref, lat_ref, hf_ref, hb_ref, df_ref,
                  db_ref, a_ref, dx_ref, dbo_ref, dco_ref, ddt_ref, ga_ref):
        g = pl.program_id(0) * gps + gg
        cols, cols128 = slice(256 * gg, 256 * (gg + 1)), slice(128 * gg, 128 * (gg + 1))
        x, bm, cm, dy_ = x_ref[:, cols], b_ref[:, cols128], c_ref[:, cols128], dy_ref[:, cols]
        xb_, bb_, cb_, dyb_ = x.astype(BF16), bm.astype(BF16), cm.astype(BF16), dy_.astype(BF16)
        st = _dot_nt(bb_, cb_)
        si, li = _iota((Q, Q), 0), _iota((Q, Q), 1)
        lane = _iota((Q, 256), 1)
        lane128 = _iota((Q, 128), 1)
        row128 = _iota((Q, 128), 0)
        sub = _iota((128, Q), 0)
        la, dt = lag_ref[gg], dtg_ref[gg]
        dtt_, lat_ = dtt_ref[8 * gg:8 * (gg + 1)], lat_ref[8 * gg:8 * (gg + 1)]
        elam = jnp.exp(la)
        dst = jnp.zeros((Q, Q), F32)
        dxa = jnp.zeros((Q, 256), F32)
        dba = jnp.zeros((Q, 128), F32)
        dca = jnp.zeros((Q, 128), F32)
        dlam = jnp.zeros((Q, 128), F32)
        ddir = jnp.zeros((Q, 128), F32)
        rows = jnp.zeros((128, Q), F32)
        hms = [(lane >> 6) == r for r in range(HPG)]
        for d, (h_ref, dh_ref) in enumerate(((hf_ref, df_ref), (hb_ref, db_ref))):
            ht, dht = h_ref[:, cols], dh_ref[:, cols]
            htb_, dhtb_ = ht.astype(BF16), dht.astype(BF16)
            tot, w_end, wnd = _dir_terms(la, dt, d)
            ex, rs = _expand_sel(d), _reduce_sel(d)
            elx = _dot2_r(elam, ex)
            wex = _dot2_r(w_end, ex)
            dye = dy_ * elx
            ch = _dot(cb_, htb_)
            bd = _dot(bb_, dhtb_)
            dca = dca + _dot_nt(dye.astype(BF16), htb_)
            dba = dba + _dot_nt((x * wex).astype(BF16), dhtb_)
            dlam = dlam + _dot2_r(dye * ch, rs)
            xbd = _dot2_r(x * bd, rs)
            e_ = w_end * xbd
            dlam = dlam - e_
            ddir = ddir + wnd * xbd
            hh = _dot2_r(jnp.broadcast_to(jnp.sum(dht * ht, axis=0, keepdims=True), (8, 256)), rs)[0:1]
            tot_term = jnp.sum(e_, axis=0, keepdims=True) + jnp.exp(tot) * hh
            dlam = dlam + jnp.where(row128 == (Q - 1 if d == 0 else 0), tot_term, 0.0)
            rhs = jnp.concatenate([dyb_, dhtb_], axis=0)
            maskt = (li >= si) if d == 0 else (li <= si)
            for r in range(HPG):
                j = 4 * d + r
                dc = dt[:, j:j + 1]
                lmt = jnp.where(maskt, jnp.exp(lat_[j:j + 1, :] - la[:, j:j + 1]), 0.0)
                ldc = lmt * jnp.broadcast_to(dc, (Q, Q))
                lhs = jnp.concatenate([st * ldc, bm * w_end[:, j:j + 1]], axis=1).astype(BF16)
                dxa = dxa + jnp.where(hms[r], _dot(lhs, rhs), 0.0)
                dwt = _dot_nt(jnp.where(hms[r], x, 0.0).astype(BF16), dyb_)
                q = dwt * st
                cs = jnp.sum(q * lmt, axis=1, keepdims=True)
                ddir = ddir + jnp.where(lane128 == j, cs, 0.0)
                dlam = dlam - jnp.where(lane128 == j, cs * dc, 0.0)
                rows = rows + jnp.where(sub == j, jnp.sum(q * ldc, axis=0, keepdims=True), 0.0)
                dst = dst + dwt * ldc
        dlam = dlam + rows.T
        dba = dba + _dot(dst.astype(BF16), cb_)
        dca = dca + _dot(dst.T.astype(BF16), bb_)
        isb = jnp.logical_and(lane128 >= 4, lane128 < 8)
        ddel = jnp.where(isb, _dot2_l(_tri(True), dlam), _dot2_l(_tri(False), dlam))
        a_l = a_ref[pl.ds(g, 1), :]
        ddt_ref[gg] = ddir + a_l * ddel
        ga_ref[gg] += jnp.broadcast_to(a_l * jnp.sum(dt * ddel, axis=0, keepdims=True), (8, 128))
        dx_ref[:, cols] = dxa + dy_ * sk_ref[:, cols]
        dbo_ref[:, cols128] = dba
        dco_ref[:, cols128] = dca

    nb = NG // gps
    st3 = pl.BlockSpec((None, 128, 256 * gps), lambda g, c: (c, 0, g))
    sm = pl.BlockSpec((gps, Q, 128), lambda g, c: (g, c, 0))
    smt = pl.BlockSpec((8 * gps, Q), lambda g, c: (g, c))
    wide = pl.BlockSpec((Q, 256 * gps), lambda g, c: (c, g))
    return pl.pallas_call(
        kern, name="ssd_bwd_out", grid=(nb, nc),
        in_specs=[wide, pl.BlockSpec((Q, 128 * gps), lambda g, c: (c, 2 * nb + g)),
                  pl.BlockSpec((Q, 128 * gps), lambda g, c: (c, 3 * nb + g)), wide,
                  pl.BlockSpec((1, 256 * gps), lambda g, c: (0, g)), sm, sm, smt, smt, st3, st3, st3, st3,
                  pl.BlockSpec((8, 128), lambda g, c: (0, 0))],
        out_specs=[wide, pl.BlockSpec((Q, 128 * gps), lambda g, c: (c, g)),
                   pl.BlockSpec((Q, 128 * gps), lambda g, c: (c, g)), sm, pl.BlockSpec((gps, 8, 128), lambda g, c: (g, 0, 0))],
        out_shape=[jax.ShapeDtypeStruct((lext, DI), F32), jax.ShapeDtypeStruct((lext, NG * NS), F32),
                   jax.ShapeDtypeStruct((lext, NG * NS), F32), jax.ShapeDtypeStruct((NG, lext, 128), F32),
                   jax.ShapeDtypeStruct((NG, 8, 128), F32)],
        compiler_params=_cp(("parallel", "arbitrary")),
    )(xbc, xbc, xbc, dy, dsk, dtg, lag, dtt, lat, htf, htb, dhf, dhb, a_rows)


def _post_fwd(yssm, xbc, proj_rest, dsk, gnw, nct):
    l = yssm.shape[0]

    def kern(y_ref, x_ref, z_ref, dsk_ref, w_ref, o_ref):
        y = y_ref[...] + dsk_ref[...] * x_ref[...]
        yz = y * _silu(z_ref[...].astype(F32))
        for g in range(NG):
            sl = slice(256 * g, 256 * (g + 1))
            v = yz[:, sl]
            r = lax.rsqrt(jnp.mean(v * v, axis=-1, keepdims=True) + EPS)
            o_ref[:, sl] = (v * r * w_ref[:, sl]).astype(BF16)

    return pl.pallas_call(
        kern, name="post_fwd", grid=(l // T,),
        in_specs=[pl.BlockSpec((T, DI), lambda i: (i, 0)), pl.BlockSpec((T, DI), lambda i: (i + nct, 0)),
                  pl.BlockSpec((T, DI), lambda i: (i, 0)), pl.BlockSpec((1, DI), lambda i: (0, 0)),
                  pl.BlockSpec((1, DI), lambda i: (0, 0))],
        out_specs=pl.BlockSpec((T, DI), lambda i: (i, 0)),
        out_shape=jax.ShapeDtypeStruct((l, DI), BF16),
        compiler_params=_cp(("parallel",)),
    )(yssm, xbc, proj_rest, dsk, gnw)


def _post_bwd(dgn, yssm, xbc, proj_rest, dsk, gnw, dpr, nct):
    l = yssm.shape[0]
    lext = xbc.shape[0]
    xi = lambda i: (jnp.maximum(i - nct, 0), 0)

    def kern(dg_ref, y_ref, x_ref, z_ref, dsk_ref, w_ref, dpr_ref, dy_ref, dz_ref, gw_ref, gd_ref):
        i = pl.program_id(0)

        @pl.when(i == 0)
        def _():
            gw_ref[...] = jnp.zeros_like(gw_ref)
            gd_ref[...] = jnp.zeros_like(gd_ref)

        @pl.when(i < nct)
        def _():
            dy_ref[...] = jnp.zeros_like(dy_ref)

        @pl.when(i >= nct)
        def _():
            xs = x_ref[...]
            z = z_ref[...].astype(F32)
            y = y_ref[...] + dsk_ref[...] * xs
            sz = _silu(z)
            yz = y * sz
            dgn_ = dg_ref[...].astype(F32)
            dyz_parts = []
            gws = []
            for g in range(NG):
                sl = slice(256 * g, 256 * (g + 1))
                v = yz[:, sl]
                r = lax.rsqrt(jnp.mean(v * v, axis=-1, keepdims=True) + EPS)
                vn = v * r
                dn = dgn_[:, sl] * w_ref[:, sl]
                gws.append(jnp.sum(dgn_[:, sl] * vn, axis=0, keepdims=True))
                dyz_parts.append(r * (dn - vn * jnp.mean(dn * vn, axis=-1, keepdims=True)))
            dyz = jnp.concatenate(dyz_parts, axis=1)
            gw_ref[...] += jnp.broadcast_to(jnp.concatenate(gws, axis=1), (8, DI))
            dy = dyz * sz
            dz_ref[...] = (dyz * y * _dsilu(z)).astype(BF16)
            gd_ref[...] += jnp.broadcast_to(jnp.sum(dy * xs, axis=0, keepdims=True), (8, DI))
            dy_ref[...] = dy

    return pl.pallas_call(
        kern, name="post_bwd", grid=(lext // T,),
        in_specs=[pl.BlockSpec((T, DI), xi), pl.BlockSpec((T, DI), xi), pl.BlockSpec((T, DI), lambda i: (i, 0)),
                  pl.BlockSpec((T, DI), xi), pl.BlockSpec((1, DI), lambda i: (0, 0)), pl.BlockSpec((1, DI), lambda i: (0, 0)),
                  pl.BlockSpec(memory_space=pl.ANY)],
        out_specs=[pl.BlockSpec((T, DI), lambda i: (i, 0)),
                   pl.BlockSpec((T, DI), xi), pl.BlockSpec((8, DI), lambda i: (0, 0)), pl.BlockSpec((8, DI), lambda i: (0, 0))],
        out_shape=[jax.ShapeDtypeStruct((lext, DI), F32),
                   jax.ShapeDtypeStruct((l, RESTW), BF16), jax.ShapeDtypeStruct((8, DI), F32), jax.ShapeDtypeStruct((8, DI), F32)],
        input_output_aliases={6: 1},
        compiler_params=_cp(("arbitrary",)),
    )(dgn, yssm, xbc, proj_rest, dsk, gnw, dpr)


C_G1, C_G2, C_GA, C_GB, C_CG = 2, 3, 4, 5, 6
PITCH = GW + 16
NROW = T // GW


GAP = PITCH - GW
PADR = GAP + NROW * PITCH
NSTRIP = D // 128


def _fill_padded(pad8, val):
    z = jnp.zeros((GAP, D), F32)
    parts = [z]
    for r in range(NROW):
        parts += [val[GW * r:GW * (r + 1)], z]
    p = jnp.concatenate(parts, axis=0)
    pad8[0] = p
    for j in range(1, pad8.shape[0]):
        pad8[j] = pltpu.roll(p, PADR - j, axis=0)


def _tap(pad8, base, off, ln):
    return pad8[off % 8, pl.ds(base + off - off % 8, GW), ln]


def _row_conv(out_ref, pad8, w_ref, transpose):
    def strip(s, carry):
        ln = pl.ds(pl.multiple_of(s * 128, 128), 128)
        for r in range(NROW):
            base = GAP + PITCH * r
            acc = jnp.zeros((GW, 128), F32)
            for k in range(CK):
                off = (k - 15) if not transpose else (15 - k)
                acc = acc + w_ref[pl.ds(k, 1), ln] * _tap(pad8, base, off, ln)
            out_ref[pl.ds(GW * r, GW), ln] = acc
        return carry

    lax.fori_loop(0, NSTRIP, strip, 0)


def _row_conv_wgrad(gcw_ref, padd8, pada8):
    def strip(s, carry):
        ln = pl.ds(pl.multiple_of(s * 128, 128), 128)
        rid = _iota((32, 128), 0)
        g = jnp.zeros((32, 128), F32)
        for k0 in range(0, CK, 8):
            taps = range(k0, min(k0 + 8, CK))
            accs = {k: jnp.zeros((8, 128), F32) for k in taps}
            for r in range(NROW):
                base = GAP + PITCH * r
                d = _tap(padd8, base, 0, ln)
                for k in taps:
                    p = d * pada8[0, pl.ds(base + k - 15, GW), ln]
                    part = p[0:8]
                    for q in range(1, GW // 8):
                        part = part + p[8 * q:8 * (q + 1)]
                    accs[k] = accs[k] + part
            for k in taps:
                g = jnp.where(rid == k, jnp.sum(accs[k], axis=0, keepdims=True), g)
        gcw_ref[:, ln] += g
        return carry

    lax.fori_loop(0, NSTRIP, strip, 0)


def _ln_stats(cv):
    mu = jnp.mean(cv, axis=-1, keepdims=True)
    xc = cv - mu
    rs = lax.rsqrt(jnp.mean(xc * xc, axis=-1, keepdims=True) + EPS)
    return xc * rs, rs


def _conf_fwd(proj_rest, cw, cb, lw, lb):
    l = proj_rest.shape[0]

    def kern(ga_ref, gb_ref, cg_ref, cw_ref, cb_ref, lw_ref, lb_ref, o_ref, cv_ref, pad8):
        _fill_padded(pad8, ga_ref[...].astype(F32) * _sig(gb_ref[...].astype(F32)))
        _row_conv(cv_ref, pad8, cw_ref, False)
        cv = cv_ref[...] + cb_ref[...]
        cv_ref[...] = cv
        xh, _ = _ln_stats(cv)
        ln = xh * lw_ref[...] + lb_ref[...]
        o_ref[...] = (_silu(ln) * _silu(cg_ref[...].astype(F32))).astype(BF16)

    vec = pl.BlockSpec((1, D), lambda i: (0, 0))
    blk = pl.BlockSpec((T, D), lambda i: (i, 0))
    return pl.pallas_call(
        kern, name="conf_fwd", grid=(l // T,),
        in_specs=[pl.BlockSpec((T, D), lambda i: (i, C_GA)), pl.BlockSpec((T, D), lambda i: (i, C_GB)),
                  pl.BlockSpec((T, D), lambda i: (i, C_CG)), pl.BlockSpec((32, D), lambda i: (0, 0)), vec, vec, vec],
        out_specs=[blk, blk],
        out_shape=[jax.ShapeDtypeStruct((l, D), BF16), jax.ShapeDtypeStruct((l, D), F32)],
        scratch_shapes=[pltpu.VMEM((8, PADR, D), F32)],
        compiler_params=_cp(("parallel",)),
    )(proj_rest, proj_rest, proj_rest, cw, cb, lw, lb)


def _conf_bwd(duc, cv, proj_rest, cw, lw, lb, dpr):
    l = proj_rest.shape[0]

    def kern(du_ref, cv_ref, ga_ref, gb_ref, cg_ref, cw_ref, lw_ref, lb_ref, dpr_ref, o_ref, gcw_ref, gv_ref, sc,
             pada, padd, da_ref):
        i, j = pl.program_id(0), pl.program_id(1)

        @pl.when(jnp.logical_and(i == 0, j == 0))
        def _():
            gcw_ref[...] = jnp.zeros_like(gcw_ref)
            gv_ref[...] = jnp.zeros_like(gv_ref)

        @pl.when(j == 0)
        def _():
            ga, gb, cg = ga_ref[...].astype(F32), gb_ref[...].astype(F32), cg_ref[...].astype(F32)
            sg = _sig(gb)
            xh, rs = _ln_stats(cv_ref[...])
            ln = xh * lw_ref[...] + lb_ref[...]
            du = du_ref[...].astype(F32)
            sc[:, 2 * D:3 * D] = (du * _silu(ln) * _dsilu(cg)).astype(BF16)
            dln = du * _silu(cg) * _dsilu(ln)
            g_lw = jnp.sum(dln * xh, axis=0, keepdims=True)
            g_lb = jnp.sum(dln, axis=0, keepdims=True)
            dxh = dln * lw_ref[...]
            dcv = rs * (dxh - jnp.mean(dxh, axis=-1, keepdims=True) - xh * jnp.mean(dxh * xh, axis=-1, keepdims=True))
            g_cb = jnp.sum(dcv, axis=0, keepdims=True)
            rid = _iota((8, D), 0)
            gv_ref[...] += jnp.where(rid == 0, g_cb, jnp.where(rid == 1, g_lw, jnp.where(rid == 2, g_lb, 0.0)))
            _fill_padded(padd, dcv)
            _fill_padded(pada, ga * sg)
            _row_conv(da_ref, padd, cw_ref, True)
            _row_conv_wgrad(gcw_ref, padd, pada)
            da = da_ref[...]
            sc[:, 0:D] = (da * sg).astype(BF16)
            sc[:, D:2 * D] = (da * ga * sg * (1.0 - sg)).astype(BF16)

        o_ref[...] = sc[:, pl.ds(pl.multiple_of(j * D, 128), D)]

    vec = pl.BlockSpec((1, D), lambda i, j: (0, 0))
    col = lambda c: pl.BlockSpec((T, D), lambda i, j: (i, c))
    return pl.pallas_call(
        kern, name="conf_bwd", grid=(l // T, 3),
        in_specs=[col(0), col(0), col(C_GA), col(C_GB), col(C_CG), pl.BlockSpec((32, D), lambda i, j: (0, 0)), vec, vec,
                  pl.BlockSpec(memory_space=pl.ANY)],
        out_specs=[pl.BlockSpec((T, D), lambda i, j: (i, C_GA + j)), pl.BlockSpec((32, D), lambda i, j: (0, 0)),
                   pl.BlockSpec((8, D), lambda i, j: (0, 0))],
        out_shape=[jax.ShapeDtypeStruct((l, RESTW), BF16), jax.ShapeDtypeStruct((32, D), F32),
                   jax.ShapeDtypeStruct((8, D), F32)],
        scratch_shapes=[pltpu.VMEM((T, 3 * D), BF16), pltpu.VMEM((1, PADR, D), F32), pltpu.VMEM((8, PADR, D), F32),
                        pltpu.VMEM((T, D), F32)],
        input_output_aliases={8: 0},
        compiler_params=_cp(("arbitrary", "arbitrary")),
    )(duc, cv, proj_rest, proj_rest, proj_rest, cw, lw, lb, dpr)


def _merge_fwd(bs, bc, proj_rest):
    l = bs.shape[0]

    def kern(bs_ref, bc_ref, g1_ref, g2_ref, o_ref):
        up = lambda r: r[...].astype(F32)
        o_ref[...] = (_sig(up(g1_ref)) * up(bs_ref) + _sig(up(g2_ref)) * up(bc_ref)).astype(BF16)

    blk = pl.BlockSpec((T, D), lambda i: (i, 0))
    return pl.pallas_call(
        kern, name="merge_fwd", grid=(l // T,),
        in_specs=[blk, blk, pl.BlockSpec((T, D), lambda i: (i, C_G1)), pl.BlockSpec((T, D), lambda i: (i, C_G2))],
        out_specs=blk, out_shape=jax.ShapeDtypeStruct((l, D), BF16),
        compiler_params=_cp(("parallel",)),
    )(bs, bc, proj_rest, proj_rest)


def _merge_bwd(dm, bs, bc, proj_rest):
    l = bs.shape[0]

    def kern(dm_ref, bs_ref, bc_ref, g1_ref, g2_ref, dbs_ref, dbc_ref, dg_ref):
        up = lambda r: r[...].astype(F32)
        dm_ = up(dm_ref)
        s1, s2 = _sig(up(g1_ref)), _sig(up(g2_ref))
        dbs_ref[...] = (dm_ * s1).astype(BF16)
        dbc_ref[...] = (dm_ * s2).astype(BF16)
        dg_ref[:, 0:D] = (dm_ * up(bs_ref) * s1 * (1.0 - s1)).astype(BF16)
        dg_ref[:, D:2 * D] = (dm_ * up(bc_ref) * s2 * (1.0 - s2)).astype(BF16)

    blk = pl.BlockSpec((T, D), lambda i: (i, 0))
    return pl.pallas_call(
        kern, name="merge_bwd", grid=(l // T,),
        in_specs=[blk, blk, blk, pl.BlockSpec((T, D), lambda i: (i, C_G1)), pl.BlockSpec((T, D), lambda i: (i, C_G2))],
        out_specs=[blk, blk, pl.BlockSpec((T, 2 * D), lambda i: (i, 1))],
        out_shape=[jax.ShapeDtypeStruct((l, D), BF16), jax.ShapeDtypeStruct((l, D), BF16),
                   jax.ShapeDtypeStruct((l, RESTW), BF16)],
        compiler_params=_cp(("parallel",)),
    )(dm, bs, bc, proj_rest, proj_rest)


def _final(x, out, tgt, mod, fw):
    l = x.shape[0]

    def kern(x_ref, o_ref, t_ref, mod_ref, fw_ref, ls_ref, dx2_ref, do_ref, gv_ref):
        @pl.when(pl.program_id(0) == 0)
        def _():
            ls_ref[...] = jnp.zeros_like(ls_ref)
            gv_ref[...] = jnp.zeros_like(gv_ref)

        gate = mod_ref[0:1, 2 * D:3 * D]
        o = o_ref[...]
        x2 = x_ref[...] + gate * o
        r = lax.rsqrt(jnp.mean(x2 * x2, axis=-1, keepdims=True) + EPS)
        yn = x2 * r
        fw_ = fw_ref[...]
        e = yn * fw_ - t_ref[...]
        ls_ref[...] += jnp.full((8, 128), 1.0, F32) * (0.5 / D) * jnp.sum(e * e)
        dy = e * (1.0 / D)
        g_fw = jnp.sum(dy * yn, axis=0, keepdims=True)
        dyn = dy * fw_
        dx2 = r * (dyn - yn * jnp.mean(dyn * yn, axis=-1, keepdims=True))
        g_gate = jnp.sum(dx2 * o, axis=0, keepdims=True)
        rid = _iota((8, D), 0)
        gv_ref[...] += jnp.where(rid == 0, g_fw, jnp.where(rid == 1, g_gate, 0.0))
        dx2_ref[...] = dx2
        do_ref[...] = (dx2 * gate).astype(BF16)

    blk = pl.BlockSpec((T, D), lambda i: (i, 0))
    return pl.pallas_call(
        kern, name="final", grid=(l // T,),
        in_specs=[blk, blk, blk, pl.BlockSpec((8, 3 * D), lambda i: (0, 0)), pl.BlockSpec((1, D), lambda i: (0, 0))],
        out_specs=[pl.BlockSpec((8, 128), lambda i: (0, 0)), blk, blk, pl.BlockSpec((8, D), lambda i: (0, 0))],
        out_shape=[jax.ShapeDtypeStruct((8, 128), F32), jax.ShapeDtypeStruct((l, D), F32),
                   jax.ShapeDtypeStruct((l, D), BF16), jax.ShapeDtypeStruct((8, D), F32)],
        compiler_params=_cp(("arbitrary",)),
    )(x, out, tgt, mod, fw)


def _perm_dt_cols(w):
    s = w.shape[:-1]
    return w.reshape(*s, 2, NG, HPG).swapaxes(-3, -2).reshape(*s, 64)


def _unperm_dt_cols(w):
    s = w.shape[:-1]
    return w.reshape(*s, NG, 2, HPG).swapaxes(-3, -2).reshape(*s, 64)


def _pad_lanes(v, width):
    return jnp.pad(v, ((0, 0), (0, width - v.shape[1])))


def _vcols(segs, a, b):
    parts, off = [], 0
    for s in segs:
        lo, hi = max(a, off), min(b, off + s.shape[1])
        if lo < hi:
            parts.append(s[:, lo - off:hi - off])
        off += s.shape[1]
    return parts[0] if len(parts) == 1 else jnp.concatenate(parts, axis=1)


def _local_step(x, c, ctx, tgt, w):
    l = x.shape[0]
    nct = CTX // T
    ncc = CTX // Q
    lext = l + CTX

    w_mod = w["w_mod"].astype(BF16)
    wsegs = [s.astype(BF16) for s in (w["w_in"] if isinstance(w["w_in"], (list, tuple)) else [w["w_in"]])]
    w_ssd = jnp.concatenate([_vcols(wsegs, 0, XBC), _perm_dt_cols(_vcols(wsegs, XBC, XBC + 64)), jnp.zeros((D, 64), BF16)], axis=1)
    r0 = XBC + 64
    w_rest = jnp.concatenate([_vcols(wsegs, r0, r0 + DI), _vcols(wsegs, r0 + DI + 3 * D, r0 + RESTW),
                              _vcols(wsegs, r0 + DI, r0 + DI + 3 * D)], axis=1)
    w_os, w_oc, w_o = w["w_out_ssm"].astype(BF16), w["w_out_conf"].astype(BF16), w["w_out"].astype(BF16)
    cw8 = jnp.pad(w["ssm_conv_w"], ((0, 4), (0, 0)))
    cb_s = w["ssm_conv_b"].reshape(1, XBC)
    dtb = _pad_lanes(_perm_dt_cols(w["dt_bias"].reshape(1, 64)), 128)
    a_all = -jnp.exp(w["a_log"].reshape(1, 64))
    a_perm = _pad_lanes(_perm_dt_cols(a_all), 128)
    a_rows = _pad_lanes(_perm_dt_cols(a_all).reshape(NG, 8), 128)
    dsk = jnp.repeat(w["d_skip"].reshape(NH), HP).reshape(1, DI)
    gnw = w["ssm_norm_w"].reshape(1, DI)
    ccw = jnp.pad(w["conf_conv_w"], ((0, 1), (0, 0)))
    ccb, clw, clb = w["conf_conv_b"].reshape(1, D), w["conf_ln_w"].reshape(1, D), w["conf_ln_b"].reshape(1, D)
    nw = w["norm_w"].reshape(1, D)
    fw = w["final_norm_w"].reshape(1, D)
    cc = jnp.concatenate([c.reshape(1, D), w["c_ctx"].reshape(1, D), jnp.zeros((6, D), F32)], axis=0)

    bx = 512
    be = 768 if lext % 768 == 0 else 256
    tk = min(1024, l)
    mod = _mod_fwd(cc, w_mod, w["b_mod"].reshape(1, 3 * D))
    h = _norm_fwd(ctx, x, mod, nw, nct)
    hx = h[CTX:]
    proj_ssd = _mm(h, w_ssd, "nn", lext, SSDW, D, be, SSDW // 3, D, F32, "proj_ssd")
    proj_rest = _mm(hx, w_rest, "nn", l, RESTW, D, bx, 1024, D, BF16, "proj_rest")
    xbc = _conv_fwd(proj_ssd, cw8, cb_s, nct)
    dtg, lag, dtt, lat = _dt_fwd(proj_ssd, dtb, a_perm)
    htf, htb = _ssd_state(xbc, dtg, lag, ncc)
    yssm = _ssd_out(xbc, dtg, lag, dtt, lat, htf, htb, ncc)
    gn = _post_fwd(yssm, xbc, proj_rest, dsk, gnw, nct)
    bs = _mm(gn, w_os, "nn", l, D, DI, bx, D, DI, BF16, "out_ssm")
    uc, cv = _conf_fwd(proj_rest, ccw, ccb, clw, clb)
    bc = _mm(uc, w_oc, "nn", l, D, D, bx, D, D, BF16, "out_conf")
    merged = _merge_fwd(bs, bc, proj_rest)
    out = _mm(merged, w_o, "nn", l, D, D, bx, D, D, F32, "out_proj")
    lsum, dx2, dout, gv_fin = _final(x, out, tgt, mod, fw)

    g = {}
    g["final_norm_w"] = gv_fin[0]
    dmerged = _mm(dout, w_o, "nt", l, D, D, bx, D, D, BF16, "d_merged")
    g["w_out"] = _mm(merged, dout, "tn", D, D, l, D, D, tk, F32, "g_w_out")
    dbs, dbc, dpr = _merge_bwd(dmerged, bs, bc, proj_rest)
    dgn = _mm(dbs, w_os, "nt", l, DI, D, bx, DI, D, BF16, "d_gn")
    g["w_out_ssm"] = _mm(gn, dbs, "tn", DI, D, l, DI, D, tk, F32, "g_w_out_ssm")
    duc = _mm(dbc, w_oc, "nt", l, D, D, bx, D, D, BF16, "d_uc")
    g["w_out_conf"] = _mm(uc, dbc, "tn", D, D, l, D, D, tk, F32, "g_w_out_conf")
    dpr, gcw, gv_conf = _conf_bwd(duc, cv, proj_rest, ccw, clw, clb, dpr)
    g["conf_conv_w"] = gcw[:CK]
    g["conf_conv_b"], g["conf_ln_w"], g["conf_ln_b"] = gv_conf[0], gv_conf[1], gv_conf[2]
    dy, dproj_rest, ggnw, gdsk = _post_bwd(dgn, yssm, xbc, proj_rest, dsk, gnw, dpr, nct)
    g["ssm_norm_w"] = ggnw[0]
    g["d_skip"] = gdsk[0].reshape(NH, HP).sum(axis=1)
    dhf, dhb = _ssd_bwd_state(xbc, dy, lag, ncc)
    dxs, dbm, dcm, ddtg, galog = _ssd_bwd_out(xbc, dy, dsk, dtg, lag, dtt, lat, htf, htb, dhf, dhb, a_rows)
    g["a_log"] = _unperm_dt_cols(galog[:, 0, 0:8].reshape(1, 64)).reshape(2, NH)
    dus, gws, gbs = [], [], []
    for dpost, off, width, nm in ((dxs, 0, DI, "conv_bwd_x"), (dbm, DI, NG * NS, "conv_bwd_b"), (dcm, DI + NG * NS, NG * NS, "conv_bwd_c")):
        du_, gw_, gb_ = _conv_bwd(dpost, proj_ssd, cw8, cb_s, off, width, nct, nm)
        dus.append(du_)
        gws.append(gw_[:SK])
        gbs.append(gb_[0])
    g["ssm_conv_w"] = jnp.concatenate(gws, axis=1)
    g["ssm_conv_b"] = jnp.concatenate(gbs, axis=0)
    ddt_raw, gdtb = _dt_bwd(ddtg, proj_ssd, dtb)
    g["dt_bias"] = _unperm_dt_cols(gdtb[0:1, 0:64]).reshape(2, NH)
    dproj_ssd = jnp.concatenate(dus + [ddt_raw], axis=1)
    gw_ssd = _mm(h, dproj_ssd, "tn", D, SSDW, lext, D, SSDW // 3, be, F32, "g_w_ssd")
    gw_rest = _mm(hx, dproj_rest, "tn", D, RESTW, l, D, 1024, tk, F32, "g_w_rest")
    gsegs = [gw_ssd[:, :XBC], _unperm_dt_cols(gw_ssd[:, XBC:XBC + 64]), gw_rest[:, :DI], gw_rest[:, 2 * DI:],
             gw_rest[:, DI:2 * DI]]
    g["w_in"] = jnp.concatenate(gsegs, axis=1)
    g["w_in_shards"] = jnp.stack([_vcols(gsegs, R_IN * s, R_IN * (s + 1)) for s in range(NSHARD)])
    dh_a = _mm(dproj_ssd, w_ssd, "nt", lext, D, SSDW, T, D, SSDW, F32, "dh_ssd")
    dh_b = _mm(dproj_rest, w_rest, "nt", l, D, RESTW, T, D, RESTW, F32, "dh_rest")
    grad_x, gnw_in, dss = _norm_bwd(dh_a, dh_b, ctx, x, dx2, mod, nw, nct)
    g["norm_w"] = gnw_in[0]
    dmod = jnp.concatenate([jnp.concatenate([dss[0:1], gv_fin[1:2]], axis=1),
                            jnp.concatenate([dss[1:2], jnp.zeros((1, D), F32)], axis=1),
                            jnp.zeros((6, 3 * D), F32)], axis=0)
    gwm, gbm, gcc = _mod_bwd(dmod, cc, cc.T, w_mod)
    g["w_mod"], g["b_mod"], g["c_ctx"] = gwm, gbm[0], gcc[1]
    return lsum[0, 0], grad_x, g


NSHARD = 4
R_MOD, R_IN, R_OS, R_OC, R_O, R_SC, R_CC = 768, 2832, 512, 256, 256, 8, 8
O_MOD = 0
O_OS = O_MOD + R_MOD
O_OC = O_OS + R_OS
O_O = O_OC + R_OC
O_SC = O_O + R_O
O_CC = O_SC + R_SC
PUSED = O_CC + R_CC
PROWS = 1824
HALF = PROWS // 2
RB = HALF // 3
WB = 128
SROWS = 16
SHARDED = ("w_mod", "w_in", "w_out_ssm", "w_out_conf", "w_out", "ssm_conv_w", "conf_conv_w")
SMALL = (("b_mod", 3 * D), ("norm_w", D), ("ssm_conv_b", XBC), ("dt_bias", 64), ("a_log", 64), ("d_skip", NH),
         ("ssm_norm_w", DI), ("conf_conv_b", D), ("conf_ln_w", D), ("conf_ln_b", D), ("final_norm_w", D), ("c_ctx", D))
SMALL_OFF = {"b_mod": 0, "norm_w": 3 * D, "ssm_conv_b": 4 * D, "dt_bias": 8 * D, "a_log": 8 * D + 64, "d_skip": 8 * D + 128,
             "ssm_norm_w": 9 * D, "conf_conv_b": 11 * D, "conf_ln_w": 12 * D, "conf_ln_b": 13 * D, "final_norm_w": 14 * D,
             "c_ctx": 15 * D}


def _pack_shard(s):
    return jnp.concatenate([s["w_mod"].reshape(R_MOD, D), _pack_rest(s), jnp.zeros((PROWS - PUSED, D), F32)], axis=0)


def _pack_rest(s):
    cc = jnp.pad(s["conf_conv_w"].reshape(1, CK * 256), ((0, 0), (0, R_CC * D - CK * 256))).reshape(R_CC, D)
    return jnp.concatenate([s["w_out_ssm"], s["w_out_conf"], s["w_out"],
                            jnp.pad(s["ssm_conv_w"], ((0, R_SC - SK), (0, 0))), cc], axis=0)


def _unpack_rest(p):
    o = lambda r: r - O_OS
    return {"w_out_ssm": p[o(O_OS):o(O_OC)][None], "w_out_conf": p[o(O_OC):o(O_O)][None], "w_out": p[o(O_O):o(O_SC)][None],
            "ssm_conv_w": p[o(O_SC):o(O_SC) + SK][None],
            "conf_conv_w": p[o(O_CC):o(O_CC) + R_CC].reshape(R_CC * D)[:CK * 256].reshape(1, CK, 256)}


def _shard_cols(a, n):
    return a.reshape(a.shape[0], NSHARD, n).transpose(1, 0, 2)


def _pack_full(g):
    cc = jnp.pad(_shard_cols(g["conf_conv_w"], 256).reshape(NSHARD, CK * 256), ((0, 0), (0, R_CC * D - CK * 256)))
    return jnp.concatenate([_shard_cols(g["w_mod"], R_MOD).reshape(NSHARD, R_MOD, D),
                            g["w_out_ssm"].reshape(NSHARD, R_OS, D), g["w_out_conf"].reshape(NSHARD, R_OC, D),
                            g["w_out"].reshape(NSHARD, R_O, D),
                            jnp.pad(_shard_cols(g["ssm_conv_w"], D), ((0, 0), (0, R_SC - SK), (0, 0))),
                            cc.reshape(NSHARD, R_CC, D), jnp.zeros((NSHARD, PROWS - PUSED, D), F32)], axis=1)


def _unpack_gathered(gm, gw, gs):
    def cols(a, r, n):
        return a.reshape(NSHARD, r, n).transpose(1, 0, 2).reshape(r, NSHARD * n)
    return {"w_mod": cols(gm[:, O_MOD:O_OS], D, R_MOD), "w_in": [gw[s] for s in range(NSHARD)],
            "w_out_ssm": gm[:, O_OS:O_OC].reshape(DI, D), "w_out_conf": gm[:, O_OC:O_O].reshape(D, D),
            "w_out": gm[:, O_O:O_SC].reshape(D, D), "ssm_conv_w": cols(gs[:, 0:SK], SK, D),
            "conf_conv_w": cols(gs[:, R_SC:R_SC + R_CC].reshape(NSHARD, R_CC * D)[:, :CK * 256], CK, 256)}


def _pack_small(d):
    flat = jnp.zeros((SROWS * D,), F32)
    for name, n in SMALL:
        flat = lax.dynamic_update_slice(flat, d[name].reshape(n).astype(F32), (SMALL_OFF[name],))
    return flat.reshape(SROWS, D)


def _unpack_small(p, shapes):
    flat = p.reshape(SROWS * D)
    return {name: flat[SMALL_OFF[name]:SMALL_OFF[name] + n].reshape(shapes[name]) for name, n in SMALL}


MESH_ID = pl.DeviceIdType.MESH
ANY = pl.BlockSpec(memory_space=pl.ANY)


def _place():
    x, y, c = lax.axis_index("x"), lax.axis_index("y"), lax.axis_index("c")
    return x, y, c, [(1 - x, y), (x, 1 - y), (1 - x, 1 - y)]


def _rcopy(src, dst, send, recv, dev):
    return pltpu.make_async_remote_copy(src_ref=src, dst_ref=dst, send_sem=send, recv_sem=recv,
                                        device_id=dev, device_id_type=MESH_ID)


def _gather_weights(mats, small):
    n = len(mats)

    def kern(*refs):
        m_refs, s_ref, g_refs, gs_ref, (send, recv) = refs[:n], refs[n], refs[n + 1:2 * n + 1], refs[2 * n + 1], refs[2 * n + 2:]
        x, y, c, chips = _place()
        me = 2 * x + y
        sib = (x, y, 1 - c)
        first, passed = [], []
        for k, (px, py) in enumerate(chips):
            first.append(_rcopy(s_ref, gs_ref.at[me], send.at[k], recv.at[k], (px, py, c)))
            for a, (m_ref, g_ref) in enumerate(zip(m_refs, g_refs)):
                mine = _half_rows(c, m_ref.shape[0])
                first.append(_rcopy(m_ref.at[mine], g_ref.at[me, mine], send.at[3 + 6 * a + k], recv.at[3 + 6 * a + k], (px, py, c)))
        for cp in first:
            cp.start()
        for k, (px, py) in enumerate(chips):
            s = 2 * px + py
            for a, (m_ref, g_ref) in enumerate(zip(m_refs, g_refs)):
                mine = _half_rows(c, m_ref.shape[0])
                _rcopy(m_ref.at[mine], g_ref.at[s, mine], send.at[3 + 6 * a + k], recv.at[3 + 6 * a + k], sib).wait_recv()
                f = _rcopy(g_ref.at[s, mine], g_ref.at[s, mine], send.at[6 + 6 * a + k], recv.at[6 + 6 * a + k], sib)
                f.start()
                passed.append(f)
        for k, (px, py) in enumerate(chips):
            s = 2 * px + py
            _rcopy(s_ref, gs_ref.at[s], send.at[k], recv.at[k], sib).wait_recv()
            for a, g_ref in enumerate(g_refs):
                other = _half_rows(1 - c, g_ref.shape[1])
                _rcopy(g_ref.at[s, other], g_ref.at[s, other], send.at[6 + 6 * a + k], recv.at[6 + 6 * a + k], sib).wait_recv()
        for cp in first + passed:
            cp.wait_send()

    nsem = 3 + 6 * n
    return pl.pallas_call(
        kern, name="gather_weights", in_specs=[ANY] * (n + 1), out_specs=[ANY] * (n + 1),
        out_shape=[jax.ShapeDtypeStruct((NSHARD,) + m.shape, m.dtype) for m in mats]
        + [jax.ShapeDtypeStruct((NSHARD, SROWS, D), F32)],
        scratch_shapes=[pltpu.SemaphoreType.DMA((nsem,)), pltpu.SemaphoreType.DMA((nsem,))],
    )(*mats, small)


def _half_rows(c, rows):
    return pl.ds(pl.multiple_of(c * (rows // 2), 16), rows // 2)


def _swap_halves(gs):
    n = len(gs)

    def kern(*refs):
        g_refs, o_refs, (send, recv) = refs[:n], refs[n:2 * n], refs[2 * n:]
        x, y, c, _ = _place()
        cps = [_rcopy(g_ref.at[s, _half_rows(1 - c, g_ref.shape[1])], o_ref.at[s], send.at[NSHARD * a + s],
                      recv.at[NSHARD * a + s], (x, y, 1 - c))
               for a, (g_ref, o_ref) in enumerate(zip(g_refs, o_refs)) for s in range(NSHARD)]
        for cp in cps:
            cp.start()
        for cp in cps:
            cp.wait()

    return pl.pallas_call(
        kern, name="swap_halves", in_specs=[ANY] * n, out_specs=[ANY] * n,
        out_shape=[jax.ShapeDtypeStruct((NSHARD, g.shape[1] // 2, g.shape[2]), F32) for g in gs],
        scratch_shapes=[pltpu.SemaphoreType.DMA((NSHARD * n,)), pltpu.SemaphoreType.DMA((NSHARD * n,))],
    )(*gs)


def _add_halves(cidx, g, ra, rb, name):
    _, half, cols = ra.shape
    nb = half // rb

    def kern(c_ref, g_ref, a_ref, o_ref):
        o_ref[...] = (g_ref[...] + a_ref[...]).astype(BF16)

    return pl.pallas_call(
        kern, name=name,
        grid_spec=pltpu.PrefetchScalarGridSpec(
            num_scalar_prefetch=1, grid=(NSHARD, nb),
            in_specs=[pl.BlockSpec((None, rb, cols), lambda s, i, c: (s, c[0] * nb + i, 0)),
                      pl.BlockSpec((None, rb, cols), lambda s, i, c: (s, i, 0))],
            out_specs=pl.BlockSpec((None, rb, cols), lambda s, i, c: (s, i, 0))),
        out_shape=jax.ShapeDtypeStruct((NSHARD, half, cols), BF16),
        compiler_params=_cp(("parallel", "parallel")),
    )(cidx, g, ra)


def _exchange_chips(ps):
    n = len(ps)

    def kern(*refs):
        p_refs, o_refs, (send, recv) = refs[:n], refs[n:2 * n], refs[2 * n:]
        x, y, c, chips = _place()
        cps = [_rcopy(p_ref.at[2 * px + py], o_ref.at[k], send.at[3 * a + k], recv.at[3 * a + k], (px, py, c))
               for a, (p_ref, o_ref) in enumerate(zip(p_refs, o_refs)) for k, (px, py) in enumerate(chips)]
        for cp in cps:
            cp.start()
        for cp in cps:
            cp.wait()

    return pl.pallas_call(
        kern, name="exchange_chips", in_specs=[ANY] * n, out_specs=[ANY] * n,
        out_shape=[jax.ShapeDtypeStruct((3,) + p.shape[1:], p.dtype) for p in ps],
        scratch_shapes=[pltpu.SemaphoreType.DMA((3 * n,)), pltpu.SemaphoreType.DMA((3 * n,))],
    )(*ps)


def _add_chips(mc, g, ra, rx, rb, name):
    _, half, cols = ra.shape
    nb = half // rb

    def kern(m_ref, g_ref, a_ref, r0_ref, r1_ref, r2_ref, o_ref):
        own = g_ref[...] + a_ref[...]
        o_ref[...] = ((own + r0_ref[...].astype(F32)) + r1_ref[...].astype(F32)) + r2_ref[...].astype(F32)

    return pl.pallas_call(
        kern, name=name,
        grid_spec=pltpu.PrefetchScalarGridSpec(
            num_scalar_prefetch=1, grid=(nb,),
            in_specs=[pl.BlockSpec((None, rb, cols), lambda i, m: (m[0], m[1] * nb + i, 0)),
                      pl.BlockSpec((None, rb, cols), lambda i, m: (m[0], i, 0))]
            + [pl.BlockSpec((None, rb, cols), functools.partial(lambda i, m, k: (k, i, 0), k=k)) for k in range(3)],
            out_specs=pl.BlockSpec((rb, cols), lambda i, m: (i, 0))),
        out_shape=jax.ShapeDtypeStruct((half, cols), F32),
        compiler_params=_cp(("parallel",)),
    )(mc, g, ra, rx, rx, rx)


def _share_halves(rs):
    n = len(rs)

    def kern(*refs):
        r_refs, o_refs, (send, recv) = refs[:n], refs[n:2 * n], refs[2 * n:]
        x, y, c, _ = _place()
        cps = [_rcopy(r_ref, o_ref, send.at[a], recv.at[a], (x, y, 1 - c))
               for a, (r_ref, o_ref) in enumerate(zip(r_refs, o_refs))]
        for cp in cps:
            cp.start()
        for cp in cps:
            cp.wait()

    return pl.pallas_call(
        kern, name="share_halves", in_specs=[ANY] * n, out_specs=[ANY] * n,
        out_shape=[jax.ShapeDtypeStruct(r.shape, F32) for r in rs],
        scratch_shapes=[pltpu.SemaphoreType.DMA((n,)), pltpu.SemaphoreType.DMA((n,))],
    )(*rs)


def _reduce_small(s):
    def kern(s_ref, o_ref, buf, send, recv):
        x, y, c, _ = _place()
        me = 4 * x + 2 * y + c
        buf[me] = s_ref[...]
        cps = []
        for r in range(1, 8):
            peer = (1 - x if r & 4 else x, 1 - y if r & 2 else y, 1 - c if r & 1 else c)
            cps.append(_rcopy(s_ref, buf.at[me], send.at[r - 1], recv.at[r - 1], peer))
        for cp in cps:
            cp.start()
        for cp in cps:
            cp.wait()
        acc = buf[0]
        for i in range(1, 8):
            acc = acc + buf[i]
        o_ref[...] = acc

    return pl.pallas_call(
        kern, name="reduce_small",
        in_specs=[pl.BlockSpec(memory_space=pltpu.VMEM)], out_specs=pl.BlockSpec(memory_space=pltpu.VMEM),
        out_shape=jax.ShapeDtypeStruct((SROWS, D), F32),
        scratch_shapes=[pltpu.VMEM((8, SROWS, D), F32), pltpu.SemaphoreType.DMA((7,)), pltpu.SemaphoreType.DMA((7,))],
    )(s)


def _adamw(g, w, m, v, rb, name):
    rows, cols = g.shape

    def kern(g_ref, w_ref, m_ref, v_ref, d_ref, nm_ref, nv_ref):
        g_ = g_ref[...]
        m_ = ADAM_B1 * m_ref[...] + (1.0 - ADAM_B1) * g_
        v_ = ADAM_B2 * v_ref[...] + (1.0 - ADAM_B2) * jnp.square(g_)
        m_hat = m_ / (1.0 - ADAM_B1 ** ADAM_STEP)
        v_hat = v_ / (1.0 - ADAM_B2 ** ADAM_STEP)
        d_ref[...] = -ADAM_LR * (m_hat / (jnp.sqrt(v_hat) + ADAM_EPS) + ADAM_WD * w_ref[...])
        nm_ref[...] = m_
        nv_ref[...] = v_

    assert rows % rb == 0
    blk = pl.BlockSpec((rb, cols), lambda i: (i, 0))
    return pl.pallas_call(
        kern, name=name, grid=(rows // rb,), in_specs=[blk] * 4, out_specs=[blk] * 3,
        out_shape=[jax.ShapeDtypeStruct((rows, cols), F32)] * 3,
        compiler_params=_cp(("parallel",)),
    )(g, w, m, v)


WEIGHTS = ("c_ctx", "w_mod", "b_mod", "norm_w", "w_in", "ssm_conv_w", "ssm_conv_b", "dt_bias", "a_log", "d_skip",
           "ssm_norm_w", "w_out_ssm", "conf_conv_w", "conf_conv_b", "conf_ln_w", "conf_ln_b", "w_out_conf", "w_out",
           "final_norm_w")


def kernel(x, c, ctx, c_ctx, w_mod, b_mod, norm_w, w_in, ssm_conv_w, ssm_conv_b, dt_bias, a_log, d_skip, ssm_norm_w, w_out_ssm, conf_conv_w, conf_conv_b, conf_ln_w, conf_ln_b, w_out_conf, w_out, final_norm_w, loss_target, m_c_ctx, m_w_mod, m_b_mod, m_norm_w, m_w_in, m_ssm_conv_w, m_ssm_conv_b, m_dt_bias, m_a_log, m_d_skip, m_ssm_norm_w, m_w_out_ssm, m_conf_conv_w, m_conf_conv_b, m_conf_ln_w, m_conf_ln_b, m_w_out_conf, m_w_out, m_final_norm_w, v_c_ctx, v_w_mod, v_b_mod, v_norm_w, v_w_in, v_ssm_conv_w, v_ssm_conv_b, v_dt_bias, v_a_log, v_d_skip, v_ssm_norm_w, v_w_out_ssm, v_conf_conv_w, v_conf_conv_b, v_conf_ln_w, v_conf_ln_b, v_w_out_conf, v_w_out, v_final_norm_w):
    wv = (c_ctx, w_mod, b_mod, norm_w, w_in, ssm_conv_w, ssm_conv_b, dt_bias, a_log, d_skip, ssm_norm_w, w_out_ssm,
          conf_conv_w, conf_conv_b, conf_ln_w, conf_ln_b, w_out_conf, w_out, final_norm_w)
    mv = (m_c_ctx, m_w_mod, m_b_mod, m_norm_w, m_w_in, m_ssm_conv_w, m_ssm_conv_b, m_dt_bias, m_a_log, m_d_skip,
          m_ssm_norm_w, m_w_out_ssm, m_conf_conv_w, m_conf_conv_b, m_conf_ln_w, m_conf_ln_b, m_w_out_conf, m_w_out,
          m_final_norm_w)
    vv = (v_c_ctx, v_w_mod, v_b_mod, v_norm_w, v_w_in, v_ssm_conv_w, v_ssm_conv_b, v_dt_bias, v_a_log, v_d_skip,
          v_ssm_norm_w, v_w_out_ssm, v_conf_conv_w, v_conf_conv_b, v_conf_ln_w, v_conf_ln_b, v_w_out_conf, v_w_out,
          v_final_norm_w)
    shapes = {n: a.shape for n, a in zip(WEIGHTS, wv)}

    def squeeze(d):
        return {n: (a if n in ("c_ctx", "final_norm_w") else a[0]) for n, a in d.items()}

    w, m, v = (squeeze(dict(zip(WEIGHTS, t))) for t in (wv, mv, vv))

    my_chip = 2 * lax.axis_index("x") + lax.axis_index("y")
    my_core = lax.axis_index("c")

    pw = _pack_shard(w)
    pwb, wib, psm = pw.astype(BF16), w["w_in"].astype(BF16), pw[O_SC:O_SC + SROWS]
    gm, gw, gs = _gather_weights([pwb, wib], psm)
    gm = lax.dynamic_update_slice(gm, pwb[None], (my_chip, 0, 0))
    gw = lax.dynamic_update_slice(gw, wib[None], (my_chip, 0, 0))
    gs = lax.dynamic_update_slice(gs, psm[None], (my_chip, 0, 0))
    full = dict(w)
    full.update(_unpack_gathered(gm, gw, gs))

    lsum, grad_x, g = _local_step(x[0], c, ctx[0], loss_target[0], full)
    loss = lax.psum(lsum, ("x", "y", "c"))

    cidx = my_core.astype(jnp.int32).reshape(1)
    mc = jnp.stack([my_chip, my_core]).astype(jnp.int32)
    gsrc = [_pack_full(g), g["w_in_shards"]]
    blocks = (RB, WB)
    sib = _swap_halves(gsrc)
    part = [_add_halves(cidx, a, b, rb, "add_halves_%d" % i) for i, (a, b, rb) in enumerate(zip(gsrc, sib, blocks))]
    far = _exchange_chips(part)
    red = [_add_chips(mc, a, b, f, rb, "add_chips_%d" % i) for i, (a, b, f, rb) in enumerate(zip(gsrc, sib, far, blocks))]
    got = _share_halves(red)
    g_pk, g_win = (jnp.concatenate([jnp.where(my_core == 0, r, o), jnp.where(my_core == 0, o, r)], axis=0)
                   for r, o in zip(red, got))
    g_sm = _reduce_small(_pack_small(g))

    gr = {"w_mod": g_pk[O_MOD:O_OS].reshape(D, R_MOD), "w_in": g_win, "rest": g_pk[O_OS:PUSED]}
    wr, mr, vr = ({"w_mod": t["w_mod"], "w_in": t["w_in"], "rest": _pack_rest(t)} for t in (w, m, v))
    res = {k: _adamw(gr[k], wr[k], mr[k], vr[k], rb, "adamw_" + k)
           for k, rb in (("w_in", WB), ("w_mod", 512), ("rest", (PUSED - O_OS) // 2))}
    res_sm = _adamw(g_sm, _pack_small(w), _pack_small(m), _pack_small(v), SROWS, "adamw_small")

    outs = []
    for i in range(4):
        pick = (lambda k: gr[k]) if i == 0 else (lambda k: res[k][i - 1])
        d = {"w_mod": pick("w_mod")[None], "w_in": pick("w_in")[None]}
        d.update(_unpack_rest(pick("rest")))
        d.update(_unpack_small(g_sm if i == 0 else res_sm[i - 1], shapes))
        outs.extend(d[n] for n in WEIGHTS)
    return (loss, grad_x[None], *outs)
```

```python
import functools

import jax
import jax.numpy as jnp
from jax import lax
from jax.experimental import pallas as pl
from jax.experimental.pallas import tpu as pltpu

F32, BF16 = jnp.float32, jnp.bfloat16

D = 1024
DI = 2048
NH = 32
HP = 64
NG = 8
HPG = 4
NS = 128
Q = 128
GW = 64
CK = 31
SK = 4
CTX = 256
EPS = 1e-6
XBC = DI + 2 * NG * NS
SSDW = XBC + 128
RESTW = 7168
T = 256
VMEM_LIMIT = 56 * 1024 * 1024

ADAM_LR, ADAM_B1, ADAM_B2, ADAM_EPS, ADAM_WD, ADAM_STEP = 0.001, 0.9, 0.999, 1e-08, 0.01, 10


def _cp(sem):
    return pltpu.CompilerParams(dimension_semantics=sem, vmem_limit_bytes=VMEM_LIMIT)


def _sig(x):
    return jax.nn.sigmoid(x)


def _silu(x):
    return x * _sig(x)


def _dsilu(x):
    s = _sig(x)
    return s * (1.0 + x * (1.0 - s))


def _dot(a, b):
    return jnp.dot(a, b, preferred_element_type=F32)


def _dot_nt(a, b):
    return lax.dot_general(a, b, (((1,), (1,)), ((), ())), preferred_element_type=F32)


def _split3(x):
    h = x.astype(BF16)
    r = x - h.astype(F32)
    m = r.astype(BF16)
    l = (r - m.astype(F32)).astype(BF16)
    return h, m, l


def _dot3_l(sel, x):
    h, m, l = _split3(x)
    return _dot(sel, h) + _dot(sel, m) + _dot(sel, l)


def _dot3_r(x, sel):
    h, m, l = _split3(x)
    return _dot(h, sel) + _dot(m, sel) + _dot(l, sel)


def _split2(x):
    h = x.astype(BF16)
    return h, (x - h.astype(F32)).astype(BF16)


def _dot2_l(sel, x):
    h, l = _split2(x)
    return _dot(sel, h) + _dot(sel, l)


def _dot2_r(x, sel):
    h, l = _split2(x)
    return _dot(h, sel) + _dot(l, sel)


def _iota(shape, dim):
    return lax.broadcasted_iota(jnp.int32, shape, dim)


def _mm(a, b, dims, m, n, k, bm, bn, bk, out_dtype, name):
    nk = k // bk
    assert m % bm == 0 and n % bn == 0 and k % bk == 0, (name, m, n, k, bm, bn, bk)

    def prod(a_ref, b_ref):
        av = a_ref[...].astype(BF16)
        bv = b_ref[...].astype(BF16)
        if dims == "nn":
            return _dot(av, bv)
        if dims == "nt":
            return _dot_nt(av, bv)
        return lax.dot_general(av, bv, (((0,), (0,)), ((), ())), preferred_element_type=F32)

    def kern_one(a_ref, b_ref, o_ref):
        o_ref[...] = prod(a_ref, b_ref).astype(out_dtype)

    def kern_acc(a_ref, b_ref, o_ref, acc):
        kk = pl.program_id(2)

        @pl.when(kk == 0)
        def _():
            acc[...] = jnp.zeros_like(acc)

        acc[...] += prod(a_ref, b_ref)

        @pl.when(kk == nk - 1)
        def _():
            o_ref[...] = acc[...].astype(out_dtype)

    if dims == "nn":
        a_spec = pl.BlockSpec((bm, bk), lambda j, i, kk: (i, kk))
        b_spec = pl.BlockSpec((bk, bn), lambda j, i, kk: (kk, j))
    elif dims == "nt":
        a_spec = pl.BlockSpec((bm, bk), lambda j, i, kk: (i, kk))
        b_spec = pl.BlockSpec((bn, bk), lambda j, i, kk: (j, kk))
    else:
        a_spec = pl.BlockSpec((bk, bm), lambda j, i, kk: (kk, i))
        b_spec = pl.BlockSpec((bk, bn), lambda j, i, kk: (kk, j))
    return pl.pallas_call(
        kern_one if nk == 1 else kern_acc, name=name,
        grid=(n // bn, m // bm, nk),
        in_specs=[a_spec, b_spec],
        out_specs=pl.BlockSpec((bm, bn), lambda j, i, kk: (i, j)),
        out_shape=jax.ShapeDtypeStruct((m, n), out_dtype),
        scratch_shapes=[] if nk == 1 else [pltpu.VMEM((bm, bn), F32)],
        compiler_params=_cp(("parallel", "parallel", "arbitrary")),
    )(a, b)


def _mod_fwd(cc, w_mod, b_mod):
    def kern(cc_ref, w_ref, b_ref, o_ref):
        s = _silu(cc_ref[...]).astype(BF16)
        o_ref[...] = _dot(s, w_ref[...]) + b_ref[...]

    return pl.pallas_call(
        kern, name="mod_fwd", grid=(3,),
        in_specs=[pl.BlockSpec((8, D), lambda j: (0, 0)), pl.BlockSpec((D, D), lambda j: (0, j)),
                  pl.BlockSpec((1, D), lambda j: (0, j))],
        out_specs=pl.BlockSpec((8, D), lambda j: (0, j)),
        out_shape=jax.ShapeDtypeStruct((8, 3 * D), F32),
        compiler_params=_cp(("parallel",)),
    )(cc, w_mod, b_mod)


def _mod_bwd(dmod, cc, cct, w_mod):
    def kern(dm_ref, cc_ref, cct_ref, w_ref, gw_ref, gb_ref, gc_ref):
        kk = pl.program_id(0)
        dm = dm_ref[...]
        sct = _silu(cct_ref[...])
        gw_ref[...] = sct[:, 0:1] * dm[0:1, :] + sct[:, 1:2] * dm[1:2, :]
        gb_ref[...] = jnp.broadcast_to(dm[0:1, :] + dm[1:2, :], dm.shape)

        @pl.when(kk == 0)
        def _():
            gc_ref[...] = jnp.zeros_like(gc_ref)

        gc_ref[...] += _dot_nt(dm.astype(BF16), w_ref[...])

        @pl.when(kk == 2)
        def _():
            gc_ref[...] = gc_ref[...] * _dsilu(cc_ref[...])

    return pl.pallas_call(
        kern, name="mod_bwd", grid=(3,),
        in_specs=[pl.BlockSpec((8, D), lambda j: (0, j)), pl.BlockSpec((8, D), lambda j: (0, 0)),
                  pl.BlockSpec((D, 8), lambda j: (0, 0)), pl.BlockSpec((D, D), lambda j: (0, j))],
        out_specs=[pl.BlockSpec((D, D), lambda j: (0, j)), pl.BlockSpec((8, D), lambda j: (0, j)),
                   pl.BlockSpec((8, D), lambda j: (0, 0))],
        out_shape=[jax.ShapeDtypeStruct((D, 3 * D), F32), jax.ShapeDtypeStruct((8, 3 * D), F32),
                   jax.ShapeDtypeStruct((8, D), F32)],
        compiler_params=_cp(("arbitrary",)),
    )(dmod, cc, cct, w_mod)


def _ext_specs(nct):
    return (pl.BlockSpec((T, D), lambda i: (jnp.minimum(i, nct - 1), 0)),
            pl.BlockSpec((T, D), lambda i: (jnp.maximum(i - nct, 0), 0)))


def _norm_fwd(ctx, xl, mod, nw, nct):
    lext = ctx.shape[0] + xl.shape[0]

    def kern(c_ref, x_ref, mod_ref, nw_ref, h_ref):
        is_ctx = pl.program_id(0) < nct
        x = jnp.where(is_ctx, c_ref[...], x_ref[...])
        r = lax.rsqrt(jnp.mean(x * x, axis=-1, keepdims=True) + EPS)
        xn = x * r * nw_ref[...]
        shift = jnp.where(is_ctx, mod_ref[1:2, 0:D], mod_ref[0:1, 0:D])
        scale = jnp.where(is_ctx, mod_ref[1:2, D:2 * D], mod_ref[0:1, D:2 * D])
        h_ref[...] = (xn * (1.0 + scale) + shift).astype(BF16)

    return pl.pallas_call(
        kern, name="norm_fwd", grid=(lext // T,),
        in_specs=[*_ext_specs(nct), pl.BlockSpec((8, 3 * D), lambda i: (0, 0)),
                  pl.BlockSpec((1, D), lambda i: (0, 0))],
        out_specs=pl.BlockSpec((T, D), lambda i: (i, 0)),
        out_shape=jax.ShapeDtypeStruct((lext, D), BF16),
        compiler_params=_cp(("parallel",)),
    )(ctx, xl, mod, nw)


def _norm_bwd(dha, dhb, ctx, xl, dx2, mod, nw, nct):
    lext = ctx.shape[0] + xl.shape[0]
    ntl = lext // T

    def kern(dha_ref, dhb_ref, c_ref, x_ref, dx2_ref, mod_ref, nw_ref, gx_ref, gnw_ref, dss_ref):
        i = pl.program_id(0)
        is_ctx = i < nct

        @pl.when(i == 0)
        def _():
            gnw_ref[...] = jnp.zeros_like(gnw_ref)
            dss_ref[...] = jnp.zeros_like(dss_ref)

        x = jnp.where(is_ctx, c_ref[...], x_ref[...])
        dh_ = dha_ref[...] + jnp.where(is_ctx, 0.0, dhb_ref[...])
        nw_ = nw_ref[...]
        r = lax.rsqrt(jnp.mean(x * x, axis=-1, keepdims=True) + EPS)
        xn = x * r
        scale = jnp.where(is_ctx, mod_ref[1:2, D:2 * D], mod_ref[0:1, D:2 * D])
        dsh = jnp.sum(dh_, axis=0, keepdims=True)
        dsc = jnp.sum(dh_ * (xn * nw_), axis=0, keepdims=True)
        row = jnp.concatenate([dsh, dsc], axis=1)
        rid = _iota((8, 2 * D), 0)
        dss_ref[...] += jnp.where(rid == jnp.where(is_ctx, 1, 0), row, 0.0)
        dxnw = dh_ * (1.0 + scale)
        gnw_ref[...] += jnp.broadcast_to(jnp.sum(dxnw * xn, axis=0, keepdims=True), (8, D))
        dxn = dxnw * nw_
        dx = r * (dxn - xn * jnp.mean(dxn * xn, axis=-1, keepdims=True))
        gx_ref[...] = dx2_ref[...] + dx

    return pl.pallas_call(
        kern, name="norm_bwd", grid=(ntl,),
        in_specs=[pl.BlockSpec((T, D), lambda i: (i, 0)), pl.BlockSpec((T, D), lambda i: (jnp.maximum(i - nct, 0), 0)),
                  *_ext_specs(nct),
                  pl.BlockSpec((T, D), lambda i: (jnp.maximum(i - nct, 0), 0)),
                  pl.BlockSpec((8, 3 * D), lambda i: (0, 0)), pl.BlockSpec((1, D), lambda i: (0, 0))],
        out_specs=[pl.BlockSpec((T, D), lambda i: (jnp.maximum(i - nct, 0), 0)),
                   pl.BlockSpec((8, D), lambda i: (0, 0)), pl.BlockSpec((8, 2 * D), lambda i: (0, 0))],
        out_shape=[jax.ShapeDtypeStruct((lext - nct * T, D), F32), jax.ShapeDtypeStruct((8, D), F32),
                   jax.ShapeDtypeStruct((8, 2 * D), F32)],
        compiler_params=_cp(("arbitrary",)),
    )(dha, dhb, ctx, xl, dx2, mod, nw)


CB = 1024


def _halo_specs(width_blk, col_off_blocks, ntl):
    t8 = T // 8
    main = pl.BlockSpec((T, width_blk), lambda j, i: (i, j + col_off_blocks))
    prev = pl.BlockSpec((8, width_blk), lambda j, i: (jnp.maximum(i * t8 - 1, 0), j + col_off_blocks))
    nxt = pl.BlockSpec((8, width_blk), lambda j, i: (jnp.minimum((i + 1) * t8, ntl * t8 - 1), j + col_off_blocks))
    return main, prev, nxt


def _seq_edges(i, nct, ntl):
    starts = jnp.logical_or(i == 0, i == nct)
    ends = jnp.logical_or(i == nct - 1, i == ntl - 1)
    return starts, ends


def _shifted(ext, off):
    n = ext.shape[0]
    return pltpu.roll(ext, (-off) % n, axis=0)[8:8 + T]


def _conv_fwd(proj_ssd, cw, cb, nct):
    lext = proj_ssd.shape[0]
    ntl = lext // T

    def kern(u_ref, up_ref, un_ref, w_ref, b_ref, o_ref):
        i = pl.program_id(1)
        starts, ends = _seq_edges(i, nct, ntl)
        up = jnp.where(starts, 0.0, up_ref[...])
        un = jnp.where(ends, 0.0, un_ref[...])
        ext = jnp.concatenate([up, u_ref[...], un], axis=0)
        w = w_ref[...]
        pre = b_ref[...] + w[0:1] * _shifted(ext, -2) + w[1:2] * _shifted(ext, -1) \
            + w[2:3] * u_ref[...] + w[3:4] * _shifted(ext, 1)
        o_ref[...] = _silu(pre)

    main, prev, nxt = _halo_specs(CB, 0, ntl)
    return pl.pallas_call(
        kern, name="conv_fwd", grid=(XBC // CB, ntl),
        in_specs=[main, prev, nxt, pl.BlockSpec((8, CB), lambda j, i: (0, j)), pl.BlockSpec((1, CB), lambda j, i: (0, j))],
        out_specs=pl.BlockSpec((T, CB), lambda j, i: (i, j)),
        out_shape=jax.ShapeDtypeStruct((lext, XBC), F32),
        compiler_params=_cp(("parallel", "parallel")),
    )(proj_ssd, proj_ssd, proj_ssd, cw, cb)


def _conv_bwd(dpost, proj_ssd, cw, cb, col_off, width, nct, name):
    lext = proj_ssd.shape[0]
    ntl = lext // T
    cob = col_off // CB

    def kern(u_ref, up_ref, un_ref, d_ref, dp_ref, dn_ref, w_ref, b_ref, du_ref, gw_ref, gb_ref):
        i = pl.program_id(1)

        @pl.when(i == 0)
        def _():
            gw_ref[...] = jnp.zeros_like(gw_ref)
            gb_ref[...] = jnp.zeros_like(gb_ref)

        starts, ends = _seq_edges(i, nct, ntl)
        ext = jnp.concatenate([jnp.where(starts, 0.0, up_ref[...]), u_ref[...], jnp.where(ends, 0.0, un_ref[...])], axis=0)
        dext = jnp.concatenate([jnp.where(starts, 0.0, dp_ref[...]), d_ref[...], jnp.where(ends, 0.0, dn_ref[...])], axis=0)
        w = w_ref[...]
        n = ext.shape[0]
        pre = b_ref[...] + w[0:1] * pltpu.roll(ext, 2, axis=0) + w[1:2] * pltpu.roll(ext, 1, axis=0) \
            + w[2:3] * ext + w[3:4] * pltpu.roll(ext, n - 1, axis=0)
        dpre = dext * _dsilu(pre)
        dm = dpre[8:8 + T]
        du = w[0:1] * _shifted(dpre, 2) + w[1:2] * _shifted(dpre, 1) + w[2:3] * dm + w[3:4] * _shifted(dpre, -1)
        du_ref[...] = du.astype(BF16)
        g0 = jnp.sum(dm * _shifted(ext, -2), axis=0, keepdims=True)
        g1 = jnp.sum(dm * _shifted(ext, -1), axis=0, keepdims=True)
        g2 = jnp.sum(dm * u_ref[...], axis=0, keepdims=True)
        g3 = jnp.sum(dm * _shifted(ext, 1), axis=0, keepdims=True)
        rid = _iota((8, CB), 0)
        gw_ref[...] += jnp.where(rid == 0, g0, jnp.where(rid == 1, g1, jnp.where(rid == 2, g2, jnp.where(rid == 3, g3, 0.0))))
        gb_ref[...] += jnp.broadcast_to(jnp.sum(dm, axis=0, keepdims=True), (8, CB))

    main, prev, nxt = _halo_specs(CB, cob, ntl)
    dmain, dprev, dnxt = _halo_specs(CB, 0, ntl)
    return pl.pallas_call(
        kern, name=name, grid=(width // CB, ntl),
        in_specs=[main, prev, nxt, dmain, dprev, dnxt,
                  pl.BlockSpec((8, CB), lambda j, i: (0, j + cob)), pl.BlockSpec((1, CB), lambda j, i: (0, j + cob))],
        out_specs=[pl.BlockSpec((T, CB), lambda j, i: (i, j)), pl.BlockSpec((8, CB), lambda j, i: (0, j)),
                   pl.BlockSpec((8, CB), lambda j, i: (0, j))],
        out_shape=[jax.ShapeDtypeStruct((lext, width), BF16), jax.ShapeDtypeStruct((8, width), F32),
                   jax.ShapeDtypeStruct((8, width), F32)],
        compiler_params=_cp(("parallel", "arbitrary")),
    )(proj_ssd, proj_ssd, proj_ssd, dpost, dpost, dpost, cw, cb)


def _tri(lower):
    r, c = _iota((Q, Q), 0), _iota((Q, Q), 1)
    return jnp.where((c <= r) if lower else (c >= r), 1.0, 0.0).astype(BF16)


def _is_bdir_lane(shape):
    ln = _iota(shape, len(shape) - 1)
    return jnp.logical_and(((ln >> 2) & 1) == 1, ln < 64)


def _dt_fwd(proj_ssd, dtb, av):
    lext = proj_ssd.shape[0]

    def kern(p_ref, b_ref, a_ref, dtg_ref, lag_ref, dtt_ref, lat_ref):
        lane = _iota((T, 128), 1)
        raw = p_ref[...] + b_ref[...]
        dt = jnp.where(lane < 64, jnp.maximum(raw, 0.0) + jnp.log1p(jnp.exp(-jnp.abs(raw))), 0.0)
        dta = dt * a_ref[...]
        tl, tu = _tri(True), _tri(False)
        isb = _is_bdir_lane((Q, 128))
        las = []
        for qq in range(T // Q):
            blk = dta[qq * Q:(qq + 1) * Q]
            las.append(jnp.where(isb, _dot3_l(tu, blk), _dot3_l(tl, blk)))
        la = jnp.concatenate(las, axis=0)
        for g in range(NG):
            sh = (128 - 8 * g) % 128
            dtg_ref[g] = jnp.where(lane < 8, pltpu.roll(dt, sh, axis=1) if sh else dt, 0.0)
            lag_ref[g] = jnp.where(lane < 8, pltpu.roll(la, sh, axis=1) if sh else la, 0.0)
        dtt_ref[...] = dt.T[0:64]
        lat_ref[...] = la.T[0:64]

    return pl.pallas_call(
        kern, name="dt_fwd", grid=(lext // T,),
        in_specs=[pl.BlockSpec((T, 128), lambda i: (i, XBC // 128)), pl.BlockSpec((1, 128), lambda i: (0, 0)),
                  pl.BlockSpec((1, 128), lambda i: (0, 0))],
        out_specs=[pl.BlockSpec((NG, T, 128), lambda i: (0, i, 0)), pl.BlockSpec((NG, T, 128), lambda i: (0, i, 0)),
                   pl.BlockSpec((64, T), lambda i: (0, i)), pl.BlockSpec((64, T), lambda i: (0, i))],
        out_shape=[jax.ShapeDtypeStruct((NG, lext, 128), F32), jax.ShapeDtypeStruct((NG, lext, 128), F32),
                   jax.ShapeDtypeStruct((64, lext), F32), jax.ShapeDtypeStruct((64, lext), F32)],
        compiler_params=_cp(("parallel",)),
    )(proj_ssd, dtb, av)


def _dt_bwd(ddtg, proj_ssd, dtb):
    lext = proj_ssd.shape[0]

    def kern(d_ref, p_ref, b_ref, o_ref, gb_ref):
        @pl.when(pl.program_id(0) == 0)
        def _():
            gb_ref[...] = jnp.zeros_like(gb_ref)

        acc = d_ref[0]
        for g in range(1, NG):
            acc = acc + pltpu.roll(d_ref[g], 8 * g, axis=1)
        draw = acc * _sig(p_ref[...] + b_ref[...])
        o_ref[...] = draw.astype(BF16)
        gb_ref[...] += jnp.broadcast_to(jnp.sum(draw, axis=0, keepdims=True), (8, 128))

    return pl.pallas_call(
        kern, name="dt_bwd", grid=(lext // T,),
        in_specs=[pl.BlockSpec((NG, T, 128), lambda i: (0, i, 0)), pl.BlockSpec((T, 128), lambda i: (i, XBC // 128)),
                  pl.BlockSpec((1, 128), lambda i: (0, 0))],
        out_specs=[pl.BlockSpec((T, 128), lambda i: (i, 0)), pl.BlockSpec((8, 128), lambda i: (0, 0))],
        out_shape=[jax.ShapeDtypeStruct((lext, 128), BF16), jax.ShapeDtypeStruct((8, 128), F32)],
        compiler_params=_cp(("arbitrary",)),
    )(ddtg, proj_ssd, dtb)


def _expand_sel(d):
    r, c = _iota((128, 256), 0), _iota((128, 256), 1)
    return jnp.where(r == 4 * d + (c >> 6), 1.0, 0.0).astype(BF16)


def _reduce_sel(d):
    r, c = _iota((256, 128), 0), _iota((256, 128), 1)
    return jnp.where(c == 4 * d + (r >> 6), 1.0, 0.0).astype(BF16)


def _chunk_of_bwd_dir(j, ncc, nc):
    return jnp.where(j < ncc, ncc - 1 - j, nc + ncc - 1 - j)


def _dir_terms(la, dt, d):
    lane = _iota(la.shape, 1)
    mine = jnp.logical_and(lane >= 4 * d, lane < 4 * d + 4)
    la = jnp.where(mine, la, 0.0)
    tot = la[Q - 1:Q] if d == 0 else la[0:1]
    wnd = jnp.exp(tot - la)
    return tot, wnd * jnp.where(mine, dt, 0.0), wnd


def _ssd_state(xbc, dtg, lag, ncc):
    lext = xbc.shape[0]
    nc = lext // Q

    def kern(xf_ref, bf_ref, dtf_ref, laf_ref, xb_ref, bb_ref, dtb_ref, lab_ref, hf_ref, hb_ref, sf, sb):
        @pl.when(pl.program_id(0) == 0)
        def _():
            sf[...] = jnp.zeros_like(sf)
            sb[...] = jnp.zeros_like(sb)

        for d, (x_ref, b_ref, dt_ref, la_ref, h_ref, s) in enumerate(
                ((xf_ref, bf_ref, dtf_ref, laf_ref, hf_ref, sf), (xb_ref, bb_ref, dtb_ref, lab_ref, hb_ref, sb))):
            h_ref[...] = s[...]
            ex = _expand_sel(d)
            for g in range(NG):
                cols = slice(256 * g, 256 * (g + 1))
                tot, w_end, _ = _dir_terms(la_ref[g], dt_ref[g], d)
                wexp = _dot2_r(w_end, ex)
                dexp = _dot2_r(jnp.broadcast_to(jnp.exp(tot), (8, 128)), ex)[0:1]
                xw = (x_ref[:, cols] * wexp).astype(BF16)
                s[:, cols] = s[:, cols] * dexp + _dot(b_ref[:, 128 * g:128 * (g + 1)].T.astype(BF16), xw)

    cb = functools.partial(_chunk_of_bwd_dir, ncc=ncc, nc=nc)
    sm = lambda f: pl.BlockSpec((NG, Q, 128), lambda j: (0, f(j), 0))
    one = lambda j: j
    return pl.pallas_call(
        kern, name="ssd_state", grid=(nc,),
        in_specs=[pl.BlockSpec((Q, DI), lambda j: (j, 0)), pl.BlockSpec((Q, NG * NS), lambda j: (j, 2)), sm(one), sm(one),
                  pl.BlockSpec((Q, DI), lambda j: (cb(j), 0)), pl.BlockSpec((Q, NG * NS), lambda j: (cb(j), 2)), sm(cb), sm(cb)],
        out_specs=[pl.BlockSpec((None, 128, DI), lambda j: (j, 0, 0)),
                   pl.BlockSpec((None, 128, DI), lambda j: (cb(j), 0, 0))],
        out_shape=[jax.ShapeDtypeStruct((nc, 128, DI), F32), jax.ShapeDtypeStruct((nc, 128, DI), F32)],
        scratch_shapes=[pltpu.VMEM((128, DI), F32), pltpu.VMEM((128, DI), F32)],
        compiler_params=_cp(("arbitrary",)),
    )(xbc, xbc, dtg, lag, xbc, xbc, dtg, lag)


def _ssd_out(xbc, dtg, lag, dtt, lat, htf, htb, ncc):
    lext = xbc.shape[0]
    nc = lext // Q
    ncx = nc - ncc

    gps = 4
    li, si = (lambda: _iota((Q, Q), 0)), (lambda: _iota((Q, Q), 1))

    def kern(x_ref, b_ref, c_ref, dtg_ref, lag_ref, dtt_ref, lat_ref, hf_ref, hb_ref, y_ref):
        lane = _iota((Q, 128), 1)
        masks = (li() >= si(), li() <= si())
        for gg in range(gps):
            cols = slice(256 * gg, 256 * (gg + 1))
            cm = c_ref[:, 128 * gg:128 * (gg + 1)]
            xb_ = x_ref[:, cols].astype(BF16)
            s_ = _dot_nt(cm.astype(BF16), b_ref[:, 128 * gg:128 * (gg + 1)].astype(BF16))
            la, dtt_, lat_ = lag_ref[gg], dtt_ref[8 * gg:8 * (gg + 1)], lat_ref[8 * gg:8 * (gg + 1)]
            elam = jnp.exp(la)
            yh = [jnp.zeros((Q, 128), F32), jnp.zeros((Q, 128), F32)]
            for d, h_ref in enumerate((hf_ref, hb_ref)):
                rhs = jnp.concatenate([xb_, h_ref[:, cols].astype(BF16)], axis=0)
                lhs = []
                for r in range(HPG):
                    j = 4 * d + r
                    lm = jnp.where(masks[d], jnp.exp(la[:, j:j + 1] - lat_[j:j + 1, :]), 0.0)
                    w = s_ * lm * dtt_[j:j + 1, :]
                    lhs.append(jnp.concatenate([w, cm * elam[:, j:j + 1]], axis=1).astype(BF16))
                for b in range(HPG // 2):
                    ypair = _dot(jnp.concatenate(lhs[2 * b:2 * b + 2], axis=0), rhs[:, 128 * b:128 * (b + 1)])
                    yh[b] = yh[b] + jnp.where(lane < 64, ypair[0:Q], ypair[Q:2 * Q])
            y_ref[:, cols] = jnp.concatenate(yh, axis=1)

    nb = NG // gps
    sm = pl.BlockSpec((gps, Q, 128), lambda c, g: (g, c + ncc, 0))
    smt = pl.BlockSpec((8 * gps, Q), lambda c, g: (g, c + ncc))
    st3 = pl.BlockSpec((None, 128, 256 * gps), lambda c, g: (c + ncc, 0, g))
    return pl.pallas_call(
        kern, name="ssd_out", grid=(ncx, nb),
        in_specs=[pl.BlockSpec((Q, 256 * gps), lambda c, g: (c + ncc, g)),
                  pl.BlockSpec((Q, 128 * gps), lambda c, g: (c + ncc, 2 * nb + g)),
                  pl.BlockSpec((Q, 128 * gps), lambda c, g: (c + ncc, 3 * nb + g)), sm, sm, smt, smt, st3, st3],
        out_specs=pl.BlockSpec((Q, 256 * gps), lambda c, g: (c, g)),
        out_shape=jax.ShapeDtypeStruct((ncx * Q, DI), F32),
        compiler_params=_cp(("parallel", "parallel")),
    )(xbc, xbc, xbc, dtg, lag, dtt, lat, htf, htb)


def _ssd_bwd_state(xbc, dy, lag, ncc):
    lext = xbc.shape[0]
    nc = lext // Q

    def kern(cf_ref, dyf_ref, laf_ref, cb_ref, dyb_ref, lab_ref, df_ref, db_ref, sf, sb):
        @pl.when(pl.program_id(0) == 0)
        def _():
            sf[...] = jnp.zeros_like(sf)
            sb[...] = jnp.zeros_like(sb)

        for d, (c_ref, dy_ref, la_ref, o_ref, s) in enumerate(
                ((cf_ref, dyf_ref, laf_ref, df_ref, sf), (cb_ref, dyb_ref, lab_ref, db_ref, sb))):
            o_ref[...] = s[...]
            ex = _expand_sel(d)
            for g in range(NG):
                cols = slice(256 * g, 256 * (g + 1))
                la = la_ref[g]
                tot = la[Q - 1:Q] if d == 0 else la[0:1]
                eexp = _dot2_r(jnp.exp(la), ex)
                dexp = _dot2_r(jnp.broadcast_to(jnp.exp(tot), (8, 128)), ex)[0:1]
                dye = (dy_ref[:, cols] * eexp).astype(BF16)
                s[:, cols] = s[:, cols] * dexp + _dot(c_ref[:, 128 * g:128 * (g + 1)].T.astype(BF16), dye)

    cf = lambda j: nc - 1 - j
    cb = lambda j: _chunk_of_bwd_dir(nc - 1 - j, ncc, nc)
    sm = lambda f: pl.BlockSpec((NG, Q, 128), lambda j: (0, f(j), 0))
    return pl.pallas_call(
        kern, name="ssd_bwd_state", grid=(nc,),
        in_specs=[pl.BlockSpec((Q, NG * NS), lambda j: (cf(j), 3)), pl.BlockSpec((Q, DI), lambda j: (cf(j), 0)), sm(cf),
                  pl.BlockSpec((Q, NG * NS), lambda j: (cb(j), 3)), pl.BlockSpec((Q, DI), lambda j: (cb(j), 0)), sm(cb)],
        out_specs=[pl.BlockSpec((None, 128, DI), lambda j: (cf(j), 0, 0)),
                   pl.BlockSpec((None, 128, DI), lambda j: (cb(j), 0, 0))],
        out_shape=[jax.ShapeDtypeStruct((nc, 128, DI), F32), jax.ShapeDtypeStruct((nc, 128, DI), F32)],
        scratch_shapes=[pltpu.VMEM((128, DI), F32), pltpu.VMEM((128, DI), F32)],
        compiler_params=_cp(("arbitrary",)),
    )(xbc, dy, lag, xbc, dy, lag)


def _ssd_bwd_out(xbc, dy, dsk, dtg, lag, dtt, lat, htf, htb, dhf, dhb, a_rows):
    lext = xbc.shape[0]
    nc = lext // Q

    gps = 1

    def kern(x_ref, b_ref, c_ref, dy_ref, sk_ref, dtg_ref, lag_ref, dtt_ref, lat_ref, hf_ref, hb_ref, df_ref, db_ref,
             a_ref, dx_ref, dbo_ref, dco_ref, ddt_ref, ga_ref):
        @pl.when(pl.program_id(1) == 0)
        def _():
            ga_ref[...] = jnp.zeros_like(ga_ref)

        for gg in range(gps):
            one_group(gg, x_ref, b_ref, c_ref, dy_ref, sk_ref, dtg_ref, lag_ref, dtt_ref, lat_ref, hf_ref, hb_ref, df_ref,
                      db_ref, a_ref, dx_ref, dbo_ref, dco_ref, ddt_ref, ga_ref)

    def one_group(gg, x_ref, b_ref, c_ref, dy_ref, sk_ref, dtg_ref, lag_ref, dtt_ref, lat_ref, hf_ref, hb_ref, df_ref,
                  db_ref, a_ref, dx_ref, dbo_ref, dco_ref, ddt_ref, ga_ref):
        g = pl.program_id(0) * gps + gg
        cols, cols128 = slice(256 * gg, 256 * (gg + 1)), slice(128 * gg, 128 * (gg + 1))
        x, bm, cm, dy_ = x_ref[:, cols], b_ref[:, cols128], c_ref[:, cols128], dy_ref[:, cols]
        xb_, bb_, cb_, dyb_ = x.astype(BF16), bm.astype(BF16), cm.astype(BF16), dy_.astype(BF16)
        st = _dot_nt(bb_, cb_)
        si, li = _iota((Q, Q), 0), _iota((Q, Q), 1)
        lane = _iota((Q, 256), 1)
        lane128 = _iota((Q, 128), 1)
        row128 = _iota((Q, 128), 0)
        sub = _iota((128, Q), 0)
        la, dt = lag_ref[gg], dtg_ref[gg]
        dtt_, lat_ = dtt_ref[8 * gg:8 * (gg + 1)], lat_ref[8 * gg:8 * (gg + 1)]
        elam = jnp.exp(la)
        dst = jnp.zeros((Q, Q), F32)
        dxh = [jnp.zeros((Q, 128), F32), jnp.zeros((Q, 128), F32)]
        cdir, clam = jnp.zeros((Q, 128), F32), jnp.zeros((Q, 128), F32)
        dba = jnp.zeros((Q, 128), F32)
        dca = jnp.zeros((Q, 128), F32)
        dlam = jnp.zeros((Q, 128), F32)
        ddir = jnp.zeros((Q, 128), F32)
        rows = jnp.zeros((128, Q), F32)
        for d, (h_ref, dh_ref) in enumerate(((hf_ref, df_ref), (hb_ref, db_ref))):
            ht, dht = h_ref[:, cols], dh_ref[:, cols]
            htb_, dhtb_ = ht.astype(BF16), dht.astype(BF16)
            tot, w_end, wnd = _dir_terms(la, dt, d)
            ex, rs = _expand_sel(d), _reduce_sel(d)
            elx = _dot2_r(elam, ex)
            wex = _dot2_r(w_end, ex)
            dye = dy_ * elx
            ch = _dot(cb_, htb_)
            bd = _dot(bb_, dhtb_)
            dca = dca + _dot_nt(dye.astype(BF16), htb_)
            dba = dba + _dot_nt((x * wex).astype(BF16), dhtb_)
            dlam = dlam + _dot2_r(dye * ch, rs)
            xbd = _dot2_r(x * bd, rs)
            e_ = w_end * xbd
            dlam = dlam - e_
            ddir = ddir + wnd * xbd
            hh = _dot2_r(jnp.broadcast_to(jnp.sum(dht * ht, axis=0, keepdims=True), (8, 256)), rs)[0:1]
            tot_term = jnp.sum(e_, axis=0, keepdims=True) + jnp.exp(tot) * hh
            dlam = dlam + jnp.where(row128 == (Q - 1 if d == 0 else 0), tot_term, 0.0)
            rhs = jnp.concatenate([dyb_, dhtb_], axis=0)
            maskt = (li >= si) if d == 0 else (li <= si)
            for r in range(HPG):
                j = 4 * d + r
                half = slice(128 * (r // 2), 128 * (r // 2 + 1))
                hm = (lane128 >> 6) == (r % 2)
                dc = dt[:, j:j + 1]
                lmt = jnp.where(maskt, jnp.exp(lat_[j:j + 1, :] - la[:, j:j + 1]), 0.0)
                ldc = lmt * jnp.broadcast_to(dc, (Q, Q))
                lhs = jnp.concatenate([st * ldc, bm * w_end[:, j:j + 1]], axis=1).astype(BF16)
                dxh[r // 2] = dxh[r // 2] + jnp.where(hm, _dot(lhs, rhs[:, half]), 0.0)
                dwt = _dot_nt(jnp.where(hm, x[:, half], 0.0).astype(BF16), dyb_[:, half])
                q = dwt * st
                cs = jnp.sum(q * lmt, axis=1, keepdims=True)
                cdir = jnp.where(lane128 == j, cs, cdir)
                clam = jnp.where(lane128 == j, cs * dc, clam)
                rows = rows + jnp.where(sub == j, jnp.sum(q * ldc, axis=0, keepdims=True), 0.0)
                dst = dst + dwt * ldc
        dxa = jnp.concatenate(dxh, axis=1)
        ddir = ddir + cdir
        dlam = dlam - clam + rows.T
        dba = dba + _dot(dst.astype(BF16), cb_)
        dca = dca + _dot(dst.T.astype(BF16), bb_)
        isb = jnp.logical_and(lane128 >= 4, lane128 < 8)
        ddel = jnp.where(isb, _dot2_l(_tri(True), dlam), _dot2_l(_tri(False), dlam))
        a_l = a_ref[pl.ds(g, 1), :]
        ddt_ref[gg] = ddir + a_l * ddel
        ga_ref[gg] += jnp.broadcast_to(a_l * jnp.sum(dt * ddel, axis=0, keepdims=True), (8, 128))
        dx_ref[:, cols] = dxa + dy_ * sk_ref[:, cols]
        dbo_ref[:, cols128] = dba
        dco_ref[:, cols128] = dca

    nb = NG // gps
    st3 = pl.BlockSpec((None, 128, 256 * gps), lambda g, c: (c, 0, g))
    sm = pl.BlockSpec((gps, Q, 128), lambda g, c: (g, c, 0))
    smt = pl.BlockSpec((8 * gps, Q), lambda g, c: (g, c))
    wide = pl.BlockSpec((Q, 256 * gps), lambda g, c: (c, g))
    return pl.pallas_call(
        kern, name="ssd_bwd_out", grid=(nb, nc),
        in_specs=[wide, pl.BlockSpec((Q, 128 * gps), lambda g, c: (c, 2 * nb + g)),
                  pl.BlockSpec((Q, 128 * gps), lambda g, c: (c, 3 * nb + g)), wide,
                  pl.BlockSpec((1, 256 * gps), lambda g, c: (0, g)), sm, sm, smt, smt, st3, st3, st3, st3,
                  pl.BlockSpec((8, 128), lambda g, c: (0, 0))],
        out_specs=[wide, pl.BlockSpec((Q, 128 * gps), lambda g, c: (c, g)),
                   pl.BlockSpec((Q, 128 * gps), lambda g, c: (c, g)), sm, pl.BlockSpec((gps, 8, 128), lambda g, c: (g, 0, 0))],
        out_shape=[jax.ShapeDtypeStruct((lext, DI), F32), jax.ShapeDtypeStruct((lext, NG * NS), F32),
                   jax.ShapeDtypeStruct((lext, NG * NS), F32), jax.ShapeDtypeStruct((NG, lext, 128), F32),
                   jax.ShapeDtypeStruct((NG, 8, 128), F32)],
        compiler_params=_cp(("parallel", "arbitrary")),
    )(xbc, xbc, xbc, dy, dsk, dtg, lag, dtt, lat, htf, htb, dhf, dhb, a_rows)


def _post_fwd(yssm, xbc, proj_rest, dsk, gnw, nct):
    l = yssm.shape[0]

    def kern(y_ref, x_ref, z_ref, dsk_ref, w_ref, o_ref):
        y = y_ref[...] + dsk_ref[...] * x_ref[...]
        yz = y * _silu(z_ref[...].astype(F32))
        for g in range(NG):
            sl = slice(256 * g, 256 * (g + 1))
            v = yz[:, sl]
            r = lax.rsqrt(jnp.mean(v * v, axis=-1, keepdims=True) + EPS)
            o_ref[:, sl] = (v * r * w_ref[:, sl]).astype(BF16)

    return pl.pallas_call(
        kern, name="post_fwd", grid=(l // T,),
        in_specs=[pl.BlockSpec((T, DI), lambda i: (i, 0)), pl.BlockSpec((T, DI), lambda i: (i + nct, 0)),
                  pl.BlockSpec((T, DI), lambda i: (i, 0)), pl.BlockSpec((1, DI), lambda i: (0, 0)),
                  pl.BlockSpec((1, DI), lambda i: (0, 0))],
        out_specs=pl.BlockSpec((T, DI), lambda i: (i, 0)),
        out_shape=jax.ShapeDtypeStruct((l, DI), BF16),
        compiler_params=_cp(("parallel",)),
    )(yssm, xbc, proj_rest, dsk, gnw)


def _post_bwd(dgn, yssm, xbc, proj_rest, dsk, gnw, dpr, nct):
    l = yssm.shape[0]
    lext = xbc.shape[0]
    xi = lambda i: (jnp.maximum(i - nct, 0), 0)

    def kern(dg_ref, y_ref, x_ref, z_ref, dsk_ref, w_ref, dpr_ref, dy_ref, dz_ref, gw_ref, gd_ref):
        i = pl.program_id(0)

        @pl.when(i == 0)
        def _():
            gw_ref[...] = jnp.zeros_like(gw_ref)
            gd_ref[...] = jnp.zeros_like(gd_ref)

        @pl.when(i < nct)
        def _():
            dy_ref[...] = jnp.zeros_like(dy_ref)

        @pl.when(i >= nct)
        def _():
            xs = x_ref[...]
            z = z_ref[...].astype(F32)
            y = y_ref[...] + dsk_ref[...] * xs
            sz = _silu(z)
            yz = y * sz
            dgn_ = dg_ref[...].astype(F32)
            dyz_parts = []
            gws = []
            for g in range(NG):
                sl = slice(256 * g, 256 * (g + 1))
                v = yz[:, sl]
                r = lax.rsqrt(jnp.mean(v * v, axis=-1, keepdims=True) + EPS)
                vn = v * r
                dn = dgn_[:, sl] * w_ref[:, sl]
                gws.append(jnp.sum(dgn_[:, sl] * vn, axis=0, keepdims=True))
                dyz_parts.append(r * (dn - vn * jnp.mean(dn * vn, axis=-1, keepdims=True)))
            dyz = jnp.concatenate(dyz_parts, axis=1)
            gw_ref[...] += jnp.broadcast_to(jnp.concatenate(gws, axis=1), (8, DI))
            dy = dyz * sz
            dz_ref[...] = (dyz * y * _dsilu(z)).astype(BF16)
            gd_ref[...] += jnp.broadcast_to(jnp.sum(dy * xs, axis=0, keepdims=True), (8, DI))
            dy_ref[...] = dy

    return pl.pallas_call(
        kern, name="post_bwd", grid=(lext // T,),
        in_specs=[pl.BlockSpec((T, DI), xi), pl.BlockSpec((T, DI), xi), pl.BlockSpec((T, DI), lambda i: (i, 0)),
                  pl.BlockSpec((T, DI), xi), pl.BlockSpec((1, DI), lambda i: (0, 0)), pl.BlockSpec((1, DI), lambda i: (0, 0)),
                  pl.BlockSpec(memory_space=pl.ANY)],
        out_specs=[pl.BlockSpec((T, DI), lambda i: (i, 0)),
                   pl.BlockSpec((T, DI), xi), pl.BlockSpec((8, DI), lambda i: (0, 0)), pl.BlockSpec((8, DI), lambda i: (0, 0))],
        out_shape=[jax.ShapeDtypeStruct((lext, DI), F32),
                   jax.ShapeDtypeStruct((l, RESTW), BF16), jax.ShapeDtypeStruct((8, DI), F32), jax.ShapeDtypeStruct((8, DI), F32)],
        input_output_aliases={6: 1},
        compiler_params=_cp(("arbitrary",)),
    )(dgn, yssm, xbc, proj_rest, dsk, gnw, dpr)


C_G1, C_G2, C_GA, C_GB, C_CG = 2, 3, 4, 5, 6
PITCH = GW + 16
NROW = T // GW


GAP = PITCH - GW
PADR = GAP + NROW * PITCH
NSTRIP = D // 128


def _fill_padded(pad8, val):
    z = jnp.zeros((GAP, D), F32)
    parts = [z]
    for r in range(NROW):
        parts += [val[GW * r:GW * (r + 1)], z]
    p = jnp.concatenate(parts, axis=0)
    pad8[0] = p
    for j in range(1, pad8.shape[0]):
        pad8[j] = pltpu.roll(p, PADR - j, axis=0)


def _tap(pad8, base, off, ln):
    return pad8[off % 8, pl.ds(base + off - off % 8, GW), ln]


def _row_conv(out_ref, pad8, w_ref, transpose):
    def strip(s, carry):
        ln = pl.ds(pl.multiple_of(s * 128, 128), 128)
        for r in range(NROW):
            base = GAP + PITCH * r
            acc = jnp.zeros((GW, 128), F32)
            for k in range(CK):
                off = (k - 15) if not transpose else (15 - k)
                acc = acc + w_ref[pl.ds(k, 1), ln] * _tap(pad8, base, off, ln)
            out_ref[pl.ds(GW * r, GW), ln] = acc
        return carry

    lax.fori_loop(0, NSTRIP, strip, 0)


def _row_conv_wgrad(gcw_ref, padd8, pada8):
    def strip(s, carry):
        ln = pl.ds(pl.multiple_of(s * 128, 128), 128)
        rid = _iota((32, 128), 0)
        g = jnp.zeros((32, 128), F32)
        for k0 in range(0, CK, 8):
            taps = range(k0, min(k0 + 8, CK))
            accs = {k: jnp.zeros((8, 128), F32) for k in taps}
            for r in range(NROW):
                base = GAP + PITCH * r
                d = _tap(padd8, base, 0, ln)
                for k in taps:
                    p = d * pada8[0, pl.ds(base + k - 15, GW), ln]
                    part = p[0:8]
                    for q in range(1, GW // 8):
                        part = part + p[8 * q:8 * (q + 1)]
                    accs[k] = accs[k] + part
            for k in taps:
                g = jnp.where(rid == k, jnp.sum(accs[k], axis=0, keepdims=True), g)
        gcw_ref[:, ln] += g
        return carry

    lax.fori_loop(0, NSTRIP, strip, 0)


def _ln_stats(cv):
    mu = jnp.mean(cv, axis=-1, keepdims=True)
    xc = cv - mu
    rs = lax.rsqrt(jnp.mean(xc * xc, axis=-1, keepdims=True) + EPS)
    return xc * rs, rs


def _conf_fwd(proj_rest, cw, cb, lw, lb):
    l = proj_rest.shape[0]

    def kern(ga_ref, gb_ref, cg_ref, cw_ref, cb_ref, lw_ref, lb_ref, o_ref, cv_ref, pad8):
        _fill_padded(pad8, ga_ref[...].astype(F32) * _sig(gb_ref[...].astype(F32)))
        _row_conv(cv_ref, pad8, cw_ref, False)
        cv = cv_ref[...] + cb_ref[...]
        cv_ref[...] = cv
        xh, _ = _ln_stats(cv)
        ln = xh * lw_ref[...] + lb_ref[...]
        o_ref[...] = (_silu(ln) * _silu(cg_ref[...].astype(F32))).astype(BF16)

    vec = pl.BlockSpec((1, D), lambda i: (0, 0))
    blk = pl.BlockSpec((T, D), lambda i: (i, 0))
    return pl.pallas_call(
        kern, name="conf_fwd", grid=(l // T,),
        in_specs=[pl.BlockSpec((T, D), lambda i: (i, C_GA)), pl.BlockSpec((T, D), lambda i: (i, C_GB)),
                  pl.BlockSpec((T, D), lambda i: (i, C_CG)), pl.BlockSpec((32, D), lambda i: (0, 0)), vec, vec, vec],
        out_specs=[blk, blk],
        out_shape=[jax.ShapeDtypeStruct((l, D), BF16), jax.ShapeDtypeStruct((l, D), F32)],
        scratch_shapes=[pltpu.VMEM((8, PADR, D), F32)],
        compiler_params=_cp(("parallel",)),
    )(proj_rest, proj_rest, proj_rest, cw, cb, lw, lb)


def _conf_bwd(duc, cv, proj_rest, cw, lw, lb, dpr):
    l = proj_rest.shape[0]

    def kern(du_ref, cv_ref, ga_ref, gb_ref, cg_ref, cw_ref, lw_ref, lb_ref, dpr_ref, o_ref, gcw_ref, gv_ref, sc,
             pada, padd, da_ref):
        i, j = pl.program_id(0), pl.program_id(1)

        @pl.when(jnp.logical_and(i == 0, j == 0))
        def _():
            gcw_ref[...] = jnp.zeros_like(gcw_ref)
            gv_ref[...] = jnp.zeros_like(gv_ref)

        @pl.when(j == 0)
        def _():
            ga, gb, cg = ga_ref[...].astype(F32), gb_ref[...].astype(F32), cg_ref[...].astype(F32)
            sg = _sig(gb)
            xh, rs = _ln_stats(cv_ref[...])
            ln = xh * lw_ref[...] + lb_ref[...]
            du = du_ref[...].astype(F32)
            sc[:, 2 * D:3 * D] = (du * _silu(ln) * _dsilu(cg)).astype(BF16)
            dln = du * _silu(cg) * _dsilu(ln)
            g_lw = jnp.sum(dln * xh, axis=0, keepdims=True)
            g_lb = jnp.sum(dln, axis=0, keepdims=True)
            dxh = dln * lw_ref[...]
            dcv = rs * (dxh - jnp.mean(dxh, axis=-1, keepdims=True) - xh * jnp.mean(dxh * xh, axis=-1, keepdims=True))
            g_cb = jnp.sum(dcv, axis=0, keepdims=True)
            rid = _iota((8, D), 0)
            gv_ref[...] += jnp.where(rid == 0, g_cb, jnp.where(rid == 1, g_lw, jnp.where(rid == 2, g_lb, 0.0)))
            _fill_padded(padd, dcv)
            _fill_padded(pada, ga * sg)
            _row_conv(da_ref, padd, cw_ref, True)
            _row_conv_wgrad(gcw_ref, padd, pada)
            da = da_ref[...]
            sc[:, 0:D] = (da * sg).astype(BF16)
            sc[:, D:2 * D] = (da * ga * sg * (1.0 - sg)).astype(BF16)

        o_ref[...] = sc[:, pl.ds(pl.multiple_of(j * D, 128), D)]

    vec = pl.BlockSpec((1, D), lambda i, j: (0, 0))
    col = lambda c: pl.BlockSpec((T, D), lambda i, j: (i, c))
    return pl.pallas_call(
        kern, name="conf_bwd", grid=(l // T, 3),
        in_specs=[col(0), col(0), col(C_GA), col(C_GB), col(C_CG), pl.BlockSpec((32, D), lambda i, j: (0, 0)), vec, vec,
                  pl.BlockSpec(memory_space=pl.ANY)],
        out_specs=[pl.BlockSpec((T, D), lambda i, j: (i, C_GA + j)), pl.BlockSpec((32, D), lambda i, j: (0, 0)),
                   pl.BlockSpec((8, D), lambda i, j: (0, 0))],
        out_shape=[jax.ShapeDtypeStruct((l, RESTW), BF16), jax.ShapeDtypeStruct((32, D), F32),
                   jax.ShapeDtypeStruct((8, D), F32)],
        scratch_shapes=[pltpu.VMEM((T, 3 * D), BF16), pltpu.VMEM((1, PADR, D), F32), pltpu.VMEM((8, PADR, D), F32),
                        pltpu.VMEM((T, D), F32)],
        input_output_aliases={8: 0},
        compiler_params=_cp(("arbitrary", "arbitrary")),
    )(duc, cv, proj_rest, proj_rest, proj_rest, cw, lw, lb, dpr)


def _merge_fwd(bs, bc, proj_rest):
    l = bs.shape[0]

    def kern(bs_ref, bc_ref, g1_ref, g2_ref, o_ref):
        up = lambda r: r[...].astype(F32)
        o_ref[...] = (_sig(up(g1_ref)) * up(bs_ref) + _sig(up(g2_ref)) * up(bc_ref)).astype(BF16)

    blk = pl.BlockSpec((T, D), lambda i: (i, 0))
    return pl.pallas_call(
        kern, name="merge_fwd", grid=(l // T,),
        in_specs=[blk, blk, pl.BlockSpec((T, D), lambda i: (i, C_G1)), pl.BlockSpec((T, D), lambda i: (i, C_G2))],
        out_specs=blk, out_shape=jax.ShapeDtypeStruct((l, D), BF16),
        compiler_params=_cp(("parallel",)),
    )(bs, bc, proj_rest, proj_rest)


def _merge_bwd(dm, bs, bc, proj_rest):
    l = bs.shape[0]

    def kern(dm_ref, bs_ref, bc_ref, g1_ref, g2_ref, dbs_ref, dbc_ref, dg_ref):
        up = lambda r: r[...].astype(F32)
        dm_ = up(dm_ref)
        s1, s2 = _sig(up(g1_ref)), _sig(up(g2_ref))
        dbs_ref[...] = (dm_ * s1).astype(BF16)
        dbc_ref[...] = (dm_ * s2).astype(BF16)
        dg_ref[:, 0:D] = (dm_ * up(bs_ref) * s1 * (1.0 - s1)).astype(BF16)
        dg_ref[:, D:2 * D] = (dm_ * up(bc_ref) * s2 * (1.0 - s2)).astype(BF16)

    blk = pl.BlockSpec((T, D), lambda i: (i, 0))
    return pl.pallas_call(
        kern, name="merge_bwd", grid=(l // T,),
        in_specs=[blk, blk, blk, pl.BlockSpec((T, D), lambda i: (i, C_G1)), pl.BlockSpec((T, D), lambda i: (i, C_G2))],
        out_specs=[blk, blk, pl.BlockSpec((T, 2 * D), lambda i: (i, 1))],
        out_shape=[jax.ShapeDtypeStruct((l, D), BF16), jax.ShapeDtypeStruct((l, D), BF16),
                   jax.ShapeDtypeStruct((l, RESTW), BF16)],
        compiler_params=_cp(("parallel",)),
    )(dm, bs, bc, proj_rest, proj_rest)


def _final(x, out, tgt, mod, fw):
    l = x.shape[0]

    def kern(x_ref, o_ref, t_ref, mod_ref, fw_ref, ls_ref, dx2_ref, do_ref, gv_ref):
        @pl.when(pl.program_id(0) == 0)
        def _():
            ls_ref[...] = jnp.zeros_like(ls_ref)
            gv_ref[...] = jnp.zeros_like(gv_ref)

        gate = mod_ref[0:1, 2 * D:3 * D]
        o = o_ref[...]
        x2 = x_ref[...] + gate * o
        r = lax.rsqrt(jnp.mean(x2 * x2, axis=-1, keepdims=True) + EPS)
        yn = x2 * r
        fw_ = fw_ref[...]
        e = yn * fw_ - t_ref[...]
        ls_ref[...] += jnp.full((8, 128), 1.0, F32) * (0.5 / D) * jnp.sum(e * e)
        dy = e * (1.0 / D)
        g_fw = jnp.sum(dy * yn, axis=0, keepdims=True)
        dyn = dy * fw_
        dx2 = r * (dyn - yn * jnp.mean(dyn * yn, axis=-1, keepdims=True))
        g_gate = jnp.sum(dx2 * o, axis=0, keepdims=True)
        rid = _iota((8, D), 0)
        gv_ref[...] += jnp.where(rid == 0, g_fw, jnp.where(rid == 1, g_gate, 0.0))
        dx2_ref[...] = dx2
        do_ref[...] = (dx2 * gate).astype(BF16)

    blk = pl.BlockSpec((T, D), lambda i: (i, 0))
    return pl.pallas_call(
        kern, name="final", grid=(l // T,),
        in_specs=[blk, blk, blk, pl.BlockSpec((8, 3 * D), lambda i: (0, 0)), pl.BlockSpec((1, D), lambda i: (0, 0))],
        out_specs=[pl.BlockSpec((8, 128), lambda i: (0, 0)), blk, blk, pl.BlockSpec((8, D), lambda i: (0, 0))],
        out_shape=[jax.ShapeDtypeStruct((8, 128), F32), jax.ShapeDtypeStruct((l, D), F32),
                   jax.ShapeDtypeStruct((l, D), BF16), jax.ShapeDtypeStruct((8, D), F32)],
        compiler_params=_cp(("arbitrary",)),
    )(x, out, tgt, mod, fw)


def _perm_dt_cols(w):
    s = w.shape[:-1]
    return w.reshape(*s, 2, NG, HPG).swapaxes(-3, -2).reshape(*s, 64)


def _unperm_dt_cols(w):
    s = w.shape[:-1]
    return w.reshape(*s, NG, 2, HPG).swapaxes(-3, -2).reshape(*s, 64)


def _pad_lanes(v, width):
    return jnp.pad(v, ((0, 0), (0, width - v.shape[1])))


def _vcols(segs, a, b):
    parts, off = [], 0
    for s in segs:
        lo, hi = max(a, off), min(b, off + s.shape[1])
        if lo < hi:
            parts.append(s[:, lo - off:hi - off])
        off += s.shape[1]
    return parts[0] if len(parts) == 1 else jnp.concatenate(parts, axis=1)


def _local_step(x, c, ctx, tgt, w):
    l = x.shape[0]
    nct = CTX // T
    ncc = CTX // Q
    lext = l + CTX

    w_mod = w["w_mod"].astype(BF16)
    wsegs = [s.astype(BF16) for s in (w["w_in"] if isinstance(w["w_in"], (list, tuple)) else [w["w_in"]])]
    w_ssd = jnp.concatenate([_vcols(wsegs, 0, XBC), _perm_dt_cols(_vcols(wsegs, XBC, XBC + 64)), jnp.zeros((D, 64), BF16)], axis=1)
    r0 = XBC + 64
    w_rest = jnp.concatenate([_vcols(wsegs, r0, r0 + DI), _vcols(wsegs, r0 + DI + 3 * D, r0 + RESTW),
                              _vcols(wsegs, r0 + DI, r0 + DI + 3 * D)], axis=1)
    w_os, w_oc, w_o = w["w_out_ssm"].astype(BF16), w["w_out_conf"].astype(BF16), w["w_out"].astype(BF16)
    cw8 = jnp.pad(w["ssm_conv_w"], ((0, 4), (0, 0)))
    cb_s = w["ssm_conv_b"].reshape(1, XBC)
    dtb = _pad_lanes(_perm_dt_cols(w["dt_bias"].reshape(1, 64)), 128)
    a_all = -jnp.exp(w["a_log"].reshape(1, 64))
    a_perm = _pad_lanes(_perm_dt_cols(a_all), 128)
    a_rows = _pad_lanes(_perm_dt_cols(a_all).reshape(NG, 8), 128)
    dsk = jnp.repeat(w["d_skip"].reshape(NH), HP).reshape(1, DI)
    gnw = w["ssm_norm_w"].reshape(1, DI)
    ccw = jnp.pad(w["conf_conv_w"], ((0, 1), (0, 0)))
    ccb, clw, clb = w["conf_conv_b"].reshape(1, D), w["conf_ln_w"].reshape(1, D), w["conf_ln_b"].reshape(1, D)
    nw = w["norm_w"].reshape(1, D)
    fw = w["final_norm_w"].reshape(1, D)
    cc = jnp.concatenate([c.reshape(1, D), w["c_ctx"].reshape(1, D), jnp.zeros((6, D), F32)], axis=0)

    bx = min(1024, l)
    be = 768 if lext % 768 == 0 else 256
    tk = min(1024, l)
    mod = _mod_fwd(cc, w_mod, w["b_mod"].reshape(1, 3 * D))
    h = _norm_fwd(ctx, x, mod, nw, nct)
    hx = h[CTX:]
    proj_ssd = _mm(h, w_ssd, "nn", lext, SSDW, D, be, SSDW // 3, D, F32, "proj_ssd")
    proj_rest = _mm(hx, w_rest, "nn", l, RESTW, D, bx, 1024, D, BF16, "proj_rest")
    xbc = _conv_fwd(proj_ssd, cw8, cb_s, nct)
    dtg, lag, dtt, lat = _dt_fwd(proj_ssd, dtb, a_perm)
    htf, htb = _ssd_state(xbc, dtg, lag, ncc)
    yssm = _ssd_out(xbc, dtg, lag, dtt, lat, htf, htb, ncc)
    gn = _post_fwd(yssm, xbc, proj_rest, dsk, gnw, nct)
    bs = _mm(gn, w_os, "nn", l, D, DI, bx, D, DI, BF16, "out_ssm")
    uc, cv = _conf_fwd(proj_rest, ccw, ccb, clw, clb)
    bc = _mm(uc, w_oc, "nn", l, D, D, bx, D, D, BF16, "out_conf")
    merged = _merge_fwd(bs, bc, proj_rest)
    out = _mm(merged, w_o, "nn", l, D, D, bx, D, D, F32, "out_proj")
    lsum, dx2, dout, gv_fin = _final(x, out, tgt, mod, fw)

    g = {}
    g["final_norm_w"] = gv_fin[0]
    dmerged = _mm(dout, w_o, "nt", l, D, D, bx, D, D, BF16, "d_merged")
    g["w_out"] = _mm(merged, dout, "tn", D, D, l, D, D, tk, F32, "g_w_out")
    dbs, dbc, dpr = _merge_bwd(dmerged, bs, bc, proj_rest)
    dgn = _mm(dbs, w_os, "nt", l, DI, D, bx, DI, D, BF16, "d_gn")
    g["w_out_ssm"] = _mm(gn, dbs, "tn", DI, D, l, DI, D, tk, F32, "g_w_out_ssm")
    duc = _mm(dbc, w_oc, "nt", l, D, D, bx, D, D, BF16, "d_uc")
    g["w_out_conf"] = _mm(uc, dbc, "tn", D, D, l, D, D, tk, F32, "g_w_out_conf")
    dpr, gcw, gv_conf = _conf_bwd(duc, cv, proj_rest, ccw, clw, clb, dpr)
    g["conf_conv_w"] = gcw[:CK]
    g["conf_conv_b"], g["conf_ln_w"], g["conf_ln_b"] = gv_conf[0], gv_conf[1], gv_conf[2]
    dy, dproj_rest, ggnw, gdsk = _post_bwd(dgn, yssm, xbc, proj_rest, dsk, gnw, dpr, nct)
    g["ssm_norm_w"] = ggnw[0]
    g["d_skip"] = gdsk[0].reshape(NH, HP).sum(axis=1)
    dhf, dhb = _ssd_bwd_state(xbc, dy, lag, ncc)
    dxs, dbm, dcm, ddtg, galog = _ssd_bwd_out(xbc, dy, dsk, dtg, lag, dtt, lat, htf, htb, dhf, dhb, a_rows)
    g["a_log"] = _unperm_dt_cols(galog[:, 0, 0:8].reshape(1, 64)).reshape(2, NH)
    dus, gws, gbs = [], [], []
    for dpost, off, width, nm in ((dxs, 0, DI, "conv_bwd_x"), (dbm, DI, NG * NS, "conv_bwd_b"), (dcm, DI + NG * NS, NG * NS, "conv_bwd_c")):
        du_, gw_, gb_ = _conv_bwd(dpost, proj_ssd, cw8, cb_s, off, width, nct, nm)
        dus.append(du_)
        gws.append(gw_[:SK])
        gbs.append(gb_[0])
    g["ssm_conv_w"] = jnp.concatenate(gws, axis=1)
    g["ssm_conv_b"] = jnp.concatenate(gbs, axis=0)
    ddt_raw, gdtb = _dt_bwd(ddtg, proj_ssd, dtb)
    g["dt_bias"] = _unperm_dt_cols(gdtb[0:1, 0:64]).reshape(2, NH)
    dproj_ssd = jnp.concatenate(dus + [ddt_raw], axis=1)
    gw_ssd = _mm(h, dproj_ssd, "tn", D, SSDW, lext, D, SSDW // 3, be, F32, "g_w_ssd")
    gw_rest = _mm(hx, dproj_rest, "tn", D, RESTW, l, D, 1024, tk, F32, "g_w_rest")
    gsegs = [gw_ssd[:, :XBC], _unperm_dt_cols(gw_ssd[:, XBC:XBC + 64]), gw_rest[:, :DI], gw_rest[:, 2 * DI:],
             gw_rest[:, DI:2 * DI]]
    g["w_in"] = jnp.concatenate(gsegs, axis=1)
    g["w_in_shards"] = jnp.stack([_vcols(gsegs, R_IN * s, R_IN * (s + 1)) for s in range(NSHARD)])
    dh_a = _mm(dproj_ssd, w_ssd, "nt", lext, D, SSDW, T, D, SSDW, F32, "dh_ssd")
    dh_b = _mm(dproj_rest, w_rest, "nt", l, D, RESTW, T, D, RESTW, F32, "dh_rest")
    grad_x, gnw_in, dss = _norm_bwd(dh_a, dh_b, ctx, x, dx2, mod, nw, nct)
    g["norm_w"] = gnw_in[0]
    dmod = jnp.concatenate([jnp.concatenate([dss[0:1], gv_fin[1:2]], axis=1),
                            jnp.concatenate([dss[1:2], jnp.zeros((1, D), F32)], axis=1),
                            jnp.zeros((6, 3 * D), F32)], axis=0)
    gwm, gbm, gcc = _mod_bwd(dmod, cc, cc.T, w_mod)
    g["w_mod"], g["b_mod"], g["c_ctx"] = gwm, gbm[0], gcc[1]
    return lsum[0, 0], grad_x, g


NSHARD = 4
R_MOD, R_IN, R_OS, R_OC, R_O, R_SC, R_CC = 768, 2832, 512, 256, 256, 8, 8
O_MOD = 0
O_OS = O_MOD + R_MOD
O_OC = O_OS + R_OS
O_O = O_OC + R_OC
O_SC = O_O + R_O
O_CC = O_SC + R_SC
PUSED = O_CC + R_CC
PROWS = 1824
HALF = PROWS // 2
RB = HALF // 3
WB = 128
SROWS = 16
SHARDED = ("w_mod", "w_in", "w_out_ssm", "w_out_conf", "w_out", "ssm_conv_w", "conf_conv_w")
SMALL = (("b_mod", 3 * D), ("norm_w", D), ("ssm_conv_b", XBC), ("dt_bias", 64), ("a_log", 64), ("d_skip", NH),
         ("ssm_norm_w", DI), ("conf_conv_b", D), ("conf_ln_w", D), ("conf_ln_b", D), ("final_norm_w", D), ("c_ctx", D))
SMALL_OFF = {"b_mod": 0, "norm_w": 3 * D, "ssm_conv_b": 4 * D, "dt_bias": 8 * D, "a_log": 8 * D + 64, "d_skip": 8 * D + 128,
             "ssm_norm_w": 9 * D, "conf_conv_b": 11 * D, "conf_ln_w": 12 * D, "conf_ln_b": 13 * D, "final_norm_w": 14 * D,
             "c_ctx": 15 * D}


def _pack_shard(s):
    return jnp.concatenate([s["w_mod"].reshape(R_MOD, D), _pack_rest(s), jnp.zeros((PROWS - PUSED, D), F32)], axis=0)


def _pack_rest(s):
    cc = jnp.pad(s["conf_conv_w"].reshape(1, CK * 256), ((0, 0), (0, R_CC * D - CK * 256))).reshape(R_CC, D)
    return jnp.concatenate([s["w_out_ssm"], s["w_out_conf"], s["w_out"],
                            jnp.pad(s["ssm_conv_w"], ((0, R_SC - SK), (0, 0))), cc], axis=0)


def _unpack_rest(p):
    o = lambda r: r - O_OS
    return {"w_out_ssm": p[o(O_OS):o(O_OC)][None], "w_out_conf": p[o(O_OC):o(O_O)][None], "w_out": p[o(O_O):o(O_SC)][None],
            "ssm_conv_w": p[o(O_SC):o(O_SC) + SK][None],
            "conf_conv_w": p[o(O_CC):o(O_CC) + R_CC].reshape(R_CC * D)[:CK * 256].reshape(1, CK, 256)}


def _shard_cols(a, n):
    return a.reshape(a.shape[0], NSHARD, n).transpose(1, 0, 2)


def _pack_full(g):
    cc = jnp.pad(_shard_cols(g["conf_conv_w"], 256).reshape(NSHARD, CK * 256), ((0, 0), (0, R_CC * D - CK * 256)))
    return jnp.concatenate([_shard_cols(g["w_mod"], R_MOD).reshape(NSHARD, R_MOD, D),
                            g["w_out_ssm"].reshape(NSHARD, R_OS, D), g["w_out_conf"].reshape(NSHARD, R_OC, D),
                            g["w_out"].reshape(NSHARD, R_O, D),
                            jnp.pad(_shard_cols(g["ssm_conv_w"], D), ((0, 0), (0, R_SC - SK), (0, 0))),
                            cc.reshape(NSHARD, R_CC, D), jnp.zeros((NSHARD, PROWS - PUSED, D), F32)], axis=1)


def _unpack_gathered(gm, gw, gs):
    def cols(a, r, n):
        return a.reshape(NSHARD, r, n).transpose(1, 0, 2).reshape(r, NSHARD * n)
    return {"w_mod": cols(gm[:, O_MOD:O_OS], D, R_MOD), "w_in": [gw[s] for s in range(NSHARD)],
            "w_out_ssm": gm[:, O_OS:O_OC].reshape(DI, D), "w_out_conf": gm[:, O_OC:O_O].reshape(D, D),
            "w_out": gm[:, O_O:O_SC].reshape(D, D), "ssm_conv_w": cols(gs[:, 0:SK], SK, D),
            "conf_conv_w": cols(gs[:, R_SC:R_SC + R_CC].reshape(NSHARD, R_CC * D)[:, :CK * 256], CK, 256)}


def _pack_small(d):
    flat = jnp.zeros((SROWS * D,), F32)
    for name, n in SMALL:
        flat = lax.dynamic_update_slice(flat, d[name].reshape(n).astype(F32), (SMALL_OFF[name],))
    return flat.reshape(SROWS, D)


def _unpack_small(p, shapes):
    flat = p.reshape(SROWS * D)
    return {name: flat[SMALL_OFF[name]:SMALL_OFF[name] + n].reshape(shapes[name]) for name, n in SMALL}


MESH_ID = pl.DeviceIdType.MESH
ANY = pl.BlockSpec(memory_space=pl.ANY)


def _place():
    x, y, c = lax.axis_index("x"), lax.axis_index("y"), lax.axis_index("c")
    return x, y, c, [(1 - x, y), (x, 1 - y), (1 - x, 1 - y)]


def _rcopy(src, dst, send, recv, dev):
    return pltpu.make_async_remote_copy(src_ref=src, dst_ref=dst, send_sem=send, recv_sem=recv,
                                        device_id=dev, device_id_type=MESH_ID)


def _gather_weights(mats, small):
    n = len(mats)

    def kern(*refs):
        m_refs, s_ref, g_refs, gs_ref, (send, recv) = refs[:n], refs[n], refs[n + 1:2 * n + 1], refs[2 * n + 1], refs[2 * n + 2:]
        x, y, c, chips = _place()
        me = 2 * x + y
        sib = (x, y, 1 - c)
        first, passed = [], []
        for k, (px, py) in enumerate(chips):
            first.append(_rcopy(s_ref, gs_ref.at[me], send.at[k], recv.at[k], (px, py, c)))
            for a, (m_ref, g_ref) in enumerate(zip(m_refs, g_refs)):
                mine = _half_rows(c, m_ref.shape[0])
                first.append(_rcopy(m_ref.at[mine], g_ref.at[me, mine], send.at[3 + 6 * a + k], recv.at[3 + 6 * a + k], (px, py, c)))
        for cp in first:
            cp.start()
        for k, (px, py) in enumerate(chips):
            s = 2 * px + py
            for a, (m_ref, g_ref) in enumerate(zip(m_refs, g_refs)):
                mine = _half_rows(c, m_ref.shape[0])
                _rcopy(m_ref.at[mine], g_ref.at[s, mine], send.at[3 + 6 * a + k], recv.at[3 + 6 * a + k], sib).wait_recv()
                f = _rcopy(g_ref.at[s, mine], g_ref.at[s, mine], send.at[6 + 6 * a + k], recv.at[6 + 6 * a + k], sib)
                f.start()
                passed.append(f)
        for k, (px, py) in enumerate(chips):
            s = 2 * px + py
            _rcopy(s_ref, gs_ref.at[s], send.at[k], recv.at[k], sib).wait_recv()
            for a, g_ref in enumerate(g_refs):
                other = _half_rows(1 - c, g_ref.shape[1])
                _rcopy(g_ref.at[s, other], g_ref.at[s, other], send.at[6 + 6 * a + k], recv.at[6 + 6 * a + k], sib).wait_recv()
        for cp in first + passed:
            cp.wait_send()

    nsem = 3 + 6 * n
    return pl.pallas_call(
        kern, name="gather_weights", in_specs=[ANY] * (n + 1), out_specs=[ANY] * (n + 1),
        out_shape=[jax.ShapeDtypeStruct((NSHARD,) + m.shape, m.dtype) for m in mats]
        + [jax.ShapeDtypeStruct((NSHARD, SROWS, D), F32)],
        scratch_shapes=[pltpu.SemaphoreType.DMA((nsem,)), pltpu.SemaphoreType.DMA((nsem,))],
    )(*mats, small)


def _half_rows(c, rows):
    return pl.ds(pl.multiple_of(c * (rows // 2), 16), rows // 2)


def _swap_halves(gs):
    n = len(gs)

    def kern(*refs):
        g_refs, o_refs, (send, recv) = refs[:n], refs[n:2 * n], refs[2 * n:]
        x, y, c, _ = _place()
        cps = [_rcopy(g_ref.at[s, _half_rows(1 - c, g_ref.shape[1])], o_ref.at[s], send.at[NSHARD * a + s],
                      recv.at[NSHARD * a + s], (x, y, 1 - c))
               for a, (g_ref, o_ref) in enumerate(zip(g_refs, o_refs)) for s in range(NSHARD)]
        for cp in cps:
            cp.start()
        for cp in cps:
            cp.wait()

    return pl.pallas_call(
        kern, name="swap_halves", in_specs=[ANY] * n, out_specs=[ANY] * n,
        out_shape=[jax.ShapeDtypeStruct((NSHARD, g.shape[1] // 2, g.shape[2]), F32) for g in gs],
        scratch_shapes=[pltpu.SemaphoreType.DMA((NSHARD * n,)), pltpu.SemaphoreType.DMA((NSHARD * n,))],
    )(*gs)


def _add_halves(cidx, g, ra, rb, name):
    _, half, cols = ra.shape
    nb = half // rb

    def kern(c_ref, g_ref, a_ref, o_ref):
        o_ref[...] = (g_ref[...] + a_ref[...]).astype(BF16)

    return pl.pallas_call(
        kern, name=name,
        grid_spec=pltpu.PrefetchScalarGridSpec(
            num_scalar_prefetch=1, grid=(NSHARD, nb),
            in_specs=[pl.BlockSpec((None, rb, cols), lambda s, i, c: (s, c[0] * nb + i, 0)),
                      pl.BlockSpec((None, rb, cols), lambda s, i, c: (s, i, 0))],
            out_specs=pl.BlockSpec((None, rb, cols), lambda s, i, c: (s, i, 0))),
        out_shape=jax.ShapeDtypeStruct((NSHARD, half, cols), BF16),
        compiler_params=_cp(("parallel", "parallel")),
    )(cidx, g, ra)


def _exchange_chips(ps):
    n = len(ps)

    def kern(*refs):
        p_refs, o_refs, (send, recv) = refs[:n], refs[n:2 * n], refs[2 * n:]
        x, y, c, chips = _place()
        cps = [_rcopy(p_ref.at[2 * px + py], o_ref.at[k], send.at[3 * a + k], recv.at[3 * a + k], (px, py, c))
               for a, (p_ref, o_ref) in enumerate(zip(p_refs, o_refs)) for k, (px, py) in enumerate(chips)]
        for cp in cps:
            cp.start()
        for cp in cps:
            cp.wait()

    return pl.pallas_call(
        kern, name="exchange_chips", in_specs=[ANY] * n, out_specs=[ANY] * n,
        out_shape=[jax.ShapeDtypeStruct((3,) + p.shape[1:], p.dtype) for p in ps],
        scratch_shapes=[pltpu.SemaphoreType.DMA((3 * n,)), pltpu.SemaphoreType.DMA((3 * n,))],
    )(*ps)


def _add_chips(mc, g, ra, rx, rb, name):
    _, half, cols = ra.shape
    nb = half // rb

    def kern(m_ref, g_ref, a_ref, r0_ref, r1_ref, r2_ref, o_ref):
        own = g_ref[...] + a_ref[...]
        o_ref[...] = ((own + r0_ref[...].astype(F32)) + r1_ref[...].astype(F32)) + r2_ref[...].astype(F32)

    return pl.pallas_call(
        kern, name=name,
        grid_spec=pltpu.PrefetchScalarGridSpec(
            num_scalar_prefetch=1, grid=(nb,),
            in_specs=[pl.BlockSpec((None, rb, cols), lambda i, m: (m[0], m[1] * nb + i, 0)),
                      pl.BlockSpec((None, rb, cols), lambda i, m: (m[0], i, 0))]
            + [pl.BlockSpec((None, rb, cols), functools.partial(lambda i, m, k: (k, i, 0), k=k)) for k in range(3)],
            out_specs=pl.BlockSpec((rb, cols), lambda i, m: (i, 0))),
        out_shape=jax.ShapeDtypeStruct((half, cols), F32),
        compiler_params=_cp(("parallel",)),
    )(mc, g, ra, rx, rx, rx)


def _share_halves(rs):
    n = len(rs)

    def kern(*refs):
        r_refs, o_refs, (send, recv) = refs[:n], refs[n:2 * n], refs[2 * n:]
        x, y, c, _ = _place()
        cps = [_rcopy(r_ref, o_ref, send.at[a], recv.at[a], (x, y, 1 - c))
               for a, (r_ref, o_ref) in enumerate(zip(r_refs, o_refs))]
        for cp in cps:
            cp.start()
        for cp in cps:
            cp.wait()

    return pl.pallas_call(
        kern, name="share_halves", in_specs=[ANY] * n, out_specs=[ANY] * n,
        out_shape=[jax.ShapeDtypeStruct(r.shape, F32) for r in rs],
        scratch_shapes=[pltpu.SemaphoreType.DMA((n,)), pltpu.SemaphoreType.DMA((n,))],
    )(*rs)


SMALL_W = XBC


def _small_update(gs, ws, ms, vs):
    n = len(gs)
    widths = [g.shape[1] for g in gs]
    assert n <= SROWS and max(widths) <= SMALL_W

    def kern(*refs):
        g_refs, w_refs, m_refs, v_refs = (refs[n * i:n * (i + 1)] for i in range(4))
        o_g, o_d, o_m, o_v = (refs[n * (4 + i):n * (5 + i)] for i in range(4))
        buf, send, recv = refs[8 * n:]
        x, y, c, _ = _place()
        me = 4 * x + 2 * y + c
        buf[me] = jnp.zeros((SROWS, SMALL_W), F32)
        for k, g_ref in enumerate(g_refs):
            buf[me, k:k + 1, 0:widths[k]] = g_ref[...]
        cps = []
        for r in range(1, 8):
            peer = (1 - x if r & 4 else x, 1 - y if r & 2 else y, 1 - c if r & 1 else c)
            cps.append(_rcopy(buf.at[me], buf.at[me], send.at[r - 1], recv.at[r - 1], peer))
        for cp in cps:
            cp.start()
        for cp in cps:
            cp.wait()
        acc = buf[0]
        for i in range(1, 8):
            acc = acc + buf[i]
        for k in range(n):
            g_ = acc[k:k + 1, 0:widths[k]]
            m_ = ADAM_B1 * m_refs[k][...] + (1.0 - ADAM_B1) * g_
            v_ = ADAM_B2 * v_refs[k][...] + (1.0 - ADAM_B2) * jnp.square(g_)
            m_hat = m_ / (1.0 - ADAM_B1 ** ADAM_STEP)
            v_hat = v_ / (1.0 - ADAM_B2 ** ADAM_STEP)
            o_g[k][...] = g_
            o_d[k][...] = -ADAM_LR * (m_hat / (jnp.sqrt(v_hat) + ADAM_EPS) + ADAM_WD * w_refs[k][...])
            o_m[k][...] = m_
            o_v[k][...] = v_

    vm = pl.BlockSpec(memory_space=pltpu.VMEM)
    outs = pl.pallas_call(
        kern, name="small_update", in_specs=[vm] * (4 * n), out_specs=[vm] * (4 * n),
        out_shape=[jax.ShapeDtypeStruct((1, wd), F32) for _ in range(4) for wd in widths],
        scratch_shapes=[pltpu.VMEM((8, SROWS, SMALL_W), F32), pltpu.SemaphoreType.DMA((7,)), pltpu.SemaphoreType.DMA((7,))],
    )(*gs, *ws, *ms, *vs)
    return [outs[n * i:n * (i + 1)] for i in range(4)]


def _adamw(g, w, m, v, rb, name):
    rows, cols = g.shape

    def kern(g_ref, w_ref, m_ref, v_ref, d_ref, nm_ref, nv_ref):
        g_ = g_ref[...]
        m_ = ADAM_B1 * m_ref[...] + (1.0 - ADAM_B1) * g_
        v_ = ADAM_B2 * v_ref[...] + (1.0 - ADAM_B2) * jnp.square(g_)
        m_hat = m_ / (1.0 - ADAM_B1 ** ADAM_STEP)
        v_hat = v_ / (1.0 - ADAM_B2 ** ADAM_STEP)
        d_ref[...] = -ADAM_LR * (m_hat / (jnp.sqrt(v_hat) + ADAM_EPS) + ADAM_WD * w_ref[...])
        nm_ref[...] = m_
        nv_ref[...] = v_

    assert rows % rb == 0
    blk = pl.BlockSpec((rb, cols), lambda i: (i, 0))
    return pl.pallas_call(
        kern, name=name, grid=(rows // rb,), in_specs=[blk] * 4, out_specs=[blk] * 3,
        out_shape=[jax.ShapeDtypeStruct((rows, cols), F32)] * 3,
        compiler_params=_cp(("parallel",)),
    )(g, w, m, v)


WEIGHTS = ("c_ctx", "w_mod", "b_mod", "norm_w", "w_in", "ssm_conv_w", "ssm_conv_b", "dt_bias", "a_log", "d_skip",
           "ssm_norm_w", "w_out_ssm", "conf_conv_w", "conf_conv_b", "conf_ln_w", "conf_ln_b", "w_out_conf", "w_out",
           "final_norm_w")


def kernel(x, c, ctx, c_ctx, w_mod, b_mod, norm_w, w_in, ssm_conv_w, ssm_conv_b, dt_bias, a_log, d_skip, ssm_norm_w, w_out_ssm, conf_conv_w, conf_conv_b, conf_ln_w, conf_ln_b, w_out_conf, w_out, final_norm_w, loss_target, m_c_ctx, m_w_mod, m_b_mod, m_norm_w, m_w_in, m_ssm_conv_w, m_ssm_conv_b, m_dt_bias, m_a_log, m_d_skip, m_ssm_norm_w, m_w_out_ssm, m_conf_conv_w, m_conf_conv_b, m_conf_ln_w, m_conf_ln_b, m_w_out_conf, m_w_out, m_final_norm_w, v_c_ctx, v_w_mod, v_b_mod, v_norm_w, v_w_in, v_ssm_conv_w, v_ssm_conv_b, v_dt_bias, v_a_log, v_d_skip, v_ssm_norm_w, v_w_out_ssm, v_conf_conv_w, v_conf_conv_b, v_conf_ln_w, v_conf_ln_b, v_w_out_conf, v_w_out, v_final_norm_w):
    wv = (c_ctx, w_mod, b_mod, norm_w, w_in, ssm_conv_w, ssm_conv_b, dt_bias, a_log, d_skip, ssm_norm_w, w_out_ssm,
          conf_conv_w, conf_conv_b, conf_ln_w, conf_ln_b, w_out_conf, w_out, final_norm_w)
    mv = (m_c_ctx, m_w_mod, m_b_mod, m_norm_w, m_w_in, m_ssm_conv_w, m_ssm_conv_b, m_dt_bias, m_a_log, m_d_skip,
          m_ssm_norm_w, m_w_out_ssm, m_conf_conv_w, m_conf_conv_b, m_conf_ln_w, m_conf_ln_b, m_w_out_conf, m_w_out,
          m_final_norm_w)
    vv = (v_c_ctx, v_w_mod, v_b_mod, v_norm_w, v_w_in, v_ssm_conv_w, v_ssm_conv_b, v_dt_bias, v_a_log, v_d_skip,
          v_ssm_norm_w, v_w_out_ssm, v_conf_conv_w, v_conf_conv_b, v_conf_ln_w, v_conf_ln_b, v_w_out_conf, v_w_out,
          v_final_norm_w)
    shapes = {n: a.shape for n, a in zip(WEIGHTS, wv)}

    def squeeze(d):
        return {n: (a if n in ("c_ctx", "final_norm_w") else a[0]) for n, a in d.items()}

    w, m, v = (squeeze(dict(zip(WEIGHTS, t))) for t in (wv, mv, vv))

    my_chip = 2 * lax.axis_index("x") + lax.axis_index("y")
    my_core = lax.axis_index("c")

    pw = _pack_shard(w)
    pwb, wib, psm = pw.astype(BF16), w["w_in"].astype(BF16), pw[O_SC:O_SC + SROWS]
    gm, gw, gs = _gather_weights([pwb, wib], psm)
    gm = lax.dynamic_update_slice(gm, pwb[None], (my_chip, 0, 0))
    gw = lax.dynamic_update_slice(gw, wib[None], (my_chip, 0, 0))
    gs = lax.dynamic_update_slice(gs, psm[None], (my_chip, 0, 0))
    full = dict(w)
    full.update(_unpack_gathered(gm, gw, gs))

    lsum, grad_x, g = _local_step(x[0], c, ctx[0], loss_target[0], full)
    loss = lax.psum(lsum, ("x", "y", "c"))

    cidx = my_core.astype(jnp.int32).reshape(1)
    mc = jnp.stack([my_chip, my_core]).astype(jnp.int32)
    gsrc = [_pack_full(g), g["w_in_shards"]]
    blocks = (RB, WB)
    sib = _swap_halves(gsrc)
    part = [_add_halves(cidx, a, b, rb, "add_halves_%d" % i) for i, (a, b, rb) in enumerate(zip(gsrc, sib, blocks))]
    far = _exchange_chips(part)
    red = [_add_chips(mc, a, b, f, rb, "add_chips_%d" % i) for i, (a, b, f, rb) in enumerate(zip(gsrc, sib, far, blocks))]
    got = _share_halves(red)
    g_pk, g_win = (jnp.concatenate([jnp.where(my_core == 0, r, o), jnp.where(my_core == 0, o, r)], axis=0)
                   for r, o in zip(red, got))
    small = [name for name, _ in SMALL]
    as_row = lambda d: [d[name].reshape(1, -1) for name in small]
    res_sm = _small_update(as_row(g), as_row(w), as_row(m), as_row(v))

    gr = {"w_mod": g_pk[O_MOD:O_OS].reshape(D, R_MOD), "w_in": g_win, "rest": g_pk[O_OS:PUSED]}
    wr, mr, vr = ({"w_mod": t["w_mod"], "w_in": t["w_in"], "rest": _pack_rest(t)} for t in (w, m, v))
    res = {k: _adamw(gr[k], wr[k], mr[k], vr[k], rb, "adamw_" + k)
           for k, rb in (("w_in", WB), ("w_mod", 512), ("rest", (PUSED - O_OS) // 2))}

    outs = []
    for i in range(4):
        pick = (lambda k: gr[k]) if i == 0 else (lambda k: res[k][i - 1])
        d = {"w_mod": pick("w_mod")[None], "w_in": pick("w_in")[None]}
        d.update(_unpack_rest(pick("rest")))
        d.update({name: a.reshape(shapes[name]) for name, a in zip(small, res_sm[i])})
        outs.extend(d[n] for n in WEIGHTS)
    return (loss, grad_x[None], *outs)
```

```python
import functools

import jax
import jax.numpy as jnp
from jax import lax
from jax.experimental import pallas as pl
from jax.experimental.pallas import tpu as pltpu

F32, BF16 = jnp.float32, jnp.bfloat16

D = 1024
DI = 2048
NH = 32
HP = 64
NG = 8
HPG = 4
NS = 128
Q = 128
GW = 64
CK = 31
SK = 4
CTX = 256
EPS = 1e-6
XBC = DI + 2 * NG * NS
SSDW = XBC + 128
RESTW = 7168
T = 256
VMEM_LIMIT = 56 * 1024 * 1024

ADAM_LR, ADAM_B1, ADAM_B2, ADAM_EPS, ADAM_WD, ADAM_STEP = 0.001, 0.9, 0.999, 1e-08, 0.01, 10


def _cp(sem):
    return pltpu.CompilerParams(dimension_semantics=sem, vmem_limit_bytes=VMEM_LIMIT)


def _sig(x):
    return jax.nn.sigmoid(x)


def _silu(x):
    return x * _sig(x)


def _dsilu(x):
    s = _sig(x)
    return s * (1.0 + x * (1.0 - s))


def _dot(a, b):
    return jnp.dot(a, b, preferred_element_type=F32)


def _dot_nt(a, b):
    return lax.dot_general(a, b, (((1,), (1,)), ((), ())), preferred_element_type=F32)


def _split3(x):
    h = x.astype(BF16)
    r = x - h.astype(F32)
    m = r.astype(BF16)
    l = (r - m.astype(F32)).astype(BF16)
    return h, m, l


def _dot3_l(sel, x):
    h, m, l = _split3(x)
    return _dot(sel, h) + _dot(sel, m) + _dot(sel, l)


def _dot3_r(x, sel):
    h, m, l = _split3(x)
    return _dot(h, sel) + _dot(m, sel) + _dot(l, sel)


def _split2(x):
    h = x.astype(BF16)
    return h, (x - h.astype(F32)).astype(BF16)


def _dot2_l(sel, x):
    h, l = _split2(x)
    return _dot(sel, h) + _dot(sel, l)


def _dot2_r(x, sel):
    h, l = _split2(x)
    return _dot(h, sel) + _dot(l, sel)


def _iota(shape, dim):
    return lax.broadcasted_iota(jnp.int32, shape, dim)


def _mm(a, b, dims, m, n, k, bm, bn, bk, out_dtype, name):
    nk = k // bk
    assert m % bm == 0 and n % bn == 0 and k % bk == 0, (name, m, n, k, bm, bn, bk)

    def prod(a_ref, b_ref):
        av = a_ref[...].astype(BF16)
        bv = b_ref[...].astype(BF16)
        if dims == "nn":
            return _dot(av, bv)
        if dims == "nt":
            return _dot_nt(av, bv)
        return lax.dot_general(av, bv, (((0,), (0,)), ((), ())), preferred_element_type=F32)

    def kern_one(a_ref, b_ref, o_ref):
        o_ref[...] = prod(a_ref, b_ref).astype(out_dtype)

    def kern_acc(a_ref, b_ref, o_ref, acc):
        kk = pl.program_id(2)

        @pl.when(kk == 0)
        def _():
            acc[...] = jnp.zeros_like(acc)

        acc[...] += prod(a_ref, b_ref)

        @pl.when(kk == nk - 1)
        def _():
            o_ref[...] = acc[...].astype(out_dtype)

    if dims == "nn":
        a_spec = pl.BlockSpec((bm, bk), lambda j, i, kk: (i, kk))
        b_spec = pl.BlockSpec((bk, bn), lambda j, i, kk: (kk, j))
    elif dims == "nt":
        a_spec = pl.BlockSpec((bm, bk), lambda j, i, kk: (i, kk))
        b_spec = pl.BlockSpec((bn, bk), lambda j, i, kk: (j, kk))
    else:
        a_spec = pl.BlockSpec((bk, bm), lambda j, i, kk: (kk, i))
        b_spec = pl.BlockSpec((bk, bn), lambda j, i, kk: (kk, j))
    return pl.pallas_call(
        kern_one if nk == 1 else kern_acc, name=name,
        grid=(n // bn, m // bm, nk),
        in_specs=[a_spec, b_spec],
        out_specs=pl.BlockSpec((bm, bn), lambda j, i, kk: (i, j)),
        out_shape=jax.ShapeDtypeStruct((m, n), out_dtype),
        scratch_shapes=[] if nk == 1 else [pltpu.VMEM((bm, bn), F32)],
        compiler_params=_cp(("parallel", "parallel", "arbitrary")),
    )(a, b)


def _mod_fwd(cc, w_mod, b_mod):
    def kern(cc_ref, w_ref, b_ref, o_ref):
        s = _silu(cc_ref[...]).astype(BF16)
        o_ref[...] = _dot(s, w_ref[...]) + b_ref[...]

    return pl.pallas_call(
        kern, name="mod_fwd", grid=(3,),
        in_specs=[pl.BlockSpec((8, D), lambda j: (0, 0)), pl.BlockSpec((D, D), lambda j: (0, j)),
                  pl.BlockSpec((1, D), lambda j: (0, j))],
        out_specs=pl.BlockSpec((8, D), lambda j: (0, j)),
        out_shape=jax.ShapeDtypeStruct((8, 3 * D), F32),
        compiler_params=_cp(("parallel",)),
    )(cc, w_mod, b_mod)


def _mod_bwd(dmod, cc, cct, w_mod):
    def kern(dm_ref, cc_ref, cct_ref, w_ref, gw_ref, gb_ref, gc_ref):
        kk = pl.program_id(0)
        dm = dm_ref[...]
        sct = _silu(cct_ref[...])
        gw_ref[...] = sct[:, 0:1] * dm[0:1, :] + sct[:, 1:2] * dm[1:2, :]
        gb_ref[...] = jnp.broadcast_to(dm[0:1, :] + dm[1:2, :], dm.shape)

        @pl.when(kk == 0)
        def _():
            gc_ref[...] = jnp.zeros_like(gc_ref)

        gc_ref[...] += _dot_nt(dm.astype(BF16), w_ref[...])

        @pl.when(kk == 2)
        def _():
            gc_ref[...] = gc_ref[...] * _dsilu(cc_ref[...])

    return pl.pallas_call(
        kern, name="mod_bwd", grid=(3,),
        in_specs=[pl.BlockSpec((8, D), lambda j: (0, j)), pl.BlockSpec((8, D), lambda j: (0, 0)),
                  pl.BlockSpec((D, 8), lambda j: (0, 0)), pl.BlockSpec((D, D), lambda j: (0, j))],
        out_specs=[pl.BlockSpec((D, D), lambda j: (0, j)), pl.BlockSpec((8, D), lambda j: (0, j)),
                   pl.BlockSpec((8, D), lambda j: (0, 0))],
        out_shape=[jax.ShapeDtypeStruct((D, 3 * D), F32), jax.ShapeDtypeStruct((8, 3 * D), F32),
                   jax.ShapeDtypeStruct((8, D), F32)],
        compiler_params=_cp(("arbitrary",)),
    )(dmod, cc, cct, w_mod)


def _ext_specs(nct):
    return (pl.BlockSpec((T, D), lambda i: (jnp.minimum(i, nct - 1), 0)),
            pl.BlockSpec((T, D), lambda i: (jnp.maximum(i - nct, 0), 0)))


def _norm_fwd(ctx, xl, mod, nw, nct):
    lext = ctx.shape[0] + xl.shape[0]

    def kern(c_ref, x_ref, mod_ref, nw_ref, h_ref):
        is_ctx = pl.program_id(0) < nct
        x = jnp.where(is_ctx, c_ref[...], x_ref[...])
        r = lax.rsqrt(jnp.mean(x * x, axis=-1, keepdims=True) + EPS)
        xn = x * r * nw_ref[...]
        shift = jnp.where(is_ctx, mod_ref[1:2, 0:D], mod_ref[0:1, 0:D])
        scale = jnp.where(is_ctx, mod_ref[1:2, D:2 * D], mod_ref[0:1, D:2 * D])
        h_ref[...] = (xn * (1.0 + scale) + shift).astype(BF16)

    return pl.pallas_call(
        kern, name="norm_fwd", grid=(lext // T,),
        in_specs=[*_ext_specs(nct), pl.BlockSpec((8, 3 * D), lambda i: (0, 0)),
                  pl.BlockSpec((1, D), lambda i: (0, 0))],
        out_specs=pl.BlockSpec((T, D), lambda i: (i, 0)),
        out_shape=jax.ShapeDtypeStruct((lext, D), BF16),
        compiler_params=_cp(("parallel",)),
    )(ctx, xl, mod, nw)


def _norm_bwd(dha, dhb, ctx, xl, dx2, mod, nw, nct):
    lext = ctx.shape[0] + xl.shape[0]
    ntl = lext // T

    def kern(dha_ref, dhb_ref, c_ref, x_ref, dx2_ref, mod_ref, nw_ref, gx_ref, gnw_ref, dss_ref):
        i = pl.program_id(0)
        is_ctx = i < nct

        @pl.when(i == 0)
        def _():
            gnw_ref[...] = jnp.zeros_like(gnw_ref)
            dss_ref[...] = jnp.zeros_like(dss_ref)

        x = jnp.where(is_ctx, c_ref[...], x_ref[...])
        dh_ = dha_ref[...].astype(F32) + jnp.where(is_ctx, 0.0, dhb_ref[...].astype(F32))
        nw_ = nw_ref[...]
        r = lax.rsqrt(jnp.mean(x * x, axis=-1, keepdims=True) + EPS)
        xn = x * r
        scale = jnp.where(is_ctx, mod_ref[1:2, D:2 * D], mod_ref[0:1, D:2 * D])
        dsh = jnp.sum(dh_, axis=0, keepdims=True)
        dsc = jnp.sum(dh_ * (xn * nw_), axis=0, keepdims=True)
        row = jnp.concatenate([dsh, dsc], axis=1)
        rid = _iota((8, 2 * D), 0)
        dss_ref[...] += jnp.where(rid == jnp.where(is_ctx, 1, 0), row, 0.0)
        dxnw = dh_ * (1.0 + scale)
        gnw_ref[...] += jnp.broadcast_to(jnp.sum(dxnw * xn, axis=0, keepdims=True), (8, D))
        dxn = dxnw * nw_
        dx = r * (dxn - xn * jnp.mean(dxn * xn, axis=-1, keepdims=True))
        gx_ref[...] = dx2_ref[...] + dx

    return pl.pallas_call(
        kern, name="norm_bwd", grid=(ntl,),
        in_specs=[pl.BlockSpec((T, D), lambda i: (i, 0)), pl.BlockSpec((T, D), lambda i: (jnp.maximum(i - nct, 0), 0)),
                  *_ext_specs(nct),
                  pl.BlockSpec((T, D), lambda i: (jnp.maximum(i - nct, 0), 0)),
                  pl.BlockSpec((8, 3 * D), lambda i: (0, 0)), pl.BlockSpec((1, D), lambda i: (0, 0))],
        out_specs=[pl.BlockSpec((T, D), lambda i: (jnp.maximum(i - nct, 0), 0)),
                   pl.BlockSpec((8, D), lambda i: (0, 0)), pl.BlockSpec((8, 2 * D), lambda i: (0, 0))],
        out_shape=[jax.ShapeDtypeStruct((lext - nct * T, D), F32), jax.ShapeDtypeStruct((8, D), F32),
                   jax.ShapeDtypeStruct((8, 2 * D), F32)],
        compiler_params=_cp(("arbitrary",)),
    )(dha, dhb, ctx, xl, dx2, mod, nw)


CB = 1024


def _halo_specs(width_blk, col_off_blocks, ntl):
    t8 = T // 8
    main = pl.BlockSpec((T, width_blk), lambda j, i: (i, j + col_off_blocks))
    prev = pl.BlockSpec((8, width_blk), lambda j, i: (jnp.maximum(i * t8 - 1, 0), j + col_off_blocks))
    nxt = pl.BlockSpec((8, width_blk), lambda j, i: (jnp.minimum((i + 1) * t8, ntl * t8 - 1), j + col_off_blocks))
    return main, prev, nxt


def _seq_edges(i, nct, ntl):
    starts = jnp.logical_or(i == 0, i == nct)
    ends = jnp.logical_or(i == nct - 1, i == ntl - 1)
    return starts, ends


def _shifted(ext, off):
    n = ext.shape[0]
    return pltpu.roll(ext, (-off) % n, axis=0)[8:8 + T]


def _conv_fwd(proj_ssd, cw, cb, nct):
    lext = proj_ssd.shape[0]
    ntl = lext // T

    def kern(u_ref, up_ref, un_ref, w_ref, b_ref, o_ref):
        i = pl.program_id(1)
        starts, ends = _seq_edges(i, nct, ntl)
        up = jnp.where(starts, 0.0, up_ref[...])
        un = jnp.where(ends, 0.0, un_ref[...])
        ext = jnp.concatenate([up, u_ref[...], un], axis=0)
        w = w_ref[...]
        pre = b_ref[...] + w[0:1] * _shifted(ext, -2) + w[1:2] * _shifted(ext, -1) \
            + w[2:3] * u_ref[...] + w[3:4] * _shifted(ext, 1)
        o_ref[...] = _silu(pre)

    cbf = 2 * CB
    main, prev, nxt = _halo_specs(cbf, 0, ntl)
    return pl.pallas_call(
        kern, name="conv_fwd", grid=(XBC // cbf, ntl),
        in_specs=[main, prev, nxt, pl.BlockSpec((8, cbf), lambda j, i: (0, j)), pl.BlockSpec((1, cbf), lambda j, i: (0, j))],
        out_specs=pl.BlockSpec((T, cbf), lambda j, i: (i, j)),
        out_shape=jax.ShapeDtypeStruct((lext, XBC), F32),
        compiler_params=_cp(("parallel", "parallel")),
    )(proj_ssd, proj_ssd, proj_ssd, cw, cb)


def _conv_bwd(dpost, proj_ssd, cw, cb, col_off, width, nct, name):
    lext = proj_ssd.shape[0]
    ntl = lext // T
    cob = col_off // CB

    def kern(u_ref, up_ref, un_ref, d_ref, dp_ref, dn_ref, w_ref, b_ref, du_ref, gw_ref, gb_ref):
        i = pl.program_id(1)

        @pl.when(i == 0)
        def _():
            gw_ref[...] = jnp.zeros_like(gw_ref)
            gb_ref[...] = jnp.zeros_like(gb_ref)

        starts, ends = _seq_edges(i, nct, ntl)
        ext = jnp.concatenate([jnp.where(starts, 0.0, up_ref[...]), u_ref[...], jnp.where(ends, 0.0, un_ref[...])], axis=0)
        dext = jnp.concatenate([jnp.where(starts, 0.0, dp_ref[...]), d_ref[...], jnp.where(ends, 0.0, dn_ref[...])], axis=0)
        w = w_ref[...]
        n = ext.shape[0]
        pre = b_ref[...] + w[0:1] * pltpu.roll(ext, 2, axis=0) + w[1:2] * pltpu.roll(ext, 1, axis=0) \
            + w[2:3] * ext + w[3:4] * pltpu.roll(ext, n - 1, axis=0)
        dpre = dext * _dsilu(pre)
        dm = dpre[8:8 + T]
        du = w[0:1] * _shifted(dpre, 2) + w[1:2] * _shifted(dpre, 1) + w[2:3] * dm + w[3:4] * _shifted(dpre, -1)
        du_ref[...] = du.astype(BF16)
        g0 = jnp.sum(dm * _shifted(ext, -2), axis=0, keepdims=True)
        g1 = jnp.sum(dm * _shifted(ext, -1), axis=0, keepdims=True)
        g2 = jnp.sum(dm * u_ref[...], axis=0, keepdims=True)
        g3 = jnp.sum(dm * _shifted(ext, 1), axis=0, keepdims=True)
        rid = _iota((8, CB), 0)
        gw_ref[...] += jnp.where(rid == 0, g0, jnp.where(rid == 1, g1, jnp.where(rid == 2, g2, jnp.where(rid == 3, g3, 0.0))))
        gb_ref[...] += jnp.broadcast_to(jnp.sum(dm, axis=0, keepdims=True), (8, CB))

    main, prev, nxt = _halo_specs(CB, cob, ntl)
    dmain, dprev, dnxt = _halo_specs(CB, 0, ntl)
    return pl.pallas_call(
        kern, name=name, grid=(width // CB, ntl),
        in_specs=[main, prev, nxt, dmain, dprev, dnxt,
                  pl.BlockSpec((8, CB), lambda j, i: (0, j + cob)), pl.BlockSpec((1, CB), lambda j, i: (0, j + cob))],
        out_specs=[pl.BlockSpec((T, CB), lambda j, i: (i, j)), pl.BlockSpec((8, CB), lambda j, i: (0, j)),
                   pl.BlockSpec((8, CB), lambda j, i: (0, j))],
        out_shape=[jax.ShapeDtypeStruct((lext, width), BF16), jax.ShapeDtypeStruct((8, width), F32),
                   jax.ShapeDtypeStruct((8, width), F32)],
        compiler_params=_cp(("parallel", "arbitrary")),
    )(proj_ssd, proj_ssd, proj_ssd, dpost, dpost, dpost, cw, cb)


def _tri(lower):
    r, c = _iota((Q, Q), 0), _iota((Q, Q), 1)
    return jnp.where((c <= r) if lower else (c >= r), 1.0, 0.0).astype(BF16)


def _is_bdir_lane(shape):
    ln = _iota(shape, len(shape) - 1)
    return jnp.logical_and(((ln >> 2) & 1) == 1, ln < 64)


def _dt_fwd(proj_ssd, dtb, av):
    lext = proj_ssd.shape[0]

    def kern(p_ref, b_ref, a_ref, dtg_ref, lag_ref, dtt_ref, lat_ref):
        lane = _iota((T, 128), 1)
        raw = p_ref[...] + b_ref[...]
        dt = jnp.where(lane < 64, jnp.maximum(raw, 0.0) + jnp.log1p(jnp.exp(-jnp.abs(raw))), 0.0)
        dta = dt * a_ref[...]
        tl, tu = _tri(True), _tri(False)
        isb = _is_bdir_lane((Q, 128))
        las = []
        for qq in range(T // Q):
            blk = dta[qq * Q:(qq + 1) * Q]
            las.append(jnp.where(isb, _dot3_l(tu, blk), _dot3_l(tl, blk)))
        la = jnp.concatenate(las, axis=0)
        for g in range(NG):
            sh = (128 - 8 * g) % 128
            dtg_ref[g] = jnp.where(lane < 8, pltpu.roll(dt, sh, axis=1) if sh else dt, 0.0)
            lag_ref[g] = jnp.where(lane < 8, pltpu.roll(la, sh, axis=1) if sh else la, 0.0)
        dtt_ref[...] = dt.T[0:64]
        lat_ref[...] = la.T[0:64]

    return pl.pallas_call(
        kern, name="dt_fwd", grid=(lext // T,),
        in_specs=[pl.BlockSpec((T, 128), lambda i: (i, XBC // 128)), pl.BlockSpec((1, 128), lambda i: (0, 0)),
                  pl.BlockSpec((1, 128), lambda i: (0, 0))],
        out_specs=[pl.BlockSpec((NG, T, 128), lambda i: (0, i, 0)), pl.BlockSpec((NG, T, 128), lambda i: (0, i, 0)),
                   pl.BlockSpec((64, T), lambda i: (0, i)), pl.BlockSpec((64, T), lambda i: (0, i))],
        out_shape=[jax.ShapeDtypeStruct((NG, lext, 128), F32), jax.ShapeDtypeStruct((NG, lext, 128), F32),
                   jax.ShapeDtypeStruct((64, lext), F32), jax.ShapeDtypeStruct((64, lext), F32)],
        compiler_params=_cp(("parallel",)),
    )(proj_ssd, dtb, av)


def _dt_bwd(ddtg, proj_ssd, dtb):
    lext = proj_ssd.shape[0]

    def kern(d_ref, p_ref, b_ref, o_ref, gb_ref):
        @pl.when(pl.program_id(0) == 0)
        def _():
            gb_ref[...] = jnp.zeros_like(gb_ref)

        acc = d_ref[0]
        for g in range(1, NG):
            acc = acc + pltpu.roll(d_ref[g], 8 * g, axis=1)
        draw = acc * _sig(p_ref[...] + b_ref[...])
        o_ref[...] = draw.astype(BF16)
        gb_ref[...] += jnp.broadcast_to(jnp.sum(draw, axis=0, keepdims=True), (8, 128))

    return pl.pallas_call(
        kern, name="dt_bwd", grid=(lext // T,),
        in_specs=[pl.BlockSpec((NG, T, 128), lambda i: (0, i, 0)), pl.BlockSpec((T, 128), lambda i: (i, XBC // 128)),
                  pl.BlockSpec((1, 128), lambda i: (0, 0))],
        out_specs=[pl.BlockSpec((T, 128), lambda i: (i, 0)), pl.BlockSpec((8, 128), lambda i: (0, 0))],
        out_shape=[jax.ShapeDtypeStruct((lext, 128), BF16), jax.ShapeDtypeStruct((8, 128), F32)],
        compiler_params=_cp(("arbitrary",)),
    )(ddtg, proj_ssd, dtb)


def _expand_sel(d):
    r, c = _iota((128, 256), 0), _iota((128, 256), 1)
    return jnp.where(r == 4 * d + (c >> 6), 1.0, 0.0).astype(BF16)


def _reduce_sel(d):
    r, c = _iota((256, 128), 0), _iota((256, 128), 1)
    return jnp.where(c == 4 * d + (r >> 6), 1.0, 0.0).astype(BF16)


def _chunk_of_bwd_dir(j, ncc, nc):
    return jnp.where(j < ncc, ncc - 1 - j, nc + ncc - 1 - j)


def _dir_terms(la, dt, d):
    lane = _iota(la.shape, 1)
    mine = jnp.logical_and(lane >= 4 * d, lane < 4 * d + 4)
    la = jnp.where(mine, la, 0.0)
    tot = la[Q - 1:Q] if d == 0 else la[0:1]
    wnd = jnp.exp(tot - la)
    return tot, wnd * jnp.where(mine, dt, 0.0), wnd


def _ssd_state(xbc, dtg, lag, ncc):
    lext = xbc.shape[0]
    nc = lext // Q

    def kern(xf_ref, bf_ref, dtf_ref, laf_ref, xb_ref, bb_ref, dtb_ref, lab_ref, hf_ref, hb_ref, sf, sb):
        @pl.when(pl.program_id(0) == 0)
        def _():
            sf[...] = jnp.zeros_like(sf)
            sb[...] = jnp.zeros_like(sb)

        for d, (x_ref, b_ref, dt_ref, la_ref, h_ref, s) in enumerate(
                ((xf_ref, bf_ref, dtf_ref, laf_ref, hf_ref, sf), (xb_ref, bb_ref, dtb_ref, lab_ref, hb_ref, sb))):
            h_ref[...] = s[...]
            ex = _expand_sel(d)
            for g in range(NG):
                cols = slice(256 * g, 256 * (g + 1))
                tot, w_end, _ = _dir_terms(la_ref[g], dt_ref[g], d)
                wexp = _dot2_r(w_end, ex)
                dexp = _dot2_r(jnp.broadcast_to(jnp.exp(tot), (8, 128)), ex)[0:1]
                xw = (x_ref[:, cols] * wexp).astype(BF16)
                s[:, cols] = s[:, cols] * dexp + _dot(b_ref[:, 128 * g:128 * (g + 1)].T.astype(BF16), xw)

    cb = functools.partial(_chunk_of_bwd_dir, ncc=ncc, nc=nc)
    sm = lambda f: pl.BlockSpec((NG, Q, 128), lambda j: (0, f(j), 0))
    one = lambda j: j
    return pl.pallas_call(
        kern, name="ssd_state", grid=(nc,),
        in_specs=[pl.BlockSpec((Q, DI), lambda j: (j, 0)), pl.BlockSpec((Q, NG * NS), lambda j: (j, 2)), sm(one), sm(one),
                  pl.BlockSpec((Q, DI), lambda j: (cb(j), 0)), pl.BlockSpec((Q, NG * NS), lambda j: (cb(j), 2)), sm(cb), sm(cb)],
        out_specs=[pl.BlockSpec((None, 128, DI), lambda j: (j, 0, 0)),
                   pl.BlockSpec((None, 128, DI), lambda j: (cb(j), 0, 0))],
        out_shape=[jax.ShapeDtypeStruct((nc, 128, DI), F32), jax.ShapeDtypeStruct((nc, 128, DI), F32)],
        scratch_shapes=[pltpu.VMEM((128, DI), F32), pltpu.VMEM((128, DI), F32)],
        compiler_params=_cp(("arbitrary",)),
    )(xbc, xbc, dtg, lag, xbc, xbc, dtg, lag)


def _ssd_out(xbc, dtg, lag, dtt, lat, htf, htb, ncc):
    lext = xbc.shape[0]
    nc = lext // Q
    ncx = nc - ncc

    gps = 4
    li, si = (lambda: _iota((Q, Q), 0)), (lambda: _iota((Q, Q), 1))

    def kern(x_ref, b_ref, c_ref, dtg_ref, lag_ref, dtt_ref, lat_ref, hf_ref, hb_ref, y_ref):
        lane = _iota((Q, 128), 1)
        masks = (li() >= si(), li() <= si())
        for gg in range(gps):
            cols = slice(256 * gg, 256 * (gg + 1))
            cm = c_ref[:, 128 * gg:128 * (gg + 1)]
            xb_ = x_ref[:, cols].astype(BF16)
            s_ = _dot_nt(cm.astype(BF16), b_ref[:, 128 * gg:128 * (gg + 1)].astype(BF16))
            la, dtt_, lat_ = lag_ref[gg], dtt_ref[8 * gg:8 * (gg + 1)], lat_ref[8 * gg:8 * (gg + 1)]
            elam = jnp.exp(la)
            yh = [jnp.zeros((Q, 128), F32), jnp.zeros((Q, 128), F32)]
            for d, h_ref in enumerate((hf_ref, hb_ref)):
                rhs = jnp.concatenate([xb_, h_ref[:, cols].astype(BF16)], axis=0)
                lhs = []
                for r in range(HPG):
                    j = 4 * d + r
                    lm = jnp.where(masks[d], jnp.exp(la[:, j:j + 1] - lat_[j:j + 1, :]), 0.0)
                    w = s_ * lm * dtt_[j:j + 1, :]
                    lhs.append(jnp.concatenate([w, cm * elam[:, j:j + 1]], axis=1).astype(BF16))
                for b in range(HPG // 2):
                    ypair = _dot(jnp.concatenate(lhs[2 * b:2 * b + 2], axis=0), rhs[:, 128 * b:128 * (b + 1)])
                    yh[b] = yh[b] + jnp.where(lane < 64, ypair[0:Q], ypair[Q:2 * Q])
            y_ref[:, cols] = jnp.concatenate(yh, axis=1).astype(BF16)

    nb = NG // gps
    sm = pl.BlockSpec((gps, Q, 128), lambda c, g: (g, c + ncc, 0))
    smt = pl.BlockSpec((8 * gps, Q), lambda c, g: (g, c + ncc))
    st3 = pl.BlockSpec((None, 128, 256 * gps), lambda c, g: (c + ncc, 0, g))
    return pl.pallas_call(
        kern, name="ssd_out", grid=(ncx, nb),
        in_specs=[pl.BlockSpec((Q, 256 * gps), lambda c, g: (c + ncc, g)),
                  pl.BlockSpec((Q, 128 * gps), lambda c, g: (c + ncc, 2 * nb + g)),
                  pl.BlockSpec((Q, 128 * gps), lambda c, g: (c + ncc, 3 * nb + g)), sm, sm, smt, smt, st3, st3],
        out_specs=pl.BlockSpec((Q, 256 * gps), lambda c, g: (c, g)),
        out_shape=jax.ShapeDtypeStruct((ncx * Q, DI), BF16),
        compiler_params=_cp(("parallel", "parallel")),
    )(xbc, xbc, xbc, dtg, lag, dtt, lat, htf, htb)


def _ssd_bwd_state(xbc, dy, lag, ncc):
    lext = xbc.shape[0]
    nc = lext // Q

    def kern(cf_ref, dyf_ref, laf_ref, cb_ref, dyb_ref, lab_ref, df_ref, db_ref, sf, sb):
        @pl.when(pl.program_id(0) == 0)
        def _():
            sf[...] = jnp.zeros_like(sf)
            sb[...] = jnp.zeros_like(sb)

        for d, (c_ref, dy_ref, la_ref, o_ref, s) in enumerate(
                ((cf_ref, dyf_ref, laf_ref, df_ref, sf), (cb_ref, dyb_ref, lab_ref, db_ref, sb))):
            o_ref[...] = s[...]
            ex = _expand_sel(d)
            for g in range(NG):
                cols = slice(256 * g, 256 * (g + 1))
                la = la_ref[g]
                tot = la[Q - 1:Q] if d == 0 else la[0:1]
                eexp = _dot2_r(jnp.exp(la), ex)
                dexp = _dot2_r(jnp.broadcast_to(jnp.exp(tot), (8, 128)), ex)[0:1]
                dye = (dy_ref[:, cols] * eexp).astype(BF16)
                s[:, cols] = s[:, cols] * dexp + _dot(c_ref[:, 128 * g:128 * (g + 1)].T.astype(BF16), dye)

    cf = lambda j: nc - 1 - j
    cb = lambda j: _chunk_of_bwd_dir(nc - 1 - j, ncc, nc)
    sm = lambda f: pl.BlockSpec((NG, Q, 128), lambda j: (0, f(j), 0))
    return pl.pallas_call(
        kern, name="ssd_bwd_state", grid=(nc,),
        in_specs=[pl.BlockSpec((Q, NG * NS), lambda j: (cf(j), 3)), pl.BlockSpec((Q, DI), lambda j: (cf(j), 0)), sm(cf),
                  pl.BlockSpec((Q, NG * NS), lambda j: (cb(j), 3)), pl.BlockSpec((Q, DI), lambda j: (cb(j), 0)), sm(cb)],
        out_specs=[pl.BlockSpec((None, 128, DI), lambda j: (cf(j), 0, 0)),
                   pl.BlockSpec((None, 128, DI), lambda j: (cb(j), 0, 0))],
        out_shape=[jax.ShapeDtypeStruct((nc, 128, DI), F32), jax.ShapeDtypeStruct((nc, 128, DI), F32)],
        scratch_shapes=[pltpu.VMEM((128, DI), F32), pltpu.VMEM((128, DI), F32)],
        compiler_params=_cp(("arbitrary",)),
    )(xbc, dy, lag, xbc, dy, lag)


def _ssd_bwd_out(xbc, dy, dsk, dtg, lag, dtt, lat, htf, htb, dhf, dhb, a_rows):
    lext = xbc.shape[0]
    nc = lext // Q

    gps = 1

    def kern(x_ref, b_ref, c_ref, dy_ref, sk_ref, dtg_ref, lag_ref, dtt_ref, lat_ref, hf_ref, hb_ref, df_ref, db_ref,
             a_ref, dx_ref, dbo_ref, dco_ref, ddt_ref, ga_ref):
        @pl.when(pl.program_id(1) == 0)
        def _():
            ga_ref[...] = jnp.zeros_like(ga_ref)

        for gg in range(gps):
            one_group(gg, x_ref, b_ref, c_ref, dy_ref, sk_ref, dtg_ref, lag_ref, dtt_ref, lat_ref, hf_ref, hb_ref, df_ref,
                      db_ref, a_ref, dx_ref, dbo_ref, dco_ref, ddt_ref, ga_ref)

    def one_group(gg, x_ref, b_ref, c_ref, dy_ref, sk_ref, dtg_ref, lag_ref, dtt_ref, lat_ref, hf_ref, hb_ref, df_ref,
                  db_ref, a_ref, dx_ref, dbo_ref, dco_ref, ddt_ref, ga_ref):
        g = pl.program_id(0) * gps + gg
        cols, cols128 = slice(256 * gg, 256 * (gg + 1)), slice(128 * gg, 128 * (gg + 1))
        x, bm, cm, dy_ = x_ref[:, cols], b_ref[:, cols128], c_ref[:, cols128], dy_ref[:, cols]
        xb_, bb_, cb_, dyb_ = x.astype(BF16), bm.astype(BF16), cm.astype(BF16), dy_.astype(BF16)
        st = _dot_nt(bb_, cb_)
        si, li = _iota((Q, Q), 0), _iota((Q, Q), 1)
        lane = _iota((Q, 256), 1)
        lane128 = _iota((Q, 128), 1)
        row128 = _iota((Q, 128), 0)
        sub = _iota((128, Q), 0)
        la, dt = lag_ref[gg], dtg_ref[gg]
        dtt_, lat_ = dtt_ref[8 * gg:8 * (gg + 1)], lat_ref[8 * gg:8 * (gg + 1)]
        elam = jnp.exp(la)
        dst = jnp.zeros((Q, Q), F32)
        dxh = [jnp.zeros((Q, 128), F32), jnp.zeros((Q, 128), F32)]
        cdir, clam = jnp.zeros((Q, 128), F32), jnp.zeros((Q, 128), F32)
        dba = jnp.zeros((Q, 128), F32)
        dca = jnp.zeros((Q, 128), F32)
        dlam = jnp.zeros((Q, 128), F32)
        ddir = jnp.zeros((Q, 128), F32)
        rows = jnp.zeros((128, Q), F32)
        for d, (h_ref, dh_ref) in enumerate(((hf_ref, df_ref), (hb_ref, db_ref))):
            ht, dht = h_ref[:, cols], dh_ref[:, cols]
            htb_, dhtb_ = ht.astype(BF16), dht.astype(BF16)
            tot, w_end, wnd = _dir_terms(la, dt, d)
            ex, rs = _expand_sel(d), _reduce_sel(d)
            elx = _dot2_r(elam, ex)
            wex = _dot2_r(w_end, ex)
            dye = dy_ * elx
            ch = _dot(cb_, htb_)
            bd = _dot(bb_, dhtb_)
            dca = dca + _dot_nt(dye.astype(BF16), htb_)
            dba = dba + _dot_nt((x * wex).astype(BF16), dhtb_)
            dlam = dlam + _dot2_r(dye * ch, rs)
            xbd = _dot2_r(x * bd, rs)
            e_ = w_end * xbd
            dlam = dlam - e_
            ddir = ddir + wnd * xbd
            hh = _dot2_r(jnp.broadcast_to(jnp.sum(dht * ht, axis=0, keepdims=True), (8, 256)), rs)[0:1]
            tot_term = jnp.sum(e_, axis=0, keepdims=True) + jnp.exp(tot) * hh
            dlam = dlam + jnp.where(row128 == (Q - 1 if d == 0 else 0), tot_term, 0.0)
            rhs = jnp.concatenate([dyb_, dhtb_], axis=0)
            maskt = (li >= si) if d == 0 else (li <= si)
            for r in range(HPG):
                j = 4 * d + r
                half = slice(128 * (r // 2), 128 * (r // 2 + 1))
                hm = (lane128 >> 6) == (r % 2)
                dc = dt[:, j:j + 1]
                lmt = jnp.where(maskt, jnp.exp(lat_[j:j + 1, :] - la[:, j:j + 1]), 0.0)
                ldc = lmt * jnp.broadcast_to(dc, (Q, Q))
                lhs = jnp.concatenate([st * ldc, bm * w_end[:, j:j + 1]], axis=1).astype(BF16)
                dxh[r // 2] = dxh[r // 2] + jnp.where(hm, _dot(lhs, rhs[:, half]), 0.0)
                dwt = _dot_nt(jnp.where(hm, x[:, half], 0.0).astype(BF16), dyb_[:, half])
                q = dwt * st
                cs = jnp.sum(q * lmt, axis=1, keepdims=True)
                cdir = jnp.where(lane128 == j, cs, cdir)
                clam = jnp.where(lane128 == j, cs * dc, clam)
                rows = rows + jnp.where(sub == j, jnp.sum(q * ldc, axis=0, keepdims=True), 0.0)
                dst = dst + dwt * ldc
        dxa = jnp.concatenate(dxh, axis=1)
        ddir = ddir + cdir
        dlam = dlam - clam + rows.T
        dba = dba + _dot(dst.astype(BF16), cb_)
        dca = dca + _dot(dst.T.astype(BF16), bb_)
        isb = jnp.logical_and(lane128 >= 4, lane128 < 8)
        ddel = jnp.where(isb, _dot2_l(_tri(True), dlam), _dot2_l(_tri(False), dlam))
        a_l = a_ref[pl.ds(g, 1), :]
        ddt_ref[gg] = ddir + a_l * ddel
        ga_ref[gg] += jnp.broadcast_to(a_l * jnp.sum(dt * ddel, axis=0, keepdims=True), (8, 128))
        dx_ref[:, cols] = dxa + dy_ * sk_ref[:, cols]
        dbo_ref[:, cols128] = dba
        dco_ref[:, cols128] = dca

    nb = NG // gps
    st3 = pl.BlockSpec((None, 128, 256 * gps), lambda g, c: (c, 0, g))
    sm = pl.BlockSpec((gps, Q, 128), lambda g, c: (g, c, 0))
    smt = pl.BlockSpec((8 * gps, Q), lambda g, c: (g, c))
    wide = pl.BlockSpec((Q, 256 * gps), lambda g, c: (c, g))
    return pl.pallas_call(
        kern, name="ssd_bwd_out", grid=(nb, nc),
        in_specs=[wide, pl.BlockSpec((Q, 128 * gps), lambda g, c: (c, 2 * nb + g)),
                  pl.BlockSpec((Q, 128 * gps), lambda g, c: (c, 3 * nb + g)), wide,
                  pl.BlockSpec((1, 256 * gps), lambda g, c: (0, g)), sm, sm, smt, smt, st3, st3, st3, st3,
                  pl.BlockSpec((8, 128), lambda g, c: (0, 0))],
        out_specs=[wide, pl.BlockSpec((Q, 128 * gps), lambda g, c: (c, g)),
                   pl.BlockSpec((Q, 128 * gps), lambda g, c: (c, g)), sm, pl.BlockSpec((gps, 8, 128), lambda g, c: (g, 0, 0))],
        out_shape=[jax.ShapeDtypeStruct((lext, DI), F32), jax.ShapeDtypeStruct((lext, NG * NS), F32),
                   jax.ShapeDtypeStruct((lext, NG * NS), F32), jax.ShapeDtypeStruct((NG, lext, 128), F32),
                   jax.ShapeDtypeStruct((NG, 8, 128), F32)],
        compiler_params=_cp(("parallel", "arbitrary")),
    )(xbc, xbc, xbc, dy, dsk, dtg, lag, dtt, lat, htf, htb, dhf, dhb, a_rows)


def _post_fwd(yssm, xbc, proj_rest, dsk, gnw, nct):
    l = yssm.shape[0]

    def kern(y_ref, x_ref, z_ref, dsk_ref, w_ref, o_ref):
        y = y_ref[...].astype(F32) + dsk_ref[...] * x_ref[...]
        yz = y * _silu(z_ref[...].astype(F32))
        for g in range(NG):
            sl = slice(256 * g, 256 * (g + 1))
            v = yz[:, sl]
            r = lax.rsqrt(jnp.mean(v * v, axis=-1, keepdims=True) + EPS)
            o_ref[:, sl] = (v * r * w_ref[:, sl]).astype(BF16)

    return pl.pallas_call(
        kern, name="post_fwd", grid=(l // T,),
        in_specs=[pl.BlockSpec((T, DI), lambda i: (i, 0)), pl.BlockSpec((T, DI), lambda i: (i + nct, 0)),
                  pl.BlockSpec((T, DI), lambda i: (i, 0)), pl.BlockSpec((1, DI), lambda i: (0, 0)),
                  pl.BlockSpec((1, DI), lambda i: (0, 0))],
        out_specs=pl.BlockSpec((T, DI), lambda i: (i, 0)),
        out_shape=jax.ShapeDtypeStruct((l, DI), BF16),
        compiler_params=_cp(("parallel",)),
    )(yssm, xbc, proj_rest, dsk, gnw)


def _post_bwd(dgn, yssm, xbc, proj_rest, dsk, gnw, dpr, nct):
    l = yssm.shape[0]
    lext = xbc.shape[0]
    xi = lambda i: (jnp.maximum(i - nct, 0), 0)

    def kern(dg_ref, y_ref, x_ref, z_ref, dsk_ref, w_ref, dpr_ref, dy_ref, dz_ref, gw_ref, gd_ref):
        i = pl.program_id(0)

        @pl.when(i == 0)
        def _():
            gw_ref[...] = jnp.zeros_like(gw_ref)
            gd_ref[...] = jnp.zeros_like(gd_ref)

        @pl.when(i < nct)
        def _():
            dy_ref[...] = jnp.zeros_like(dy_ref)

        @pl.when(i >= nct)
        def _():
            xs = x_ref[...]
            z = z_ref[...].astype(F32)
            y = y_ref[...].astype(F32) + dsk_ref[...] * xs
            sz = _silu(z)
            yz = y * sz
            dgn_ = dg_ref[...].astype(F32)
            dyz_parts = []
            gws = []
            for g in range(NG):
                sl = slice(256 * g, 256 * (g + 1))
                v = yz[:, sl]
                r = lax.rsqrt(jnp.mean(v * v, axis=-1, keepdims=True) + EPS)
                vn = v * r
                dn = dgn_[:, sl] * w_ref[:, sl]
                gws.append(jnp.sum(dgn_[:, sl] * vn, axis=0, keepdims=True))
                dyz_parts.append(r * (dn - vn * jnp.mean(dn * vn, axis=-1, keepdims=True)))
            dyz = jnp.concatenate(dyz_parts, axis=1)
            gw_ref[...] += jnp.broadcast_to(jnp.concatenate(gws, axis=1), (8, DI))
            dy = dyz * sz
            dz_ref[...] = (dyz * y * _dsilu(z)).astype(BF16)
            gd_ref[...] += jnp.broadcast_to(jnp.sum(dy * xs, axis=0, keepdims=True), (8, DI))
            dy_ref[...] = dy

    return pl.pallas_call(
        kern, name="post_bwd", grid=(lext // T,),
        in_specs=[pl.BlockSpec((T, DI), xi), pl.BlockSpec((T, DI), xi), pl.BlockSpec((T, DI), lambda i: (i, 0)),
                  pl.BlockSpec((T, DI), xi), pl.BlockSpec((1, DI), lambda i: (0, 0)), pl.BlockSpec((1, DI), lambda i: (0, 0)),
                  pl.BlockSpec(memory_space=pl.ANY)],
        out_specs=[pl.BlockSpec((T, DI), lambda i: (i, 0)),
                   pl.BlockSpec((T, DI), xi), pl.BlockSpec((8, DI), lambda i: (0, 0)), pl.BlockSpec((8, DI), lambda i: (0, 0))],
        out_shape=[jax.ShapeDtypeStruct((lext, DI), F32),
                   jax.ShapeDtypeStruct((l, RESTW), BF16), jax.ShapeDtypeStruct((8, DI), F32), jax.ShapeDtypeStruct((8, DI), F32)],
        input_output_aliases={6: 1},
        compiler_params=_cp(("arbitrary",)),
    )(dgn, yssm, xbc, proj_rest, dsk, gnw, dpr)


C_G1, C_G2, C_GA, C_GB, C_CG = 2, 3, 4, 5, 6
PITCH = GW + 16
NROW = T // GW


GAP = PITCH - GW
PADR = GAP + NROW * PITCH
NSTRIP = D // 128


def _fill_padded(pad8, val):
    z = jnp.zeros((GAP, D), F32)
    parts = [z]
    for r in range(NROW):
        parts += [val[GW * r:GW * (r + 1)], z]
    p = jnp.concatenate(parts, axis=0)
    pad8[0] = p
    for j in range(1, pad8.shape[0]):
        pad8[j] = pltpu.roll(p, PADR - j, axis=0)


def _tap(pad8, base, off, ln):
    return pad8[off % 8, pl.ds(base + off - off % 8, GW), ln]


def _row_conv(out_ref, pad8, w_ref, transpose):
    def strip(s, carry):
        ln = pl.ds(pl.multiple_of(s * 128, 128), 128)
        for r in range(NROW):
            base = GAP + PITCH * r
            acc = jnp.zeros((GW, 128), F32)
            for k in range(CK):
                off = (k - 15) if not transpose else (15 - k)
                acc = acc + w_ref[pl.ds(k, 1), ln] * _tap(pad8, base, off, ln)
            out_ref[pl.ds(GW * r, GW), ln] = acc
        return carry

    lax.fori_loop(0, NSTRIP, strip, 0)


def _row_conv_wgrad(gcw_ref, padd8, pada8):
    def strip(s, carry):
        ln = pl.ds(pl.multiple_of(s * 128, 128), 128)
        rid = _iota((32, 128), 0)
        g = jnp.zeros((32, 128), F32)
        for k0 in range(0, CK, 8):
            taps = range(k0, min(k0 + 8, CK))
            accs = {k: jnp.zeros((8, 128), F32) for k in taps}
            for r in range(NROW):
                base = GAP + PITCH * r
                d = _tap(padd8, base, 0, ln)
                for k in taps:
                    p = d * pada8[0, pl.ds(base + k - 15, GW), ln]
                    part = p[0:8]
                    for q in range(1, GW // 8):
                        part = part + p[8 * q:8 * (q + 1)]
                    accs[k] = accs[k] + part
            for k in taps:
                g = jnp.where(rid == k, jnp.sum(accs[k], axis=0, keepdims=True), g)
        gcw_ref[:, ln] += g
        return carry

    lax.fori_loop(0, NSTRIP, strip, 0)


def _ln_stats(cv):
    mu = jnp.mean(cv, axis=-1, keepdims=True)
    xc = cv - mu
    rs = lax.rsqrt(jnp.mean(xc * xc, axis=-1, keepdims=True) + EPS)
    return xc * rs, rs


def _conf_fwd(proj_rest, cw, cb, lw, lb):
    l = proj_rest.shape[0]

    def kern(ga_ref, gb_ref, cg_ref, cw_ref, cb_ref, lw_ref, lb_ref, o_ref, cv_ref, pad8):
        _fill_padded(pad8, ga_ref[...].astype(F32) * _sig(gb_ref[...].astype(F32)))
        _row_conv(cv_ref, pad8, cw_ref, False)
        cv = cv_ref[...] + cb_ref[...]
        cv_ref[...] = cv
        xh, _ = _ln_stats(cv)
        ln = xh * lw_ref[...] + lb_ref[...]
        o_ref[...] = (_silu(ln) * _silu(cg_ref[...].astype(F32))).astype(BF16)

    vec = pl.BlockSpec((1, D), lambda i: (0, 0))
    blk = pl.BlockSpec((T, D), lambda i: (i, 0))
    return pl.pallas_call(
        kern, name="conf_fwd", grid=(l // T,),
        in_specs=[pl.BlockSpec((T, D), lambda i: (i, C_GA)), pl.BlockSpec((T, D), lambda i: (i, C_GB)),
                  pl.BlockSpec((T, D), lambda i: (i, C_CG)), pl.BlockSpec((32, D), lambda i: (0, 0)), vec, vec, vec],
        out_specs=[blk, blk],
        out_shape=[jax.ShapeDtypeStruct((l, D), BF16), jax.ShapeDtypeStruct((l, D), F32)],
        scratch_shapes=[pltpu.VMEM((8, PADR, D), F32)],
        compiler_params=_cp(("parallel",)),
    )(proj_rest, proj_rest, proj_rest, cw, cb, lw, lb)


def _conf_bwd(duc, cv, proj_rest, cw, lw, lb, dpr):
    l = proj_rest.shape[0]

    def kern(du_ref, cv_ref, ga_ref, gb_ref, cg_ref, cw_ref, lw_ref, lb_ref, dpr_ref, o_ref, gcw_ref, gv_ref, sc,
             pada, padd, da_ref):
        i, j = pl.program_id(0), pl.program_id(1)

        @pl.when(jnp.logical_and(i == 0, j == 0))
        def _():
            gcw_ref[...] = jnp.zeros_like(gcw_ref)
            gv_ref[...] = jnp.zeros_like(gv_ref)

        @pl.when(j == 0)
        def _():
            ga, gb, cg = ga_ref[...].astype(F32), gb_ref[...].astype(F32), cg_ref[...].astype(F32)
            sg = _sig(gb)
            xh, rs = _ln_stats(cv_ref[...])
            ln = xh * lw_ref[...] + lb_ref[...]
            du = du_ref[...].astype(F32)
            sc[:, 2 * D:3 * D] = (du * _silu(ln) * _dsilu(cg)).astype(BF16)
            dln = du * _silu(cg) * _dsilu(ln)
            g_lw = jnp.sum(dln * xh, axis=0, keepdims=True)
            g_lb = jnp.sum(dln, axis=0, keepdims=True)
            dxh = dln * lw_ref[...]
            dcv = rs * (dxh - jnp.mean(dxh, axis=-1, keepdims=True) - xh * jnp.mean(dxh * xh, axis=-1, keepdims=True))
            g_cb = jnp.sum(dcv, axis=0, keepdims=True)
            rid = _iota((8, D), 0)
            gv_ref[...] += jnp.where(rid == 0, g_cb, jnp.where(rid == 1, g_lw, jnp.where(rid == 2, g_lb, 0.0)))
            _fill_padded(padd, dcv)
            _fill_padded(pada, ga * sg)
            _row_conv(da_ref, padd, cw_ref, True)
            _row_conv_wgrad(gcw_ref, padd, pada)
            da = da_ref[...]
            sc[:, 0:D] = (da * sg).astype(BF16)
            sc[:, D:2 * D] = (da * ga * sg * (1.0 - sg)).astype(BF16)

        o_ref[...] = sc[:, pl.ds(pl.multiple_of(j * D, 128), D)]

    vec = pl.BlockSpec((1, D), lambda i, j: (0, 0))
    col = lambda c: pl.BlockSpec((T, D), lambda i, j: (i, c))
    return pl.pallas_call(
        kern, name="conf_bwd", grid=(l // T, 3),
        in_specs=[col(0), col(0), col(C_GA), col(C_GB), col(C_CG), pl.BlockSpec((32, D), lambda i, j: (0, 0)), vec, vec,
                  pl.BlockSpec(memory_space=pl.ANY)],
        out_specs=[pl.BlockSpec((T, D), lambda i, j: (i, C_GA + j)), pl.BlockSpec((32, D), lambda i, j: (0, 0)),
                   pl.BlockSpec((8, D), lambda i, j: (0, 0))],
        out_shape=[jax.ShapeDtypeStruct((l, RESTW), BF16), jax.ShapeDtypeStruct((32, D), F32),
                   jax.ShapeDtypeStruct((8, D), F32)],
        scratch_shapes=[pltpu.VMEM((T, 3 * D), BF16), pltpu.VMEM((1, PADR, D), F32), pltpu.VMEM((8, PADR, D), F32),
                        pltpu.VMEM((T, D), F32)],
        input_output_aliases={8: 0},
        compiler_params=_cp(("arbitrary", "arbitrary")),
    )(duc, cv, proj_rest, proj_rest, proj_rest, cw, lw, lb, dpr)


def _merge_fwd(bs, bc, proj_rest):
    l = bs.shape[0]

    def kern(bs_ref, bc_ref, g1_ref, g2_ref, o_ref):
        up = lambda r: r[...].astype(F32)
        o_ref[...] = (_sig(up(g1_ref)) * up(bs_ref) + _sig(up(g2_ref)) * up(bc_ref)).astype(BF16)

    blk = pl.BlockSpec((T, D), lambda i: (i, 0))
    return pl.pallas_call(
        kern, name="merge_fwd", grid=(l // T,),
        in_specs=[blk, blk, pl.BlockSpec((T, D), lambda i: (i, C_G1)), pl.BlockSpec((T, D), lambda i: (i, C_G2))],
        out_specs=blk, out_shape=jax.ShapeDtypeStruct((l, D), BF16),
        compiler_params=_cp(("parallel",)),
    )(bs, bc, proj_rest, proj_rest)


def _merge_bwd(dm, bs, bc, proj_rest):
    l = bs.shape[0]

    def kern(dm_ref, bs_ref, bc_ref, g1_ref, g2_ref, dbs_ref, dbc_ref, dg_ref):
        up = lambda r: r[...].astype(F32)
        dm_ = up(dm_ref)
        s1, s2 = _sig(up(g1_ref)), _sig(up(g2_ref))
        dbs_ref[...] = (dm_ * s1).astype(BF16)
        dbc_ref[...] = (dm_ * s2).astype(BF16)
        dg_ref[:, 0:D] = (dm_ * up(bs_ref) * s1 * (1.0 - s1)).astype(BF16)
        dg_ref[:, D:2 * D] = (dm_ * up(bc_ref) * s2 * (1.0 - s2)).astype(BF16)

    blk = pl.BlockSpec((T, D), lambda i: (i, 0))
    return pl.pallas_call(
        kern, name="merge_bwd", grid=(l // T,),
        in_specs=[blk, blk, blk, pl.BlockSpec((T, D), lambda i: (i, C_G1)), pl.BlockSpec((T, D), lambda i: (i, C_G2))],
        out_specs=[blk, blk, pl.BlockSpec((T, 2 * D), lambda i: (i, 1))],
        out_shape=[jax.ShapeDtypeStruct((l, D), BF16), jax.ShapeDtypeStruct((l, D), BF16),
                   jax.ShapeDtypeStruct((l, RESTW), BF16)],
        compiler_params=_cp(("parallel",)),
    )(dm, bs, bc, proj_rest, proj_rest)


def _final(x, out, tgt, mod, fw):
    l = x.shape[0]

    def kern(x_ref, o_ref, t_ref, mod_ref, fw_ref, ls_ref, dx2_ref, do_ref, gv_ref):
        @pl.when(pl.program_id(0) == 0)
        def _():
            ls_ref[...] = jnp.zeros_like(ls_ref)
            gv_ref[...] = jnp.zeros_like(gv_ref)

        gate = mod_ref[0:1, 2 * D:3 * D]
        o = o_ref[...]
        x2 = x_ref[...] + gate * o
        r = lax.rsqrt(jnp.mean(x2 * x2, axis=-1, keepdims=True) + EPS)
        yn = x2 * r
        fw_ = fw_ref[...]
        e = yn * fw_ - t_ref[...]
        ls_ref[...] += jnp.full((8, 128), 1.0, F32) * (0.5 / D) * jnp.sum(e * e)
        dy = e * (1.0 / D)
        g_fw = jnp.sum(dy * yn, axis=0, keepdims=True)
        dyn = dy * fw_
        dx2 = r * (dyn - yn * jnp.mean(dyn * yn, axis=-1, keepdims=True))
        g_gate = jnp.sum(dx2 * o, axis=0, keepdims=True)
        rid = _iota((8, D), 0)
        gv_ref[...] += jnp.where(rid == 0, g_fw, jnp.where(rid == 1, g_gate, 0.0))
        dx2_ref[...] = dx2
        do_ref[...] = (dx2 * gate).astype(BF16)

    blk = pl.BlockSpec((T, D), lambda i: (i, 0))
    return pl.pallas_call(
        kern, name="final", grid=(l // T,),
        in_specs=[blk, blk, blk, pl.BlockSpec((8, 3 * D), lambda i: (0, 0)), pl.BlockSpec((1, D), lambda i: (0, 0))],
        out_specs=[pl.BlockSpec((8, 128), lambda i: (0, 0)), blk, blk, pl.BlockSpec((8, D), lambda i: (0, 0))],
        out_shape=[jax.ShapeDtypeStruct((8, 128), F32), jax.ShapeDtypeStruct((l, D), F32),
                   jax.ShapeDtypeStruct((l, D), BF16), jax.ShapeDtypeStruct((8, D), F32)],
        compiler_params=_cp(("arbitrary",)),
    )(x, out, tgt, mod, fw)


def _perm_dt_cols(w):
    s = w.shape[:-1]
    return w.reshape(*s, 2, NG, HPG).swapaxes(-3, -2).reshape(*s, 64)


def _unperm_dt_cols(w):
    s = w.shape[:-1]
    return w.reshape(*s, NG, 2, HPG).swapaxes(-3, -2).reshape(*s, 64)


def _pad_lanes(v, width):
    return jnp.pad(v, ((0, 0), (0, width - v.shape[1])))


def _vcols(segs, a, b):
    parts, off = [], 0
    for s in segs:
        lo, hi = max(a, off), min(b, off + s.shape[1])
        if lo < hi:
            parts.append(s[:, lo - off:hi - off])
        off += s.shape[1]
    return parts[0] if len(parts) == 1 else jnp.concatenate(parts, axis=1)


def _local_step(x, c, ctx, tgt, w):
    l = x.shape[0]
    nct = CTX // T
    ncc = CTX // Q
    lext = l + CTX

    w_mod = w["w_mod"].astype(BF16)
    wsegs = [s.astype(BF16) for s in (w["w_in"] if isinstance(w["w_in"], (list, tuple)) else [w["w_in"]])]
    w_ssd = jnp.concatenate([_vcols(wsegs, 0, XBC), _perm_dt_cols(_vcols(wsegs, XBC, XBC + 64)), jnp.zeros((D, 64), BF16)], axis=1)
    r0 = XBC + 64
    w_rest = jnp.concatenate([_vcols(wsegs, r0, r0 + DI), _vcols(wsegs, r0 + DI + 3 * D, r0 + RESTW),
                              _vcols(wsegs, r0 + DI, r0 + DI + 3 * D)], axis=1)
    w_os, w_oc, w_o = w["w_out_ssm"].astype(BF16), w["w_out_conf"].astype(BF16), w["w_out"].astype(BF16)
    cw8 = jnp.pad(w["ssm_conv_w"], ((0, 4), (0, 0)))
    cb_s = w["ssm_conv_b"].reshape(1, XBC)
    dtb = _pad_lanes(_perm_dt_cols(w["dt_bias"].reshape(1, 64)), 128)
    a_all = -jnp.exp(w["a_log"].reshape(1, 64))
    a_perm = _pad_lanes(_perm_dt_cols(a_all), 128)
    a_rows = _pad_lanes(_perm_dt_cols(a_all).reshape(NG, 8), 128)
    dsk = jnp.repeat(w["d_skip"].reshape(NH), HP).reshape(1, DI)
    gnw = w["ssm_norm_w"].reshape(1, DI)
    ccw = jnp.pad(w["conf_conv_w"], ((0, 1), (0, 0)))
    ccb, clw, clb = w["conf_conv_b"].reshape(1, D), w["conf_ln_w"].reshape(1, D), w["conf_ln_b"].reshape(1, D)
    nw = w["norm_w"].reshape(1, D)
    fw = w["final_norm_w"].reshape(1, D)
    cc = jnp.concatenate([c.reshape(1, D), w["c_ctx"].reshape(1, D), jnp.zeros((6, D), F32)], axis=0)

    bx = min(1024, l)
    be = 768 if lext % 768 == 0 else 256
    tk = min(1024, l)
    mod = _mod_fwd(cc, w_mod, w["b_mod"].reshape(1, 3 * D))
    h = _norm_fwd(ctx, x, mod, nw, nct)
    hx = h[CTX:]
    proj_ssd = _mm(h, w_ssd, "nn", lext, SSDW, D, be, SSDW // 3, D, F32, "proj_ssd")
    proj_rest = _mm(hx, w_rest, "nn", l, RESTW, D, bx, 1024, D, BF16, "proj_rest")
    xbc = _conv_fwd(proj_ssd, cw8, cb_s, nct)
    dtg, lag, dtt, lat = _dt_fwd(proj_ssd, dtb, a_perm)
    htf, htb = _ssd_state(xbc, dtg, lag, ncc)
    yssm = _ssd_out(xbc, dtg, lag, dtt, lat, htf, htb, ncc)
    gn = _post_fwd(yssm, xbc, proj_rest, dsk, gnw, nct)
    bs = _mm(gn, w_os, "nn", l, D, DI, bx, D, DI, BF16, "out_ssm")
    uc, cv = _conf_fwd(proj_rest, ccw, ccb, clw, clb)
    bc = _mm(uc, w_oc, "nn", l, D, D, bx, D, D, BF16, "out_conf")
    merged = _merge_fwd(bs, bc, proj_rest)
    out = _mm(merged, w_o, "nn", l, D, D, bx, D, D, F32, "out_proj")
    lsum, dx2, dout, gv_fin = _final(x, out, tgt, mod, fw)

    g = {}
    g["final_norm_w"] = gv_fin[0]
    dmerged = _mm(dout, w_o, "nt", l, D, D, bx, D, D, BF16, "d_merged")
    g["w_out"] = _mm(merged, dout, "tn", D, D, l, D, D, tk, F32, "g_w_out")
    dbs, dbc, dpr = _merge_bwd(dmerged, bs, bc, proj_rest)
    dgn = _mm(dbs, w_os, "nt", l, DI, D, bx, DI, D, BF16, "d_gn")
    g["w_out_ssm"] = _mm(gn, dbs, "tn", DI, D, l, DI, D, tk, F32, "g_w_out_ssm")
    duc = _mm(dbc, w_oc, "nt", l, D, D, bx, D, D, BF16, "d_uc")
    g["w_out_conf"] = _mm(uc, dbc, "tn", D, D, l, D, D, tk, F32, "g_w_out_conf")
    dpr, gcw, gv_conf = _conf_bwd(duc, cv, proj_rest, ccw, clw, clb, dpr)
    g["conf_conv_w"] = gcw[:CK]
    g["conf_conv_b"], g["conf_ln_w"], g["conf_ln_b"] = gv_conf[0], gv_conf[1], gv_conf[2]
    dy, dproj_rest, ggnw, gdsk = _post_bwd(dgn, yssm, xbc, proj_rest, dsk, gnw, dpr, nct)
    g["ssm_norm_w"] = ggnw[0]
    g["d_skip"] = gdsk[0].reshape(NH, HP).sum(axis=1)
    dhf, dhb = _ssd_bwd_state(xbc, dy, lag, ncc)
    dxs, dbm, dcm, ddtg, galog = _ssd_bwd_out(xbc, dy, dsk, dtg, lag, dtt, lat, htf, htb, dhf, dhb, a_rows)
    g["a_log"] = _unperm_dt_cols(galog[:, 0, 0:8].reshape(1, 64)).reshape(2, NH)
    dus, gws, gbs = [], [], []
    for dpost, off, width, nm in ((dxs, 0, DI, "conv_bwd_x"), (dbm, DI, NG * NS, "conv_bwd_b"), (dcm, DI + NG * NS, NG * NS, "conv_bwd_c")):
        du_, gw_, gb_ = _conv_bwd(dpost, proj_ssd, cw8, cb_s, off, width, nct, nm)
        dus.append(du_)
        gws.append(gw_[:SK])
        gbs.append(gb_[0])
    g["ssm_conv_w"] = jnp.concatenate(gws, axis=1)
    g["ssm_conv_b"] = jnp.concatenate(gbs, axis=0)
    ddt_raw, gdtb = _dt_bwd(ddtg, proj_ssd, dtb)
    g["dt_bias"] = _unperm_dt_cols(gdtb[0:1, 0:64]).reshape(2, NH)
    dproj_ssd = jnp.concatenate(dus + [ddt_raw], axis=1)
    gw_ssd = _mm(h, dproj_ssd, "tn", D, SSDW, lext, D, SSDW // 3, be, F32, "g_w_ssd")
    gw_rest = _mm(hx, dproj_rest, "tn", D, RESTW, l, D, 1024, tk, F32, "g_w_rest")
    gsegs = [gw_ssd[:, :XBC], _unperm_dt_cols(gw_ssd[:, XBC:XBC + 64]), gw_rest[:, :DI], gw_rest[:, 2 * DI:],
             gw_rest[:, DI:2 * DI]]
    g["w_in"] = jnp.concatenate(gsegs, axis=1)
    g["w_in_shards"] = jnp.stack([_vcols(gsegs, R_IN * s, R_IN * (s + 1)) for s in range(NSHARD)])
    dh_a = _mm(dproj_ssd, w_ssd, "nt", lext, D, SSDW, T, D, SSDW, BF16, "dh_ssd")
    dh_b = _mm(dproj_rest, w_rest, "nt", l, D, RESTW, T, D, RESTW, BF16, "dh_rest")
    grad_x, gnw_in, dss = _norm_bwd(dh_a, dh_b, ctx, x, dx2, mod, nw, nct)
    g["norm_w"] = gnw_in[0]
    dmod = jnp.concatenate([jnp.concatenate([dss[0:1], gv_fin[1:2]], axis=1),
                            jnp.concatenate([dss[1:2], jnp.zeros((1, D), F32)], axis=1),
                            jnp.zeros((6, 3 * D), F32)], axis=0)
    gwm, gbm, gcc = _mod_bwd(dmod, cc, cc.T, w_mod)
    g["w_mod"], g["b_mod"], g["c_ctx"] = gwm, gbm[0], gcc[1]
    return lsum[0, 0], grad_x, g


NSHARD = 4
R_MOD, R_IN, R_OS, R_OC, R_O, R_SC, R_CC = 768, 2832, 512, 256, 256, 8, 8
O_MOD = 0
O_OS = O_MOD + R_MOD
O_OC = O_OS + R_OS
O_O = O_OC + R_OC
O_SC = O_O + R_O
O_CC = O_SC + R_SC
PUSED = O_CC + R_CC
PROWS = 1824
HALF = PROWS // 2
RB = HALF // 3
WB = 128
SROWS = 16
SMALL = (("b_mod", 3 * D), ("norm_w", D), ("ssm_conv_b", XBC), ("dt_bias", 64), ("a_log", 64), ("d_skip", NH),
         ("ssm_norm_w", DI), ("conf_conv_b", D), ("conf_ln_w", D), ("conf_ln_b", D), ("final_norm_w", D), ("c_ctx", D))


def _pack_shard(s):
    return jnp.concatenate([s["w_mod"].reshape(R_MOD, D), _pack_rest(s), jnp.zeros((PROWS - PUSED, D), F32)], axis=0)


def _pack_rest(s):
    cc = jnp.pad(s["conf_conv_w"].reshape(1, CK * 256), ((0, 0), (0, R_CC * D - CK * 256))).reshape(R_CC, D)
    return jnp.concatenate([s["w_out_ssm"], s["w_out_conf"], s["w_out"],
                            jnp.pad(s["ssm_conv_w"], ((0, R_SC - SK), (0, 0))), cc], axis=0)


def _unpack_rest(p):
    o = lambda r: r - O_OS
    return {"w_out_ssm": p[o(O_OS):o(O_OC)][None], "w_out_conf": p[o(O_OC):o(O_O)][None], "w_out": p[o(O_O):o(O_SC)][None],
            "ssm_conv_w": p[o(O_SC):o(O_SC) + SK][None],
            "conf_conv_w": p[o(O_CC):o(O_CC) + R_CC].reshape(R_CC * D)[:CK * 256].reshape(1, CK, 256)}


def _shard_cols(a, n):
    return a.reshape(a.shape[0], NSHARD, n).transpose(1, 0, 2)


def _pack_full(g):
    cc = jnp.pad(_shard_cols(g["conf_conv_w"], 256).reshape(NSHARD, CK * 256), ((0, 0), (0, R_CC * D - CK * 256)))
    return jnp.concatenate([_shard_cols(g["w_mod"], R_MOD).reshape(NSHARD, R_MOD, D),
                            g["w_out_ssm"].reshape(NSHARD, R_OS, D), g["w_out_conf"].reshape(NSHARD, R_OC, D),
                            g["w_out"].reshape(NSHARD, R_O, D),
                            jnp.pad(_shard_cols(g["ssm_conv_w"], D), ((0, 0), (0, R_SC - SK), (0, 0))),
                            cc.reshape(NSHARD, R_CC, D), jnp.zeros((NSHARD, PROWS - PUSED, D), F32)], axis=1)


def _unpack_gathered(gm, gw, gs):
    def cols(a, r, n):
        return a.reshape(NSHARD, r, n).transpose(1, 0, 2).reshape(r, NSHARD * n)
    return {"w_mod": cols(gm[:, O_MOD:O_OS], D, R_MOD), "w_in": [gw[s] for s in range(NSHARD)],
            "w_out_ssm": gm[:, O_OS:O_OC].reshape(DI, D), "w_out_conf": gm[:, O_OC:O_O].reshape(D, D),
            "w_out": gm[:, O_O:O_SC].reshape(D, D), "ssm_conv_w": cols(gs[:, 0:SK], SK, D),
            "conf_conv_w": cols(gs[:, R_SC:R_SC + R_CC].reshape(NSHARD, R_CC * D)[:, :CK * 256], CK, 256)}


MESH_ID = pl.DeviceIdType.MESH
ANY = pl.BlockSpec(memory_space=pl.ANY)


def _place():
    x, y, c = lax.axis_index("x"), lax.axis_index("y"), lax.axis_index("c")
    return x, y, c, [(1 - x, y), (x, 1 - y), (1 - x, 1 - y)]


def _rcopy(src, dst, send, recv, dev):
    return pltpu.make_async_remote_copy(src_ref=src, dst_ref=dst, send_sem=send, recv_sem=recv,
                                        device_id=dev, device_id_type=MESH_ID)


def _gather_weights(mats, small):
    n = len(mats)

    def kern(*refs):
        m_refs, s_ref, g_refs, gs_ref, (send, recv) = refs[:n], refs[n], refs[n + 1:2 * n + 1], refs[2 * n + 1], refs[2 * n + 2:]
        x, y, c, chips = _place()
        me = 2 * x + y
        sib = (x, y, 1 - c)
        first, passed = [], []
        for k, (px, py) in enumerate(chips):
            first.append(_rcopy(s_ref, gs_ref.at[me], send.at[k], recv.at[k], (px, py, c)))
            for a, (m_ref, g_ref) in enumerate(zip(m_refs, g_refs)):
                mine = _half_rows(c, m_ref.shape[0])
                first.append(_rcopy(m_ref.at[mine], g_ref.at[me, mine], send.at[3 + 6 * a + k], recv.at[3 + 6 * a + k], (px, py, c)))
        for cp in first:
            cp.start()
        for k, (px, py) in enumerate(chips):
            s = 2 * px + py
            for a, (m_ref, g_ref) in enumerate(zip(m_refs, g_refs)):
                mine = _half_rows(c, m_ref.shape[0])
                _rcopy(m_ref.at[mine], g_ref.at[s, mine], send.at[3 + 6 * a + k], recv.at[3 + 6 * a + k], sib).wait_recv()
                f = _rcopy(g_ref.at[s, mine], g_ref.at[s, mine], send.at[6 + 6 * a + k], recv.at[6 + 6 * a + k], sib)
                f.start()
                passed.append(f)
        for k, (px, py) in enumerate(chips):
            s = 2 * px + py
            _rcopy(s_ref, gs_ref.at[s], send.at[k], recv.at[k], sib).wait_recv()
            for a, g_ref in enumerate(g_refs):
                other = _half_rows(1 - c, g_ref.shape[1])
                _rcopy(g_ref.at[s, other], g_ref.at[s, other], send.at[6 + 6 * a + k], recv.at[6 + 6 * a + k], sib).wait_recv()
        for cp in first + passed:
            cp.wait_send()

    nsem = 3 + 6 * n
    return pl.pallas_call(
        kern, name="gather_weights", in_specs=[ANY] * (n + 1), out_specs=[ANY] * (n + 1),
        out_shape=[jax.ShapeDtypeStruct((NSHARD,) + m.shape, m.dtype) for m in mats]
        + [jax.ShapeDtypeStruct((NSHARD, SROWS, D), F32)],
        scratch_shapes=[pltpu.SemaphoreType.DMA((nsem,)), pltpu.SemaphoreType.DMA((nsem,))],
    )(*mats, small)


def _half_rows(c, rows):
    return pl.ds(pl.multiple_of(c * (rows // 2), 16), rows // 2)


def _swap_halves(gs):
    n = len(gs)

    def kern(*refs):
        g_refs, o_refs, (send, recv) = refs[:n], refs[n:2 * n], refs[2 * n:]
        x, y, c, _ = _place()
        cps = [_rcopy(g_ref.at[s, _half_rows(1 - c, g_ref.shape[1])], o_ref.at[s], send.at[NSHARD * a + s],
                      recv.at[NSHARD * a + s], (x, y, 1 - c))
               for a, (g_ref, o_ref) in enumerate(zip(g_refs, o_refs)) for s in range(NSHARD)]
        for cp in cps:
            cp.start()
        for cp in cps:
            cp.wait()

    return pl.pallas_call(
        kern, name="swap_halves", in_specs=[ANY] * n, out_specs=[ANY] * n,
        out_shape=[jax.ShapeDtypeStruct((NSHARD, g.shape[1] // 2, g.shape[2]), F32) for g in gs],
        scratch_shapes=[pltpu.SemaphoreType.DMA((NSHARD * n,)), pltpu.SemaphoreType.DMA((NSHARD * n,))],
    )(*gs)


def _add_halves(cidx, g, ra, rb, name):
    _, half, cols = ra.shape
    nb = half // rb

    def kern(c_ref, g_ref, a_ref, o_ref):
        o_ref[...] = (g_ref[...] + a_ref[...]).astype(BF16)

    return pl.pallas_call(
        kern, name=name,
        grid_spec=pltpu.PrefetchScalarGridSpec(
            num_scalar_prefetch=1, grid=(NSHARD, nb),
            in_specs=[pl.BlockSpec((None, rb, cols), lambda s, i, c: (s, c[0] * nb + i, 0)),
                      pl.BlockSpec((None, rb, cols), lambda s, i, c: (s, i, 0))],
            out_specs=pl.BlockSpec((None, rb, cols), lambda s, i, c: (s, i, 0))),
        out_shape=jax.ShapeDtypeStruct((NSHARD, half, cols), BF16),
        compiler_params=_cp(("parallel", "parallel")),
    )(cidx, g, ra)


def _exchange_chips(ps):
    n = len(ps)

    def kern(*refs):
        p_refs, o_refs, (send, recv) = refs[:n], refs[n:2 * n], refs[2 * n:]
        x, y, c, chips = _place()
        cps = [_rcopy(p_ref.at[2 * px + py], o_ref.at[k], send.at[3 * a + k], recv.at[3 * a + k], (px, py, c))
               for a, (p_ref, o_ref) in enumerate(zip(p_refs, o_refs)) for k, (px, py) in enumerate(chips)]
        for cp in cps:
            cp.start()
        for cp in cps:
            cp.wait()

    return pl.pallas_call(
        kern, name="exchange_chips", in_specs=[ANY] * n, out_specs=[ANY] * n,
        out_shape=[jax.ShapeDtypeStruct((3,) + p.shape[1:], p.dtype) for p in ps],
        scratch_shapes=[pltpu.SemaphoreType.DMA((3 * n,)), pltpu.SemaphoreType.DMA((3 * n,))],
    )(*ps)


def _add_chips(mc, g, ra, rx, rb, name):
    _, half, cols = ra.shape
    nb = half // rb

    def kern(m_ref, g_ref, a_ref, r0_ref, r1_ref, r2_ref, o_ref):
        own = g_ref[...] + a_ref[...]
        o_ref[...] = ((own + r0_ref[...].astype(F32)) + r1_ref[...].astype(F32)) + r2_ref[...].astype(F32)

    return pl.pallas_call(
        kern, name=name,
        grid_spec=pltpu.PrefetchScalarGridSpec(
            num_scalar_prefetch=1, grid=(nb,),
            in_specs=[pl.BlockSpec((None, rb, cols), lambda i, m: (m[0], m[1] * nb + i, 0)),
                      pl.BlockSpec((None, rb, cols), lambda i, m: (m[0], i, 0))]
            + [pl.BlockSpec((None, rb, cols), functools.partial(lambda i, m, k: (k, i, 0), k=k)) for k in range(3)],
            out_specs=pl.BlockSpec((rb, cols), lambda i, m: (i, 0))),
        out_shape=jax.ShapeDtypeStruct((half, cols), F32),
        compiler_params=_cp(("parallel",)),
    )(mc, g, ra, rx, rx, rx)


def _share_halves(rs):
    n = len(rs)

    def kern(*refs):
        r_refs, o_refs, (send, recv) = refs[:n], refs[n:2 * n], refs[2 * n:]
        x, y, c, _ = _place()
        cps = [_rcopy(r_ref, o_ref, send.at[a], recv.at[a], (x, y, 1 - c))
               for a, (r_ref, o_ref) in enumerate(zip(r_refs, o_refs))]
        for cp in cps:
            cp.start()
        for cp in cps:
            cp.wait()

    return pl.pallas_call(
        kern, name="share_halves", in_specs=[ANY] * n, out_specs=[ANY] * n,
        out_shape=[jax.ShapeDtypeStruct(r.shape, F32) for r in rs],
        scratch_shapes=[pltpu.SemaphoreType.DMA((n,)), pltpu.SemaphoreType.DMA((n,))],
    )(*rs)


SMALL_W = XBC


def _small_update(gs, ws, ms, vs):
    n = len(gs)
    widths = [g.shape[1] for g in gs]
    assert n <= SROWS and max(widths) <= SMALL_W

    def kern(*refs):
        g_refs, w_refs, m_refs, v_refs = (refs[n * i:n * (i + 1)] for i in range(4))
        o_g, o_d, o_m, o_v = (refs[n * (4 + i):n * (5 + i)] for i in range(4))
        buf, send, recv = refs[8 * n:]
        x, y, c, _ = _place()
        me = 4 * x + 2 * y + c
        buf[me] = jnp.zeros((SROWS, SMALL_W), F32)
        for k, g_ref in enumerate(g_refs):
            buf[me, k:k + 1, 0:widths[k]] = g_ref[...]
        cps = []
        for r in range(1, 8):
            peer = (1 - x if r & 4 else x, 1 - y if r & 2 else y, 1 - c if r & 1 else c)
            cps.append(_rcopy(buf.at[me], buf.at[me], send.at[r - 1], recv.at[r - 1], peer))
        for cp in cps:
            cp.start()
        for cp in cps:
            cp.wait()
        acc = buf[0]
        for i in range(1, 8):
            acc = acc + buf[i]
        for k in range(n):
            g_ = acc[k:k + 1, 0:widths[k]]
            m_ = ADAM_B1 * m_refs[k][...] + (1.0 - ADAM_B1) * g_
            v_ = ADAM_B2 * v_refs[k][...] + (1.0 - ADAM_B2) * jnp.square(g_)
            m_hat = m_ / (1.0 - ADAM_B1 ** ADAM_STEP)
            v_hat = v_ / (1.0 - ADAM_B2 ** ADAM_STEP)
            o_g[k][...] = g_
            o_d[k][...] = -ADAM_LR * (m_hat / (jnp.sqrt(v_hat) + ADAM_EPS) + ADAM_WD * w_refs[k][...])
            o_m[k][...] = m_
            o_v[k][...] = v_

    vm = pl.BlockSpec(memory_space=pltpu.VMEM)
    outs = pl.pallas_call(
        kern, name="small_update", in_specs=[vm] * (4 * n), out_specs=[vm] * (4 * n),
        out_shape=[jax.ShapeDtypeStruct((1, wd), F32) for _ in range(4) for wd in widths],
        scratch_shapes=[pltpu.VMEM((8, SROWS, SMALL_W), F32), pltpu.SemaphoreType.DMA((7,)), pltpu.SemaphoreType.DMA((7,))],
    )(*gs, *ws, *ms, *vs)
    return [outs[n * i:n * (i + 1)] for i in range(4)]


def _adamw(g, w, m, v, rb, name):
    rows, cols = g.shape

    def kern(g_ref, w_ref, m_ref, v_ref, d_ref, nm_ref, nv_ref):
        g_ = g_ref[...]
        m_ = ADAM_B1 * m_ref[...] + (1.0 - ADAM_B1) * g_
        v_ = ADAM_B2 * v_ref[...] + (1.0 - ADAM_B2) * jnp.square(g_)
        m_hat = m_ / (1.0 - ADAM_B1 ** ADAM_STEP)
        v_hat = v_ / (1.0 - ADAM_B2 ** ADAM_STEP)
        d_ref[...] = -ADAM_LR * (m_hat / (jnp.sqrt(v_hat) + ADAM_EPS) + ADAM_WD * w_ref[...])
        nm_ref[...] = m_
        nv_ref[...] = v_

    assert rows % rb == 0
    blk = pl.BlockSpec((rb, cols), lambda i: (i, 0))
    return pl.pallas_call(
        kern, name=name, grid=(rows // rb,), in_specs=[blk] * 4, out_specs=[blk] * 3,
        out_shape=[jax.ShapeDtypeStruct((rows, cols), F32)] * 3,
        compiler_params=_cp(("parallel",)),
    )(g, w, m, v)


WEIGHTS = ("c_ctx", "w_mod", "b_mod", "norm_w", "w_in", "ssm_conv_w", "ssm_conv_b", "dt_bias", "a_log", "d_skip",
           "ssm_norm_w", "w_out_ssm", "conf_conv_w", "conf_conv_b", "conf_ln_w", "conf_ln_b", "w_out_conf", "w_out",
           "final_norm_w")


def kernel(x, c, ctx, c_ctx, w_mod, b_mod, norm_w, w_in, ssm_conv_w, ssm_conv_b, dt_bias, a_log, d_skip, ssm_norm_w, w_out_ssm, conf_conv_w, conf_conv_b, conf_ln_w, conf_ln_b, w_out_conf, w_out, final_norm_w, loss_target, m_c_ctx, m_w_mod, m_b_mod, m_norm_w, m_w_in, m_ssm_conv_w, m_ssm_conv_b, m_dt_bias, m_a_log, m_d_skip, m_ssm_norm_w, m_w_out_ssm, m_conf_conv_w, m_conf_conv_b, m_conf_ln_w, m_conf_ln_b, m_w_out_conf, m_w_out, m_final_norm_w, v_c_ctx, v_w_mod, v_b_mod, v_norm_w, v_w_in, v_ssm_conv_w, v_ssm_conv_b, v_dt_bias, v_a_log, v_d_skip, v_ssm_norm_w, v_w_out_ssm, v_conf_conv_w, v_conf_conv_b, v_conf_ln_w, v_conf_ln_b, v_w_out_conf, v_w_out, v_final_norm_w):
    wv = (c_ctx, w_mod, b_mod, norm_w, w_in, ssm_conv_w, ssm_conv_b, dt_bias, a_log, d_skip, ssm_norm_w, w_out_ssm,
          conf_conv_w, conf_conv_b, conf_ln_w, conf_ln_b, w_out_conf, w_out, final_norm_w)
    mv = (m_c_ctx, m_w_mod, m_b_mod, m_norm_w, m_w_in, m_ssm_conv_w, m_ssm_conv_b, m_dt_bias, m_a_log, m_d_skip,
          m_ssm_norm_w, m_w_out_ssm, m_conf_conv_w, m_conf_conv_b, m_conf_ln_w, m_conf_ln_b, m_w_out_conf, m_w_out,
          m_final_norm_w)
    vv = (v_c_ctx, v_w_mod, v_b_mod, v_norm_w, v_w_in, v_ssm_conv_w, v_ssm_conv_b, v_dt_bias, v_a_log, v_d_skip,
          v_ssm_norm_w, v_w_out_ssm, v_conf_conv_w, v_conf_conv_b, v_conf_ln_w, v_conf_ln_b, v_w_out_conf, v_w_out,
          v_final_norm_w)
    shapes = {n: a.shape for n, a in zip(WEIGHTS, wv)}

    def squeeze(d):
        return {n: (a if n in ("c_ctx", "final_norm_w") else a[0]) for n, a in d.items()}

    w, m, v = (squeeze(dict(zip(WEIGHTS, t))) for t in (wv, mv, vv))

    my_chip = 2 * lax.axis_index("x") + lax.axis_index("y")
    my_core = lax.axis_index("c")

    pw = _pack_shard(w)
    pwb, wib, psm = pw.astype(BF16), w["w_in"].astype(BF16), pw[O_SC:O_SC + SROWS]
    gm, gw, gs = _gather_weights([pwb, wib], psm)
    mine = (jnp.arange(NSHARD) == my_chip)[:, None, None]
    gm, gw, gs = jnp.where(mine, pwb[None], gm), jnp.where(mine, wib[None], gw), jnp.where(mine, psm[None], gs)
    full = dict(w)
    full.update(_unpack_gathered(gm, gw, gs))

    lsum, grad_x, g = _local_step(x[0], c, ctx[0], loss_target[0], full)
    loss = lax.psum(lsum, ("x", "y", "c"))

    cidx = my_core.astype(jnp.int32).reshape(1)
    mc = jnp.stack([my_chip, my_core]).astype(jnp.int32)
    gsrc = [_pack_full(g), g["w_in_shards"]]
    blocks = (RB, WB)
    sib = _swap_halves(gsrc)
    part = [_add_halves(cidx, a, b, rb, "add_halves_%d" % i) for i, (a, b, rb) in enumerate(zip(gsrc, sib, blocks))]
    far = _exchange_chips(part)
    red = [_add_chips(mc, a, b, f, rb, "add_chips_%d" % i) for i, (a, b, f, rb) in enumerate(zip(gsrc, sib, far, blocks))]
    got = _share_halves(red)
    g_pk, g_win = (jnp.concatenate([jnp.where(my_core == 0, r, o), jnp.where(my_core == 0, o, r)], axis=0)
                   for r, o in zip(red, got))
    small = [name for name, _ in SMALL]
    as_row = lambda d: [d[name].reshape(1, -1) for name in small]
    res_sm = _small_update(as_row(g), as_row(w), as_row(m), as_row(v))

    gr = {"w_mod": g_pk[O_MOD:O_OS].reshape(D, R_MOD), "w_in": g_win, "rest": g_pk[O_OS:PUSED]}
    wr, mr, vr = ({"w_mod": t["w_mod"], "w_in": t["w_in"], "rest": _pack_rest(t)} for t in (w, m, v))
    res = {k: _adamw(gr[k], wr[k], mr[k], vr[k], rb, "adamw_" + k)
           for k, rb in (("w_in", WB), ("w_mod", 512), ("rest", (PUSED - O_OS) // 2))}

    outs = []
    for i in range(4):
        pick = (lambda k: gr[k]) if i == 0 else (lambda k: res[k][i - 1])
        d = {"w_mod": pick("w_mod")[None], "w_in": pick("w_in")[None]}
        d.update(_unpack_rest(pick("rest")))
        d.update({name: a.reshape(shapes[name]) for name, a in zip(small, res_sm[i])})
        outs.extend(d[n] for n in WEIGHTS)
    return (loss, grad_x[None], *outs)
```

```python
import functools

import jax
import jax.numpy as jnp
from jax import lax
from jax.experimental import pallas as pl
from jax.experimental.pallas import tpu as pltpu

F32, BF16 = jnp.float32, jnp.bfloat16

D = 1024
DI = 2048
NH = 32
HP = 64
NG = 8
HPG = 4
NS = 128
Q = 128
GW = 64
CK = 31
SK = 4
CTX = 256
EPS = 1e-6
XBC = DI + 2 * NG * NS
SSDW = XBC + 128
RESTW = 7168
T = 256
TX = 512
VMEM_LIMIT = 56 * 1024 * 1024

ADAM_LR, ADAM_B1, ADAM_B2, ADAM_EPS, ADAM_WD, ADAM_STEP = 0.001, 0.9, 0.999, 1e-08, 0.01, 10


def _cp(sem):
    return pltpu.CompilerParams(dimension_semantics=sem, vmem_limit_bytes=VMEM_LIMIT)


def _sig(x):
    return jax.nn.sigmoid(x)


def _silu(x):
    return x * _sig(x)


def _dsilu(x):
    s = _sig(x)
    return s * (1.0 + x * (1.0 - s))


def _dot(a, b):
    return jnp.dot(a, b, preferred_element_type=F32)


def _dot_nt(a, b):
    return lax.dot_general(a, b, (((1,), (1,)), ((), ())), preferred_element_type=F32)


def _split3(x):
    h = x.astype(BF16)
    r = x - h.astype(F32)
    m = r.astype(BF16)
    l = (r - m.astype(F32)).astype(BF16)
    return h, m, l


def _dot3_l(sel, x):
    h, m, l = _split3(x)
    return _dot(sel, h) + _dot(sel, m) + _dot(sel, l)


def _dot3_r(x, sel):
    h, m, l = _split3(x)
    return _dot(h, sel) + _dot(m, sel) + _dot(l, sel)


def _split2(x):
    h = x.astype(BF16)
    return h, (x - h.astype(F32)).astype(BF16)


def _dot2_l(sel, x):
    h, l = _split2(x)
    return _dot(sel, h) + _dot(sel, l)


def _dot2_r(x, sel):
    h, l = _split2(x)
    return _dot(h, sel) + _dot(l, sel)


def _iota(shape, dim):
    return lax.broadcasted_iota(jnp.int32, shape, dim)


def _mm(a, b, dims, m, n, k, bm, bn, bk, out_dtype, name):
    nk = k // bk
    assert m % bm == 0 and n % bn == 0 and k % bk == 0, (name, m, n, k, bm, bn, bk)

    def prod(a_ref, b_ref):
        av = a_ref[...].astype(BF16)
        bv = b_ref[...].astype(BF16)
        if dims == "nn":
            return _dot(av, bv)
        if dims == "nt":
            return _dot_nt(av, bv)
        return lax.dot_general(av, bv, (((0,), (0,)), ((), ())), preferred_element_type=F32)

    def kern_one(a_ref, b_ref, o_ref):
        o_ref[...] = prod(a_ref, b_ref).astype(out_dtype)

    def kern_acc(a_ref, b_ref, o_ref, acc):
        kk = pl.program_id(2)

        @pl.when(kk == 0)
        def _():
            acc[...] = jnp.zeros_like(acc)

        acc[...] += prod(a_ref, b_ref)

        @pl.when(kk == nk - 1)
        def _():
            o_ref[...] = acc[...].astype(out_dtype)

    if dims == "nn":
        a_spec = pl.BlockSpec((bm, bk), lambda j, i, kk: (i, kk))
        b_spec = pl.BlockSpec((bk, bn), lambda j, i, kk: (kk, j))
    elif dims == "nt":
        a_spec = pl.BlockSpec((bm, bk), lambda j, i, kk: (i, kk))
        b_spec = pl.BlockSpec((bn, bk), lambda j, i, kk: (j, kk))
    else:
        a_spec = pl.BlockSpec((bk, bm), lambda j, i, kk: (kk, i))
        b_spec = pl.BlockSpec((bk, bn), lambda j, i, kk: (kk, j))
    return pl.pallas_call(
        kern_one if nk == 1 else kern_acc, name=name,
        grid=(n // bn, m // bm, nk),
        in_specs=[a_spec, b_spec],
        out_specs=pl.BlockSpec((bm, bn), lambda j, i, kk: (i, j)),
        out_shape=jax.ShapeDtypeStruct((m, n), out_dtype),
        scratch_shapes=[] if nk == 1 else [pltpu.VMEM((bm, bn), F32)],
        compiler_params=_cp(("parallel", "parallel", "arbitrary")),
    )(a, b)


def _mod_fwd(cc, w_mod, b_mod):
    def kern(cc_ref, w_ref, b_ref, o_ref):
        s = _silu(cc_ref[...]).astype(BF16)
        o_ref[...] = _dot(s, w_ref[...]) + b_ref[...]

    return pl.pallas_call(
        kern, name="mod_fwd", grid=(3,),
        in_specs=[pl.BlockSpec((8, D), lambda j: (0, 0)), pl.BlockSpec((D, D), lambda j: (0, j)),
                  pl.BlockSpec((1, D), lambda j: (0, j))],
        out_specs=pl.BlockSpec((8, D), lambda j: (0, j)),
        out_shape=jax.ShapeDtypeStruct((8, 3 * D), F32),
        compiler_params=_cp(("parallel",)),
    )(cc, w_mod, b_mod)


def _mod_bwd(dmod, cc, cct, w_mod):
    def kern(dm_ref, cc_ref, cct_ref, w_ref, gw_ref, gb_ref, gc_ref):
        kk = pl.program_id(0)
        dm = dm_ref[...]
        sct = _silu(cct_ref[...])
        gw_ref[...] = sct[:, 0:1] * dm[0:1, :] + sct[:, 1:2] * dm[1:2, :]
        gb_ref[...] = jnp.broadcast_to(dm[0:1, :] + dm[1:2, :], dm.shape)

        @pl.when(kk == 0)
        def _():
            gc_ref[...] = jnp.zeros_like(gc_ref)

        gc_ref[...] += _dot_nt(dm.astype(BF16), w_ref[...])

        @pl.when(kk == 2)
        def _():
            gc_ref[...] = gc_ref[...] * _dsilu(cc_ref[...])

    return pl.pallas_call(
        kern, name="mod_bwd", grid=(3,),
        in_specs=[pl.BlockSpec((8, D), lambda j: (0, j)), pl.BlockSpec((8, D), lambda j: (0, 0)),
                  pl.BlockSpec((D, 8), lambda j: (0, 0)), pl.BlockSpec((D, D), lambda j: (0, j))],
        out_specs=[pl.BlockSpec((D, D), lambda j: (0, j)), pl.BlockSpec((8, D), lambda j: (0, j)),
                   pl.BlockSpec((8, D), lambda j: (0, 0))],
        out_shape=[jax.ShapeDtypeStruct((D, 3 * D), F32), jax.ShapeDtypeStruct((8, 3 * D), F32),
                   jax.ShapeDtypeStruct((8, D), F32)],
        compiler_params=_cp(("arbitrary",)),
    )(dmod, cc, cct, w_mod)


def _ext_specs(nct):
    return (pl.BlockSpec((T, D), lambda i: (jnp.minimum(i, nct - 1), 0)),
            pl.BlockSpec((T, D), lambda i: (jnp.maximum(i - nct, 0), 0)))


def _norm_fwd(ctx, xl, mod, nw, nct):
    lext = ctx.shape[0] + xl.shape[0]

    def kern(c_ref, x_ref, mod_ref, nw_ref, h_ref):
        is_ctx = pl.program_id(0) < nct
        x = jnp.where(is_ctx, c_ref[...], x_ref[...])
        r = lax.rsqrt(jnp.mean(x * x, axis=-1, keepdims=True) + EPS)
        xn = x * r * nw_ref[...]
        shift = jnp.where(is_ctx, mod_ref[1:2, 0:D], mod_ref[0:1, 0:D])
        scale = jnp.where(is_ctx, mod_ref[1:2, D:2 * D], mod_ref[0:1, D:2 * D])
        h_ref[...] = (xn * (1.0 + scale) + shift).astype(BF16)

    return pl.pallas_call(
        kern, name="norm_fwd", grid=(lext // T,),
        in_specs=[*_ext_specs(nct), pl.BlockSpec((8, 3 * D), lambda i: (0, 0)),
                  pl.BlockSpec((1, D), lambda i: (0, 0))],
        out_specs=pl.BlockSpec((T, D), lambda i: (i, 0)),
        out_shape=jax.ShapeDtypeStruct((lext, D), BF16),
        compiler_params=_cp(("parallel",)),
    )(ctx, xl, mod, nw)


def _norm_bwd(dha, dhb, ctx, xl, dx2, mod, nw, nct):
    lext = ctx.shape[0] + xl.shape[0]
    ntl = lext // T

    def kern(dha_ref, dhb_ref, c_ref, x_ref, dx2_ref, mod_ref, nw_ref, gx_ref, gnw_ref, dss_ref):
        i = pl.program_id(0)
        is_ctx = i < nct

        @pl.when(i == 0)
        def _():
            gnw_ref[...] = jnp.zeros_like(gnw_ref)
            dss_ref[...] = jnp.zeros_like(dss_ref)

        x = jnp.where(is_ctx, c_ref[...], x_ref[...])
        dh_ = dha_ref[...].astype(F32) + jnp.where(is_ctx, 0.0, dhb_ref[...].astype(F32))
        nw_ = nw_ref[...]
        r = lax.rsqrt(jnp.mean(x * x, axis=-1, keepdims=True) + EPS)
        xn = x * r
        scale = jnp.where(is_ctx, mod_ref[1:2, D:2 * D], mod_ref[0:1, D:2 * D])
        dsh = jnp.sum(dh_, axis=0, keepdims=True)
        dsc = jnp.sum(dh_ * (xn * nw_), axis=0, keepdims=True)
        row = jnp.concatenate([dsh, dsc], axis=1)
        rid = _iota((8, 2 * D), 0)
        dss_ref[...] += jnp.where(rid == jnp.where(is_ctx, 1, 0), row, 0.0)
        dxnw = dh_ * (1.0 + scale)
        gnw_ref[...] += jnp.broadcast_to(jnp.sum(dxnw * xn, axis=0, keepdims=True), (8, D))
        dxn = dxnw * nw_
        dx = r * (dxn - xn * jnp.mean(dxn * xn, axis=-1, keepdims=True))
        gx_ref[...] = dx2_ref[...] + dx

    return pl.pallas_call(
        kern, name="norm_bwd", grid=(ntl,),
        in_specs=[pl.BlockSpec((T, D), lambda i: (i, 0)), pl.BlockSpec((T, D), lambda i: (jnp.maximum(i - nct, 0), 0)),
                  *_ext_specs(nct),
                  pl.BlockSpec((T, D), lambda i: (jnp.maximum(i - nct, 0), 0)),
                  pl.BlockSpec((8, 3 * D), lambda i: (0, 0)), pl.BlockSpec((1, D), lambda i: (0, 0))],
        out_specs=[pl.BlockSpec((T, D), lambda i: (jnp.maximum(i - nct, 0), 0)),
                   pl.BlockSpec((8, D), lambda i: (0, 0)), pl.BlockSpec((8, 2 * D), lambda i: (0, 0))],
        out_shape=[jax.ShapeDtypeStruct((lext - nct * T, D), F32), jax.ShapeDtypeStruct((8, D), F32),
                   jax.ShapeDtypeStruct((8, 2 * D), F32)],
        compiler_params=_cp(("arbitrary",)),
    )(dha, dhb, ctx, xl, dx2, mod, nw)


CB = 1024


def _halo_specs(width_blk, col_off_blocks, ntl):
    t8 = T // 8
    main = pl.BlockSpec((T, width_blk), lambda j, i: (i, j + col_off_blocks))
    prev = pl.BlockSpec((8, width_blk), lambda j, i: (jnp.maximum(i * t8 - 1, 0), j + col_off_blocks))
    nxt = pl.BlockSpec((8, width_blk), lambda j, i: (jnp.minimum((i + 1) * t8, ntl * t8 - 1), j + col_off_blocks))
    return main, prev, nxt


def _seq_edges(i, nct, ntl):
    starts = jnp.logical_or(i == 0, i == nct)
    ends = jnp.logical_or(i == nct - 1, i == ntl - 1)
    return starts, ends


def _shifted(ext, off):
    n = ext.shape[0]
    return pltpu.roll(ext, (-off) % n, axis=0)[8:8 + T]


def _conv_fwd(proj_ssd, cw, cb, nct):
    lext = proj_ssd.shape[0]
    ntl = lext // T

    def kern(u_ref, up_ref, un_ref, w_ref, b_ref, o_ref):
        i = pl.program_id(1)
        starts, ends = _seq_edges(i, nct, ntl)
        up = jnp.where(starts, 0.0, up_ref[...])
        un = jnp.where(ends, 0.0, un_ref[...])
        ext = jnp.concatenate([up, u_ref[...], un], axis=0)
        w = w_ref[...]
        pre = b_ref[...] + w[0:1] * _shifted(ext, -2) + w[1:2] * _shifted(ext, -1) \
            + w[2:3] * u_ref[...] + w[3:4] * _shifted(ext, 1)
        o_ref[...] = _silu(pre)

    cbf = 2 * CB
    main, prev, nxt = _halo_specs(cbf, 0, ntl)
    return pl.pallas_call(
        kern, name="conv_fwd", grid=(XBC // cbf, ntl),
        in_specs=[main, prev, nxt, pl.BlockSpec((8, cbf), lambda j, i: (0, j)), pl.BlockSpec((1, cbf), lambda j, i: (0, j))],
        out_specs=pl.BlockSpec((T, cbf), lambda j, i: (i, j)),
        out_shape=jax.ShapeDtypeStruct((lext, XBC), F32),
        compiler_params=_cp(("parallel", "parallel")),
    )(proj_ssd, proj_ssd, proj_ssd, cw, cb)


def _conv_bwd(dpost, proj_ssd, cw, cb, col_off, width, nct, name):
    lext = proj_ssd.shape[0]
    ntl = lext // T
    bw = min(width, 2 * CB)
    assert col_off % bw == 0 and width % bw == 0
    cob = col_off // bw

    def kern(u_ref, up_ref, un_ref, d_ref, dp_ref, dn_ref, w_ref, b_ref, du_ref, gw_ref, gb_ref):
        i = pl.program_id(1)

        @pl.when(i == 0)
        def _():
            gw_ref[...] = jnp.zeros_like(gw_ref)
            gb_ref[...] = jnp.zeros_like(gb_ref)

        starts, ends = _seq_edges(i, nct, ntl)
        ext = jnp.concatenate([jnp.where(starts, 0.0, up_ref[...]), u_ref[...], jnp.where(ends, 0.0, un_ref[...])], axis=0)
        dext = jnp.concatenate([jnp.where(starts, 0.0, dp_ref[...]), d_ref[...], jnp.where(ends, 0.0, dn_ref[...])], axis=0)
        w = w_ref[...]
        n = ext.shape[0]
        pre = b_ref[...] + w[0:1] * pltpu.roll(ext, 2, axis=0) + w[1:2] * pltpu.roll(ext, 1, axis=0) \
            + w[2:3] * ext + w[3:4] * pltpu.roll(ext, n - 1, axis=0)
        dpre = dext * _dsilu(pre)
        dm = dpre[8:8 + T]
        du = w[0:1] * _shifted(dpre, 2) + w[1:2] * _shifted(dpre, 1) + w[2:3] * dm + w[3:4] * _shifted(dpre, -1)
        du_ref[...] = du.astype(BF16)
        g0 = jnp.sum(dm * _shifted(ext, -2), axis=0, keepdims=True)
        g1 = jnp.sum(dm * _shifted(ext, -1), axis=0, keepdims=True)
        g2 = jnp.sum(dm * u_ref[...], axis=0, keepdims=True)
        g3 = jnp.sum(dm * _shifted(ext, 1), axis=0, keepdims=True)
        rid = _iota((8, bw), 0)
        gw_ref[...] += jnp.where(rid == 0, g0, jnp.where(rid == 1, g1, jnp.where(rid == 2, g2, jnp.where(rid == 3, g3, 0.0))))
        gb_ref[...] += jnp.broadcast_to(jnp.sum(dm, axis=0, keepdims=True), (8, bw))

    main, prev, nxt = _halo_specs(bw, cob, ntl)
    dmain, dprev, dnxt = _halo_specs(bw, 0, ntl)
    return pl.pallas_call(
        kern, name=name, grid=(width // bw, ntl),
        in_specs=[main, prev, nxt, dmain, dprev, dnxt,
                  pl.BlockSpec((8, bw), lambda j, i: (0, j + cob)), pl.BlockSpec((1, bw), lambda j, i: (0, j + cob))],
        out_specs=[pl.BlockSpec((T, bw), lambda j, i: (i, j)), pl.BlockSpec((8, bw), lambda j, i: (0, j)),
                   pl.BlockSpec((8, bw), lambda j, i: (0, j))],
        out_shape=[jax.ShapeDtypeStruct((lext, width), BF16), jax.ShapeDtypeStruct((8, width), F32),
                   jax.ShapeDtypeStruct((8, width), F32)],
        compiler_params=_cp(("parallel", "arbitrary")),
    )(proj_ssd, proj_ssd, proj_ssd, dpost, dpost, dpost, cw, cb)


def _tri(lower):
    r, c = _iota((Q, Q), 0), _iota((Q, Q), 1)
    return jnp.where((c <= r) if lower else (c >= r), 1.0, 0.0).astype(BF16)


def _is_bdir_lane(shape):
    ln = _iota(shape, len(shape) - 1)
    return jnp.logical_and(((ln >> 2) & 1) == 1, ln < 64)


def _dt_fwd(proj_ssd, dtb, av):
    lext = proj_ssd.shape[0]

    def kern(p_ref, b_ref, a_ref, dtg_ref, lag_ref, dtt_ref, lat_ref):
        lane = _iota((T, 128), 1)
        raw = p_ref[...] + b_ref[...]
        dt = jnp.where(lane < 64, jnp.maximum(raw, 0.0) + jnp.log1p(jnp.exp(-jnp.abs(raw))), 0.0)
        dta = dt * a_ref[...]
        tl, tu = _tri(True), _tri(False)
        isb = _is_bdir_lane((Q, 128))
        las = []
        for qq in range(T // Q):
            blk = dta[qq * Q:(qq + 1) * Q]
            las.append(jnp.where(isb, _dot3_l(tu, blk), _dot3_l(tl, blk)))
        la = jnp.concatenate(las, axis=0)
        for g in range(NG):
            sh = (128 - 8 * g) % 128
            dtg_ref[g] = jnp.where(lane < 8, pltpu.roll(dt, sh, axis=1) if sh else dt, 0.0)
            lag_ref[g] = jnp.where(lane < 8, pltpu.roll(la, sh, axis=1) if sh else la, 0.0)
        dtt_ref[...] = dt.T[0:64]
        lat_ref[...] = la.T[0:64]

    return pl.pallas_call(
        kern, name="dt_fwd", grid=(lext // T,),
        in_specs=[pl.BlockSpec((T, 128), lambda i: (i, XBC // 128)), pl.BlockSpec((1, 128), lambda i: (0, 0)),
                  pl.BlockSpec((1, 128), lambda i: (0, 0))],
        out_specs=[pl.BlockSpec((NG, T, 128), lambda i: (0, i, 0)), pl.BlockSpec((NG, T, 128), lambda i: (0, i, 0)),
                   pl.BlockSpec((64, T), lambda i: (0, i)), pl.BlockSpec((64, T), lambda i: (0, i))],
        out_shape=[jax.ShapeDtypeStruct((NG, lext, 128), F32), jax.ShapeDtypeStruct((NG, lext, 128), F32),
                   jax.ShapeDtypeStruct((64, lext), F32), jax.ShapeDtypeStruct((64, lext), F32)],
        compiler_params=_cp(("parallel",)),
    )(proj_ssd, dtb, av)


def _dt_bwd(ddtg, proj_ssd, dtb):
    lext = proj_ssd.shape[0]

    def kern(d_ref, p_ref, b_ref, o_ref, gb_ref):
        @pl.when(pl.program_id(0) == 0)
        def _():
            gb_ref[...] = jnp.zeros_like(gb_ref)

        acc = d_ref[0]
        for g in range(1, NG):
            acc = acc + pltpu.roll(d_ref[g], 8 * g, axis=1)
        draw = acc * _sig(p_ref[...] + b_ref[...])
        o_ref[...] = draw.astype(BF16)
        gb_ref[...] += jnp.broadcast_to(jnp.sum(draw, axis=0, keepdims=True), (8, 128))

    return pl.pallas_call(
        kern, name="dt_bwd", grid=(lext // T,),
        in_specs=[pl.BlockSpec((NG, T, 128), lambda i: (0, i, 0)), pl.BlockSpec((T, 128), lambda i: (i, XBC // 128)),
                  pl.BlockSpec((1, 128), lambda i: (0, 0))],
        out_specs=[pl.BlockSpec((T, 128), lambda i: (i, 0)), pl.BlockSpec((8, 128), lambda i: (0, 0))],
        out_shape=[jax.ShapeDtypeStruct((lext, 128), BF16), jax.ShapeDtypeStruct((8, 128), F32)],
        compiler_params=_cp(("arbitrary",)),
    )(ddtg, proj_ssd, dtb)


def _expand_sel(d):
    r, c = _iota((128, 256), 0), _iota((128, 256), 1)
    return jnp.where(r == 4 * d + (c >> 6), 1.0, 0.0).astype(BF16)


def _reduce_sel(d):
    r, c = _iota((256, 128), 0), _iota((256, 128), 1)
    return jnp.where(c == 4 * d + (r >> 6), 1.0, 0.0).astype(BF16)


def _chunk_of_bwd_dir(j, ncc, nc):
    return jnp.where(j < ncc, ncc - 1 - j, nc + ncc - 1 - j)


def _dir_terms(la, dt, d):
    lane = _iota(la.shape, 1)
    mine = jnp.logical_and(lane >= 4 * d, lane < 4 * d + 4)
    la = jnp.where(mine, la, 0.0)
    tot = la[Q - 1:Q] if d == 0 else la[0:1]
    wnd = jnp.exp(tot - la)
    return tot, wnd * jnp.where(mine, dt, 0.0), wnd


def _ssd_state(xbc, dtg, lag, ncc):
    lext = xbc.shape[0]
    nc = lext // Q

    def kern(xf_ref, bf_ref, dtf_ref, laf_ref, xb_ref, bb_ref, dtb_ref, lab_ref, hf_ref, hb_ref, sf, sb):
        @pl.when(pl.program_id(0) == 0)
        def _():
            sf[...] = jnp.zeros_like(sf)
            sb[...] = jnp.zeros_like(sb)

        for d, (x_ref, b_ref, dt_ref, la_ref, h_ref, s) in enumerate(
                ((xf_ref, bf_ref, dtf_ref, laf_ref, hf_ref, sf), (xb_ref, bb_ref, dtb_ref, lab_ref, hb_ref, sb))):
            h_ref[...] = s[...]
            ex = _expand_sel(d)
            for g in range(NG):
                cols = slice(256 * g, 256 * (g + 1))
                tot, w_end, _ = _dir_terms(la_ref[g], dt_ref[g], d)
                wexp = _dot2_r(w_end, ex)
                dexp = _dot2_r(jnp.broadcast_to(jnp.exp(tot), (8, 128)), ex)[0:1]
                xw = (x_ref[:, cols] * wexp).astype(BF16)
                s[:, cols] = s[:, cols] * dexp + _dot(b_ref[:, 128 * g:128 * (g + 1)].T.astype(BF16), xw)

    cb = functools.partial(_chunk_of_bwd_dir, ncc=ncc, nc=nc)
    sm = lambda f: pl.BlockSpec((NG, Q, 128), lambda j: (0, f(j), 0))
    one = lambda j: j
    return pl.pallas_call(
        kern, name="ssd_state", grid=(nc,),
        in_specs=[pl.BlockSpec((Q, DI), lambda j: (j, 0)), pl.BlockSpec((Q, NG * NS), lambda j: (j, 2)), sm(one), sm(one),
                  pl.BlockSpec((Q, DI), lambda j: (cb(j), 0)), pl.BlockSpec((Q, NG * NS), lambda j: (cb(j), 2)), sm(cb), sm(cb)],
        out_specs=[pl.BlockSpec((None, 128, DI), lambda j: (j, 0, 0)),
                   pl.BlockSpec((None, 128, DI), lambda j: (cb(j), 0, 0))],
        out_shape=[jax.ShapeDtypeStruct((nc, 128, DI), F32), jax.ShapeDtypeStruct((nc, 128, DI), F32)],
        scratch_shapes=[pltpu.VMEM((128, DI), F32), pltpu.VMEM((128, DI), F32)],
        compiler_params=_cp(("arbitrary",)),
    )(xbc, xbc, dtg, lag, xbc, xbc, dtg, lag)


def _ssd_out(xbc, dtg, lag, dtt, lat, htf, htb, ncc):
    lext = xbc.shape[0]
    nc = lext // Q
    ncx = nc - ncc

    gps = 4
    li, si = (lambda: _iota((Q, Q), 0)), (lambda: _iota((Q, Q), 1))

    def kern(x_ref, b_ref, c_ref, dtg_ref, lag_ref, dtt_ref, lat_ref, hf_ref, hb_ref, y_ref):
        lane = _iota((Q, 128), 1)
        masks = (li() >= si(), li() <= si())
        for gg in range(gps):
            cols = slice(256 * gg, 256 * (gg + 1))
            cm = c_ref[:, 128 * gg:128 * (gg + 1)]
            xb_ = x_ref[:, cols].astype(BF16)
            s_ = _dot_nt(cm.astype(BF16), b_ref[:, 128 * gg:128 * (gg + 1)].astype(BF16))
            la, dtt_, lat_ = lag_ref[gg], dtt_ref[8 * gg:8 * (gg + 1)], lat_ref[8 * gg:8 * (gg + 1)]
            elam = jnp.exp(la)
            yh = [jnp.zeros((Q, 128), F32), jnp.zeros((Q, 128), F32)]
            for d, h_ref in enumerate((hf_ref, hb_ref)):
                rhs = jnp.concatenate([xb_, h_ref[:, cols].astype(BF16)], axis=0)
                lhs = []
                for r in range(HPG):
                    j = 4 * d + r
                    lm = jnp.where(masks[d], jnp.exp(la[:, j:j + 1] - lat_[j:j + 1, :]), 0.0)
                    w = s_ * lm * dtt_[j:j + 1, :]
                    lhs.append(jnp.concatenate([w, cm * elam[:, j:j + 1]], axis=1).astype(BF16))
                for b in range(HPG // 2):
                    ypair = _dot(jnp.concatenate(lhs[2 * b:2 * b + 2], axis=0), rhs[:, 128 * b:128 * (b + 1)])
                    yh[b] = yh[b] + jnp.where(lane < 64, ypair[0:Q], ypair[Q:2 * Q])
            y_ref[:, cols] = jnp.concatenate(yh, axis=1).astype(BF16)

    nb = NG // gps
    sm = pl.BlockSpec((gps, Q, 128), lambda c, g: (g, c + ncc, 0))
    smt = pl.BlockSpec((8 * gps, Q), lambda c, g: (g, c + ncc))
    st3 = pl.BlockSpec((None, 128, 256 * gps), lambda c, g: (c + ncc, 0, g))
    return pl.pallas_call(
        kern, name="ssd_out", grid=(ncx, nb),
        in_specs=[pl.BlockSpec((Q, 256 * gps), lambda c, g: (c + ncc, g)),
                  pl.BlockSpec((Q, 128 * gps), lambda c, g: (c + ncc, 2 * nb + g)),
                  pl.BlockSpec((Q, 128 * gps), lambda c, g: (c + ncc, 3 * nb + g)), sm, sm, smt, smt, st3, st3],
        out_specs=pl.BlockSpec((Q, 256 * gps), lambda c, g: (c, g)),
        out_shape=jax.ShapeDtypeStruct((ncx * Q, DI), BF16),
        compiler_params=_cp(("parallel", "parallel")),
    )(xbc, xbc, xbc, dtg, lag, dtt, lat, htf, htb)


def _ssd_bwd_state(xbc, dy, lag, ncc):
    lext = xbc.shape[0]
    nc = lext // Q

    def kern(cf_ref, dyf_ref, laf_ref, cb_ref, dyb_ref, lab_ref, df_ref, db_ref, sf, sb):
        @pl.when(pl.program_id(0) == 0)
        def _():
            sf[...] = jnp.zeros_like(sf)
            sb[...] = jnp.zeros_like(sb)

        for d, (c_ref, dy_ref, la_ref, o_ref, s) in enumerate(
                ((cf_ref, dyf_ref, laf_ref, df_ref, sf), (cb_ref, dyb_ref, lab_ref, db_ref, sb))):
            o_ref[...] = s[...]
            ex = _expand_sel(d)
            for g in range(NG):
                cols = slice(256 * g, 256 * (g + 1))
                la = la_ref[g]
                tot = la[Q - 1:Q] if d == 0 else la[0:1]
                eexp = _dot2_r(jnp.exp(la), ex)
                dexp = _dot2_r(jnp.broadcast_to(jnp.exp(tot), (8, 128)), ex)[0:1]
                dye = (dy_ref[:, cols] * eexp).astype(BF16)
                s[:, cols] = s[:, cols] * dexp + _dot(c_ref[:, 128 * g:128 * (g + 1)].T.astype(BF16), dye)

    cf = lambda j: nc - 1 - j
    cb = lambda j: _chunk_of_bwd_dir(nc - 1 - j, ncc, nc)
    sm = lambda f: pl.BlockSpec((NG, Q, 128), lambda j: (0, f(j), 0))
    return pl.pallas_call(
        kern, name="ssd_bwd_state", grid=(nc,),
        in_specs=[pl.BlockSpec((Q, NG * NS), lambda j: (cf(j), 3)), pl.BlockSpec((Q, DI), lambda j: (cf(j), 0)), sm(cf),
                  pl.BlockSpec((Q, NG * NS), lambda j: (cb(j), 3)), pl.BlockSpec((Q, DI), lambda j: (cb(j), 0)), sm(cb)],
        out_specs=[pl.BlockSpec((None, 128, DI), lambda j: (cf(j), 0, 0)),
                   pl.BlockSpec((None, 128, DI), lambda j: (cb(j), 0, 0))],
        out_shape=[jax.ShapeDtypeStruct((nc, 128, DI), F32), jax.ShapeDtypeStruct((nc, 128, DI), F32)],
        scratch_shapes=[pltpu.VMEM((128, DI), F32), pltpu.VMEM((128, DI), F32)],
        compiler_params=_cp(("arbitrary",)),
    )(xbc, dy, lag, xbc, dy, lag)


def _ssd_bwd_out(xbc, dy, dsk, dtg, lag, dtt, lat, htf, htb, dhf, dhb, a_rows):
    lext = xbc.shape[0]
    nc = lext // Q

    gps = 1

    def kern(x_ref, b_ref, c_ref, dy_ref, sk_ref, dtg_ref, lag_ref, dtt_ref, lat_ref, hf_ref, hb_ref, df_ref, db_ref,
             a_ref, dx_ref, dbo_ref, dco_ref, ddt_ref, ga_ref):
        @pl.when(pl.program_id(1) == 0)
        def _():
            ga_ref[...] = jnp.zeros_like(ga_ref)

        for gg in range(gps):
            one_group(gg, x_ref, b_ref, c_ref, dy_ref, sk_ref, dtg_ref, lag_ref, dtt_ref, lat_ref, hf_ref, hb_ref, df_ref,
                      db_ref, a_ref, dx_ref, dbo_ref, dco_ref, ddt_ref, ga_ref)

    def one_group(gg, x_ref, b_ref, c_ref, dy_ref, sk_ref, dtg_ref, lag_ref, dtt_ref, lat_ref, hf_ref, hb_ref, df_ref,
                  db_ref, a_ref, dx_ref, dbo_ref, dco_ref, ddt_ref, ga_ref):
        g = pl.program_id(0) * gps + gg
        cols, cols128 = slice(256 * gg, 256 * (gg + 1)), slice(128 * gg, 128 * (gg + 1))
        x, bm, cm, dy_ = x_ref[:, cols], b_ref[:, cols128], c_ref[:, cols128], dy_ref[:, cols]
        xb_, bb_, cb_, dyb_ = x.astype(BF16), bm.astype(BF16), cm.astype(BF16), dy_.astype(BF16)
        st = _dot_nt(bb_, cb_)
        si, li = _iota((Q, Q), 0), _iota((Q, Q), 1)
        lane = _iota((Q, 256), 1)
        lane128 = _iota((Q, 128), 1)
        row128 = _iota((Q, 128), 0)
        sub = _iota((128, Q), 0)
        la, dt = lag_ref[gg], dtg_ref[gg]
        dtt_, lat_ = dtt_ref[8 * gg:8 * (gg + 1)], lat_ref[8 * gg:8 * (gg + 1)]
        elam = jnp.exp(la)
        dst = jnp.zeros((Q, Q), F32)
        dxh = [jnp.zeros((Q, 128), F32), jnp.zeros((Q, 128), F32)]
        cdir, clam = jnp.zeros((Q, 128), F32), jnp.zeros((Q, 128), F32)
        dba = jnp.zeros((Q, 128), F32)
        dca = jnp.zeros((Q, 128), F32)
        dlam = jnp.zeros((Q, 128), F32)
        ddir = jnp.zeros((Q, 128), F32)
        rows = jnp.zeros((128, Q), F32)
        per_dir = []
        for d, (h_ref, dh_ref) in enumerate(((hf_ref, df_ref), (hb_ref, db_ref))):
            ht, dht = h_ref[:, cols], dh_ref[:, cols]
            htb_, dhtb_ = ht.astype(BF16), dht.astype(BF16)
            tot, w_end, wnd = _dir_terms(la, dt, d)
            ex, rs = _expand_sel(d), _reduce_sel(d)
            elx = _dot2_r(elam, ex)
            wex = _dot2_r(w_end, ex)
            dye = dy_ * elx
            ch = _dot(cb_, htb_)
            bd = _dot(bb_, dhtb_)
            dca = dca + _dot_nt(dye.astype(BF16), htb_)
            dba = dba + _dot_nt((x * wex).astype(BF16), dhtb_)
            dlam = dlam + _dot2_r(dye * ch, rs)
            xbd = _dot2_r(x * bd, rs)
            e_ = w_end * xbd
            dlam = dlam - e_
            ddir = ddir + wnd * xbd
            hh = _dot2_r(jnp.broadcast_to(jnp.sum(dht * ht, axis=0, keepdims=True), (8, 256)), rs)[0:1]
            tot_term = jnp.sum(e_, axis=0, keepdims=True) + jnp.exp(tot) * hh
            dlam = dlam + jnp.where(row128 == (Q - 1 if d == 0 else 0), tot_term, 0.0)
            per_dir.append((w_end, jnp.concatenate([dyb_, dhtb_], axis=0), (li >= si) if d == 0 else (li <= si)))
        for r in range(HPG):
            half = slice(128 * (r // 2), 128 * (r // 2 + 1))
            hm = (lane128 >> 6) == (r % 2)
            dwt = _dot_nt(jnp.where(hm, x[:, half], 0.0).astype(BF16), dyb_[:, half])
            q = dwt * st
            for d, (w_end, rhs, maskt) in enumerate(per_dir):
                j = 4 * d + r
                dc = dt[:, j:j + 1]
                lmt = jnp.where(maskt, jnp.exp(lat_[j:j + 1, :] - la[:, j:j + 1]), 0.0)
                ldc = lmt * jnp.broadcast_to(dc, (Q, Q))
                lhs = jnp.concatenate([st * ldc, bm * w_end[:, j:j + 1]], axis=1).astype(BF16)
                dxh[r // 2] = dxh[r // 2] + jnp.where(hm, _dot(lhs, rhs[:, half]), 0.0)
                cs = jnp.sum(q * lmt, axis=1, keepdims=True)
                cdir = jnp.where(lane128 == j, cs, cdir)
                clam = jnp.where(lane128 == j, cs * dc, clam)
                rows = rows + jnp.where(sub == j, jnp.sum(q * ldc, axis=0, keepdims=True), 0.0)
                dst = dst + dwt * ldc
        dxa = jnp.concatenate(dxh, axis=1)
        ddir = ddir + cdir
        dlam = dlam - clam + rows.T
        dba = dba + _dot(dst.astype(BF16), cb_)
        dca = dca + _dot(dst.T.astype(BF16), bb_)
        isb = jnp.logical_and(lane128 >= 4, lane128 < 8)
        ddel = jnp.where(isb, _dot2_l(_tri(True), dlam), _dot2_l(_tri(False), dlam))
        a_l = a_ref[pl.ds(g, 1), :]
        ddt_ref[gg] = ddir + a_l * ddel
        ga_ref[gg] += jnp.broadcast_to(a_l * jnp.sum(dt * ddel, axis=0, keepdims=True), (8, 128))
        dx_ref[:, cols] = dxa + dy_ * sk_ref[:, cols]
        dbo_ref[:, cols128] = dba
        dco_ref[:, cols128] = dca

    nb = NG // gps
    st3 = pl.BlockSpec((None, 128, 256 * gps), lambda g, c: (c, 0, g))
    sm = pl.BlockSpec((gps, Q, 128), lambda g, c: (g, c, 0))
    smt = pl.BlockSpec((8 * gps, Q), lambda g, c: (g, c))
    wide = pl.BlockSpec((Q, 256 * gps), lambda g, c: (c, g))
    return pl.pallas_call(
        kern, name="ssd_bwd_out", grid=(nb, nc),
        in_specs=[wide, pl.BlockSpec((Q, 128 * gps), lambda g, c: (c, 2 * nb + g)),
                  pl.BlockSpec((Q, 128 * gps), lambda g, c: (c, 3 * nb + g)), wide,
                  pl.BlockSpec((1, 256 * gps), lambda g, c: (0, g)), sm, sm, smt, smt, st3, st3, st3, st3,
                  pl.BlockSpec((8, 128), lambda g, c: (0, 0))],
        out_specs=[wide, pl.BlockSpec((Q, 128 * gps), lambda g, c: (c, g)),
                   pl.BlockSpec((Q, 128 * gps), lambda g, c: (c, g)), sm, pl.BlockSpec((gps, 8, 128), lambda g, c: (g, 0, 0))],
        out_shape=[jax.ShapeDtypeStruct((lext, DI), F32), jax.ShapeDtypeStruct((lext, NG * NS), F32),
                   jax.ShapeDtypeStruct((lext, NG * NS), F32), jax.ShapeDtypeStruct((NG, lext, 128), F32),
                   jax.ShapeDtypeStruct((NG, 8, 128), F32)],
        compiler_params=_cp(("parallel", "arbitrary")),
    )(xbc, xbc, xbc, dy, dsk, dtg, lag, dtt, lat, htf, htb, dhf, dhb, a_rows)


def _post_fwd(yssm, xbc, proj_rest, dsk, gnw, nct):
    l = yssm.shape[0]

    def kern(y_ref, x_ref, z_ref, dsk_ref, w_ref, o_ref):
        y = y_ref[...].astype(F32) + dsk_ref[...] * x_ref[...]
        yz = y * _silu(z_ref[...].astype(F32))
        for g in range(NG):
            sl = slice(256 * g, 256 * (g + 1))
            v = yz[:, sl]
            r = lax.rsqrt(jnp.mean(v * v, axis=-1, keepdims=True) + EPS)
            o_ref[:, sl] = (v * r * w_ref[:, sl]).astype(BF16)

    return pl.pallas_call(
        kern, name="post_fwd", grid=(l // T,),
        in_specs=[pl.BlockSpec((T, DI), lambda i: (i, 0)), pl.BlockSpec((T, DI), lambda i: (i + nct, 0)),
                  pl.BlockSpec((T, DI), lambda i: (i, 0)), pl.BlockSpec((1, DI), lambda i: (0, 0)),
                  pl.BlockSpec((1, DI), lambda i: (0, 0))],
        out_specs=pl.BlockSpec((T, DI), lambda i: (i, 0)),
        out_shape=jax.ShapeDtypeStruct((l, DI), BF16),
        compiler_params=_cp(("parallel",)),
    )(yssm, xbc, proj_rest, dsk, gnw)


def _post_bwd(dgn, yssm, xbc, proj_rest, dsk, gnw, dpr, nct):
    l = yssm.shape[0]
    lext = xbc.shape[0]
    xi = lambda i: (jnp.maximum(i - nct, 0), 0)

    def kern(dg_ref, y_ref, x_ref, z_ref, dsk_ref, w_ref, dpr_ref, dy_ref, dz_ref, gw_ref, gd_ref):
        i = pl.program_id(0)

        @pl.when(i == 0)
        def _():
            gw_ref[...] = jnp.zeros_like(gw_ref)
            gd_ref[...] = jnp.zeros_like(gd_ref)

        @pl.when(i < nct)
        def _():
            dy_ref[...] = jnp.zeros_like(dy_ref)

        @pl.when(i >= nct)
        def _():
            xs = x_ref[...]
            z = z_ref[...].astype(F32)
            y = y_ref[...].astype(F32) + dsk_ref[...] * xs
            sz = _silu(z)
            yz = y * sz
            dgn_ = dg_ref[...].astype(F32)
            dyz_parts = []
            gws = []
            for g in range(NG):
                sl = slice(256 * g, 256 * (g + 1))
                v = yz[:, sl]
                r = lax.rsqrt(jnp.mean(v * v, axis=-1, keepdims=True) + EPS)
                vn = v * r
                dn = dgn_[:, sl] * w_ref[:, sl]
                gws.append(jnp.sum(dgn_[:, sl] * vn, axis=0, keepdims=True))
                dyz_parts.append(r * (dn - vn * jnp.mean(dn * vn, axis=-1, keepdims=True)))
            dyz = jnp.concatenate(dyz_parts, axis=1)
            gw_ref[...] += jnp.broadcast_to(jnp.concatenate(gws, axis=1), (8, DI))
            dy = dyz * sz
            dz_ref[...] = (dyz * y * _dsilu(z)).astype(BF16)
            gd_ref[...] += jnp.broadcast_to(jnp.sum(dy * xs, axis=0, keepdims=True), (8, DI))
            dy_ref[...] = dy

    return pl.pallas_call(
        kern, name="post_bwd", grid=(lext // T,),
        in_specs=[pl.BlockSpec((T, DI), xi), pl.BlockSpec((T, DI), xi), pl.BlockSpec((T, DI), lambda i: (i, 0)),
                  pl.BlockSpec((T, DI), xi), pl.BlockSpec((1, DI), lambda i: (0, 0)), pl.BlockSpec((1, DI), lambda i: (0, 0)),
                  pl.BlockSpec(memory_space=pl.ANY)],
        out_specs=[pl.BlockSpec((T, DI), lambda i: (i, 0)),
                   pl.BlockSpec((T, DI), xi), pl.BlockSpec((8, DI), lambda i: (0, 0)), pl.BlockSpec((8, DI), lambda i: (0, 0))],
        out_shape=[jax.ShapeDtypeStruct((lext, DI), F32),
                   jax.ShapeDtypeStruct((l, RESTW), BF16), jax.ShapeDtypeStruct((8, DI), F32), jax.ShapeDtypeStruct((8, DI), F32)],
        input_output_aliases={6: 1},
        compiler_params=_cp(("arbitrary",)),
    )(dgn, yssm, xbc, proj_rest, dsk, gnw, dpr)


C_G1, C_G2, C_GA, C_GB, C_CG = 2, 3, 4, 5, 6
PITCH = GW + 16
NROW = T // GW


GAP = PITCH - GW
PADR = GAP + NROW * PITCH
NSTRIP = D // 128


def _fill_padded(pad8, val):
    z = jnp.zeros((GAP, D), F32)
    parts = [z]
    for r in range(NROW):
        parts += [val[GW * r:GW * (r + 1)], z]
    p = jnp.concatenate(parts, axis=0)
    pad8[0] = p
    for j in range(1, pad8.shape[0]):
        pad8[j] = pltpu.roll(p, PADR - j, axis=0)


def _tap(pad8, base, off, ln):
    return pad8[off % 8, pl.ds(base + off - off % 8, GW), ln]


def _row_conv(out_ref, pad8, w_ref, transpose):
    def strip(s, carry):
        ln = pl.ds(pl.multiple_of(s * 128, 128), 128)
        for r in range(NROW):
            base = GAP + PITCH * r
            acc = jnp.zeros((GW, 128), F32)
            for k in range(CK):
                off = (k - 15) if not transpose else (15 - k)
                acc = acc + w_ref[pl.ds(k, 1), ln] * _tap(pad8, base, off, ln)
            out_ref[pl.ds(GW * r, GW), ln] = acc
        return carry

    lax.fori_loop(0, NSTRIP, strip, 0)


def _row_conv_wgrad(gcw_ref, padd8, pada8):
    def strip(s, carry):
        ln = pl.ds(pl.multiple_of(s * 128, 128), 128)
        rid = _iota((32, 128), 0)
        g = jnp.zeros((32, 128), F32)
        for k0 in range(0, CK, 8):
            taps = range(k0, min(k0 + 8, CK))
            accs = {k: jnp.zeros((8, 128), F32) for k in taps}
            for r in range(NROW):
                base = GAP + PITCH * r
                d = _tap(padd8, base, 0, ln)
                for k in taps:
                    p = d * pada8[0, pl.ds(base + k - 15, GW), ln]
                    part = p[0:8]
                    for q in range(1, GW // 8):
                        part = part + p[8 * q:8 * (q + 1)]
                    accs[k] = accs[k] + part
            for k in taps:
                g = jnp.where(rid == k, jnp.sum(accs[k], axis=0, keepdims=True), g)
        gcw_ref[:, ln] += g
        return carry

    lax.fori_loop(0, NSTRIP, strip, 0)


def _ln_stats(cv):
    mu = jnp.mean(cv, axis=-1, keepdims=True)
    xc = cv - mu
    rs = lax.rsqrt(jnp.mean(xc * xc, axis=-1, keepdims=True) + EPS)
    return xc * rs, rs


def _conf_fwd(proj_rest, cw, cb, lw, lb):
    l = proj_rest.shape[0]

    def kern(ga_ref, gb_ref, cg_ref, cw_ref, cb_ref, lw_ref, lb_ref, o_ref, cv_ref, pad8):
        _fill_padded(pad8, ga_ref[...].astype(F32) * _sig(gb_ref[...].astype(F32)))
        _row_conv(cv_ref, pad8, cw_ref, False)
        cv = cv_ref[...] + cb_ref[...]
        cv_ref[...] = cv
        xh, _ = _ln_stats(cv)
        ln = xh * lw_ref[...] + lb_ref[...]
        o_ref[...] = (_silu(ln) * _silu(cg_ref[...].astype(F32))).astype(BF16)

    vec = pl.BlockSpec((1, D), lambda i: (0, 0))
    blk = pl.BlockSpec((T, D), lambda i: (i, 0))
    return pl.pallas_call(
        kern, name="conf_fwd", grid=(l // T,),
        in_specs=[pl.BlockSpec((T, D), lambda i: (i, C_GA)), pl.BlockSpec((T, D), lambda i: (i, C_GB)),
                  pl.BlockSpec((T, D), lambda i: (i, C_CG)), pl.BlockSpec((32, D), lambda i: (0, 0)), vec, vec, vec],
        out_specs=[blk, blk],
        out_shape=[jax.ShapeDtypeStruct((l, D), BF16), jax.ShapeDtypeStruct((l, D), F32)],
        scratch_shapes=[pltpu.VMEM((8, PADR, D), F32)],
        compiler_params=_cp(("parallel",)),
    )(proj_rest, proj_rest, proj_rest, cw, cb, lw, lb)


def _conf_bwd(duc, cv, proj_rest, cw, lw, lb, dpr):
    l = proj_rest.shape[0]

    def kern(du_ref, cv_ref, ga_ref, gb_ref, cg_ref, cw_ref, lw_ref, lb_ref, dpr_ref, o_ref, gcw_ref, gv_ref, sc,
             pada, padd, da_ref):
        i, j = pl.program_id(0), pl.program_id(1)

        @pl.when(jnp.logical_and(i == 0, j == 0))
        def _():
            gcw_ref[...] = jnp.zeros_like(gcw_ref)
            gv_ref[...] = jnp.zeros_like(gv_ref)

        @pl.when(j == 0)
        def _():
            ga, gb, cg = ga_ref[...].astype(F32), gb_ref[...].astype(F32), cg_ref[...].astype(F32)
            sg = _sig(gb)
            xh, rs = _ln_stats(cv_ref[...])
            ln = xh * lw_ref[...] + lb_ref[...]
            du = du_ref[...].astype(F32)
            sc[:, 2 * D:3 * D] = (du * _silu(ln) * _dsilu(cg)).astype(BF16)
            dln = du * _silu(cg) * _dsilu(ln)
            g_lw = jnp.sum(dln * xh, axis=0, keepdims=True)
            g_lb = jnp.sum(dln, axis=0, keepdims=True)
            dxh = dln * lw_ref[...]
            dcv = rs * (dxh - jnp.mean(dxh, axis=-1, keepdims=True) - xh * jnp.mean(dxh * xh, axis=-1, keepdims=True))
            g_cb = jnp.sum(dcv, axis=0, keepdims=True)
            rid = _iota((8, D), 0)
            gv_ref[...] += jnp.where(rid == 0, g_cb, jnp.where(rid == 1, g_lw, jnp.where(rid == 2, g_lb, 0.0)))
            _fill_padded(padd, dcv)
            _fill_padded(pada, ga * sg)
            _row_conv(da_ref, padd, cw_ref, True)
            _row_conv_wgrad(gcw_ref, padd, pada)
            da = da_ref[...]
            sc[:, 0:D] = (da * sg).astype(BF16)
            sc[:, D:2 * D] = (da * ga * sg * (1.0 - sg)).astype(BF16)

        o_ref[...] = sc[:, pl.ds(pl.multiple_of(j * D, 128), D)]

    vec = pl.BlockSpec((1, D), lambda i, j: (0, 0))
    col = lambda c: pl.BlockSpec((T, D), lambda i, j: (i, c))
    return pl.pallas_call(
        kern, name="conf_bwd", grid=(l // T, 3),
        in_specs=[col(0), col(0), col(C_GA), col(C_GB), col(C_CG), pl.BlockSpec((32, D), lambda i, j: (0, 0)), vec, vec,
                  pl.BlockSpec(memory_space=pl.ANY)],
        out_specs=[pl.BlockSpec((T, D), lambda i, j: (i, C_GA + j)), pl.BlockSpec((32, D), lambda i, j: (0, 0)),
                   pl.BlockSpec((8, D), lambda i, j: (0, 0))],
        out_shape=[jax.ShapeDtypeStruct((l, RESTW), BF16), jax.ShapeDtypeStruct((32, D), F32),
                   jax.ShapeDtypeStruct((8, D), F32)],
        scratch_shapes=[pltpu.VMEM((T, 3 * D), BF16), pltpu.VMEM((1, PADR, D), F32), pltpu.VMEM((8, PADR, D), F32),
                        pltpu.VMEM((T, D), F32)],
        input_output_aliases={8: 0},
        compiler_params=_cp(("arbitrary", "arbitrary")),
    )(duc, cv, proj_rest, proj_rest, proj_rest, cw, lw, lb, dpr)


def _merge_fwd(bs, bc, proj_rest):
    l = bs.shape[0]

    def kern(bs_ref, bc_ref, g1_ref, g2_ref, o_ref):
        up = lambda r: r[...].astype(F32)
        o_ref[...] = (_sig(up(g1_ref)) * up(bs_ref) + _sig(up(g2_ref)) * up(bc_ref)).astype(BF16)

    blk = pl.BlockSpec((TX, D), lambda i: (i, 0))
    return pl.pallas_call(
        kern, name="merge_fwd", grid=(l // TX,),
        in_specs=[blk, blk, pl.BlockSpec((TX, D), lambda i: (i, C_G1)), pl.BlockSpec((TX, D), lambda i: (i, C_G2))],
        out_specs=blk, out_shape=jax.ShapeDtypeStruct((l, D), BF16),
        compiler_params=_cp(("parallel",)),
    )(bs, bc, proj_rest, proj_rest)


def _merge_bwd(dm, bs, bc, proj_rest):
    l = bs.shape[0]

    def kern(dm_ref, bs_ref, bc_ref, g1_ref, g2_ref, dbs_ref, dbc_ref, dg_ref):
        up = lambda r: r[...].astype(F32)
        dm_ = up(dm_ref)
        s1, s2 = _sig(up(g1_ref)), _sig(up(g2_ref))
        dbs_ref[...] = (dm_ * s1).astype(BF16)
        dbc_ref[...] = (dm_ * s2).astype(BF16)
        dg_ref[:, 0:D] = (dm_ * up(bs_ref) * s1 * (1.0 - s1)).astype(BF16)
        dg_ref[:, D:2 * D] = (dm_ * up(bc_ref) * s2 * (1.0 - s2)).astype(BF16)

    blk = pl.BlockSpec((TX, D), lambda i: (i, 0))
    return pl.pallas_call(
        kern, name="merge_bwd", grid=(l // TX,),
        in_specs=[blk, blk, blk, pl.BlockSpec((TX, D), lambda i: (i, C_G1)), pl.BlockSpec((TX, D), lambda i: (i, C_G2))],
        out_specs=[blk, blk, pl.BlockSpec((TX, 2 * D), lambda i: (i, 1))],
        out_shape=[jax.ShapeDtypeStruct((l, D), BF16), jax.ShapeDtypeStruct((l, D), BF16),
                   jax.ShapeDtypeStruct((l, RESTW), BF16)],
        compiler_params=_cp(("parallel",)),
    )(dm, bs, bc, proj_rest, proj_rest)


def _final(x, out, tgt, mod, fw):
    l = x.shape[0]

    def kern(x_ref, o_ref, t_ref, mod_ref, fw_ref, ls_ref, dx2_ref, do_ref, gv_ref):
        @pl.when(pl.program_id(0) == 0)
        def _():
            ls_ref[...] = jnp.zeros_like(ls_ref)
            gv_ref[...] = jnp.zeros_like(gv_ref)

        gate = mod_ref[0:1, 2 * D:3 * D]
        o = o_ref[...]
        x2 = x_ref[...] + gate * o
        r = lax.rsqrt(jnp.mean(x2 * x2, axis=-1, keepdims=True) + EPS)
        yn = x2 * r
        fw_ = fw_ref[...]
        e = yn * fw_ - t_ref[...]
        ls_ref[...] += jnp.full((8, 128), 1.0, F32) * (0.5 / D) * jnp.sum(e * e)
        dy = e * (1.0 / D)
        g_fw = jnp.sum(dy * yn, axis=0, keepdims=True)
        dyn = dy * fw_
        dx2 = r * (dyn - yn * jnp.mean(dyn * yn, axis=-1, keepdims=True))
        g_gate = jnp.sum(dx2 * o, axis=0, keepdims=True)
        rid = _iota((8, D), 0)
        gv_ref[...] += jnp.where(rid == 0, g_fw, jnp.where(rid == 1, g_gate, 0.0))
        dx2_ref[...] = dx2
        do_ref[...] = (dx2 * gate).astype(BF16)

    blk = pl.BlockSpec((TX, D), lambda i: (i, 0))
    return pl.pallas_call(
        kern, name="final", grid=(l // TX,),
        in_specs=[blk, blk, blk, pl.BlockSpec((8, 3 * D), lambda i: (0, 0)), pl.BlockSpec((1, D), lambda i: (0, 0))],
        out_specs=[pl.BlockSpec((8, 128), lambda i: (0, 0)), blk, blk, pl.BlockSpec((8, D), lambda i: (0, 0))],
        out_shape=[jax.ShapeDtypeStruct((8, 128), F32), jax.ShapeDtypeStruct((l, D), F32),
                   jax.ShapeDtypeStruct((l, D), BF16), jax.ShapeDtypeStruct((8, D), F32)],
        compiler_params=_cp(("arbitrary",)),
    )(x, out, tgt, mod, fw)


def _perm_dt_cols(w):
    s = w.shape[:-1]
    return w.reshape(*s, 2, NG, HPG).swapaxes(-3, -2).reshape(*s, 64)


def _unperm_dt_cols(w):
    s = w.shape[:-1]
    return w.reshape(*s, NG, 2, HPG).swapaxes(-3, -2).reshape(*s, 64)


def _pad_lanes(v, width):
    return jnp.pad(v, ((0, 0), (0, width - v.shape[1])))


def _vcols(segs, a, b):
    parts, off = [], 0
    for s in segs:
        lo, hi = max(a, off), min(b, off + s.shape[1])
        if lo < hi:
            parts.append(s[:, lo - off:hi - off])
        off += s.shape[1]
    return parts[0] if len(parts) == 1 else jnp.concatenate(parts, axis=1)


def _local_step(x, c, ctx, tgt, w):
    l = x.shape[0]
    nct = CTX // T
    ncc = CTX // Q
    lext = l + CTX

    w_mod = w["w_mod"].astype(BF16)
    wsegs = [s.astype(BF16) for s in (w["w_in"] if isinstance(w["w_in"], (list, tuple)) else [w["w_in"]])]
    w_ssd = jnp.concatenate([_vcols(wsegs, 0, XBC), _perm_dt_cols(_vcols(wsegs, XBC, XBC + 64)), jnp.zeros((D, 64), BF16)], axis=1)
    r0 = XBC + 64
    w_rest = jnp.concatenate([_vcols(wsegs, r0, r0 + DI), _vcols(wsegs, r0 + DI + 3 * D, r0 + RESTW),
                              _vcols(wsegs, r0 + DI, r0 + DI + 3 * D)], axis=1)
    w_os, w_oc, w_o = w["w_out_ssm"].astype(BF16), w["w_out_conf"].astype(BF16), w["w_out"].astype(BF16)
    cw8 = jnp.pad(w["ssm_conv_w"], ((0, 4), (0, 0)))
    cb_s = w["ssm_conv_b"].reshape(1, XBC)
    dtb = _pad_lanes(_perm_dt_cols(w["dt_bias"].reshape(1, 64)), 128)
    a_all = -jnp.exp(w["a_log"].reshape(1, 64))
    a_perm = _pad_lanes(_perm_dt_cols(a_all), 128)
    a_rows = _pad_lanes(_perm_dt_cols(a_all).reshape(NG, 8), 128)
    dsk = jnp.repeat(w["d_skip"].reshape(NH), HP).reshape(1, DI)
    gnw = w["ssm_norm_w"].reshape(1, DI)
    ccw = jnp.pad(w["conf_conv_w"], ((0, 1), (0, 0)))
    ccb, clw, clb = w["conf_conv_b"].reshape(1, D), w["conf_ln_w"].reshape(1, D), w["conf_ln_b"].reshape(1, D)
    nw = w["norm_w"].reshape(1, D)
    fw = w["final_norm_w"].reshape(1, D)
    cc = jnp.concatenate([c.reshape(1, D), w["c_ctx"].reshape(1, D), jnp.zeros((6, D), F32)], axis=0)

    bx = min(1024, l)
    be = 768 if lext % 768 == 0 else 256
    tk = min(1024, l)
    mod = _mod_fwd(cc, w_mod, w["b_mod"].reshape(1, 3 * D))
    h = _norm_fwd(ctx, x, mod, nw, nct)
    hx = h[CTX:]
    proj_ssd = _mm(h, w_ssd, "nn", lext, SSDW, D, be, SSDW // 3, D, F32, "proj_ssd")
    proj_rest = _mm(hx, w_rest, "nn", l, RESTW, D, bx, 1024, D, BF16, "proj_rest")
    xbc = _conv_fwd(proj_ssd, cw8, cb_s, nct)
    dtg, lag, dtt, lat = _dt_fwd(proj_ssd, dtb, a_perm)
    htf, htb = _ssd_state(xbc, dtg, lag, ncc)
    yssm = _ssd_out(xbc, dtg, lag, dtt, lat, htf, htb, ncc)
    gn = _post_fwd(yssm, xbc, proj_rest, dsk, gnw, nct)
    bs = _mm(gn, w_os, "nn", l, D, DI, bx, D, DI, BF16, "out_ssm")
    uc, cv = _conf_fwd(proj_rest, ccw, ccb, clw, clb)
    bc = _mm(uc, w_oc, "nn", l, D, D, bx, D, D, BF16, "out_conf")
    merged = _merge_fwd(bs, bc, proj_rest)
    out = _mm(merged, w_o, "nn", l, D, D, bx, D, D, F32, "out_proj")
    lsum, dx2, dout, gv_fin = _final(x, out, tgt, mod, fw)

    g = {}
    g["final_norm_w"] = gv_fin[0]
    dmerged = _mm(dout, w_o, "nt", l, D, D, bx, D, D, BF16, "d_merged")
    g["w_out"] = _mm(merged, dout, "tn", D, D, l, D, D, tk, F32, "g_w_out")
    dbs, dbc, dpr = _merge_bwd(dmerged, bs, bc, proj_rest)
    dgn = _mm(dbs, w_os, "nt", l, DI, D, bx, DI, D, BF16, "d_gn")
    g["w_out_ssm"] = _mm(gn, dbs, "tn", DI, D, l, DI, D, tk, F32, "g_w_out_ssm")
    duc = _mm(dbc, w_oc, "nt", l, D, D, bx, D, D, BF16, "d_uc")
    g["w_out_conf"] = _mm(uc, dbc, "tn", D, D, l, D, D, tk, F32, "g_w_out_conf")
    dpr, gcw, gv_conf = _conf_bwd(duc, cv, proj_rest, ccw, clw, clb, dpr)
    g["conf_conv_w"] = gcw[:CK]
    g["conf_conv_b"], g["conf_ln_w"], g["conf_ln_b"] = gv_conf[0], gv_conf[1], gv_conf[2]
    dy, dproj_rest, ggnw, gdsk = _post_bwd(dgn, yssm, xbc, proj_rest, dsk, gnw, dpr, nct)
    g["ssm_norm_w"] = ggnw[0]
    g["d_skip"] = gdsk[0].reshape(NH, HP).sum(axis=1)
    dhf, dhb = _ssd_bwd_state(xbc, dy, lag, ncc)
    dxs, dbm, dcm, ddtg, galog = _ssd_bwd_out(xbc, dy, dsk, dtg, lag, dtt, lat, htf, htb, dhf, dhb, a_rows)
    g["a_log"] = _unperm_dt_cols(galog[:, 0, 0:8].reshape(1, 64)).reshape(2, NH)
    dus, gws, gbs = [], [], []
    for dpost, off, width, nm in ((dxs, 0, DI, "conv_bwd_x"), (dbm, DI, NG * NS, "conv_bwd_b"), (dcm, DI + NG * NS, NG * NS, "conv_bwd_c")):
        du_, gw_, gb_ = _conv_bwd(dpost, proj_ssd, cw8, cb_s, off, width, nct, nm)
        dus.append(du_)
        gws.append(gw_[:SK])
        gbs.append(gb_[0])
    g["ssm_conv_w"] = jnp.concatenate(gws, axis=1)
    g["ssm_conv_b"] = jnp.concatenate(gbs, axis=0)
    ddt_raw, gdtb = _dt_bwd(ddtg, proj_ssd, dtb)
    g["dt_bias"] = _unperm_dt_cols(gdtb[0:1, 0:64]).reshape(2, NH)
    dproj_ssd = jnp.concatenate(dus + [ddt_raw], axis=1)
    gw_ssd = _mm(h, dproj_ssd, "tn", D, SSDW, lext, D, SSDW // 3, be, F32, "g_w_ssd")
    gw_rest = _mm(hx, dproj_rest, "tn", D, RESTW, l, D, 1024, tk, F32, "g_w_rest")
    gsegs = [gw_ssd[:, :XBC], _unperm_dt_cols(gw_ssd[:, XBC:XBC + 64]), gw_rest[:, :DI], gw_rest[:, 2 * DI:],
             gw_rest[:, DI:2 * DI]]
    g["w_in"] = jnp.concatenate(gsegs, axis=1)
    g["w_in_shards"] = jnp.stack([_vcols(gsegs, R_IN * s, R_IN * (s + 1)) for s in range(NSHARD)])
    dh_a = _mm(dproj_ssd, w_ssd, "nt", lext, D, SSDW, T, D, SSDW, BF16, "dh_ssd")
    dh_b = _mm(dproj_rest, w_rest, "nt", l, D, RESTW, T, D, RESTW, BF16, "dh_rest")
    grad_x, gnw_in, dss = _norm_bwd(dh_a, dh_b, ctx, x, dx2, mod, nw, nct)
    g["norm_w"] = gnw_in[0]
    dmod = jnp.concatenate([jnp.concatenate([dss[0:1], gv_fin[1:2]], axis=1),
                            jnp.concatenate([dss[1:2], jnp.zeros((1, D), F32)], axis=1),
                            jnp.zeros((6, 3 * D), F32)], axis=0)
    gwm, gbm, gcc = _mod_bwd(dmod, cc, cc.T, w_mod)
    g["w_mod"], g["b_mod"], g["c_ctx"] = gwm, gbm[0], gcc[1]
    return lsum[0, 0], grad_x, g


NSHARD = 4
R_MOD, R_IN, R_OS, R_OC, R_O, R_SC, R_CC = 768, 2832, 512, 256, 256, 8, 8
O_MOD = 0
O_OS = O_MOD + R_MOD
O_OC = O_OS + R_OS
O_O = O_OC + R_OC
O_SC = O_O + R_O
O_CC = O_SC + R_SC
PUSED = O_CC + R_CC
PROWS = 1824
HALF = PROWS // 2
RB = HALF // 3
WB = 128
SROWS = 16
SMALL = (("b_mod", 3 * D), ("norm_w", D), ("ssm_conv_b", XBC), ("dt_bias", 64), ("a_log", 64), ("d_skip", NH),
         ("ssm_norm_w", DI), ("conf_conv_b", D), ("conf_ln_w", D), ("conf_ln_b", D), ("final_norm_w", D), ("c_ctx", D))


def _pack_shard(s):
    return jnp.concatenate([s["w_mod"].reshape(R_MOD, D), _pack_rest(s), jnp.zeros((PROWS - PUSED, D), F32)], axis=0)


def _pack_rest(s):
    cc = jnp.pad(s["conf_conv_w"].reshape(1, CK * 256), ((0, 0), (0, R_CC * D - CK * 256))).reshape(R_CC, D)
    return jnp.concatenate([s["w_out_ssm"], s["w_out_conf"], s["w_out"],
                            jnp.pad(s["ssm_conv_w"], ((0, R_SC - SK), (0, 0))), cc], axis=0)


def _unpack_rest(p):
    o = lambda r: r - O_OS
    return {"w_out_ssm": p[o(O_OS):o(O_OC)][None], "w_out_conf": p[o(O_OC):o(O_O)][None], "w_out": p[o(O_O):o(O_SC)][None],
            "ssm_conv_w": p[o(O_SC):o(O_SC) + SK][None],
            "conf_conv_w": p[o(O_CC):o(O_CC) + R_CC].reshape(R_CC * D)[:CK * 256].reshape(1, CK, 256)}


def _shard_cols(a, n):
    return a.reshape(a.shape[0], NSHARD, n).transpose(1, 0, 2)


def _pack_full(g):
    cc = jnp.pad(_shard_cols(g["conf_conv_w"], 256).reshape(NSHARD, CK * 256), ((0, 0), (0, R_CC * D - CK * 256)))
    return jnp.concatenate([_shard_cols(g["w_mod"], R_MOD).reshape(NSHARD, R_MOD, D),
                            g["w_out_ssm"].reshape(NSHARD, R_OS, D), g["w_out_conf"].reshape(NSHARD, R_OC, D),
                            g["w_out"].reshape(NSHARD, R_O, D),
                            jnp.pad(_shard_cols(g["ssm_conv_w"], D), ((0, 0), (0, R_SC - SK), (0, 0))),
                            cc.reshape(NSHARD, R_CC, D), jnp.zeros((NSHARD, PROWS - PUSED, D), F32)], axis=1)


def _unpack_gathered(gm, gw, gs):
    def cols(a, r, n):
        return a.reshape(NSHARD, r, n).transpose(1, 0, 2).reshape(r, NSHARD * n)
    return {"w_mod": cols(gm[:, O_MOD:O_OS], D, R_MOD), "w_in": [gw[s] for s in range(NSHARD)],
            "w_out_ssm": gm[:, O_OS:O_OC].reshape(DI, D), "w_out_conf": gm[:, O_OC:O_O].reshape(D, D),
            "w_out": gm[:, O_O:O_SC].reshape(D, D), "ssm_conv_w": cols(gs[:, 0:SK], SK, D),
            "conf_conv_w": cols(gs[:, R_SC:R_SC + R_CC].reshape(NSHARD, R_CC * D)[:, :CK * 256], CK, 256)}


MESH_ID = pl.DeviceIdType.MESH
ANY = pl.BlockSpec(memory_space=pl.ANY)


def _place():
    x, y, c = lax.axis_index("x"), lax.axis_index("y"), lax.axis_index("c")
    return x, y, c, [(1 - x, y), (x, 1 - y), (1 - x, 1 - y)]


def _rcopy(src, dst, send, recv, dev):
    return pltpu.make_async_remote_copy(src_ref=src, dst_ref=dst, send_sem=send, recv_sem=recv,
                                        device_id=dev, device_id_type=MESH_ID)


def _gather_weights(mats, small):
    n = len(mats)

    def kern(*refs):
        m_refs, s_ref, g_refs, gs_ref, (send, recv) = refs[:n], refs[n], refs[n + 1:2 * n + 1], refs[2 * n + 1], refs[2 * n + 2:]
        x, y, c, chips = _place()
        me = 2 * x + y
        sib = (x, y, 1 - c)
        first, passed = [], []
        for k, (px, py) in enumerate(chips):
            first.append(_rcopy(s_ref, gs_ref.at[me], send.at[k], recv.at[k], (px, py, c)))
            for a, (m_ref, g_ref) in enumerate(zip(m_refs, g_refs)):
                mine = _half_rows(c, m_ref.shape[0])
                first.append(_rcopy(m_ref.at[mine], g_ref.at[me, mine], send.at[3 + 6 * a + k], recv.at[3 + 6 * a + k], (px, py, c)))
        for cp in first:
            cp.start()
        for k, (px, py) in enumerate(chips):
            s = 2 * px + py
            for a, (m_ref, g_ref) in enumerate(zip(m_refs, g_refs)):
                mine = _half_rows(c, m_ref.shape[0])
                _rcopy(m_ref.at[mine], g_ref.at[s, mine], send.at[3 + 6 * a + k], recv.at[3 + 6 * a + k], sib).wait_recv()
                f = _rcopy(g_ref.at[s, mine], g_ref.at[s, mine], send.at[6 + 6 * a + k], recv.at[6 + 6 * a + k], sib)
                f.start()
                passed.append(f)
        for k, (px, py) in enumerate(chips):
            s = 2 * px + py
            _rcopy(s_ref, gs_ref.at[s], send.at[k], recv.at[k], sib).wait_recv()
            for a, g_ref in enumerate(g_refs):
                other = _half_rows(1 - c, g_ref.shape[1])
                _rcopy(g_ref.at[s, other], g_ref.at[s, other], send.at[6 + 6 * a + k], recv.at[6 + 6 * a + k], sib).wait_recv()
        for cp in first + passed:
            cp.wait_send()

    nsem = 3 + 6 * n
    return pl.pallas_call(
        kern, name="gather_weights", in_specs=[ANY] * (n + 1), out_specs=[ANY] * (n + 1),
        out_shape=[jax.ShapeDtypeStruct((NSHARD,) + m.shape, m.dtype) for m in mats]
        + [jax.ShapeDtypeStruct((NSHARD, SROWS, D), F32)],
        scratch_shapes=[pltpu.SemaphoreType.DMA((nsem,)), pltpu.SemaphoreType.DMA((nsem,))],
    )(*mats, small)


def _half_rows(c, rows):
    return pl.ds(pl.multiple_of(c * (rows // 2), 16), rows // 2)


def _swap_halves(gs):
    n = len(gs)

    def kern(*refs):
        g_refs, o_refs, (send, recv) = refs[:n], refs[n:2 * n], refs[2 * n:]
        x, y, c, _ = _place()
        cps = [_rcopy(g_ref.at[s, _half_rows(1 - c, g_ref.shape[1])], o_ref.at[s], send.at[NSHARD * a + s],
                      recv.at[NSHARD * a + s], (x, y, 1 - c))
               for a, (g_ref, o_ref) in enumerate(zip(g_refs, o_refs)) for s in range(NSHARD)]
        for cp in cps:
            cp.start()
        for cp in cps:
            cp.wait()

    return pl.pallas_call(
        kern, name="swap_halves", in_specs=[ANY] * n, out_specs=[ANY] * n,
        out_shape=[jax.ShapeDtypeStruct((NSHARD, g.shape[1] // 2, g.shape[2]), F32) for g in gs],
        scratch_shapes=[pltpu.SemaphoreType.DMA((NSHARD * n,)), pltpu.SemaphoreType.DMA((NSHARD * n,))],
    )(*gs)


def _add_halves(cidx, g, ra, rb, name):
    _, half, cols = ra.shape
    nb = half // rb

    def kern(c_ref, g_ref, a_ref, o_ref):
        o_ref[...] = (g_ref[...] + a_ref[...]).astype(BF16)

    return pl.pallas_call(
        kern, name=name,
        grid_spec=pltpu.PrefetchScalarGridSpec(
            num_scalar_prefetch=1, grid=(NSHARD, nb),
            in_specs=[pl.BlockSpec((None, rb, cols), lambda s, i, c: (s, c[0] * nb + i, 0)),
                      pl.BlockSpec((None, rb, cols), lambda s, i, c: (s, i, 0))],
            out_specs=pl.BlockSpec((None, rb, cols), lambda s, i, c: (s, i, 0))),
        out_shape=jax.ShapeDtypeStruct((NSHARD, half, cols), BF16),
        compiler_params=_cp(("parallel", "parallel")),
    )(cidx, g, ra)


def _exchange_chips(ps):
    n = len(ps)

    def kern(*refs):
        p_refs, o_refs, (send, recv) = refs[:n], refs[n:2 * n], refs[2 * n:]
        x, y, c, chips = _place()
        cps = [_rcopy(p_ref.at[2 * px + py], o_ref.at[k], send.at[3 * a + k], recv.at[3 * a + k], (px, py, c))
               for a, (p_ref, o_ref) in enumerate(zip(p_refs, o_refs)) for k, (px, py) in enumerate(chips)]
        for cp in cps:
            cp.start()
        for cp in cps:
            cp.wait()

    return pl.pallas_call(
        kern, name="exchange_chips", in_specs=[ANY] * n, out_specs=[ANY] * n,
        out_shape=[jax.ShapeDtypeStruct((3,) + p.shape[1:], p.dtype) for p in ps],
        scratch_shapes=[pltpu.SemaphoreType.DMA((3 * n,)), pltpu.SemaphoreType.DMA((3 * n,))],
    )(*ps)


def _add_chips(mc, g, ra, rx, rb, name):
    _, half, cols = ra.shape
    nb = half // rb

    def kern(m_ref, g_ref, a_ref, r0_ref, r1_ref, r2_ref, o_ref):
        own = g_ref[...] + a_ref[...]
        o_ref[...] = ((own + r0_ref[...].astype(F32)) + r1_ref[...].astype(F32)) + r2_ref[...].astype(F32)

    return pl.pallas_call(
        kern, name=name,
        grid_spec=pltpu.PrefetchScalarGridSpec(
            num_scalar_prefetch=1, grid=(nb,),
            in_specs=[pl.BlockSpec((None, rb, cols), lambda i, m: (m[0], m[1] * nb + i, 0)),
                      pl.BlockSpec((None, rb, cols), lambda i, m: (m[0], i, 0))]
            + [pl.BlockSpec((None, rb, cols), functools.partial(lambda i, m, k: (k, i, 0), k=k)) for k in range(3)],
            out_specs=pl.BlockSpec((rb, cols), lambda i, m: (i, 0))),
        out_shape=jax.ShapeDtypeStruct((half, cols), F32),
        compiler_params=_cp(("parallel",)),
    )(mc, g, ra, rx, rx, rx)


def _share_halves(rs):
    n = len(rs)

    def kern(*refs):
        r_refs, o_refs, (send, recv) = refs[:n], refs[n:2 * n], refs[2 * n:]
        x, y, c, _ = _place()
        cps = [_rcopy(r_ref, o_ref, send.at[a], recv.at[a], (x, y, 1 - c))
               for a, (r_ref, o_ref) in enumerate(zip(r_refs, o_refs))]
        for cp in cps:
            cp.start()
        for cp in cps:
            cp.wait()

    return pl.pallas_call(
        kern, name="share_halves", in_specs=[ANY] * n, out_specs=[ANY] * n,
        out_shape=[jax.ShapeDtypeStruct(r.shape, F32) for r in rs],
        scratch_shapes=[pltpu.SemaphoreType.DMA((n,)), pltpu.SemaphoreType.DMA((n,))],
    )(*rs)


SMALL_W = XBC


def _small_update(gs, ws, ms, vs):
    n = len(gs)
    widths = [g.shape[1] for g in gs]
    assert n <= SROWS and max(widths) <= SMALL_W

    def kern(*refs):
        g_refs, w_refs, m_refs, v_refs = (refs[n * i:n * (i + 1)] for i in range(4))
        o_g, o_d, o_m, o_v = (refs[n * (4 + i):n * (5 + i)] for i in range(4))
        buf, send, recv = refs[8 * n:]
        x, y, c, _ = _place()
        me = 4 * x + 2 * y + c
        buf[me] = jnp.zeros((SROWS, SMALL_W), F32)
        for k, g_ref in enumerate(g_refs):
            buf[me, k:k + 1, 0:widths[k]] = g_ref[...]
        cps = []
        for r in range(1, 8):
            peer = (1 - x if r & 4 else x, 1 - y if r & 2 else y, 1 - c if r & 1 else c)
            cps.append(_rcopy(buf.at[me], buf.at[me], send.at[r - 1], recv.at[r - 1], peer))
        for cp in cps:
            cp.start()
        for cp in cps:
            cp.wait()
        acc = buf[0]
        for i in range(1, 8):
            acc = acc + buf[i]
        for k in range(n):
            g_ = acc[k:k + 1, 0:widths[k]]
            m_ = ADAM_B1 * m_refs[k][...] + (1.0 - ADAM_B1) * g_
            v_ = ADAM_B2 * v_refs[k][...] + (1.0 - ADAM_B2) * jnp.square(g_)
            m_hat = m_ / (1.0 - ADAM_B1 ** ADAM_STEP)
            v_hat = v_ / (1.0 - ADAM_B2 ** ADAM_STEP)
            o_g[k][...] = g_
            o_d[k][...] = -ADAM_LR * (m_hat / (jnp.sqrt(v_hat) + ADAM_EPS) + ADAM_WD * w_refs[k][...])
            o_m[k][...] = m_
            o_v[k][...] = v_

    vm = pl.BlockSpec(memory_space=pltpu.VMEM)
    outs = pl.pallas_call(
        kern, name="small_update", in_specs=[vm] * (4 * n), out_specs=[vm] * (4 * n),
        out_shape=[jax.ShapeDtypeStruct((1, wd), F32) for _ in range(4) for wd in widths],
        scratch_shapes=[pltpu.VMEM((8, SROWS, SMALL_W), F32), pltpu.SemaphoreType.DMA((7,)), pltpu.SemaphoreType.DMA((7,))],
    )(*gs, *ws, *ms, *vs)
    return [outs[n * i:n * (i + 1)] for i in range(4)]


def _adamw(g, w, m, v, rb, name):
    rows, cols = g.shape

    def kern(g_ref, w_ref, m_ref, v_ref, d_ref, nm_ref, nv_ref):
        g_ = g_ref[...]
        m_ = ADAM_B1 * m_ref[...] + (1.0 - ADAM_B1) * g_
        v_ = ADAM_B2 * v_ref[...] + (1.0 - ADAM_B2) * jnp.square(g_)
        m_hat = m_ / (1.0 - ADAM_B1 ** ADAM_STEP)
        v_hat = v_ / (1.0 - ADAM_B2 ** ADAM_STEP)
        d_ref[...] = -ADAM_LR * (m_hat / (jnp.sqrt(v_hat) + ADAM_EPS) + ADAM_WD * w_ref[...])
        nm_ref[...] = m_
        nv_ref[...] = v_

    assert rows % rb == 0
    blk = pl.BlockSpec((rb, cols), lambda i: (i, 0))
    return pl.pallas_call(
        kern, name=name, grid=(rows // rb,), in_specs=[blk] * 4, out_specs=[blk] * 3,
        out_shape=[jax.ShapeDtypeStruct((rows, cols), F32)] * 3,
        compiler_params=_cp(("parallel",)),
    )(g, w, m, v)


def _adamw_halves(cidx, mine, other, w, m, v, rb, name):
    rows, cols = w.shape
    nbh = rows // 2 // rb

    def kern(c_ref, a_ref, b_ref, w_ref, m_ref, v_ref, g_ref, d_ref, nm_ref, nv_ref):
        g_ = jnp.where(pl.program_id(0) // nbh == c_ref[0], a_ref[...], b_ref[...])
        m_ = ADAM_B1 * m_ref[...] + (1.0 - ADAM_B1) * g_
        v_ = ADAM_B2 * v_ref[...] + (1.0 - ADAM_B2) * jnp.square(g_)
        m_hat = m_ / (1.0 - ADAM_B1 ** ADAM_STEP)
        v_hat = v_ / (1.0 - ADAM_B2 ** ADAM_STEP)
        g_ref[...] = g_
        d_ref[...] = -ADAM_LR * (m_hat / (jnp.sqrt(v_hat) + ADAM_EPS) + ADAM_WD * w_ref[...])
        nm_ref[...] = m_
        nv_ref[...] = v_

    half = pl.BlockSpec((rb, cols), lambda i, c: (i % nbh, 0))
    blk = pl.BlockSpec((rb, cols), lambda i, c: (i, 0))
    return pl.pallas_call(
        kern, name=name,
        grid_spec=pltpu.PrefetchScalarGridSpec(num_scalar_prefetch=1, grid=(2 * nbh,), in_specs=[half, half, blk, blk, blk],
                                               out_specs=[blk] * 4),
        out_shape=[jax.ShapeDtypeStruct((rows, cols), F32)] * 4,
        compiler_params=_cp(("parallel",)),
    )(cidx, mine, other, w, m, v)


WEIGHTS = ("c_ctx", "w_mod", "b_mod", "norm_w", "w_in", "ssm_conv_w", "ssm_conv_b", "dt_bias", "a_log", "d_skip",
           "ssm_norm_w", "w_out_ssm", "conf_conv_w", "conf_conv_b", "conf_ln_w", "conf_ln_b", "w_out_conf", "w_out",
           "final_norm_w")


def kernel(x, c, ctx, c_ctx, w_mod, b_mod, norm_w, w_in, ssm_conv_w, ssm_conv_b, dt_bias, a_log, d_skip, ssm_norm_w, w_out_ssm, conf_conv_w, conf_conv_b, conf_ln_w, conf_ln_b, w_out_conf, w_out, final_norm_w, loss_target, m_c_ctx, m_w_mod, m_b_mod, m_norm_w, m_w_in, m_ssm_conv_w, m_ssm_conv_b, m_dt_bias, m_a_log, m_d_skip, m_ssm_norm_w, m_w_out_ssm, m_conf_conv_w, m_conf_conv_b, m_conf_ln_w, m_conf_ln_b, m_w_out_conf, m_w_out, m_final_norm_w, v_c_ctx, v_w_mod, v_b_mod, v_norm_w, v_w_in, v_ssm_conv_w, v_ssm_conv_b, v_dt_bias, v_a_log, v_d_skip, v_ssm_norm_w, v_w_out_ssm, v_conf_conv_w, v_conf_conv_b, v_conf_ln_w, v_conf_ln_b, v_w_out_conf, v_w_out, v_final_norm_w):
    wv = (c_ctx, w_mod, b_mod, norm_w, w_in, ssm_conv_w, ssm_conv_b, dt_bias, a_log, d_skip, ssm_norm_w, w_out_ssm,
          conf_conv_w, conf_conv_b, conf_ln_w, conf_ln_b, w_out_conf, w_out, final_norm_w)
    mv = (m_c_ctx, m_w_mod, m_b_mod, m_norm_w, m_w_in, m_ssm_conv_w, m_ssm_conv_b, m_dt_bias, m_a_log, m_d_skip,
          m_ssm_norm_w, m_w_out_ssm, m_conf_conv_w, m_conf_conv_b, m_conf_ln_w, m_conf_ln_b, m_w_out_conf, m_w_out,
          m_final_norm_w)
    vv = (v_c_ctx, v_w_mod, v_b_mod, v_norm_w, v_w_in, v_ssm_conv_w, v_ssm_conv_b, v_dt_bias, v_a_log, v_d_skip,
          v_ssm_norm_w, v_w_out_ssm, v_conf_conv_w, v_conf_conv_b, v_conf_ln_w, v_conf_ln_b, v_w_out_conf, v_w_out,
          v_final_norm_w)
    shapes = {n: a.shape for n, a in zip(WEIGHTS, wv)}

    def squeeze(d):
        return {n: (a if n in ("c_ctx", "final_norm_w") else a[0]) for n, a in d.items()}

    w, m, v = (squeeze(dict(zip(WEIGHTS, t))) for t in (wv, mv, vv))

    my_chip = 2 * lax.axis_index("x") + lax.axis_index("y")
    my_core = lax.axis_index("c")

    pw = _pack_shard(w)
    pwb, wib, psm = pw.astype(BF16), w["w_in"].astype(BF16), pw[O_SC:O_SC + SROWS]
    gm, gw, gs = _gather_weights([pwb, wib], psm)
    mine = (jnp.arange(NSHARD) == my_chip)[:, None, None]
    gm, gw, gs = jnp.where(mine, pwb[None], gm), jnp.where(mine, wib[None], gw), jnp.where(mine, psm[None], gs)
    full = dict(w)
    full.update(_unpack_gathered(gm, gw, gs))

    lsum, grad_x, g = _local_step(x[0], c, ctx[0], loss_target[0], full)
    loss = lax.psum(lsum, ("x", "y", "c"))

    cidx = my_core.astype(jnp.int32).reshape(1)
    mc = jnp.stack([my_chip, my_core]).astype(jnp.int32)
    gsrc = [_pack_full(g), g["w_in_shards"]]
    blocks = (RB, WB)
    sib = _swap_halves(gsrc)
    part = [_add_halves(cidx, a, b, rb, "add_halves_%d" % i) for i, (a, b, rb) in enumerate(zip(gsrc, sib, blocks))]
    far = _exchange_chips(part)
    red = [_add_chips(mc, a, b, f, rb, "add_chips_%d" % i) for i, (a, b, f, rb) in enumerate(zip(gsrc, sib, far, blocks))]
    got = _share_halves(red)
    g_pk = jnp.concatenate([jnp.where(my_core == 0, red[0], got[0]), jnp.where(my_core == 0, got[0], red[0])], axis=0)
    small = [name for name, _ in SMALL]
    as_row = lambda d: [d[name].reshape(1, -1) for name in small]
    res_sm = _small_update(as_row(g), as_row(w), as_row(m), as_row(v))

    gr = {"w_mod": g_pk[O_MOD:O_OS].reshape(D, R_MOD), "rest": g_pk[O_OS:PUSED]}
    wr, mr, vr = ({"w_mod": t["w_mod"], "rest": _pack_rest(t)} for t in (w, m, v))
    res = {k: _adamw(gr[k], wr[k], mr[k], vr[k], rb, "adamw_" + k)
           for k, rb in (("w_mod", 512), ("rest", (PUSED - O_OS) // 2))}
    gr["w_in"], *res["w_in"] = _adamw_halves(cidx, red[1], got[1], w["w_in"], m["w_in"], v["w_in"], WB, "adamw_w_in")

    outs = []
    for i in range(4):
        pick = (lambda k: gr[k]) if i == 0 else (lambda k: res[k][i - 1])
        d = {"w_mod": pick("w_mod")[None], "w_in": pick("w_in")[None]}
        d.update(_unpack_rest(pick("rest")))
        d.update({name: a.reshape(shapes[name]) for name, a in zip(small, res_sm[i])})
        outs.extend(d[n] for n in WEIGHTS)
    return (loss, grad_x[None], *outs)
```

```python
import functools

import jax
import jax.numpy as jnp
from jax import lax
from jax.experimental import pallas as pl
from jax.experimental.pallas import tpu as pltpu

F32, BF16 = jnp.float32, jnp.bfloat16

D = 1024
DI = 2048
NH = 32
HP = 64
NG = 8
HPG = 4
NS = 128
Q = 128
GW = 64
CK = 31
SK = 4
CTX = 256
EPS = 1e-6
XBC = DI + 2 * NG * NS
SSDW = XBC + 128
RESTW = 7168
T = 256
TX = 512
VMEM_LIMIT = 56 * 1024 * 1024

ADAM_LR, ADAM_B1, ADAM_B2, ADAM_EPS, ADAM_WD, ADAM_STEP = 0.001, 0.9, 0.999, 1e-08, 0.01, 10


def _cp(sem):
    return pltpu.CompilerParams(dimension_semantics=sem, vmem_limit_bytes=VMEM_LIMIT)


def _sig(x):
    return jax.nn.sigmoid(x)


def _silu(x):
    return x * _sig(x)


def _dsilu(x):
    s = _sig(x)
    return s * (1.0 + x * (1.0 - s))


def _dot(a, b):
    return jnp.dot(a, b, preferred_element_type=F32)


def _dot_nt(a, b):
    return lax.dot_general(a, b, (((1,), (1,)), ((), ())), preferred_element_type=F32)


def _split3(x):
    h = x.astype(BF16)
    r = x - h.astype(F32)
    m = r.astype(BF16)
    l = (r - m.astype(F32)).astype(BF16)
    return h, m, l


def _dot3_l(sel, x):
    h, m, l = _split3(x)
    return _dot(sel, h) + _dot(sel, m) + _dot(sel, l)


def _dot3_r(x, sel):
    h, m, l = _split3(x)
    return _dot(h, sel) + _dot(m, sel) + _dot(l, sel)


def _split2(x):
    h = x.astype(BF16)
    return h, (x - h.astype(F32)).astype(BF16)


def _dot2_l(sel, x):
    h, l = _split2(x)
    return _dot(sel, h) + _dot(sel, l)


def _dot2_r(x, sel):
    h, l = _split2(x)
    return _dot(h, sel) + _dot(l, sel)


def _iota(shape, dim):
    return lax.broadcasted_iota(jnp.int32, shape, dim)


def _mm(a, b, dims, m, n, k, bm, bn, bk, out_dtype, name):
    nk = k // bk
    assert m % bm == 0 and n % bn == 0 and k % bk == 0, (name, m, n, k, bm, bn, bk)

    def prod(a_ref, b_ref):
        av = a_ref[...].astype(BF16)
        bv = b_ref[...].astype(BF16)
        if dims == "nn":
            return _dot(av, bv)
        if dims == "nt":
            return _dot_nt(av, bv)
        return lax.dot_general(av, bv, (((0,), (0,)), ((), ())), preferred_element_type=F32)

    def kern_one(a_ref, b_ref, o_ref):
        o_ref[...] = prod(a_ref, b_ref).astype(out_dtype)

    def kern_acc(a_ref, b_ref, o_ref, acc):
        kk = pl.program_id(2)

        @pl.when(kk == 0)
        def _():
            acc[...] = jnp.zeros_like(acc)

        acc[...] += prod(a_ref, b_ref)

        @pl.when(kk == nk - 1)
        def _():
            o_ref[...] = acc[...].astype(out_dtype)

    if dims == "nn":
        a_spec = pl.BlockSpec((bm, bk), lambda j, i, kk: (i, kk))
        b_spec = pl.BlockSpec((bk, bn), lambda j, i, kk: (kk, j))
    elif dims == "nt":
        a_spec = pl.BlockSpec((bm, bk), lambda j, i, kk: (i, kk))
        b_spec = pl.BlockSpec((bn, bk), lambda j, i, kk: (j, kk))
    else:
        a_spec = pl.BlockSpec((bk, bm), lambda j, i, kk: (kk, i))
        b_spec = pl.BlockSpec((bk, bn), lambda j, i, kk: (kk, j))
    return pl.pallas_call(
        kern_one if nk == 1 else kern_acc, name=name,
        grid=(n // bn, m // bm, nk),
        in_specs=[a_spec, b_spec],
        out_specs=pl.BlockSpec((bm, bn), lambda j, i, kk: (i, j)),
        out_shape=jax.ShapeDtypeStruct((m, n), out_dtype),
        scratch_shapes=[] if nk == 1 else [pltpu.VMEM((bm, bn), F32)],
        compiler_params=_cp(("parallel", "parallel", "arbitrary")),
    )(a, b)


def _mod_fwd(cc, w_mod, b_mod):
    def kern(cc_ref, w_ref, b_ref, o_ref):
        s = _silu(cc_ref[...]).astype(BF16)
        o_ref[...] = _dot(s, w_ref[...]) + b_ref[...]

    return pl.pallas_call(
        kern, name="mod_fwd", grid=(3,),
        in_specs=[pl.BlockSpec((8, D), lambda j: (0, 0)), pl.BlockSpec((D, D), lambda j: (0, j)),
                  pl.BlockSpec((1, D), lambda j: (0, j))],
        out_specs=pl.BlockSpec((8, D), lambda j: (0, j)),
        out_shape=jax.ShapeDtypeStruct((8, 3 * D), F32),
        compiler_params=_cp(("parallel",)),
    )(cc, w_mod, b_mod)


def _mod_bwd(dmod, cc, cct, w_mod):
    def kern(dm_ref, cc_ref, cct_ref, w_ref, gw_ref, gb_ref, gc_ref):
        kk = pl.program_id(0)
        dm = dm_ref[...]
        sct = _silu(cct_ref[...])
        gw_ref[...] = sct[:, 0:1] * dm[0:1, :] + sct[:, 1:2] * dm[1:2, :]
        gb_ref[...] = jnp.broadcast_to(dm[0:1, :] + dm[1:2, :], dm.shape)

        @pl.when(kk == 0)
        def _():
            gc_ref[...] = jnp.zeros_like(gc_ref)

        gc_ref[...] += _dot_nt(dm.astype(BF16), w_ref[...])

        @pl.when(kk == 2)
        def _():
            gc_ref[...] = gc_ref[...] * _dsilu(cc_ref[...])

    return pl.pallas_call(
        kern, name="mod_bwd", grid=(3,),
        in_specs=[pl.BlockSpec((8, D), lambda j: (0, j)), pl.BlockSpec((8, D), lambda j: (0, 0)),
                  pl.BlockSpec((D, 8), lambda j: (0, 0)), pl.BlockSpec((D, D), lambda j: (0, j))],
        out_specs=[pl.BlockSpec((D, D), lambda j: (0, j)), pl.BlockSpec((8, D), lambda j: (0, j)),
                   pl.BlockSpec((8, D), lambda j: (0, 0))],
        out_shape=[jax.ShapeDtypeStruct((D, 3 * D), F32), jax.ShapeDtypeStruct((8, 3 * D), F32),
                   jax.ShapeDtypeStruct((8, D), F32)],
        compiler_params=_cp(("arbitrary",)),
    )(dmod, cc, cct, w_mod)


def _ext_specs(nct):
    return (pl.BlockSpec((T, D), lambda i: (jnp.minimum(i, nct - 1), 0)),
            pl.BlockSpec((T, D), lambda i: (jnp.maximum(i - nct, 0), 0)))


def _norm_fwd(ctx, xl, mod, nw, nct):
    lext = ctx.shape[0] + xl.shape[0]

    def kern(c_ref, x_ref, mod_ref, nw_ref, h_ref):
        is_ctx = pl.program_id(0) < nct
        x = jnp.where(is_ctx, c_ref[...], x_ref[...])
        r = lax.rsqrt(jnp.mean(x * x, axis=-1, keepdims=True) + EPS)
        xn = x * r * nw_ref[...]
        shift = jnp.where(is_ctx, mod_ref[1:2, 0:D], mod_ref[0:1, 0:D])
        scale = jnp.where(is_ctx, mod_ref[1:2, D:2 * D], mod_ref[0:1, D:2 * D])
        h_ref[...] = (xn * (1.0 + scale) + shift).astype(BF16)

    return pl.pallas_call(
        kern, name="norm_fwd", grid=(lext // T,),
        in_specs=[*_ext_specs(nct), pl.BlockSpec((8, 3 * D), lambda i: (0, 0)),
                  pl.BlockSpec((1, D), lambda i: (0, 0))],
        out_specs=pl.BlockSpec((T, D), lambda i: (i, 0)),
        out_shape=jax.ShapeDtypeStruct((lext, D), BF16),
        compiler_params=_cp(("parallel",)),
    )(ctx, xl, mod, nw)


def _norm_bwd(dha, dhb, ctx, xl, dx2, mod, nw, nct):
    lext = ctx.shape[0] + xl.shape[0]
    ntl = lext // T

    def kern(dha_ref, dhb_ref, c_ref, x_ref, dx2_ref, mod_ref, nw_ref, gx_ref, gnw_ref, dss_ref):
        i = pl.program_id(0)
        is_ctx = i < nct

        @pl.when(i == 0)
        def _():
            gnw_ref[...] = jnp.zeros_like(gnw_ref)
            dss_ref[...] = jnp.zeros_like(dss_ref)

        x = jnp.where(is_ctx, c_ref[...], x_ref[...])
        dh_ = dha_ref[...].astype(F32) + jnp.where(is_ctx, 0.0, dhb_ref[...].astype(F32))
        nw_ = nw_ref[...]
        r = lax.rsqrt(jnp.mean(x * x, axis=-1, keepdims=True) + EPS)
        xn = x * r
        scale = jnp.where(is_ctx, mod_ref[1:2, D:2 * D], mod_ref[0:1, D:2 * D])
        dsh = jnp.sum(dh_, axis=0, keepdims=True)
        dsc = jnp.sum(dh_ * (xn * nw_), axis=0, keepdims=True)
        row = jnp.concatenate([dsh, dsc], axis=1)
        rid = _iota((8, 2 * D), 0)
        dss_ref[...] += jnp.where(rid == jnp.where(is_ctx, 1, 0), row, 0.0)
        dxnw = dh_ * (1.0 + scale)
        gnw_ref[...] += jnp.broadcast_to(jnp.sum(dxnw * xn, axis=0, keepdims=True), (8, D))
        dxn = dxnw * nw_
        dx = r * (dxn - xn * jnp.mean(dxn * xn, axis=-1, keepdims=True))
        gx_ref[...] = dx2_ref[...] + dx

    return pl.pallas_call(
        kern, name="norm_bwd", grid=(ntl,),
        in_specs=[pl.BlockSpec((T, D), lambda i: (i, 0)), pl.BlockSpec((T, D), lambda i: (jnp.maximum(i - nct, 0), 0)),
                  *_ext_specs(nct),
                  pl.BlockSpec((T, D), lambda i: (jnp.maximum(i - nct, 0), 0)),
                  pl.BlockSpec((8, 3 * D), lambda i: (0, 0)), pl.BlockSpec((1, D), lambda i: (0, 0))],
        out_specs=[pl.BlockSpec((T, D), lambda i: (jnp.maximum(i - nct, 0), 0)),
                   pl.BlockSpec((8, D), lambda i: (0, 0)), pl.BlockSpec((8, 2 * D), lambda i: (0, 0))],
        out_shape=[jax.ShapeDtypeStruct((lext - nct * T, D), F32), jax.ShapeDtypeStruct((8, D), F32),
                   jax.ShapeDtypeStruct((8, 2 * D), F32)],
        compiler_params=_cp(("arbitrary",)),
    )(dha, dhb, ctx, xl, dx2, mod, nw)


CB = 1024


def _halo_specs(width_blk, col_off_blocks, ntl):
    t8 = T // 8
    main = pl.BlockSpec((T, width_blk), lambda j, i: (i, j + col_off_blocks))
    prev = pl.BlockSpec((8, width_blk), lambda j, i: (jnp.maximum(i * t8 - 1, 0), j + col_off_blocks))
    nxt = pl.BlockSpec((8, width_blk), lambda j, i: (jnp.minimum((i + 1) * t8, ntl * t8 - 1), j + col_off_blocks))
    return main, prev, nxt


def _seq_edges(i, nct, ntl):
    starts = jnp.logical_or(i == 0, i == nct)
    ends = jnp.logical_or(i == nct - 1, i == ntl - 1)
    return starts, ends


def _shifted(ext, off):
    n = ext.shape[0]
    return pltpu.roll(ext, (-off) % n, axis=0)[8:8 + T]


def _conv_fwd(proj_ssd, cw, cb, nct):
    lext = proj_ssd.shape[0]
    ntl = lext // T

    def kern(u_ref, up_ref, un_ref, w_ref, b_ref, o_ref):
        i = pl.program_id(1)
        starts, ends = _seq_edges(i, nct, ntl)
        up = jnp.where(starts, 0.0, up_ref[...])
        un = jnp.where(ends, 0.0, un_ref[...])
        ext = jnp.concatenate([up, u_ref[...], un], axis=0)
        w = w_ref[...]
        pre = b_ref[...] + w[0:1] * _shifted(ext, -2) + w[1:2] * _shifted(ext, -1) \
            + w[2:3] * u_ref[...] + w[3:4] * _shifted(ext, 1)
        o_ref[...] = _silu(pre)

    cbf = 2 * CB
    main, prev, nxt = _halo_specs(cbf, 0, ntl)
    return pl.pallas_call(
        kern, name="conv_fwd", grid=(XBC // cbf, ntl),
        in_specs=[main, prev, nxt, pl.BlockSpec((8, cbf), lambda j, i: (0, j)), pl.BlockSpec((1, cbf), lambda j, i: (0, j))],
        out_specs=pl.BlockSpec((T, cbf), lambda j, i: (i, j)),
        out_shape=jax.ShapeDtypeStruct((lext, XBC), F32),
        compiler_params=_cp(("parallel", "parallel")),
    )(proj_ssd, proj_ssd, proj_ssd, cw, cb)


def _conv_bwd(dpost, proj_ssd, cw, cb, col_off, width, nct, name):
    lext = proj_ssd.shape[0]
    ntl = lext // T
    bw = min(width, 2 * CB)
    assert col_off % bw == 0 and width % bw == 0
    cob = col_off // bw

    def kern(u_ref, up_ref, un_ref, d_ref, dp_ref, dn_ref, w_ref, b_ref, du_ref, gw_ref, gb_ref):
        i = pl.program_id(1)

        @pl.when(i == 0)
        def _():
            gw_ref[...] = jnp.zeros_like(gw_ref)
            gb_ref[...] = jnp.zeros_like(gb_ref)

        starts, ends = _seq_edges(i, nct, ntl)
        ext = jnp.concatenate([jnp.where(starts, 0.0, up_ref[...]), u_ref[...], jnp.where(ends, 0.0, un_ref[...])], axis=0)
        dext = jnp.concatenate([jnp.where(starts, 0.0, dp_ref[...]), d_ref[...], jnp.where(ends, 0.0, dn_ref[...])], axis=0)
        w = w_ref[...]
        n = ext.shape[0]
        pre = b_ref[...] + w[0:1] * pltpu.roll(ext, 2, axis=0) + w[1:2] * pltpu.roll(ext, 1, axis=0) \
            + w[2:3] * ext + w[3:4] * pltpu.roll(ext, n - 1, axis=0)
        dpre = dext * _dsilu(pre)
        dm = dpre[8:8 + T]
        du = w[0:1] * _shifted(dpre, 2) + w[1:2] * _shifted(dpre, 1) + w[2:3] * dm + w[3:4] * _shifted(dpre, -1)
        du_ref[...] = du.astype(BF16)
        g0 = jnp.sum(dm * _shifted(ext, -2), axis=0, keepdims=True)
        g1 = jnp.sum(dm * _shifted(ext, -1), axis=0, keepdims=True)
        g2 = jnp.sum(dm * u_ref[...], axis=0, keepdims=True)
        g3 = jnp.sum(dm * _shifted(ext, 1), axis=0, keepdims=True)
        rid = _iota((8, bw), 0)
        gw_ref[...] += jnp.where(rid == 0, g0, jnp.where(rid == 1, g1, jnp.where(rid == 2, g2, jnp.where(rid == 3, g3, 0.0))))
        gb_ref[...] += jnp.broadcast_to(jnp.sum(dm, axis=0, keepdims=True), (8, bw))

    main, prev, nxt = _halo_specs(bw, cob, ntl)
    dmain, dprev, dnxt = _halo_specs(bw, 0, ntl)
    return pl.pallas_call(
        kern, name=name, grid=(width // bw, ntl),
        in_specs=[main, prev, nxt, dmain, dprev, dnxt,
                  pl.BlockSpec((8, bw), lambda j, i: (0, j + cob)), pl.BlockSpec((1, bw), lambda j, i: (0, j + cob))],
        out_specs=[pl.BlockSpec((T, bw), lambda j, i: (i, j)), pl.BlockSpec((8, bw), lambda j, i: (0, j)),
                   pl.BlockSpec((8, bw), lambda j, i: (0, j))],
        out_shape=[jax.ShapeDtypeStruct((lext, width), BF16), jax.ShapeDtypeStruct((8, width), F32),
                   jax.ShapeDtypeStruct((8, width), F32)],
        compiler_params=_cp(("parallel", "arbitrary")),
    )(proj_ssd, proj_ssd, proj_ssd, dpost, dpost, dpost, cw, cb)


def _tri(lower):
    r, c = _iota((Q, Q), 0), _iota((Q, Q), 1)
    return jnp.where((c <= r) if lower else (c >= r), 1.0, 0.0).astype(BF16)


def _is_bdir_lane(shape):
    ln = _iota(shape, len(shape) - 1)
    return jnp.logical_and(((ln >> 2) & 1) == 1, ln < 64)


def _dt_fwd(proj_ssd, dtb, av):
    lext = proj_ssd.shape[0]

    def kern(p_ref, b_ref, a_ref, dtg_ref, lag_ref, dtt_ref, lat_ref):
        lane = _iota((T, 128), 1)
        raw = p_ref[...] + b_ref[...]
        dt = jnp.where(lane < 64, jnp.maximum(raw, 0.0) + jnp.log1p(jnp.exp(-jnp.abs(raw))), 0.0)
        dta = dt * a_ref[...]
        tl, tu = _tri(True), _tri(False)
        isb = _is_bdir_lane((Q, 128))
        las = []
        for qq in range(T // Q):
            blk = dta[qq * Q:(qq + 1) * Q]
            las.append(jnp.where(isb, _dot3_l(tu, blk), _dot3_l(tl, blk)))
        la = jnp.concatenate(las, axis=0)
        for g in range(NG):
            sh = (128 - 8 * g) % 128
            dtg_ref[g] = jnp.where(lane < 8, pltpu.roll(dt, sh, axis=1) if sh else dt, 0.0)
            lag_ref[g] = jnp.where(lane < 8, pltpu.roll(la, sh, axis=1) if sh else la, 0.0)
        dtt_ref[...] = dt.T[0:64]
        lat_ref[...] = la.T[0:64]

    return pl.pallas_call(
        kern, name="dt_fwd", grid=(lext // T,),
        in_specs=[pl.BlockSpec((T, 128), lambda i: (i, XBC // 128)), pl.BlockSpec((1, 128), lambda i: (0, 0)),
                  pl.BlockSpec((1, 128), lambda i: (0, 0))],
        out_specs=[pl.BlockSpec((NG, T, 128), lambda i: (0, i, 0)), pl.BlockSpec((NG, T, 128), lambda i: (0, i, 0)),
                   pl.BlockSpec((64, T), lambda i: (0, i)), pl.BlockSpec((64, T), lambda i: (0, i))],
        out_shape=[jax.ShapeDtypeStruct((NG, lext, 128), F32), jax.ShapeDtypeStruct((NG, lext, 128), F32),
                   jax.ShapeDtypeStruct((64, lext), F32), jax.ShapeDtypeStruct((64, lext), F32)],
        compiler_params=_cp(("parallel",)),
    )(proj_ssd, dtb, av)


def _dt_bwd(ddtg, proj_ssd, dtb):
    lext = proj_ssd.shape[0]

    def kern(d_ref, p_ref, b_ref, o_ref, gb_ref):
        @pl.when(pl.program_id(0) == 0)
        def _():
            gb_ref[...] = jnp.zeros_like(gb_ref)

        acc = d_ref[0]
        for g in range(1, NG):
            acc = acc + pltpu.roll(d_ref[g], 8 * g, axis=1)
        draw = acc * _sig(p_ref[...] + b_ref[...])
        o_ref[...] = draw.astype(BF16)
        gb_ref[...] += jnp.broadcast_to(jnp.sum(draw, axis=0, keepdims=True), (8, 128))

    return pl.pallas_call(
        kern, name="dt_bwd", grid=(lext // T,),
        in_specs=[pl.BlockSpec((NG, T, 128), lambda i: (0, i, 0)), pl.BlockSpec((T, 128), lambda i: (i, XBC // 128)),
                  pl.BlockSpec((1, 128), lambda i: (0, 0))],
        out_specs=[pl.BlockSpec((T, 128), lambda i: (i, 0)), pl.BlockSpec((8, 128), lambda i: (0, 0))],
        out_shape=[jax.ShapeDtypeStruct((lext, 128), BF16), jax.ShapeDtypeStruct((8, 128), F32)],
        compiler_params=_cp(("arbitrary",)),
    )(ddtg, proj_ssd, dtb)


def _expand_sel(d):
    r, c = _iota((128, 256), 0), _iota((128, 256), 1)
    return jnp.where(r == 4 * d + (c >> 6), 1.0, 0.0).astype(BF16)


def _reduce_sel(d):
    r, c = _iota((256, 128), 0), _iota((256, 128), 1)
    return jnp.where(c == 4 * d + (r >> 6), 1.0, 0.0).astype(BF16)


def _chunk_of_bwd_dir(j, ncc, nc):
    return jnp.where(j < ncc, ncc - 1 - j, nc + ncc - 1 - j)


def _dir_terms(la, dt, d):
    lane = _iota(la.shape, 1)
    mine = jnp.logical_and(lane >= 4 * d, lane < 4 * d + 4)
    la = jnp.where(mine, la, 0.0)
    tot = la[Q - 1:Q] if d == 0 else la[0:1]
    wnd = jnp.exp(tot - la)
    return tot, wnd * jnp.where(mine, dt, 0.0), wnd


def _ssd_state(xbc, dtg, lag, ncc):
    lext = xbc.shape[0]
    nc = lext // Q

    def kern(xf_ref, bf_ref, dtf_ref, laf_ref, xb_ref, bb_ref, dtb_ref, lab_ref, hf_ref, hb_ref, sf, sb):
        @pl.when(pl.program_id(0) == 0)
        def _():
            sf[...] = jnp.zeros_like(sf)
            sb[...] = jnp.zeros_like(sb)

        for d, (x_ref, b_ref, dt_ref, la_ref, h_ref, s) in enumerate(
                ((xf_ref, bf_ref, dtf_ref, laf_ref, hf_ref, sf), (xb_ref, bb_ref, dtb_ref, lab_ref, hb_ref, sb))):
            h_ref[...] = s[...]
            ex = _expand_sel(d)
            for g in range(NG):
                cols = slice(256 * g, 256 * (g + 1))
                tot, w_end, _ = _dir_terms(la_ref[g], dt_ref[g], d)
                wexp = _dot2_r(w_end, ex)
                dexp = _dot2_r(jnp.broadcast_to(jnp.exp(tot), (8, 128)), ex)[0:1]
                xw = (x_ref[:, cols] * wexp).astype(BF16)
                s[:, cols] = s[:, cols] * dexp + _dot(b_ref[:, 128 * g:128 * (g + 1)].T.astype(BF16), xw)

    cb = functools.partial(_chunk_of_bwd_dir, ncc=ncc, nc=nc)
    sm = lambda f: pl.BlockSpec((NG, Q, 128), lambda j: (0, f(j), 0))
    one = lambda j: j
    return pl.pallas_call(
        kern, name="ssd_state", grid=(nc,),
        in_specs=[pl.BlockSpec((Q, DI), lambda j: (j, 0)), pl.BlockSpec((Q, NG * NS), lambda j: (j, 2)), sm(one), sm(one),
                  pl.BlockSpec((Q, DI), lambda j: (cb(j), 0)), pl.BlockSpec((Q, NG * NS), lambda j: (cb(j), 2)), sm(cb), sm(cb)],
        out_specs=[pl.BlockSpec((None, 128, DI), lambda j: (j, 0, 0)),
                   pl.BlockSpec((None, 128, DI), lambda j: (cb(j), 0, 0))],
        out_shape=[jax.ShapeDtypeStruct((nc, 128, DI), F32), jax.ShapeDtypeStruct((nc, 128, DI), F32)],
        scratch_shapes=[pltpu.VMEM((128, DI), F32), pltpu.VMEM((128, DI), F32)],
        compiler_params=_cp(("arbitrary",)),
    )(xbc, xbc, dtg, lag, xbc, xbc, dtg, lag)


def _ssd_out(xbc, dtg, lag, dtt, lat, htf, htb, ncc):
    lext = xbc.shape[0]
    nc = lext // Q
    ncx = nc - ncc

    gps = 8
    li, si = (lambda: _iota((Q, Q), 0)), (lambda: _iota((Q, Q), 1))

    def kern(x_ref, b_ref, c_ref, dtg_ref, lag_ref, dtt_ref, lat_ref, hf_ref, hb_ref, y_ref):
        lane = _iota((Q, 128), 1)
        masks = (li() >= si(), li() <= si())
        for gg in range(gps):
            cols = slice(256 * gg, 256 * (gg + 1))
            cm = c_ref[:, 128 * gg:128 * (gg + 1)]
            xb_ = x_ref[:, cols].astype(BF16)
            s_ = _dot_nt(cm.astype(BF16), b_ref[:, 128 * gg:128 * (gg + 1)].astype(BF16))
            la, dtt_, lat_ = lag_ref[gg], dtt_ref[8 * gg:8 * (gg + 1)], lat_ref[8 * gg:8 * (gg + 1)]
            elam = jnp.exp(la)
            yh = [jnp.zeros((Q, 128), F32), jnp.zeros((Q, 128), F32)]
            for d, h_ref in enumerate((hf_ref, hb_ref)):
                rhs = jnp.concatenate([xb_, h_ref[:, cols].astype(BF16)], axis=0)
                lhs = []
                for r in range(HPG):
                    j = 4 * d + r
                    lm = jnp.where(masks[d], jnp.exp(la[:, j:j + 1] - lat_[j:j + 1, :]), 0.0)
                    w = s_ * lm * dtt_[j:j + 1, :]
                    lhs.append(jnp.concatenate([w, cm * elam[:, j:j + 1]], axis=1).astype(BF16))
                for b in range(HPG // 2):
                    ypair = _dot(jnp.concatenate(lhs[2 * b:2 * b + 2], axis=0), rhs[:, 128 * b:128 * (b + 1)])
                    yh[b] = yh[b] + jnp.where(lane < 64, ypair[0:Q], ypair[Q:2 * Q])
            y_ref[:, cols] = jnp.concatenate(yh, axis=1).astype(BF16)

    nb = NG // gps
    sm = pl.BlockSpec((gps, Q, 128), lambda c, g: (g, c + ncc, 0))
    smt = pl.BlockSpec((8 * gps, Q), lambda c, g: (g, c + ncc))
    st3 = pl.BlockSpec((None, 128, 256 * gps), lambda c, g: (c + ncc, 0, g))
    return pl.pallas_call(
        kern, name="ssd_out", grid=(ncx, nb),
        in_specs=[pl.BlockSpec((Q, 256 * gps), lambda c, g: (c + ncc, g)),
                  pl.BlockSpec((Q, 128 * gps), lambda c, g: (c + ncc, 2 * nb + g)),
                  pl.BlockSpec((Q, 128 * gps), lambda c, g: (c + ncc, 3 * nb + g)), sm, sm, smt, smt, st3, st3],
        out_specs=pl.BlockSpec((Q, 256 * gps), lambda c, g: (c, g)),
        out_shape=jax.ShapeDtypeStruct((ncx * Q, DI), BF16),
        compiler_params=_cp(("parallel", "parallel")),
    )(xbc, xbc, xbc, dtg, lag, dtt, lat, htf, htb)


def _ssd_bwd_state(xbc, dy, lag, ncc):
    lext = xbc.shape[0]
    nc = lext // Q

    def kern(cf_ref, dyf_ref, laf_ref, cb_ref, dyb_ref, lab_ref, df_ref, db_ref, sf, sb):
        @pl.when(pl.program_id(0) == 0)
        def _():
            sf[...] = jnp.zeros_like(sf)
            sb[...] = jnp.zeros_like(sb)

        for d, (c_ref, dy_ref, la_ref, o_ref, s) in enumerate(
                ((cf_ref, dyf_ref, laf_ref, df_ref, sf), (cb_ref, dyb_ref, lab_ref, db_ref, sb))):
            o_ref[...] = s[...]
            ex = _expand_sel(d)
            for g in range(NG):
                cols = slice(256 * g, 256 * (g + 1))
                la = la_ref[g]
                tot = la[Q - 1:Q] if d == 0 else la[0:1]
                eexp = _dot2_r(jnp.exp(la), ex)
                dexp = _dot2_r(jnp.broadcast_to(jnp.exp(tot), (8, 128)), ex)[0:1]
                dye = (dy_ref[:, cols] * eexp).astype(BF16)
                s[:, cols] = s[:, cols] * dexp + _dot(c_ref[:, 128 * g:128 * (g + 1)].T.astype(BF16), dye)

    cf = lambda j: nc - 1 - j
    cb = lambda j: _chunk_of_bwd_dir(nc - 1 - j, ncc, nc)
    sm = lambda f: pl.BlockSpec((NG, Q, 128), lambda j: (0, f(j), 0))
    return pl.pallas_call(
        kern, name="ssd_bwd_state", grid=(nc,),
        in_specs=[pl.BlockSpec((Q, NG * NS), lambda j: (cf(j), 3)), pl.BlockSpec((Q, DI), lambda j: (cf(j), 0)), sm(cf),
                  pl.BlockSpec((Q, NG * NS), lambda j: (cb(j), 3)), pl.BlockSpec((Q, DI), lambda j: (cb(j), 0)), sm(cb)],
        out_specs=[pl.BlockSpec((None, 128, DI), lambda j: (cf(j), 0, 0)),
                   pl.BlockSpec((None, 128, DI), lambda j: (cb(j), 0, 0))],
        out_shape=[jax.ShapeDtypeStruct((nc, 128, DI), F32), jax.ShapeDtypeStruct((nc, 128, DI), F32)],
        scratch_shapes=[pltpu.VMEM((128, DI), F32), pltpu.VMEM((128, DI), F32)],
        compiler_params=_cp(("arbitrary",)),
    )(xbc, dy, lag, xbc, dy, lag)


def _ssd_bwd_out(xbc, dy, dsk, dtg, lag, dtt, lat, htf, htb, dhf, dhb, a_rows):
    lext = xbc.shape[0]
    nc = lext // Q

    gps = 1

    def kern(x_ref, b_ref, c_ref, dy_ref, sk_ref, dtg_ref, lag_ref, dtt_ref, lat_ref, hf_ref, hb_ref, df_ref, db_ref,
             a_ref, dx_ref, dbo_ref, dco_ref, ddt_ref, ga_ref):
        @pl.when(pl.program_id(1) == 0)
        def _():
            ga_ref[...] = jnp.zeros_like(ga_ref)

        for gg in range(gps):
            one_group(gg, x_ref, b_ref, c_ref, dy_ref, sk_ref, dtg_ref, lag_ref, dtt_ref, lat_ref, hf_ref, hb_ref, df_ref,
                      db_ref, a_ref, dx_ref, dbo_ref, dco_ref, ddt_ref, ga_ref)

    def one_group(gg, x_ref, b_ref, c_ref, dy_ref, sk_ref, dtg_ref, lag_ref, dtt_ref, lat_ref, hf_ref, hb_ref, df_ref,
                  db_ref, a_ref, dx_ref, dbo_ref, dco_ref, ddt_ref, ga_ref):
        g = pl.program_id(0) * gps + gg
        cols, cols128 = slice(256 * gg, 256 * (gg + 1)), slice(128 * gg, 128 * (gg + 1))
        x, bm, cm, dy_ = x_ref[:, cols], b_ref[:, cols128], c_ref[:, cols128], dy_ref[:, cols]
        xb_, bb_, cb_, dyb_ = x.astype(BF16), bm.astype(BF16), cm.astype(BF16), dy_.astype(BF16)
        st = _dot_nt(bb_, cb_)
        si, li = _iota((Q, Q), 0), _iota((Q, Q), 1)
        lane = _iota((Q, 256), 1)
        lane128 = _iota((Q, 128), 1)
        row128 = _iota((Q, 128), 0)
        sub = _iota((128, Q), 0)
        la, dt = lag_ref[gg], dtg_ref[gg]
        dtt_, lat_ = dtt_ref[8 * gg:8 * (gg + 1)], lat_ref[8 * gg:8 * (gg + 1)]
        elam = jnp.exp(la)
        dst = jnp.zeros((Q, Q), F32)
        dxh = [jnp.zeros((Q, 128), F32), jnp.zeros((Q, 128), F32)]
        cdir, clam = jnp.zeros((Q, 128), F32), jnp.zeros((Q, 128), F32)
        dba = jnp.zeros((Q, 128), F32)
        dca = jnp.zeros((Q, 128), F32)
        dlam = jnp.zeros((Q, 128), F32)
        ddir = jnp.zeros((Q, 128), F32)
        rows = jnp.zeros((128, Q), F32)
        per_dir = []
        for d, (h_ref, dh_ref) in enumerate(((hf_ref, df_ref), (hb_ref, db_ref))):
            ht, dht = h_ref[:, cols], dh_ref[:, cols]
            htb_, dhtb_ = ht.astype(BF16), dht.astype(BF16)
            tot, w_end, wnd = _dir_terms(la, dt, d)
            ex, rs = _expand_sel(d), _reduce_sel(d)
            elx = _dot2_r(elam, ex)
            wex = _dot2_r(w_end, ex)
            dye = dy_ * elx
            ch = _dot(cb_, htb_)
            bd = _dot(bb_, dhtb_)
            dca = dca + _dot_nt(dye.astype(BF16), htb_)
            dba = dba + _dot_nt((x * wex).astype(BF16), dhtb_)
            dlam = dlam + _dot2_r(dye * ch, rs)
            xbd = _dot2_r(x * bd, rs)
            e_ = w_end * xbd
            dlam = dlam - e_
            ddir = ddir + wnd * xbd
            hh = _dot2_r(jnp.broadcast_to(jnp.sum(dht * ht, axis=0, keepdims=True), (8, 256)), rs)[0:1]
            tot_term = jnp.sum(e_, axis=0, keepdims=True) + jnp.exp(tot) * hh
            dlam = dlam + jnp.where(row128 == (Q - 1 if d == 0 else 0), tot_term, 0.0)
            per_dir.append((w_end, jnp.concatenate([dyb_, dhtb_], axis=0), (li >= si) if d == 0 else (li <= si)))
        for r in range(HPG):
            half = slice(128 * (r // 2), 128 * (r // 2 + 1))
            hm = (lane128 >> 6) == (r % 2)
            dwt = _dot_nt(jnp.where(hm, x[:, half], 0.0).astype(BF16), dyb_[:, half])
            q = dwt * st
            for d, (w_end, rhs, maskt) in enumerate(per_dir):
                j = 4 * d + r
                dc = dt[:, j:j + 1]
                lmt = jnp.where(maskt, jnp.exp(lat_[j:j + 1, :] - la[:, j:j + 1]), 0.0)
                ldc = lmt * jnp.broadcast_to(dc, (Q, Q))
                lhs = jnp.concatenate([st * ldc, bm * w_end[:, j:j + 1]], axis=1).astype(BF16)
                dxh[r // 2] = dxh[r // 2] + jnp.where(hm, _dot(lhs, rhs[:, half]), 0.0)
                cs = jnp.sum(q * lmt, axis=1, keepdims=True)
                cdir = jnp.where(lane128 == j, cs, cdir)
                clam = jnp.where(lane128 == j, cs * dc, clam)
                rows = rows + jnp.where(sub == j, jnp.sum(q * ldc, axis=0, keepdims=True), 0.0)
                dst = dst + dwt * ldc
        dxa = jnp.concatenate(dxh, axis=1)
        ddir = ddir + cdir
        dlam = dlam - clam + rows.T
        dba = dba + _dot(dst.astype(BF16), cb_)
        dca = dca + _dot(dst.T.astype(BF16), bb_)
        isb = jnp.logical_and(lane128 >= 4, lane128 < 8)
        ddel = jnp.where(isb, _dot2_l(_tri(True), dlam), _dot2_l(_tri(False), dlam))
        a_l = a_ref[pl.ds(g, 1), :]
        ddt_ref[gg] = ddir + a_l * ddel
        ga_ref[gg] += jnp.broadcast_to(a_l * jnp.sum(dt * ddel, axis=0, keepdims=True), (8, 128))
        dx_ref[:, cols] = dxa + dy_ * sk_ref[:, cols]
        dbo_ref[:, cols128] = dba
        dco_ref[:, cols128] = dca

    nb = NG // gps
    st3 = pl.BlockSpec((None, 128, 256 * gps), lambda g, c: (c, 0, g))
    sm = pl.BlockSpec((gps, Q, 128), lambda g, c: (g, c, 0))
    smt = pl.BlockSpec((8 * gps, Q), lambda g, c: (g, c))
    wide = pl.BlockSpec((Q, 256 * gps), lambda g, c: (c, g))
    return pl.pallas_call(
        kern, name="ssd_bwd_out", grid=(nb, nc),
        in_specs=[wide, pl.BlockSpec((Q, 128 * gps), lambda g, c: (c, 2 * nb + g)),
                  pl.BlockSpec((Q, 128 * gps), lambda g, c: (c, 3 * nb + g)), wide,
                  pl.BlockSpec((1, 256 * gps), lambda g, c: (0, g)), sm, sm, smt, smt, st3, st3, st3, st3,
                  pl.BlockSpec((8, 128), lambda g, c: (0, 0))],
        out_specs=[wide, pl.BlockSpec((Q, 128 * gps), lambda g, c: (c, g)),
                   pl.BlockSpec((Q, 128 * gps), lambda g, c: (c, g)), sm, pl.BlockSpec((gps, 8, 128), lambda g, c: (g, 0, 0))],
        out_shape=[jax.ShapeDtypeStruct((lext, DI), F32), jax.ShapeDtypeStruct((lext, NG * NS), F32),
                   jax.ShapeDtypeStruct((lext, NG * NS), F32), jax.ShapeDtypeStruct((NG, lext, 128), F32),
                   jax.ShapeDtypeStruct((NG, 8, 128), F32)],
        compiler_params=_cp(("parallel", "arbitrary")),
    )(xbc, xbc, xbc, dy, dsk, dtg, lag, dtt, lat, htf, htb, dhf, dhb, a_rows)


def _post_fwd(yssm, xbc, proj_rest, dsk, gnw, nct):
    l = yssm.shape[0]

    def kern(y_ref, x_ref, z_ref, dsk_ref, w_ref, o_ref):
        y = y_ref[...].astype(F32) + dsk_ref[...] * x_ref[...]
        yz = y * _silu(z_ref[...].astype(F32))
        for g in range(NG):
            sl = slice(256 * g, 256 * (g + 1))
            v = yz[:, sl]
            r = lax.rsqrt(jnp.mean(v * v, axis=-1, keepdims=True) + EPS)
            o_ref[:, sl] = (v * r * w_ref[:, sl]).astype(BF16)

    return pl.pallas_call(
        kern, name="post_fwd", grid=(l // T,),
        in_specs=[pl.BlockSpec((T, DI), lambda i: (i, 0)), pl.BlockSpec((T, DI), lambda i: (i + nct, 0)),
                  pl.BlockSpec((T, DI), lambda i: (i, 0)), pl.BlockSpec((1, DI), lambda i: (0, 0)),
                  pl.BlockSpec((1, DI), lambda i: (0, 0))],
        out_specs=pl.BlockSpec((T, DI), lambda i: (i, 0)),
        out_shape=jax.ShapeDtypeStruct((l, DI), BF16),
        compiler_params=_cp(("parallel",)),
    )(yssm, xbc, proj_rest, dsk, gnw)


def _post_bwd(dgn, yssm, xbc, proj_rest, dsk, gnw, dpr, nct):
    l = yssm.shape[0]
    lext = xbc.shape[0]
    xi = lambda i: (jnp.maximum(i - nct, 0), 0)

    def kern(dg_ref, y_ref, x_ref, z_ref, dsk_ref, w_ref, dpr_ref, dy_ref, dz_ref, gw_ref, gd_ref):
        i = pl.program_id(0)

        @pl.when(i == 0)
        def _():
            gw_ref[...] = jnp.zeros_like(gw_ref)
            gd_ref[...] = jnp.zeros_like(gd_ref)

        @pl.when(i < nct)
        def _():
            dy_ref[...] = jnp.zeros_like(dy_ref)

        @pl.when(i >= nct)
        def _():
            xs = x_ref[...]
            z = z_ref[...].astype(F32)
            y = y_ref[...].astype(F32) + dsk_ref[...] * xs
            sz = _silu(z)
            yz = y * sz
            dgn_ = dg_ref[...].astype(F32)
            dyz_parts = []
            gws = []
            for g in range(NG):
                sl = slice(256 * g, 256 * (g + 1))
                v = yz[:, sl]
                r = lax.rsqrt(jnp.mean(v * v, axis=-1, keepdims=True) + EPS)
                vn = v * r
                dn = dgn_[:, sl] * w_ref[:, sl]
                gws.append(jnp.sum(dgn_[:, sl] * vn, axis=0, keepdims=True))
                dyz_parts.append(r * (dn - vn * jnp.mean(dn * vn, axis=-1, keepdims=True)))
            dyz = jnp.concatenate(dyz_parts, axis=1)
            gw_ref[...] += jnp.broadcast_to(jnp.concatenate(gws, axis=1), (8, DI))
            dy = dyz * sz
            dz_ref[...] = (dyz * y * _dsilu(z)).astype(BF16)
            gd_ref[...] += jnp.broadcast_to(jnp.sum(dy * xs, axis=0, keepdims=True), (8, DI))
            dy_ref[...] = dy

    return pl.pallas_call(
        kern, name="post_bwd", grid=(lext // T,),
        in_specs=[pl.BlockSpec((T, DI), xi), pl.BlockSpec((T, DI), xi), pl.BlockSpec((T, DI), lambda i: (i, 0)),
                  pl.BlockSpec((T, DI), xi), pl.BlockSpec((1, DI), lambda i: (0, 0)), pl.BlockSpec((1, DI), lambda i: (0, 0)),
                  pl.BlockSpec(memory_space=pl.ANY)],
        out_specs=[pl.BlockSpec((T, DI), lambda i: (i, 0)),
                   pl.BlockSpec((T, DI), xi), pl.BlockSpec((8, DI), lambda i: (0, 0)), pl.BlockSpec((8, DI), lambda i: (0, 0))],
        out_shape=[jax.ShapeDtypeStruct((lext, DI), F32),
                   jax.ShapeDtypeStruct((l, RESTW), BF16), jax.ShapeDtypeStruct((8, DI), F32), jax.ShapeDtypeStruct((8, DI), F32)],
        input_output_aliases={6: 1},
        compiler_params=_cp(("arbitrary",)),
    )(dgn, yssm, xbc, proj_rest, dsk, gnw, dpr)


C_G1, C_G2, C_GA, C_GB, C_CG = 2, 3, 4, 5, 6
PITCH = GW + 16
NROW = T // GW


GAP = PITCH - GW
PADR = GAP + NROW * PITCH
NSTRIP = D // 128


def _fill_padded(pad8, val):
    z = jnp.zeros((GAP, D), F32)
    parts = [z]
    for r in range(NROW):
        parts += [val[GW * r:GW * (r + 1)], z]
    p = jnp.concatenate(parts, axis=0)
    pad8[0] = p
    for j in range(1, pad8.shape[0]):
        pad8[j] = pltpu.roll(p, PADR - j, axis=0)


def _tap(pad8, base, off, ln):
    return pad8[off % 8, pl.ds(base + off - off % 8, GW), ln]


def _row_conv(out_ref, pad8, w_ref, transpose):
    def strip(s, carry):
        ln = pl.ds(pl.multiple_of(s * 128, 128), 128)
        for r in range(NROW):
            base = GAP + PITCH * r
            acc = jnp.zeros((GW, 128), F32)
            for k in range(CK):
                off = (k - 15) if not transpose else (15 - k)
                acc = acc + w_ref[pl.ds(k, 1), ln] * _tap(pad8, base, off, ln)
            out_ref[pl.ds(GW * r, GW), ln] = acc
        return carry

    lax.fori_loop(0, NSTRIP, strip, 0)


def _row_conv_wgrad(gcw_ref, padd8, pada8):
    def strip(s, carry):
        ln = pl.ds(pl.multiple_of(s * 128, 128), 128)
        rid = _iota((32, 128), 0)
        g = jnp.zeros((32, 128), F32)
        for k0 in range(0, CK, 8):
            taps = range(k0, min(k0 + 8, CK))
            accs = {k: jnp.zeros((8, 128), F32) for k in taps}
            for r in range(NROW):
                base = GAP + PITCH * r
                d = _tap(padd8, base, 0, ln)
                for k in taps:
                    p = d * pada8[0, pl.ds(base + k - 15, GW), ln]
                    part = p[0:8]
                    for q in range(1, GW // 8):
                        part = part + p[8 * q:8 * (q + 1)]
                    accs[k] = accs[k] + part
            for k in taps:
                g = jnp.where(rid == k, jnp.sum(accs[k], axis=0, keepdims=True), g)
        gcw_ref[:, ln] += g
        return carry

    lax.fori_loop(0, NSTRIP, strip, 0)


def _ln_stats(cv):
    mu = jnp.mean(cv, axis=-1, keepdims=True)
    xc = cv - mu
    rs = lax.rsqrt(jnp.mean(xc * xc, axis=-1, keepdims=True) + EPS)
    return xc * rs, rs


def _conf_fwd(proj_rest, cw, cb, lw, lb):
    l = proj_rest.shape[0]

    def kern(ga_ref, gb_ref, cg_ref, cw_ref, cb_ref, lw_ref, lb_ref, o_ref, cv_ref, pad8):
        _fill_padded(pad8, ga_ref[...].astype(F32) * _sig(gb_ref[...].astype(F32)))
        _row_conv(cv_ref, pad8, cw_ref, False)
        cv = cv_ref[...] + cb_ref[...]
        cv_ref[...] = cv
        xh, _ = _ln_stats(cv)
        ln = xh * lw_ref[...] + lb_ref[...]
        o_ref[...] = (_silu(ln) * _silu(cg_ref[...].astype(F32))).astype(BF16)

    vec = pl.BlockSpec((1, D), lambda i: (0, 0))
    blk = pl.BlockSpec((T, D), lambda i: (i, 0))
    return pl.pallas_call(
        kern, name="conf_fwd", grid=(l // T,),
        in_specs=[pl.BlockSpec((T, D), lambda i: (i, C_GA)), pl.BlockSpec((T, D), lambda i: (i, C_GB)),
                  pl.BlockSpec((T, D), lambda i: (i, C_CG)), pl.BlockSpec((32, D), lambda i: (0, 0)), vec, vec, vec],
        out_specs=[blk, blk],
        out_shape=[jax.ShapeDtypeStruct((l, D), BF16), jax.ShapeDtypeStruct((l, D), F32)],
        scratch_shapes=[pltpu.VMEM((8, PADR, D), F32)],
        compiler_params=_cp(("parallel",)),
    )(proj_rest, proj_rest, proj_rest, cw, cb, lw, lb)


def _conf_bwd(duc, cv, proj_rest, cw, lw, lb, dpr):
    l = proj_rest.shape[0]

    def kern(du_ref, cv_ref, ga_ref, gb_ref, cg_ref, cw_ref, lw_ref, lb_ref, dpr_ref, o_ref, gcw_ref, gv_ref, sc,
             pada, padd, da_ref):
        i, j = pl.program_id(0), pl.program_id(1)

        @pl.when(jnp.logical_and(i == 0, j == 0))
        def _():
            gcw_ref[...] = jnp.zeros_like(gcw_ref)
            gv_ref[...] = jnp.zeros_like(gv_ref)

        @pl.when(j == 0)
        def _():
            ga, gb, cg = ga_ref[...].astype(F32), gb_ref[...].astype(F32), cg_ref[...].astype(F32)
            sg = _sig(gb)
            xh, rs = _ln_stats(cv_ref[...])
            ln = xh * lw_ref[...] + lb_ref[...]
            du = du_ref[...].astype(F32)
            sc[:, 2 * D:3 * D] = (du * _silu(ln) * _dsilu(cg)).astype(BF16)
            dln = du * _silu(cg) * _dsilu(ln)
            g_lw = jnp.sum(dln * xh, axis=0, keepdims=True)
            g_lb = jnp.sum(dln, axis=0, keepdims=True)
            dxh = dln * lw_ref[...]
            dcv = rs * (dxh - jnp.mean(dxh, axis=-1, keepdims=True) - xh * jnp.mean(dxh * xh, axis=-1, keepdims=True))
            g_cb = jnp.sum(dcv, axis=0, keepdims=True)
            rid = _iota((8, D), 0)
            gv_ref[...] += jnp.where(rid == 0, g_cb, jnp.where(rid == 1, g_lw, jnp.where(rid == 2, g_lb, 0.0)))
            _fill_padded(padd, dcv)
            _fill_padded(pada, ga * sg)
            _row_conv(da_ref, padd, cw_ref, True)
            _row_conv_wgrad(gcw_ref, padd, pada)
            da = da_ref[...]
            sc[:, 0:D] = (da * sg).astype(BF16)
            sc[:, D:2 * D] = (da * ga * sg * (1.0 - sg)).astype(BF16)

        o_ref[...] = sc[:, pl.ds(pl.multiple_of(j * D, 128), D)]

    vec = pl.BlockSpec((1, D), lambda i, j: (0, 0))
    col = lambda c: pl.BlockSpec((T, D), lambda i, j: (i, c))
    return pl.pallas_call(
        kern, name="conf_bwd", grid=(l // T, 3),
        in_specs=[col(0), col(0), col(C_GA), col(C_GB), col(C_CG), pl.BlockSpec((32, D), lambda i, j: (0, 0)), vec, vec,
                  pl.BlockSpec(memory_space=pl.ANY)],
        out_specs=[pl.BlockSpec((T, D), lambda i, j: (i, C_GA + j)), pl.BlockSpec((32, D), lambda i, j: (0, 0)),
                   pl.BlockSpec((8, D), lambda i, j: (0, 0))],
        out_shape=[jax.ShapeDtypeStruct((l, RESTW), BF16), jax.ShapeDtypeStruct((32, D), F32),
                   jax.ShapeDtypeStruct((8, D), F32)],
        scratch_shapes=[pltpu.VMEM((T, 3 * D), BF16), pltpu.VMEM((1, PADR, D), F32), pltpu.VMEM((8, PADR, D), F32),
                        pltpu.VMEM((T, D), F32)],
        input_output_aliases={8: 0},
        compiler_params=_cp(("arbitrary", "arbitrary")),
    )(duc, cv, proj_rest, proj_rest, proj_rest, cw, lw, lb, dpr)


def _merge_fwd(bs, bc, proj_rest):
    l = bs.shape[0]

    def kern(bs_ref, bc_ref, g1_ref, g2_ref, o_ref):
        up = lambda r: r[...].astype(F32)
        o_ref[...] = (_sig(up(g1_ref)) * up(bs_ref) + _sig(up(g2_ref)) * up(bc_ref)).astype(BF16)

    blk = pl.BlockSpec((TX, D), lambda i: (i, 0))
    return pl.pallas_call(
        kern, name="merge_fwd", grid=(l // TX,),
        in_specs=[blk, blk, pl.BlockSpec((TX, D), lambda i: (i, C_G1)), pl.BlockSpec((TX, D), lambda i: (i, C_G2))],
        out_specs=blk, out_shape=jax.ShapeDtypeStruct((l, D), BF16),
        compiler_params=_cp(("parallel",)),
    )(bs, bc, proj_rest, proj_rest)


def _merge_bwd(dm, bs, bc, proj_rest):
    l = bs.shape[0]

    def kern(dm_ref, bs_ref, bc_ref, g1_ref, g2_ref, dbs_ref, dbc_ref, dg_ref):
        up = lambda r: r[...].astype(F32)
        dm_ = up(dm_ref)
        s1, s2 = _sig(up(g1_ref)), _sig(up(g2_ref))
        dbs_ref[...] = (dm_ * s1).astype(BF16)
        dbc_ref[...] = (dm_ * s2).astype(BF16)
        dg_ref[:, 0:D] = (dm_ * up(bs_ref) * s1 * (1.0 - s1)).astype(BF16)
        dg_ref[:, D:2 * D] = (dm_ * up(bc_ref) * s2 * (1.0 - s2)).astype(BF16)

    blk = pl.BlockSpec((TX, D), lambda i: (i, 0))
    return pl.pallas_call(
        kern, name="merge_bwd", grid=(l // TX,),
        in_specs=[blk, blk, blk, pl.BlockSpec((TX, D), lambda i: (i, C_G1)), pl.BlockSpec((TX, D), lambda i: (i, C_G2))],
        out_specs=[blk, blk, pl.BlockSpec((TX, 2 * D), lambda i: (i, 1))],
        out_shape=[jax.ShapeDtypeStruct((l, D), BF16), jax.ShapeDtypeStruct((l, D), BF16),
                   jax.ShapeDtypeStruct((l, RESTW), BF16)],
        compiler_params=_cp(("parallel",)),
    )(dm, bs, bc, proj_rest, proj_rest)


def _final(x, out, tgt, mod, fw):
    l = x.shape[0]

    def kern(x_ref, o_ref, t_ref, mod_ref, fw_ref, ls_ref, dx2_ref, do_ref, gv_ref):
        @pl.when(pl.program_id(0) == 0)
        def _():
            ls_ref[...] = jnp.zeros_like(ls_ref)
            gv_ref[...] = jnp.zeros_like(gv_ref)

        gate = mod_ref[0:1, 2 * D:3 * D]
        o = o_ref[...]
        x2 = x_ref[...] + gate * o
        r = lax.rsqrt(jnp.mean(x2 * x2, axis=-1, keepdims=True) + EPS)
        yn = x2 * r
        fw_ = fw_ref[...]
        e = yn * fw_ - t_ref[...]
        ls_ref[...] += jnp.full((8, 128), 1.0, F32) * (0.5 / D) * jnp.sum(e * e)
        dy = e * (1.0 / D)
        g_fw = jnp.sum(dy * yn, axis=0, keepdims=True)
        dyn = dy * fw_
        dx2 = r * (dyn - yn * jnp.mean(dyn * yn, axis=-1, keepdims=True))
        g_gate = jnp.sum(dx2 * o, axis=0, keepdims=True)
        rid = _iota((8, D), 0)
        gv_ref[...] += jnp.where(rid == 0, g_fw, jnp.where(rid == 1, g_gate, 0.0))
        dx2_ref[...] = dx2
        do_ref[...] = (dx2 * gate).astype(BF16)

    blk = pl.BlockSpec((TX, D), lambda i: (i, 0))
    return pl.pallas_call(
        kern, name="final", grid=(l // TX,),
        in_specs=[blk, blk, blk, pl.BlockSpec((8, 3 * D), lambda i: (0, 0)), pl.BlockSpec((1, D), lambda i: (0, 0))],
        out_specs=[pl.BlockSpec((8, 128), lambda i: (0, 0)), blk, blk, pl.BlockSpec((8, D), lambda i: (0, 0))],
        out_shape=[jax.ShapeDtypeStruct((8, 128), F32), jax.ShapeDtypeStruct((l, D), F32),
                   jax.ShapeDtypeStruct((l, D), BF16), jax.ShapeDtypeStruct((8, D), F32)],
        compiler_params=_cp(("arbitrary",)),
    )(x, out, tgt, mod, fw)


def _perm_dt_cols(w):
    s = w.shape[:-1]
    return w.reshape(*s, 2, NG, HPG).swapaxes(-3, -2).reshape(*s, 64)


def _unperm_dt_cols(w):
    s = w.shape[:-1]
    return w.reshape(*s, NG, 2, HPG).swapaxes(-3, -2).reshape(*s, 64)


def _pad_lanes(v, width):
    return jnp.pad(v, ((0, 0), (0, width - v.shape[1])))


def _vcols(segs, a, b):
    parts, off = [], 0
    for s in segs:
        lo, hi = max(a, off), min(b, off + s.shape[1])
        if lo < hi:
            parts.append(s[:, lo - off:hi - off])
        off += s.shape[1]
    return parts[0] if len(parts) == 1 else jnp.concatenate(parts, axis=1)


def _local_step(x, c, ctx, tgt, w):
    l = x.shape[0]
    nct = CTX // T
    ncc = CTX // Q
    lext = l + CTX

    w_mod = w["w_mod"].astype(BF16)
    wsegs = [s.astype(BF16) for s in (w["w_in"] if isinstance(w["w_in"], (list, tuple)) else [w["w_in"]])]
    w_ssd = jnp.concatenate([_vcols(wsegs, 0, XBC), _perm_dt_cols(_vcols(wsegs, XBC, XBC + 64)), jnp.zeros((D, 64), BF16)], axis=1)
    r0 = XBC + 64
    w_rest = jnp.concatenate([_vcols(wsegs, r0, r0 + DI), _vcols(wsegs, r0 + DI + 3 * D, r0 + RESTW),
                              _vcols(wsegs, r0 + DI, r0 + DI + 3 * D)], axis=1)
    w_os, w_oc, w_o = w["w_out_ssm"].astype(BF16), w["w_out_conf"].astype(BF16), w["w_out"].astype(BF16)
    cw8 = jnp.pad(w["ssm_conv_w"], ((0, 4), (0, 0)))
    cb_s = w["ssm_conv_b"].reshape(1, XBC)
    dtb = _pad_lanes(_perm_dt_cols(w["dt_bias"].reshape(1, 64)), 128)
    a_all = -jnp.exp(w["a_log"].reshape(1, 64))
    a_perm = _pad_lanes(_perm_dt_cols(a_all), 128)
    a_rows = _pad_lanes(_perm_dt_cols(a_all).reshape(NG, 8), 128)
    dsk = jnp.repeat(w["d_skip"].reshape(NH), HP).reshape(1, DI)
    gnw = w["ssm_norm_w"].reshape(1, DI)
    ccw = jnp.pad(w["conf_conv_w"], ((0, 1), (0, 0)))
    ccb, clw, clb = w["conf_conv_b"].reshape(1, D), w["conf_ln_w"].reshape(1, D), w["conf_ln_b"].reshape(1, D)
    nw = w["norm_w"].reshape(1, D)
    fw = w["final_norm_w"].reshape(1, D)
    cc = jnp.concatenate([c.reshape(1, D), w["c_ctx"].reshape(1, D), jnp.zeros((6, D), F32)], axis=0)

    bx = min(1024, l)
    be = 768 if lext % 768 == 0 else 256
    tk = min(1024, l)
    mod = _mod_fwd(cc, w_mod, w["b_mod"].reshape(1, 3 * D))
    h = _norm_fwd(ctx, x, mod, nw, nct)
    hx = h[CTX:]
    proj_ssd = _mm(h, w_ssd, "nn", lext, SSDW, D, be, SSDW // 3, D, F32, "proj_ssd")
    proj_rest = _mm(hx, w_rest, "nn", l, RESTW, D, bx, 1024, D, BF16, "proj_rest")
    xbc = _conv_fwd(proj_ssd, cw8, cb_s, nct)
    dtg, lag, dtt, lat = _dt_fwd(proj_ssd, dtb, a_perm)
    htf, htb = _ssd_state(xbc, dtg, lag, ncc)
    yssm = _ssd_out(xbc, dtg, lag, dtt, lat, htf, htb, ncc)
    gn = _post_fwd(yssm, xbc, proj_rest, dsk, gnw, nct)
    bs = _mm(gn, w_os, "nn", l, D, DI, bx, D, DI, BF16, "out_ssm")
    uc, cv = _conf_fwd(proj_rest, ccw, ccb, clw, clb)
    bc = _mm(uc, w_oc, "nn", l, D, D, bx, D, D, BF16, "out_conf")
    merged = _merge_fwd(bs, bc, proj_rest)
    out = _mm(merged, w_o, "nn", l, D, D, bx, D, D, F32, "out_proj")
    lsum, dx2, dout, gv_fin = _final(x, out, tgt, mod, fw)

    g = {}
    g["final_norm_w"] = gv_fin[0]
    dmerged = _mm(dout, w_o, "nt", l, D, D, bx, D, D, BF16, "d_merged")
    g["w_out"] = _mm(merged, dout, "tn", D, D, l, D, D, tk, F32, "g_w_out")
    dbs, dbc, dpr = _merge_bwd(dmerged, bs, bc, proj_rest)
    dgn = _mm(dbs, w_os, "nt", l, DI, D, bx, DI, D, BF16, "d_gn")
    g["w_out_ssm"] = _mm(gn, dbs, "tn", DI, D, l, DI, D, tk, F32, "g_w_out_ssm")
    duc = _mm(dbc, w_oc, "nt", l, D, D, bx, D, D, BF16, "d_uc")
    g["w_out_conf"] = _mm(uc, dbc, "tn", D, D, l, D, D, tk, F32, "g_w_out_conf")
    dpr, gcw, gv_conf = _conf_bwd(duc, cv, proj_rest, ccw, clw, clb, dpr)
    g["conf_conv_w"] = gcw[:CK]
    g["conf_conv_b"], g["conf_ln_w"], g["conf_ln_b"] = gv_conf[0], gv_conf[1], gv_conf[2]
    dy, dproj_rest, ggnw, gdsk = _post_bwd(dgn, yssm, xbc, proj_rest, dsk, gnw, dpr, nct)
    g["ssm_norm_w"] = ggnw[0]
    g["d_skip"] = gdsk[0].reshape(NH, HP).sum(axis=1)
    dhf, dhb = _ssd_bwd_state(xbc, dy, lag, ncc)
    dxs, dbm, dcm, ddtg, galog = _ssd_bwd_out(xbc, dy, dsk, dtg, lag, dtt, lat, htf, htb, dhf, dhb, a_rows)
    g["a_log"] = _unperm_dt_cols(galog[:, 0, 0:8].reshape(1, 64)).reshape(2, NH)
    dus, gws, gbs = [], [], []
    for dpost, off, width, nm in ((dxs, 0, DI, "conv_bwd_x"), (dbm, DI, NG * NS, "conv_bwd_b"), (dcm, DI + NG * NS, NG * NS, "conv_bwd_c")):
        du_, gw_, gb_ = _conv_bwd(dpost, proj_ssd, cw8, cb_s, off, width, nct, nm)
        dus.append(du_)
        gws.append(gw_[:SK])
        gbs.append(gb_[0])
    g["ssm_conv_w"] = jnp.concatenate(gws, axis=1)
    g["ssm_conv_b"] = jnp.concatenate(gbs, axis=0)
    ddt_raw, gdtb = _dt_bwd(ddtg, proj_ssd, dtb)
    g["dt_bias"] = _unperm_dt_cols(gdtb[0:1, 0:64]).reshape(2, NH)
    dproj_ssd = jnp.concatenate(dus + [ddt_raw], axis=1)
    gw_ssd = _mm(h, dproj_ssd, "tn", D, SSDW, lext, D, SSDW // 3, be, F32, "g_w_ssd")
    gw_rest = _mm(hx, dproj_rest, "tn", D, RESTW, l, D, 1024, tk, F32, "g_w_rest")
    gsegs = [gw_ssd[:, :XBC], _unperm_dt_cols(gw_ssd[:, XBC:XBC + 64]), gw_rest[:, :DI], gw_rest[:, 2 * DI:],
             gw_rest[:, DI:2 * DI]]
    g["w_in"] = jnp.concatenate(gsegs, axis=1)
    g["w_in_shards"] = jnp.stack([_vcols(gsegs, R_IN * s, R_IN * (s + 1)) for s in range(NSHARD)])
    dh_a = _mm(dproj_ssd, w_ssd, "nt", lext, D, SSDW, T, D, SSDW, BF16, "dh_ssd")
    dh_b = _mm(dproj_rest, w_rest, "nt", l, D, RESTW, T, D, RESTW, BF16, "dh_rest")
    grad_x, gnw_in, dss = _norm_bwd(dh_a, dh_b, ctx, x, dx2, mod, nw, nct)
    g["norm_w"] = gnw_in[0]
    dmod = jnp.concatenate([jnp.concatenate([dss[0:1], gv_fin[1:2]], axis=1),
                            jnp.concatenate([dss[1:2], jnp.zeros((1, D), F32)], axis=1),
                            jnp.zeros((6, 3 * D), F32)], axis=0)
    gwm, gbm, gcc = _mod_bwd(dmod, cc, cc.T, w_mod)
    g["w_mod"], g["b_mod"], g["c_ctx"] = gwm, gbm[0], gcc[1]
    return lsum[0, 0], grad_x, g


NSHARD = 4
R_MOD, R_IN, R_OS, R_OC, R_O, R_SC, R_CC = 768, 2832, 512, 256, 256, 8, 8
O_MOD = 0
O_OS = O_MOD + R_MOD
O_OC = O_OS + R_OS
O_O = O_OC + R_OC
O_SC = O_O + R_O
O_CC = O_SC + R_SC
PUSED = O_CC + R_CC
PROWS = 1824
HALF = PROWS // 2
RB = HALF // 3
WB = 128
SROWS = 16
SMALL = (("b_mod", 3 * D), ("norm_w", D), ("ssm_conv_b", XBC), ("dt_bias", 64), ("a_log", 64), ("d_skip", NH),
         ("ssm_norm_w", DI), ("conf_conv_b", D), ("conf_ln_w", D), ("conf_ln_b", D), ("final_norm_w", D), ("c_ctx", D))


def _pack_shard(s):
    return jnp.concatenate([s["w_mod"].reshape(R_MOD, D), _pack_rest(s), jnp.zeros((PROWS - PUSED, D), F32)], axis=0)


def _pack_rest(s):
    cc = jnp.pad(s["conf_conv_w"].reshape(1, CK * 256), ((0, 0), (0, R_CC * D - CK * 256))).reshape(R_CC, D)
    return jnp.concatenate([s["w_out_ssm"], s["w_out_conf"], s["w_out"],
                            jnp.pad(s["ssm_conv_w"], ((0, R_SC - SK), (0, 0))), cc], axis=0)


def _unpack_rest(p):
    o = lambda r: r - O_OS
    return {"w_out_ssm": p[o(O_OS):o(O_OC)][None], "w_out_conf": p[o(O_OC):o(O_O)][None], "w_out": p[o(O_O):o(O_SC)][None],
            "ssm_conv_w": p[o(O_SC):o(O_SC) + SK][None],
            "conf_conv_w": p[o(O_CC):o(O_CC) + R_CC].reshape(R_CC * D)[:CK * 256].reshape(1, CK, 256)}


def _shard_cols(a, n):
    return a.reshape(a.shape[0], NSHARD, n).transpose(1, 0, 2)


def _pack_full(g):
    cc = jnp.pad(_shard_cols(g["conf_conv_w"], 256).reshape(NSHARD, CK * 256), ((0, 0), (0, R_CC * D - CK * 256)))
    return jnp.concatenate([_shard_cols(g["w_mod"], R_MOD).reshape(NSHARD, R_MOD, D),
                            g["w_out_ssm"].reshape(NSHARD, R_OS, D), g["w_out_conf"].reshape(NSHARD, R_OC, D),
                            g["w_out"].reshape(NSHARD, R_O, D),
                            jnp.pad(_shard_cols(g["ssm_conv_w"], D), ((0, 0), (0, R_SC - SK), (0, 0))),
                            cc.reshape(NSHARD, R_CC, D), jnp.zeros((NSHARD, PROWS - PUSED, D), F32)], axis=1)


def _unpack_gathered(gm, gw, gs):
    def cols(a, r, n):
        return a.reshape(NSHARD, r, n).transpose(1, 0, 2).reshape(r, NSHARD * n)
    return {"w_mod": cols(gm[:, O_MOD:O_OS], D, R_MOD), "w_in": [gw[s] for s in range(NSHARD)],
            "w_out_ssm": gm[:, O_OS:O_OC].reshape(DI, D), "w_out_conf": gm[:, O_OC:O_O].reshape(D, D),
            "w_out": gm[:, O_O:O_SC].reshape(D, D), "ssm_conv_w": cols(gs[:, 0:SK], SK, D),
            "conf_conv_w": cols(gs[:, R_SC:R_SC + R_CC].reshape(NSHARD, R_CC * D)[:, :CK * 256], CK, 256)}


MESH_ID = pl.DeviceIdType.MESH
ANY = pl.BlockSpec(memory_space=pl.ANY)


def _place():
    x, y, c = lax.axis_index("x"), lax.axis_index("y"), lax.axis_index("c")
    return x, y, c, [(1 - x, y), (x, 1 - y), (1 - x, 1 - y)]


def _rcopy(src, dst, send, recv, dev):
    return pltpu.make_async_remote_copy(src_ref=src, dst_ref=dst, send_sem=send, recv_sem=recv,
                                        device_id=dev, device_id_type=MESH_ID)


def _gather_weights(mats, small):
    n = len(mats)

    def kern(*refs):
        m_refs, s_ref, g_refs, gs_ref, (send, recv) = refs[:n], refs[n], refs[n + 1:2 * n + 1], refs[2 * n + 1], refs[2 * n + 2:]
        x, y, c, chips = _place()
        me = 2 * x + y
        sib = (x, y, 1 - c)
        first, passed = [], []
        for k, (px, py) in enumerate(chips):
            first.append(_rcopy(s_ref, gs_ref.at[me], send.at[k], recv.at[k], (px, py, c)))
            for a, (m_ref, g_ref) in enumerate(zip(m_refs, g_refs)):
                mine = _half_rows(c, m_ref.shape[0])
                first.append(_rcopy(m_ref.at[mine], g_ref.at[me, mine], send.at[3 + 6 * a + k], recv.at[3 + 6 * a + k], (px, py, c)))
        for cp in first:
            cp.start()
        for k, (px, py) in enumerate(chips):
            s = 2 * px + py
            for a, (m_ref, g_ref) in enumerate(zip(m_refs, g_refs)):
                mine = _half_rows(c, m_ref.shape[0])
                _rcopy(m_ref.at[mine], g_ref.at[s, mine], send.at[3 + 6 * a + k], recv.at[3 + 6 * a + k], sib).wait_recv()
                f = _rcopy(g_ref.at[s, mine], g_ref.at[s, mine], send.at[6 + 6 * a + k], recv.at[6 + 6 * a + k], sib)
                f.start()
                passed.append(f)
        for k, (px, py) in enumerate(chips):
            s = 2 * px + py
            _rcopy(s_ref, gs_ref.at[s], send.at[k], recv.at[k], sib).wait_recv()
            for a, g_ref in enumerate(g_refs):
                other = _half_rows(1 - c, g_ref.shape[1])
                _rcopy(g_ref.at[s, other], g_ref.at[s, other], send.at[6 + 6 * a + k], recv.at[6 + 6 * a + k], sib).wait_recv()
        for cp in first + passed:
            cp.wait_send()

    nsem = 3 + 6 * n
    return pl.pallas_call(
        kern, name="gather_weights", in_specs=[ANY] * (n + 1), out_specs=[ANY] * (n + 1),
        out_shape=[jax.ShapeDtypeStruct((NSHARD,) + m.shape, m.dtype) for m in mats]
        + [jax.ShapeDtypeStruct((NSHARD, SROWS, D), F32)],
        scratch_shapes=[pltpu.SemaphoreType.DMA((nsem,)), pltpu.SemaphoreType.DMA((nsem,))],
    )(*mats, small)


def _half_rows(c, rows):
    return pl.ds(pl.multiple_of(c * (rows // 2), 16), rows // 2)


def _swap_halves(gs):
    n = len(gs)

    def kern(*refs):
        g_refs, o_refs, (send, recv) = refs[:n], refs[n:2 * n], refs[2 * n:]
        x, y, c, _ = _place()
        cps = [_rcopy(g_ref.at[s, _half_rows(1 - c, g_ref.shape[1])], o_ref.at[s], send.at[NSHARD * a + s],
                      recv.at[NSHARD * a + s], (x, y, 1 - c))
               for a, (g_ref, o_ref) in enumerate(zip(g_refs, o_refs)) for s in range(NSHARD)]
        for cp in cps:
            cp.start()
        for cp in cps:
            cp.wait()

    return pl.pallas_call(
        kern, name="swap_halves", in_specs=[ANY] * n, out_specs=[ANY] * n,
        out_shape=[jax.ShapeDtypeStruct((NSHARD, g.shape[1] // 2, g.shape[2]), F32) for g in gs],
        scratch_shapes=[pltpu.SemaphoreType.DMA((NSHARD * n,)), pltpu.SemaphoreType.DMA((NSHARD * n,))],
    )(*gs)


def _add_halves(cidx, g, ra, rb, name):
    _, half, cols = ra.shape
    nb = half // rb

    def kern(c_ref, g_ref, a_ref, o_ref):
        o_ref[...] = (g_ref[...] + a_ref[...]).astype(BF16)

    return pl.pallas_call(
        kern, name=name,
        grid_spec=pltpu.PrefetchScalarGridSpec(
            num_scalar_prefetch=1, grid=(NSHARD, nb),
            in_specs=[pl.BlockSpec((None, rb, cols), lambda s, i, c: (s, c[0] * nb + i, 0)),
                      pl.BlockSpec((None, rb, cols), lambda s, i, c: (s, i, 0))],
            out_specs=pl.BlockSpec((None, rb, cols), lambda s, i, c: (s, i, 0))),
        out_shape=jax.ShapeDtypeStruct((NSHARD, half, cols), BF16),
        compiler_params=_cp(("parallel", "parallel")),
    )(cidx, g, ra)


def _exchange_chips(ps):
    n = len(ps)

    def kern(*refs):
        p_refs, o_refs, (send, recv) = refs[:n], refs[n:2 * n], refs[2 * n:]
        x, y, c, chips = _place()
        cps = [_rcopy(p_ref.at[2 * px + py], o_ref.at[k], send.at[3 * a + k], recv.at[3 * a + k], (px, py, c))
               for a, (p_ref, o_ref) in enumerate(zip(p_refs, o_refs)) for k, (px, py) in enumerate(chips)]
        for cp in cps:
            cp.start()
        for cp in cps:
            cp.wait()

    return pl.pallas_call(
        kern, name="exchange_chips", in_specs=[ANY] * n, out_specs=[ANY] * n,
        out_shape=[jax.ShapeDtypeStruct((3,) + p.shape[1:], p.dtype) for p in ps],
        scratch_shapes=[pltpu.SemaphoreType.DMA((3 * n,)), pltpu.SemaphoreType.DMA((3 * n,))],
    )(*ps)


def _add_chips(mc, g, ra, rx, rb, name):
    _, half, cols = ra.shape
    nb = half // rb

    def kern(m_ref, g_ref, a_ref, r0_ref, r1_ref, r2_ref, o_ref):
        own = g_ref[...] + a_ref[...]
        o_ref[...] = ((own + r0_ref[...].astype(F32)) + r1_ref[...].astype(F32)) + r2_ref[...].astype(F32)

    return pl.pallas_call(
        kern, name=name,
        grid_spec=pltpu.PrefetchScalarGridSpec(
            num_scalar_prefetch=1, grid=(nb,),
            in_specs=[pl.BlockSpec((None, rb, cols), lambda i, m: (m[0], m[1] * nb + i, 0)),
                      pl.BlockSpec((None, rb, cols), lambda i, m: (m[0], i, 0))]
            + [pl.BlockSpec((None, rb, cols), functools.partial(lambda i, m, k: (k, i, 0), k=k)) for k in range(3)],
            out_specs=pl.BlockSpec((rb, cols), lambda i, m: (i, 0))),
        out_shape=jax.ShapeDtypeStruct((half, cols), F32),
        compiler_params=_cp(("parallel",)),
    )(mc, g, ra, rx, rx, rx)


def _share_halves(rs):
    n = len(rs)

    def kern(*refs):
        r_refs, o_refs, (send, recv) = refs[:n], refs[n:2 * n], refs[2 * n:]
        x, y, c, _ = _place()
        cps = [_rcopy(r_ref, o_ref, send.at[a], recv.at[a], (x, y, 1 - c))
               for a, (r_ref, o_ref) in enumerate(zip(r_refs, o_refs))]
        for cp in cps:
            cp.start()
        for cp in cps:
            cp.wait()

    return pl.pallas_call(
        kern, name="share_halves", in_specs=[ANY] * n, out_specs=[ANY] * n,
        out_shape=[jax.ShapeDtypeStruct(r.shape, F32) for r in rs],
        scratch_shapes=[pltpu.SemaphoreType.DMA((n,)), pltpu.SemaphoreType.DMA((n,))],
    )(*rs)


SMALL_W = XBC


def _small_update(gs, ws, ms, vs):
    n = len(gs)
    widths = [g.shape[1] for g in gs]
    assert n <= SROWS and max(widths) <= SMALL_W

    def kern(*refs):
        g_refs, w_refs, m_refs, v_refs = (refs[n * i:n * (i + 1)] for i in range(4))
        o_g, o_d, o_m, o_v = (refs[n * (4 + i):n * (5 + i)] for i in range(4))
        buf, send, recv = refs[8 * n:]
        x, y, c, _ = _place()
        me = 4 * x + 2 * y + c
        buf[me] = jnp.zeros((SROWS, SMALL_W), F32)
        for k, g_ref in enumerate(g_refs):
            buf[me, k:k + 1, 0:widths[k]] = g_ref[...]
        cps = []
        for r in range(1, 8):
            peer = (1 - x if r & 4 else x, 1 - y if r & 2 else y, 1 - c if r & 1 else c)
            cps.append(_rcopy(buf.at[me], buf.at[me], send.at[r - 1], recv.at[r - 1], peer))
        for cp in cps:
            cp.start()
        for cp in cps:
            cp.wait()
        acc = buf[0]
        for i in range(1, 8):
            acc = acc + buf[i]
        for k in range(n):
            g_ = acc[k:k + 1, 0:widths[k]]
            m_ = ADAM_B1 * m_refs[k][...] + (1.0 - ADAM_B1) * g_
            v_ = ADAM_B2 * v_refs[k][...] + (1.0 - ADAM_B2) * jnp.square(g_)
            m_hat = m_ / (1.0 - ADAM_B1 ** ADAM_STEP)
            v_hat = v_ / (1.0 - ADAM_B2 ** ADAM_STEP)
            o_g[k][...] = g_
            o_d[k][...] = -ADAM_LR * (m_hat / (jnp.sqrt(v_hat) + ADAM_EPS) + ADAM_WD * w_refs[k][...])
            o_m[k][...] = m_
            o_v[k][...] = v_

    vm = pl.BlockSpec(memory_space=pltpu.VMEM)
    outs = pl.pallas_call(
        kern, name="small_update", in_specs=[vm] * (4 * n), out_specs=[vm] * (4 * n),
        out_shape=[jax.ShapeDtypeStruct((1, wd), F32) for _ in range(4) for wd in widths],
        scratch_shapes=[pltpu.VMEM((8, SROWS, SMALL_W), F32), pltpu.SemaphoreType.DMA((7,)), pltpu.SemaphoreType.DMA((7,))],
    )(*gs, *ws, *ms, *vs)
    return [outs[n * i:n * (i + 1)] for i in range(4)]


def _adamw(g, w, m, v, rb, name):
    rows, cols = g.shape

    def kern(g_ref, w_ref, m_ref, v_ref, d_ref, nm_ref, nv_ref):
        g_ = g_ref[...]
        m_ = ADAM_B1 * m_ref[...] + (1.0 - ADAM_B1) * g_
        v_ = ADAM_B2 * v_ref[...] + (1.0 - ADAM_B2) * jnp.square(g_)
        m_hat = m_ / (1.0 - ADAM_B1 ** ADAM_STEP)
        v_hat = v_ / (1.0 - ADAM_B2 ** ADAM_STEP)
        d_ref[...] = -ADAM_LR * (m_hat / (jnp.sqrt(v_hat) + ADAM_EPS) + ADAM_WD * w_ref[...])
        nm_ref[...] = m_
        nv_ref[...] = v_

    assert rows % rb == 0
    blk = pl.BlockSpec((rb, cols), lambda i: (i, 0))
    return pl.pallas_call(
        kern, name=name, grid=(rows // rb,), in_specs=[blk] * 4, out_specs=[blk] * 3,
        out_shape=[jax.ShapeDtypeStruct((rows, cols), F32)] * 3,
        compiler_params=_cp(("parallel",)),
    )(g, w, m, v)


def _adamw_halves(cidx, mine, other, w, m, v, rb, name):
    rows, cols = w.shape
    nbh = rows // 2 // rb

    def kern(c_ref, a_ref, b_ref, w_ref, m_ref, v_ref, g_ref, d_ref, nm_ref, nv_ref):
        g_ = jnp.where(pl.program_id(0) // nbh == c_ref[0], a_ref[...], b_ref[...])
        m_ = ADAM_B1 * m_ref[...] + (1.0 - ADAM_B1) * g_
        v_ = ADAM_B2 * v_ref[...] + (1.0 - ADAM_B2) * jnp.square(g_)
        m_hat = m_ / (1.0 - ADAM_B1 ** ADAM_STEP)
        v_hat = v_ / (1.0 - ADAM_B2 ** ADAM_STEP)
        g_ref[...] = g_
        d_ref[...] = -ADAM_LR * (m_hat / (jnp.sqrt(v_hat) + ADAM_EPS) + ADAM_WD * w_ref[...])
        nm_ref[...] = m_
        nv_ref[...] = v_

    half = pl.BlockSpec((rb, cols), lambda i, c: (i % nbh, 0))
    blk = pl.BlockSpec((rb, cols), lambda i, c: (i, 0))
    return pl.pallas_call(
        kern, name=name,
        grid_spec=pltpu.PrefetchScalarGridSpec(num_scalar_prefetch=1, grid=(2 * nbh,), in_specs=[half, half, blk, blk, blk],
                                               out_specs=[blk] * 4),
        out_shape=[jax.ShapeDtypeStruct((rows, cols), F32)] * 4,
        compiler_params=_cp(("parallel",)),
    )(cidx, mine, other, w, m, v)


WEIGHTS = ("c_ctx", "w_mod", "b_mod", "norm_w", "w_in", "ssm_conv_w", "ssm_conv_b", "dt_bias", "a_log", "d_skip",
           "ssm_norm_w", "w_out_ssm", "conf_conv_w", "conf_conv_b", "conf_ln_w", "conf_ln_b", "w_out_conf", "w_out",
           "final_norm_w")


def kernel(x, c, ctx, c_ctx, w_mod, b_mod, norm_w, w_in, ssm_conv_w, ssm_conv_b, dt_bias, a_log, d_skip, ssm_norm_w, w_out_ssm, conf_conv_w, conf_conv_b, conf_ln_w, conf_ln_b, w_out_conf, w_out, final_norm_w, loss_target, m_c_ctx, m_w_mod, m_b_mod, m_norm_w, m_w_in, m_ssm_conv_w, m_ssm_conv_b, m_dt_bias, m_a_log, m_d_skip, m_ssm_norm_w, m_w_out_ssm, m_conf_conv_w, m_conf_conv_b, m_conf_ln_w, m_conf_ln_b, m_w_out_conf, m_w_out, m_final_norm_w, v_c_ctx, v_w_mod, v_b_mod, v_norm_w, v_w_in, v_ssm_conv_w, v_ssm_conv_b, v_dt_bias, v_a_log, v_d_skip, v_ssm_norm_w, v_w_out_ssm, v_conf_conv_w, v_conf_conv_b, v_conf_ln_w, v_conf_ln_b, v_w_out_conf, v_w_out, v_final_norm_w):
    wv = (c_ctx, w_mod, b_mod, norm_w, w_in, ssm_conv_w, ssm_conv_b, dt_bias, a_log, d_skip, ssm_norm_w, w_out_ssm,
          conf_conv_w, conf_conv_b, conf_ln_w, conf_ln_b, w_out_conf, w_out, final_norm_w)
    mv = (m_c_ctx, m_w_mod, m_b_mod, m_norm_w, m_w_in, m_ssm_conv_w, m_ssm_conv_b, m_dt_bias, m_a_log, m_d_skip,
          m_ssm_norm_w, m_w_out_ssm, m_conf_conv_w, m_conf_conv_b, m_conf_ln_w, m_conf_ln_b, m_w_out_conf, m_w_out,
          m_final_norm_w)
    vv = (v_c_ctx, v_w_mod, v_b_mod, v_norm_w, v_w_in, v_ssm_conv_w, v_ssm_conv_b, v_dt_bias, v_a_log, v_d_skip,
          v_ssm_norm_w, v_w_out_ssm, v_conf_conv_w, v_conf_conv_b, v_conf_ln_w, v_conf_ln_b, v_w_out_conf, v_w_out,
          v_final_norm_w)
    shapes = {n: a.shape for n, a in zip(WEIGHTS, wv)}

    def squeeze(d):
        return {n: (a if n in ("c_ctx", "final_norm_w") else a[0]) for n, a in d.items()}

    w, m, v = (squeeze(dict(zip(WEIGHTS, t))) for t in (wv, mv, vv))

    my_chip = 2 * lax.axis_index("x") + lax.axis_index("y")
    my_core = lax.axis_index("c")

    pw = _pack_shard(w)
    pwb, wib, psm = pw.astype(BF16), w["w_in"].astype(BF16), pw[O_SC:O_SC + SROWS]
    gm, gw, gs = _gather_weights([pwb, wib], psm)
    mine = (jnp.arange(NSHARD) == my_chip)[:, None, None]
    gm, gw, gs = jnp.where(mine, pwb[None], gm), jnp.where(mine, wib[None], gw), jnp.where(mine, psm[None], gs)
    full = dict(w)
    full.update(_unpack_gathered(gm, gw, gs))

    lsum, grad_x, g = _local_step(x[0], c, ctx[0], loss_target[0], full)
    loss = lax.psum(lsum, ("x", "y", "c"))

    cidx = my_core.astype(jnp.int32).reshape(1)
    mc = jnp.stack([my_chip, my_core]).astype(jnp.int32)
    gsrc = [_pack_full(g), g["w_in_shards"]]
    blocks = (RB, WB)
    sib = _swap_halves(gsrc)
    part = [_add_halves(cidx, a, b, rb, "add_halves_%d" % i) for i, (a, b, rb) in enumerate(zip(gsrc, sib, blocks))]
    far = _exchange_chips(part)
    red = [_add_chips(mc, a, b, f, rb, "add_chips_%d" % i) for i, (a, b, f, rb) in enumerate(zip(gsrc, sib, far, blocks))]
    got = _share_halves(red)
    g_pk = jnp.concatenate([jnp.where(my_core == 0, red[0], got[0]), jnp.where(my_core == 0, got[0], red[0])], axis=0)
    small = [name for name, _ in SMALL]
    as_row = lambda d: [d[name].reshape(1, -1) for name in small]
    res_sm = _small_update(as_row(g), as_row(w), as_row(m), as_row(v))

    gr = {"w_mod": g_pk[O_MOD:O_OS].reshape(D, R_MOD), "rest": g_pk[O_OS:PUSED]}
    wr, mr, vr = ({"w_mod": t["w_mod"], "rest": _pack_rest(t)} for t in (w, m, v))
    res = {k: _adamw(gr[k], wr[k], mr[k], vr[k], rb, "adamw_" + k)
           for k, rb in (("w_mod", 512), ("rest", (PUSED - O_OS) // 2))}
    gr["w_in"], *res["w_in"] = _adamw_halves(cidx, red[1], got[1], w["w_in"], m["w_in"], v["w_in"], WB, "adamw_w_in")

    outs = []
    for i in range(4):
        pick = (lambda k: gr[k]) if i == 0 else (lambda k: res[k][i - 1])
        d = {"w_mod": pick("w_mod")[None], "w_in": pick("w_in")[None]}
        d.update(_unpack_rest(pick("rest")))
        d.update({name: a.reshape(shapes[name]) for name, a in zip(small, res_sm[i])})
        outs.extend(d[n] for n in WEIGHTS)
    return (loss, grad_x[None], *outs)
```

```python
import functools

import jax
import jax.numpy as jnp
from jax import lax
from jax.experimental import pallas as pl
from jax.experimental.pallas import tpu as pltpu

F32, BF16 = jnp.float32, jnp.bfloat16

D = 1024
DI = 2048
NH = 32
HP = 64
NG = 8
HPG = 4
NS = 128
Q = 128
GW = 64
CK = 31
SK = 4
CTX = 256
EPS = 1e-6
XBC = DI + 2 * NG * NS
SSDW = XBC + 128
RESTW = 7168
T = 256
TX = 512
VMEM_LIMIT = 56 * 1024 * 1024

ADAM_LR, ADAM_B1, ADAM_B2, ADAM_EPS, ADAM_WD, ADAM_STEP = 0.001, 0.9, 0.999, 1e-08, 0.01, 10


def _cp(sem):
    return pltpu.CompilerParams(dimension_semantics=sem, vmem_limit_bytes=VMEM_LIMIT)


def _sig(x):
    return jax.nn.sigmoid(x)


def _silu(x):
    return x * _sig(x)


def _dsilu(x):
    s = _sig(x)
    return s * (1.0 + x * (1.0 - s))


def _dot(a, b):
    return jnp.dot(a, b, preferred_element_type=F32)


def _dot_nt(a, b):
    return lax.dot_general(a, b, (((1,), (1,)), ((), ())), preferred_element_type=F32)


def _split3(x):
    h = x.astype(BF16)
    r = x - h.astype(F32)
    m = r.astype(BF16)
    l = (r - m.astype(F32)).astype(BF16)
    return h, m, l


def _dot3_l(sel, x):
    h, m, l = _split3(x)
    return _dot(sel, h) + _dot(sel, m) + _dot(sel, l)


def _dot3_r(x, sel):
    h, m, l = _split3(x)
    return _dot(h, sel) + _dot(m, sel) + _dot(l, sel)


def _split2(x):
    h = x.astype(BF16)
    return h, (x - h.astype(F32)).astype(BF16)


def _dot2_l(sel, x):
    h, l = _split2(x)
    return _dot(sel, h) + _dot(sel, l)


def _dot2_r(x, sel):
    h, l = _split2(x)
    return _dot(h, sel) + _dot(l, sel)


def _iota(shape, dim):
    return lax.broadcasted_iota(jnp.int32, shape, dim)


def _mm(a, b, dims, m, n, k, bm, bn, bk, out_dtype, name):
    nk = k // bk
    assert m % bm == 0 and n % bn == 0 and k % bk == 0, (name, m, n, k, bm, bn, bk)

    def prod(a_ref, b_ref):
        av = a_ref[...].astype(BF16)
        bv = b_ref[...].astype(BF16)
        if dims == "nn":
            return _dot(av, bv)
        if dims == "nt":
            return _dot_nt(av, bv)
        return lax.dot_general(av, bv, (((0,), (0,)), ((), ())), preferred_element_type=F32)

    def kern_one(a_ref, b_ref, o_ref):
        o_ref[...] = prod(a_ref, b_ref).astype(out_dtype)

    def kern_acc(a_ref, b_ref, o_ref, acc):
        kk = pl.program_id(2)

        @pl.when(kk == 0)
        def _():
            acc[...] = jnp.zeros_like(acc)

        acc[...] += prod(a_ref, b_ref)

        @pl.when(kk == nk - 1)
        def _():
            o_ref[...] = acc[...].astype(out_dtype)

    if dims == "nn":
        a_spec = pl.BlockSpec((bm, bk), lambda j, i, kk: (i, kk))
        b_spec = pl.BlockSpec((bk, bn), lambda j, i, kk: (kk, j))
    elif dims == "nt":
        a_spec = pl.BlockSpec((bm, bk), lambda j, i, kk: (i, kk))
        b_spec = pl.BlockSpec((bn, bk), lambda j, i, kk: (j, kk))
    else:
        a_spec = pl.BlockSpec((bk, bm), lambda j, i, kk: (kk, i))
        b_spec = pl.BlockSpec((bk, bn), lambda j, i, kk: (kk, j))
    return pl.pallas_call(
        kern_one if nk == 1 else kern_acc, name=name,
        grid=(n // bn, m // bm, nk),
        in_specs=[a_spec, b_spec],
        out_specs=pl.BlockSpec((bm, bn), lambda j, i, kk: (i, j)),
        out_shape=jax.ShapeDtypeStruct((m, n), out_dtype),
        scratch_shapes=[] if nk == 1 else [pltpu.VMEM((bm, bn), F32)],
        compiler_params=_cp(("parallel", "parallel", "arbitrary")),
    )(a, b)


def _mod_fwd(cc, w_mod, b_mod):
    def kern(cc_ref, w_ref, b_ref, o_ref):
        s = _silu(cc_ref[...]).astype(BF16)
        o_ref[...] = _dot(s, w_ref[...]) + b_ref[...]

    return pl.pallas_call(
        kern, name="mod_fwd", grid=(3,),
        in_specs=[pl.BlockSpec((8, D), lambda j: (0, 0)), pl.BlockSpec((D, D), lambda j: (0, j)),
                  pl.BlockSpec((1, D), lambda j: (0, j))],
        out_specs=pl.BlockSpec((8, D), lambda j: (0, j)),
        out_shape=jax.ShapeDtypeStruct((8, 3 * D), F32),
        compiler_params=_cp(("parallel",)),
    )(cc, w_mod, b_mod)


def _mod_bwd(dmod, cc, cct, w_mod):
    def kern(dm_ref, cc_ref, cct_ref, w_ref, gw_ref, gb_ref, gc_ref):
        kk = pl.program_id(0)
        dm = dm_ref[...]
        sct = _silu(cct_ref[...])
        gw_ref[...] = sct[:, 0:1] * dm[0:1, :] + sct[:, 1:2] * dm[1:2, :]
        gb_ref[...] = jnp.broadcast_to(dm[0:1, :] + dm[1:2, :], dm.shape)

        @pl.when(kk == 0)
        def _():
            gc_ref[...] = jnp.zeros_like(gc_ref)

        gc_ref[...] += _dot_nt(dm.astype(BF16), w_ref[...])

        @pl.when(kk == 2)
        def _():
            gc_ref[...] = gc_ref[...] * _dsilu(cc_ref[...])

    return pl.pallas_call(
        kern, name="mod_bwd", grid=(3,),
        in_specs=[pl.BlockSpec((8, D), lambda j: (0, j)), pl.BlockSpec((8, D), lambda j: (0, 0)),
                  pl.BlockSpec((D, 8), lambda j: (0, 0)), pl.BlockSpec((D, D), lambda j: (0, j))],
        out_specs=[pl.BlockSpec((D, D), lambda j: (0, j)), pl.BlockSpec((8, D), lambda j: (0, j)),
                   pl.BlockSpec((8, D), lambda j: (0, 0))],
        out_shape=[jax.ShapeDtypeStruct((D, 3 * D), F32), jax.ShapeDtypeStruct((8, 3 * D), F32),
                   jax.ShapeDtypeStruct((8, D), F32)],
        compiler_params=_cp(("arbitrary",)),
    )(dmod, cc, cct, w_mod)


def _ext_specs(nct):
    return (pl.BlockSpec((T, D), lambda i: (jnp.minimum(i, nct - 1), 0)),
            pl.BlockSpec((T, D), lambda i: (jnp.maximum(i - nct, 0), 0)))


def _norm_fwd(ctx, xl, mod, nw, nct):
    lext = ctx.shape[0] + xl.shape[0]

    def kern(c_ref, x_ref, mod_ref, nw_ref, h_ref):
        is_ctx = pl.program_id(0) < nct
        x = jnp.where(is_ctx, c_ref[...], x_ref[...])
        r = lax.rsqrt(jnp.mean(x * x, axis=-1, keepdims=True) + EPS)
        xn = x * r * nw_ref[...]
        shift = jnp.where(is_ctx, mod_ref[1:2, 0:D], mod_ref[0:1, 0:D])
        scale = jnp.where(is_ctx, mod_ref[1:2, D:2 * D], mod_ref[0:1, D:2 * D])
        h_ref[...] = (xn * (1.0 + scale) + shift).astype(BF16)

    return pl.pallas_call(
        kern, name="norm_fwd", grid=(lext // T,),
        in_specs=[*_ext_specs(nct), pl.BlockSpec((8, 3 * D), lambda i: (0, 0)),
                  pl.BlockSpec((1, D), lambda i: (0, 0))],
        out_specs=pl.BlockSpec((T, D), lambda i: (i, 0)),
        out_shape=jax.ShapeDtypeStruct((lext, D), BF16),
        compiler_params=_cp(("parallel",)),
    )(ctx, xl, mod, nw)


def _norm_bwd(dha, dhb, ctx, xl, dx2, mod, nw, nct):
    lext = ctx.shape[0] + xl.shape[0]
    ntl = lext // T

    def kern(dha_ref, dhb_ref, c_ref, x_ref, dx2_ref, mod_ref, nw_ref, gx_ref, gnw_ref, dss_ref):
        i = pl.program_id(0)
        is_ctx = i < nct

        @pl.when(i == 0)
        def _():
            gnw_ref[...] = jnp.zeros_like(gnw_ref)
            dss_ref[...] = jnp.zeros_like(dss_ref)

        x = jnp.where(is_ctx, c_ref[...], x_ref[...])
        dh_ = dha_ref[...].astype(F32) + jnp.where(is_ctx, 0.0, dhb_ref[...].astype(F32))
        nw_ = nw_ref[...]
        r = lax.rsqrt(jnp.mean(x * x, axis=-1, keepdims=True) + EPS)
        xn = x * r
        scale = jnp.where(is_ctx, mod_ref[1:2, D:2 * D], mod_ref[0:1, D:2 * D])
        dsh = jnp.sum(dh_, axis=0, keepdims=True)
        dsc = jnp.sum(dh_ * (xn * nw_), axis=0, keepdims=True)
        row = jnp.concatenate([dsh, dsc], axis=1)
        rid = _iota((8, 2 * D), 0)
        dss_ref[...] += jnp.where(rid == jnp.where(is_ctx, 1, 0), row, 0.0)
        dxnw = dh_ * (1.0 + scale)
        gnw_ref[...] += jnp.broadcast_to(jnp.sum(dxnw * xn, axis=0, keepdims=True), (8, D))
        dxn = dxnw * nw_
        dx = r * (dxn - xn * jnp.mean(dxn * xn, axis=-1, keepdims=True))
        gx_ref[...] = dx2_ref[...] + dx

    return pl.pallas_call(
        kern, name="norm_bwd", grid=(ntl,),
        in_specs=[pl.BlockSpec((T, D), lambda i: (i, 0)), pl.BlockSpec((T, D), lambda i: (jnp.maximum(i - nct, 0), 0)),
                  *_ext_specs(nct),
                  pl.BlockSpec((T, D), lambda i: (jnp.maximum(i - nct, 0), 0)),
                  pl.BlockSpec((8, 3 * D), lambda i: (0, 0)), pl.BlockSpec((1, D), lambda i: (0, 0))],
        out_specs=[pl.BlockSpec((T, D), lambda i: (jnp.maximum(i - nct, 0), 0)),
                   pl.BlockSpec((8, D), lambda i: (0, 0)), pl.BlockSpec((8, 2 * D), lambda i: (0, 0))],
        out_shape=[jax.ShapeDtypeStruct((lext - nct * T, D), F32), jax.ShapeDtypeStruct((8, D), F32),
                   jax.ShapeDtypeStruct((8, 2 * D), F32)],
        compiler_params=_cp(("arbitrary",)),
    )(dha, dhb, ctx, xl, dx2, mod, nw)


CB = 1024


def _halo_specs(width_blk, col_off_blocks, ntl):
    t8 = T // 8
    main = pl.BlockSpec((T, width_blk), lambda j, i: (i, j + col_off_blocks))
    prev = pl.BlockSpec((8, width_blk), lambda j, i: (jnp.maximum(i * t8 - 1, 0), j + col_off_blocks))
    nxt = pl.BlockSpec((8, width_blk), lambda j, i: (jnp.minimum((i + 1) * t8, ntl * t8 - 1), j + col_off_blocks))
    return main, prev, nxt


def _seq_edges(i, nct, ntl):
    starts = jnp.logical_or(i == 0, i == nct)
    ends = jnp.logical_or(i == nct - 1, i == ntl - 1)
    return starts, ends


def _shifted(ext, off):
    n = ext.shape[0]
    return pltpu.roll(ext, (-off) % n, axis=0)[8:8 + T]


def _conv_fwd(proj_ssd, cw, cb, nct):
    lext = proj_ssd.shape[0]
    ntl = lext // T

    def kern(u_ref, up_ref, un_ref, w_ref, b_ref, o_ref):
        i = pl.program_id(1)
        starts, ends = _seq_edges(i, nct, ntl)
        up = jnp.where(starts, 0.0, up_ref[...])
        un = jnp.where(ends, 0.0, un_ref[...])
        ext = jnp.concatenate([up, u_ref[...], un], axis=0)
        w = w_ref[...]
        pre = b_ref[...] + w[0:1] * _shifted(ext, -2) + w[1:2] * _shifted(ext, -1) \
            + w[2:3] * u_ref[...] + w[3:4] * _shifted(ext, 1)
        o_ref[...] = _silu(pre)

    cbf = 4 * CB
    main, prev, nxt = _halo_specs(cbf, 0, ntl)
    return pl.pallas_call(
        kern, name="conv_fwd", grid=(XBC // cbf, ntl),
        in_specs=[main, prev, nxt, pl.BlockSpec((8, cbf), lambda j, i: (0, j)), pl.BlockSpec((1, cbf), lambda j, i: (0, j))],
        out_specs=pl.BlockSpec((T, cbf), lambda j, i: (i, j)),
        out_shape=jax.ShapeDtypeStruct((lext, XBC), F32),
        compiler_params=_cp(("parallel", "parallel")),
    )(proj_ssd, proj_ssd, proj_ssd, cw, cb)


def _conv_bwd(dpost, proj_ssd, cw, cb, col_off, width, nct, name):
    lext = proj_ssd.shape[0]
    ntl = lext // T
    bw = min(width, 2 * CB)
    assert col_off % bw == 0 and width % bw == 0
    cob = col_off // bw

    def kern(u_ref, up_ref, un_ref, d_ref, dp_ref, dn_ref, w_ref, b_ref, du_ref, gw_ref, gb_ref):
        i = pl.program_id(1)

        @pl.when(i == 0)
        def _():
            gw_ref[...] = jnp.zeros_like(gw_ref)
            gb_ref[...] = jnp.zeros_like(gb_ref)

        starts, ends = _seq_edges(i, nct, ntl)
        ext = jnp.concatenate([jnp.where(starts, 0.0, up_ref[...]), u_ref[...], jnp.where(ends, 0.0, un_ref[...])], axis=0)
        dext = jnp.concatenate([jnp.where(starts, 0.0, dp_ref[...]), d_ref[...], jnp.where(ends, 0.0, dn_ref[...])], axis=0)
        w = w_ref[...]
        n = ext.shape[0]
        pre = b_ref[...] + w[0:1] * pltpu.roll(ext, 2, axis=0) + w[1:2] * pltpu.roll(ext, 1, axis=0) \
            + w[2:3] * ext + w[3:4] * pltpu.roll(ext, n - 1, axis=0)
        dpre = dext * _dsilu(pre)
        dm = dpre[8:8 + T]
        du = w[0:1] * _shifted(dpre, 2) + w[1:2] * _shifted(dpre, 1) + w[2:3] * dm + w[3:4] * _shifted(dpre, -1)
        du_ref[...] = du.astype(BF16)
        g0 = jnp.sum(dm * _shifted(ext, -2), axis=0, keepdims=True)
        g1 = jnp.sum(dm * _shifted(ext, -1), axis=0, keepdims=True)
        g2 = jnp.sum(dm * u_ref[...], axis=0, keepdims=True)
        g3 = jnp.sum(dm * _shifted(ext, 1), axis=0, keepdims=True)
        rid = _iota((8, bw), 0)
        gw_ref[...] += jnp.where(rid == 0, g0, jnp.where(rid == 1, g1, jnp.where(rid == 2, g2, jnp.where(rid == 3, g3, 0.0))))
        gb_ref[...] += jnp.broadcast_to(jnp.sum(dm, axis=0, keepdims=True), (8, bw))

    main, prev, nxt = _halo_specs(bw, cob, ntl)
    dmain, dprev, dnxt = _halo_specs(bw, 0, ntl)
    return pl.pallas_call(
        kern, name=name, grid=(width // bw, ntl),
        in_specs=[main, prev, nxt, dmain, dprev, dnxt,
                  pl.BlockSpec((8, bw), lambda j, i: (0, j + cob)), pl.BlockSpec((1, bw), lambda j, i: (0, j + cob))],
        out_specs=[pl.BlockSpec((T, bw), lambda j, i: (i, j)), pl.BlockSpec((8, bw), lambda j, i: (0, j)),
                   pl.BlockSpec((8, bw), lambda j, i: (0, j))],
        out_shape=[jax.ShapeDtypeStruct((lext, width), BF16), jax.ShapeDtypeStruct((8, width), F32),
                   jax.ShapeDtypeStruct((8, width), F32)],
        compiler_params=_cp(("parallel", "arbitrary")),
    )(proj_ssd, proj_ssd, proj_ssd, dpost, dpost, dpost, cw, cb)


def _tri(lower):
    r, c = _iota((Q, Q), 0), _iota((Q, Q), 1)
    return jnp.where((c <= r) if lower else (c >= r), 1.0, 0.0).astype(BF16)


def _is_bdir_lane(shape):
    ln = _iota(shape, len(shape) - 1)
    return jnp.logical_and(((ln >> 2) & 1) == 1, ln < 64)


def _dt_fwd(proj_ssd, dtb, av):
    lext = proj_ssd.shape[0]

    def kern(p_ref, b_ref, a_ref, dtg_ref, lag_ref, dtt_ref, lat_ref):
        lane = _iota((T, 128), 1)
        raw = p_ref[...] + b_ref[...]
        dt = jnp.where(lane < 64, jnp.maximum(raw, 0.0) + jnp.log1p(jnp.exp(-jnp.abs(raw))), 0.0)
        dta = dt * a_ref[...]
        tl, tu = _tri(True), _tri(False)
        isb = _is_bdir_lane((Q, 128))
        las = []
        for qq in range(T // Q):
            blk = dta[qq * Q:(qq + 1) * Q]
            las.append(jnp.where(isb, _dot3_l(tu, blk), _dot3_l(tl, blk)))
        la = jnp.concatenate(las, axis=0)
        for g in range(NG):
            sh = (128 - 8 * g) % 128
            dtg_ref[g] = jnp.where(lane < 8, pltpu.roll(dt, sh, axis=1) if sh else dt, 0.0)
            lag_ref[g] = jnp.where(lane < 8, pltpu.roll(la, sh, axis=1) if sh else la, 0.0)
        dtt_ref[...] = dt.T[0:64]
        lat_ref[...] = la.T[0:64]

    return pl.pallas_call(
        kern, name="dt_fwd", grid=(lext // T,),
        in_specs=[pl.BlockSpec((T, 128), lambda i: (i, XBC // 128)), pl.BlockSpec((1, 128), lambda i: (0, 0)),
                  pl.BlockSpec((1, 128), lambda i: (0, 0))],
        out_specs=[pl.BlockSpec((NG, T, 128), lambda i: (0, i, 0)), pl.BlockSpec((NG, T, 128), lambda i: (0, i, 0)),
                   pl.BlockSpec((64, T), lambda i: (0, i)), pl.BlockSpec((64, T), lambda i: (0, i))],
        out_shape=[jax.ShapeDtypeStruct((NG, lext, 128), F32), jax.ShapeDtypeStruct((NG, lext, 128), F32),
                   jax.ShapeDtypeStruct((64, lext), F32), jax.ShapeDtypeStruct((64, lext), F32)],
        compiler_params=_cp(("parallel",)),
    )(proj_ssd, dtb, av)


def _dt_bwd(ddtg, proj_ssd, dtb):
    lext = proj_ssd.shape[0]

    def kern(d_ref, p_ref, b_ref, o_ref, gb_ref):
        @pl.when(pl.program_id(0) == 0)
        def _():
            gb_ref[...] = jnp.zeros_like(gb_ref)

        acc = d_ref[0]
        for g in range(1, NG):
            acc = acc + pltpu.roll(d_ref[g], 8 * g, axis=1)
        draw = acc * _sig(p_ref[...] + b_ref[...])
        o_ref[...] = draw.astype(BF16)
        gb_ref[...] += jnp.broadcast_to(jnp.sum(draw, axis=0, keepdims=True), (8, 128))

    return pl.pallas_call(
        kern, name="dt_bwd", grid=(lext // T,),
        in_specs=[pl.BlockSpec((NG, T, 128), lambda i: (0, i, 0)), pl.BlockSpec((T, 128), lambda i: (i, XBC // 128)),
                  pl.BlockSpec((1, 128), lambda i: (0, 0))],
        out_specs=[pl.BlockSpec((T, 128), lambda i: (i, 0)), pl.BlockSpec((8, 128), lambda i: (0, 0))],
        out_shape=[jax.ShapeDtypeStruct((lext, 128), BF16), jax.ShapeDtypeStruct((8, 128), F32)],
        compiler_params=_cp(("arbitrary",)),
    )(ddtg, proj_ssd, dtb)


def _expand_sel(d):
    r, c = _iota((128, 256), 0), _iota((128, 256), 1)
    return jnp.where(r == 4 * d + (c >> 6), 1.0, 0.0).astype(BF16)


def _reduce_sel(d):
    r, c = _iota((256, 128), 0), _iota((256, 128), 1)
    return jnp.where(c == 4 * d + (r >> 6), 1.0, 0.0).astype(BF16)


def _chunk_of_bwd_dir(j, ncc, nc):
    return jnp.where(j < ncc, ncc - 1 - j, nc + ncc - 1 - j)


def _dir_terms(la, dt, d):
    lane = _iota(la.shape, 1)
    mine = jnp.logical_and(lane >= 4 * d, lane < 4 * d + 4)
    la = jnp.where(mine, la, 0.0)
    tot = la[Q - 1:Q] if d == 0 else la[0:1]
    wnd = jnp.exp(tot - la)
    return tot, wnd * jnp.where(mine, dt, 0.0), wnd


def _ssd_state(xbc, dtg, lag, ncc):
    lext = xbc.shape[0]
    nc = lext // Q

    def kern(xf_ref, bf_ref, dtf_ref, laf_ref, xb_ref, bb_ref, dtb_ref, lab_ref, hf_ref, hb_ref, sf, sb):
        @pl.when(pl.program_id(0) == 0)
        def _():
            sf[...] = jnp.zeros_like(sf)
            sb[...] = jnp.zeros_like(sb)

        for d, (x_ref, b_ref, dt_ref, la_ref, h_ref, s) in enumerate(
                ((xf_ref, bf_ref, dtf_ref, laf_ref, hf_ref, sf), (xb_ref, bb_ref, dtb_ref, lab_ref, hb_ref, sb))):
            h_ref[...] = s[...]
            ex = _expand_sel(d)
            for g in range(NG):
                cols = slice(256 * g, 256 * (g + 1))
                tot, w_end, _ = _dir_terms(la_ref[g], dt_ref[g], d)
                wexp = _dot2_r(w_end, ex)
                dexp = _dot2_r(jnp.broadcast_to(jnp.exp(tot), (8, 128)), ex)[0:1]
                xw = (x_ref[:, cols] * wexp).astype(BF16)
                s[:, cols] = s[:, cols] * dexp + _dot(b_ref[:, 128 * g:128 * (g + 1)].T.astype(BF16), xw)

    cb = functools.partial(_chunk_of_bwd_dir, ncc=ncc, nc=nc)
    sm = lambda f: pl.BlockSpec((NG, Q, 128), lambda j: (0, f(j), 0))
    one = lambda j: j
    return pl.pallas_call(
        kern, name="ssd_state", grid=(nc,),
        in_specs=[pl.BlockSpec((Q, DI), lambda j: (j, 0)), pl.BlockSpec((Q, NG * NS), lambda j: (j, 2)), sm(one), sm(one),
                  pl.BlockSpec((Q, DI), lambda j: (cb(j), 0)), pl.BlockSpec((Q, NG * NS), lambda j: (cb(j), 2)), sm(cb), sm(cb)],
        out_specs=[pl.BlockSpec((None, 128, DI), lambda j: (j, 0, 0)),
                   pl.BlockSpec((None, 128, DI), lambda j: (cb(j), 0, 0))],
        out_shape=[jax.ShapeDtypeStruct((nc, 128, DI), F32), jax.ShapeDtypeStruct((nc, 128, DI), F32)],
        scratch_shapes=[pltpu.VMEM((128, DI), F32), pltpu.VMEM((128, DI), F32)],
        compiler_params=_cp(("arbitrary",)),
    )(xbc, xbc, dtg, lag, xbc, xbc, dtg, lag)


def _ssd_out(xbc, dtg, lag, dtt, lat, htf, htb, ncc):
    lext = xbc.shape[0]
    nc = lext // Q
    ncx = nc - ncc

    gps = 8
    li, si = (lambda: _iota((Q, Q), 0)), (lambda: _iota((Q, Q), 1))

    def kern(x_ref, b_ref, c_ref, dtg_ref, lag_ref, dtt_ref, lat_ref, hf_ref, hb_ref, y_ref):
        lane = _iota((Q, 128), 1)
        masks = (li() >= si(), li() <= si())
        for gg in range(gps):
            cols = slice(256 * gg, 256 * (gg + 1))
            cm = c_ref[:, 128 * gg:128 * (gg + 1)]
            xb_ = x_ref[:, cols].astype(BF16)
            s_ = _dot_nt(cm.astype(BF16), b_ref[:, 128 * gg:128 * (gg + 1)].astype(BF16))
            la, dtt_, lat_ = lag_ref[gg], dtt_ref[8 * gg:8 * (gg + 1)], lat_ref[8 * gg:8 * (gg + 1)]
            elam = jnp.exp(la)
            yh = [jnp.zeros((Q, 128), F32), jnp.zeros((Q, 128), F32)]
            for d, h_ref in enumerate((hf_ref, hb_ref)):
                rhs = jnp.concatenate([xb_, h_ref[:, cols].astype(BF16)], axis=0)
                lhs = []
                for r in range(HPG):
                    j = 4 * d + r
                    lm = jnp.where(masks[d], jnp.exp(la[:, j:j + 1] - lat_[j:j + 1, :]), 0.0)
                    w = s_ * lm * dtt_[j:j + 1, :]
                    lhs.append(jnp.concatenate([w, cm * elam[:, j:j + 1]], axis=1).astype(BF16))
                for b in range(HPG // 2):
                    ypair = _dot(jnp.concatenate(lhs[2 * b:2 * b + 2], axis=0), rhs[:, 128 * b:128 * (b + 1)])
                    yh[b] = yh[b] + jnp.where(lane < 64, ypair[0:Q], ypair[Q:2 * Q])
            y_ref[:, cols] = jnp.concatenate(yh, axis=1).astype(BF16)

    nb = NG // gps
    sm = pl.BlockSpec((gps, Q, 128), lambda c, g: (g, c + ncc, 0))
    smt = pl.BlockSpec((8 * gps, Q), lambda c, g: (g, c + ncc))
    st3 = pl.BlockSpec((None, 128, 256 * gps), lambda c, g: (c + ncc, 0, g))
    return pl.pallas_call(
        kern, name="ssd_out", grid=(ncx, nb),
        in_specs=[pl.BlockSpec((Q, 256 * gps), lambda c, g: (c + ncc, g)),
                  pl.BlockSpec((Q, 128 * gps), lambda c, g: (c + ncc, 2 * nb + g)),
                  pl.BlockSpec((Q, 128 * gps), lambda c, g: (c + ncc, 3 * nb + g)), sm, sm, smt, smt, st3, st3],
        out_specs=pl.BlockSpec((Q, 256 * gps), lambda c, g: (c, g)),
        out_shape=jax.ShapeDtypeStruct((ncx * Q, DI), BF16),
        compiler_params=_cp(("parallel", "parallel")),
    )(xbc, xbc, xbc, dtg, lag, dtt, lat, htf, htb)


def _ssd_bwd_state(xbc, dy, lag, ncc):
    lext = xbc.shape[0]
    nc = lext // Q

    def kern(cf_ref, dyf_ref, laf_ref, cb_ref, dyb_ref, lab_ref, df_ref, db_ref, sf, sb):
        @pl.when(pl.program_id(0) == 0)
        def _():
            sf[...] = jnp.zeros_like(sf)
            sb[...] = jnp.zeros_like(sb)

        for d, (c_ref, dy_ref, la_ref, o_ref, s) in enumerate(
                ((cf_ref, dyf_ref, laf_ref, df_ref, sf), (cb_ref, dyb_ref, lab_ref, db_ref, sb))):
            o_ref[...] = s[...]
            ex = _expand_sel(d)
            for g in range(NG):
                cols = slice(256 * g, 256 * (g + 1))
                la = la_ref[g]
                tot = la[Q - 1:Q] if d == 0 else la[0:1]
                eexp = _dot2_r(jnp.exp(la), ex)
                dexp = _dot2_r(jnp.broadcast_to(jnp.exp(tot), (8, 128)), ex)[0:1]
                dye = (dy_ref[:, cols] * eexp).astype(BF16)
                s[:, cols] = s[:, cols] * dexp + _dot(c_ref[:, 128 * g:128 * (g + 1)].T.astype(BF16), dye)

    cf = lambda j: nc - 1 - j
    cb = lambda j: _chunk_of_bwd_dir(nc - 1 - j, ncc, nc)
    sm = lambda f: pl.BlockSpec((NG, Q, 128), lambda j: (0, f(j), 0))
    return pl.pallas_call(
        kern, name="ssd_bwd_state", grid=(nc,),
        in_specs=[pl.BlockSpec((Q, NG * NS), lambda j: (cf(j), 3)), pl.BlockSpec((Q, DI), lambda j: (cf(j), 0)), sm(cf),
                  pl.BlockSpec((Q, NG * NS), lambda j: (cb(j), 3)), pl.BlockSpec((Q, DI), lambda j: (cb(j), 0)), sm(cb)],
        out_specs=[pl.BlockSpec((None, 128, DI), lambda j: (cf(j), 0, 0)),
                   pl.BlockSpec((None, 128, DI), lambda j: (cb(j), 0, 0))],
        out_shape=[jax.ShapeDtypeStruct((nc, 128, DI), F32), jax.ShapeDtypeStruct((nc, 128, DI), F32)],
        scratch_shapes=[pltpu.VMEM((128, DI), F32), pltpu.VMEM((128, DI), F32)],
        compiler_params=_cp(("arbitrary",)),
    )(xbc, dy, lag, xbc, dy, lag)


def _ssd_bwd_out(xbc, dy, dsk, dtg, lag, dtt, lat, htf, htb, dhf, dhb, a_rows):
    lext = xbc.shape[0]
    nc = lext // Q

    gps = 1

    def kern(x_ref, b_ref, c_ref, dy_ref, sk_ref, dtg_ref, lag_ref, dtt_ref, lat_ref, hf_ref, hb_ref, df_ref, db_ref,
             a_ref, dx_ref, dbo_ref, dco_ref, ddt_ref, ga_ref):
        @pl.when(pl.program_id(1) == 0)
        def _():
            ga_ref[...] = jnp.zeros_like(ga_ref)

        for gg in range(gps):
            one_group(gg, x_ref, b_ref, c_ref, dy_ref, sk_ref, dtg_ref, lag_ref, dtt_ref, lat_ref, hf_ref, hb_ref, df_ref,
                      db_ref, a_ref, dx_ref, dbo_ref, dco_ref, ddt_ref, ga_ref)

    def one_group(gg, x_ref, b_ref, c_ref, dy_ref, sk_ref, dtg_ref, lag_ref, dtt_ref, lat_ref, hf_ref, hb_ref, df_ref,
                  db_ref, a_ref, dx_ref, dbo_ref, dco_ref, ddt_ref, ga_ref):
        g = pl.program_id(0) * gps + gg
        cols, cols128 = slice(256 * gg, 256 * (gg + 1)), slice(128 * gg, 128 * (gg + 1))
        x, bm, cm, dy_ = x_ref[:, cols], b_ref[:, cols128], c_ref[:, cols128], dy_ref[:, cols]
        xb_, bb_, cb_, dyb_ = x.astype(BF16), bm.astype(BF16), cm.astype(BF16), dy_.astype(BF16)
        st = _dot_nt(bb_, cb_)
        si, li = _iota((Q, Q), 0), _iota((Q, Q), 1)
        lane = _iota((Q, 256), 1)
        lane128 = _iota((Q, 128), 1)
        row128 = _iota((Q, 128), 0)
        sub = _iota((128, Q), 0)
        la, dt = lag_ref[gg], dtg_ref[gg]
        dtt_, lat_ = dtt_ref[8 * gg:8 * (gg + 1)], lat_ref[8 * gg:8 * (gg + 1)]
        elam = jnp.exp(la)
        dst = jnp.zeros((Q, Q), F32)
        dxh = [jnp.zeros((Q, 128), F32), jnp.zeros((Q, 128), F32)]
        cdir, clam = jnp.zeros((Q, 128), F32), jnp.zeros((Q, 128), F32)
        dba = jnp.zeros((Q, 128), F32)
        dca = jnp.zeros((Q, 128), F32)
        dlam = jnp.zeros((Q, 128), F32)
        ddir = jnp.zeros((Q, 128), F32)
        rows = jnp.zeros((128, Q), F32)
        per_dir = []
        for d, (h_ref, dh_ref) in enumerate(((hf_ref, df_ref), (hb_ref, db_ref))):
            ht, dht = h_ref[:, cols], dh_ref[:, cols]
            htb_, dhtb_ = ht.astype(BF16), dht.astype(BF16)
            tot, w_end, wnd = _dir_terms(la, dt, d)
            ex, rs = _expand_sel(d), _reduce_sel(d)
            elx = _dot2_r(elam, ex)
            wex = _dot2_r(w_end, ex)
            dye = dy_ * elx
            ch = _dot(cb_, htb_)
            bd = _dot(bb_, dhtb_)
            dca = dca + _dot_nt(dye.astype(BF16), htb_)
            dba = dba + _dot_nt((x * wex).astype(BF16), dhtb_)
            dlam = dlam + _dot2_r(dye * ch, rs)
            xbd = _dot2_r(x * bd, rs)
            e_ = w_end * xbd
            dlam = dlam - e_
            ddir = ddir + wnd * xbd
            hh = _dot2_r(jnp.broadcast_to(jnp.sum(dht * ht, axis=0, keepdims=True), (8, 256)), rs)[0:1]
            tot_term = jnp.sum(e_, axis=0, keepdims=True) + jnp.exp(tot) * hh
            dlam = dlam + jnp.where(row128 == (Q - 1 if d == 0 else 0), tot_term, 0.0)
            per_dir.append((w_end, jnp.concatenate([dyb_, dhtb_], axis=0), (li >= si) if d == 0 else (li <= si)))
        for r in range(HPG):
            half = slice(128 * (r // 2), 128 * (r // 2 + 1))
            hm = (lane128 >> 6) == (r % 2)
            dwt = _dot_nt(jnp.where(hm, x[:, half], 0.0).astype(BF16), dyb_[:, half])
            q = dwt * st
            for d, (w_end, rhs, maskt) in enumerate(per_dir):
                j = 4 * d + r
                dc = dt[:, j:j + 1]
                lmt = jnp.where(maskt, jnp.exp(lat_[j:j + 1, :] - la[:, j:j + 1]), 0.0)
                ldc = lmt * jnp.broadcast_to(dc, (Q, Q))
                lhs = jnp.concatenate([st * ldc, bm * w_end[:, j:j + 1]], axis=1).astype(BF16)
                dxh[r // 2] = dxh[r // 2] + jnp.where(hm, _dot(lhs, rhs[:, half]), 0.0)
                cs = jnp.sum(q * lmt, axis=1, keepdims=True)
                cdir = jnp.where(lane128 == j, cs, cdir)
                clam = jnp.where(lane128 == j, cs * dc, clam)
                rows = rows + jnp.where(sub == j, jnp.sum(q * ldc, axis=0, keepdims=True), 0.0)
                dst = dst + dwt * ldc
        dxa = jnp.concatenate(dxh, axis=1)
        ddir = ddir + cdir
        dlam = dlam - clam + rows.T
        dba = dba + _dot(dst.astype(BF16), cb_)
        dca = dca + _dot(dst.T.astype(BF16), bb_)
        isb = jnp.logical_and(lane128 >= 4, lane128 < 8)
        ddel = jnp.where(isb, _dot2_l(_tri(True), dlam), _dot2_l(_tri(False), dlam))
        a_l = a_ref[pl.ds(g, 1), :]
        ddt_ref[gg] = ddir + a_l * ddel
        ga_ref[gg] += jnp.broadcast_to(a_l * jnp.sum(dt * ddel, axis=0, keepdims=True), (8, 128))
        dx_ref[:, cols] = dxa + dy_ * sk_ref[:, cols]
        dbo_ref[:, cols128] = dba
        dco_ref[:, cols128] = dca

    nb = NG // gps
    st3 = pl.BlockSpec((None, 128, 256 * gps), lambda g, c: (c, 0, g))
    sm = pl.BlockSpec((gps, Q, 128), lambda g, c: (g, c, 0))
    smt = pl.BlockSpec((8 * gps, Q), lambda g, c: (g, c))
    wide = pl.BlockSpec((Q, 256 * gps), lambda g, c: (c, g))
    return pl.pallas_call(
        kern, name="ssd_bwd_out", grid=(nb, nc),
        in_specs=[wide, pl.BlockSpec((Q, 128 * gps), lambda g, c: (c, 2 * nb + g)),
                  pl.BlockSpec((Q, 128 * gps), lambda g, c: (c, 3 * nb + g)), wide,
                  pl.BlockSpec((1, 256 * gps), lambda g, c: (0, g)), sm, sm, smt, smt, st3, st3, st3, st3,
                  pl.BlockSpec((8, 128), lambda g, c: (0, 0))],
        out_specs=[wide, pl.BlockSpec((Q, 128 * gps), lambda g, c: (c, g)),
                   pl.BlockSpec((Q, 128 * gps), lambda g, c: (c, g)), sm, pl.BlockSpec((gps, 8, 128), lambda g, c: (g, 0, 0))],
        out_shape=[jax.ShapeDtypeStruct((lext, DI), F32), jax.ShapeDtypeStruct((lext, NG * NS), F32),
                   jax.ShapeDtypeStruct((lext, NG * NS), F32), jax.ShapeDtypeStruct((NG, lext, 128), F32),
                   jax.ShapeDtypeStruct((NG, 8, 128), F32)],
        compiler_params=_cp(("parallel", "arbitrary")),
    )(xbc, xbc, xbc, dy, dsk, dtg, lag, dtt, lat, htf, htb, dhf, dhb, a_rows)


def _post_fwd(yssm, xbc, proj_rest, dsk, gnw, nct):
    l = yssm.shape[0]

    def kern(y_ref, x_ref, z_ref, dsk_ref, w_ref, o_ref):
        y = y_ref[...].astype(F32) + dsk_ref[...] * x_ref[...]
        yz = y * _silu(z_ref[...].astype(F32))
        for g in range(NG):
            sl = slice(256 * g, 256 * (g + 1))
            v = yz[:, sl]
            r = lax.rsqrt(jnp.mean(v * v, axis=-1, keepdims=True) + EPS)
            o_ref[:, sl] = (v * r * w_ref[:, sl]).astype(BF16)

    return pl.pallas_call(
        kern, name="post_fwd", grid=(l // T,),
        in_specs=[pl.BlockSpec((T, DI), lambda i: (i, 0)), pl.BlockSpec((T, DI), lambda i: (i + nct, 0)),
                  pl.BlockSpec((T, DI), lambda i: (i, 0)), pl.BlockSpec((1, DI), lambda i: (0, 0)),
                  pl.BlockSpec((1, DI), lambda i: (0, 0))],
        out_specs=pl.BlockSpec((T, DI), lambda i: (i, 0)),
        out_shape=jax.ShapeDtypeStruct((l, DI), BF16),
        compiler_params=_cp(("parallel",)),
    )(yssm, xbc, proj_rest, dsk, gnw)


def _post_bwd(dgn, yssm, xbc, proj_rest, dsk, gnw, dpr, nct):
    l = yssm.shape[0]
    lext = xbc.shape[0]
    xi = lambda i: (jnp.maximum(i - nct, 0), 0)

    def kern(dg_ref, y_ref, x_ref, z_ref, dsk_ref, w_ref, dpr_ref, dy_ref, dz_ref, gw_ref, gd_ref):
        i = pl.program_id(0)

        @pl.when(i == 0)
        def _():
            gw_ref[...] = jnp.zeros_like(gw_ref)
            gd_ref[...] = jnp.zeros_like(gd_ref)

        @pl.when(i < nct)
        def _():
            dy_ref[...] = jnp.zeros_like(dy_ref)

        @pl.when(i >= nct)
        def _():
            xs = x_ref[...]
            z = z_ref[...].astype(F32)
            y = y_ref[...].astype(F32) + dsk_ref[...] * xs
            sz = _silu(z)
            yz = y * sz
            dgn_ = dg_ref[...].astype(F32)
            dyz_parts = []
            gws = []
            for g in range(NG):
                sl = slice(256 * g, 256 * (g + 1))
                v = yz[:, sl]
                r = lax.rsqrt(jnp.mean(v * v, axis=-1, keepdims=True) + EPS)
                vn = v * r
                dn = dgn_[:, sl] * w_ref[:, sl]
                gws.append(jnp.sum(dgn_[:, sl] * vn, axis=0, keepdims=True))
                dyz_parts.append(r * (dn - vn * jnp.mean(dn * vn, axis=-1, keepdims=True)))
            dyz = jnp.concatenate(dyz_parts, axis=1)
            gw_ref[...] += jnp.broadcast_to(jnp.concatenate(gws, axis=1), (8, DI))
            dy = dyz * sz
            dz_ref[...] = (dyz * y * _dsilu(z)).astype(BF16)
            gd_ref[...] += jnp.broadcast_to(jnp.sum(dy * xs, axis=0, keepdims=True), (8, DI))
            dy_ref[...] = dy

    return pl.pallas_call(
        kern, name="post_bwd", grid=(lext // T,),
        in_specs=[pl.BlockSpec((T, DI), xi), pl.BlockSpec((T, DI), xi), pl.BlockSpec((T, DI), lambda i: (i, 0)),
                  pl.BlockSpec((T, DI), xi), pl.BlockSpec((1, DI), lambda i: (0, 0)), pl.BlockSpec((1, DI), lambda i: (0, 0)),
                  pl.BlockSpec(memory_space=pl.ANY)],
        out_specs=[pl.BlockSpec((T, DI), lambda i: (i, 0)),
                   pl.BlockSpec((T, DI), xi), pl.BlockSpec((8, DI), lambda i: (0, 0)), pl.BlockSpec((8, DI), lambda i: (0, 0))],
        out_shape=[jax.ShapeDtypeStruct((lext, DI), F32),
                   jax.ShapeDtypeStruct((l, RESTW), BF16), jax.ShapeDtypeStruct((8, DI), F32), jax.ShapeDtypeStruct((8, DI), F32)],
        input_output_aliases={6: 1},
        compiler_params=_cp(("arbitrary",)),
    )(dgn, yssm, xbc, proj_rest, dsk, gnw, dpr)


C_G1, C_G2, C_GA, C_GB, C_CG = 2, 3, 4, 5, 6
PITCH = GW + 16
NROW = T // GW


GAP = PITCH - GW
PADR = GAP + NROW * PITCH
NSTRIP = D // 128


def _fill_padded(pad8, val):
    z = jnp.zeros((GAP, D), F32)
    parts = [z]
    for r in range(NROW):
        parts += [val[GW * r:GW * (r + 1)], z]
    p = jnp.concatenate(parts, axis=0)
    pad8[0] = p
    for j in range(1, pad8.shape[0]):
        pad8[j] = pltpu.roll(p, PADR - j, axis=0)


def _tap(pad8, base, off, ln):
    return pad8[off % 8, pl.ds(base + off - off % 8, GW), ln]


def _row_conv(out_ref, pad8, w_ref, transpose):
    def strip(s, carry):
        ln = pl.ds(pl.multiple_of(s * 128, 128), 128)
        for r in range(NROW):
            base = GAP + PITCH * r
            acc = jnp.zeros((GW, 128), F32)
            for k in range(CK):
                off = (k - 15) if not transpose else (15 - k)
                acc = acc + w_ref[pl.ds(k, 1), ln] * _tap(pad8, base, off, ln)
            out_ref[pl.ds(GW * r, GW), ln] = acc
        return carry

    lax.fori_loop(0, NSTRIP, strip, 0)


def _row_conv_wgrad(gcw_ref, padd8, pada8):
    def strip(s, carry):
        ln = pl.ds(pl.multiple_of(s * 128, 128), 128)
        rid = _iota((32, 128), 0)
        g = jnp.zeros((32, 128), F32)
        for k0 in range(0, CK, 8):
            taps = range(k0, min(k0 + 8, CK))
            accs = {k: jnp.zeros((8, 128), F32) for k in taps}
            for r in range(NROW):
                base = GAP + PITCH * r
                d = _tap(padd8, base, 0, ln)
                for k in taps:
                    p = d * pada8[0, pl.ds(base + k - 15, GW), ln]
                    part = p[0:8]
                    for q in range(1, GW // 8):
                        part = part + p[8 * q:8 * (q + 1)]
                    accs[k] = accs[k] + part
            for k in taps:
                g = jnp.where(rid == k, jnp.sum(accs[k], axis=0, keepdims=True), g)
        gcw_ref[:, ln] += g
        return carry

    lax.fori_loop(0, NSTRIP, strip, 0)


def _ln_stats(cv):
    mu = jnp.mean(cv, axis=-1, keepdims=True)
    xc = cv - mu
    rs = lax.rsqrt(jnp.mean(xc * xc, axis=-1, keepdims=True) + EPS)
    return xc * rs, rs


def _conf_fwd(proj_rest, cw, cb, lw, lb):
    l = proj_rest.shape[0]

    def kern(ga_ref, gb_ref, cg_ref, cw_ref, cb_ref, lw_ref, lb_ref, o_ref, cv_ref, pad8):
        _fill_padded(pad8, ga_ref[...].astype(F32) * _sig(gb_ref[...].astype(F32)))
        _row_conv(cv_ref, pad8, cw_ref, False)
        cv = cv_ref[...] + cb_ref[...]
        cv_ref[...] = cv
        xh, _ = _ln_stats(cv)
        ln = xh * lw_ref[...] + lb_ref[...]
        o_ref[...] = (_silu(ln) * _silu(cg_ref[...].astype(F32))).astype(BF16)

    vec = pl.BlockSpec((1, D), lambda i: (0, 0))
    blk = pl.BlockSpec((T, D), lambda i: (i, 0))
    return pl.pallas_call(
        kern, name="conf_fwd", grid=(l // T,),
        in_specs=[pl.BlockSpec((T, D), lambda i: (i, C_GA)), pl.BlockSpec((T, D), lambda i: (i, C_GB)),
                  pl.BlockSpec((T, D), lambda i: (i, C_CG)), pl.BlockSpec((32, D), lambda i: (0, 0)), vec, vec, vec],
        out_specs=[blk, blk],
        out_shape=[jax.ShapeDtypeStruct((l, D), BF16), jax.ShapeDtypeStruct((l, D), F32)],
        scratch_shapes=[pltpu.VMEM((8, PADR, D), F32)],
        compiler_params=_cp(("parallel",)),
    )(proj_rest, proj_rest, proj_rest, cw, cb, lw, lb)


def _conf_bwd(duc, cv, proj_rest, cw, lw, lb, dpr):
    l = proj_rest.shape[0]

    def kern(du_ref, cv_ref, ga_ref, gb_ref, cg_ref, cw_ref, lw_ref, lb_ref, dpr_ref, o_ref, gcw_ref, gv_ref, sc,
             pada, padd, da_ref):
        i, j = pl.program_id(0), pl.program_id(1)

        @pl.when(jnp.logical_and(i == 0, j == 0))
        def _():
            gcw_ref[...] = jnp.zeros_like(gcw_ref)
            gv_ref[...] = jnp.zeros_like(gv_ref)

        @pl.when(j == 0)
        def _():
            ga, gb, cg = ga_ref[...].astype(F32), gb_ref[...].astype(F32), cg_ref[...].astype(F32)
            sg = _sig(gb)
            xh, rs = _ln_stats(cv_ref[...])
            ln = xh * lw_ref[...] + lb_ref[...]
            du = du_ref[...].astype(F32)
            sc[:, 2 * D:3 * D] = (du * _silu(ln) * _dsilu(cg)).astype(BF16)
            dln = du * _silu(cg) * _dsilu(ln)
            g_lw = jnp.sum(dln * xh, axis=0, keepdims=True)
            g_lb = jnp.sum(dln, axis=0, keepdims=True)
            dxh = dln * lw_ref[...]
            dcv = rs * (dxh - jnp.mean(dxh, axis=-1, keepdims=True) - xh * jnp.mean(dxh * xh, axis=-1, keepdims=True))
            g_cb = jnp.sum(dcv, axis=0, keepdims=True)
            rid = _iota((8, D), 0)
            gv_ref[...] += jnp.where(rid == 0, g_cb, jnp.where(rid == 1, g_lw, jnp.where(rid == 2, g_lb, 0.0)))
            _fill_padded(padd, dcv)
            _fill_padded(pada, ga * sg)
            _row_conv(da_ref, padd, cw_ref, True)
            _row_conv_wgrad(gcw_ref, padd, pada)
            da = da_ref[...]
            sc[:, 0:D] = (da * sg).astype(BF16)
            sc[:, D:2 * D] = (da * ga * sg * (1.0 - sg)).astype(BF16)

        o_ref[...] = sc[:, pl.ds(pl.multiple_of(j * D, 128), D)]

    vec = pl.BlockSpec((1, D), lambda i, j: (0, 0))
    col = lambda c: pl.BlockSpec((T, D), lambda i, j: (i, c))
    return pl.pallas_call(
        kern, name="conf_bwd", grid=(l // T, 3),
        in_specs=[col(0), col(0), col(C_GA), col(C_GB), col(C_CG), pl.BlockSpec((32, D), lambda i, j: (0, 0)), vec, vec,
                  pl.BlockSpec(memory_space=pl.ANY)],
        out_specs=[pl.BlockSpec((T, D), lambda i, j: (i, C_GA + j)), pl.BlockSpec((32, D), lambda i, j: (0, 0)),
                   pl.BlockSpec((8, D), lambda i, j: (0, 0))],
        out_shape=[jax.ShapeDtypeStruct((l, RESTW), BF16), jax.ShapeDtypeStruct((32, D), F32),
                   jax.ShapeDtypeStruct((8, D), F32)],
        scratch_shapes=[pltpu.VMEM((T, 3 * D), BF16), pltpu.VMEM((1, PADR, D), F32), pltpu.VMEM((8, PADR, D), F32),
                        pltpu.VMEM((T, D), F32)],
        input_output_aliases={8: 0},
        compiler_params=_cp(("arbitrary", "arbitrary")),
    )(duc, cv, proj_rest, proj_rest, proj_rest, cw, lw, lb, dpr)


def _merge_fwd(bs, bc, proj_rest):
    l = bs.shape[0]

    def kern(bs_ref, bc_ref, g1_ref, g2_ref, o_ref):
        up = lambda r: r[...].astype(F32)
        o_ref[...] = (_sig(up(g1_ref)) * up(bs_ref) + _sig(up(g2_ref)) * up(bc_ref)).astype(BF16)

    blk = pl.BlockSpec((TX, D), lambda i: (i, 0))
    return pl.pallas_call(
        kern, name="merge_fwd", grid=(l // TX,),
        in_specs=[blk, blk, pl.BlockSpec((TX, D), lambda i: (i, C_G1)), pl.BlockSpec((TX, D), lambda i: (i, C_G2))],
        out_specs=blk, out_shape=jax.ShapeDtypeStruct((l, D), BF16),
        compiler_params=_cp(("parallel",)),
    )(bs, bc, proj_rest, proj_rest)


def _merge_bwd(dm, bs, bc, proj_rest):
    l = bs.shape[0]

    def kern(dm_ref, bs_ref, bc_ref, g1_ref, g2_ref, dbs_ref, dbc_ref, dg_ref):
        up = lambda r: r[...].astype(F32)
        dm_ = up(dm_ref)
        s1, s2 = _sig(up(g1_ref)), _sig(up(g2_ref))
        dbs_ref[...] = (dm_ * s1).astype(BF16)
        dbc_ref[...] = (dm_ * s2).astype(BF16)
        dg_ref[:, 0:D] = (dm_ * up(bs_ref) * s1 * (1.0 - s1)).astype(BF16)
        dg_ref[:, D:2 * D] = (dm_ * up(bc_ref) * s2 * (1.0 - s2)).astype(BF16)

    blk = pl.BlockSpec((TX, D), lambda i: (i, 0))
    return pl.pallas_call(
        kern, name="merge_bwd", grid=(l // TX,),
        in_specs=[blk, blk, blk, pl.BlockSpec((TX, D), lambda i: (i, C_G1)), pl.BlockSpec((TX, D), lambda i: (i, C_G2))],
        out_specs=[blk, blk, pl.BlockSpec((TX, 2 * D), lambda i: (i, 1))],
        out_shape=[jax.ShapeDtypeStruct((l, D), BF16), jax.ShapeDtypeStruct((l, D), BF16),
                   jax.ShapeDtypeStruct((l, RESTW), BF16)],
        compiler_params=_cp(("parallel",)),
    )(dm, bs, bc, proj_rest, proj_rest)


def _final(x, out, tgt, mod, fw):
    l = x.shape[0]

    def kern(x_ref, o_ref, t_ref, mod_ref, fw_ref, ls_ref, dx2_ref, do_ref, gv_ref):
        @pl.when(pl.program_id(0) == 0)
        def _():
            ls_ref[...] = jnp.zeros_like(ls_ref)
            gv_ref[...] = jnp.zeros_like(gv_ref)

        gate = mod_ref[0:1, 2 * D:3 * D]
        o = o_ref[...]
        x2 = x_ref[...] + gate * o
        r = lax.rsqrt(jnp.mean(x2 * x2, axis=-1, keepdims=True) + EPS)
        yn = x2 * r
        fw_ = fw_ref[...]
        e = yn * fw_ - t_ref[...]
        ls_ref[...] += jnp.full((8, 128), 1.0, F32) * (0.5 / D) * jnp.sum(e * e)
        dy = e * (1.0 / D)
        g_fw = jnp.sum(dy * yn, axis=0, keepdims=True)
        dyn = dy * fw_
        dx2 = r * (dyn - yn * jnp.mean(dyn * yn, axis=-1, keepdims=True))
        g_gate = jnp.sum(dx2 * o, axis=0, keepdims=True)
        rid = _iota((8, D), 0)
        gv_ref[...] += jnp.where(rid == 0, g_fw, jnp.where(rid == 1, g_gate, 0.0))
        dx2_ref[...] = dx2
        do_ref[...] = (dx2 * gate).astype(BF16)

    blk = pl.BlockSpec((TX, D), lambda i: (i, 0))
    return pl.pallas_call(
        kern, name="final", grid=(l // TX,),
        in_specs=[blk, blk, blk, pl.BlockSpec((8, 3 * D), lambda i: (0, 0)), pl.BlockSpec((1, D), lambda i: (0, 0))],
        out_specs=[pl.BlockSpec((8, 128), lambda i: (0, 0)), blk, blk, pl.BlockSpec((8, D), lambda i: (0, 0))],
        out_shape=[jax.ShapeDtypeStruct((8, 128), F32), jax.ShapeDtypeStruct((l, D), F32),
                   jax.ShapeDtypeStruct((l, D), BF16), jax.ShapeDtypeStruct((8, D), F32)],
        compiler_params=_cp(("arbitrary",)),
    )(x, out, tgt, mod, fw)


def _perm_dt_cols(w):
    s = w.shape[:-1]
    return w.reshape(*s, 2, NG, HPG).swapaxes(-3, -2).reshape(*s, 64)


def _unperm_dt_cols(w):
    s = w.shape[:-1]
    return w.reshape(*s, NG, 2, HPG).swapaxes(-3, -2).reshape(*s, 64)


def _pad_lanes(v, width):
    return jnp.pad(v, ((0, 0), (0, width - v.shape[1])))


def _vcols(segs, a, b):
    parts, off = [], 0
    for s in segs:
        lo, hi = max(a, off), min(b, off + s.shape[1])
        if lo < hi:
            parts.append(s[:, lo - off:hi - off])
        off += s.shape[1]
    return parts[0] if len(parts) == 1 else jnp.concatenate(parts, axis=1)


def _local_step(x, c, ctx, tgt, w):
    l = x.shape[0]
    nct = CTX // T
    ncc = CTX // Q
    lext = l + CTX

    w_mod = w["w_mod"].astype(BF16)
    wsegs = [s.astype(BF16) for s in (w["w_in"] if isinstance(w["w_in"], (list, tuple)) else [w["w_in"]])]
    w_ssd = jnp.concatenate([_vcols(wsegs, 0, XBC), _perm_dt_cols(_vcols(wsegs, XBC, XBC + 64)), jnp.zeros((D, 64), BF16)], axis=1)
    r0 = XBC + 64
    w_rest = jnp.concatenate([_vcols(wsegs, r0, r0 + DI), _vcols(wsegs, r0 + DI + 3 * D, r0 + RESTW),
                              _vcols(wsegs, r0 + DI, r0 + DI + 3 * D)], axis=1)
    w_os, w_oc, w_o = w["w_out_ssm"].astype(BF16), w["w_out_conf"].astype(BF16), w["w_out"].astype(BF16)
    cw8 = jnp.pad(w["ssm_conv_w"], ((0, 4), (0, 0)))
    cb_s = w["ssm_conv_b"].reshape(1, XBC)
    dtb = _pad_lanes(_perm_dt_cols(w["dt_bias"].reshape(1, 64)), 128)
    a_all = -jnp.exp(w["a_log"].reshape(1, 64))
    a_perm = _pad_lanes(_perm_dt_cols(a_all), 128)
    a_rows = _pad_lanes(_perm_dt_cols(a_all).reshape(NG, 8), 128)
    dsk = jnp.repeat(w["d_skip"].reshape(NH), HP).reshape(1, DI)
    gnw = w["ssm_norm_w"].reshape(1, DI)
    ccw = jnp.pad(w["conf_conv_w"], ((0, 1), (0, 0)))
    ccb, clw, clb = w["conf_conv_b"].reshape(1, D), w["conf_ln_w"].reshape(1, D), w["conf_ln_b"].reshape(1, D)
    nw = w["norm_w"].reshape(1, D)
    fw = w["final_norm_w"].reshape(1, D)
    cc = jnp.concatenate([c.reshape(1, D), w["c_ctx"].reshape(1, D), jnp.zeros((6, D), F32)], axis=0)

    bx = min(1024, l)
    be = 768 if lext % 768 == 0 else 256
    tk = min(1024, l)
    mod = _mod_fwd(cc, w_mod, w["b_mod"].reshape(1, 3 * D))
    h = _norm_fwd(ctx, x, mod, nw, nct)
    hx = h[CTX:]
    proj_ssd = _mm(h, w_ssd, "nn", lext, SSDW, D, be, SSDW // 3, D, F32, "proj_ssd")
    proj_rest = _mm(hx, w_rest, "nn", l, RESTW, D, bx, 1024, D, BF16, "proj_rest")
    xbc = _conv_fwd(proj_ssd, cw8, cb_s, nct)
    dtg, lag, dtt, lat = _dt_fwd(proj_ssd, dtb, a_perm)
    htf, htb = _ssd_state(xbc, dtg, lag, ncc)
    yssm = _ssd_out(xbc, dtg, lag, dtt, lat, htf, htb, ncc)
    gn = _post_fwd(yssm, xbc, proj_rest, dsk, gnw, nct)
    bs = _mm(gn, w_os, "nn", l, D, DI, bx, D, DI, BF16, "out_ssm")
    uc, cv = _conf_fwd(proj_rest, ccw, ccb, clw, clb)
    bc = _mm(uc, w_oc, "nn", l, D, D, bx, D, D, BF16, "out_conf")
    merged = _merge_fwd(bs, bc, proj_rest)
    out = _mm(merged, w_o, "nn", l, D, D, bx, D, D, F32, "out_proj")
    lsum, dx2, dout, gv_fin = _final(x, out, tgt, mod, fw)

    g = {}
    g["final_norm_w"] = gv_fin[0]
    dmerged = _mm(dout, w_o, "nt", l, D, D, bx, D, D, BF16, "d_merged")
    g["w_out"] = _mm(merged, dout, "tn", D, D, l, D, D, tk, F32, "g_w_out")
    dbs, dbc, dpr = _merge_bwd(dmerged, bs, bc, proj_rest)
    dgn = _mm(dbs, w_os, "nt", l, DI, D, bx, DI, D, BF16, "d_gn")
    g["w_out_ssm"] = _mm(gn, dbs, "tn", DI, D, l, DI, D, tk, F32, "g_w_out_ssm")
    duc = _mm(dbc, w_oc, "nt", l, D, D, bx, D, D, BF16, "d_uc")
    g["w_out_conf"] = _mm(uc, dbc, "tn", D, D, l, D, D, tk, F32, "g_w_out_conf")
    dpr, gcw, gv_conf = _conf_bwd(duc, cv, proj_rest, ccw, clw, clb, dpr)
    g["conf_conv_w"] = gcw[:CK]
    g["conf_conv_b"], g["conf_ln_w"], g["conf_ln_b"] = gv_conf[0], gv_conf[1], gv_conf[2]
    dy, dproj_rest, ggnw, gdsk = _post_bwd(dgn, yssm, xbc, proj_rest, dsk, gnw, dpr, nct)
    g["ssm_norm_w"] = ggnw[0]
    g["d_skip"] = gdsk[0].reshape(NH, HP).sum(axis=1)
    dhf, dhb = _ssd_bwd_state(xbc, dy, lag, ncc)
    dxs, dbm, dcm, ddtg, galog = _ssd_bwd_out(xbc, dy, dsk, dtg, lag, dtt, lat, htf, htb, dhf, dhb, a_rows)
    g["a_log"] = _unperm_dt_cols(galog[:, 0, 0:8].reshape(1, 64)).reshape(2, NH)
    dus, gws, gbs = [], [], []
    for dpost, off, width, nm in ((dxs, 0, DI, "conv_bwd_x"), (dbm, DI, NG * NS, "conv_bwd_b"), (dcm, DI + NG * NS, NG * NS, "conv_bwd_c")):
        du_, gw_, gb_ = _conv_bwd(dpost, proj_ssd, cw8, cb_s, off, width, nct, nm)
        dus.append(du_)
        gws.append(gw_[:SK])
        gbs.append(gb_[0])
    g["ssm_conv_w"] = jnp.concatenate(gws, axis=1)
    g["ssm_conv_b"] = jnp.concatenate(gbs, axis=0)
    ddt_raw, gdtb = _dt_bwd(ddtg, proj_ssd, dtb)
    g["dt_bias"] = _unperm_dt_cols(gdtb[0:1, 0:64]).reshape(2, NH)
    dproj_ssd = jnp.concatenate(dus + [ddt_raw], axis=1)
    gw_ssd = _mm(h, dproj_ssd, "tn", D, SSDW, lext, D, SSDW // 3, be, F32, "g_w_ssd")
    gw_rest = _mm(hx, dproj_rest, "tn", D, RESTW, l, D, 1024, tk, F32, "g_w_rest")
    gsegs = [gw_ssd[:, :XBC], _unperm_dt_cols(gw_ssd[:, XBC:XBC + 64]), gw_rest[:, :DI], gw_rest[:, 2 * DI:],
             gw_rest[:, DI:2 * DI]]
    g["w_in"] = jnp.concatenate(gsegs, axis=1)
    g["w_in_shards"] = jnp.stack([_vcols(gsegs, R_IN * s, R_IN * (s + 1)) for s in range(NSHARD)])
    dh_a = _mm(dproj_ssd, w_ssd, "nt", lext, D, SSDW, T, D, SSDW, BF16, "dh_ssd")
    dh_b = _mm(dproj_rest, w_rest, "nt", l, D, RESTW, T, D, RESTW, BF16, "dh_rest")
    grad_x, gnw_in, dss = _norm_bwd(dh_a, dh_b, ctx, x, dx2, mod, nw, nct)
    g["norm_w"] = gnw_in[0]
    dmod = jnp.concatenate([jnp.concatenate([dss[0:1], gv_fin[1:2]], axis=1),
                            jnp.concatenate([dss[1:2], jnp.zeros((1, D), F32)], axis=1),
                            jnp.zeros((6, 3 * D), F32)], axis=0)
    gwm, gbm, gcc = _mod_bwd(dmod, cc, cc.T, w_mod)
    g["w_mod"], g["b_mod"], g["c_ctx"] = gwm, gbm[0], gcc[1]
    return lsum[0, 0], grad_x, g


NSHARD = 4
R_MOD, R_IN, R_OS, R_OC, R_O, R_SC, R_CC = 768, 2832, 512, 256, 256, 8, 8
O_MOD = 0
O_OS = O_MOD + R_MOD
O_OC = O_OS + R_OS
O_O = O_OC + R_OC
O_SC = O_O + R_O
O_CC = O_SC + R_SC
PUSED = O_CC + R_CC
PROWS = 1824
HALF = PROWS // 2
RB = HALF // 3
WB = 128
SROWS = 16
SMALL = (("b_mod", 3 * D), ("norm_w", D), ("ssm_conv_b", XBC), ("dt_bias", 64), ("a_log", 64), ("d_skip", NH),
         ("ssm_norm_w", DI), ("conf_conv_b", D), ("conf_ln_w", D), ("conf_ln_b", D), ("final_norm_w", D), ("c_ctx", D))


def _pack_shard(s):
    return jnp.concatenate([s["w_mod"].reshape(R_MOD, D), _pack_rest(s), jnp.zeros((PROWS - PUSED, D), F32)], axis=0)


def _pack_rest(s):
    cc = jnp.pad(s["conf_conv_w"].reshape(1, CK * 256), ((0, 0), (0, R_CC * D - CK * 256))).reshape(R_CC, D)
    return jnp.concatenate([s["w_out_ssm"], s["w_out_conf"], s["w_out"],
                            jnp.pad(s["ssm_conv_w"], ((0, R_SC - SK), (0, 0))), cc], axis=0)


def _unpack_rest(p):
    o = lambda r: r - O_OS
    return {"w_out_ssm": p[o(O_OS):o(O_OC)][None], "w_out_conf": p[o(O_OC):o(O_O)][None], "w_out": p[o(O_O):o(O_SC)][None],
            "ssm_conv_w": p[o(O_SC):o(O_SC) + SK][None],
            "conf_conv_w": p[o(O_CC):o(O_CC) + R_CC].reshape(R_CC * D)[:CK * 256].reshape(1, CK, 256)}


def _shard_cols(a, n):
    return a.reshape(a.shape[0], NSHARD, n).transpose(1, 0, 2)


def _pack_full(g):
    cc = jnp.pad(_shard_cols(g["conf_conv_w"], 256).reshape(NSHARD, CK * 256), ((0, 0), (0, R_CC * D - CK * 256)))
    return jnp.concatenate([_shard_cols(g["w_mod"], R_MOD).reshape(NSHARD, R_MOD, D),
                            g["w_out_ssm"].reshape(NSHARD, R_OS, D), g["w_out_conf"].reshape(NSHARD, R_OC, D),
                            g["w_out"].reshape(NSHARD, R_O, D),
                            jnp.pad(_shard_cols(g["ssm_conv_w"], D), ((0, 0), (0, R_SC - SK), (0, 0))),
                            cc.reshape(NSHARD, R_CC, D), jnp.zeros((NSHARD, PROWS - PUSED, D), F32)], axis=1)


def _unpack_gathered(gm, gw, gs):
    def cols(a, r, n):
        return a.reshape(NSHARD, r, n).transpose(1, 0, 2).reshape(r, NSHARD * n)
    return {"w_mod": cols(gm[:, O_MOD:O_OS], D, R_MOD), "w_in": [gw[s] for s in range(NSHARD)],
            "w_out_ssm": gm[:, O_OS:O_OC].reshape(DI, D), "w_out_conf": gm[:, O_OC:O_O].reshape(D, D),
            "w_out": gm[:, O_O:O_SC].reshape(D, D), "ssm_conv_w": cols(gs[:, 0:SK], SK, D),
            "conf_conv_w": cols(gs[:, R_SC:R_SC + R_CC].reshape(NSHARD, R_CC * D)[:, :CK * 256], CK, 256)}


MESH_ID = pl.DeviceIdType.MESH
ANY = pl.BlockSpec(memory_space=pl.ANY)


def _place():
    x, y, c = lax.axis_index("x"), lax.axis_index("y"), lax.axis_index("c")
    return x, y, c, [(1 - x, y), (x, 1 - y), (1 - x, 1 - y)]


def _rcopy(src, dst, send, recv, dev):
    return pltpu.make_async_remote_copy(src_ref=src, dst_ref=dst, send_sem=send, recv_sem=recv,
                                        device_id=dev, device_id_type=MESH_ID)


def _gather_weights(mats, small):
    n = len(mats)

    def kern(*refs):
        m_refs, s_ref, g_refs, gs_ref, (send, recv) = refs[:n], refs[n], refs[n + 1:2 * n + 1], refs[2 * n + 1], refs[2 * n + 2:]
        x, y, c, chips = _place()
        me = 2 * x + y
        sib = (x, y, 1 - c)
        first, passed = [], []
        for k, (px, py) in enumerate(chips):
            first.append(_rcopy(s_ref, gs_ref.at[me], send.at[k], recv.at[k], (px, py, c)))
            for a, (m_ref, g_ref) in enumerate(zip(m_refs, g_refs)):
                mine = _half_rows(c, m_ref.shape[0])
                first.append(_rcopy(m_ref.at[mine], g_ref.at[me, mine], send.at[3 + 6 * a + k], recv.at[3 + 6 * a + k], (px, py, c)))
        for cp in first:
            cp.start()
        for k, (px, py) in enumerate(chips):
            s = 2 * px + py
            for a, (m_ref, g_ref) in enumerate(zip(m_refs, g_refs)):
                mine = _half_rows(c, m_ref.shape[0])
                _rcopy(m_ref.at[mine], g_ref.at[s, mine], send.at[3 + 6 * a + k], recv.at[3 + 6 * a + k], sib).wait_recv()
                f = _rcopy(g_ref.at[s, mine], g_ref.at[s, mine], send.at[6 + 6 * a + k], recv.at[6 + 6 * a + k], sib)
                f.start()
                passed.append(f)
        for k, (px, py) in enumerate(chips):
            s = 2 * px + py
            _rcopy(s_ref, gs_ref.at[s], send.at[k], recv.at[k], sib).wait_recv()
            for a, g_ref in enumerate(g_refs):
                other = _half_rows(1 - c, g_ref.shape[1])
                _rcopy(g_ref.at[s, other], g_ref.at[s, other], send.at[6 + 6 * a + k], recv.at[6 + 6 * a + k], sib).wait_recv()
        for cp in first + passed:
            cp.wait_send()

    nsem = 3 + 6 * n
    return pl.pallas_call(
        kern, name="gather_weights", in_specs=[ANY] * (n + 1), out_specs=[ANY] * (n + 1),
        out_shape=[jax.ShapeDtypeStruct((NSHARD,) + m.shape, m.dtype) for m in mats]
        + [jax.ShapeDtypeStruct((NSHARD, SROWS, D), F32)],
        scratch_shapes=[pltpu.SemaphoreType.DMA((nsem,)), pltpu.SemaphoreType.DMA((nsem,))],
    )(*mats, small)


def _half_rows(c, rows):
    return pl.ds(pl.multiple_of(c * (rows // 2), 16), rows // 2)


def _swap_halves(gs):
    n = len(gs)

    def kern(*refs):
        g_refs, o_refs, (send, recv) = refs[:n], refs[n:2 * n], refs[2 * n:]
        x, y, c, _ = _place()
        cps = [_rcopy(g_ref.at[s, _half_rows(1 - c, g_ref.shape[1])], o_ref.at[s], send.at[NSHARD * a + s],
                      recv.at[NSHARD * a + s], (x, y, 1 - c))
               for a, (g_ref, o_ref) in enumerate(zip(g_refs, o_refs)) for s in range(NSHARD)]
        for cp in cps:
            cp.start()
        for cp in cps:
            cp.wait()

    return pl.pallas_call(
        kern, name="swap_halves", in_specs=[ANY] * n, out_specs=[ANY] * n,
        out_shape=[jax.ShapeDtypeStruct((NSHARD, g.shape[1] // 2, g.shape[2]), F32) for g in gs],
        scratch_shapes=[pltpu.SemaphoreType.DMA((NSHARD * n,)), pltpu.SemaphoreType.DMA((NSHARD * n,))],
    )(*gs)


def _add_halves(cidx, g, ra, rb, name):
    _, half, cols = ra.shape
    nb = half // rb

    def kern(c_ref, g_ref, a_ref, o_ref):
        o_ref[...] = (g_ref[...] + a_ref[...]).astype(BF16)

    return pl.pallas_call(
        kern, name=name,
        grid_spec=pltpu.PrefetchScalarGridSpec(
            num_scalar_prefetch=1, grid=(NSHARD, nb),
            in_specs=[pl.BlockSpec((None, rb, cols), lambda s, i, c: (s, c[0] * nb + i, 0)),
                      pl.BlockSpec((None, rb, cols), lambda s, i, c: (s, i, 0))],
            out_specs=pl.BlockSpec((None, rb, cols), lambda s, i, c: (s, i, 0))),
        out_shape=jax.ShapeDtypeStruct((NSHARD, half, cols), BF16),
        compiler_params=_cp(("parallel", "parallel")),
    )(cidx, g, ra)


def _exchange_chips(ps):
    n = len(ps)

    def kern(*refs):
        p_refs, o_refs, (send, recv) = refs[:n], refs[n:2 * n], refs[2 * n:]
        x, y, c, chips = _place()
        cps = [_rcopy(p_ref.at[2 * px + py], o_ref.at[k], send.at[3 * a + k], recv.at[3 * a + k], (px, py, c))
               for a, (p_ref, o_ref) in enumerate(zip(p_refs, o_refs)) for k, (px, py) in enumerate(chips)]
        for cp in cps:
            cp.start()
        for cp in cps:
            cp.wait()

    return pl.pallas_call(
        kern, name="exchange_chips", in_specs=[ANY] * n, out_specs=[ANY] * n,
        out_shape=[jax.ShapeDtypeStruct((3,) + p.shape[1:], p.dtype) for p in ps],
        scratch_shapes=[pltpu.SemaphoreType.DMA((3 * n,)), pltpu.SemaphoreType.DMA((3 * n,))],
    )(*ps)


def _add_chips(mc, g, ra, rx, rb, name):
    _, half, cols = ra.shape
    nb = half // rb

    def kern(m_ref, g_ref, a_ref, r0_ref, r1_ref, r2_ref, o_ref):
        own = g_ref[...] + a_ref[...]
        o_ref[...] = ((own + r0_ref[...].astype(F32)) + r1_ref[...].astype(F32)) + r2_ref[...].astype(F32)

    return pl.pallas_call(
        kern, name=name,
        grid_spec=pltpu.PrefetchScalarGridSpec(
            num_scalar_prefetch=1, grid=(nb,),
            in_specs=[pl.BlockSpec((None, rb, cols), lambda i, m: (m[0], m[1] * nb + i, 0)),
                      pl.BlockSpec((None, rb, cols), lambda i, m: (m[0], i, 0))]
            + [pl.BlockSpec((None, rb, cols), functools.partial(lambda i, m, k: (k, i, 0), k=k)) for k in range(3)],
            out_specs=pl.BlockSpec((rb, cols), lambda i, m: (i, 0))),
        out_shape=jax.ShapeDtypeStruct((half, cols), F32),
        compiler_params=_cp(("parallel",)),
    )(mc, g, ra, rx, rx, rx)


def _share_halves(rs):
    n = len(rs)

    def kern(*refs):
        r_refs, o_refs, (send, recv) = refs[:n], refs[n:2 * n], refs[2 * n:]
        x, y, c, _ = _place()
        cps = [_rcopy(r_ref, o_ref, send.at[a], recv.at[a], (x, y, 1 - c))
               for a, (r_ref, o_ref) in enumerate(zip(r_refs, o_refs))]
        for cp in cps:
            cp.start()
        for cp in cps:
            cp.wait()

    return pl.pallas_call(
        kern, name="share_halves", in_specs=[ANY] * n, out_specs=[ANY] * n,
        out_shape=[jax.ShapeDtypeStruct(r.shape, F32) for r in rs],
        scratch_shapes=[pltpu.SemaphoreType.DMA((n,)), pltpu.SemaphoreType.DMA((n,))],
    )(*rs)


SMALL_W = XBC


def _small_update(gs, ws, ms, vs):
    n = len(gs)
    widths = [g.shape[1] for g in gs]
    assert n <= SROWS and max(widths) <= SMALL_W

    def kern(*refs):
        g_refs, w_refs, m_refs, v_refs = (refs[n * i:n * (i + 1)] for i in range(4))
        o_g, o_d, o_m, o_v = (refs[n * (4 + i):n * (5 + i)] for i in range(4))
        buf, send, recv = refs[8 * n:]
        x, y, c, _ = _place()
        me = 4 * x + 2 * y + c
        buf[me] = jnp.zeros((SROWS, SMALL_W), F32)
        for k, g_ref in enumerate(g_refs):
            buf[me, k:k + 1, 0:widths[k]] = g_ref[...]
        cps = []
        for r in range(1, 8):
            peer = (1 - x if r & 4 else x, 1 - y if r & 2 else y, 1 - c if r & 1 else c)
            cps.append(_rcopy(buf.at[me], buf.at[me], send.at[r - 1], recv.at[r - 1], peer))
        for cp in cps:
            cp.start()
        for cp in cps:
            cp.wait()
        acc = buf[0]
        for i in range(1, 8):
            acc = acc + buf[i]
        for k in range(n):
            g_ = acc[k:k + 1, 0:widths[k]]
            m_ = ADAM_B1 * m_refs[k][...] + (1.0 - ADAM_B1) * g_
            v_ = ADAM_B2 * v_refs[k][...] + (1.0 - ADAM_B2) * jnp.square(g_)
            m_hat = m_ / (1.0 - ADAM_B1 ** ADAM_STEP)
            v_hat = v_ / (1.0 - ADAM_B2 ** ADAM_STEP)
            o_g[k][...] = g_
            o_d[k][...] = -ADAM_LR * (m_hat / (jnp.sqrt(v_hat) + ADAM_EPS) + ADAM_WD * w_refs[k][...])
            o_m[k][...] = m_
            o_v[k][...] = v_

    vm = pl.BlockSpec(memory_space=pltpu.VMEM)
    outs = pl.pallas_call(
        kern, name="small_update", in_specs=[vm] * (4 * n), out_specs=[vm] * (4 * n),
        out_shape=[jax.ShapeDtypeStruct((1, wd), F32) for _ in range(4) for wd in widths],
        scratch_shapes=[pltpu.VMEM((8, SROWS, SMALL_W), F32), pltpu.SemaphoreType.DMA((7,)), pltpu.SemaphoreType.DMA((7,))],
    )(*gs, *ws, *ms, *vs)
    return [outs[n * i:n * (i + 1)] for i in range(4)]


def _adamw(g, w, m, v, rb, name):
    rows, cols = g.shape

    def kern(g_ref, w_ref, m_ref, v_ref, d_ref, nm_ref, nv_ref):
        g_ = g_ref[...]
        m_ = ADAM_B1 * m_ref[...] + (1.0 - ADAM_B1) * g_
        v_ = ADAM_B2 * v_ref[...] + (1.0 - ADAM_B2) * jnp.square(g_)
        m_hat = m_ / (1.0 - ADAM_B1 ** ADAM_STEP)
        v_hat = v_ / (1.0 - ADAM_B2 ** ADAM_STEP)
        d_ref[...] = -ADAM_LR * (m_hat / (jnp.sqrt(v_hat) + ADAM_EPS) + ADAM_WD * w_ref[...])
        nm_ref[...] = m_
        nv_ref[...] = v_

    assert rows % rb == 0
    blk = pl.BlockSpec((rb, cols), lambda i: (i, 0))
    return pl.pallas_call(
        kern, name=name, grid=(rows // rb,), in_specs=[blk] * 4, out_specs=[blk] * 3,
        out_shape=[jax.ShapeDtypeStruct((rows, cols), F32)] * 3,
        compiler_params=_cp(("parallel",)),
    )(g, w, m, v)


def _adamw_halves(cidx, mine, other, w, m, v, rb, name):
    rows, cols = w.shape
    nbh = rows // 2 // rb

    def kern(c_ref, a_ref, b_ref, w_ref, m_ref, v_ref, g_ref, d_ref, nm_ref, nv_ref):
        g_ = jnp.where(pl.program_id(0) // nbh == c_ref[0], a_ref[...], b_ref[...])
        m_ = ADAM_B1 * m_ref[...] + (1.0 - ADAM_B1) * g_
        v_ = ADAM_B2 * v_ref[...] + (1.0 - ADAM_B2) * jnp.square(g_)
        m_hat = m_ / (1.0 - ADAM_B1 ** ADAM_STEP)
        v_hat = v_ / (1.0 - ADAM_B2 ** ADAM_STEP)
        g_ref[...] = g_
        d_ref[...] = -ADAM_LR * (m_hat / (jnp.sqrt(v_hat) + ADAM_EPS) + ADAM_WD * w_ref[...])
        nm_ref[...] = m_
        nv_ref[...] = v_

    half = pl.BlockSpec((rb, cols), lambda i, c: (i % nbh, 0))
    blk = pl.BlockSpec((rb, cols), lambda i, c: (i, 0))
    return pl.pallas_call(
        kern, name=name,
        grid_spec=pltpu.PrefetchScalarGridSpec(num_scalar_prefetch=1, grid=(2 * nbh,), in_specs=[half, half, blk, blk, blk],
                                               out_specs=[blk] * 4),
        out_shape=[jax.ShapeDtypeStruct((rows, cols), F32)] * 4,
        compiler_params=_cp(("parallel",)),
    )(cidx, mine, other, w, m, v)


WEIGHTS = ("c_ctx", "w_mod", "b_mod", "norm_w", "w_in", "ssm_conv_w", "ssm_conv_b", "dt_bias", "a_log", "d_skip",
           "ssm_norm_w", "w_out_ssm", "conf_conv_w", "conf_conv_b", "conf_ln_w", "conf_ln_b", "w_out_conf", "w_out",
           "final_norm_w")


def kernel(x, c, ctx, c_ctx, w_mod, b_mod, norm_w, w_in, ssm_conv_w, ssm_conv_b, dt_bias, a_log, d_skip, ssm_norm_w, w_out_ssm, conf_conv_w, conf_conv_b, conf_ln_w, conf_ln_b, w_out_conf, w_out, final_norm_w, loss_target, m_c_ctx, m_w_mod, m_b_mod, m_norm_w, m_w_in, m_ssm_conv_w, m_ssm_conv_b, m_dt_bias, m_a_log, m_d_skip, m_ssm_norm_w, m_w_out_ssm, m_conf_conv_w, m_conf_conv_b, m_conf_ln_w, m_conf_ln_b, m_w_out_conf, m_w_out, m_final_norm_w, v_c_ctx, v_w_mod, v_b_mod, v_norm_w, v_w_in, v_ssm_conv_w, v_ssm_conv_b, v_dt_bias, v_a_log, v_d_skip, v_ssm_norm_w, v_w_out_ssm, v_conf_conv_w, v_conf_conv_b, v_conf_ln_w, v_conf_ln_b, v_w_out_conf, v_w_out, v_final_norm_w):
    wv = (c_ctx, w_mod, b_mod, norm_w, w_in, ssm_conv_w, ssm_conv_b, dt_bias, a_log, d_skip, ssm_norm_w, w_out_ssm,
          conf_conv_w, conf_conv_b, conf_ln_w, conf_ln_b, w_out_conf, w_out, final_norm_w)
    mv = (m_c_ctx, m_w_mod, m_b_mod, m_norm_w, m_w_in, m_ssm_conv_w, m_ssm_conv_b, m_dt_bias, m_a_log, m_d_skip,
          m_ssm_norm_w, m_w_out_ssm, m_conf_conv_w, m_conf_conv_b, m_conf_ln_w, m_conf_ln_b, m_w_out_conf, m_w_out,
          m_final_norm_w)
    vv = (v_c_ctx, v_w_mod, v_b_mod, v_norm_w, v_w_in, v_ssm_conv_w, v_ssm_conv_b, v_dt_bias, v_a_log, v_d_skip,
          v_ssm_norm_w, v_w_out_ssm, v_conf_conv_w, v_conf_conv_b, v_conf_ln_w, v_conf_ln_b, v_w_out_conf, v_w_out,
          v_final_norm_w)
    shapes = {n: a.shape for n, a in zip(WEIGHTS, wv)}

    def squeeze(d):
        return {n: (a if n in ("c_ctx", "final_norm_w") else a[0]) for n, a in d.items()}

    w, m, v = (squeeze(dict(zip(WEIGHTS, t))) for t in (wv, mv, vv))

    my_chip = 2 * lax.axis_index("x") + lax.axis_index("y")
    my_core = lax.axis_index("c")

    pw = _pack_shard(w)
    pwb, wib, psm = pw.astype(BF16), w["w_in"].astype(BF16), pw[O_SC:O_SC + SROWS]
    gm, gw, gs = _gather_weights([pwb, wib], psm)
    mine = (jnp.arange(NSHARD) == my_chip)[:, None, None]
    gm, gw, gs = jnp.where(mine, pwb[None], gm), jnp.where(mine, wib[None], gw), jnp.where(mine, psm[None], gs)
    full = dict(w)
    full.update(_unpack_gathered(gm, gw, gs))

    lsum, grad_x, g = _local_step(x[0], c, ctx[0], loss_target[0], full)
    loss = lax.psum(lsum, ("x", "y", "c"))

    cidx = my_core.astype(jnp.int32).reshape(1)
    mc = jnp.stack([my_chip, my_core]).astype(jnp.int32)
    gsrc = [_pack_full(g), g["w_in_shards"]]
    blocks = (RB, WB)
    sib = _swap_halves(gsrc)
    part = [_add_halves(cidx, a, b, rb, "add_halves_%d" % i) for i, (a, b, rb) in enumerate(zip(gsrc, sib, blocks))]
    far = _exchange_chips(part)
    red = [_add_chips(mc, a, b, f, rb, "add_chips_%d" % i) for i, (a, b, f, rb) in enumerate(zip(gsrc, sib, far, blocks))]
    got = _share_halves(red)
    g_pk = jnp.concatenate([jnp.where(my_core == 0, red[0], got[0]), jnp.where(my_core == 0, got[0], red[0])], axis=0)
    small = [name for name, _ in SMALL]
    as_row = lambda d: [d[name].reshape(1, -1) for name in small]
    res_sm = _small_update(as_row(g), as_row(w), as_row(m), as_row(v))

    gr = {"w_mod": g_pk[O_MOD:O_OS].reshape(D, R_MOD), "rest": g_pk[O_OS:PUSED]}
    wr, mr, vr = ({"w_mod": t["w_mod"], "rest": _pack_rest(t)} for t in (w, m, v))
    res = {k: _adamw(gr[k], wr[k], mr[k], vr[k], rb, "adamw_" + k)
           for k, rb in (("w_mod", 512), ("rest", (PUSED - O_OS) // 2))}
    gr["w_in"], *res["w_in"] = _adamw_halves(cidx, red[1], got[1], w["w_in"], m["w_in"], v["w_in"], WB, "adamw_w_in")

    outs = []
    for i in range(4):
        pick = (lambda k: gr[k]) if i == 0 else (lambda k: res[k][i - 1])
        d = {"w_mod": pick("w_mod")[None], "w_in": pick("w_in")[None]}
        d.update(_unpack_rest(pick("rest")))
        d.update({name: a.reshape(shapes[name]) for name, a in zip(small, res_sm[i])})
        outs.extend(d[n] for n in WEIGHTS)
    return (loss, grad_x[None], *outs)
```

```python
import functools

import jax
import jax.numpy as jnp
from jax import lax
from jax.experimental import pallas as pl
from jax.experimental.pallas import tpu as pltpu

F32, BF16 = jnp.float32, jnp.bfloat16

D = 1024
DI = 2048
NH = 32
HP = 64
NG = 8
HPG = 4
NS = 128
Q = 128
GW = 64
CK = 31
SK = 4
CTX = 256
EPS = 1e-6
XBC = DI + 2 * NG * NS
SSDW = XBC + 128
RESTW = 7168
T = 256
TX = 512
VMEM_LIMIT = 56 * 1024 * 1024

ADAM_LR, ADAM_B1, ADAM_B2, ADAM_EPS, ADAM_WD, ADAM_STEP = 0.001, 0.9, 0.999, 1e-08, 0.01, 10


def _cp(sem):
    return pltpu.CompilerParams(dimension_semantics=sem, vmem_limit_bytes=VMEM_LIMIT)


def _sig(x):
    return jax.nn.sigmoid(x)


def _silu(x):
    return x * _sig(x)


def _dsilu(x):
    s = _sig(x)
    return s * (1.0 + x * (1.0 - s))


def _dot(a, b):
    return jnp.dot(a, b, preferred_element_type=F32)


def _dot_nt(a, b):
    return lax.dot_general(a, b, (((1,), (1,)), ((), ())), preferred_element_type=F32)


def _split3(x):
    h = x.astype(BF16)
    r = x - h.astype(F32)
    m = r.astype(BF16)
    l = (r - m.astype(F32)).astype(BF16)
    return h, m, l


def _dot3_l(sel, x):
    h, m, l = _split3(x)
    return _dot(sel, h) + _dot(sel, m) + _dot(sel, l)


def _dot3_r(x, sel):
    h, m, l = _split3(x)
    return _dot(h, sel) + _dot(m, sel) + _dot(l, sel)


def _split2(x):
    h = x.astype(BF16)
    return h, (x - h.astype(F32)).astype(BF16)


def _dot2_l(sel, x):
    h, l = _split2(x)
    return _dot(sel, h) + _dot(sel, l)


def _dot2_r(x, sel):
    h, l = _split2(x)
    return _dot(h, sel) + _dot(l, sel)


def _iota(shape, dim):
    return lax.broadcasted_iota(jnp.int32, shape, dim)


def _mm(a, b, dims, m, n, k, bm, bn, bk, out_dtype, name):
    nk = k // bk
    assert m % bm == 0 and n % bn == 0 and k % bk == 0, (name, m, n, k, bm, bn, bk)

    def prod(a_ref, b_ref):
        av = a_ref[...].astype(BF16)
        bv = b_ref[...].astype(BF16)
        if dims == "nn":
            return _dot(av, bv)
        if dims == "nt":
            return _dot_nt(av, bv)
        return lax.dot_general(av, bv, (((0,), (0,)), ((), ())), preferred_element_type=F32)

    def kern_one(a_ref, b_ref, o_ref):
        o_ref[...] = prod(a_ref, b_ref).astype(out_dtype)

    def kern_acc(a_ref, b_ref, o_ref, acc):
        kk = pl.program_id(2)

        @pl.when(kk == 0)
        def _():
            acc[...] = jnp.zeros_like(acc)

        acc[...] += prod(a_ref, b_ref)

        @pl.when(kk == nk - 1)
        def _():
            o_ref[...] = acc[...].astype(out_dtype)

    if dims == "nn":
        a_spec = pl.BlockSpec((bm, bk), lambda j, i, kk: (i, kk))
        b_spec = pl.BlockSpec((bk, bn), lambda j, i, kk: (kk, j))
    elif dims == "nt":
        a_spec = pl.BlockSpec((bm, bk), lambda j, i, kk: (i, kk))
        b_spec = pl.BlockSpec((bn, bk), lambda j, i, kk: (j, kk))
    else:
        a_spec = pl.BlockSpec((bk, bm), lambda j, i, kk: (kk, i))
        b_spec = pl.BlockSpec((bk, bn), lambda j, i, kk: (kk, j))
    return pl.pallas_call(
        kern_one if nk == 1 else kern_acc, name=name,
        grid=(n // bn, m // bm, nk),
        in_specs=[a_spec, b_spec],
        out_specs=pl.BlockSpec((bm, bn), lambda j, i, kk: (i, j)),
        out_shape=jax.ShapeDtypeStruct((m, n), out_dtype),
        scratch_shapes=[] if nk == 1 else [pltpu.VMEM((bm, bn), F32)],
        compiler_params=_cp(("parallel", "parallel", "arbitrary")),
    )(a, b)


def _mod_fwd(cc, w_mod, b_mod):
    def kern(cc_ref, w_ref, b_ref, o_ref):
        s = _silu(cc_ref[...]).astype(BF16)
        o_ref[...] = _dot(s, w_ref[...]) + b_ref[...]

    return pl.pallas_call(
        kern, name="mod_fwd", grid=(3,),
        in_specs=[pl.BlockSpec((8, D), lambda j: (0, 0)), pl.BlockSpec((D, D), lambda j: (0, j)),
                  pl.BlockSpec((1, D), lambda j: (0, j))],
        out_specs=pl.BlockSpec((8, D), lambda j: (0, j)),
        out_shape=jax.ShapeDtypeStruct((8, 3 * D), F32),
        compiler_params=_cp(("parallel",)),
    )(cc, w_mod, b_mod)


def _mod_bwd(dmod, cc, cct, w_mod):
    def kern(dm_ref, cc_ref, cct_ref, w_ref, gw_ref, gb_ref, gc_ref):
        kk = pl.program_id(0)
        dm = dm_ref[...]
        sct = _silu(cct_ref[...])
        gw_ref[...] = sct[:, 0:1] * dm[0:1, :] + sct[:, 1:2] * dm[1:2, :]
        gb_ref[...] = jnp.broadcast_to(dm[0:1, :] + dm[1:2, :], dm.shape)

        @pl.when(kk == 0)
        def _():
            gc_ref[...] = jnp.zeros_like(gc_ref)

        gc_ref[...] += _dot_nt(dm.astype(BF16), w_ref[...])

        @pl.when(kk == 2)
        def _():
            gc_ref[...] = gc_ref[...] * _dsilu(cc_ref[...])

    return pl.pallas_call(
        kern, name="mod_bwd", grid=(3,),
        in_specs=[pl.BlockSpec((8, D), lambda j: (0, j)), pl.BlockSpec((8, D), lambda j: (0, 0)),
                  pl.BlockSpec((D, 8), lambda j: (0, 0)), pl.BlockSpec((D, D), lambda j: (0, j))],
        out_specs=[pl.BlockSpec((D, D), lambda j: (0, j)), pl.BlockSpec((8, D), lambda j: (0, j)),
                   pl.BlockSpec((8, D), lambda j: (0, 0))],
        out_shape=[jax.ShapeDtypeStruct((D, 3 * D), F32), jax.ShapeDtypeStruct((8, 3 * D), F32),
                   jax.ShapeDtypeStruct((8, D), F32)],
        compiler_params=_cp(("arbitrary",)),
    )(dmod, cc, cct, w_mod)


def _ext_specs(nct):
    return (pl.BlockSpec((T, D), lambda i: (jnp.minimum(i, nct - 1), 0)),
            pl.BlockSpec((T, D), lambda i: (jnp.maximum(i - nct, 0), 0)))


def _norm_fwd(ctx, xl, mod, nw, nct):
    lext = ctx.shape[0] + xl.shape[0]

    def kern(c_ref, x_ref, mod_ref, nw_ref, h_ref):
        is_ctx = pl.program_id(0) < nct
        x = jnp.where(is_ctx, c_ref[...], x_ref[...])
        r = lax.rsqrt(jnp.mean(x * x, axis=-1, keepdims=True) + EPS)
        xn = x * r * nw_ref[...]
        shift = jnp.where(is_ctx, mod_ref[1:2, 0:D], mod_ref[0:1, 0:D])
        scale = jnp.where(is_ctx, mod_ref[1:2, D:2 * D], mod_ref[0:1, D:2 * D])
        h_ref[...] = (xn * (1.0 + scale) + shift).astype(BF16)

    return pl.pallas_call(
        kern, name="norm_fwd", grid=(lext // T,),
        in_specs=[*_ext_specs(nct), pl.BlockSpec((8, 3 * D), lambda i: (0, 0)),
                  pl.BlockSpec((1, D), lambda i: (0, 0))],
        out_specs=pl.BlockSpec((T, D), lambda i: (i, 0)),
        out_shape=jax.ShapeDtypeStruct((lext, D), BF16),
        compiler_params=_cp(("parallel",)),
    )(ctx, xl, mod, nw)


def _norm_bwd(dha, dhb, ctx, xl, dx2, mod, nw, nct):
    lext = ctx.shape[0] + xl.shape[0]
    ntl = lext // T

    def kern(dha_ref, dhb_ref, c_ref, x_ref, dx2_ref, mod_ref, nw_ref, gx_ref, gnw_ref, dss_ref):
        i = pl.program_id(0)
        is_ctx = i < nct

        @pl.when(i == 0)
        def _():
            gnw_ref[...] = jnp.zeros_like(gnw_ref)
            dss_ref[...] = jnp.zeros_like(dss_ref)

        x = jnp.where(is_ctx, c_ref[...], x_ref[...])
        dh_ = dha_ref[...].astype(F32) + jnp.where(is_ctx, 0.0, dhb_ref[...].astype(F32))
        nw_ = nw_ref[...]
        r = lax.rsqrt(jnp.mean(x * x, axis=-1, keepdims=True) + EPS)
        xn = x * r
        scale = jnp.where(is_ctx, mod_ref[1:2, D:2 * D], mod_ref[0:1, D:2 * D])
        dsh = jnp.sum(dh_, axis=0, keepdims=True)
        dsc = jnp.sum(dh_ * (xn * nw_), axis=0, keepdims=True)
        row = jnp.concatenate([dsh, dsc], axis=1)
        rid = _iota((8, 2 * D), 0)
        dss_ref[...] += jnp.where(rid == jnp.where(is_ctx, 1, 0), row, 0.0)
        dxnw = dh_ * (1.0 + scale)
        gnw_ref[...] += jnp.broadcast_to(jnp.sum(dxnw * xn, axis=0, keepdims=True), (8, D))
        dxn = dxnw * nw_
        dx = r * (dxn - xn * jnp.mean(dxn * xn, axis=-1, keepdims=True))
        gx_ref[...] = dx2_ref[...] + dx

    return pl.pallas_call(
        kern, name="norm_bwd", grid=(ntl,),
        in_specs=[pl.BlockSpec((T, D), lambda i: (i, 0)), pl.BlockSpec((T, D), lambda i: (jnp.maximum(i - nct, 0), 0)),
                  *_ext_specs(nct),
                  pl.BlockSpec((T, D), lambda i: (jnp.maximum(i - nct, 0), 0)),
                  pl.BlockSpec((8, 3 * D), lambda i: (0, 0)), pl.BlockSpec((1, D), lambda i: (0, 0))],
        out_specs=[pl.BlockSpec((T, D), lambda i: (jnp.maximum(i - nct, 0), 0)),
                   pl.BlockSpec((8, D), lambda i: (0, 0)), pl.BlockSpec((8, 2 * D), lambda i: (0, 0))],
        out_shape=[jax.ShapeDtypeStruct((lext - nct * T, D), F32), jax.ShapeDtypeStruct((8, D), F32),
                   jax.ShapeDtypeStruct((8, 2 * D), F32)],
        compiler_params=_cp(("arbitrary",)),
    )(dha, dhb, ctx, xl, dx2, mod, nw)


CB = 1024


def _halo_specs(width_blk, col_off_blocks, ntl):
    t8 = T // 8
    main = pl.BlockSpec((T, width_blk), lambda j, i: (i, j + col_off_blocks))
    prev = pl.BlockSpec((8, width_blk), lambda j, i: (jnp.maximum(i * t8 - 1, 0), j + col_off_blocks))
    nxt = pl.BlockSpec((8, width_blk), lambda j, i: (jnp.minimum((i + 1) * t8, ntl * t8 - 1), j + col_off_blocks))
    return main, prev, nxt


def _seq_edges(i, nct, ntl):
    starts = jnp.logical_or(i == 0, i == nct)
    ends = jnp.logical_or(i == nct - 1, i == ntl - 1)
    return starts, ends


def _shifted(ext, off):
    n = ext.shape[0]
    return pltpu.roll(ext, (-off) % n, axis=0)[8:8 + T]


def _conv_fwd(proj_ssd, cw, cb, nct):
    lext = proj_ssd.shape[0]
    ntl = lext // T

    def kern(u_ref, up_ref, un_ref, w_ref, b_ref, o_ref):
        i = pl.program_id(1)
        starts, ends = _seq_edges(i, nct, ntl)
        up = jnp.where(starts, 0.0, up_ref[...])
        un = jnp.where(ends, 0.0, un_ref[...])
        ext = jnp.concatenate([up, u_ref[...], un], axis=0)
        w = w_ref[...]
        pre = b_ref[...] + w[0:1] * _shifted(ext, -2) + w[1:2] * _shifted(ext, -1) \
            + w[2:3] * u_ref[...] + w[3:4] * _shifted(ext, 1)
        o_ref[...] = _silu(pre)

    cbf = 4 * CB
    main, prev, nxt = _halo_specs(cbf, 0, ntl)
    return pl.pallas_call(
        kern, name="conv_fwd", grid=(XBC // cbf, ntl),
        in_specs=[main, prev, nxt, pl.BlockSpec((8, cbf), lambda j, i: (0, j)), pl.BlockSpec((1, cbf), lambda j, i: (0, j))],
        out_specs=pl.BlockSpec((T, cbf), lambda j, i: (i, j)),
        out_shape=jax.ShapeDtypeStruct((lext, XBC), F32),
        compiler_params=_cp(("parallel", "parallel")),
    )(proj_ssd, proj_ssd, proj_ssd, cw, cb)


def _conv_bwd(dpost, proj_ssd, cw, cb, col_off, width, nct, name):
    lext = proj_ssd.shape[0]
    ntl = lext // T
    bw = min(width, 2 * CB)
    assert col_off % bw == 0 and width % bw == 0
    cob = col_off // bw

    def kern(u_ref, up_ref, un_ref, d_ref, dp_ref, dn_ref, w_ref, b_ref, du_ref, gw_ref, gb_ref):
        i = pl.program_id(1)

        @pl.when(i == 0)
        def _():
            gw_ref[...] = jnp.zeros_like(gw_ref)
            gb_ref[...] = jnp.zeros_like(gb_ref)

        starts, ends = _seq_edges(i, nct, ntl)
        ext = jnp.concatenate([jnp.where(starts, 0.0, up_ref[...]), u_ref[...], jnp.where(ends, 0.0, un_ref[...])], axis=0)
        dext = jnp.concatenate([jnp.where(starts, 0.0, dp_ref[...]), d_ref[...], jnp.where(ends, 0.0, dn_ref[...])], axis=0)
        w = w_ref[...]
        n = ext.shape[0]
        pre = b_ref[...] + w[0:1] * pltpu.roll(ext, 2, axis=0) + w[1:2] * pltpu.roll(ext, 1, axis=0) \
            + w[2:3] * ext + w[3:4] * pltpu.roll(ext, n - 1, axis=0)
        dpre = dext * _dsilu(pre)
        dm = dpre[8:8 + T]
        du = w[0:1] * _shifted(dpre, 2) + w[1:2] * _shifted(dpre, 1) + w[2:3] * dm + w[3:4] * _shifted(dpre, -1)
        du_ref[...] = du.astype(BF16)
        g0 = jnp.sum(dm * _shifted(ext, -2), axis=0, keepdims=True)
        g1 = jnp.sum(dm * _shifted(ext, -1), axis=0, keepdims=True)
        g2 = jnp.sum(dm * u_ref[...], axis=0, keepdims=True)
        g3 = jnp.sum(dm * _shifted(ext, 1), axis=0, keepdims=True)
        rid = _iota((8, bw), 0)
        gw_ref[...] += jnp.where(rid == 0, g0, jnp.where(rid == 1, g1, jnp.where(rid == 2, g2, jnp.where(rid == 3, g3, 0.0))))
        gb_ref[...] += jnp.broadcast_to(jnp.sum(dm, axis=0, keepdims=True), (8, bw))

    main, prev, nxt = _halo_specs(bw, cob, ntl)
    dmain, dprev, dnxt = _halo_specs(bw, 0, ntl)
    return pl.pallas_call(
        kern, name=name, grid=(width // bw, ntl),
        in_specs=[main, prev, nxt, dmain, dprev, dnxt,
                  pl.BlockSpec((8, bw), lambda j, i: (0, j + cob)), pl.BlockSpec((1, bw), lambda j, i: (0, j + cob))],
        out_specs=[pl.BlockSpec((T, bw), lambda j, i: (i, j)), pl.BlockSpec((8, bw), lambda j, i: (0, j)),
                   pl.BlockSpec((8, bw), lambda j, i: (0, j))],
        out_shape=[jax.ShapeDtypeStruct((lext, width), BF16), jax.ShapeDtypeStruct((8, width), F32),
                   jax.ShapeDtypeStruct((8, width), F32)],
        compiler_params=_cp(("parallel", "arbitrary")),
    )(proj_ssd, proj_ssd, proj_ssd, dpost, dpost, dpost, cw, cb)


def _tri(lower):
    r, c = _iota((Q, Q), 0), _iota((Q, Q), 1)
    return jnp.where((c <= r) if lower else (c >= r), 1.0, 0.0).astype(BF16)


def _is_bdir_lane(shape):
    ln = _iota(shape, len(shape) - 1)
    return jnp.logical_and(((ln >> 2) & 1) == 1, ln < 64)


def _dt_fwd(proj_ssd, dtb, av):
    lext = proj_ssd.shape[0]

    def kern(p_ref, b_ref, a_ref, dtg_ref, lag_ref, dtt_ref, lat_ref):
        lane = _iota((T, 128), 1)
        raw = p_ref[...] + b_ref[...]
        dt = jnp.where(lane < 64, jnp.maximum(raw, 0.0) + jnp.log1p(jnp.exp(-jnp.abs(raw))), 0.0)
        dta = dt * a_ref[...]
        tl, tu = _tri(True), _tri(False)
        isb = _is_bdir_lane((Q, 128))
        las = []
        for qq in range(T // Q):
            blk = dta[qq * Q:(qq + 1) * Q]
            las.append(jnp.where(isb, _dot3_l(tu, blk), _dot3_l(tl, blk)))
        la = jnp.concatenate(las, axis=0)
        for g in range(NG):
            sh = (128 - 8 * g) % 128
            dtg_ref[g] = jnp.where(lane < 8, pltpu.roll(dt, sh, axis=1) if sh else dt, 0.0)
            lag_ref[g] = jnp.where(lane < 8, pltpu.roll(la, sh, axis=1) if sh else la, 0.0)
        dtt_ref[...] = dt.T[0:64]
        lat_ref[...] = la.T[0:64]

    return pl.pallas_call(
        kern, name="dt_fwd", grid=(lext // T,),
        in_specs=[pl.BlockSpec((T, 128), lambda i: (i, XBC // 128)), pl.BlockSpec((1, 128), lambda i: (0, 0)),
                  pl.BlockSpec((1, 128), lambda i: (0, 0))],
        out_specs=[pl.BlockSpec((NG, T, 128), lambda i: (0, i, 0)), pl.BlockSpec((NG, T, 128), lambda i: (0, i, 0)),
                   pl.BlockSpec((64, T), lambda i: (0, i)), pl.BlockSpec((64, T), lambda i: (0, i))],
        out_shape=[jax.ShapeDtypeStruct((NG, lext, 128), F32), jax.ShapeDtypeStruct((NG, lext, 128), F32),
                   jax.ShapeDtypeStruct((64, lext), F32), jax.ShapeDtypeStruct((64, lext), F32)],
        compiler_params=_cp(("parallel",)),
    )(proj_ssd, dtb, av)


def _dt_bwd(ddtg, proj_ssd, dtb):
    lext = proj_ssd.shape[0]

    def kern(d_ref, p_ref, b_ref, o_ref, gb_ref):
        @pl.when(pl.program_id(0) == 0)
        def _():
            gb_ref[...] = jnp.zeros_like(gb_ref)

        acc = d_ref[0]
        for g in range(1, NG):
            acc = acc + pltpu.roll(d_ref[g], 8 * g, axis=1)
        draw = acc * _sig(p_ref[...] + b_ref[...])
        o_ref[...] = draw.astype(BF16)
        gb_ref[...] += jnp.broadcast_to(jnp.sum(draw, axis=0, keepdims=True), (8, 128))

    return pl.pallas_call(
        kern, name="dt_bwd", grid=(lext // T,),
        in_specs=[pl.BlockSpec((NG, T, 128), lambda i: (0, i, 0)), pl.BlockSpec((T, 128), lambda i: (i, XBC // 128)),
                  pl.BlockSpec((1, 128), lambda i: (0, 0))],
        out_specs=[pl.BlockSpec((T, 128), lambda i: (i, 0)), pl.BlockSpec((8, 128), lambda i: (0, 0))],
        out_shape=[jax.ShapeDtypeStruct((lext, 128), BF16), jax.ShapeDtypeStruct((8, 128), F32)],
        compiler_params=_cp(("arbitrary",)),
    )(ddtg, proj_ssd, dtb)


def _expand_sel(d):
    r, c = _iota((128, 256), 0), _iota((128, 256), 1)
    return jnp.where(r == 4 * d + (c >> 6), 1.0, 0.0).astype(BF16)


def _reduce_sel(d):
    r, c = _iota((256, 128), 0), _iota((256, 128), 1)
    return jnp.where(c == 4 * d + (r >> 6), 1.0, 0.0).astype(BF16)


def _chunk_of_bwd_dir(j, ncc, nc):
    return jnp.where(j < ncc, ncc - 1 - j, nc + ncc - 1 - j)


def _dir_terms(la, dt, d):
    lane = _iota(la.shape, 1)
    mine = jnp.logical_and(lane >= 4 * d, lane < 4 * d + 4)
    la = jnp.where(mine, la, 0.0)
    tot = la[Q - 1:Q] if d == 0 else la[0:1]
    wnd = jnp.exp(tot - la)
    return tot, wnd * jnp.where(mine, dt, 0.0), wnd


def _ssd_state(xbc, dtg, lag, ncc):
    lext = xbc.shape[0]
    nc = lext // Q

    def kern(xf_ref, bf_ref, dtf_ref, laf_ref, xb_ref, bb_ref, dtb_ref, lab_ref, hf_ref, hb_ref, sf, sb):
        @pl.when(pl.program_id(0) == 0)
        def _():
            sf[...] = jnp.zeros_like(sf)
            sb[...] = jnp.zeros_like(sb)

        for d, (x_ref, b_ref, dt_ref, la_ref, h_ref, s) in enumerate(
                ((xf_ref, bf_ref, dtf_ref, laf_ref, hf_ref, sf), (xb_ref, bb_ref, dtb_ref, lab_ref, hb_ref, sb))):
            h_ref[...] = s[...].astype(BF16)
            ex = _expand_sel(d)
            for g in range(NG):
                cols = slice(256 * g, 256 * (g + 1))
                tot, w_end, _ = _dir_terms(la_ref[g], dt_ref[g], d)
                wexp = _dot2_r(w_end, ex)
                dexp = _dot2_r(jnp.broadcast_to(jnp.exp(tot), (8, 128)), ex)[0:1]
                xw = (x_ref[:, cols] * wexp).astype(BF16)
                s[:, cols] = s[:, cols] * dexp + _dot(b_ref[:, 128 * g:128 * (g + 1)].T.astype(BF16), xw)

    cb = functools.partial(_chunk_of_bwd_dir, ncc=ncc, nc=nc)
    sm = lambda f: pl.BlockSpec((NG, Q, 128), lambda j: (0, f(j), 0))
    one = lambda j: j
    return pl.pallas_call(
        kern, name="ssd_state", grid=(nc,),
        in_specs=[pl.BlockSpec((Q, DI), lambda j: (j, 0)), pl.BlockSpec((Q, NG * NS), lambda j: (j, 2)), sm(one), sm(one),
                  pl.BlockSpec((Q, DI), lambda j: (cb(j), 0)), pl.BlockSpec((Q, NG * NS), lambda j: (cb(j), 2)), sm(cb), sm(cb)],
        out_specs=[pl.BlockSpec((None, 128, DI), lambda j: (j, 0, 0)),
                   pl.BlockSpec((None, 128, DI), lambda j: (cb(j), 0, 0))],
        out_shape=[jax.ShapeDtypeStruct((nc, 128, DI), BF16), jax.ShapeDtypeStruct((nc, 128, DI), BF16)],
        scratch_shapes=[pltpu.VMEM((128, DI), F32), pltpu.VMEM((128, DI), F32)],
        compiler_params=_cp(("arbitrary",)),
    )(xbc, xbc, dtg, lag, xbc, xbc, dtg, lag)


def _ssd_out(xbc, dtg, lag, dtt, lat, htf, htb, ncc):
    lext = xbc.shape[0]
    nc = lext // Q
    ncx = nc - ncc

    gps = 8
    li, si = (lambda: _iota((Q, Q), 0)), (lambda: _iota((Q, Q), 1))

    def kern(x_ref, b_ref, c_ref, dtg_ref, lag_ref, dtt_ref, lat_ref, hf_ref, hb_ref, y_ref):
        lane = _iota((Q, 128), 1)
        masks = (li() >= si(), li() <= si())
        for gg in range(gps):
            cols = slice(256 * gg, 256 * (gg + 1))
            cm = c_ref[:, 128 * gg:128 * (gg + 1)]
            xb_ = x_ref[:, cols].astype(BF16)
            s_ = _dot_nt(cm.astype(BF16), b_ref[:, 128 * gg:128 * (gg + 1)].astype(BF16))
            la, dtt_, lat_ = lag_ref[gg], dtt_ref[8 * gg:8 * (gg + 1)], lat_ref[8 * gg:8 * (gg + 1)]
            elam = jnp.exp(la)
            yh = [jnp.zeros((Q, 128), F32), jnp.zeros((Q, 128), F32)]
            for d, h_ref in enumerate((hf_ref, hb_ref)):
                rhs = jnp.concatenate([xb_, h_ref[:, cols].astype(BF16)], axis=0)
                lhs = []
                for r in range(HPG):
                    j = 4 * d + r
                    lm = jnp.where(masks[d], jnp.exp(la[:, j:j + 1] - lat_[j:j + 1, :]), 0.0)
                    w = s_ * lm * dtt_[j:j + 1, :]
                    lhs.append(jnp.concatenate([w, cm * elam[:, j:j + 1]], axis=1).astype(BF16))
                for b in range(HPG // 2):
                    ypair = _dot(jnp.concatenate(lhs[2 * b:2 * b + 2], axis=0), rhs[:, 128 * b:128 * (b + 1)])
                    yh[b] = yh[b] + jnp.where(lane < 64, ypair[0:Q], ypair[Q:2 * Q])
            y_ref[:, cols] = jnp.concatenate(yh, axis=1).astype(BF16)

    nb = NG // gps
    sm = pl.BlockSpec((gps, Q, 128), lambda c, g: (g, c + ncc, 0))
    smt = pl.BlockSpec((8 * gps, Q), lambda c, g: (g, c + ncc))
    st3 = pl.BlockSpec((None, 128, 256 * gps), lambda c, g: (c + ncc, 0, g))
    return pl.pallas_call(
        kern, name="ssd_out", grid=(ncx, nb),
        in_specs=[pl.BlockSpec((Q, 256 * gps), lambda c, g: (c + ncc, g)),
                  pl.BlockSpec((Q, 128 * gps), lambda c, g: (c + ncc, 2 * nb + g)),
                  pl.BlockSpec((Q, 128 * gps), lambda c, g: (c + ncc, 3 * nb + g)), sm, sm, smt, smt, st3, st3],
        out_specs=pl.BlockSpec((Q, 256 * gps), lambda c, g: (c, g)),
        out_shape=jax.ShapeDtypeStruct((ncx * Q, DI), BF16),
        compiler_params=_cp(("parallel", "parallel")),
    )(xbc, xbc, xbc, dtg, lag, dtt, lat, htf, htb)


def _ssd_bwd_state(xbc, dy, lag, ncc):
    lext = xbc.shape[0]
    nc = lext // Q

    def kern(cf_ref, dyf_ref, laf_ref, cb_ref, dyb_ref, lab_ref, df_ref, db_ref, sf, sb):
        @pl.when(pl.program_id(0) == 0)
        def _():
            sf[...] = jnp.zeros_like(sf)
            sb[...] = jnp.zeros_like(sb)

        for d, (c_ref, dy_ref, la_ref, o_ref, s) in enumerate(
                ((cf_ref, dyf_ref, laf_ref, df_ref, sf), (cb_ref, dyb_ref, lab_ref, db_ref, sb))):
            o_ref[...] = s[...].astype(BF16)
            ex = _expand_sel(d)
            for g in range(NG):
                cols = slice(256 * g, 256 * (g + 1))
                la = la_ref[g]
                tot = la[Q - 1:Q] if d == 0 else la[0:1]
                eexp = _dot2_r(jnp.exp(la), ex)
                dexp = _dot2_r(jnp.broadcast_to(jnp.exp(tot), (8, 128)), ex)[0:1]
                dye = (dy_ref[:, cols] * eexp).astype(BF16)
                s[:, cols] = s[:, cols] * dexp + _dot(c_ref[:, 128 * g:128 * (g + 1)].T.astype(BF16), dye)

    cf = lambda j: nc - 1 - j
    cb = lambda j: _chunk_of_bwd_dir(nc - 1 - j, ncc, nc)
    sm = lambda f: pl.BlockSpec((NG, Q, 128), lambda j: (0, f(j), 0))
    return pl.pallas_call(
        kern, name="ssd_bwd_state", grid=(nc,),
        in_specs=[pl.BlockSpec((Q, NG * NS), lambda j: (cf(j), 3)), pl.BlockSpec((Q, DI), lambda j: (cf(j), 0)), sm(cf),
                  pl.BlockSpec((Q, NG * NS), lambda j: (cb(j), 3)), pl.BlockSpec((Q, DI), lambda j: (cb(j), 0)), sm(cb)],
        out_specs=[pl.BlockSpec((None, 128, DI), lambda j: (cf(j), 0, 0)),
                   pl.BlockSpec((None, 128, DI), lambda j: (cb(j), 0, 0))],
        out_shape=[jax.ShapeDtypeStruct((nc, 128, DI), BF16), jax.ShapeDtypeStruct((nc, 128, DI), BF16)],
        scratch_shapes=[pltpu.VMEM((128, DI), F32), pltpu.VMEM((128, DI), F32)],
        compiler_params=_cp(("arbitrary",)),
    )(xbc, dy, lag, xbc, dy, lag)


def _ssd_bwd_out(xbc, dy, dsk, dtg, lag, dtt, lat, htf, htb, dhf, dhb, a_rows):
    lext = xbc.shape[0]
    nc = lext // Q

    gps = 1

    def kern(x_ref, b_ref, c_ref, dy_ref, sk_ref, dtg_ref, lag_ref, dtt_ref, lat_ref, hf_ref, hb_ref, df_ref, db_ref,
             a_ref, dx_ref, dbo_ref, dco_ref, ddt_ref, ga_ref):
        @pl.when(pl.program_id(1) == 0)
        def _():
            ga_ref[...] = jnp.zeros_like(ga_ref)

        for gg in range(gps):
            one_group(gg, x_ref, b_ref, c_ref, dy_ref, sk_ref, dtg_ref, lag_ref, dtt_ref, lat_ref, hf_ref, hb_ref, df_ref,
                      db_ref, a_ref, dx_ref, dbo_ref, dco_ref, ddt_ref, ga_ref)

    def one_group(gg, x_ref, b_ref, c_ref, dy_ref, sk_ref, dtg_ref, lag_ref, dtt_ref, lat_ref, hf_ref, hb_ref, df_ref,
                  db_ref, a_ref, dx_ref, dbo_ref, dco_ref, ddt_ref, ga_ref):
        g = pl.program_id(0) * gps + gg
        cols, cols128 = slice(256 * gg, 256 * (gg + 1)), slice(128 * gg, 128 * (gg + 1))
        x, bm, cm, dy_ = x_ref[:, cols], b_ref[:, cols128], c_ref[:, cols128], dy_ref[:, cols]
        xb_, bb_, cb_, dyb_ = x.astype(BF16), bm.astype(BF16), cm.astype(BF16), dy_.astype(BF16)
        st = _dot_nt(bb_, cb_)
        si, li = _iota((Q, Q), 0), _iota((Q, Q), 1)
        lane = _iota((Q, 256), 1)
        lane128 = _iota((Q, 128), 1)
        row128 = _iota((Q, 128), 0)
        sub = _iota((128, Q), 0)
        la, dt = lag_ref[gg], dtg_ref[gg]
        dtt_, lat_ = dtt_ref[8 * gg:8 * (gg + 1)], lat_ref[8 * gg:8 * (gg + 1)]
        elam = jnp.exp(la)
        dst = jnp.zeros((Q, Q), F32)
        dxh = [jnp.zeros((Q, 128), F32), jnp.zeros((Q, 128), F32)]
        cdir, clam = jnp.zeros((Q, 128), F32), jnp.zeros((Q, 128), F32)
        dba = jnp.zeros((Q, 128), F32)
        dca = jnp.zeros((Q, 128), F32)
        dlam = jnp.zeros((Q, 128), F32)
        ddir = jnp.zeros((Q, 128), F32)
        rows = jnp.zeros((128, Q), F32)
        per_dir = []
        for d, (h_ref, dh_ref) in enumerate(((hf_ref, df_ref), (hb_ref, db_ref))):
            ht, dht = h_ref[:, cols].astype(F32), dh_ref[:, cols].astype(F32)
            htb_, dhtb_ = ht.astype(BF16), dht.astype(BF16)
            tot, w_end, wnd = _dir_terms(la, dt, d)
            ex, rs = _expand_sel(d), _reduce_sel(d)
            elx = _dot2_r(elam, ex)
            wex = _dot2_r(w_end, ex)
            dye = dy_ * elx
            ch = _dot(cb_, htb_)
            bd = _dot(bb_, dhtb_)
            dca = dca + _dot_nt(dye.astype(BF16), htb_)
            dba = dba + _dot_nt((x * wex).astype(BF16), dhtb_)
            dlam = dlam + _dot2_r(dye * ch, rs)
            xbd = _dot2_r(x * bd, rs)
            e_ = w_end * xbd
            dlam = dlam - e_
            ddir = ddir + wnd * xbd
            hh = _dot2_r(jnp.broadcast_to(jnp.sum(dht * ht, axis=0, keepdims=True), (8, 256)), rs)[0:1]
            tot_term = jnp.sum(e_, axis=0, keepdims=True) + jnp.exp(tot) * hh
            dlam = dlam + jnp.where(row128 == (Q - 1 if d == 0 else 0), tot_term, 0.0)
            per_dir.append((w_end, jnp.concatenate([dyb_, dhtb_], axis=0), (li >= si) if d == 0 else (li <= si)))
        for r in range(HPG):
            half = slice(128 * (r // 2), 128 * (r // 2 + 1))
            hm = (lane128 >> 6) == (r % 2)
            dwt = _dot_nt(jnp.where(hm, x[:, half], 0.0).astype(BF16), dyb_[:, half])
            q = dwt * st
            for d, (w_end, rhs, maskt) in enumerate(per_dir):
                j = 4 * d + r
                dc = dt[:, j:j + 1]
                lmt = jnp.where(maskt, jnp.exp(lat_[j:j + 1, :] - la[:, j:j + 1]), 0.0)
                ldc = lmt * jnp.broadcast_to(dc, (Q, Q))
                lhs = jnp.concatenate([st * ldc, bm * w_end[:, j:j + 1]], axis=1).astype(BF16)
                dxh[r // 2] = dxh[r // 2] + jnp.where(hm, _dot(lhs, rhs[:, half]), 0.0)
                cs = jnp.sum(q * lmt, axis=1, keepdims=True)
                cdir = jnp.where(lane128 == j, cs, cdir)
                clam = jnp.where(lane128 == j, cs * dc, clam)
                rows = rows + jnp.where(sub == j, jnp.sum(q * ldc, axis=0, keepdims=True), 0.0)
                dst = dst + dwt * ldc
        dxa = jnp.concatenate(dxh, axis=1)
        ddir = ddir + cdir
        dlam = dlam - clam + rows.T
        dba = dba + _dot(dst.astype(BF16), cb_)
        dca = dca + _dot(dst.T.astype(BF16), bb_)
        isb = jnp.logical_and(lane128 >= 4, lane128 < 8)
        ddel = jnp.where(isb, _dot2_l(_tri(True), dlam), _dot2_l(_tri(False), dlam))
        a_l = a_ref[pl.ds(g, 1), :]
        ddt_ref[gg] = ddir + a_l * ddel
        ga_ref[gg] += jnp.broadcast_to(a_l * jnp.sum(dt * ddel, axis=0, keepdims=True), (8, 128))
        dx_ref[:, cols] = dxa + dy_ * sk_ref[:, cols]
        dbo_ref[:, cols128] = dba
        dco_ref[:, cols128] = dca

    nb = NG // gps
    st3 = pl.BlockSpec((None, 128, 256 * gps), lambda g, c: (c, 0, g))
    sm = pl.BlockSpec((gps, Q, 128), lambda g, c: (g, c, 0))
    smt = pl.BlockSpec((8 * gps, Q), lambda g, c: (g, c))
    wide = pl.BlockSpec((Q, 256 * gps), lambda g, c: (c, g))
    return pl.pallas_call(
        kern, name="ssd_bwd_out", grid=(nb, nc),
        in_specs=[wide, pl.BlockSpec((Q, 128 * gps), lambda g, c: (c, 2 * nb + g)),
                  pl.BlockSpec((Q, 128 * gps), lambda g, c: (c, 3 * nb + g)), wide,
                  pl.BlockSpec((1, 256 * gps), lambda g, c: (0, g)), sm, sm, smt, smt, st3, st3, st3, st3,
                  pl.BlockSpec((8, 128), lambda g, c: (0, 0))],
        out_specs=[wide, pl.BlockSpec((Q, 128 * gps), lambda g, c: (c, g)),
                   pl.BlockSpec((Q, 128 * gps), lambda g, c: (c, g)), sm, pl.BlockSpec((gps, 8, 128), lambda g, c: (g, 0, 0))],
        out_shape=[jax.ShapeDtypeStruct((lext, DI), F32), jax.ShapeDtypeStruct((lext, NG * NS), F32),
                   jax.ShapeDtypeStruct((lext, NG * NS), F32), jax.ShapeDtypeStruct((NG, lext, 128), F32),
                   jax.ShapeDtypeStruct((NG, 8, 128), F32)],
        compiler_params=_cp(("parallel", "arbitrary")),
    )(xbc, xbc, xbc, dy, dsk, dtg, lag, dtt, lat, htf, htb, dhf, dhb, a_rows)


def _post_fwd(yssm, xbc, proj_rest, dsk, gnw, nct):
    l = yssm.shape[0]

    def kern(y_ref, x_ref, z_ref, dsk_ref, w_ref, o_ref):
        y = y_ref[...].astype(F32) + dsk_ref[...] * x_ref[...]
        yz = y * _silu(z_ref[...].astype(F32))
        for g in range(NG):
            sl = slice(256 * g, 256 * (g + 1))
            v = yz[:, sl]
            r = lax.rsqrt(jnp.mean(v * v, axis=-1, keepdims=True) + EPS)
            o_ref[:, sl] = (v * r * w_ref[:, sl]).astype(BF16)

    return pl.pallas_call(
        kern, name="post_fwd", grid=(l // T,),
        in_specs=[pl.BlockSpec((T, DI), lambda i: (i, 0)), pl.BlockSpec((T, DI), lambda i: (i + nct, 0)),
                  pl.BlockSpec((T, DI), lambda i: (i, 0)), pl.BlockSpec((1, DI), lambda i: (0, 0)),
                  pl.BlockSpec((1, DI), lambda i: (0, 0))],
        out_specs=pl.BlockSpec((T, DI), lambda i: (i, 0)),
        out_shape=jax.ShapeDtypeStruct((l, DI), BF16),
        compiler_params=_cp(("parallel",)),
    )(yssm, xbc, proj_rest, dsk, gnw)


def _post_bwd(dgn, yssm, xbc, proj_rest, dsk, gnw, dpr, nct):
    l = yssm.shape[0]
    lext = xbc.shape[0]
    xi = lambda i: (jnp.maximum(i - nct, 0), 0)

    def kern(dg_ref, y_ref, x_ref, z_ref, dsk_ref, w_ref, dpr_ref, dy_ref, dz_ref, gw_ref, gd_ref):
        i = pl.program_id(0)

        @pl.when(i == 0)
        def _():
            gw_ref[...] = jnp.zeros_like(gw_ref)
            gd_ref[...] = jnp.zeros_like(gd_ref)

        @pl.when(i < nct)
        def _():
            dy_ref[...] = jnp.zeros_like(dy_ref)

        @pl.when(i >= nct)
        def _():
            xs = x_ref[...]
            z = z_ref[...].astype(F32)
            y = y_ref[...].astype(F32) + dsk_ref[...] * xs
            sz = _silu(z)
            yz = y * sz
            dgn_ = dg_ref[...].astype(F32)
            dyz_parts = []
            gws = []
            for g in range(NG):
                sl = slice(256 * g, 256 * (g + 1))
                v = yz[:, sl]
                r = lax.rsqrt(jnp.mean(v * v, axis=-1, keepdims=True) + EPS)
                vn = v * r
                dn = dgn_[:, sl] * w_ref[:, sl]
                gws.append(jnp.sum(dgn_[:, sl] * vn, axis=0, keepdims=True))
                dyz_parts.append(r * (dn - vn * jnp.mean(dn * vn, axis=-1, keepdims=True)))
            dyz = jnp.concatenate(dyz_parts, axis=1)
            gw_ref[...] += jnp.broadcast_to(jnp.concatenate(gws, axis=1), (8, DI))
            dy = dyz * sz
            dz_ref[...] = (dyz * y * _dsilu(z)).astype(BF16)
            gd_ref[...] += jnp.broadcast_to(jnp.sum(dy * xs, axis=0, keepdims=True), (8, DI))
            dy_ref[...] = dy

    return pl.pallas_call(
        kern, name="post_bwd", grid=(lext // T,),
        in_specs=[pl.BlockSpec((T, DI), xi), pl.BlockSpec((T, DI), xi), pl.BlockSpec((T, DI), lambda i: (i, 0)),
                  pl.BlockSpec((T, DI), xi), pl.BlockSpec((1, DI), lambda i: (0, 0)), pl.BlockSpec((1, DI), lambda i: (0, 0)),
                  pl.BlockSpec(memory_space=pl.ANY)],
        out_specs=[pl.BlockSpec((T, DI), lambda i: (i, 0)),
                   pl.BlockSpec((T, DI), xi), pl.BlockSpec((8, DI), lambda i: (0, 0)), pl.BlockSpec((8, DI), lambda i: (0, 0))],
        out_shape=[jax.ShapeDtypeStruct((lext, DI), F32),
                   jax.ShapeDtypeStruct((l, RESTW), BF16), jax.ShapeDtypeStruct((8, DI), F32), jax.ShapeDtypeStruct((8, DI), F32)],
        input_output_aliases={6: 1},
        compiler_params=_cp(("arbitrary",)),
    )(dgn, yssm, xbc, proj_rest, dsk, gnw, dpr)


C_G1, C_G2, C_GA, C_GB, C_CG = 2, 3, 4, 5, 6
PITCH = GW + 16
NROW = T // GW


GAP = PITCH - GW
PADR = GAP + NROW * PITCH
NSTRIP = D // 128


def _fill_padded(pad8, val):
    z = jnp.zeros((GAP, D), F32)
    parts = [z]
    for r in range(NROW):
        parts += [val[GW * r:GW * (r + 1)], z]
    p = jnp.concatenate(parts, axis=0)
    pad8[0] = p
    for j in range(1, pad8.shape[0]):
        pad8[j] = pltpu.roll(p, PADR - j, axis=0)


def _tap(pad8, base, off, ln):
    return pad8[off % 8, pl.ds(base + off - off % 8, GW), ln]


def _row_conv(out_ref, pad8, w_ref, transpose):
    def strip(s, carry):
        ln = pl.ds(pl.multiple_of(s * 128, 128), 128)
        for r in range(NROW):
            base = GAP + PITCH * r
            acc = jnp.zeros((GW, 128), F32)
            for k in range(CK):
                off = (k - 15) if not transpose else (15 - k)
                acc = acc + w_ref[pl.ds(k, 1), ln] * _tap(pad8, base, off, ln)
            out_ref[pl.ds(GW * r, GW), ln] = acc
        return carry

    lax.fori_loop(0, NSTRIP, strip, 0)


def _row_conv_wgrad(gcw_ref, padd8, pada8):
    def strip(s, carry):
        ln = pl.ds(pl.multiple_of(s * 128, 128), 128)
        rid = _iota((32, 128), 0)
        g = jnp.zeros((32, 128), F32)
        for k0 in range(0, CK, 8):
            taps = range(k0, min(k0 + 8, CK))
            accs = {k: jnp.zeros((8, 128), F32) for k in taps}
            for r in range(NROW):
                base = GAP + PITCH * r
                d = _tap(padd8, base, 0, ln)
                for k in taps:
                    p = d * pada8[0, pl.ds(base + k - 15, GW), ln]
                    part = p[0:8]
                    for q in range(1, GW // 8):
                        part = part + p[8 * q:8 * (q + 1)]
                    accs[k] = accs[k] + part
            for k in taps:
                g = jnp.where(rid == k, jnp.sum(accs[k], axis=0, keepdims=True), g)
        gcw_ref[:, ln] += g
        return carry

    lax.fori_loop(0, NSTRIP, strip, 0)


def _ln_stats(cv):
    mu = jnp.mean(cv, axis=-1, keepdims=True)
    xc = cv - mu
    rs = lax.rsqrt(jnp.mean(xc * xc, axis=-1, keepdims=True) + EPS)
    return xc * rs, rs


def _conf_fwd(proj_rest, cw, cb, lw, lb):
    l = proj_rest.shape[0]

    def kern(ga_ref, gb_ref, cg_ref, cw_ref, cb_ref, lw_ref, lb_ref, o_ref, cv_ref, pad8):
        _fill_padded(pad8, ga_ref[...].astype(F32) * _sig(gb_ref[...].astype(F32)))
        _row_conv(cv_ref, pad8, cw_ref, False)
        cv = cv_ref[...] + cb_ref[...]
        cv_ref[...] = cv
        xh, _ = _ln_stats(cv)
        ln = xh * lw_ref[...] + lb_ref[...]
        o_ref[...] = (_silu(ln) * _silu(cg_ref[...].astype(F32))).astype(BF16)

    vec = pl.BlockSpec((1, D), lambda i: (0, 0))
    blk = pl.BlockSpec((T, D), lambda i: (i, 0))
    return pl.pallas_call(
        kern, name="conf_fwd", grid=(l // T,),
        in_specs=[pl.BlockSpec((T, D), lambda i: (i, C_GA)), pl.BlockSpec((T, D), lambda i: (i, C_GB)),
                  pl.BlockSpec((T, D), lambda i: (i, C_CG)), pl.BlockSpec((32, D), lambda i: (0, 0)), vec, vec, vec],
        out_specs=[blk, blk],
        out_shape=[jax.ShapeDtypeStruct((l, D), BF16), jax.ShapeDtypeStruct((l, D), F32)],
        scratch_shapes=[pltpu.VMEM((8, PADR, D), F32)],
        compiler_params=_cp(("parallel",)),
    )(proj_rest, proj_rest, proj_rest, cw, cb, lw, lb)


def _conf_bwd(duc, cv, proj_rest, cw, lw, lb, dpr):
    l = proj_rest.shape[0]

    def kern(du_ref, cv_ref, ga_ref, gb_ref, cg_ref, cw_ref, lw_ref, lb_ref, dpr_ref, o_ref, gcw_ref, gv_ref, sc,
             pada, padd, da_ref):
        i, j = pl.program_id(0), pl.program_id(1)

        @pl.when(jnp.logical_and(i == 0, j == 0))
        def _():
            gcw_ref[...] = jnp.zeros_like(gcw_ref)
            gv_ref[...] = jnp.zeros_like(gv_ref)

        @pl.when(j == 0)
        def _():
            ga, gb, cg = ga_ref[...].astype(F32), gb_ref[...].astype(F32), cg_ref[...].astype(F32)
            sg = _sig(gb)
            xh, rs = _ln_stats(cv_ref[...])
            ln = xh * lw_ref[...] + lb_ref[...]
            du = du_ref[...].astype(F32)
            sc[:, 2 * D:3 * D] = (du * _silu(ln) * _dsilu(cg)).astype(BF16)
            dln = du * _silu(cg) * _dsilu(ln)
            g_lw = jnp.sum(dln * xh, axis=0, keepdims=True)
            g_lb = jnp.sum(dln, axis=0, keepdims=True)
            dxh = dln * lw_ref[...]
            dcv = rs * (dxh - jnp.mean(dxh, axis=-1, keepdims=True) - xh * jnp.mean(dxh * xh, axis=-1, keepdims=True))
            g_cb = jnp.sum(dcv, axis=0, keepdims=True)
            rid = _iota((8, D), 0)
            gv_ref[...] += jnp.where(rid == 0, g_cb, jnp.where(rid == 1, g_lw, jnp.where(rid == 2, g_lb, 0.0)))
            _fill_padded(padd, dcv)
            _fill_padded(pada, ga * sg)
            _row_conv(da_ref, padd, cw_ref, True)
            _row_conv_wgrad(gcw_ref, padd, pada)
            da = da_ref[...]
            sc[:, 0:D] = (da * sg).astype(BF16)
            sc[:, D:2 * D] = (da * ga * sg * (1.0 - sg)).astype(BF16)

        o_ref[...] = sc[:, pl.ds(pl.multiple_of(j * D, 128), D)]

    vec = pl.BlockSpec((1, D), lambda i, j: (0, 0))
    col = lambda c: pl.BlockSpec((T, D), lambda i, j: (i, c))
    return pl.pallas_call(
        kern, name="conf_bwd", grid=(l // T, 3),
        in_specs=[col(0), col(0), col(C_GA), col(C_GB), col(C_CG), pl.BlockSpec((32, D), lambda i, j: (0, 0)), vec, vec,
                  pl.BlockSpec(memory_space=pl.ANY)],
        out_specs=[pl.BlockSpec((T, D), lambda i, j: (i, C_GA + j)), pl.BlockSpec((32, D), lambda i, j: (0, 0)),
                   pl.BlockSpec((8, D), lambda i, j: (0, 0))],
        out_shape=[jax.ShapeDtypeStruct((l, RESTW), BF16), jax.ShapeDtypeStruct((32, D), F32),
                   jax.ShapeDtypeStruct((8, D), F32)],
        scratch_shapes=[pltpu.VMEM((T, 3 * D), BF16), pltpu.VMEM((1, PADR, D), F32), pltpu.VMEM((8, PADR, D), F32),
                        pltpu.VMEM((T, D), F32)],
        input_output_aliases={8: 0},
        compiler_params=_cp(("arbitrary", "arbitrary")),
    )(duc, cv, proj_rest, proj_rest, proj_rest, cw, lw, lb, dpr)


def _merge_fwd(bs, bc, proj_rest):
    l = bs.shape[0]

    def kern(bs_ref, bc_ref, g1_ref, g2_ref, o_ref):
        up = lambda r: r[...].astype(F32)
        o_ref[...] = (_sig(up(g1_ref)) * up(bs_ref) + _sig(up(g2_ref)) * up(bc_ref)).astype(BF16)

    blk = pl.BlockSpec((TX, D), lambda i: (i, 0))
    return pl.pallas_call(
        kern, name="merge_fwd", grid=(l // TX,),
        in_specs=[blk, blk, pl.BlockSpec((TX, D), lambda i: (i, C_G1)), pl.BlockSpec((TX, D), lambda i: (i, C_G2))],
        out_specs=blk, out_shape=jax.ShapeDtypeStruct((l, D), BF16),
        compiler_params=_cp(("parallel",)),
    )(bs, bc, proj_rest, proj_rest)


def _merge_bwd(dm, bs, bc, proj_rest):
    l = bs.shape[0]

    def kern(dm_ref, bs_ref, bc_ref, g1_ref, g2_ref, dbs_ref, dbc_ref, dg_ref):
        up = lambda r: r[...].astype(F32)
        dm_ = up(dm_ref)
        s1, s2 = _sig(up(g1_ref)), _sig(up(g2_ref))
        dbs_ref[...] = (dm_ * s1).astype(BF16)
        dbc_ref[...] = (dm_ * s2).astype(BF16)
        dg_ref[:, 0:D] = (dm_ * up(bs_ref) * s1 * (1.0 - s1)).astype(BF16)
        dg_ref[:, D:2 * D] = (dm_ * up(bc_ref) * s2 * (1.0 - s2)).astype(BF16)

    blk = pl.BlockSpec((TX, D), lambda i: (i, 0))
    return pl.pallas_call(
        kern, name="merge_bwd", grid=(l // TX,),
        in_specs=[blk, blk, blk, pl.BlockSpec((TX, D), lambda i: (i, C_G1)), pl.BlockSpec((TX, D), lambda i: (i, C_G2))],
        out_specs=[blk, blk, pl.BlockSpec((TX, 2 * D), lambda i: (i, 1))],
        out_shape=[jax.ShapeDtypeStruct((l, D), BF16), jax.ShapeDtypeStruct((l, D), BF16),
                   jax.ShapeDtypeStruct((l, RESTW), BF16)],
        compiler_params=_cp(("parallel",)),
    )(dm, bs, bc, proj_rest, proj_rest)


def _final(x, out, tgt, mod, fw):
    l = x.shape[0]

    def kern(x_ref, o_ref, t_ref, mod_ref, fw_ref, ls_ref, dx2_ref, do_ref, gv_ref):
        @pl.when(pl.program_id(0) == 0)
        def _():
            ls_ref[...] = jnp.zeros_like(ls_ref)
            gv_ref[...] = jnp.zeros_like(gv_ref)

        gate = mod_ref[0:1, 2 * D:3 * D]
        o = o_ref[...]
        x2 = x_ref[...] + gate * o
        r = lax.rsqrt(jnp.mean(x2 * x2, axis=-1, keepdims=True) + EPS)
        yn = x2 * r
        fw_ = fw_ref[...]
        e = yn * fw_ - t_ref[...]
        ls_ref[...] += jnp.full((8, 128), 1.0, F32) * (0.5 / D) * jnp.sum(e * e)
        dy = e * (1.0 / D)
        g_fw = jnp.sum(dy * yn, axis=0, keepdims=True)
        dyn = dy * fw_
        dx2 = r * (dyn - yn * jnp.mean(dyn * yn, axis=-1, keepdims=True))
        g_gate = jnp.sum(dx2 * o, axis=0, keepdims=True)
        rid = _iota((8, D), 0)
        gv_ref[...] += jnp.where(rid == 0, g_fw, jnp.where(rid == 1, g_gate, 0.0))
        dx2_ref[...] = dx2
        do_ref[...] = (dx2 * gate).astype(BF16)

    blk = pl.BlockSpec((TX, D), lambda i: (i, 0))
    return pl.pallas_call(
        kern, name="final", grid=(l // TX,),
        in_specs=[blk, blk, blk, pl.BlockSpec((8, 3 * D), lambda i: (0, 0)), pl.BlockSpec((1, D), lambda i: (0, 0))],
        out_specs=[pl.BlockSpec((8, 128), lambda i: (0, 0)), blk, blk, pl.BlockSpec((8, D), lambda i: (0, 0))],
        out_shape=[jax.ShapeDtypeStruct((8, 128), F32), jax.ShapeDtypeStruct((l, D), F32),
                   jax.ShapeDtypeStruct((l, D), BF16), jax.ShapeDtypeStruct((8, D), F32)],
        compiler_params=_cp(("arbitrary",)),
    )(x, out, tgt, mod, fw)


def _perm_dt_cols(w):
    s = w.shape[:-1]
    return w.reshape(*s, 2, NG, HPG).swapaxes(-3, -2).reshape(*s, 64)


def _unperm_dt_cols(w):
    s = w.shape[:-1]
    return w.reshape(*s, NG, 2, HPG).swapaxes(-3, -2).reshape(*s, 64)


def _pad_lanes(v, width):
    return jnp.pad(v, ((0, 0), (0, width - v.shape[1])))


def _vcols(segs, a, b):
    parts, off = [], 0
    for s in segs:
        lo, hi = max(a, off), min(b, off + s.shape[1])
        if lo < hi:
            parts.append(s[:, lo - off:hi - off])
        off += s.shape[1]
    return parts[0] if len(parts) == 1 else jnp.concatenate(parts, axis=1)


def _local_step(x, c, ctx, tgt, w):
    l = x.shape[0]
    nct = CTX // T
    ncc = CTX // Q
    lext = l + CTX

    w_mod = w["w_mod"].astype(BF16)
    wsegs = [s.astype(BF16) for s in (w["w_in"] if isinstance(w["w_in"], (list, tuple)) else [w["w_in"]])]
    w_ssd = jnp.concatenate([_vcols(wsegs, 0, XBC), _perm_dt_cols(_vcols(wsegs, XBC, XBC + 64)), jnp.zeros((D, 64), BF16)], axis=1)
    r0 = XBC + 64
    w_rest = jnp.concatenate([_vcols(wsegs, r0, r0 + DI), _vcols(wsegs, r0 + DI + 3 * D, r0 + RESTW),
                              _vcols(wsegs, r0 + DI, r0 + DI + 3 * D)], axis=1)
    w_os, w_oc, w_o = w["w_out_ssm"].astype(BF16), w["w_out_conf"].astype(BF16), w["w_out"].astype(BF16)
    cw8 = jnp.pad(w["ssm_conv_w"], ((0, 4), (0, 0)))
    cb_s = w["ssm_conv_b"].reshape(1, XBC)
    dtb = _pad_lanes(_perm_dt_cols(w["dt_bias"].reshape(1, 64)), 128)
    a_all = -jnp.exp(w["a_log"].reshape(1, 64))
    a_perm = _pad_lanes(_perm_dt_cols(a_all), 128)
    a_rows = _pad_lanes(_perm_dt_cols(a_all).reshape(NG, 8), 128)
    dsk = jnp.repeat(w["d_skip"].reshape(NH), HP).reshape(1, DI)
    gnw = w["ssm_norm_w"].reshape(1, DI)
    ccw = jnp.pad(w["conf_conv_w"], ((0, 1), (0, 0)))
    ccb, clw, clb = w["conf_conv_b"].reshape(1, D), w["conf_ln_w"].reshape(1, D), w["conf_ln_b"].reshape(1, D)
    nw = w["norm_w"].reshape(1, D)
    fw = w["final_norm_w"].reshape(1, D)
    cc = jnp.concatenate([c.reshape(1, D), w["c_ctx"].reshape(1, D), jnp.zeros((6, D), F32)], axis=0)

    bx = min(1024, l)
    be = 768 if lext % 768 == 0 else 256
    tk = min(1024, l)
    mod = _mod_fwd(cc, w_mod, w["b_mod"].reshape(1, 3 * D))
    h = _norm_fwd(ctx, x, mod, nw, nct)
    hx = h[CTX:]
    proj_ssd = _mm(h, w_ssd, "nn", lext, SSDW, D, be, SSDW // 3, D, F32, "proj_ssd")
    proj_rest = _mm(hx, w_rest, "nn", l, RESTW, D, bx, 1024, D, BF16, "proj_rest")
    xbc = _conv_fwd(proj_ssd, cw8, cb_s, nct)
    dtg, lag, dtt, lat = _dt_fwd(proj_ssd, dtb, a_perm)
    htf, htb = _ssd_state(xbc, dtg, lag, ncc)
    yssm = _ssd_out(xbc, dtg, lag, dtt, lat, htf, htb, ncc)
    gn = _post_fwd(yssm, xbc, proj_rest, dsk, gnw, nct)
    bs = _mm(gn, w_os, "nn", l, D, DI, bx, D, DI, BF16, "out_ssm")
    uc, cv = _conf_fwd(proj_rest, ccw, ccb, clw, clb)
    bc = _mm(uc, w_oc, "nn", l, D, D, bx, D, D, BF16, "out_conf")
    merged = _merge_fwd(bs, bc, proj_rest)
    out = _mm(merged, w_o, "nn", l, D, D, bx, D, D, F32, "out_proj")
    lsum, dx2, dout, gv_fin = _final(x, out, tgt, mod, fw)

    g = {}
    g["final_norm_w"] = gv_fin[0]
    dmerged = _mm(dout, w_o, "nt", l, D, D, bx, D, D, BF16, "d_merged")
    g["w_out"] = _mm(merged, dout, "tn", D, D, l, D, D, tk, F32, "g_w_out")
    dbs, dbc, dpr = _merge_bwd(dmerged, bs, bc, proj_rest)
    dgn = _mm(dbs, w_os, "nt", l, DI, D, bx, DI, D, BF16, "d_gn")
    g["w_out_ssm"] = _mm(gn, dbs, "tn", DI, D, l, DI, D, tk, F32, "g_w_out_ssm")
    duc = _mm(dbc, w_oc, "nt", l, D, D, bx, D, D, BF16, "d_uc")
    g["w_out_conf"] = _mm(uc, dbc, "tn", D, D, l, D, D, tk, F32, "g_w_out_conf")
    dpr, gcw, gv_conf = _conf_bwd(duc, cv, proj_rest, ccw, clw, clb, dpr)
    g["conf_conv_w"] = gcw[:CK]
    g["conf_conv_b"], g["conf_ln_w"], g["conf_ln_b"] = gv_conf[0], gv_conf[1], gv_conf[2]
    dy, dproj_rest, ggnw, gdsk = _post_bwd(dgn, yssm, xbc, proj_rest, dsk, gnw, dpr, nct)
    g["ssm_norm_w"] = ggnw[0]
    g["d_skip"] = gdsk[0].reshape(NH, HP).sum(axis=1)
    dhf, dhb = _ssd_bwd_state(xbc, dy, lag, ncc)
    dxs, dbm, dcm, ddtg, galog = _ssd_bwd_out(xbc, dy, dsk, dtg, lag, dtt, lat, htf, htb, dhf, dhb, a_rows)
    g["a_log"] = _unperm_dt_cols(galog[:, 0, 0:8].reshape(1, 64)).reshape(2, NH)
    dus, gws, gbs = [], [], []
    for dpost, off, width, nm in ((dxs, 0, DI, "conv_bwd_x"), (dbm, DI, NG * NS, "conv_bwd_b"), (dcm, DI + NG * NS, NG * NS, "conv_bwd_c")):
        du_, gw_, gb_ = _conv_bwd(dpost, proj_ssd, cw8, cb_s, off, width, nct, nm)
        dus.append(du_)
        gws.append(gw_[:SK])
        gbs.append(gb_[0])
    g["ssm_conv_w"] = jnp.concatenate(gws, axis=1)
    g["ssm_conv_b"] = jnp.concatenate(gbs, axis=0)
    ddt_raw, gdtb = _dt_bwd(ddtg, proj_ssd, dtb)
    g["dt_bias"] = _unperm_dt_cols(gdtb[0:1, 0:64]).reshape(2, NH)
    dproj_ssd = jnp.concatenate(dus + [ddt_raw], axis=1)
    gw_ssd = _mm(h, dproj_ssd, "tn", D, SSDW, lext, D, SSDW // 3, be, F32, "g_w_ssd")
    gw_rest = _mm(hx, dproj_rest, "tn", D, RESTW, l, D, 1024, tk, F32, "g_w_rest")
    gsegs = [gw_ssd[:, :XBC], _unperm_dt_cols(gw_ssd[:, XBC:XBC + 64]), gw_rest[:, :DI], gw_rest[:, 2 * DI:],
             gw_rest[:, DI:2 * DI]]
    g["w_in"] = jnp.concatenate(gsegs, axis=1)
    g["w_in_shards"] = jnp.stack([_vcols(gsegs, R_IN * s, R_IN * (s + 1)) for s in range(NSHARD)])
    dh_a = _mm(dproj_ssd, w_ssd, "nt", lext, D, SSDW, T, D, SSDW, BF16, "dh_ssd")
    dh_b = _mm(dproj_rest, w_rest, "nt", l, D, RESTW, T, D, RESTW, BF16, "dh_rest")
    grad_x, gnw_in, dss = _norm_bwd(dh_a, dh_b, ctx, x, dx2, mod, nw, nct)
    g["norm_w"] = gnw_in[0]
    dmod = jnp.concatenate([jnp.concatenate([dss[0:1], gv_fin[1:2]], axis=1),
                            jnp.concatenate([dss[1:2], jnp.zeros((1, D), F32)], axis=1),
                            jnp.zeros((6, 3 * D), F32)], axis=0)
    gwm, gbm, gcc = _mod_bwd(dmod, cc, cc.T, w_mod)
    g["w_mod"], g["b_mod"], g["c_ctx"] = gwm, gbm[0], gcc[1]
    return lsum[0, 0], grad_x, g


NSHARD = 4
R_MOD, R_IN, R_OS, R_OC, R_O, R_SC, R_CC = 768, 2832, 512, 256, 256, 8, 8
O_MOD = 0
O_OS = O_MOD + R_MOD
O_OC = O_OS + R_OS
O_O = O_OC + R_OC
O_SC = O_O + R_O
O_CC = O_SC + R_SC
PUSED = O_CC + R_CC
PROWS = 1824
HALF = PROWS // 2
RB = HALF // 3
WB = 128
SROWS = 16
SMALL = (("b_mod", 3 * D), ("norm_w", D), ("ssm_conv_b", XBC), ("dt_bias", 64), ("a_log", 64), ("d_skip", NH),
         ("ssm_norm_w", DI), ("conf_conv_b", D), ("conf_ln_w", D), ("conf_ln_b", D), ("final_norm_w", D), ("c_ctx", D))


def _pack_shard(s):
    return jnp.concatenate([s["w_mod"].reshape(R_MOD, D), _pack_rest(s), jnp.zeros((PROWS - PUSED, D), F32)], axis=0)


def _pack_rest(s):
    cc = jnp.pad(s["conf_conv_w"].reshape(1, CK * 256), ((0, 0), (0, R_CC * D - CK * 256))).reshape(R_CC, D)
    return jnp.concatenate([s["w_out_ssm"], s["w_out_conf"], s["w_out"],
                            jnp.pad(s["ssm_conv_w"], ((0, R_SC - SK), (0, 0))), cc], axis=0)


def _unpack_rest(p):
    o = lambda r: r - O_OS
    return {"w_out_ssm": p[o(O_OS):o(O_OC)][None], "w_out_conf": p[o(O_OC):o(O_O)][None], "w_out": p[o(O_O):o(O_SC)][None],
            "ssm_conv_w": p[o(O_SC):o(O_SC) + SK][None],
            "conf_conv_w": p[o(O_CC):o(O_CC) + R_CC].reshape(R_CC * D)[:CK * 256].reshape(1, CK, 256)}


def _shard_cols(a, n):
    return a.reshape(a.shape[0], NSHARD, n).transpose(1, 0, 2)


def _pack_full(g):
    cc = jnp.pad(_shard_cols(g["conf_conv_w"], 256).reshape(NSHARD, CK * 256), ((0, 0), (0, R_CC * D - CK * 256)))
    return jnp.concatenate([_shard_cols(g["w_mod"], R_MOD).reshape(NSHARD, R_MOD, D),
                            g["w_out_ssm"].reshape(NSHARD, R_OS, D), g["w_out_conf"].reshape(NSHARD, R_OC, D),
                            g["w_out"].reshape(NSHARD, R_O, D),
                            jnp.pad(_shard_cols(g["ssm_conv_w"], D), ((0, 0), (0, R_SC - SK), (0, 0))),
                            cc.reshape(NSHARD, R_CC, D), jnp.zeros((NSHARD, PROWS - PUSED, D), F32)], axis=1)


def _unpack_gathered(gm, gw, gs):
    def cols(a, r, n):
        return a.reshape(NSHARD, r, n).transpose(1, 0, 2).reshape(r, NSHARD * n)
    return {"w_mod": cols(gm[:, O_MOD:O_OS], D, R_MOD), "w_in": [gw[s] for s in range(NSHARD)],
            "w_out_ssm": gm[:, O_OS:O_OC].reshape(DI, D), "w_out_conf": gm[:, O_OC:O_O].reshape(D, D),
            "w_out": gm[:, O_O:O_SC].reshape(D, D), "ssm_conv_w": cols(gs[:, 0:SK], SK, D),
            "conf_conv_w": cols(gs[:, R_SC:R_SC + R_CC].reshape(NSHARD, R_CC * D)[:, :CK * 256], CK, 256)}


MESH_ID = pl.DeviceIdType.MESH
ANY = pl.BlockSpec(memory_space=pl.ANY)


def _place():
    x, y, c = lax.axis_index("x"), lax.axis_index("y"), lax.axis_index("c")
    return x, y, c, [(1 - x, y), (x, 1 - y), (1 - x, 1 - y)]


def _rcopy(src, dst, send, recv, dev):
    return pltpu.make_async_remote_copy(src_ref=src, dst_ref=dst, send_sem=send, recv_sem=recv,
                                        device_id=dev, device_id_type=MESH_ID)


def _gather_weights(mats, small):
    n = len(mats)

    def kern(*refs):
        m_refs, s_ref, g_refs, gs_ref, (send, recv) = refs[:n], refs[n], refs[n + 1:2 * n + 1], refs[2 * n + 1], refs[2 * n + 2:]
        x, y, c, chips = _place()
        me = 2 * x + y
        sib = (x, y, 1 - c)
        first, passed = [], []
        for k, (px, py) in enumerate(chips):
            first.append(_rcopy(s_ref, gs_ref.at[me], send.at[k], recv.at[k], (px, py, c)))
            for a, (m_ref, g_ref) in enumerate(zip(m_refs, g_refs)):
                mine = _half_rows(c, m_ref.shape[0])
                first.append(_rcopy(m_ref.at[mine], g_ref.at[me, mine], send.at[3 + 6 * a + k], recv.at[3 + 6 * a + k], (px, py, c)))
        for cp in first:
            cp.start()
        for k, (px, py) in enumerate(chips):
            s = 2 * px + py
            for a, (m_ref, g_ref) in enumerate(zip(m_refs, g_refs)):
                mine = _half_rows(c, m_ref.shape[0])
                _rcopy(m_ref.at[mine], g_ref.at[s, mine], send.at[3 + 6 * a + k], recv.at[3 + 6 * a + k], sib).wait_recv()
                f = _rcopy(g_ref.at[s, mine], g_ref.at[s, mine], send.at[6 + 6 * a + k], recv.at[6 + 6 * a + k], sib)
                f.start()
                passed.append(f)
        for k, (px, py) in enumerate(chips):
            s = 2 * px + py
            _rcopy(s_ref, gs_ref.at[s], send.at[k], recv.at[k], sib).wait_recv()
            for a, g_ref in enumerate(g_refs):
                other = _half_rows(1 - c, g_ref.shape[1])
                _rcopy(g_ref.at[s, other], g_ref.at[s, other], send.at[6 + 6 * a + k], recv.at[6 + 6 * a + k], sib).wait_recv()
        for cp in first + passed:
            cp.wait_send()

    nsem = 3 + 6 * n
    return pl.pallas_call(
        kern, name="gather_weights", in_specs=[ANY] * (n + 1), out_specs=[ANY] * (n + 1),
        out_shape=[jax.ShapeDtypeStruct((NSHARD,) + m.shape, m.dtype) for m in mats]
        + [jax.ShapeDtypeStruct((NSHARD, SROWS, D), F32)],
        scratch_shapes=[pltpu.SemaphoreType.DMA((nsem,)), pltpu.SemaphoreType.DMA((nsem,))],
    )(*mats, small)


def _half_rows(c, rows):
    return pl.ds(pl.multiple_of(c * (rows // 2), 16), rows // 2)


def _swap_halves(gs):
    n = len(gs)

    def kern(*refs):
        g_refs, o_refs, (send, recv) = refs[:n], refs[n:2 * n], refs[2 * n:]
        x, y, c, _ = _place()
        cps = [_rcopy(g_ref.at[s, _half_rows(1 - c, g_ref.shape[1])], o_ref.at[s], send.at[NSHARD * a + s],
                      recv.at[NSHARD * a + s], (x, y, 1 - c))
               for a, (g_ref, o_ref) in enumerate(zip(g_refs, o_refs)) for s in range(NSHARD)]
        for cp in cps:
            cp.start()
        for cp in cps:
            cp.wait()

    return pl.pallas_call(
        kern, name="swap_halves", in_specs=[ANY] * n, out_specs=[ANY] * n,
        out_shape=[jax.ShapeDtypeStruct((NSHARD, g.shape[1] // 2, g.shape[2]), F32) for g in gs],
        scratch_shapes=[pltpu.SemaphoreType.DMA((NSHARD * n,)), pltpu.SemaphoreType.DMA((NSHARD * n,))],
    )(*gs)


def _add_halves(cidx, g, ra, rb, name):
    _, half, cols = ra.shape
    nb = half // rb

    def kern(c_ref, g_ref, a_ref, o_ref):
        o_ref[...] = (g_ref[...] + a_ref[...]).astype(BF16)

    return pl.pallas_call(
        kern, name=name,
        grid_spec=pltpu.PrefetchScalarGridSpec(
            num_scalar_prefetch=1, grid=(NSHARD, nb),
            in_specs=[pl.BlockSpec((None, rb, cols), lambda s, i, c: (s, c[0] * nb + i, 0)),
                      pl.BlockSpec((None, rb, cols), lambda s, i, c: (s, i, 0))],
            out_specs=pl.BlockSpec((None, rb, cols), lambda s, i, c: (s, i, 0))),
        out_shape=jax.ShapeDtypeStruct((NSHARD, half, cols), BF16),
        compiler_params=_cp(("parallel", "parallel")),
    )(cidx, g, ra)


def _exchange_chips(ps):
    n = len(ps)

    def kern(*refs):
        p_refs, o_refs, (send, recv) = refs[:n], refs[n:2 * n], refs[2 * n:]
        x, y, c, chips = _place()
        cps = [_rcopy(p_ref.at[2 * px + py], o_ref.at[k], send.at[3 * a + k], recv.at[3 * a + k], (px, py, c))
               for a, (p_ref, o_ref) in enumerate(zip(p_refs, o_refs)) for k, (px, py) in enumerate(chips)]
        for cp in cps:
            cp.start()
        for cp in cps:
            cp.wait()

    return pl.pallas_call(
        kern, name="exchange_chips", in_specs=[ANY] * n, out_specs=[ANY] * n,
        out_shape=[jax.ShapeDtypeStruct((3,) + p.shape[1:], p.dtype) for p in ps],
        scratch_shapes=[pltpu.SemaphoreType.DMA((3 * n,)), pltpu.SemaphoreType.DMA((3 * n,))],
    )(*ps)


def _add_chips(mc, g, ra, rx, rb, name):
    _, half, cols = ra.shape
    nb = half // rb

    def kern(m_ref, g_ref, a_ref, r0_ref, r1_ref, r2_ref, o_ref):
        own = g_ref[...] + a_ref[...]
        o_ref[...] = ((own + r0_ref[...].astype(F32)) + r1_ref[...].astype(F32)) + r2_ref[...].astype(F32)

    return pl.pallas_call(
        kern, name=name,
        grid_spec=pltpu.PrefetchScalarGridSpec(
            num_scalar_prefetch=1, grid=(nb,),
            in_specs=[pl.BlockSpec((None, rb, cols), lambda i, m: (m[0], m[1] * nb + i, 0)),
                      pl.BlockSpec((None, rb, cols), lambda i, m: (m[0], i, 0))]
            + [pl.BlockSpec((None, rb, cols), functools.partial(lambda i, m, k: (k, i, 0), k=k)) for k in range(3)],
            out_specs=pl.BlockSpec((rb, cols), lambda i, m: (i, 0))),
        out_shape=jax.ShapeDtypeStruct((half, cols), F32),
        compiler_params=_cp(("parallel",)),
    )(mc, g, ra, rx, rx, rx)


def _share_halves(rs):
    n = len(rs)

    def kern(*refs):
        r_refs, o_refs, (send, recv) = refs[:n], refs[n:2 * n], refs[2 * n:]
        x, y, c, _ = _place()
        cps = [_rcopy(r_ref, o_ref, send.at[a], recv.at[a], (x, y, 1 - c))
               for a, (r_ref, o_ref) in enumerate(zip(r_refs, o_refs))]
        for cp in cps:
            cp.start()
        for cp in cps:
            cp.wait()

    return pl.pallas_call(
        kern, name="share_halves", in_specs=[ANY] * n, out_specs=[ANY] * n,
        out_shape=[jax.ShapeDtypeStruct(r.shape, F32) for r in rs],
        scratch_shapes=[pltpu.SemaphoreType.DMA((n,)), pltpu.SemaphoreType.DMA((n,))],
    )(*rs)


SMALL_W = XBC


def _small_update(gs, ws, ms, vs):
    n = len(gs)
    widths = [g.shape[1] for g in gs]
    assert n <= SROWS and max(widths) <= SMALL_W

    def kern(*refs):
        g_refs, w_refs, m_refs, v_refs = (refs[n * i:n * (i + 1)] for i in range(4))
        o_g, o_d, o_m, o_v = (refs[n * (4 + i):n * (5 + i)] for i in range(4))
        buf, send, recv = refs[8 * n:]
        x, y, c, _ = _place()
        me = 4 * x + 2 * y + c
        buf[me] = jnp.zeros((SROWS, SMALL_W), F32)
        for k, g_ref in enumerate(g_refs):
            buf[me, k:k + 1, 0:widths[k]] = g_ref[...]
        cps = []
        for r in range(1, 8):
            peer = (1 - x if r & 4 else x, 1 - y if r & 2 else y, 1 - c if r & 1 else c)
            cps.append(_rcopy(buf.at[me], buf.at[me], send.at[r - 1], recv.at[r - 1], peer))
        for cp in cps:
            cp.start()
        for cp in cps:
            cp.wait()
        acc = buf[0]
        for i in range(1, 8):
            acc = acc + buf[i]
        for k in range(n):
            g_ = acc[k:k + 1, 0:widths[k]]
            m_ = ADAM_B1 * m_refs[k][...] + (1.0 - ADAM_B1) * g_
            v_ = ADAM_B2 * v_refs[k][...] + (1.0 - ADAM_B2) * jnp.square(g_)
            m_hat = m_ / (1.0 - ADAM_B1 ** ADAM_STEP)
            v_hat = v_ / (1.0 - ADAM_B2 ** ADAM_STEP)
            o_g[k][...] = g_
            o_d[k][...] = -ADAM_LR * (m_hat / (jnp.sqrt(v_hat) + ADAM_EPS) + ADAM_WD * w_refs[k][...])
            o_m[k][...] = m_
            o_v[k][...] = v_

    vm = pl.BlockSpec(memory_space=pltpu.VMEM)
    outs = pl.pallas_call(
        kern, name="small_update", in_specs=[vm] * (4 * n), out_specs=[vm] * (4 * n),
        out_shape=[jax.ShapeDtypeStruct((1, wd), F32) for _ in range(4) for wd in widths],
        scratch_shapes=[pltpu.VMEM((8, SROWS, SMALL_W), F32), pltpu.SemaphoreType.DMA((7,)), pltpu.SemaphoreType.DMA((7,))],
    )(*gs, *ws, *ms, *vs)
    return [outs[n * i:n * (i + 1)] for i in range(4)]


def _adamw(g, w, m, v, rb, name):
    rows, cols = g.shape

    def kern(g_ref, w_ref, m_ref, v_ref, d_ref, nm_ref, nv_ref):
        g_ = g_ref[...]
        m_ = ADAM_B1 * m_ref[...] + (1.0 - ADAM_B1) * g_
        v_ = ADAM_B2 * v_ref[...] + (1.0 - ADAM_B2) * jnp.square(g_)
        m_hat = m_ / (1.0 - ADAM_B1 ** ADAM_STEP)
        v_hat = v_ / (1.0 - ADAM_B2 ** ADAM_STEP)
        d_ref[...] = -ADAM_LR * (m_hat / (jnp.sqrt(v_hat) + ADAM_EPS) + ADAM_WD * w_ref[...])
        nm_ref[...] = m_
        nv_ref[...] = v_

    assert rows % rb == 0
    blk = pl.BlockSpec((rb, cols), lambda i: (i, 0))
    return pl.pallas_call(
        kern, name=name, grid=(rows // rb,), in_specs=[blk] * 4, out_specs=[blk] * 3,
        out_shape=[jax.ShapeDtypeStruct((rows, cols), F32)] * 3,
        compiler_params=_cp(("parallel",)),
    )(g, w, m, v)


def _adamw_halves(cidx, mine, other, w, m, v, rb, name):
    rows, cols = w.shape
    nbh = rows // 2 // rb

    def kern(c_ref, a_ref, b_ref, w_ref, m_ref, v_ref, g_ref, d_ref, nm_ref, nv_ref):
        g_ = jnp.where(pl.program_id(0) // nbh == c_ref[0], a_ref[...], b_ref[...])
        m_ = ADAM_B1 * m_ref[...] + (1.0 - ADAM_B1) * g_
        v_ = ADAM_B2 * v_ref[...] + (1.0 - ADAM_B2) * jnp.square(g_)
        m_hat = m_ / (1.0 - ADAM_B1 ** ADAM_STEP)
        v_hat = v_ / (1.0 - ADAM_B2 ** ADAM_STEP)
        g_ref[...] = g_
        d_ref[...] = -ADAM_LR * (m_hat / (jnp.sqrt(v_hat) + ADAM_EPS) + ADAM_WD * w_ref[...])
        nm_ref[...] = m_
        nv_ref[...] = v_

    half = pl.BlockSpec((rb, cols), lambda i, c: (i % nbh, 0))
    blk = pl.BlockSpec((rb, cols), lambda i, c: (i, 0))
    return pl.pallas_call(
        kern, name=name,
        grid_spec=pltpu.PrefetchScalarGridSpec(num_scalar_prefetch=1, grid=(2 * nbh,), in_specs=[half, half, blk, blk, blk],
                                               out_specs=[blk] * 4),
        out_shape=[jax.ShapeDtypeStruct((rows, cols), F32)] * 4,
        compiler_params=_cp(("parallel",)),
    )(cidx, mine, other, w, m, v)


WEIGHTS = ("c_ctx", "w_mod", "b_mod", "norm_w", "w_in", "ssm_conv_w", "ssm_conv_b", "dt_bias", "a_log", "d_skip",
           "ssm_norm_w", "w_out_ssm", "conf_conv_w", "conf_conv_b", "conf_ln_w", "conf_ln_b", "w_out_conf", "w_out",
           "final_norm_w")


def kernel(x, c, ctx, c_ctx, w_mod, b_mod, norm_w, w_in, ssm_conv_w, ssm_conv_b, dt_bias, a_log, d_skip, ssm_norm_w, w_out_ssm, conf_conv_w, conf_conv_b, conf_ln_w, conf_ln_b, w_out_conf, w_out, final_norm_w, loss_target, m_c_ctx, m_w_mod, m_b_mod, m_norm_w, m_w_in, m_ssm_conv_w, m_ssm_conv_b, m_dt_bias, m_a_log, m_d_skip, m_ssm_norm_w, m_w_out_ssm, m_conf_conv_w, m_conf_conv_b, m_conf_ln_w, m_conf_ln_b, m_w_out_conf, m_w_out, m_final_norm_w, v_c_ctx, v_w_mod, v_b_mod, v_norm_w, v_w_in, v_ssm_conv_w, v_ssm_conv_b, v_dt_bias, v_a_log, v_d_skip, v_ssm_norm_w, v_w_out_ssm, v_conf_conv_w, v_conf_conv_b, v_conf_ln_w, v_conf_ln_b, v_w_out_conf, v_w_out, v_final_norm_w):
    wv = (c_ctx, w_mod, b_mod, norm_w, w_in, ssm_conv_w, ssm_conv_b, dt_bias, a_log, d_skip, ssm_norm_w, w_out_ssm,
          conf_conv_w, conf_conv_b, conf_ln_w, conf_ln_b, w_out_conf, w_out, final_norm_w)
    mv = (m_c_ctx, m_w_mod, m_b_mod, m_norm_w, m_w_in, m_ssm_conv_w, m_ssm_conv_b, m_dt_bias, m_a_log, m_d_skip,
          m_ssm_norm_w, m_w_out_ssm, m_conf_conv_w, m_conf_conv_b, m_conf_ln_w, m_conf_ln_b, m_w_out_conf, m_w_out,
          m_final_norm_w)
    vv = (v_c_ctx, v_w_mod, v_b_mod, v_norm_w, v_w_in, v_ssm_conv_w, v_ssm_conv_b, v_dt_bias, v_a_log, v_d_skip,
          v_ssm_norm_w, v_w_out_ssm, v_conf_conv_w, v_conf_conv_b, v_conf_ln_w, v_conf_ln_b, v_w_out_conf, v_w_out,
          v_final_norm_w)
    shapes = {n: a.shape for n, a in zip(WEIGHTS, wv)}

    def squeeze(d):
        return {n: (a if n in ("c_ctx", "final_norm_w") else a[0]) for n, a in d.items()}

    w, m, v = (squeeze(dict(zip(WEIGHTS, t))) for t in (wv, mv, vv))

    my_chip = 2 * lax.axis_index("x") + lax.axis_index("y")
    my_core = lax.axis_index("c")

    pw = _pack_shard(w)
    pwb, wib, psm = pw.astype(BF16), w["w_in"].astype(BF16), pw[O_SC:O_SC + SROWS]
    gm, gw, gs = _gather_weights([pwb, wib], psm)
    mine = (jnp.arange(NSHARD) == my_chip)[:, None, None]
    gm, gw, gs = jnp.where(mine, pwb[None], gm), jnp.where(mine, wib[None], gw), jnp.where(mine, psm[None], gs)
    full = dict(w)
    full.update(_unpack_gathered(gm, gw, gs))

    lsum, grad_x, g = _local_step(x[0], c, ctx[0], loss_target[0], full)
    loss = lax.psum(lsum, ("x", "y", "c"))

    cidx = my_core.astype(jnp.int32).reshape(1)
    mc = jnp.stack([my_chip, my_core]).astype(jnp.int32)
    gsrc = [_pack_full(g), g["w_in_shards"]]
    blocks = (RB, WB)
    sib = _swap_halves(gsrc)
    part = [_add_halves(cidx, a, b, rb, "add_halves_%d" % i) for i, (a, b, rb) in enumerate(zip(gsrc, sib, blocks))]
    far = _exchange_chips(part)
    red = [_add_chips(mc, a, b, f, rb, "add_chips_%d" % i) for i, (a, b, f, rb) in enumerate(zip(gsrc, sib, far, blocks))]
    got = _share_halves(red)
    g_pk = jnp.concatenate([jnp.where(my_core == 0, red[0], got[0]), jnp.where(my_core == 0, got[0], red[0])], axis=0)
    small = [name for name, _ in SMALL]
    as_row = lambda d: [d[name].reshape(1, -1) for name in small]
    res_sm = _small_update(as_row(g), as_row(w), as_row(m), as_row(v))

    gr = {"w_mod": g_pk[O_MOD:O_OS].reshape(D, R_MOD), "rest": g_pk[O_OS:PUSED]}
    wr, mr, vr = ({"w_mod": t["w_mod"], "rest": _pack_rest(t)} for t in (w, m, v))
    res = {k: _adamw(gr[k], wr[k], mr[k], vr[k], rb, "adamw_" + k)
           for k, rb in (("w_mod", 512), ("rest", (PUSED - O_OS) // 2))}
    gr["w_in"], *res["w_in"] = _adamw_halves(cidx, red[1], got[1], w["w_in"], m["w_in"], v["w_in"], WB, "adamw_w_in")

    outs = []
    for i in range(4):
        pick = (lambda k: gr[k]) if i == 0 else (lambda k: res[k][i - 1])
        d = {"w_mod": pick("w_mod")[None], "w_in": pick("w_in")[None]}
        d.update(_unpack_rest(pick("rest")))
        d.update({name: a.reshape(shapes[name]) for name, a in zip(small, res_sm[i])})
        outs.extend(d[n] for n in WEIGHTS)
    return (loss, grad_x[None], *outs)
```

```python
import functools

import jax
import jax.numpy as jnp
from jax import lax
from jax.experimental import pallas as pl
from jax.experimental.pallas import tpu as pltpu

F32, BF16 = jnp.float32, jnp.bfloat16

D = 1024
DI = 2048
NH = 32
HP = 64
NG = 8
HPG = 4
NS = 128
Q = 128
GW = 64
CK = 31
SK = 4
CTX = 256
EPS = 1e-6
XBC = DI + 2 * NG * NS
SSDW = XBC + 128
RESTW = 7168
T = 256
TX = 512
VMEM_LIMIT = 56 * 1024 * 1024

ADAM_LR, ADAM_B1, ADAM_B2, ADAM_EPS, ADAM_WD, ADAM_STEP = 0.001, 0.9, 0.999, 1e-08, 0.01, 10


def _cp(sem):
    return pltpu.CompilerParams(dimension_semantics=sem, vmem_limit_bytes=VMEM_LIMIT)


def _sig(x):
    return jax.nn.sigmoid(x)


def _silu(x):
    return x * _sig(x)


def _dsilu(x):
    s = _sig(x)
    return s * (1.0 + x * (1.0 - s))


def _dot(a, b):
    return jnp.dot(a, b, preferred_element_type=F32)


def _dot_nt(a, b):
    return lax.dot_general(a, b, (((1,), (1,)), ((), ())), preferred_element_type=F32)


def _split3(x):
    h = x.astype(BF16)
    r = x - h.astype(F32)
    m = r.astype(BF16)
    l = (r - m.astype(F32)).astype(BF16)
    return h, m, l


def _dot3_l(sel, x):
    h, m, l = _split3(x)
    return _dot(sel, h) + _dot(sel, m) + _dot(sel, l)


def _dot3_r(x, sel):
    h, m, l = _split3(x)
    return _dot(h, sel) + _dot(m, sel) + _dot(l, sel)


def _split2(x):
    h = x.astype(BF16)
    return h, (x - h.astype(F32)).astype(BF16)


def _dot2_l(sel, x):
    h, l = _split2(x)
    return _dot(sel, h) + _dot(sel, l)


def _dot2_r(x, sel):
    h, l = _split2(x)
    return _dot(h, sel) + _dot(l, sel)


def _iota(shape, dim):
    return lax.broadcasted_iota(jnp.int32, shape, dim)


def _mm(a, b, dims, m, n, k, bm, bn, bk, out_dtype, name):
    nk = k // bk
    assert m % bm == 0 and n % bn == 0 and k % bk == 0, (name, m, n, k, bm, bn, bk)

    def prod(a_ref, b_ref):
        av = a_ref[...].astype(BF16)
        bv = b_ref[...].astype(BF16)
        if dims == "nn":
            return _dot(av, bv)
        if dims == "nt":
            return _dot_nt(av, bv)
        return lax.dot_general(av, bv, (((0,), (0,)), ((), ())), preferred_element_type=F32)

    def kern_one(a_ref, b_ref, o_ref):
        o_ref[...] = prod(a_ref, b_ref).astype(out_dtype)

    def kern_acc(a_ref, b_ref, o_ref, acc):
        kk = pl.program_id(2)

        @pl.when(kk == 0)
        def _():
            acc[...] = jnp.zeros_like(acc)

        acc[...] += prod(a_ref, b_ref)

        @pl.when(kk == nk - 1)
        def _():
            o_ref[...] = acc[...].astype(out_dtype)

    if dims == "nn":
        a_spec = pl.BlockSpec((bm, bk), lambda j, i, kk: (i, kk))
        b_spec = pl.BlockSpec((bk, bn), lambda j, i, kk: (kk, j))
    elif dims == "nt":
        a_spec = pl.BlockSpec((bm, bk), lambda j, i, kk: (i, kk))
        b_spec = pl.BlockSpec((bn, bk), lambda j, i, kk: (j, kk))
    else:
        a_spec = pl.BlockSpec((bk, bm), lambda j, i, kk: (kk, i))
        b_spec = pl.BlockSpec((bk, bn), lambda j, i, kk: (kk, j))
    return pl.pallas_call(
        kern_one if nk == 1 else kern_acc, name=name,
        grid=(n // bn, m // bm, nk),
        in_specs=[a_spec, b_spec],
        out_specs=pl.BlockSpec((bm, bn), lambda j, i, kk: (i, j)),
        out_shape=jax.ShapeDtypeStruct((m, n), out_dtype),
        scratch_shapes=[] if nk == 1 else [pltpu.VMEM((bm, bn), F32)],
        compiler_params=_cp(("parallel", "parallel", "arbitrary")),
    )(a, b)


def _mod_fwd(cc, w_mod, b_mod):
    def kern(cc_ref, w_ref, b_ref, o_ref):
        s = _silu(cc_ref[...]).astype(BF16)
        o_ref[...] = _dot(s, w_ref[...]) + b_ref[...]

    return pl.pallas_call(
        kern, name="mod_fwd", grid=(3,),
        in_specs=[pl.BlockSpec((8, D), lambda j: (0, 0)), pl.BlockSpec((D, D), lambda j: (0, j)),
                  pl.BlockSpec((1, D), lambda j: (0, j))],
        out_specs=pl.BlockSpec((8, D), lambda j: (0, j)),
        out_shape=jax.ShapeDtypeStruct((8, 3 * D), F32),
        compiler_params=_cp(("parallel",)),
    )(cc, w_mod, b_mod)


def _mod_bwd(dmod, cc, cct, w_mod):
    def kern(dm_ref, cc_ref, cct_ref, w_ref, gw_ref, gb_ref, gc_ref):
        kk = pl.program_id(0)
        dm = dm_ref[...]
        sct = _silu(cct_ref[...])
        gw_ref[...] = sct[:, 0:1] * dm[0:1, :] + sct[:, 1:2] * dm[1:2, :]
        gb_ref[...] = jnp.broadcast_to(dm[0:1, :] + dm[1:2, :], dm.shape)

        @pl.when(kk == 0)
        def _():
            gc_ref[...] = jnp.zeros_like(gc_ref)

        gc_ref[...] += _dot_nt(dm.astype(BF16), w_ref[...])

        @pl.when(kk == 2)
        def _():
            gc_ref[...] = gc_ref[...] * _dsilu(cc_ref[...])

    return pl.pallas_call(
        kern, name="mod_bwd", grid=(3,),
        in_specs=[pl.BlockSpec((8, D), lambda j: (0, j)), pl.BlockSpec((8, D), lambda j: (0, 0)),
                  pl.BlockSpec((D, 8), lambda j: (0, 0)), pl.BlockSpec((D, D), lambda j: (0, j))],
        out_specs=[pl.BlockSpec((D, D), lambda j: (0, j)), pl.BlockSpec((8, D), lambda j: (0, j)),
                   pl.BlockSpec((8, D), lambda j: (0, 0))],
        out_shape=[jax.ShapeDtypeStruct((D, 3 * D), F32), jax.ShapeDtypeStruct((8, 3 * D), F32),
                   jax.ShapeDtypeStruct((8, D), F32)],
        compiler_params=_cp(("arbitrary",)),
    )(dmod, cc, cct, w_mod)


def _ext_specs(nct):
    return (pl.BlockSpec((T, D), lambda i: (jnp.minimum(i, nct - 1), 0)),
            pl.BlockSpec((T, D), lambda i: (jnp.maximum(i - nct, 0), 0)))


def _norm_fwd(ctx, xl, mod, nw, nct):
    lext = ctx.shape[0] + xl.shape[0]

    def kern(c_ref, x_ref, mod_ref, nw_ref, h_ref):
        is_ctx = pl.program_id(0) < nct
        x = jnp.where(is_ctx, c_ref[...], x_ref[...])
        r = lax.rsqrt(jnp.mean(x * x, axis=-1, keepdims=True) + EPS)
        xn = x * r * nw_ref[...]
        shift = jnp.where(is_ctx, mod_ref[1:2, 0:D], mod_ref[0:1, 0:D])
        scale = jnp.where(is_ctx, mod_ref[1:2, D:2 * D], mod_ref[0:1, D:2 * D])
        h_ref[...] = (xn * (1.0 + scale) + shift).astype(BF16)

    return pl.pallas_call(
        kern, name="norm_fwd", grid=(lext // T,),
        in_specs=[*_ext_specs(nct), pl.BlockSpec((8, 3 * D), lambda i: (0, 0)),
                  pl.BlockSpec((1, D), lambda i: (0, 0))],
        out_specs=pl.BlockSpec((T, D), lambda i: (i, 0)),
        out_shape=jax.ShapeDtypeStruct((lext, D), BF16),
        compiler_params=_cp(("parallel",)),
    )(ctx, xl, mod, nw)


def _norm_bwd(dha, dhb, ctx, xl, dx2, mod, nw, nct):
    lext = ctx.shape[0] + xl.shape[0]
    ntl = lext // T

    def kern(dha_ref, dhb_ref, c_ref, x_ref, dx2_ref, mod_ref, nw_ref, gx_ref, gnw_ref, dss_ref):
        i = pl.program_id(0)
        is_ctx = i < nct

        @pl.when(i == 0)
        def _():
            gnw_ref[...] = jnp.zeros_like(gnw_ref)
            dss_ref[...] = jnp.zeros_like(dss_ref)

        x = jnp.where(is_ctx, c_ref[...], x_ref[...])
        dh_ = dha_ref[...].astype(F32) + jnp.where(is_ctx, 0.0, dhb_ref[...].astype(F32))
        nw_ = nw_ref[...]
        r = lax.rsqrt(jnp.mean(x * x, axis=-1, keepdims=True) + EPS)
        xn = x * r
        scale = jnp.where(is_ctx, mod_ref[1:2, D:2 * D], mod_ref[0:1, D:2 * D])
        dsh = jnp.sum(dh_, axis=0, keepdims=True)
        dsc = jnp.sum(dh_ * (xn * nw_), axis=0, keepdims=True)
        row = jnp.concatenate([dsh, dsc], axis=1)
        rid = _iota((8, 2 * D), 0)
        dss_ref[...] += jnp.where(rid == jnp.where(is_ctx, 1, 0), row, 0.0)
        dxnw = dh_ * (1.0 + scale)
        gnw_ref[...] += jnp.broadcast_to(jnp.sum(dxnw * xn, axis=0, keepdims=True), (8, D))
        dxn = dxnw * nw_
        dx = r * (dxn - xn * jnp.mean(dxn * xn, axis=-1, keepdims=True))
        gx_ref[...] = dx2_ref[...] + dx

    return pl.pallas_call(
        kern, name="norm_bwd", grid=(ntl,),
        in_specs=[pl.BlockSpec((T, D), lambda i: (i, 0)), pl.BlockSpec((T, D), lambda i: (jnp.maximum(i - nct, 0), 0)),
                  *_ext_specs(nct),
                  pl.BlockSpec((T, D), lambda i: (jnp.maximum(i - nct, 0), 0)),
                  pl.BlockSpec((8, 3 * D), lambda i: (0, 0)), pl.BlockSpec((1, D), lambda i: (0, 0))],
        out_specs=[pl.BlockSpec((T, D), lambda i: (jnp.maximum(i - nct, 0), 0)),
                   pl.BlockSpec((8, D), lambda i: (0, 0)), pl.BlockSpec((8, 2 * D), lambda i: (0, 0))],
        out_shape=[jax.ShapeDtypeStruct((lext - nct * T, D), F32), jax.ShapeDtypeStruct((8, D), F32),
                   jax.ShapeDtypeStruct((8, 2 * D), F32)],
        compiler_params=_cp(("arbitrary",)),
    )(dha, dhb, ctx, xl, dx2, mod, nw)


CB = 1024


def _halo_specs(width_blk, col_off_blocks, ntl):
    t8 = T // 8
    main = pl.BlockSpec((T, width_blk), lambda j, i: (i, j + col_off_blocks))
    prev = pl.BlockSpec((8, width_blk), lambda j, i: (jnp.maximum(i * t8 - 1, 0), j + col_off_blocks))
    nxt = pl.BlockSpec((8, width_blk), lambda j, i: (jnp.minimum((i + 1) * t8, ntl * t8 - 1), j + col_off_blocks))
    return main, prev, nxt


def _seq_edges(i, nct, ntl):
    starts = jnp.logical_or(i == 0, i == nct)
    ends = jnp.logical_or(i == nct - 1, i == ntl - 1)
    return starts, ends


def _shifted(ext, off):
    n = ext.shape[0]
    return pltpu.roll(ext, (-off) % n, axis=0)[8:8 + T]


def _conv_fwd(proj_ssd, cw, cb, nct):
    lext = proj_ssd.shape[0]
    ntl = lext // T

    def kern(u_ref, up_ref, un_ref, w_ref, b_ref, o_ref, o16_ref):
        i = pl.program_id(1)
        starts, ends = _seq_edges(i, nct, ntl)
        up = jnp.where(starts, 0.0, up_ref[...])
        un = jnp.where(ends, 0.0, un_ref[...])
        ext = jnp.concatenate([up, u_ref[...], un], axis=0)
        w = w_ref[...]
        pre = b_ref[...] + w[0:1] * _shifted(ext, -2) + w[1:2] * _shifted(ext, -1) \
            + w[2:3] * u_ref[...] + w[3:4] * _shifted(ext, 1)
        act = _silu(pre)
        o_ref[...] = act
        o16_ref[...] = act.astype(BF16)

    cbf = 4 * CB
    main, prev, nxt = _halo_specs(cbf, 0, ntl)
    return pl.pallas_call(
        kern, name="conv_fwd", grid=(XBC // cbf, ntl),
        in_specs=[main, prev, nxt, pl.BlockSpec((8, cbf), lambda j, i: (0, j)), pl.BlockSpec((1, cbf), lambda j, i: (0, j))],
        out_specs=[pl.BlockSpec((T, cbf), lambda j, i: (i, j))] * 2,
        out_shape=[jax.ShapeDtypeStruct((lext, XBC), F32), jax.ShapeDtypeStruct((lext, XBC), BF16)],
        compiler_params=_cp(("parallel", "parallel")),
    )(proj_ssd, proj_ssd, proj_ssd, cw, cb)


def _conv_bwd(dpost, proj_ssd, cw, cb, col_off, width, nct, name):
    lext = proj_ssd.shape[0]
    ntl = lext // T
    bw = min(width, 2 * CB)
    assert col_off % bw == 0 and width % bw == 0
    cob = col_off // bw

    def kern(u_ref, up_ref, un_ref, d_ref, dp_ref, dn_ref, w_ref, b_ref, du_ref, gw_ref, gb_ref):
        i = pl.program_id(1)

        @pl.when(i == 0)
        def _():
            gw_ref[...] = jnp.zeros_like(gw_ref)
            gb_ref[...] = jnp.zeros_like(gb_ref)

        starts, ends = _seq_edges(i, nct, ntl)
        ext = jnp.concatenate([jnp.where(starts, 0.0, up_ref[...]), u_ref[...], jnp.where(ends, 0.0, un_ref[...])], axis=0)
        dext = jnp.concatenate([jnp.where(starts, 0.0, dp_ref[...]), d_ref[...], jnp.where(ends, 0.0, dn_ref[...])], axis=0)
        w = w_ref[...]
        n = ext.shape[0]
        pre = b_ref[...] + w[0:1] * pltpu.roll(ext, 2, axis=0) + w[1:2] * pltpu.roll(ext, 1, axis=0) \
            + w[2:3] * ext + w[3:4] * pltpu.roll(ext, n - 1, axis=0)
        dpre = dext * _dsilu(pre)
        dm = dpre[8:8 + T]
        du = w[0:1] * _shifted(dpre, 2) + w[1:2] * _shifted(dpre, 1) + w[2:3] * dm + w[3:4] * _shifted(dpre, -1)
        du_ref[...] = du.astype(BF16)
        g0 = jnp.sum(dm * _shifted(ext, -2), axis=0, keepdims=True)
        g1 = jnp.sum(dm * _shifted(ext, -1), axis=0, keepdims=True)
        g2 = jnp.sum(dm * u_ref[...], axis=0, keepdims=True)
        g3 = jnp.sum(dm * _shifted(ext, 1), axis=0, keepdims=True)
        rid = _iota((8, bw), 0)
        gw_ref[...] += jnp.where(rid == 0, g0, jnp.where(rid == 1, g1, jnp.where(rid == 2, g2, jnp.where(rid == 3, g3, 0.0))))
        gb_ref[...] += jnp.broadcast_to(jnp.sum(dm, axis=0, keepdims=True), (8, bw))

    main, prev, nxt = _halo_specs(bw, cob, ntl)
    dmain, dprev, dnxt = _halo_specs(bw, 0, ntl)
    return pl.pallas_call(
        kern, name=name, grid=(width // bw, ntl),
        in_specs=[main, prev, nxt, dmain, dprev, dnxt,
                  pl.BlockSpec((8, bw), lambda j, i: (0, j + cob)), pl.BlockSpec((1, bw), lambda j, i: (0, j + cob))],
        out_specs=[pl.BlockSpec((T, bw), lambda j, i: (i, j)), pl.BlockSpec((8, bw), lambda j, i: (0, j)),
                   pl.BlockSpec((8, bw), lambda j, i: (0, j))],
        out_shape=[jax.ShapeDtypeStruct((lext, width), BF16), jax.ShapeDtypeStruct((8, width), F32),
                   jax.ShapeDtypeStruct((8, width), F32)],
        compiler_params=_cp(("parallel", "arbitrary")),
    )(proj_ssd, proj_ssd, proj_ssd, dpost, dpost, dpost, cw, cb)


def _tri(lower):
    r, c = _iota((Q, Q), 0), _iota((Q, Q), 1)
    return jnp.where((c <= r) if lower else (c >= r), 1.0, 0.0).astype(BF16)


def _is_bdir_lane(shape):
    ln = _iota(shape, len(shape) - 1)
    return jnp.logical_and(((ln >> 2) & 1) == 1, ln < 64)


def _dt_fwd(proj_ssd, dtb, av):
    lext = proj_ssd.shape[0]

    def kern(p_ref, b_ref, a_ref, dtg_ref, lag_ref, dtt_ref, lat_ref):
        lane = _iota((T, 128), 1)
        raw = p_ref[...] + b_ref[...]
        dt = jnp.where(lane < 64, jnp.maximum(raw, 0.0) + jnp.log1p(jnp.exp(-jnp.abs(raw))), 0.0)
        dta = dt * a_ref[...]
        tl, tu = _tri(True), _tri(False)
        isb = _is_bdir_lane((Q, 128))
        las = []
        for qq in range(T // Q):
            blk = dta[qq * Q:(qq + 1) * Q]
            las.append(jnp.where(isb, _dot3_l(tu, blk), _dot3_l(tl, blk)))
        la = jnp.concatenate(las, axis=0)
        for g in range(NG):
            sh = (128 - 8 * g) % 128
            dtg_ref[g] = jnp.where(lane < 8, pltpu.roll(dt, sh, axis=1) if sh else dt, 0.0)
            lag_ref[g] = jnp.where(lane < 8, pltpu.roll(la, sh, axis=1) if sh else la, 0.0)
        dtt_ref[...] = dt.T[0:64]
        lat_ref[...] = la.T[0:64]

    return pl.pallas_call(
        kern, name="dt_fwd", grid=(lext // T,),
        in_specs=[pl.BlockSpec((T, 128), lambda i: (i, XBC // 128)), pl.BlockSpec((1, 128), lambda i: (0, 0)),
                  pl.BlockSpec((1, 128), lambda i: (0, 0))],
        out_specs=[pl.BlockSpec((NG, T, 128), lambda i: (0, i, 0)), pl.BlockSpec((NG, T, 128), lambda i: (0, i, 0)),
                   pl.BlockSpec((64, T), lambda i: (0, i)), pl.BlockSpec((64, T), lambda i: (0, i))],
        out_shape=[jax.ShapeDtypeStruct((NG, lext, 128), F32), jax.ShapeDtypeStruct((NG, lext, 128), F32),
                   jax.ShapeDtypeStruct((64, lext), F32), jax.ShapeDtypeStruct((64, lext), F32)],
        compiler_params=_cp(("parallel",)),
    )(proj_ssd, dtb, av)


def _dt_bwd(ddtg, proj_ssd, dtb):
    lext = proj_ssd.shape[0]

    def kern(d_ref, p_ref, b_ref, o_ref, gb_ref):
        @pl.when(pl.program_id(0) == 0)
        def _():
            gb_ref[...] = jnp.zeros_like(gb_ref)

        acc = d_ref[0]
        for g in range(1, NG):
            acc = acc + pltpu.roll(d_ref[g], 8 * g, axis=1)
        draw = acc * _sig(p_ref[...] + b_ref[...])
        o_ref[...] = draw.astype(BF16)
        gb_ref[...] += jnp.broadcast_to(jnp.sum(draw, axis=0, keepdims=True), (8, 128))

    return pl.pallas_call(
        kern, name="dt_bwd", grid=(lext // T,),
        in_specs=[pl.BlockSpec((NG, T, 128), lambda i: (0, i, 0)), pl.BlockSpec((T, 128), lambda i: (i, XBC // 128)),
                  pl.BlockSpec((1, 128), lambda i: (0, 0))],
        out_specs=[pl.BlockSpec((T, 128), lambda i: (i, 0)), pl.BlockSpec((8, 128), lambda i: (0, 0))],
        out_shape=[jax.ShapeDtypeStruct((lext, 128), BF16), jax.ShapeDtypeStruct((8, 128), F32)],
        compiler_params=_cp(("arbitrary",)),
    )(ddtg, proj_ssd, dtb)


def _expand_sel(d):
    r, c = _iota((128, 256), 0), _iota((128, 256), 1)
    return jnp.where(r == 4 * d + (c >> 6), 1.0, 0.0).astype(BF16)


def _reduce_sel(d):
    r, c = _iota((256, 128), 0), _iota((256, 128), 1)
    return jnp.where(c == 4 * d + (r >> 6), 1.0, 0.0).astype(BF16)


def _chunk_of_bwd_dir(j, ncc, nc):
    return jnp.where(j < ncc, ncc - 1 - j, nc + ncc - 1 - j)


def _dir_terms(la, dt, d):
    lane = _iota(la.shape, 1)
    mine = jnp.logical_and(lane >= 4 * d, lane < 4 * d + 4)
    la = jnp.where(mine, la, 0.0)
    tot = la[Q - 1:Q] if d == 0 else la[0:1]
    wnd = jnp.exp(tot - la)
    return tot, wnd * jnp.where(mine, dt, 0.0), wnd


def _ssd_state(xbc, dtg, lag, ncc):
    lext = xbc.shape[0]
    nc = lext // Q

    def kern(xf_ref, bf_ref, dtf_ref, laf_ref, xb_ref, bb_ref, dtb_ref, lab_ref, hf_ref, hb_ref, sf, sb):
        @pl.when(pl.program_id(0) == 0)
        def _():
            sf[...] = jnp.zeros_like(sf)
            sb[...] = jnp.zeros_like(sb)

        for d, (x_ref, b_ref, dt_ref, la_ref, h_ref, s) in enumerate(
                ((xf_ref, bf_ref, dtf_ref, laf_ref, hf_ref, sf), (xb_ref, bb_ref, dtb_ref, lab_ref, hb_ref, sb))):
            h_ref[...] = s[...].astype(BF16)
            ex = _expand_sel(d)
            for g in range(NG):
                cols = slice(256 * g, 256 * (g + 1))
                tot, w_end, _ = _dir_terms(la_ref[g], dt_ref[g], d)
                wexp = _dot2_r(w_end, ex)
                dexp = _dot2_r(jnp.broadcast_to(jnp.exp(tot), (8, 128)), ex)[0:1]
                xw = (x_ref[:, cols] * wexp).astype(BF16)
                s[:, cols] = s[:, cols] * dexp + _dot(b_ref[:, 128 * g:128 * (g + 1)].astype(F32).T.astype(BF16), xw)

    cb = functools.partial(_chunk_of_bwd_dir, ncc=ncc, nc=nc)
    sm = lambda f: pl.BlockSpec((NG, Q, 128), lambda j: (0, f(j), 0))
    one = lambda j: j
    return pl.pallas_call(
        kern, name="ssd_state", grid=(nc,),
        in_specs=[pl.BlockSpec((Q, DI), lambda j: (j, 0)), pl.BlockSpec((Q, NG * NS), lambda j: (j, 2)), sm(one), sm(one),
                  pl.BlockSpec((Q, DI), lambda j: (cb(j), 0)), pl.BlockSpec((Q, NG * NS), lambda j: (cb(j), 2)), sm(cb), sm(cb)],
        out_specs=[pl.BlockSpec((None, 128, DI), lambda j: (j, 0, 0)),
                   pl.BlockSpec((None, 128, DI), lambda j: (cb(j), 0, 0))],
        out_shape=[jax.ShapeDtypeStruct((nc, 128, DI), BF16), jax.ShapeDtypeStruct((nc, 128, DI), BF16)],
        scratch_shapes=[pltpu.VMEM((128, DI), F32), pltpu.VMEM((128, DI), F32)],
        compiler_params=_cp(("arbitrary",)),
    )(xbc, xbc, dtg, lag, xbc, xbc, dtg, lag)


def _ssd_out(xbc, dtg, lag, dtt, lat, htf, htb, ncc):
    lext = xbc.shape[0]
    nc = lext // Q
    ncx = nc - ncc

    gps = 8
    li, si = (lambda: _iota((Q, Q), 0)), (lambda: _iota((Q, Q), 1))

    def kern(x_ref, b_ref, c_ref, dtg_ref, lag_ref, dtt_ref, lat_ref, hf_ref, hb_ref, y_ref):
        lane = _iota((Q, 128), 1)
        masks = (li() >= si(), li() <= si())
        for gg in range(gps):
            cols = slice(256 * gg, 256 * (gg + 1))
            cm = c_ref[:, 128 * gg:128 * (gg + 1)]
            xb_ = x_ref[:, cols].astype(BF16)
            s_ = _dot_nt(cm.astype(BF16), b_ref[:, 128 * gg:128 * (gg + 1)].astype(BF16))
            la, dtt_, lat_ = lag_ref[gg], dtt_ref[8 * gg:8 * (gg + 1)], lat_ref[8 * gg:8 * (gg + 1)]
            elam = jnp.exp(la)
            yh = [jnp.zeros((Q, 128), F32), jnp.zeros((Q, 128), F32)]
            for d, h_ref in enumerate((hf_ref, hb_ref)):
                rhs = jnp.concatenate([xb_, h_ref[:, cols].astype(BF16)], axis=0)
                lhs = []
                for r in range(HPG):
                    j = 4 * d + r
                    lm = jnp.where(masks[d], jnp.exp(la[:, j:j + 1] - lat_[j:j + 1, :]), 0.0)
                    w = s_ * lm * dtt_[j:j + 1, :]
                    lhs.append(jnp.concatenate([w, cm * elam[:, j:j + 1]], axis=1).astype(BF16))
                for b in range(HPG // 2):
                    ypair = _dot(jnp.concatenate(lhs[2 * b:2 * b + 2], axis=0), rhs[:, 128 * b:128 * (b + 1)])
                    yh[b] = yh[b] + jnp.where(lane < 64, ypair[0:Q], ypair[Q:2 * Q])
            y_ref[:, cols] = jnp.concatenate(yh, axis=1).astype(BF16)

    nb = NG // gps
    sm = pl.BlockSpec((gps, Q, 128), lambda c, g: (g, c + ncc, 0))
    smt = pl.BlockSpec((8 * gps, Q), lambda c, g: (g, c + ncc))
    st3 = pl.BlockSpec((None, 128, 256 * gps), lambda c, g: (c + ncc, 0, g))
    return pl.pallas_call(
        kern, name="ssd_out", grid=(ncx, nb),
        in_specs=[pl.BlockSpec((Q, 256 * gps), lambda c, g: (c + ncc, g)),
                  pl.BlockSpec((Q, 128 * gps), lambda c, g: (c + ncc, 2 * nb + g)),
                  pl.BlockSpec((Q, 128 * gps), lambda c, g: (c + ncc, 3 * nb + g)), sm, sm, smt, smt, st3, st3],
        out_specs=pl.BlockSpec((Q, 256 * gps), lambda c, g: (c, g)),
        out_shape=jax.ShapeDtypeStruct((ncx * Q, DI), BF16),
        compiler_params=_cp(("parallel", "parallel")),
    )(xbc, xbc, xbc, dtg, lag, dtt, lat, htf, htb)


def _ssd_bwd_state(xbc, dy, lag, ncc):
    lext = xbc.shape[0]
    nc = lext // Q

    def kern(cf_ref, dyf_ref, laf_ref, cb_ref, dyb_ref, lab_ref, df_ref, db_ref, sf, sb):
        @pl.when(pl.program_id(0) == 0)
        def _():
            sf[...] = jnp.zeros_like(sf)
            sb[...] = jnp.zeros_like(sb)

        for d, (c_ref, dy_ref, la_ref, o_ref, s) in enumerate(
                ((cf_ref, dyf_ref, laf_ref, df_ref, sf), (cb_ref, dyb_ref, lab_ref, db_ref, sb))):
            o_ref[...] = s[...].astype(BF16)
            ex = _expand_sel(d)
            for g in range(NG):
                cols = slice(256 * g, 256 * (g + 1))
                la = la_ref[g]
                tot = la[Q - 1:Q] if d == 0 else la[0:1]
                eexp = _dot2_r(jnp.exp(la), ex)
                dexp = _dot2_r(jnp.broadcast_to(jnp.exp(tot), (8, 128)), ex)[0:1]
                dye = (dy_ref[:, cols] * eexp).astype(BF16)
                s[:, cols] = s[:, cols] * dexp + _dot(c_ref[:, 128 * g:128 * (g + 1)].astype(F32).T.astype(BF16), dye)

    cf = lambda j: nc - 1 - j
    cb = lambda j: _chunk_of_bwd_dir(nc - 1 - j, ncc, nc)
    sm = lambda f: pl.BlockSpec((NG, Q, 128), lambda j: (0, f(j), 0))
    return pl.pallas_call(
        kern, name="ssd_bwd_state", grid=(nc,),
        in_specs=[pl.BlockSpec((Q, NG * NS), lambda j: (cf(j), 3)), pl.BlockSpec((Q, DI), lambda j: (cf(j), 0)), sm(cf),
                  pl.BlockSpec((Q, NG * NS), lambda j: (cb(j), 3)), pl.BlockSpec((Q, DI), lambda j: (cb(j), 0)), sm(cb)],
        out_specs=[pl.BlockSpec((None, 128, DI), lambda j: (cf(j), 0, 0)),
                   pl.BlockSpec((None, 128, DI), lambda j: (cb(j), 0, 0))],
        out_shape=[jax.ShapeDtypeStruct((nc, 128, DI), BF16), jax.ShapeDtypeStruct((nc, 128, DI), BF16)],
        scratch_shapes=[pltpu.VMEM((128, DI), F32), pltpu.VMEM((128, DI), F32)],
        compiler_params=_cp(("arbitrary",)),
    )(xbc, dy, lag, xbc, dy, lag)


def _ssd_bwd_out(xbc, dy, dsk, dtg, lag, dtt, lat, htf, htb, dhf, dhb, a_rows):
    lext = xbc.shape[0]
    nc = lext // Q

    gps = 1

    def kern(x_ref, b_ref, c_ref, dy_ref, sk_ref, dtg_ref, lag_ref, dtt_ref, lat_ref, hf_ref, hb_ref, df_ref, db_ref,
             a_ref, dx_ref, dbo_ref, dco_ref, ddt_ref, ga_ref):
        @pl.when(pl.program_id(1) == 0)
        def _():
            ga_ref[...] = jnp.zeros_like(ga_ref)

        for gg in range(gps):
            one_group(gg, x_ref, b_ref, c_ref, dy_ref, sk_ref, dtg_ref, lag_ref, dtt_ref, lat_ref, hf_ref, hb_ref, df_ref,
                      db_ref, a_ref, dx_ref, dbo_ref, dco_ref, ddt_ref, ga_ref)

    def one_group(gg, x_ref, b_ref, c_ref, dy_ref, sk_ref, dtg_ref, lag_ref, dtt_ref, lat_ref, hf_ref, hb_ref, df_ref,
                  db_ref, a_ref, dx_ref, dbo_ref, dco_ref, ddt_ref, ga_ref):
        g = pl.program_id(0) * gps + gg
        cols, cols128 = slice(256 * gg, 256 * (gg + 1)), slice(128 * gg, 128 * (gg + 1))
        x, bm, cm, dy_ = x_ref[:, cols], b_ref[:, cols128], c_ref[:, cols128], dy_ref[:, cols]
        xb_, bb_, cb_, dyb_ = x.astype(BF16), bm.astype(BF16), cm.astype(BF16), dy_.astype(BF16)
        st = _dot_nt(bb_, cb_)
        si, li = _iota((Q, Q), 0), _iota((Q, Q), 1)
        lane = _iota((Q, 256), 1)
        lane128 = _iota((Q, 128), 1)
        row128 = _iota((Q, 128), 0)
        sub = _iota((128, Q), 0)
        la, dt = lag_ref[gg], dtg_ref[gg]
        dtt_, lat_ = dtt_ref[8 * gg:8 * (gg + 1)], lat_ref[8 * gg:8 * (gg + 1)]
        elam = jnp.exp(la)
        dst = jnp.zeros((Q, Q), F32)
        dxh = [jnp.zeros((Q, 128), F32), jnp.zeros((Q, 128), F32)]
        cdir, clam = jnp.zeros((Q, 128), F32), jnp.zeros((Q, 128), F32)
        dba = jnp.zeros((Q, 128), F32)
        dca = jnp.zeros((Q, 128), F32)
        dlam = jnp.zeros((Q, 128), F32)
        ddir = jnp.zeros((Q, 128), F32)
        rows = jnp.zeros((128, Q), F32)
        per_dir = []
        for d, (h_ref, dh_ref) in enumerate(((hf_ref, df_ref), (hb_ref, db_ref))):
            ht, dht = h_ref[:, cols].astype(F32), dh_ref[:, cols].astype(F32)
            htb_, dhtb_ = ht.astype(BF16), dht.astype(BF16)
            tot, w_end, wnd = _dir_terms(la, dt, d)
            ex, rs = _expand_sel(d), _reduce_sel(d)
            elx = _dot2_r(elam, ex)
            wex = _dot2_r(w_end, ex)
            dye = dy_ * elx
            ch = _dot(cb_, htb_)
            bd = _dot(bb_, dhtb_)
            dca = dca + _dot_nt(dye.astype(BF16), htb_)
            dba = dba + _dot_nt((x * wex).astype(BF16), dhtb_)
            dlam = dlam + _dot2_r(dye * ch, rs)
            xbd = _dot2_r(x * bd, rs)
            e_ = w_end * xbd
            dlam = dlam - e_
            ddir = ddir + wnd * xbd
            hh = _dot2_r(jnp.broadcast_to(jnp.sum(dht * ht, axis=0, keepdims=True), (8, 256)), rs)[0:1]
            tot_term = jnp.sum(e_, axis=0, keepdims=True) + jnp.exp(tot) * hh
            dlam = dlam + jnp.where(row128 == (Q - 1 if d == 0 else 0), tot_term, 0.0)
            per_dir.append((w_end, jnp.concatenate([dyb_, dhtb_], axis=0), (li >= si) if d == 0 else (li <= si)))
        for r in range(HPG):
            half = slice(128 * (r // 2), 128 * (r // 2 + 1))
            hm = (lane128 >> 6) == (r % 2)
            dwt = _dot_nt(jnp.where(hm, x[:, half], 0.0).astype(BF16), dyb_[:, half])
            q = dwt * st
            for d, (w_end, rhs, maskt) in enumerate(per_dir):
                j = 4 * d + r
                dc = dt[:, j:j + 1]
                lmt = jnp.where(maskt, jnp.exp(lat_[j:j + 1, :] - la[:, j:j + 1]), 0.0)
                ldc = lmt * jnp.broadcast_to(dc, (Q, Q))
                lhs = jnp.concatenate([st * ldc, bm * w_end[:, j:j + 1]], axis=1).astype(BF16)
                dxh[r // 2] = dxh[r // 2] + jnp.where(hm, _dot(lhs, rhs[:, half]), 0.0)
                cs = jnp.sum(q * lmt, axis=1, keepdims=True)
                cdir = jnp.where(lane128 == j, cs, cdir)
                clam = jnp.where(lane128 == j, cs * dc, clam)
                rows = rows + jnp.where(sub == j, jnp.sum(q * ldc, axis=0, keepdims=True), 0.0)
                dst = dst + dwt * ldc
        dxa = jnp.concatenate(dxh, axis=1)
        ddir = ddir + cdir
        dlam = dlam - clam + rows.T
        dba = dba + _dot(dst.astype(BF16), cb_)
        dca = dca + _dot(dst.T.astype(BF16), bb_)
        isb = jnp.logical_and(lane128 >= 4, lane128 < 8)
        ddel = jnp.where(isb, _dot2_l(_tri(True), dlam), _dot2_l(_tri(False), dlam))
        a_l = a_ref[pl.ds(g, 1), :]
        ddt_ref[gg] = ddir + a_l * ddel
        ga_ref[gg] += jnp.broadcast_to(a_l * jnp.sum(dt * ddel, axis=0, keepdims=True), (8, 128))
        dx_ref[:, cols] = dxa + dy_ * sk_ref[:, cols]
        dbo_ref[:, cols128] = dba
        dco_ref[:, cols128] = dca

    nb = NG // gps
    st3 = pl.BlockSpec((None, 128, 256 * gps), lambda g, c: (c, 0, g))
    sm = pl.BlockSpec((gps, Q, 128), lambda g, c: (g, c, 0))
    smt = pl.BlockSpec((8 * gps, Q), lambda g, c: (g, c))
    wide = pl.BlockSpec((Q, 256 * gps), lambda g, c: (c, g))
    return pl.pallas_call(
        kern, name="ssd_bwd_out", grid=(nb, nc),
        in_specs=[wide, pl.BlockSpec((Q, 128 * gps), lambda g, c: (c, 2 * nb + g)),
                  pl.BlockSpec((Q, 128 * gps), lambda g, c: (c, 3 * nb + g)), wide,
                  pl.BlockSpec((1, 256 * gps), lambda g, c: (0, g)), sm, sm, smt, smt, st3, st3, st3, st3,
                  pl.BlockSpec((8, 128), lambda g, c: (0, 0))],
        out_specs=[wide, pl.BlockSpec((Q, 128 * gps), lambda g, c: (c, g)),
                   pl.BlockSpec((Q, 128 * gps), lambda g, c: (c, g)), sm, pl.BlockSpec((gps, 8, 128), lambda g, c: (g, 0, 0))],
        out_shape=[jax.ShapeDtypeStruct((lext, DI), F32), jax.ShapeDtypeStruct((lext, NG * NS), F32),
                   jax.ShapeDtypeStruct((lext, NG * NS), F32), jax.ShapeDtypeStruct((NG, lext, 128), F32),
                   jax.ShapeDtypeStruct((NG, 8, 128), F32)],
        compiler_params=_cp(("parallel", "arbitrary")),
    )(xbc, xbc, xbc, dy, dsk, dtg, lag, dtt, lat, htf, htb, dhf, dhb, a_rows)


def _post_fwd(yssm, xbc, proj_rest, dsk, gnw, nct):
    l = yssm.shape[0]

    def kern(y_ref, x_ref, z_ref, dsk_ref, w_ref, o_ref):
        y = y_ref[...].astype(F32) + dsk_ref[...] * x_ref[...]
        yz = y * _silu(z_ref[...].astype(F32))
        for g in range(NG):
            sl = slice(256 * g, 256 * (g + 1))
            v = yz[:, sl]
            r = lax.rsqrt(jnp.mean(v * v, axis=-1, keepdims=True) + EPS)
            o_ref[:, sl] = (v * r * w_ref[:, sl]).astype(BF16)

    return pl.pallas_call(
        kern, name="post_fwd", grid=(l // T,),
        in_specs=[pl.BlockSpec((T, DI), lambda i: (i, 0)), pl.BlockSpec((T, DI), lambda i: (i + nct, 0)),
                  pl.BlockSpec((T, DI), lambda i: (i, 0)), pl.BlockSpec((1, DI), lambda i: (0, 0)),
                  pl.BlockSpec((1, DI), lambda i: (0, 0))],
        out_specs=pl.BlockSpec((T, DI), lambda i: (i, 0)),
        out_shape=jax.ShapeDtypeStruct((l, DI), BF16),
        compiler_params=_cp(("parallel",)),
    )(yssm, xbc, proj_rest, dsk, gnw)


def _post_bwd(dgn, yssm, xbc, proj_rest, dsk, gnw, dpr, nct):
    l = yssm.shape[0]
    lext = xbc.shape[0]
    xi = lambda i: (jnp.maximum(i - nct, 0), 0)

    def kern(dg_ref, y_ref, x_ref, z_ref, dsk_ref, w_ref, dpr_ref, dy_ref, dz_ref, gw_ref, gd_ref):
        i = pl.program_id(0)

        @pl.when(i == 0)
        def _():
            gw_ref[...] = jnp.zeros_like(gw_ref)
            gd_ref[...] = jnp.zeros_like(gd_ref)

        @pl.when(i < nct)
        def _():
            dy_ref[...] = jnp.zeros_like(dy_ref)

        @pl.when(i >= nct)
        def _():
            xs = x_ref[...]
            z = z_ref[...].astype(F32)
            y = y_ref[...].astype(F32) + dsk_ref[...] * xs
            sz = _silu(z)
            yz = y * sz
            dgn_ = dg_ref[...].astype(F32)
            dyz_parts = []
            gws = []
            for g in range(NG):
                sl = slice(256 * g, 256 * (g + 1))
                v = yz[:, sl]
                r = lax.rsqrt(jnp.mean(v * v, axis=-1, keepdims=True) + EPS)
                vn = v * r
                dn = dgn_[:, sl] * w_ref[:, sl]
                gws.append(jnp.sum(dgn_[:, sl] * vn, axis=0, keepdims=True))
                dyz_parts.append(r * (dn - vn * jnp.mean(dn * vn, axis=-1, keepdims=True)))
            dyz = jnp.concatenate(dyz_parts, axis=1)
            gw_ref[...] += jnp.broadcast_to(jnp.concatenate(gws, axis=1), (8, DI))
            dy = dyz * sz
            dz_ref[...] = (dyz * y * _dsilu(z)).astype(BF16)
            gd_ref[...] += jnp.broadcast_to(jnp.sum(dy * xs, axis=0, keepdims=True), (8, DI))
            dy_ref[...] = dy

    return pl.pallas_call(
        kern, name="post_bwd", grid=(lext // T,),
        in_specs=[pl.BlockSpec((T, DI), xi), pl.BlockSpec((T, DI), xi), pl.BlockSpec((T, DI), lambda i: (i, 0)),
                  pl.BlockSpec((T, DI), xi), pl.BlockSpec((1, DI), lambda i: (0, 0)), pl.BlockSpec((1, DI), lambda i: (0, 0)),
                  pl.BlockSpec(memory_space=pl.ANY)],
        out_specs=[pl.BlockSpec((T, DI), lambda i: (i, 0)),
                   pl.BlockSpec((T, DI), xi), pl.BlockSpec((8, DI), lambda i: (0, 0)), pl.BlockSpec((8, DI), lambda i: (0, 0))],
        out_shape=[jax.ShapeDtypeStruct((lext, DI), F32),
                   jax.ShapeDtypeStruct((l, RESTW), BF16), jax.ShapeDtypeStruct((8, DI), F32), jax.ShapeDtypeStruct((8, DI), F32)],
        input_output_aliases={6: 1},
        compiler_params=_cp(("arbitrary",)),
    )(dgn, yssm, xbc, proj_rest, dsk, gnw, dpr)


C_G1, C_G2, C_GA, C_GB, C_CG = 2, 3, 4, 5, 6
PITCH = GW + 16
NROW = T // GW


GAP = PITCH - GW
PADR = GAP + NROW * PITCH
NSTRIP = D // 128


def _fill_padded(pad8, val):
    z = jnp.zeros((GAP, D), F32)
    parts = [z]
    for r in range(NROW):
        parts += [val[GW * r:GW * (r + 1)], z]
    p = jnp.concatenate(parts, axis=0)
    pad8[0] = p
    for j in range(1, pad8.shape[0]):
        pad8[j] = pltpu.roll(p, PADR - j, axis=0)


def _tap(pad8, base, off, ln):
    return pad8[off % 8, pl.ds(base + off - off % 8, GW), ln]


def _row_conv(out_ref, pad8, w_ref, transpose):
    def strip(s, carry):
        ln = pl.ds(pl.multiple_of(s * 128, 128), 128)
        for r in range(NROW):
            base = GAP + PITCH * r
            acc = jnp.zeros((GW, 128), F32)
            for k in range(CK):
                off = (k - 15) if not transpose else (15 - k)
                acc = acc + w_ref[pl.ds(k, 1), ln] * _tap(pad8, base, off, ln)
            out_ref[pl.ds(GW * r, GW), ln] = acc
        return carry

    lax.fori_loop(0, NSTRIP, strip, 0)


def _row_conv_wgrad(gcw_ref, padd8, pada8):
    def strip(s, carry):
        ln = pl.ds(pl.multiple_of(s * 128, 128), 128)
        rid = _iota((32, 128), 0)
        g = jnp.zeros((32, 128), F32)
        for k0 in range(0, CK, 8):
            taps = range(k0, min(k0 + 8, CK))
            accs = {k: jnp.zeros((8, 128), F32) for k in taps}
            for r in range(NROW):
                base = GAP + PITCH * r
                d = _tap(padd8, base, 0, ln)
                for k in taps:
                    p = d * pada8[0, pl.ds(base + k - 15, GW), ln]
                    part = p[0:8]
                    for q in range(1, GW // 8):
                        part = part + p[8 * q:8 * (q + 1)]
                    accs[k] = accs[k] + part
            for k in taps:
                g = jnp.where(rid == k, jnp.sum(accs[k], axis=0, keepdims=True), g)
        gcw_ref[:, ln] += g
        return carry

    lax.fori_loop(0, NSTRIP, strip, 0)


def _ln_stats(cv):
    mu = jnp.mean(cv, axis=-1, keepdims=True)
    xc = cv - mu
    rs = lax.rsqrt(jnp.mean(xc * xc, axis=-1, keepdims=True) + EPS)
    return xc * rs, rs


def _conf_fwd(proj_rest, cw, cb, lw, lb):
    l = proj_rest.shape[0]

    def kern(ga_ref, gb_ref, cg_ref, cw_ref, cb_ref, lw_ref, lb_ref, o_ref, cv_ref, pad8):
        _fill_padded(pad8, ga_ref[...].astype(F32) * _sig(gb_ref[...].astype(F32)))
        _row_conv(cv_ref, pad8, cw_ref, False)
        cv = cv_ref[...] + cb_ref[...]
        cv_ref[...] = cv
        xh, _ = _ln_stats(cv)
        ln = xh * lw_ref[...] + lb_ref[...]
        o_ref[...] = (_silu(ln) * _silu(cg_ref[...].astype(F32))).astype(BF16)

    vec = pl.BlockSpec((1, D), lambda i: (0, 0))
    blk = pl.BlockSpec((T, D), lambda i: (i, 0))
    return pl.pallas_call(
        kern, name="conf_fwd", grid=(l // T,),
        in_specs=[pl.BlockSpec((T, D), lambda i: (i, C_GA)), pl.BlockSpec((T, D), lambda i: (i, C_GB)),
                  pl.BlockSpec((T, D), lambda i: (i, C_CG)), pl.BlockSpec((32, D), lambda i: (0, 0)), vec, vec, vec],
        out_specs=[blk, blk],
        out_shape=[jax.ShapeDtypeStruct((l, D), BF16), jax.ShapeDtypeStruct((l, D), F32)],
        scratch_shapes=[pltpu.VMEM((8, PADR, D), F32)],
        compiler_params=_cp(("parallel",)),
    )(proj_rest, proj_rest, proj_rest, cw, cb, lw, lb)


def _conf_bwd(duc, cv, proj_rest, cw, lw, lb, dpr):
    l = proj_rest.shape[0]

    def kern(du_ref, cv_ref, ga_ref, gb_ref, cg_ref, cw_ref, lw_ref, lb_ref, dpr_ref, o_ref, gcw_ref, gv_ref, sc,
             pada, padd, da_ref):
        i, j = pl.program_id(0), pl.program_id(1)

        @pl.when(jnp.logical_and(i == 0, j == 0))
        def _():
            gcw_ref[...] = jnp.zeros_like(gcw_ref)
            gv_ref[...] = jnp.zeros_like(gv_ref)

        @pl.when(j == 0)
        def _():
            ga, gb, cg = ga_ref[...].astype(F32), gb_ref[...].astype(F32), cg_ref[...].astype(F32)
            sg = _sig(gb)
            xh, rs = _ln_stats(cv_ref[...])
            ln = xh * lw_ref[...] + lb_ref[...]
            du = du_ref[...].astype(F32)
            sc[:, 2 * D:3 * D] = (du * _silu(ln) * _dsilu(cg)).astype(BF16)
            dln = du * _silu(cg) * _dsilu(ln)
            g_lw = jnp.sum(dln * xh, axis=0, keepdims=True)
            g_lb = jnp.sum(dln, axis=0, keepdims=True)
            dxh = dln * lw_ref[...]
            dcv = rs * (dxh - jnp.mean(dxh, axis=-1, keepdims=True) - xh * jnp.mean(dxh * xh, axis=-1, keepdims=True))
            g_cb = jnp.sum(dcv, axis=0, keepdims=True)
            rid = _iota((8, D), 0)
            gv_ref[...] += jnp.where(rid == 0, g_cb, jnp.where(rid == 1, g_lw, jnp.where(rid == 2, g_lb, 0.0)))
            _fill_padded(padd, dcv)
            _fill_padded(pada, ga * sg)
            _row_conv(da_ref, padd, cw_ref, True)
            _row_conv_wgrad(gcw_ref, padd, pada)
            da = da_ref[...]
            sc[:, 0:D] = (da * sg).astype(BF16)
            sc[:, D:2 * D] = (da * ga * sg * (1.0 - sg)).astype(BF16)

        o_ref[...] = sc[:, pl.ds(pl.multiple_of(j * D, 128), D)]

    vec = pl.BlockSpec((1, D), lambda i, j: (0, 0))
    col = lambda c: pl.BlockSpec((T, D), lambda i, j: (i, c))
    return pl.pallas_call(
        kern, name="conf_bwd", grid=(l // T, 3),
        in_specs=[col(0), col(0), col(C_GA), col(C_GB), col(C_CG), pl.BlockSpec((32, D), lambda i, j: (0, 0)), vec, vec,
                  pl.BlockSpec(memory_space=pl.ANY)],
        out_specs=[pl.BlockSpec((T, D), lambda i, j: (i, C_GA + j)), pl.BlockSpec((32, D), lambda i, j: (0, 0)),
                   pl.BlockSpec((8, D), lambda i, j: (0, 0))],
        out_shape=[jax.ShapeDtypeStruct((l, RESTW), BF16), jax.ShapeDtypeStruct((32, D), F32),
                   jax.ShapeDtypeStruct((8, D), F32)],
        scratch_shapes=[pltpu.VMEM((T, 3 * D), BF16), pltpu.VMEM((1, PADR, D), F32), pltpu.VMEM((8, PADR, D), F32),
                        pltpu.VMEM((T, D), F32)],
        input_output_aliases={8: 0},
        compiler_params=_cp(("arbitrary", "arbitrary")),
    )(duc, cv, proj_rest, proj_rest, proj_rest, cw, lw, lb, dpr)


def _merge_fwd(bs, bc, proj_rest):
    l = bs.shape[0]

    def kern(bs_ref, bc_ref, g1_ref, g2_ref, o_ref):
        up = lambda r: r[...].astype(F32)
        o_ref[...] = (_sig(up(g1_ref)) * up(bs_ref) + _sig(up(g2_ref)) * up(bc_ref)).astype(BF16)

    blk = pl.BlockSpec((TX, D), lambda i: (i, 0))
    return pl.pallas_call(
        kern, name="merge_fwd", grid=(l // TX,),
        in_specs=[blk, blk, pl.BlockSpec((TX, D), lambda i: (i, C_G1)), pl.BlockSpec((TX, D), lambda i: (i, C_G2))],
        out_specs=blk, out_shape=jax.ShapeDtypeStruct((l, D), BF16),
        compiler_params=_cp(("parallel",)),
    )(bs, bc, proj_rest, proj_rest)


def _merge_bwd(dm, bs, bc, proj_rest):
    l = bs.shape[0]

    def kern(dm_ref, bs_ref, bc_ref, g1_ref, g2_ref, dbs_ref, dbc_ref, dg_ref):
        up = lambda r: r[...].astype(F32)
        dm_ = up(dm_ref)
        s1, s2 = _sig(up(g1_ref)), _sig(up(g2_ref))
        dbs_ref[...] = (dm_ * s1).astype(BF16)
        dbc_ref[...] = (dm_ * s2).astype(BF16)
        dg_ref[:, 0:D] = (dm_ * up(bs_ref) * s1 * (1.0 - s1)).astype(BF16)
        dg_ref[:, D:2 * D] = (dm_ * up(bc_ref) * s2 * (1.0 - s2)).astype(BF16)

    blk = pl.BlockSpec((TX, D), lambda i: (i, 0))
    return pl.pallas_call(
        kern, name="merge_bwd", grid=(l // TX,),
        in_specs=[blk, blk, blk, pl.BlockSpec((TX, D), lambda i: (i, C_G1)), pl.BlockSpec((TX, D), lambda i: (i, C_G2))],
        out_specs=[blk, blk, pl.BlockSpec((TX, 2 * D), lambda i: (i, 1))],
        out_shape=[jax.ShapeDtypeStruct((l, D), BF16), jax.ShapeDtypeStruct((l, D), BF16),
                   jax.ShapeDtypeStruct((l, RESTW), BF16)],
        compiler_params=_cp(("parallel",)),
    )(dm, bs, bc, proj_rest, proj_rest)


def _final(x, out, tgt, mod, fw):
    l = x.shape[0]

    def kern(x_ref, o_ref, t_ref, mod_ref, fw_ref, ls_ref, dx2_ref, do_ref, gv_ref):
        @pl.when(pl.program_id(0) == 0)
        def _():
            ls_ref[...] = jnp.zeros_like(ls_ref)
            gv_ref[...] = jnp.zeros_like(gv_ref)

        gate = mod_ref[0:1, 2 * D:3 * D]
        o = o_ref[...]
        x2 = x_ref[...] + gate * o
        r = lax.rsqrt(jnp.mean(x2 * x2, axis=-1, keepdims=True) + EPS)
        yn = x2 * r
        fw_ = fw_ref[...]
        e = yn * fw_ - t_ref[...]
        ls_ref[...] += jnp.full((8, 128), 1.0, F32) * (0.5 / D) * jnp.sum(e * e)
        dy = e * (1.0 / D)
        g_fw = jnp.sum(dy * yn, axis=0, keepdims=True)
        dyn = dy * fw_
        dx2 = r * (dyn - yn * jnp.mean(dyn * yn, axis=-1, keepdims=True))
        g_gate = jnp.sum(dx2 * o, axis=0, keepdims=True)
        rid = _iota((8, D), 0)
        gv_ref[...] += jnp.where(rid == 0, g_fw, jnp.where(rid == 1, g_gate, 0.0))
        dx2_ref[...] = dx2
        do_ref[...] = (dx2 * gate).astype(BF16)

    blk = pl.BlockSpec((TX, D), lambda i: (i, 0))
    return pl.pallas_call(
        kern, name="final", grid=(l // TX,),
        in_specs=[blk, blk, blk, pl.BlockSpec((8, 3 * D), lambda i: (0, 0)), pl.BlockSpec((1, D), lambda i: (0, 0))],
        out_specs=[pl.BlockSpec((8, 128), lambda i: (0, 0)), blk, blk, pl.BlockSpec((8, D), lambda i: (0, 0))],
        out_shape=[jax.ShapeDtypeStruct((8, 128), F32), jax.ShapeDtypeStruct((l, D), F32),
                   jax.ShapeDtypeStruct((l, D), BF16), jax.ShapeDtypeStruct((8, D), F32)],
        compiler_params=_cp(("arbitrary",)),
    )(x, out, tgt, mod, fw)


def _perm_dt_cols(w):
    s = w.shape[:-1]
    return w.reshape(*s, 2, NG, HPG).swapaxes(-3, -2).reshape(*s, 64)


def _unperm_dt_cols(w):
    s = w.shape[:-1]
    return w.reshape(*s, NG, 2, HPG).swapaxes(-3, -2).reshape(*s, 64)


def _pad_lanes(v, width):
    return jnp.pad(v, ((0, 0), (0, width - v.shape[1])))


def _vcols(segs, a, b):
    parts, off = [], 0
    for s in segs:
        lo, hi = max(a, off), min(b, off + s.shape[1])
        if lo < hi:
            parts.append(s[:, lo - off:hi - off])
        off += s.shape[1]
    return parts[0] if len(parts) == 1 else jnp.concatenate(parts, axis=1)


def _local_step(x, c, ctx, tgt, w):
    l = x.shape[0]
    nct = CTX // T
    ncc = CTX // Q
    lext = l + CTX

    w_mod = w["w_mod"].astype(BF16)
    wsegs = [s.astype(BF16) for s in (w["w_in"] if isinstance(w["w_in"], (list, tuple)) else [w["w_in"]])]
    w_ssd = jnp.concatenate([_vcols(wsegs, 0, XBC), _perm_dt_cols(_vcols(wsegs, XBC, XBC + 64)), jnp.zeros((D, 64), BF16)], axis=1)
    r0 = XBC + 64
    w_rest = jnp.concatenate([_vcols(wsegs, r0, r0 + DI), _vcols(wsegs, r0 + DI + 3 * D, r0 + RESTW),
                              _vcols(wsegs, r0 + DI, r0 + DI + 3 * D)], axis=1)
    w_os, w_oc, w_o = w["w_out_ssm"].astype(BF16), w["w_out_conf"].astype(BF16), w["w_out"].astype(BF16)
    cw8 = jnp.pad(w["ssm_conv_w"], ((0, 4), (0, 0)))
    cb_s = w["ssm_conv_b"].reshape(1, XBC)
    dtb = _pad_lanes(_perm_dt_cols(w["dt_bias"].reshape(1, 64)), 128)
    a_all = -jnp.exp(w["a_log"].reshape(1, 64))
    a_perm = _pad_lanes(_perm_dt_cols(a_all), 128)
    a_rows = _pad_lanes(_perm_dt_cols(a_all).reshape(NG, 8), 128)
    dsk = jnp.repeat(w["d_skip"].reshape(NH), HP).reshape(1, DI)
    gnw = w["ssm_norm_w"].reshape(1, DI)
    ccw = jnp.pad(w["conf_conv_w"], ((0, 1), (0, 0)))
    ccb, clw, clb = w["conf_conv_b"].reshape(1, D), w["conf_ln_w"].reshape(1, D), w["conf_ln_b"].reshape(1, D)
    nw = w["norm_w"].reshape(1, D)
    fw = w["final_norm_w"].reshape(1, D)
    cc = jnp.concatenate([c.reshape(1, D), w["c_ctx"].reshape(1, D), jnp.zeros((6, D), F32)], axis=0)

    bx = min(1024, l)
    be = 768 if lext % 768 == 0 else 256
    tk = min(1024, l)
    mod = _mod_fwd(cc, w_mod, w["b_mod"].reshape(1, 3 * D))
    h = _norm_fwd(ctx, x, mod, nw, nct)
    hx = h[CTX:]
    proj_ssd = _mm(h, w_ssd, "nn", lext, SSDW, D, be, SSDW // 3, D, F32, "proj_ssd")
    proj_rest = _mm(hx, w_rest, "nn", l, RESTW, D, bx, 1024, D, BF16, "proj_rest")
    xbc, xbc16 = _conv_fwd(proj_ssd, cw8, cb_s, nct)
    dtg, lag, dtt, lat = _dt_fwd(proj_ssd, dtb, a_perm)
    htf, htb = _ssd_state(xbc16, dtg, lag, ncc)
    yssm = _ssd_out(xbc16, dtg, lag, dtt, lat, htf, htb, ncc)
    gn = _post_fwd(yssm, xbc, proj_rest, dsk, gnw, nct)
    bs = _mm(gn, w_os, "nn", l, D, DI, bx, D, DI, BF16, "out_ssm")
    uc, cv = _conf_fwd(proj_rest, ccw, ccb, clw, clb)
    bc = _mm(uc, w_oc, "nn", l, D, D, bx, D, D, BF16, "out_conf")
    merged = _merge_fwd(bs, bc, proj_rest)
    out = _mm(merged, w_o, "nn", l, D, D, bx, D, D, F32, "out_proj")
    lsum, dx2, dout, gv_fin = _final(x, out, tgt, mod, fw)

    g = {}
    g["final_norm_w"] = gv_fin[0]
    dmerged = _mm(dout, w_o, "nt", l, D, D, bx, D, D, BF16, "d_merged")
    g["w_out"] = _mm(merged, dout, "tn", D, D, l, D, D, tk, F32, "g_w_out")
    dbs, dbc, dpr = _merge_bwd(dmerged, bs, bc, proj_rest)
    dgn = _mm(dbs, w_os, "nt", l, DI, D, bx, DI, D, BF16, "d_gn")
    g["w_out_ssm"] = _mm(gn, dbs, "tn", DI, D, l, DI, D, tk, F32, "g_w_out_ssm")
    duc = _mm(dbc, w_oc, "nt", l, D, D, bx, D, D, BF16, "d_uc")
    g["w_out_conf"] = _mm(uc, dbc, "tn", D, D, l, D, D, tk, F32, "g_w_out_conf")
    dpr, gcw, gv_conf = _conf_bwd(duc, cv, proj_rest, ccw, clw, clb, dpr)
    g["conf_conv_w"] = gcw[:CK]
    g["conf_conv_b"], g["conf_ln_w"], g["conf_ln_b"] = gv_conf[0], gv_conf[1], gv_conf[2]
    dy, dproj_rest, ggnw, gdsk = _post_bwd(dgn, yssm, xbc, proj_rest, dsk, gnw, dpr, nct)
    g["ssm_norm_w"] = ggnw[0]
    g["d_skip"] = gdsk[0].reshape(NH, HP).sum(axis=1)
    dhf, dhb = _ssd_bwd_state(xbc16, dy, lag, ncc)
    dxs, dbm, dcm, ddtg, galog = _ssd_bwd_out(xbc, dy, dsk, dtg, lag, dtt, lat, htf, htb, dhf, dhb, a_rows)
    g["a_log"] = _unperm_dt_cols(galog[:, 0, 0:8].reshape(1, 64)).reshape(2, NH)
    dus, gws, gbs = [], [], []
    for dpost, off, width, nm in ((dxs, 0, DI, "conv_bwd_x"), (dbm, DI, NG * NS, "conv_bwd_b"), (dcm, DI + NG * NS, NG * NS, "conv_bwd_c")):
        du_, gw_, gb_ = _conv_bwd(dpost, proj_ssd, cw8, cb_s, off, width, nct, nm)
        dus.append(du_)
        gws.append(gw_[:SK])
        gbs.append(gb_[0])
    g["ssm_conv_w"] = jnp.concatenate(gws, axis=1)
    g["ssm_conv_b"] = jnp.concatenate(gbs, axis=0)
    ddt_raw, gdtb = _dt_bwd(ddtg, proj_ssd, dtb)
    g["dt_bias"] = _unperm_dt_cols(gdtb[0:1, 0:64]).reshape(2, NH)
    dproj_ssd = jnp.concatenate(dus + [ddt_raw], axis=1)
    gw_ssd = _mm(h, dproj_ssd, "tn", D, SSDW, lext, D, SSDW // 3, be, F32, "g_w_ssd")
    gw_rest = _mm(hx, dproj_rest, "tn", D, RESTW, l, D, 1024, tk, F32, "g_w_rest")
    gsegs = [gw_ssd[:, :XBC], _unperm_dt_cols(gw_ssd[:, XBC:XBC + 64]), gw_rest[:, :DI], gw_rest[:, 2 * DI:],
             gw_rest[:, DI:2 * DI]]
    g["w_in"] = jnp.concatenate(gsegs, axis=1)
    g["w_in_shards"] = jnp.stack([_vcols(gsegs, R_IN * s, R_IN * (s + 1)) for s in range(NSHARD)])
    dh_a = _mm(dproj_ssd, w_ssd, "nt", lext, D, SSDW, T, D, SSDW, BF16, "dh_ssd")
    dh_b = _mm(dproj_rest, w_rest, "nt", l, D, RESTW, T, D, RESTW, BF16, "dh_rest")
    grad_x, gnw_in, dss = _norm_bwd(dh_a, dh_b, ctx, x, dx2, mod, nw, nct)
    g["norm_w"] = gnw_in[0]
    dmod = jnp.concatenate([jnp.concatenate([dss[0:1], gv_fin[1:2]], axis=1),
                            jnp.concatenate([dss[1:2], jnp.zeros((1, D), F32)], axis=1),
                            jnp.zeros((6, 3 * D), F32)], axis=0)
    gwm, gbm, gcc = _mod_bwd(dmod, cc, cc.T, w_mod)
    g["w_mod"], g["b_mod"], g["c_ctx"] = gwm, gbm[0], gcc[1]
    return lsum[0, 0], grad_x, g


NSHARD = 4
R_MOD, R_IN, R_OS, R_OC, R_O, R_SC, R_CC = 768, 2832, 512, 256, 256, 8, 8
O_MOD = 0
O_OS = O_MOD + R_MOD
O_OC = O_OS + R_OS
O_O = O_OC + R_OC
O_SC = O_O + R_O
O_CC = O_SC + R_SC
PUSED = O_CC + R_CC
PROWS = 1824
HALF = PROWS // 2
RB = HALF // 3
WB = 128
SROWS = 16
SMALL = (("b_mod", 3 * D), ("norm_w", D), ("ssm_conv_b", XBC), ("dt_bias", 64), ("a_log", 64), ("d_skip", NH),
         ("ssm_norm_w", DI), ("conf_conv_b", D), ("conf_ln_w", D), ("conf_ln_b", D), ("final_norm_w", D), ("c_ctx", D))


def _pack_shard(s):
    return jnp.concatenate([s["w_mod"].reshape(R_MOD, D), _pack_rest(s), jnp.zeros((PROWS - PUSED, D), F32)], axis=0)


def _pack_rest(s):
    cc = jnp.pad(s["conf_conv_w"].reshape(1, CK * 256), ((0, 0), (0, R_CC * D - CK * 256))).reshape(R_CC, D)
    return jnp.concatenate([s["w_out_ssm"], s["w_out_conf"], s["w_out"],
                            jnp.pad(s["ssm_conv_w"], ((0, R_SC - SK), (0, 0))), cc], axis=0)


def _unpack_rest(p):
    o = lambda r: r - O_OS
    return {"w_out_ssm": p[o(O_OS):o(O_OC)][None], "w_out_conf": p[o(O_OC):o(O_O)][None], "w_out": p[o(O_O):o(O_SC)][None],
            "ssm_conv_w": p[o(O_SC):o(O_SC) + SK][None],
            "conf_conv_w": p[o(O_CC):o(O_CC) + R_CC].reshape(R_CC * D)[:CK * 256].reshape(1, CK, 256)}


def _shard_cols(a, n):
    return a.reshape(a.shape[0], NSHARD, n).transpose(1, 0, 2)


def _pack_full(g):
    cc = jnp.pad(_shard_cols(g["conf_conv_w"], 256).reshape(NSHARD, CK * 256), ((0, 0), (0, R_CC * D - CK * 256)))
    return jnp.concatenate([_shard_cols(g["w_mod"], R_MOD).reshape(NSHARD, R_MOD, D),
                            g["w_out_ssm"].reshape(NSHARD, R_OS, D), g["w_out_conf"].reshape(NSHARD, R_OC, D),
                            g["w_out"].reshape(NSHARD, R_O, D),
                            jnp.pad(_shard_cols(g["ssm_conv_w"], D), ((0, 0), (0, R_SC - SK), (0, 0))),
                            cc.reshape(NSHARD, R_CC, D), jnp.zeros((NSHARD, PROWS - PUSED, D), F32)], axis=1)


def _unpack_gathered(gm, gw, gs):
    def cols(a, r, n):
        return a.reshape(NSHARD, r, n).transpose(1, 0, 2).reshape(r, NSHARD * n)
    return {"w_mod": cols(gm[:, O_MOD:O_OS], D, R_MOD), "w_in": [gw[s] for s in range(NSHARD)],
            "w_out_ssm": gm[:, O_OS:O_OC].reshape(DI, D), "w_out_conf": gm[:, O_OC:O_O].reshape(D, D),
            "w_out": gm[:, O_O:O_SC].reshape(D, D), "ssm_conv_w": cols(gs[:, 0:SK], SK, D),
            "conf_conv_w": cols(gs[:, R_SC:R_SC + R_CC].reshape(NSHARD, R_CC * D)[:, :CK * 256], CK, 256)}


MESH_ID = pl.DeviceIdType.MESH
ANY = pl.BlockSpec(memory_space=pl.ANY)


def _place():
    x, y, c = lax.axis_index("x"), lax.axis_index("y"), lax.axis_index("c")
    return x, y, c, [(1 - x, y), (x, 1 - y), (1 - x, 1 - y)]


def _rcopy(src, dst, send, recv, dev):
    return pltpu.make_async_remote_copy(src_ref=src, dst_ref=dst, send_sem=send, recv_sem=recv,
                                        device_id=dev, device_id_type=MESH_ID)


def _gather_weights(mats, small):
    n = len(mats)

    def kern(*refs):
        m_refs, s_ref, g_refs, gs_ref, (send, recv) = refs[:n], refs[n], refs[n + 1:2 * n + 1], refs[2 * n + 1], refs[2 * n + 2:]
        x, y, c, chips = _place()
        me = 2 * x + y
        sib = (x, y, 1 - c)
        first, passed = [], []
        for k, (px, py) in enumerate(chips):
            first.append(_rcopy(s_ref, gs_ref.at[me], send.at[k], recv.at[k], (px, py, c)))
            for a, (m_ref, g_ref) in enumerate(zip(m_refs, g_refs)):
                mine = _half_rows(c, m_ref.shape[0])
                first.append(_rcopy(m_ref.at[mine], g_ref.at[me, mine], send.at[3 + 6 * a + k], recv.at[3 + 6 * a + k], (px, py, c)))
        for cp in first:
            cp.start()
        for k, (px, py) in enumerate(chips):
            s = 2 * px + py
            for a, (m_ref, g_ref) in enumerate(zip(m_refs, g_refs)):
                mine = _half_rows(c, m_ref.shape[0])
                _rcopy(m_ref.at[mine], g_ref.at[s, mine], send.at[3 + 6 * a + k], recv.at[3 + 6 * a + k], sib).wait_recv()
                f = _rcopy(g_ref.at[s, mine], g_ref.at[s, mine], send.at[6 + 6 * a + k], recv.at[6 + 6 * a + k], sib)
                f.start()
                passed.append(f)
        for k, (px, py) in enumerate(chips):
            s = 2 * px + py
            _rcopy(s_ref, gs_ref.at[s], send.at[k], recv.at[k], sib).wait_recv()
            for a, g_ref in enumerate(g_refs):
                other = _half_rows(1 - c, g_ref.shape[1])
                _rcopy(g_ref.at[s, other], g_ref.at[s, other], send.at[6 + 6 * a + k], recv.at[6 + 6 * a + k], sib).wait_recv()
        for cp in first + passed:
            cp.wait_send()

    nsem = 3 + 6 * n
    return pl.pallas_call(
        kern, name="gather_weights", in_specs=[ANY] * (n + 1), out_specs=[ANY] * (n + 1),
        out_shape=[jax.ShapeDtypeStruct((NSHARD,) + m.shape, m.dtype) for m in mats]
        + [jax.ShapeDtypeStruct((NSHARD, SROWS, D), F32)],
        scratch_shapes=[pltpu.SemaphoreType.DMA((nsem,)), pltpu.SemaphoreType.DMA((nsem,))],
    )(*mats, small)


def _half_rows(c, rows):
    return pl.ds(pl.multiple_of(c * (rows // 2), 16), rows // 2)


def _swap_halves(gs):
    n = len(gs)

    def kern(*refs):
        g_refs, o_refs, (send, recv) = refs[:n], refs[n:2 * n], refs[2 * n:]
        x, y, c, _ = _place()
        cps = [_rcopy(g_ref.at[s, _half_rows(1 - c, g_ref.shape[1])], o_ref.at[s], send.at[NSHARD * a + s],
                      recv.at[NSHARD * a + s], (x, y, 1 - c))
               for a, (g_ref, o_ref) in enumerate(zip(g_refs, o_refs)) for s in range(NSHARD)]
        for cp in cps:
            cp.start()
        for cp in cps:
            cp.wait()

    return pl.pallas_call(
        kern, name="swap_halves", in_specs=[ANY] * n, out_specs=[ANY] * n,
        out_shape=[jax.ShapeDtypeStruct((NSHARD, g.shape[1] // 2, g.shape[2]), F32) for g in gs],
        scratch_shapes=[pltpu.SemaphoreType.DMA((NSHARD * n,)), pltpu.SemaphoreType.DMA((NSHARD * n,))],
    )(*gs)


def _add_halves(cidx, g, ra, rb, name):
    _, half, cols = ra.shape
    nb = half // rb

    def kern(c_ref, g_ref, a_ref, o_ref):
        o_ref[...] = (g_ref[...] + a_ref[...]).astype(BF16)

    return pl.pallas_call(
        kern, name=name,
        grid_spec=pltpu.PrefetchScalarGridSpec(
            num_scalar_prefetch=1, grid=(NSHARD, nb),
            in_specs=[pl.BlockSpec((None, rb, cols), lambda s, i, c: (s, c[0] * nb + i, 0)),
                      pl.BlockSpec((None, rb, cols), lambda s, i, c: (s, i, 0))],
            out_specs=pl.BlockSpec((None, rb, cols), lambda s, i, c: (s, i, 0))),
        out_shape=jax.ShapeDtypeStruct((NSHARD, half, cols), BF16),
        compiler_params=_cp(("parallel", "parallel")),
    )(cidx, g, ra)


def _exchange_chips(ps):
    n = len(ps)

    def kern(*refs):
        p_refs, o_refs, (send, recv) = refs[:n], refs[n:2 * n], refs[2 * n:]
        x, y, c, chips = _place()
        cps = [_rcopy(p_ref.at[2 * px + py], o_ref.at[k], send.at[3 * a + k], recv.at[3 * a + k], (px, py, c))
               for a, (p_ref, o_ref) in enumerate(zip(p_refs, o_refs)) for k, (px, py) in enumerate(chips)]
        for cp in cps:
            cp.start()
        for cp in cps:
            cp.wait()

    return pl.pallas_call(
        kern, name="exchange_chips", in_specs=[ANY] * n, out_specs=[ANY] * n,
        out_shape=[jax.ShapeDtypeStruct((3,) + p.shape[1:], p.dtype) for p in ps],
        scratch_shapes=[pltpu.SemaphoreType.DMA((3 * n,)), pltpu.SemaphoreType.DMA((3 * n,))],
    )(*ps)


def _add_chips(mc, g, ra, rx, rb, name):
    _, half, cols = ra.shape
    nb = half // rb

    def kern(m_ref, g_ref, a_ref, r0_ref, r1_ref, r2_ref, o_ref):
        own = g_ref[...] + a_ref[...]
        o_ref[...] = ((own + r0_ref[...].astype(F32)) + r1_ref[...].astype(F32)) + r2_ref[...].astype(F32)

    return pl.pallas_call(
        kern, name=name,
        grid_spec=pltpu.PrefetchScalarGridSpec(
            num_scalar_prefetch=1, grid=(nb,),
            in_specs=[pl.BlockSpec((None, rb, cols), lambda i, m: (m[0], m[1] * nb + i, 0)),
                      pl.BlockSpec((None, rb, cols), lambda i, m: (m[0], i, 0))]
            + [pl.BlockSpec((None, rb, cols), functools.partial(lambda i, m, k: (k, i, 0), k=k)) for k in range(3)],
            out_specs=pl.BlockSpec((rb, cols), lambda i, m: (i, 0))),
        out_shape=jax.ShapeDtypeStruct((half, cols), F32),
        compiler_params=_cp(("parallel",)),
    )(mc, g, ra, rx, rx, rx)


def _share_halves(rs):
    n = len(rs)

    def kern(*refs):
        r_refs, o_refs, (send, recv) = refs[:n], refs[n:2 * n], refs[2 * n:]
        x, y, c, _ = _place()
        cps = [_rcopy(r_ref, o_ref, send.at[a], recv.at[a], (x, y, 1 - c))
               for a, (r_ref, o_ref) in enumerate(zip(r_refs, o_refs))]
        for cp in cps:
            cp.start()
        for cp in cps:
            cp.wait()

    return pl.pallas_call(
        kern, name="share_halves", in_specs=[ANY] * n, out_specs=[ANY] * n,
        out_shape=[jax.ShapeDtypeStruct(r.shape, F32) for r in rs],
        scratch_shapes=[pltpu.SemaphoreType.DMA((n,)), pltpu.SemaphoreType.DMA((n,))],
    )(*rs)


SMALL_W = XBC


def _small_update(gs, ws, ms, vs):
    n = len(gs)
    widths = [g.shape[1] for g in gs]
    assert n <= SROWS and max(widths) <= SMALL_W

    def kern(*refs):
        g_refs, w_refs, m_refs, v_refs = (refs[n * i:n * (i + 1)] for i in range(4))
        o_g, o_d, o_m, o_v = (refs[n * (4 + i):n * (5 + i)] for i in range(4))
        buf, send, recv = refs[8 * n:]
        x, y, c, _ = _place()
        me = 4 * x + 2 * y + c
        buf[me] = jnp.zeros((SROWS, SMALL_W), F32)
        for k, g_ref in enumerate(g_refs):
            buf[me, k:k + 1, 0:widths[k]] = g_ref[...]
        cps = []
        for r in range(1, 8):
            peer = (1 - x if r & 4 else x, 1 - y if r & 2 else y, 1 - c if r & 1 else c)
            cps.append(_rcopy(buf.at[me], buf.at[me], send.at[r - 1], recv.at[r - 1], peer))
        for cp in cps:
            cp.start()
        for cp in cps:
            cp.wait()
        acc = buf[0]
        for i in range(1, 8):
            acc = acc + buf[i]
        for k in range(n):
            g_ = acc[k:k + 1, 0:widths[k]]
            m_ = ADAM_B1 * m_refs[k][...] + (1.0 - ADAM_B1) * g_
            v_ = ADAM_B2 * v_refs[k][...] + (1.0 - ADAM_B2) * jnp.square(g_)
            m_hat = m_ / (1.0 - ADAM_B1 ** ADAM_STEP)
            v_hat = v_ / (1.0 - ADAM_B2 ** ADAM_STEP)
            o_g[k][...] = g_
            o_d[k][...] = -ADAM_LR * (m_hat / (jnp.sqrt(v_hat) + ADAM_EPS) + ADAM_WD * w_refs[k][...])
            o_m[k][...] = m_
            o_v[k][...] = v_

    vm = pl.BlockSpec(memory_space=pltpu.VMEM)
    outs = pl.pallas_call(
        kern, name="small_update", in_specs=[vm] * (4 * n), out_specs=[vm] * (4 * n),
        out_shape=[jax.ShapeDtypeStruct((1, wd), F32) for _ in range(4) for wd in widths],
        scratch_shapes=[pltpu.VMEM((8, SROWS, SMALL_W), F32), pltpu.SemaphoreType.DMA((7,)), pltpu.SemaphoreType.DMA((7,))],
    )(*gs, *ws, *ms, *vs)
    return [outs[n * i:n * (i + 1)] for i in range(4)]


def _adamw(g, w, m, v, rb, name):
    rows, cols = g.shape

    def kern(g_ref, w_ref, m_ref, v_ref, d_ref, nm_ref, nv_ref):
        g_ = g_ref[...]
        m_ = ADAM_B1 * m_ref[...] + (1.0 - ADAM_B1) * g_
        v_ = ADAM_B2 * v_ref[...] + (1.0 - ADAM_B2) * jnp.square(g_)
        m_hat = m_ / (1.0 - ADAM_B1 ** ADAM_STEP)
        v_hat = v_ / (1.0 - ADAM_B2 ** ADAM_STEP)
        d_ref[...] = -ADAM_LR * (m_hat / (jnp.sqrt(v_hat) + ADAM_EPS) + ADAM_WD * w_ref[...])
        nm_ref[...] = m_
        nv_ref[...] = v_

    assert rows % rb == 0
    blk = pl.BlockSpec((rb, cols), lambda i: (i, 0))
    return pl.pallas_call(
        kern, name=name, grid=(rows // rb,), in_specs=[blk] * 4, out_specs=[blk] * 3,
        out_shape=[jax.ShapeDtypeStruct((rows, cols), F32)] * 3,
        compiler_params=_cp(("parallel",)),
    )(g, w, m, v)


def _adamw_halves(cidx, mine, other, w, m, v, rb, name):
    rows, cols = w.shape
    nbh = rows // 2 // rb

    def kern(c_ref, a_ref, b_ref, w_ref, m_ref, v_ref, g_ref, d_ref, nm_ref, nv_ref):
        g_ = jnp.where(pl.program_id(0) // nbh == c_ref[0], a_ref[...], b_ref[...])
        m_ = ADAM_B1 * m_ref[...] + (1.0 - ADAM_B1) * g_
        v_ = ADAM_B2 * v_ref[...] + (1.0 - ADAM_B2) * jnp.square(g_)
        m_hat = m_ / (1.0 - ADAM_B1 ** ADAM_STEP)
        v_hat = v_ / (1.0 - ADAM_B2 ** ADAM_STEP)
        g_ref[...] = g_
        d_ref[...] = -ADAM_LR * (m_hat / (jnp.sqrt(v_hat) + ADAM_EPS) + ADAM_WD * w_ref[...])
        nm_ref[...] = m_
        nv_ref[...] = v_

    half = pl.BlockSpec((rb, cols), lambda i, c: (i % nbh, 0))
    blk = pl.BlockSpec((rb, cols), lambda i, c: (i, 0))
    return pl.pallas_call(
        kern, name=name,
        grid_spec=pltpu.PrefetchScalarGridSpec(num_scalar_prefetch=1, grid=(2 * nbh,), in_specs=[half, half, blk, blk, blk],
                                               out_specs=[blk] * 4),
        out_shape=[jax.ShapeDtypeStruct((rows, cols), F32)] * 4,
        compiler_params=_cp(("parallel",)),
    )(cidx, mine, other, w, m, v)


WEIGHTS = ("c_ctx", "w_mod", "b_mod", "norm_w", "w_in", "ssm_conv_w", "ssm_conv_b", "dt_bias", "a_log", "d_skip",
           "ssm_norm_w", "w_out_ssm", "conf_conv_w", "conf_conv_b", "conf_ln_w", "conf_ln_b", "w_out_conf", "w_out",
           "final_norm_w")


def kernel(x, c, ctx, c_ctx, w_mod, b_mod, norm_w, w_in, ssm_conv_w, ssm_conv_b, dt_bias, a_log, d_skip, ssm_norm_w, w_out_ssm, conf_conv_w, conf_conv_b, conf_ln_w, conf_ln_b, w_out_conf, w_out, final_norm_w, loss_target, m_c_ctx, m_w_mod, m_b_mod, m_norm_w, m_w_in, m_ssm_conv_w, m_ssm_conv_b, m_dt_bias, m_a_log, m_d_skip, m_ssm_norm_w, m_w_out_ssm, m_conf_conv_w, m_conf_conv_b, m_conf_ln_w, m_conf_ln_b, m_w_out_conf, m_w_out, m_final_norm_w, v_c_ctx, v_w_mod, v_b_mod, v_norm_w, v_w_in, v_ssm_conv_w, v_ssm_conv_b, v_dt_bias, v_a_log, v_d_skip, v_ssm_norm_w, v_w_out_ssm, v_conf_conv_w, v_conf_conv_b, v_conf_ln_w, v_conf_ln_b, v_w_out_conf, v_w_out, v_final_norm_w):
    wv = (c_ctx, w_mod, b_mod, norm_w, w_in, ssm_conv_w, ssm_conv_b, dt_bias, a_log, d_skip, ssm_norm_w, w_out_ssm,
          conf_conv_w, conf_conv_b, conf_ln_w, conf_ln_b, w_out_conf, w_out, final_norm_w)
    mv = (m_c_ctx, m_w_mod, m_b_mod, m_norm_w, m_w_in, m_ssm_conv_w, m_ssm_conv_b, m_dt_bias, m_a_log, m_d_skip,
          m_ssm_norm_w, m_w_out_ssm, m_conf_conv_w, m_conf_conv_b, m_conf_ln_w, m_conf_ln_b, m_w_out_conf, m_w_out,
          m_final_norm_w)
    vv = (v_c_ctx, v_w_mod, v_b_mod, v_norm_w, v_w_in, v_ssm_conv_w, v_ssm_conv_b, v_dt_bias, v_a_log, v_d_skip,
          v_ssm_norm_w, v_w_out_ssm, v_conf_conv_w, v_conf_conv_b, v_conf_ln_w, v_conf_ln_b, v_w_out_conf, v_w_out,
          v_final_norm_w)
    shapes = {n: a.shape for n, a in zip(WEIGHTS, wv)}

    def squeeze(d):
        return {n: (a if n in ("c_ctx", "final_norm_w") else a[0]) for n, a in d.items()}

    w, m, v = (squeeze(dict(zip(WEIGHTS, t))) for t in (wv, mv, vv))

    my_chip = 2 * lax.axis_index("x") + lax.axis_index("y")
    my_core = lax.axis_index("c")

    pw = _pack_shard(w)
    pwb, wib, psm = pw.astype(BF16), w["w_in"].astype(BF16), pw[O_SC:O_SC + SROWS]
    gm, gw, gs = _gather_weights([pwb, wib], psm)
    mine = (jnp.arange(NSHARD) == my_chip)[:, None, None]
    gm, gw, gs = jnp.where(mine, pwb[None], gm), jnp.where(mine, wib[None], gw), jnp.where(mine, psm[None], gs)
    full = dict(w)
    full.update(_unpack_gathered(gm, gw, gs))

    lsum, grad_x, g = _local_step(x[0], c, ctx[0], loss_target[0], full)
    loss = lax.psum(lsum, ("x", "y", "c"))

    cidx = my_core.astype(jnp.int32).reshape(1)
    mc = jnp.stack([my_chip, my_core]).astype(jnp.int32)
    gsrc = [_pack_full(g), g["w_in_shards"]]
    blocks = (RB, WB)
    sib = _swap_halves(gsrc)
    part = [_add_halves(cidx, a, b, rb, "add_halves_%d" % i) for i, (a, b, rb) in enumerate(zip(gsrc, sib, blocks))]
    far = _exchange_chips(part)
    red = [_add_chips(mc, a, b, f, rb, "add_chips_%d" % i) for i, (a, b, f, rb) in enumerate(zip(gsrc, sib, far, blocks))]
    got = _share_halves(red)
    g_pk = jnp.concatenate([jnp.where(my_core == 0, red[0], got[0]), jnp.where(my_core == 0, got[0], red[0])], axis=0)
    small = [name for name, _ in SMALL]
    as_row = lambda d: [d[name].reshape(1, -1) for name in small]
    res_sm = _small_update(as_row(g), as_row(w), as_row(m), as_row(v))

    gr = {"w_mod": g_pk[O_MOD:O_OS].reshape(D, R_MOD), "rest": g_pk[O_OS:PUSED]}
    wr, mr, vr = ({"w_mod": t["w_mod"], "rest": _pack_rest(t)} for t in (w, m, v))
    res = {k: _adamw(gr[k], wr[k], mr[k], vr[k], rb, "adamw_" + k)
           for k, rb in (("w_mod", 512), ("rest", (PUSED - O_OS) // 2))}
    gr["w_in"], *res["w_in"] = _adamw_halves(cidx, red[1], got[1], w["w_in"], m["w_in"], v["w_in"], WB, "adamw_w_in")

    outs = []
    for i in range(4):
        pick = (lambda k: gr[k]) if i == 0 else (lambda k: res[k][i - 1])
        d = {"w_mod": pick("w_mod")[None], "w_in": pick("w_in")[None]}
        d.update(_unpack_rest(pick("rest")))
        d.update({name: a.reshape(shapes[name]) for name, a in zip(small, res_sm[i])})
        outs.extend(d[n] for n in WEIGHTS)
    return (loss, grad_x[None], *outs)
```

```python
import functools

import jax
import jax.numpy as jnp
from jax import lax
from jax.experimental import pallas as pl
from jax.experimental.pallas import tpu as pltpu

F32, BF16 = jnp.float32, jnp.bfloat16

D = 1024
DI = 2048
NH = 32
HP = 64
NG = 8
HPG = 4
NS = 128
Q = 128
GW = 64
CK = 31
SK = 4
CTX = 256
EPS = 1e-6
XBC = DI + 2 * NG * NS
SSDW = XBC + 128
RESTW = 7168
T = 256
TX = 512
VMEM_LIMIT = 56 * 1024 * 1024

ADAM_LR, ADAM_B1, ADAM_B2, ADAM_EPS, ADAM_WD, ADAM_STEP = 0.001, 0.9, 0.999, 1e-08, 0.01, 10


def _cp(sem):
    return pltpu.CompilerParams(dimension_semantics=sem, vmem_limit_bytes=VMEM_LIMIT)


def _sig(x):
    return jax.nn.sigmoid(x)


def _silu(x):
    return x * _sig(x)


def _dsilu(x):
    s = _sig(x)
    return s * (1.0 + x * (1.0 - s))


def _dot(a, b):
    return jnp.dot(a, b, preferred_element_type=F32)


def _dot_nt(a, b):
    return lax.dot_general(a, b, (((1,), (1,)), ((), ())), preferred_element_type=F32)


def _split3(x):
    h = x.astype(BF16)
    r = x - h.astype(F32)
    m = r.astype(BF16)
    l = (r - m.astype(F32)).astype(BF16)
    return h, m, l


def _dot3_l(sel, x):
    h, m, l = _split3(x)
    return _dot(sel, h) + _dot(sel, m) + _dot(sel, l)


def _dot3_r(x, sel):
    h, m, l = _split3(x)
    return _dot(h, sel) + _dot(m, sel) + _dot(l, sel)


def _split2(x):
    h = x.astype(BF16)
    return h, (x - h.astype(F32)).astype(BF16)


def _dot2_l(sel, x):
    h, l = _split2(x)
    return _dot(sel, h) + _dot(sel, l)


def _dot2_r(x, sel):
    h, l = _split2(x)
    return _dot(h, sel) + _dot(l, sel)


def _iota(shape, dim):
    return lax.broadcasted_iota(jnp.int32, shape, dim)


def _mm(a, b, dims, m, n, k, bm, bn, bk, out_dtype, name):
    nk = k // bk
    assert m % bm == 0 and n % bn == 0 and k % bk == 0, (name, m, n, k, bm, bn, bk)

    def prod(a_ref, b_ref):
        av = a_ref[...].astype(BF16)
        bv = b_ref[...].astype(BF16)
        if dims == "nn":
            return _dot(av, bv)
        if dims == "nt":
            return _dot_nt(av, bv)
        return lax.dot_general(av, bv, (((0,), (0,)), ((), ())), preferred_element_type=F32)

    def kern_one(a_ref, b_ref, o_ref):
        o_ref[...] = prod(a_ref, b_ref).astype(out_dtype)

    def kern_acc(a_ref, b_ref, o_ref, acc):
        kk = pl.program_id(2)

        @pl.when(kk == 0)
        def _():
            acc[...] = jnp.zeros_like(acc)

        acc[...] += prod(a_ref, b_ref)

        @pl.when(kk == nk - 1)
        def _():
            o_ref[...] = acc[...].astype(out_dtype)

    if dims == "nn":
        a_spec = pl.BlockSpec((bm, bk), lambda j, i, kk: (i, kk))
        b_spec = pl.BlockSpec((bk, bn), lambda j, i, kk: (kk, j))
    elif dims == "nt":
        a_spec = pl.BlockSpec((bm, bk), lambda j, i, kk: (i, kk))
        b_spec = pl.BlockSpec((bn, bk), lambda j, i, kk: (j, kk))
    else:
        a_spec = pl.BlockSpec((bk, bm), lambda j, i, kk: (kk, i))
        b_spec = pl.BlockSpec((bk, bn), lambda j, i, kk: (kk, j))
    return pl.pallas_call(
        kern_one if nk == 1 else kern_acc, name=name,
        grid=(n // bn, m // bm, nk),
        in_specs=[a_spec, b_spec],
        out_specs=pl.BlockSpec((bm, bn), lambda j, i, kk: (i, j)),
        out_shape=jax.ShapeDtypeStruct((m, n), out_dtype),
        scratch_shapes=[] if nk == 1 else [pltpu.VMEM((bm, bn), F32)],
        compiler_params=_cp(("parallel", "parallel", "arbitrary")),
    )(a, b)


def _mod_fwd(cc, w_mod, b_mod):
    def kern(cc_ref, w_ref, b_ref, o_ref):
        s = _silu(cc_ref[...]).astype(BF16)
        o_ref[...] = _dot(s, w_ref[...]) + b_ref[...]

    return pl.pallas_call(
        kern, name="mod_fwd", grid=(3,),
        in_specs=[pl.BlockSpec((8, D), lambda j: (0, 0)), pl.BlockSpec((D, D), lambda j: (0, j)),
                  pl.BlockSpec((1, D), lambda j: (0, j))],
        out_specs=pl.BlockSpec((8, D), lambda j: (0, j)),
        out_shape=jax.ShapeDtypeStruct((8, 3 * D), F32),
        compiler_params=_cp(("parallel",)),
    )(cc, w_mod, b_mod)


def _mod_bwd(dmod, cc, cct, w_mod):
    def kern(dm_ref, cc_ref, cct_ref, w_ref, gw_ref, gb_ref, gc_ref):
        kk = pl.program_id(0)
        dm = dm_ref[...]
        sct = _silu(cct_ref[...])
        gw_ref[...] = sct[:, 0:1] * dm[0:1, :] + sct[:, 1:2] * dm[1:2, :]
        gb_ref[...] = jnp.broadcast_to(dm[0:1, :] + dm[1:2, :], dm.shape)

        @pl.when(kk == 0)
        def _():
            gc_ref[...] = jnp.zeros_like(gc_ref)

        gc_ref[...] += _dot_nt(dm.astype(BF16), w_ref[...])

        @pl.when(kk == 2)
        def _():
            gc_ref[...] = gc_ref[...] * _dsilu(cc_ref[...])

    return pl.pallas_call(
        kern, name="mod_bwd", grid=(3,),
        in_specs=[pl.BlockSpec((8, D), lambda j: (0, j)), pl.BlockSpec((8, D), lambda j: (0, 0)),
                  pl.BlockSpec((D, 8), lambda j: (0, 0)), pl.BlockSpec((D, D), lambda j: (0, j))],
        out_specs=[pl.BlockSpec((D, D), lambda j: (0, j)), pl.BlockSpec((8, D), lambda j: (0, j)),
                   pl.BlockSpec((8, D), lambda j: (0, 0))],
        out_shape=[jax.ShapeDtypeStruct((D, 3 * D), F32), jax.ShapeDtypeStruct((8, 3 * D), F32),
                   jax.ShapeDtypeStruct((8, D), F32)],
        compiler_params=_cp(("arbitrary",)),
    )(dmod, cc, cct, w_mod)


def _ext_specs(nct):
    return (pl.BlockSpec((T, D), lambda i: (jnp.minimum(i, nct - 1), 0)),
            pl.BlockSpec((T, D), lambda i: (jnp.maximum(i - nct, 0), 0)))


def _norm_fwd(ctx, xl, mod, nw, nct):
    lext = ctx.shape[0] + xl.shape[0]

    def kern(c_ref, x_ref, mod_ref, nw_ref, h_ref):
        is_ctx = pl.program_id(0) < nct
        x = jnp.where(is_ctx, c_ref[...], x_ref[...])
        r = lax.rsqrt(jnp.mean(x * x, axis=-1, keepdims=True) + EPS)
        xn = x * r * nw_ref[...]
        shift = jnp.where(is_ctx, mod_ref[1:2, 0:D], mod_ref[0:1, 0:D])
        scale = jnp.where(is_ctx, mod_ref[1:2, D:2 * D], mod_ref[0:1, D:2 * D])
        h_ref[...] = (xn * (1.0 + scale) + shift).astype(BF16)

    return pl.pallas_call(
        kern, name="norm_fwd", grid=(lext // T,),
        in_specs=[*_ext_specs(nct), pl.BlockSpec((8, 3 * D), lambda i: (0, 0)),
                  pl.BlockSpec((1, D), lambda i: (0, 0))],
        out_specs=pl.BlockSpec((T, D), lambda i: (i, 0)),
        out_shape=jax.ShapeDtypeStruct((lext, D), BF16),
        compiler_params=_cp(("parallel",)),
    )(ctx, xl, mod, nw)


def _norm_bwd(dha, dhb, ctx, xl, dx2, mod, nw, nct):
    lext = ctx.shape[0] + xl.shape[0]
    ntl = lext // T

    def kern(dha_ref, dhb_ref, c_ref, x_ref, dx2_ref, mod_ref, nw_ref, gx_ref, gnw_ref, dss_ref):
        i = pl.program_id(0)
        is_ctx = i < nct

        @pl.when(i == 0)
        def _():
            gnw_ref[...] = jnp.zeros_like(gnw_ref)
            dss_ref[...] = jnp.zeros_like(dss_ref)

        x = jnp.where(is_ctx, c_ref[...], x_ref[...])
        dh_ = dha_ref[...].astype(F32) + jnp.where(is_ctx, 0.0, dhb_ref[...].astype(F32))
        nw_ = nw_ref[...]
        r = lax.rsqrt(jnp.mean(x * x, axis=-1, keepdims=True) + EPS)
        xn = x * r
        scale = jnp.where(is_ctx, mod_ref[1:2, D:2 * D], mod_ref[0:1, D:2 * D])
        dsh = jnp.sum(dh_, axis=0, keepdims=True)
        dsc = jnp.sum(dh_ * (xn * nw_), axis=0, keepdims=True)
        row = jnp.concatenate([dsh, dsc], axis=1)
        rid = _iota((8, 2 * D), 0)
        dss_ref[...] += jnp.where(rid == jnp.where(is_ctx, 1, 0), row, 0.0)
        dxnw = dh_ * (1.0 + scale)
        gnw_ref[...] += jnp.broadcast_to(jnp.sum(dxnw * xn, axis=0, keepdims=True), (8, D))
        dxn = dxnw * nw_
        dx = r * (dxn - xn * jnp.mean(dxn * xn, axis=-1, keepdims=True))
        gx_ref[...] = dx2_ref[...] + dx

    return pl.pallas_call(
        kern, name="norm_bwd", grid=(ntl,),
        in_specs=[pl.BlockSpec((T, D), lambda i: (i, 0)), pl.BlockSpec((T, D), lambda i: (jnp.maximum(i - nct, 0), 0)),
                  *_ext_specs(nct),
                  pl.BlockSpec((T, D), lambda i: (jnp.maximum(i - nct, 0), 0)),
                  pl.BlockSpec((8, 3 * D), lambda i: (0, 0)), pl.BlockSpec((1, D), lambda i: (0, 0))],
        out_specs=[pl.BlockSpec((T, D), lambda i: (jnp.maximum(i - nct, 0), 0)),
                   pl.BlockSpec((8, D), lambda i: (0, 0)), pl.BlockSpec((8, 2 * D), lambda i: (0, 0))],
        out_shape=[jax.ShapeDtypeStruct((lext - nct * T, D), F32), jax.ShapeDtypeStruct((8, D), F32),
                   jax.ShapeDtypeStruct((8, 2 * D), F32)],
        compiler_params=_cp(("arbitrary",)),
    )(dha, dhb, ctx, xl, dx2, mod, nw)


CB = 1024


def _halo_specs(width_blk, col_off_blocks, ntl):
    t8 = T // 8
    main = pl.BlockSpec((T, width_blk), lambda j, i: (i, j + col_off_blocks))
    prev = pl.BlockSpec((8, width_blk), lambda j, i: (jnp.maximum(i * t8 - 1, 0), j + col_off_blocks))
    nxt = pl.BlockSpec((8, width_blk), lambda j, i: (jnp.minimum((i + 1) * t8, ntl * t8 - 1), j + col_off_blocks))
    return main, prev, nxt


def _seq_edges(i, nct, ntl):
    starts = jnp.logical_or(i == 0, i == nct)
    ends = jnp.logical_or(i == nct - 1, i == ntl - 1)
    return starts, ends


def _shifted(ext, off):
    n = ext.shape[0]
    return pltpu.roll(ext, (-off) % n, axis=0)[8:8 + T]


def _conv_fwd(proj_ssd, cw, cb, nct):
    lext = proj_ssd.shape[0]
    ntl = lext // T

    def kern(u_ref, up_ref, un_ref, w_ref, b_ref, o_ref, o16_ref):
        i = pl.program_id(1)
        starts, ends = _seq_edges(i, nct, ntl)
        up = jnp.where(starts, 0.0, up_ref[...])
        un = jnp.where(ends, 0.0, un_ref[...])
        ext = jnp.concatenate([up, u_ref[...], un], axis=0)
        w = w_ref[...]
        pre = b_ref[...] + w[0:1] * _shifted(ext, -2) + w[1:2] * _shifted(ext, -1) \
            + w[2:3] * u_ref[...] + w[3:4] * _shifted(ext, 1)
        act = _silu(pre)
        o_ref[...] = act
        o16_ref[...] = act.astype(BF16)

    cbf = 4 * CB
    main, prev, nxt = _halo_specs(cbf, 0, ntl)
    return pl.pallas_call(
        kern, name="conv_fwd", grid=(XBC // cbf, ntl),
        in_specs=[main, prev, nxt, pl.BlockSpec((8, cbf), lambda j, i: (0, j)), pl.BlockSpec((1, cbf), lambda j, i: (0, j))],
        out_specs=[pl.BlockSpec((T, cbf), lambda j, i: (i, j))] * 2,
        out_shape=[jax.ShapeDtypeStruct((lext, XBC), F32), jax.ShapeDtypeStruct((lext, XBC), BF16)],
        compiler_params=_cp(("parallel", "parallel")),
    )(proj_ssd, proj_ssd, proj_ssd, cw, cb)


def _conv_bwd(dpost, proj_ssd, cw, cb, col_off, width, nct, name):
    lext = proj_ssd.shape[0]
    ntl = lext // T
    bw = min(width, 2 * CB)
    assert col_off % bw == 0 and width % bw == 0
    cob = col_off // bw

    def kern(u_ref, up_ref, un_ref, d_ref, dp_ref, dn_ref, w_ref, b_ref, du_ref, gw_ref, gb_ref):
        i = pl.program_id(1)

        @pl.when(i == 0)
        def _():
            gw_ref[...] = jnp.zeros_like(gw_ref)
            gb_ref[...] = jnp.zeros_like(gb_ref)

        starts, ends = _seq_edges(i, nct, ntl)
        ext = jnp.concatenate([jnp.where(starts, 0.0, up_ref[...]), u_ref[...], jnp.where(ends, 0.0, un_ref[...])], axis=0)
        dext = jnp.concatenate([jnp.where(starts, 0.0, dp_ref[...]), d_ref[...], jnp.where(ends, 0.0, dn_ref[...])], axis=0)
        w = w_ref[...]
        n = ext.shape[0]
        pre = b_ref[...] + w[0:1] * pltpu.roll(ext, 2, axis=0) + w[1:2] * pltpu.roll(ext, 1, axis=0) \
            + w[2:3] * ext + w[3:4] * pltpu.roll(ext, n - 1, axis=0)
        dpre = dext * _dsilu(pre)
        dm = dpre[8:8 + T]
        du = w[0:1] * _shifted(dpre, 2) + w[1:2] * _shifted(dpre, 1) + w[2:3] * dm + w[3:4] * _shifted(dpre, -1)
        du_ref[...] = du.astype(BF16)
        g0 = jnp.sum(dm * _shifted(ext, -2), axis=0, keepdims=True)
        g1 = jnp.sum(dm * _shifted(ext, -1), axis=0, keepdims=True)
        g2 = jnp.sum(dm * u_ref[...], axis=0, keepdims=True)
        g3 = jnp.sum(dm * _shifted(ext, 1), axis=0, keepdims=True)
        rid = _iota((8, bw), 0)
        gw_ref[...] += jnp.where(rid == 0, g0, jnp.where(rid == 1, g1, jnp.where(rid == 2, g2, jnp.where(rid == 3, g3, 0.0))))
        gb_ref[...] += jnp.broadcast_to(jnp.sum(dm, axis=0, keepdims=True), (8, bw))

    main, prev, nxt = _halo_specs(bw, cob, ntl)
    dmain, dprev, dnxt = _halo_specs(bw, 0, ntl)
    return pl.pallas_call(
        kern, name=name, grid=(width // bw, ntl),
        in_specs=[main, prev, nxt, dmain, dprev, dnxt,
                  pl.BlockSpec((8, bw), lambda j, i: (0, j + cob)), pl.BlockSpec((1, bw), lambda j, i: (0, j + cob))],
        out_specs=[pl.BlockSpec((T, bw), lambda j, i: (i, j)), pl.BlockSpec((8, bw), lambda j, i: (0, j)),
                   pl.BlockSpec((8, bw), lambda j, i: (0, j))],
        out_shape=[jax.ShapeDtypeStruct((lext, width), BF16), jax.ShapeDtypeStruct((8, width), F32),
                   jax.ShapeDtypeStruct((8, width), F32)],
        compiler_params=_cp(("parallel", "arbitrary")),
    )(proj_ssd, proj_ssd, proj_ssd, dpost, dpost, dpost, cw, cb)


def _tri(lower):
    r, c = _iota((Q, Q), 0), _iota((Q, Q), 1)
    return jnp.where((c <= r) if lower else (c >= r), 1.0, 0.0).astype(BF16)


def _is_bdir_lane(shape):
    ln = _iota(shape, len(shape) - 1)
    return jnp.logical_and(((ln >> 2) & 1) == 1, ln < 64)


def _dt_fwd(proj_ssd, dtb, av):
    lext = proj_ssd.shape[0]

    def kern(p_ref, b_ref, a_ref, dtg_ref, lag_ref, dtt_ref, lat_ref):
        lane = _iota((T, 128), 1)
        raw = p_ref[...] + b_ref[...]
        dt = jnp.where(lane < 64, jnp.maximum(raw, 0.0) + jnp.log1p(jnp.exp(-jnp.abs(raw))), 0.0)
        dta = dt * a_ref[...]
        tl, tu = _tri(True), _tri(False)
        isb = _is_bdir_lane((Q, 128))
        las = []
        for qq in range(T // Q):
            blk = dta[qq * Q:(qq + 1) * Q]
            las.append(jnp.where(isb, _dot3_l(tu, blk), _dot3_l(tl, blk)))
        la = jnp.concatenate(las, axis=0)
        for g in range(NG):
            sh = (128 - 8 * g) % 128
            dtg_ref[g] = jnp.where(lane < 8, pltpu.roll(dt, sh, axis=1) if sh else dt, 0.0)
            lag_ref[g] = jnp.where(lane < 8, pltpu.roll(la, sh, axis=1) if sh else la, 0.0)
        dtt_ref[...] = dt.T[0:64]
        lat_ref[...] = la.T[0:64]

    return pl.pallas_call(
        kern, name="dt_fwd", grid=(lext // T,),
        in_specs=[pl.BlockSpec((T, 128), lambda i: (i, XBC // 128)), pl.BlockSpec((1, 128), lambda i: (0, 0)),
                  pl.BlockSpec((1, 128), lambda i: (0, 0))],
        out_specs=[pl.BlockSpec((NG, T, 128), lambda i: (0, i, 0)), pl.BlockSpec((NG, T, 128), lambda i: (0, i, 0)),
                   pl.BlockSpec((64, T), lambda i: (0, i)), pl.BlockSpec((64, T), lambda i: (0, i))],
        out_shape=[jax.ShapeDtypeStruct((NG, lext, 128), F32), jax.ShapeDtypeStruct((NG, lext, 128), F32),
                   jax.ShapeDtypeStruct((64, lext), F32), jax.ShapeDtypeStruct((64, lext), F32)],
        compiler_params=_cp(("parallel",)),
    )(proj_ssd, dtb, av)


def _dt_bwd(ddtg, proj_ssd, dtb):
    lext = proj_ssd.shape[0]

    def kern(d_ref, p_ref, b_ref, o_ref, gb_ref):
        @pl.when(pl.program_id(0) == 0)
        def _():
            gb_ref[...] = jnp.zeros_like(gb_ref)

        acc = d_ref[0]
        for g in range(1, NG):
            acc = acc + pltpu.roll(d_ref[g], 8 * g, axis=1)
        draw = acc * _sig(p_ref[...] + b_ref[...])
        o_ref[...] = draw.astype(BF16)
        gb_ref[...] += jnp.broadcast_to(jnp.sum(draw, axis=0, keepdims=True), (8, 128))

    return pl.pallas_call(
        kern, name="dt_bwd", grid=(lext // T,),
        in_specs=[pl.BlockSpec((NG, T, 128), lambda i: (0, i, 0)), pl.BlockSpec((T, 128), lambda i: (i, XBC // 128)),
                  pl.BlockSpec((1, 128), lambda i: (0, 0))],
        out_specs=[pl.BlockSpec((T, 128), lambda i: (i, 0)), pl.BlockSpec((8, 128), lambda i: (0, 0))],
        out_shape=[jax.ShapeDtypeStruct((lext, 128), BF16), jax.ShapeDtypeStruct((8, 128), F32)],
        compiler_params=_cp(("arbitrary",)),
    )(ddtg, proj_ssd, dtb)


def _expand_sel(d):
    r, c = _iota((128, 256), 0), _iota((128, 256), 1)
    return jnp.where(r == 4 * d + (c >> 6), 1.0, 0.0).astype(BF16)


def _reduce_sel(d):
    r, c = _iota((256, 128), 0), _iota((256, 128), 1)
    return jnp.where(c == 4 * d + (r >> 6), 1.0, 0.0).astype(BF16)


def _chunk_of_bwd_dir(j, ncc, nc):
    return jnp.where(j < ncc, ncc - 1 - j, nc + ncc - 1 - j)


def _dir_terms(la, dt, d):
    lane = _iota(la.shape, 1)
    mine = jnp.logical_and(lane >= 4 * d, lane < 4 * d + 4)
    la = jnp.where(mine, la, 0.0)
    tot = la[Q - 1:Q] if d == 0 else la[0:1]
    wnd = jnp.exp(tot - la)
    return tot, wnd * jnp.where(mine, dt, 0.0), wnd


def _ssd_state(xbc, dtg, lag, ncc):
    lext = xbc.shape[0]
    nc = lext // Q

    def kern(xf_ref, bf_ref, dtf_ref, laf_ref, xb_ref, bb_ref, dtb_ref, lab_ref, hf_ref, hb_ref, sf, sb):
        @pl.when(pl.program_id(0) == 0)
        def _():
            sf[...] = jnp.zeros_like(sf)
            sb[...] = jnp.zeros_like(sb)

        for d, (x_ref, b_ref, dt_ref, la_ref, h_ref, s) in enumerate(
                ((xf_ref, bf_ref, dtf_ref, laf_ref, hf_ref, sf), (xb_ref, bb_ref, dtb_ref, lab_ref, hb_ref, sb))):
            h_ref[...] = s[...].astype(BF16)
            ex = _expand_sel(d)
            for g in range(NG):
                cols = slice(256 * g, 256 * (g + 1))
                tot, w_end, _ = _dir_terms(la_ref[g], dt_ref[g], d)
                wexp = _dot2_r(w_end, ex)
                dexp = _dot2_r(jnp.broadcast_to(jnp.exp(tot), (8, 128)), ex)[0:1]
                xw = (x_ref[:, cols] * wexp).astype(BF16)
                s[:, cols] = s[:, cols] * dexp + _dot(b_ref[:, 128 * g:128 * (g + 1)].astype(F32).T.astype(BF16), xw)

    cb = functools.partial(_chunk_of_bwd_dir, ncc=ncc, nc=nc)
    sm = lambda f: pl.BlockSpec((NG, Q, 128), lambda j: (0, f(j), 0))
    one = lambda j: j
    return pl.pallas_call(
        kern, name="ssd_state", grid=(nc,),
        in_specs=[pl.BlockSpec((Q, DI), lambda j: (j, 0)), pl.BlockSpec((Q, NG * NS), lambda j: (j, 2)), sm(one), sm(one),
                  pl.BlockSpec((Q, DI), lambda j: (cb(j), 0)), pl.BlockSpec((Q, NG * NS), lambda j: (cb(j), 2)), sm(cb), sm(cb)],
        out_specs=[pl.BlockSpec((None, 128, DI), lambda j: (j, 0, 0)),
                   pl.BlockSpec((None, 128, DI), lambda j: (cb(j), 0, 0))],
        out_shape=[jax.ShapeDtypeStruct((nc, 128, DI), BF16), jax.ShapeDtypeStruct((nc, 128, DI), BF16)],
        scratch_shapes=[pltpu.VMEM((128, DI), F32), pltpu.VMEM((128, DI), F32)],
        compiler_params=_cp(("arbitrary",)),
    )(xbc, xbc, dtg, lag, xbc, xbc, dtg, lag)


def _ssd_out(xbc, dtg, lag, dtt, lat, htf, htb, ncc):
    lext = xbc.shape[0]
    nc = lext // Q
    ncx = nc - ncc

    gps = 8
    li, si = (lambda: _iota((Q, Q), 0)), (lambda: _iota((Q, Q), 1))

    def kern(x_ref, b_ref, c_ref, dtg_ref, lag_ref, dtt_ref, lat_ref, hf_ref, hb_ref, y_ref):
        lane = _iota((Q, 128), 1)
        masks = (li() >= si(), li() <= si())
        for gg in range(gps):
            cols = slice(256 * gg, 256 * (gg + 1))
            cm = c_ref[:, 128 * gg:128 * (gg + 1)]
            xb_ = x_ref[:, cols].astype(BF16)
            s_ = _dot_nt(cm.astype(BF16), b_ref[:, 128 * gg:128 * (gg + 1)].astype(BF16))
            la, dtt_, lat_ = lag_ref[gg], dtt_ref[8 * gg:8 * (gg + 1)], lat_ref[8 * gg:8 * (gg + 1)]
            elam = jnp.exp(la)
            yh = [jnp.zeros((Q, 128), F32), jnp.zeros((Q, 128), F32)]
            for d, h_ref in enumerate((hf_ref, hb_ref)):
                rhs = jnp.concatenate([xb_, h_ref[:, cols].astype(BF16)], axis=0)
                lhs = []
                for r in range(HPG):
                    j = 4 * d + r
                    lm = jnp.where(masks[d], jnp.exp(la[:, j:j + 1] - lat_[j:j + 1, :]), 0.0)
                    w = s_ * lm * dtt_[j:j + 1, :]
                    lhs.append(jnp.concatenate([w, cm * elam[:, j:j + 1]], axis=1).astype(BF16))
                for b in range(HPG // 2):
                    ypair = _dot(jnp.concatenate(lhs[2 * b:2 * b + 2], axis=0), rhs[:, 128 * b:128 * (b + 1)])
                    yh[b] = yh[b] + jnp.where(lane < 64, ypair[0:Q], ypair[Q:2 * Q])
            y_ref[:, cols] = jnp.concatenate(yh, axis=1).astype(BF16)

    nb = NG // gps
    sm = pl.BlockSpec((gps, Q, 128), lambda c, g: (g, c + ncc, 0))
    smt = pl.BlockSpec((8 * gps, Q), lambda c, g: (g, c + ncc))
    st3 = pl.BlockSpec((None, 128, 256 * gps), lambda c, g: (c + ncc, 0, g))
    return pl.pallas_call(
        kern, name="ssd_out", grid=(ncx, nb),
        in_specs=[pl.BlockSpec((Q, 256 * gps), lambda c, g: (c + ncc, g)),
                  pl.BlockSpec((Q, 128 * gps), lambda c, g: (c + ncc, 2 * nb + g)),
                  pl.BlockSpec((Q, 128 * gps), lambda c, g: (c + ncc, 3 * nb + g)), sm, sm, smt, smt, st3, st3],
        out_specs=pl.BlockSpec((Q, 256 * gps), lambda c, g: (c, g)),
        out_shape=jax.ShapeDtypeStruct((ncx * Q, DI), BF16),
        compiler_params=_cp(("parallel", "parallel")),
    )(xbc, xbc, xbc, dtg, lag, dtt, lat, htf, htb)


def _ssd_bwd_state(xbc, dy, lag, ncc):
    lext = xbc.shape[0]
    nc = lext // Q

    def kern(cf_ref, dyf_ref, laf_ref, cb_ref, dyb_ref, lab_ref, df_ref, db_ref, sf, sb):
        @pl.when(pl.program_id(0) == 0)
        def _():
            sf[...] = jnp.zeros_like(sf)
            sb[...] = jnp.zeros_like(sb)

        for d, (c_ref, dy_ref, la_ref, o_ref, s) in enumerate(
                ((cf_ref, dyf_ref, laf_ref, df_ref, sf), (cb_ref, dyb_ref, lab_ref, db_ref, sb))):
            o_ref[...] = s[...].astype(BF16)
            ex = _expand_sel(d)
            for g in range(NG):
                cols = slice(256 * g, 256 * (g + 1))
                la = la_ref[g]
                tot = la[Q - 1:Q] if d == 0 else la[0:1]
                eexp = _dot2_r(jnp.exp(la), ex)
                dexp = _dot2_r(jnp.broadcast_to(jnp.exp(tot), (8, 128)), ex)[0:1]
                dye = (dy_ref[:, cols] * eexp).astype(BF16)
                s[:, cols] = s[:, cols] * dexp + _dot(c_ref[:, 128 * g:128 * (g + 1)].astype(F32).T.astype(BF16), dye)

    cf = lambda j: nc - 1 - j
    cb = lambda j: _chunk_of_bwd_dir(nc - 1 - j, ncc, nc)
    sm = lambda f: pl.BlockSpec((NG, Q, 128), lambda j: (0, f(j), 0))
    return pl.pallas_call(
        kern, name="ssd_bwd_state", grid=(nc,),
        in_specs=[pl.BlockSpec((Q, NG * NS), lambda j: (cf(j), 3)), pl.BlockSpec((Q, DI), lambda j: (cf(j), 0)), sm(cf),
                  pl.BlockSpec((Q, NG * NS), lambda j: (cb(j), 3)), pl.BlockSpec((Q, DI), lambda j: (cb(j), 0)), sm(cb)],
        out_specs=[pl.BlockSpec((None, 128, DI), lambda j: (cf(j), 0, 0)),
                   pl.BlockSpec((None, 128, DI), lambda j: (cb(j), 0, 0))],
        out_shape=[jax.ShapeDtypeStruct((nc, 128, DI), BF16), jax.ShapeDtypeStruct((nc, 128, DI), BF16)],
        scratch_shapes=[pltpu.VMEM((128, DI), F32), pltpu.VMEM((128, DI), F32)],
        compiler_params=_cp(("arbitrary",)),
    )(xbc, dy, lag, xbc, dy, lag)


def _ssd_bwd_out(xbc, dy, dsk, dtg, lag, dtt, lat, htf, htb, dhf, dhb, a_rows):
    lext = xbc.shape[0]
    nc = lext // Q

    gps = 1

    def kern(x_ref, b_ref, c_ref, dy_ref, sk_ref, dtg_ref, lag_ref, dtt_ref, lat_ref, hf_ref, hb_ref, df_ref, db_ref,
             a_ref, dx_ref, dbo_ref, dco_ref, ddt_ref, ga_ref):
        @pl.when(pl.program_id(1) == 0)
        def _():
            ga_ref[...] = jnp.zeros_like(ga_ref)

        for gg in range(gps):
            one_group(gg, x_ref, b_ref, c_ref, dy_ref, sk_ref, dtg_ref, lag_ref, dtt_ref, lat_ref, hf_ref, hb_ref, df_ref,
                      db_ref, a_ref, dx_ref, dbo_ref, dco_ref, ddt_ref, ga_ref)

    def one_group(gg, x_ref, b_ref, c_ref, dy_ref, sk_ref, dtg_ref, lag_ref, dtt_ref, lat_ref, hf_ref, hb_ref, df_ref,
                  db_ref, a_ref, dx_ref, dbo_ref, dco_ref, ddt_ref, ga_ref):
        g = pl.program_id(0) * gps + gg
        cols, cols128 = slice(256 * gg, 256 * (gg + 1)), slice(128 * gg, 128 * (gg + 1))
        x, bm, cm, dy_ = x_ref[:, cols], b_ref[:, cols128], c_ref[:, cols128], dy_ref[:, cols].astype(F32)
        xb_, bb_, cb_, dyb_ = x.astype(BF16), bm.astype(BF16), cm.astype(BF16), dy_.astype(BF16)
        st = _dot_nt(bb_, cb_)
        si, li = _iota((Q, Q), 0), _iota((Q, Q), 1)
        lane = _iota((Q, 256), 1)
        lane128 = _iota((Q, 128), 1)
        row128 = _iota((Q, 128), 0)
        sub = _iota((128, Q), 0)
        la, dt = lag_ref[gg], dtg_ref[gg]
        dtt_, lat_ = dtt_ref[8 * gg:8 * (gg + 1)], lat_ref[8 * gg:8 * (gg + 1)]
        elam = jnp.exp(la)
        dst = jnp.zeros((Q, Q), F32)
        dxh = [jnp.zeros((Q, 128), F32), jnp.zeros((Q, 128), F32)]
        cdir, clam = jnp.zeros((Q, 128), F32), jnp.zeros((Q, 128), F32)
        dba = jnp.zeros((Q, 128), F32)
        dca = jnp.zeros((Q, 128), F32)
        dlam = jnp.zeros((Q, 128), F32)
        ddir = jnp.zeros((Q, 128), F32)
        rows = jnp.zeros((128, Q), F32)
        per_dir = []
        for d, (h_ref, dh_ref) in enumerate(((hf_ref, df_ref), (hb_ref, db_ref))):
            ht, dht = h_ref[:, cols].astype(F32), dh_ref[:, cols].astype(F32)
            htb_, dhtb_ = ht.astype(BF16), dht.astype(BF16)
            tot, w_end, wnd = _dir_terms(la, dt, d)
            ex, rs = _expand_sel(d), _reduce_sel(d)
            elx = _dot2_r(elam, ex)
            wex = _dot2_r(w_end, ex)
            dye = dy_ * elx
            ch = _dot(cb_, htb_)
            bd = _dot(bb_, dhtb_)
            dca = dca + _dot_nt(dye.astype(BF16), htb_)
            dba = dba + _dot_nt((x * wex).astype(BF16), dhtb_)
            dlam = dlam + _dot2_r(dye * ch, rs)
            xbd = _dot2_r(x * bd, rs)
            e_ = w_end * xbd
            dlam = dlam - e_
            ddir = ddir + wnd * xbd
            hh = _dot2_r(jnp.broadcast_to(jnp.sum(dht * ht, axis=0, keepdims=True), (8, 256)), rs)[0:1]
            tot_term = jnp.sum(e_, axis=0, keepdims=True) + jnp.exp(tot) * hh
            dlam = dlam + jnp.where(row128 == (Q - 1 if d == 0 else 0), tot_term, 0.0)
            per_dir.append((w_end, jnp.concatenate([dyb_, dhtb_], axis=0), (li >= si) if d == 0 else (li <= si)))
        for r in range(HPG):
            half = slice(128 * (r // 2), 128 * (r // 2 + 1))
            hm = (lane128 >> 6) == (r % 2)
            dwt = _dot_nt(jnp.where(hm, x[:, half], 0.0).astype(BF16), dyb_[:, half])
            q = dwt * st
            for d, (w_end, rhs, maskt) in enumerate(per_dir):
                j = 4 * d + r
                dc = dt[:, j:j + 1]
                lmt = jnp.where(maskt, jnp.exp(lat_[j:j + 1, :] - la[:, j:j + 1]), 0.0)
                ldc = lmt * jnp.broadcast_to(dc, (Q, Q))
                lhs = jnp.concatenate([st * ldc, bm * w_end[:, j:j + 1]], axis=1).astype(BF16)
                dxh[r // 2] = dxh[r // 2] + jnp.where(hm, _dot(lhs, rhs[:, half]), 0.0)
                cs = jnp.sum(q * lmt, axis=1, keepdims=True)
                cdir = jnp.where(lane128 == j, cs, cdir)
                clam = jnp.where(lane128 == j, cs * dc, clam)
                rows = rows + jnp.where(sub == j, jnp.sum(q * ldc, axis=0, keepdims=True), 0.0)
                dst = dst + dwt * ldc
        dxa = jnp.concatenate(dxh, axis=1)
        ddir = ddir + cdir
        dlam = dlam - clam + rows.T
        dba = dba + _dot(dst.astype(BF16), cb_)
        dca = dca + _dot(dst.T.astype(BF16), bb_)
        isb = jnp.logical_and(lane128 >= 4, lane128 < 8)
        ddel = jnp.where(isb, _dot2_l(_tri(True), dlam), _dot2_l(_tri(False), dlam))
        a_l = a_ref[pl.ds(g, 1), :]
        ddt_ref[gg] = ddir + a_l * ddel
        ga_ref[gg] += jnp.broadcast_to(a_l * jnp.sum(dt * ddel, axis=0, keepdims=True), (8, 128))
        dx_ref[:, cols] = dxa + dy_ * sk_ref[:, cols]
        dbo_ref[:, cols128] = dba
        dco_ref[:, cols128] = dca

    nb = NG // gps
    st3 = pl.BlockSpec((None, 128, 256 * gps), lambda g, c: (c, 0, g))
    sm = pl.BlockSpec((gps, Q, 128), lambda g, c: (g, c, 0))
    smt = pl.BlockSpec((8 * gps, Q), lambda g, c: (g, c))
    wide = pl.BlockSpec((Q, 256 * gps), lambda g, c: (c, g))
    return pl.pallas_call(
        kern, name="ssd_bwd_out", grid=(nb, nc),
        in_specs=[wide, pl.BlockSpec((Q, 128 * gps), lambda g, c: (c, 2 * nb + g)),
                  pl.BlockSpec((Q, 128 * gps), lambda g, c: (c, 3 * nb + g)), wide,
                  pl.BlockSpec((1, 256 * gps), lambda g, c: (0, g)), sm, sm, smt, smt, st3, st3, st3, st3,
                  pl.BlockSpec((8, 128), lambda g, c: (0, 0))],
        out_specs=[wide, pl.BlockSpec((Q, 128 * gps), lambda g, c: (c, g)),
                   pl.BlockSpec((Q, 128 * gps), lambda g, c: (c, g)), sm, pl.BlockSpec((gps, 8, 128), lambda g, c: (g, 0, 0))],
        out_shape=[jax.ShapeDtypeStruct((lext, DI), F32), jax.ShapeDtypeStruct((lext, NG * NS), F32),
                   jax.ShapeDtypeStruct((lext, NG * NS), F32), jax.ShapeDtypeStruct((NG, lext, 128), F32),
                   jax.ShapeDtypeStruct((NG, 8, 128), F32)],
        compiler_params=_cp(("parallel", "arbitrary")),
    )(xbc, xbc, xbc, dy, dsk, dtg, lag, dtt, lat, htf, htb, dhf, dhb, a_rows)


def _post_fwd(yssm, xbc, proj_rest, dsk, gnw, nct):
    l = yssm.shape[0]

    def kern(y_ref, x_ref, z_ref, dsk_ref, w_ref, o_ref):
        y = y_ref[...].astype(F32) + dsk_ref[...] * x_ref[...]
        yz = y * _silu(z_ref[...].astype(F32))
        for g in range(NG):
            sl = slice(256 * g, 256 * (g + 1))
            v = yz[:, sl]
            r = lax.rsqrt(jnp.mean(v * v, axis=-1, keepdims=True) + EPS)
            o_ref[:, sl] = (v * r * w_ref[:, sl]).astype(BF16)

    return pl.pallas_call(
        kern, name="post_fwd", grid=(l // T,),
        in_specs=[pl.BlockSpec((T, DI), lambda i: (i, 0)), pl.BlockSpec((T, DI), lambda i: (i + nct, 0)),
                  pl.BlockSpec((T, DI), lambda i: (i, 0)), pl.BlockSpec((1, DI), lambda i: (0, 0)),
                  pl.BlockSpec((1, DI), lambda i: (0, 0))],
        out_specs=pl.BlockSpec((T, DI), lambda i: (i, 0)),
        out_shape=jax.ShapeDtypeStruct((l, DI), BF16),
        compiler_params=_cp(("parallel",)),
    )(yssm, xbc, proj_rest, dsk, gnw)


def _post_bwd(dgn, yssm, xbc, proj_rest, dsk, gnw, dpr, nct):
    l = yssm.shape[0]
    lext = xbc.shape[0]
    xi = lambda i: (jnp.maximum(i - nct, 0), 0)

    def kern(dg_ref, y_ref, x_ref, z_ref, dsk_ref, w_ref, dpr_ref, dy_ref, dz_ref, gw_ref, gd_ref):
        i = pl.program_id(0)

        @pl.when(i == 0)
        def _():
            gw_ref[...] = jnp.zeros_like(gw_ref)
            gd_ref[...] = jnp.zeros_like(gd_ref)

        @pl.when(i < nct)
        def _():
            dy_ref[...] = jnp.zeros_like(dy_ref)

        @pl.when(i >= nct)
        def _():
            xs = x_ref[...]
            z = z_ref[...].astype(F32)
            y = y_ref[...].astype(F32) + dsk_ref[...] * xs
            sz = _silu(z)
            yz = y * sz
            dgn_ = dg_ref[...].astype(F32)
            dyz_parts = []
            gws = []
            for g in range(NG):
                sl = slice(256 * g, 256 * (g + 1))
                v = yz[:, sl]
                r = lax.rsqrt(jnp.mean(v * v, axis=-1, keepdims=True) + EPS)
                vn = v * r
                dn = dgn_[:, sl] * w_ref[:, sl]
                gws.append(jnp.sum(dgn_[:, sl] * vn, axis=0, keepdims=True))
                dyz_parts.append(r * (dn - vn * jnp.mean(dn * vn, axis=-1, keepdims=True)))
            dyz = jnp.concatenate(dyz_parts, axis=1)
            gw_ref[...] += jnp.broadcast_to(jnp.concatenate(gws, axis=1), (8, DI))
            dy = dyz * sz
            dz_ref[...] = (dyz * y * _dsilu(z)).astype(BF16)
            gd_ref[...] += jnp.broadcast_to(jnp.sum(dy * xs, axis=0, keepdims=True), (8, DI))
            dy_ref[...] = dy.astype(BF16)

    return pl.pallas_call(
        kern, name="post_bwd", grid=(lext // T,),
        in_specs=[pl.BlockSpec((T, DI), xi), pl.BlockSpec((T, DI), xi), pl.BlockSpec((T, DI), lambda i: (i, 0)),
                  pl.BlockSpec((T, DI), xi), pl.BlockSpec((1, DI), lambda i: (0, 0)), pl.BlockSpec((1, DI), lambda i: (0, 0)),
                  pl.BlockSpec(memory_space=pl.ANY)],
        out_specs=[pl.BlockSpec((T, DI), lambda i: (i, 0)),
                   pl.BlockSpec((T, DI), xi), pl.BlockSpec((8, DI), lambda i: (0, 0)), pl.BlockSpec((8, DI), lambda i: (0, 0))],
        out_shape=[jax.ShapeDtypeStruct((lext, DI), BF16),
                   jax.ShapeDtypeStruct((l, RESTW), BF16), jax.ShapeDtypeStruct((8, DI), F32), jax.ShapeDtypeStruct((8, DI), F32)],
        input_output_aliases={6: 1},
        compiler_params=_cp(("arbitrary",)),
    )(dgn, yssm, xbc, proj_rest, dsk, gnw, dpr)


C_G1, C_G2, C_GA, C_GB, C_CG = 2, 3, 4, 5, 6
PITCH = GW + 16
NROW = T // GW


GAP = PITCH - GW
PADR = GAP + NROW * PITCH
NSTRIP = D // 128


def _fill_padded(pad8, val):
    z = jnp.zeros((GAP, D), F32)
    parts = [z]
    for r in range(NROW):
        parts += [val[GW * r:GW * (r + 1)], z]
    p = jnp.concatenate(parts, axis=0)
    pad8[0] = p
    for j in range(1, pad8.shape[0]):
        pad8[j] = pltpu.roll(p, PADR - j, axis=0)


def _tap(pad8, base, off, ln):
    return pad8[off % 8, pl.ds(base + off - off % 8, GW), ln]


def _row_conv(out_ref, pad8, w_ref, transpose):
    def strip(s, carry):
        ln = pl.ds(pl.multiple_of(s * 128, 128), 128)
        for r in range(NROW):
            base = GAP + PITCH * r
            acc = jnp.zeros((GW, 128), F32)
            for k in range(CK):
                off = (k - 15) if not transpose else (15 - k)
                acc = acc + w_ref[pl.ds(k, 1), ln] * _tap(pad8, base, off, ln)
            out_ref[pl.ds(GW * r, GW), ln] = acc
        return carry

    lax.fori_loop(0, NSTRIP, strip, 0)


def _row_conv_wgrad(gcw_ref, padd8, pada8):
    def strip(s, carry):
        ln = pl.ds(pl.multiple_of(s * 128, 128), 128)
        rid = _iota((32, 128), 0)
        g = jnp.zeros((32, 128), F32)
        for k0 in range(0, CK, 8):
            taps = range(k0, min(k0 + 8, CK))
            accs = {k: jnp.zeros((8, 128), F32) for k in taps}
            for r in range(NROW):
                base = GAP + PITCH * r
                d = _tap(padd8, base, 0, ln)
                for k in taps:
                    p = d * pada8[0, pl.ds(base + k - 15, GW), ln]
                    part = p[0:8]
                    for q in range(1, GW // 8):
                        part = part + p[8 * q:8 * (q + 1)]
                    accs[k] = accs[k] + part
            for k in taps:
                g = jnp.where(rid == k, jnp.sum(accs[k], axis=0, keepdims=True), g)
        gcw_ref[:, ln] += g
        return carry

    lax.fori_loop(0, NSTRIP, strip, 0)


def _ln_stats(cv):
    mu = jnp.mean(cv, axis=-1, keepdims=True)
    xc = cv - mu
    rs = lax.rsqrt(jnp.mean(xc * xc, axis=-1, keepdims=True) + EPS)
    return xc * rs, rs


def _conf_fwd(proj_rest, cw, cb, lw, lb):
    l = proj_rest.shape[0]

    def kern(ga_ref, gb_ref, cg_ref, cw_ref, cb_ref, lw_ref, lb_ref, o_ref, cv_ref, pad8):
        _fill_padded(pad8, ga_ref[...].astype(F32) * _sig(gb_ref[...].astype(F32)))
        _row_conv(cv_ref, pad8, cw_ref, False)
        cv = cv_ref[...] + cb_ref[...]
        cv_ref[...] = cv
        xh, _ = _ln_stats(cv)
        ln = xh * lw_ref[...] + lb_ref[...]
        o_ref[...] = (_silu(ln) * _silu(cg_ref[...].astype(F32))).astype(BF16)

    vec = pl.BlockSpec((1, D), lambda i: (0, 0))
    blk = pl.BlockSpec((T, D), lambda i: (i, 0))
    return pl.pallas_call(
        kern, name="conf_fwd", grid=(l // T,),
        in_specs=[pl.BlockSpec((T, D), lambda i: (i, C_GA)), pl.BlockSpec((T, D), lambda i: (i, C_GB)),
                  pl.BlockSpec((T, D), lambda i: (i, C_CG)), pl.BlockSpec((32, D), lambda i: (0, 0)), vec, vec, vec],
        out_specs=[blk, blk],
        out_shape=[jax.ShapeDtypeStruct((l, D), BF16), jax.ShapeDtypeStruct((l, D), F32)],
        scratch_shapes=[pltpu.VMEM((8, PADR, D), F32)],
        compiler_params=_cp(("parallel",)),
    )(proj_rest, proj_rest, proj_rest, cw, cb, lw, lb)


def _conf_bwd(duc, cv, proj_rest, cw, lw, lb, dpr):
    l = proj_rest.shape[0]

    def kern(du_ref, cv_ref, ga_ref, gb_ref, cg_ref, cw_ref, lw_ref, lb_ref, dpr_ref, o_ref, gcw_ref, gv_ref, sc,
             pada, padd, da_ref):
        i, j = pl.program_id(0), pl.program_id(1)

        @pl.when(jnp.logical_and(i == 0, j == 0))
        def _():
            gcw_ref[...] = jnp.zeros_like(gcw_ref)
            gv_ref[...] = jnp.zeros_like(gv_ref)

        @pl.when(j == 0)
        def _():
            ga, gb, cg = ga_ref[...].astype(F32), gb_ref[...].astype(F32), cg_ref[...].astype(F32)
            sg = _sig(gb)
            xh, rs = _ln_stats(cv_ref[...])
            ln = xh * lw_ref[...] + lb_ref[...]
            du = du_ref[...].astype(F32)
            sc[:, 2 * D:3 * D] = (du * _silu(ln) * _dsilu(cg)).astype(BF16)
            dln = du * _silu(cg) * _dsilu(ln)
            g_lw = jnp.sum(dln * xh, axis=0, keepdims=True)
            g_lb = jnp.sum(dln, axis=0, keepdims=True)
            dxh = dln * lw_ref[...]
            dcv = rs * (dxh - jnp.mean(dxh, axis=-1, keepdims=True) - xh * jnp.mean(dxh * xh, axis=-1, keepdims=True))
            g_cb = jnp.sum(dcv, axis=0, keepdims=True)
            rid = _iota((8, D), 0)
            gv_ref[...] += jnp.where(rid == 0, g_cb, jnp.where(rid == 1, g_lw, jnp.where(rid == 2, g_lb, 0.0)))
            _fill_padded(padd, dcv)
            _fill_padded(pada, ga * sg)
            _row_conv(da_ref, padd, cw_ref, True)
            _row_conv_wgrad(gcw_ref, padd, pada)
            da = da_ref[...]
            sc[:, 0:D] = (da * sg).astype(BF16)
            sc[:, D:2 * D] = (da * ga * sg * (1.0 - sg)).astype(BF16)

        o_ref[...] = sc[:, pl.ds(pl.multiple_of(j * D, 128), D)]

    vec = pl.BlockSpec((1, D), lambda i, j: (0, 0))
    col = lambda c: pl.BlockSpec((T, D), lambda i, j: (i, c))
    return pl.pallas_call(
        kern, name="conf_bwd", grid=(l // T, 3),
        in_specs=[col(0), col(0), col(C_GA), col(C_GB), col(C_CG), pl.BlockSpec((32, D), lambda i, j: (0, 0)), vec, vec,
                  pl.BlockSpec(memory_space=pl.ANY)],
        out_specs=[pl.BlockSpec((T, D), lambda i, j: (i, C_GA + j)), pl.BlockSpec((32, D), lambda i, j: (0, 0)),
                   pl.BlockSpec((8, D), lambda i, j: (0, 0))],
        out_shape=[jax.ShapeDtypeStruct((l, RESTW), BF16), jax.ShapeDtypeStruct((32, D), F32),
                   jax.ShapeDtypeStruct((8, D), F32)],
        scratch_shapes=[pltpu.VMEM((T, 3 * D), BF16), pltpu.VMEM((1, PADR, D), F32), pltpu.VMEM((8, PADR, D), F32),
                        pltpu.VMEM((T, D), F32)],
        input_output_aliases={8: 0},
        compiler_params=_cp(("arbitrary", "arbitrary")),
    )(duc, cv, proj_rest, proj_rest, proj_rest, cw, lw, lb, dpr)


def _merge_fwd(bs, bc, proj_rest):
    l = bs.shape[0]

    def kern(bs_ref, bc_ref, g1_ref, g2_ref, o_ref):
        up = lambda r: r[...].astype(F32)
        o_ref[...] = (_sig(up(g1_ref)) * up(bs_ref) + _sig(up(g2_ref)) * up(bc_ref)).astype(BF16)

    blk = pl.BlockSpec((TX, D), lambda i: (i, 0))
    return pl.pallas_call(
        kern, name="merge_fwd", grid=(l // TX,),
        in_specs=[blk, blk, pl.BlockSpec((TX, D), lambda i: (i, C_G1)), pl.BlockSpec((TX, D), lambda i: (i, C_G2))],
        out_specs=blk, out_shape=jax.ShapeDtypeStruct((l, D), BF16),
        compiler_params=_cp(("parallel",)),
    )(bs, bc, proj_rest, proj_rest)


def _merge_bwd(dm, bs, bc, proj_rest):
    l = bs.shape[0]

    def kern(dm_ref, bs_ref, bc_ref, g1_ref, g2_ref, dbs_ref, dbc_ref, dg_ref):
        up = lambda r: r[...].astype(F32)
        dm_ = up(dm_ref)
        s1, s2 = _sig(up(g1_ref)), _sig(up(g2_ref))
        dbs_ref[...] = (dm_ * s1).astype(BF16)
        dbc_ref[...] = (dm_ * s2).astype(BF16)
        dg_ref[:, 0:D] = (dm_ * up(bs_ref) * s1 * (1.0 - s1)).astype(BF16)
        dg_ref[:, D:2 * D] = (dm_ * up(bc_ref) * s2 * (1.0 - s2)).astype(BF16)

    blk = pl.BlockSpec((TX, D), lambda i: (i, 0))
    return pl.pallas_call(
        kern, name="merge_bwd", grid=(l // TX,),
        in_specs=[blk, blk, blk, pl.BlockSpec((TX, D), lambda i: (i, C_G1)), pl.BlockSpec((TX, D), lambda i: (i, C_G2))],
        out_specs=[blk, blk, pl.BlockSpec((TX, 2 * D), lambda i: (i, 1))],
        out_shape=[jax.ShapeDtypeStruct((l, D), BF16), jax.ShapeDtypeStruct((l, D), BF16),
                   jax.ShapeDtypeStruct((l, RESTW), BF16)],
        compiler_params=_cp(("parallel",)),
    )(dm, bs, bc, proj_rest, proj_rest)


def _final(x, out, tgt, mod, fw):
    l = x.shape[0]

    def kern(x_ref, o_ref, t_ref, mod_ref, fw_ref, ls_ref, dx2_ref, do_ref, gv_ref):
        @pl.when(pl.program_id(0) == 0)
        def _():
            ls_ref[...] = jnp.zeros_like(ls_ref)
            gv_ref[...] = jnp.zeros_like(gv_ref)

        gate = mod_ref[0:1, 2 * D:3 * D]
        o = o_ref[...]
        x2 = x_ref[...] + gate * o
        r = lax.rsqrt(jnp.mean(x2 * x2, axis=-1, keepdims=True) + EPS)
        yn = x2 * r
        fw_ = fw_ref[...]
        e = yn * fw_ - t_ref[...]
        ls_ref[...] += jnp.full((8, 128), 1.0, F32) * (0.5 / D) * jnp.sum(e * e)
        dy = e * (1.0 / D)
        g_fw = jnp.sum(dy * yn, axis=0, keepdims=True)
        dyn = dy * fw_
        dx2 = r * (dyn - yn * jnp.mean(dyn * yn, axis=-1, keepdims=True))
        g_gate = jnp.sum(dx2 * o, axis=0, keepdims=True)
        rid = _iota((8, D), 0)
        gv_ref[...] += jnp.where(rid == 0, g_fw, jnp.where(rid == 1, g_gate, 0.0))
        dx2_ref[...] = dx2
        do_ref[...] = (dx2 * gate).astype(BF16)

    blk = pl.BlockSpec((TX, D), lambda i: (i, 0))
    return pl.pallas_call(
        kern, name="final", grid=(l // TX,),
        in_specs=[blk, blk, blk, pl.BlockSpec((8, 3 * D), lambda i: (0, 0)), pl.BlockSpec((1, D), lambda i: (0, 0))],
        out_specs=[pl.BlockSpec((8, 128), lambda i: (0, 0)), blk, blk, pl.BlockSpec((8, D), lambda i: (0, 0))],
        out_shape=[jax.ShapeDtypeStruct((8, 128), F32), jax.ShapeDtypeStruct((l, D), F32),
                   jax.ShapeDtypeStruct((l, D), BF16), jax.ShapeDtypeStruct((8, D), F32)],
        compiler_params=_cp(("arbitrary",)),
    )(x, out, tgt, mod, fw)


def _perm_dt_cols(w):
    s = w.shape[:-1]
    return w.reshape(*s, 2, NG, HPG).swapaxes(-3, -2).reshape(*s, 64)


def _unperm_dt_cols(w):
    s = w.shape[:-1]
    return w.reshape(*s, NG, 2, HPG).swapaxes(-3, -2).reshape(*s, 64)


def _pad_lanes(v, width):
    return jnp.pad(v, ((0, 0), (0, width - v.shape[1])))


def _vcols(segs, a, b):
    parts, off = [], 0
    for s in segs:
        lo, hi = max(a, off), min(b, off + s.shape[1])
        if lo < hi:
            parts.append(s[:, lo - off:hi - off])
        off += s.shape[1]
    return parts[0] if len(parts) == 1 else jnp.concatenate(parts, axis=1)


def _local_step(x, c, ctx, tgt, w):
    l = x.shape[0]
    nct = CTX // T
    ncc = CTX // Q
    lext = l + CTX

    w_mod = w["w_mod"].astype(BF16)
    wsegs = [s.astype(BF16) for s in (w["w_in"] if isinstance(w["w_in"], (list, tuple)) else [w["w_in"]])]
    w_ssd = jnp.concatenate([_vcols(wsegs, 0, XBC), _perm_dt_cols(_vcols(wsegs, XBC, XBC + 64)), jnp.zeros((D, 64), BF16)], axis=1)
    r0 = XBC + 64
    w_rest = jnp.concatenate([_vcols(wsegs, r0, r0 + DI), _vcols(wsegs, r0 + DI + 3 * D, r0 + RESTW),
                              _vcols(wsegs, r0 + DI, r0 + DI + 3 * D)], axis=1)
    w_os, w_oc, w_o = w["w_out_ssm"].astype(BF16), w["w_out_conf"].astype(BF16), w["w_out"].astype(BF16)
    cw8 = jnp.pad(w["ssm_conv_w"], ((0, 4), (0, 0)))
    cb_s = w["ssm_conv_b"].reshape(1, XBC)
    dtb = _pad_lanes(_perm_dt_cols(w["dt_bias"].reshape(1, 64)), 128)
    a_all = -jnp.exp(w["a_log"].reshape(1, 64))
    a_perm = _pad_lanes(_perm_dt_cols(a_all), 128)
    a_rows = _pad_lanes(_perm_dt_cols(a_all).reshape(NG, 8), 128)
    dsk = jnp.repeat(w["d_skip"].reshape(NH), HP).reshape(1, DI)
    gnw = w["ssm_norm_w"].reshape(1, DI)
    ccw = jnp.pad(w["conf_conv_w"], ((0, 1), (0, 0)))
    ccb, clw, clb = w["conf_conv_b"].reshape(1, D), w["conf_ln_w"].reshape(1, D), w["conf_ln_b"].reshape(1, D)
    nw = w["norm_w"].reshape(1, D)
    fw = w["final_norm_w"].reshape(1, D)
    cc = jnp.concatenate([c.reshape(1, D), w["c_ctx"].reshape(1, D), jnp.zeros((6, D), F32)], axis=0)

    bx = min(1024, l)
    be = 768 if lext % 768 == 0 else 256
    tk = min(1024, l)
    mod = _mod_fwd(cc, w_mod, w["b_mod"].reshape(1, 3 * D))
    h = _norm_fwd(ctx, x, mod, nw, nct)
    hx = h[CTX:]
    proj_ssd = _mm(h, w_ssd, "nn", lext, SSDW, D, be, SSDW // 3, D, F32, "proj_ssd")
    proj_rest = _mm(hx, w_rest, "nn", l, RESTW, D, bx, 1024, D, BF16, "proj_rest")
    xbc, xbc16 = _conv_fwd(proj_ssd, cw8, cb_s, nct)
    dtg, lag, dtt, lat = _dt_fwd(proj_ssd, dtb, a_perm)
    htf, htb = _ssd_state(xbc16, dtg, lag, ncc)
    yssm = _ssd_out(xbc16, dtg, lag, dtt, lat, htf, htb, ncc)
    gn = _post_fwd(yssm, xbc, proj_rest, dsk, gnw, nct)
    bs = _mm(gn, w_os, "nn", l, D, DI, bx, D, DI, BF16, "out_ssm")
    uc, cv = _conf_fwd(proj_rest, ccw, ccb, clw, clb)
    bc = _mm(uc, w_oc, "nn", l, D, D, bx, D, D, BF16, "out_conf")
    merged = _merge_fwd(bs, bc, proj_rest)
    out = _mm(merged, w_o, "nn", l, D, D, bx, D, D, F32, "out_proj")
    lsum, dx2, dout, gv_fin = _final(x, out, tgt, mod, fw)

    g = {}
    g["final_norm_w"] = gv_fin[0]
    dmerged = _mm(dout, w_o, "nt", l, D, D, bx, D, D, BF16, "d_merged")
    g["w_out"] = _mm(merged, dout, "tn", D, D, l, D, D, tk, F32, "g_w_out")
    dbs, dbc, dpr = _merge_bwd(dmerged, bs, bc, proj_rest)
    dgn = _mm(dbs, w_os, "nt", l, DI, D, bx, DI, D, BF16, "d_gn")
    g["w_out_ssm"] = _mm(gn, dbs, "tn", DI, D, l, DI, D, tk, F32, "g_w_out_ssm")
    duc = _mm(dbc, w_oc, "nt", l, D, D, bx, D, D, BF16, "d_uc")
    g["w_out_conf"] = _mm(uc, dbc, "tn", D, D, l, D, D, tk, F32, "g_w_out_conf")
    dpr, gcw, gv_conf = _conf_bwd(duc, cv, proj_rest, ccw, clw, clb, dpr)
    g["conf_conv_w"] = gcw[:CK]
    g["conf_conv_b"], g["conf_ln_w"], g["conf_ln_b"] = gv_conf[0], gv_conf[1], gv_conf[2]
    dy, dproj_rest, ggnw, gdsk = _post_bwd(dgn, yssm, xbc, proj_rest, dsk, gnw, dpr, nct)
    g["ssm_norm_w"] = ggnw[0]
    g["d_skip"] = gdsk[0].reshape(NH, HP).sum(axis=1)
    dhf, dhb = _ssd_bwd_state(xbc16, dy, lag, ncc)
    dxs, dbm, dcm, ddtg, galog = _ssd_bwd_out(xbc, dy, dsk, dtg, lag, dtt, lat, htf, htb, dhf, dhb, a_rows)
    g["a_log"] = _unperm_dt_cols(galog[:, 0, 0:8].reshape(1, 64)).reshape(2, NH)
    dus, gws, gbs = [], [], []
    for dpost, off, width, nm in ((dxs, 0, DI, "conv_bwd_x"), (dbm, DI, NG * NS, "conv_bwd_b"), (dcm, DI + NG * NS, NG * NS, "conv_bwd_c")):
        du_, gw_, gb_ = _conv_bwd(dpost, proj_ssd, cw8, cb_s, off, width, nct, nm)
        dus.append(du_)
        gws.append(gw_[:SK])
        gbs.append(gb_[0])
    g["ssm_conv_w"] = jnp.concatenate(gws, axis=1)
    g["ssm_conv_b"] = jnp.concatenate(gbs, axis=0)
    ddt_raw, gdtb = _dt_bwd(ddtg, proj_ssd, dtb)
    g["dt_bias"] = _unperm_dt_cols(gdtb[0:1, 0:64]).reshape(2, NH)
    dproj_ssd = jnp.concatenate(dus + [ddt_raw], axis=1)
    gw_ssd = _mm(h, dproj_ssd, "tn", D, SSDW, lext, D, SSDW // 3, be, F32, "g_w_ssd")
    gw_rest = _mm(hx, dproj_rest, "tn", D, RESTW, l, D, 1024, tk, F32, "g_w_rest")
    gsegs = [gw_ssd[:, :XBC], _unperm_dt_cols(gw_ssd[:, XBC:XBC + 64]), gw_rest[:, :DI], gw_rest[:, 2 * DI:],
             gw_rest[:, DI:2 * DI]]
    g["w_in"] = jnp.concatenate(gsegs, axis=1)
    g["w_in_shards"] = jnp.stack([_vcols(gsegs, R_IN * s, R_IN * (s + 1)) for s in range(NSHARD)])
    dh_a = _mm(dproj_ssd, w_ssd, "nt", lext, D, SSDW, T, D, SSDW, BF16, "dh_ssd")
    dh_b = _mm(dproj_rest, w_rest, "nt", l, D, RESTW, T, D, RESTW, BF16, "dh_rest")
    grad_x, gnw_in, dss = _norm_bwd(dh_a, dh_b, ctx, x, dx2, mod, nw, nct)
    g["norm_w"] = gnw_in[0]
    dmod = jnp.concatenate([jnp.concatenate([dss[0:1], gv_fin[1:2]], axis=1),
                            jnp.concatenate([dss[1:2], jnp.zeros((1, D), F32)], axis=1),
                            jnp.zeros((6, 3 * D), F32)], axis=0)
    gwm, gbm, gcc = _mod_bwd(dmod, cc, cc.T, w_mod)
    g["w_mod"], g["b_mod"], g["c_ctx"] = gwm, gbm[0], gcc[1]
    return lsum[0, 0], grad_x, g


NSHARD = 4
R_MOD, R_IN, R_OS, R_OC, R_O, R_SC, R_CC = 768, 2832, 512, 256, 256, 8, 8
O_MOD = 0
O_OS = O_MOD + R_MOD
O_OC = O_OS + R_OS
O_O = O_OC + R_OC
O_SC = O_O + R_O
O_CC = O_SC + R_SC
PUSED = O_CC + R_CC
PROWS = 1824
HALF = PROWS // 2
RB = HALF // 3
WB = 128
SROWS = 16
SMALL = (("b_mod", 3 * D), ("norm_w", D), ("ssm_conv_b", XBC), ("dt_bias", 64), ("a_log", 64), ("d_skip", NH),
         ("ssm_norm_w", DI), ("conf_conv_b", D), ("conf_ln_w", D), ("conf_ln_b", D), ("final_norm_w", D), ("c_ctx", D))


def _pack_shard(s):
    return jnp.concatenate([s["w_mod"].reshape(R_MOD, D), _pack_rest(s), jnp.zeros((PROWS - PUSED, D), F32)], axis=0)


def _pack_rest(s):
    cc = jnp.pad(s["conf_conv_w"].reshape(1, CK * 256), ((0, 0), (0, R_CC * D - CK * 256))).reshape(R_CC, D)
    return jnp.concatenate([s["w_out_ssm"], s["w_out_conf"], s["w_out"],
                            jnp.pad(s["ssm_conv_w"], ((0, R_SC - SK), (0, 0))), cc], axis=0)


def _unpack_rest(p):
    o = lambda r: r - O_OS
    return {"w_out_ssm": p[o(O_OS):o(O_OC)][None], "w_out_conf": p[o(O_OC):o(O_O)][None], "w_out": p[o(O_O):o(O_SC)][None],
            "ssm_conv_w": p[o(O_SC):o(O_SC) + SK][None],
            "conf_conv_w": p[o(O_CC):o(O_CC) + R_CC].reshape(R_CC * D)[:CK * 256].reshape(1, CK, 256)}


def _shard_cols(a, n):
    return a.reshape(a.shape[0], NSHARD, n).transpose(1, 0, 2)


def _pack_full(g):
    cc = jnp.pad(_shard_cols(g["conf_conv_w"], 256).reshape(NSHARD, CK * 256), ((0, 0), (0, R_CC * D - CK * 256)))
    return jnp.concatenate([_shard_cols(g["w_mod"], R_MOD).reshape(NSHARD, R_MOD, D),
                            g["w_out_ssm"].reshape(NSHARD, R_OS, D), g["w_out_conf"].reshape(NSHARD, R_OC, D),
                            g["w_out"].reshape(NSHARD, R_O, D),
                            jnp.pad(_shard_cols(g["ssm_conv_w"], D), ((0, 0), (0, R_SC - SK), (0, 0))),
                            cc.reshape(NSHARD, R_CC, D), jnp.zeros((NSHARD, PROWS - PUSED, D), F32)], axis=1)


def _unpack_gathered(gm, gw, gs):
    def cols(a, r, n):
        return a.reshape(NSHARD, r, n).transpose(1, 0, 2).reshape(r, NSHARD * n)
    return {"w_mod": cols(gm[:, O_MOD:O_OS], D, R_MOD), "w_in": [gw[s] for s in range(NSHARD)],
            "w_out_ssm": gm[:, O_OS:O_OC].reshape(DI, D), "w_out_conf": gm[:, O_OC:O_O].reshape(D, D),
            "w_out": gm[:, O_O:O_SC].reshape(D, D), "ssm_conv_w": cols(gs[:, 0:SK], SK, D),
            "conf_conv_w": cols(gs[:, R_SC:R_SC + R_CC].reshape(NSHARD, R_CC * D)[:, :CK * 256], CK, 256)}


MESH_ID = pl.DeviceIdType.MESH
ANY = pl.BlockSpec(memory_space=pl.ANY)


def _place():
    x, y, c = lax.axis_index("x"), lax.axis_index("y"), lax.axis_index("c")
    return x, y, c, [(1 - x, y), (x, 1 - y), (1 - x, 1 - y)]


def _rcopy(src, dst, send, recv, dev):
    return pltpu.make_async_remote_copy(src_ref=src, dst_ref=dst, send_sem=send, recv_sem=recv,
                                        device_id=dev, device_id_type=MESH_ID)


def _gather_weights(mats, small):
    n = len(mats)

    def kern(*refs):
        m_refs, s_ref, g_refs, gs_ref, (send, recv) = refs[:n], refs[n], refs[n + 1:2 * n + 1], refs[2 * n + 1], refs[2 * n + 2:]
        x, y, c, chips = _place()
        me = 2 * x + y
        sib = (x, y, 1 - c)
        first, passed = [], []
        for k, (px, py) in enumerate(chips):
            first.append(_rcopy(s_ref, gs_ref.at[me], send.at[k], recv.at[k], (px, py, c)))
            for a, (m_ref, g_ref) in enumerate(zip(m_refs, g_refs)):
                mine = _half_rows(c, m_ref.shape[0])
                first.append(_rcopy(m_ref.at[mine], g_ref.at[me, mine], send.at[3 + 6 * a + k], recv.at[3 + 6 * a + k], (px, py, c)))
        for cp in first:
            cp.start()
        for k, (px, py) in enumerate(chips):
            s = 2 * px + py
            for a, (m_ref, g_ref) in enumerate(zip(m_refs, g_refs)):
                mine = _half_rows(c, m_ref.shape[0])
                _rcopy(m_ref.at[mine], g_ref.at[s, mine], send.at[3 + 6 * a + k], recv.at[3 + 6 * a + k], sib).wait_recv()
                f = _rcopy(g_ref.at[s, mine], g_ref.at[s, mine], send.at[6 + 6 * a + k], recv.at[6 + 6 * a + k], sib)
                f.start()
                passed.append(f)
        for k, (px, py) in enumerate(chips):
            s = 2 * px + py
            _rcopy(s_ref, gs_ref.at[s], send.at[k], recv.at[k], sib).wait_recv()
            for a, g_ref in enumerate(g_refs):
                other = _half_rows(1 - c, g_ref.shape[1])
                _rcopy(g_ref.at[s, other], g_ref.at[s, other], send.at[6 + 6 * a + k], recv.at[6 + 6 * a + k], sib).wait_recv()
        for cp in first + passed:
            cp.wait_send()

    nsem = 3 + 6 * n
    return pl.pallas_call(
        kern, name="gather_weights", in_specs=[ANY] * (n + 1), out_specs=[ANY] * (n + 1),
        out_shape=[jax.ShapeDtypeStruct((NSHARD,) + m.shape, m.dtype) for m in mats]
        + [jax.ShapeDtypeStruct((NSHARD, SROWS, D), F32)],
        scratch_shapes=[pltpu.SemaphoreType.DMA((nsem,)), pltpu.SemaphoreType.DMA((nsem,))],
    )(*mats, small)


def _half_rows(c, rows):
    return pl.ds(pl.multiple_of(c * (rows // 2), 16), rows // 2)


def _swap_halves(gs):
    n = len(gs)

    def kern(*refs):
        g_refs, o_refs, (send, recv) = refs[:n], refs[n:2 * n], refs[2 * n:]
        x, y, c, _ = _place()
        cps = [_rcopy(g_ref.at[s, _half_rows(1 - c, g_ref.shape[1])], o_ref.at[s], send.at[NSHARD * a + s],
                      recv.at[NSHARD * a + s], (x, y, 1 - c))
               for a, (g_ref, o_ref) in enumerate(zip(g_refs, o_refs)) for s in range(NSHARD)]
        for cp in cps:
            cp.start()
        for cp in cps:
            cp.wait()

    return pl.pallas_call(
        kern, name="swap_halves", in_specs=[ANY] * n, out_specs=[ANY] * n,
        out_shape=[jax.ShapeDtypeStruct((NSHARD, g.shape[1] // 2, g.shape[2]), F32) for g in gs],
        scratch_shapes=[pltpu.SemaphoreType.DMA((NSHARD * n,)), pltpu.SemaphoreType.DMA((NSHARD * n,))],
    )(*gs)


def _add_halves(cidx, g, ra, rb, name):
    _, half, cols = ra.shape
    nb = half // rb

    def kern(c_ref, g_ref, a_ref, o_ref):
        o_ref[...] = (g_ref[...] + a_ref[...]).astype(BF16)

    return pl.pallas_call(
        kern, name=name,
        grid_spec=pltpu.PrefetchScalarGridSpec(
            num_scalar_prefetch=1, grid=(NSHARD, nb),
            in_specs=[pl.BlockSpec((None, rb, cols), lambda s, i, c: (s, c[0] * nb + i, 0)),
                      pl.BlockSpec((None, rb, cols), lambda s, i, c: (s, i, 0))],
            out_specs=pl.BlockSpec((None, rb, cols), lambda s, i, c: (s, i, 0))),
        out_shape=jax.ShapeDtypeStruct((NSHARD, half, cols), BF16),
        compiler_params=_cp(("parallel", "parallel")),
    )(cidx, g, ra)


def _exchange_chips(ps):
    n = len(ps)

    def kern(*refs):
        p_refs, o_refs, (send, recv) = refs[:n], refs[n:2 * n], refs[2 * n:]
        x, y, c, chips = _place()
        cps = [_rcopy(p_ref.at[2 * px + py], o_ref.at[k], send.at[3 * a + k], recv.at[3 * a + k], (px, py, c))
               for a, (p_ref, o_ref) in enumerate(zip(p_refs, o_refs)) for k, (px, py) in enumerate(chips)]
        for cp in cps:
            cp.start()
        for cp in cps:
            cp.wait()

    return pl.pallas_call(
        kern, name="exchange_chips", in_specs=[ANY] * n, out_specs=[ANY] * n,
        out_shape=[jax.ShapeDtypeStruct((3,) + p.shape[1:], p.dtype) for p in ps],
        scratch_shapes=[pltpu.SemaphoreType.DMA((3 * n,)), pltpu.SemaphoreType.DMA((3 * n,))],
    )(*ps)


def _add_chips(mc, g, ra, rx, rb, name):
    _, half, cols = ra.shape
    nb = half // rb

    def kern(m_ref, g_ref, a_ref, r0_ref, r1_ref, r2_ref, o_ref):
        own = g_ref[...] + a_ref[...]
        o_ref[...] = ((own + r0_ref[...].astype(F32)) + r1_ref[...].astype(F32)) + r2_ref[...].astype(F32)

    return pl.pallas_call(
        kern, name=name,
        grid_spec=pltpu.PrefetchScalarGridSpec(
            num_scalar_prefetch=1, grid=(nb,),
            in_specs=[pl.BlockSpec((None, rb, cols), lambda i, m: (m[0], m[1] * nb + i, 0)),
                      pl.BlockSpec((None, rb, cols), lambda i, m: (m[0], i, 0))]
            + [pl.BlockSpec((None, rb, cols), functools.partial(lambda i, m, k: (k, i, 0), k=k)) for k in range(3)],
            out_specs=pl.BlockSpec((rb, cols), lambda i, m: (i, 0))),
        out_shape=jax.ShapeDtypeStruct((half, cols), F32),
        compiler_params=_cp(("parallel",)),
    )(mc, g, ra, rx, rx, rx)


def _share_halves(rs):
    n = len(rs)

    def kern(*refs):
        r_refs, o_refs, (send, recv) = refs[:n], refs[n:2 * n], refs[2 * n:]
        x, y, c, _ = _place()
        cps = [_rcopy(r_ref, o_ref, send.at[a], recv.at[a], (x, y, 1 - c))
               for a, (r_ref, o_ref) in enumerate(zip(r_refs, o_refs))]
        for cp in cps:
            cp.start()
        for cp in cps:
            cp.wait()

    return pl.pallas_call(
        kern, name="share_halves", in_specs=[ANY] * n, out_specs=[ANY] * n,
        out_shape=[jax.ShapeDtypeStruct(r.shape, F32) for r in rs],
        scratch_shapes=[pltpu.SemaphoreType.DMA((n,)), pltpu.SemaphoreType.DMA((n,))],
    )(*rs)


SMALL_W = XBC


def _small_update(gs, ws, ms, vs):
    n = len(gs)
    widths = [g.shape[1] for g in gs]
    assert n <= SROWS and max(widths) <= SMALL_W

    def kern(*refs):
        g_refs, w_refs, m_refs, v_refs = (refs[n * i:n * (i + 1)] for i in range(4))
        o_g, o_d, o_m, o_v = (refs[n * (4 + i):n * (5 + i)] for i in range(4))
        buf, send, recv = refs[8 * n:]
        x, y, c, _ = _place()
        me = 4 * x + 2 * y + c
        buf[me] = jnp.zeros((SROWS, SMALL_W), F32)
        for k, g_ref in enumerate(g_refs):
            buf[me, k:k + 1, 0:widths[k]] = g_ref[...]
        cps = []
        for r in range(1, 8):
            peer = (1 - x if r & 4 else x, 1 - y if r & 2 else y, 1 - c if r & 1 else c)
            cps.append(_rcopy(buf.at[me], buf.at[me], send.at[r - 1], recv.at[r - 1], peer))
        for cp in cps:
            cp.start()
        for cp in cps:
            cp.wait()
        acc = buf[0]
        for i in range(1, 8):
            acc = acc + buf[i]
        for k in range(n):
            g_ = acc[k:k + 1, 0:widths[k]]
            m_ = ADAM_B1 * m_refs[k][...] + (1.0 - ADAM_B1) * g_
            v_ = ADAM_B2 * v_refs[k][...] + (1.0 - ADAM_B2) * jnp.square(g_)
            m_hat = m_ / (1.0 - ADAM_B1 ** ADAM_STEP)
            v_hat = v_ / (1.0 - ADAM_B2 ** ADAM_STEP)
            o_g[k][...] = g_
            o_d[k][...] = -ADAM_LR * (m_hat / (jnp.sqrt(v_hat) + ADAM_EPS) + ADAM_WD * w_refs[k][...])
            o_m[k][...] = m_
            o_v[k][...] = v_

    vm = pl.BlockSpec(memory_space=pltpu.VMEM)
    outs = pl.pallas_call(
        kern, name="small_update", in_specs=[vm] * (4 * n), out_specs=[vm] * (4 * n),
        out_shape=[jax.ShapeDtypeStruct((1, wd), F32) for _ in range(4) for wd in widths],
        scratch_shapes=[pltpu.VMEM((8, SROWS, SMALL_W), F32), pltpu.SemaphoreType.DMA((7,)), pltpu.SemaphoreType.DMA((7,))],
    )(*gs, *ws, *ms, *vs)
    return [outs[n * i:n * (i + 1)] for i in range(4)]


def _adamw(g, w, m, v, rb, name):
    rows, cols = g.shape

    def kern(g_ref, w_ref, m_ref, v_ref, d_ref, nm_ref, nv_ref):
        g_ = g_ref[...]
        m_ = ADAM_B1 * m_ref[...] + (1.0 - ADAM_B1) * g_
        v_ = ADAM_B2 * v_ref[...] + (1.0 - ADAM_B2) * jnp.square(g_)
        m_hat = m_ / (1.0 - ADAM_B1 ** ADAM_STEP)
        v_hat = v_ / (1.0 - ADAM_B2 ** ADAM_STEP)
        d_ref[...] = -ADAM_LR * (m_hat / (jnp.sqrt(v_hat) + ADAM_EPS) + ADAM_WD * w_ref[...])
        nm_ref[...] = m_
        nv_ref[...] = v_

    assert rows % rb == 0
    blk = pl.BlockSpec((rb, cols), lambda i: (i, 0))
    return pl.pallas_call(
        kern, name=name, grid=(rows // rb,), in_specs=[blk] * 4, out_specs=[blk] * 3,
        out_shape=[jax.ShapeDtypeStruct((rows, cols), F32)] * 3,
        compiler_params=_cp(("parallel",)),
    )(g, w, m, v)


def _adamw_halves(cidx, mine, other, w, m, v, rb, name):
    rows, cols = w.shape
    nbh = rows // 2 // rb

    def kern(c_ref, a_ref, b_ref, w_ref, m_ref, v_ref, g_ref, d_ref, nm_ref, nv_ref):
        g_ = jnp.where(pl.program_id(0) // nbh == c_ref[0], a_ref[...], b_ref[...])
        m_ = ADAM_B1 * m_ref[...] + (1.0 - ADAM_B1) * g_
        v_ = ADAM_B2 * v_ref[...] + (1.0 - ADAM_B2) * jnp.square(g_)
        m_hat = m_ / (1.0 - ADAM_B1 ** ADAM_STEP)
        v_hat = v_ / (1.0 - ADAM_B2 ** ADAM_STEP)
        g_ref[...] = g_
        d_ref[...] = -ADAM_LR * (m_hat / (jnp.sqrt(v_hat) + ADAM_EPS) + ADAM_WD * w_ref[...])
        nm_ref[...] = m_
        nv_ref[...] = v_

    half = pl.BlockSpec((rb, cols), lambda i, c: (i % nbh, 0))
    blk = pl.BlockSpec((rb, cols), lambda i, c: (i, 0))
    return pl.pallas_call(
        kern, name=name,
        grid_spec=pltpu.PrefetchScalarGridSpec(num_scalar_prefetch=1, grid=(2 * nbh,), in_specs=[half, half, blk, blk, blk],
                                               out_specs=[blk] * 4),
        out_shape=[jax.ShapeDtypeStruct((rows, cols), F32)] * 4,
        compiler_params=_cp(("parallel",)),
    )(cidx, mine, other, w, m, v)


WEIGHTS = ("c_ctx", "w_mod", "b_mod", "norm_w", "w_in", "ssm_conv_w", "ssm_conv_b", "dt_bias", "a_log", "d_skip",
           "ssm_norm_w", "w_out_ssm", "conf_conv_w", "conf_conv_b", "conf_ln_w", "conf_ln_b", "w_out_conf", "w_out",
           "final_norm_w")


def kernel(x, c, ctx, c_ctx, w_mod, b_mod, norm_w, w_in, ssm_conv_w, ssm_conv_b, dt_bias, a_log, d_skip, ssm_norm_w, w_out_ssm, conf_conv_w, conf_conv_b, conf_ln_w, conf_ln_b, w_out_conf, w_out, final_norm_w, loss_target, m_c_ctx, m_w_mod, m_b_mod, m_norm_w, m_w_in, m_ssm_conv_w, m_ssm_conv_b, m_dt_bias, m_a_log, m_d_skip, m_ssm_norm_w, m_w_out_ssm, m_conf_conv_w, m_conf_conv_b, m_conf_ln_w, m_conf_ln_b, m_w_out_conf, m_w_out, m_final_norm_w, v_c_ctx, v_w_mod, v_b_mod, v_norm_w, v_w_in, v_ssm_conv_w, v_ssm_conv_b, v_dt_bias, v_a_log, v_d_skip, v_ssm_norm_w, v_w_out_ssm, v_conf_conv_w, v_conf_conv_b, v_conf_ln_w, v_conf_ln_b, v_w_out_conf, v_w_out, v_final_norm_w):
    wv = (c_ctx, w_mod, b_mod, norm_w, w_in, ssm_conv_w, ssm_conv_b, dt_bias, a_log, d_skip, ssm_norm_w, w_out_ssm,
          conf_conv_w, conf_conv_b, conf_ln_w, conf_ln_b, w_out_conf, w_out, final_norm_w)
    mv = (m_c_ctx, m_w_mod, m_b_mod, m_norm_w, m_w_in, m_ssm_conv_w, m_ssm_conv_b, m_dt_bias, m_a_log, m_d_skip,
          m_ssm_norm_w, m_w_out_ssm, m_conf_conv_w, m_conf_conv_b, m_conf_ln_w, m_conf_ln_b, m_w_out_conf, m_w_out,
          m_final_norm_w)
    vv = (v_c_ctx, v_w_mod, v_b_mod, v_norm_w, v_w_in, v_ssm_conv_w, v_ssm_conv_b, v_dt_bias, v_a_log, v_d_skip,
          v_ssm_norm_w, v_w_out_ssm, v_conf_conv_w, v_conf_conv_b, v_conf_ln_w, v_conf_ln_b, v_w_out_conf, v_w_out,
          v_final_norm_w)
    shapes = {n: a.shape for n, a in zip(WEIGHTS, wv)}

    def squeeze(d):
        return {n: (a if n in ("c_ctx", "final_norm_w") else a[0]) for n, a in d.items()}

    w, m, v = (squeeze(dict(zip(WEIGHTS, t))) for t in (wv, mv, vv))

    my_chip = 2 * lax.axis_index("x") + lax.axis_index("y")
    my_core = lax.axis_index("c")

    pw = _pack_shard(w)
    pwb, wib, psm = pw.astype(BF16), w["w_in"].astype(BF16), pw[O_SC:O_SC + SROWS]
    gm, gw, gs = _gather_weights([pwb, wib], psm)
    mine = (jnp.arange(NSHARD) == my_chip)[:, None, None]
    gm, gw, gs = jnp.where(mine, pwb[None], gm), jnp.where(mine, wib[None], gw), jnp.where(mine, psm[None], gs)
    full = dict(w)
    full.update(_unpack_gathered(gm, gw, gs))

    lsum, grad_x, g = _local_step(x[0], c, ctx[0], loss_target[0], full)
    loss = lax.psum(lsum, ("x", "y", "c"))

    cidx = my_core.astype(jnp.int32).reshape(1)
    mc = jnp.stack([my_chip, my_core]).astype(jnp.int32)
    gsrc = [_pack_full(g), g["w_in_shards"]]
    blocks = (RB, WB)
    sib = _swap_halves(gsrc)
    part = [_add_halves(cidx, a, b, rb, "add_halves_%d" % i) for i, (a, b, rb) in enumerate(zip(gsrc, sib, blocks))]
    far = _exchange_chips(part)
    red = [_add_chips(mc, a, b, f, rb, "add_chips_%d" % i) for i, (a, b, f, rb) in enumerate(zip(gsrc, sib, far, blocks))]
    got = _share_halves(red)
    g_pk = jnp.concatenate([jnp.where(my_core == 0, red[0], got[0]), jnp.where(my_core == 0, got[0], red[0])], axis=0)
    small = [name for name, _ in SMALL]
    as_row = lambda d: [d[name].reshape(1, -1) for name in small]
    res_sm = _small_update(as_row(g), as_row(w), as_row(m), as_row(v))

    gr = {"w_mod": g_pk[O_MOD:O_OS].reshape(D, R_MOD), "rest": g_pk[O_OS:PUSED]}
    wr, mr, vr = ({"w_mod": t["w_mod"], "rest": _pack_rest(t)} for t in (w, m, v))
    res = {k: _adamw(gr[k], wr[k], mr[k], vr[k], rb, "adamw_" + k)
           for k, rb in (("w_mod", 512), ("rest", (PUSED - O_OS) // 2))}
    gr["w_in"], *res["w_in"] = _adamw_halves(cidx, red[1], got[1], w["w_in"], m["w_in"], v["w_in"], WB, "adamw_w_in")

    outs = []
    for i in range(4):
        pick = (lambda k: gr[k]) if i == 0 else (lambda k: res[k][i - 1])
        d = {"w_mod": pick("w_mod")[None], "w_in": pick("w_in")[None]}
        d.update(_unpack_rest(pick("rest")))
        d.update({name: a.reshape(shapes[name]) for name, a in zip(small, res_sm[i])})
        outs.extend(d[n] for n in WEIGHTS)
    return (loss, grad_x[None], *outs)
```

```python
import functools

import jax
import jax.numpy as jnp
from jax import lax
from jax.experimental import pallas as pl
from jax.experimental.pallas import tpu as pltpu

F32, BF16 = jnp.float32, jnp.bfloat16

D = 1024
DI = 2048
NH = 32
HP = 64
NG = 8
HPG = 4
NS = 128
Q = 128
GW = 64
CK = 31
SK = 4
CTX = 256
EPS = 1e-6
XBC = DI + 2 * NG * NS
SSDW = XBC + 128
RESTW = 7168
T = 256
TX = 512
VMEM_LIMIT = 56 * 1024 * 1024

ADAM_LR, ADAM_B1, ADAM_B2, ADAM_EPS, ADAM_WD, ADAM_STEP = 0.001, 0.9, 0.999, 1e-08, 0.01, 10


def _cp(sem):
    return pltpu.CompilerParams(dimension_semantics=sem, vmem_limit_bytes=VMEM_LIMIT)


def _sig(x):
    return jax.nn.sigmoid(x)


def _silu(x):
    return x * _sig(x)


def _dsilu(x):
    s = _sig(x)
    return s * (1.0 + x * (1.0 - s))


def _dot(a, b):
    return jnp.dot(a, b, preferred_element_type=F32)


def _dot_nt(a, b):
    return lax.dot_general(a, b, (((1,), (1,)), ((), ())), preferred_element_type=F32)


def _split3(x):
    h = x.astype(BF16)
    r = x - h.astype(F32)
    m = r.astype(BF16)
    l = (r - m.astype(F32)).astype(BF16)
    return h, m, l


def _dot3_l(sel, x):
    h, m, l = _split3(x)
    return _dot(sel, h) + _dot(sel, m) + _dot(sel, l)


def _dot3_r(x, sel):
    h, m, l = _split3(x)
    return _dot(h, sel) + _dot(m, sel) + _dot(l, sel)


def _split2(x):
    h = x.astype(BF16)
    return h, (x - h.astype(F32)).astype(BF16)


def _dot2_l(sel, x):
    h, l = _split2(x)
    return _dot(sel, h) + _dot(sel, l)


def _dot2_r(x, sel):
    h, l = _split2(x)
    return _dot(h, sel) + _dot(l, sel)


def _iota(shape, dim):
    return lax.broadcasted_iota(jnp.int32, shape, dim)


def _mm(a, b, dims, m, n, k, bm, bn, bk, out_dtype, name):
    nk = k // bk
    assert m % bm == 0 and n % bn == 0 and k % bk == 0, (name, m, n, k, bm, bn, bk)

    def prod(a_ref, b_ref):
        av = a_ref[...].astype(BF16)
        bv = b_ref[...].astype(BF16)
        if dims == "nn":
            return _dot(av, bv)
        if dims == "nt":
            return _dot_nt(av, bv)
        return lax.dot_general(av, bv, (((0,), (0,)), ((), ())), preferred_element_type=F32)

    def kern_one(a_ref, b_ref, o_ref):
        o_ref[...] = prod(a_ref, b_ref).astype(out_dtype)

    def kern_acc(a_ref, b_ref, o_ref, acc):
        kk = pl.program_id(2)

        @pl.when(kk == 0)
        def _():
            acc[...] = jnp.zeros_like(acc)

        acc[...] += prod(a_ref, b_ref)

        @pl.when(kk == nk - 1)
        def _():
            o_ref[...] = acc[...].astype(out_dtype)

    if dims == "nn":
        a_spec = pl.BlockSpec((bm, bk), lambda j, i, kk: (i, kk))
        b_spec = pl.BlockSpec((bk, bn), lambda j, i, kk: (kk, j))
    elif dims == "nt":
        a_spec = pl.BlockSpec((bm, bk), lambda j, i, kk: (i, kk))
        b_spec = pl.BlockSpec((bn, bk), lambda j, i, kk: (j, kk))
    else:
        a_spec = pl.BlockSpec((bk, bm), lambda j, i, kk: (kk, i))
        b_spec = pl.BlockSpec((bk, bn), lambda j, i, kk: (kk, j))
    return pl.pallas_call(
        kern_one if nk == 1 else kern_acc, name=name,
        grid=(n // bn, m // bm, nk),
        in_specs=[a_spec, b_spec],
        out_specs=pl.BlockSpec((bm, bn), lambda j, i, kk: (i, j)),
        out_shape=jax.ShapeDtypeStruct((m, n), out_dtype),
        scratch_shapes=[] if nk == 1 else [pltpu.VMEM((bm, bn), F32)],
        compiler_params=_cp(("parallel", "parallel", "arbitrary")),
    )(a, b)


def _mod_fwd(cc, w_mod, b_mod):
    def kern(cc_ref, w_ref, b_ref, o_ref):
        s = _silu(cc_ref[...]).astype(BF16)
        o_ref[...] = _dot(s, w_ref[...]) + b_ref[...]

    return pl.pallas_call(
        kern, name="mod_fwd", grid=(3,),
        in_specs=[pl.BlockSpec((8, D), lambda j: (0, 0)), pl.BlockSpec((D, D), lambda j: (0, j)),
                  pl.BlockSpec((1, D), lambda j: (0, j))],
        out_specs=pl.BlockSpec((8, D), lambda j: (0, j)),
        out_shape=jax.ShapeDtypeStruct((8, 3 * D), F32),
        compiler_params=_cp(("parallel",)),
    )(cc, w_mod, b_mod)


def _mod_bwd(dmod, cc, cct, w_mod):
    def kern(dm_ref, cc_ref, cct_ref, w_ref, gw_ref, gb_ref, gc_ref):
        kk = pl.program_id(0)
        dm = dm_ref[...]
        sct = _silu(cct_ref[...])
        gw_ref[...] = sct[:, 0:1] * dm[0:1, :] + sct[:, 1:2] * dm[1:2, :]
        gb_ref[...] = jnp.broadcast_to(dm[0:1, :] + dm[1:2, :], dm.shape)

        @pl.when(kk == 0)
        def _():
            gc_ref[...] = jnp.zeros_like(gc_ref)

        gc_ref[...] += _dot_nt(dm.astype(BF16), w_ref[...])

        @pl.when(kk == 2)
        def _():
            gc_ref[...] = gc_ref[...] * _dsilu(cc_ref[...])

    return pl.pallas_call(
        kern, name="mod_bwd", grid=(3,),
        in_specs=[pl.BlockSpec((8, D), lambda j: (0, j)), pl.BlockSpec((8, D), lambda j: (0, 0)),
                  pl.BlockSpec((D, 8), lambda j: (0, 0)), pl.BlockSpec((D, D), lambda j: (0, j))],
        out_specs=[pl.BlockSpec((D, D), lambda j: (0, j)), pl.BlockSpec((8, D), lambda j: (0, j)),
                   pl.BlockSpec((8, D), lambda j: (0, 0))],
        out_shape=[jax.ShapeDtypeStruct((D, 3 * D), F32), jax.ShapeDtypeStruct((8, 3 * D), F32),
                   jax.ShapeDtypeStruct((8, D), F32)],
        compiler_params=_cp(("arbitrary",)),
    )(dmod, cc, cct, w_mod)


def _ext_specs(nct):
    return (pl.BlockSpec((T, D), lambda i: (jnp.minimum(i, nct - 1), 0)),
            pl.BlockSpec((T, D), lambda i: (jnp.maximum(i - nct, 0), 0)))


def _norm_fwd(ctx, xl, mod, nw, nct):
    lext = ctx.shape[0] + xl.shape[0]

    def kern(c_ref, x_ref, mod_ref, nw_ref, h_ref):
        is_ctx = pl.program_id(0) < nct
        x = jnp.where(is_ctx, c_ref[...], x_ref[...])
        r = lax.rsqrt(jnp.mean(x * x, axis=-1, keepdims=True) + EPS)
        xn = x * r * nw_ref[...]
        shift = jnp.where(is_ctx, mod_ref[1:2, 0:D], mod_ref[0:1, 0:D])
        scale = jnp.where(is_ctx, mod_ref[1:2, D:2 * D], mod_ref[0:1, D:2 * D])
        h_ref[...] = (xn * (1.0 + scale) + shift).astype(BF16)

    return pl.pallas_call(
        kern, name="norm_fwd", grid=(lext // T,),
        in_specs=[*_ext_specs(nct), pl.BlockSpec((8, 3 * D), lambda i: (0, 0)),
                  pl.BlockSpec((1, D), lambda i: (0, 0))],
        out_specs=pl.BlockSpec((T, D), lambda i: (i, 0)),
        out_shape=jax.ShapeDtypeStruct((lext, D), BF16),
        compiler_params=_cp(("parallel",)),
    )(ctx, xl, mod, nw)


def _norm_bwd(dha, dhb, ctx, xl, dx2, mod, nw, nct):
    lext = ctx.shape[0] + xl.shape[0]
    ntl = lext // T

    def kern(dha_ref, dhb_ref, c_ref, x_ref, dx2_ref, mod_ref, nw_ref, gx_ref, gnw_ref, dss_ref):
        i = pl.program_id(0)
        is_ctx = i < nct

        @pl.when(i == 0)
        def _():
            gnw_ref[...] = jnp.zeros_like(gnw_ref)
            dss_ref[...] = jnp.zeros_like(dss_ref)

        x = jnp.where(is_ctx, c_ref[...], x_ref[...])
        dh_ = dha_ref[...].astype(F32) + jnp.where(is_ctx, 0.0, dhb_ref[...].astype(F32))
        nw_ = nw_ref[...]
        r = lax.rsqrt(jnp.mean(x * x, axis=-1, keepdims=True) + EPS)
        xn = x * r
        scale = jnp.where(is_ctx, mod_ref[1:2, D:2 * D], mod_ref[0:1, D:2 * D])
        dsh = jnp.sum(dh_, axis=0, keepdims=True)
        dsc = jnp.sum(dh_ * (xn * nw_), axis=0, keepdims=True)
        row = jnp.concatenate([dsh, dsc], axis=1)
        rid = _iota((8, 2 * D), 0)
        dss_ref[...] += jnp.where(rid == jnp.where(is_ctx, 1, 0), row, 0.0)
        dxnw = dh_ * (1.0 + scale)
        gnw_ref[...] += jnp.broadcast_to(jnp.sum(dxnw * xn, axis=0, keepdims=True), (8, D))
        dxn = dxnw * nw_
        dx = r * (dxn - xn * jnp.mean(dxn * xn, axis=-1, keepdims=True))
        gx_ref[...] = dx2_ref[...].astype(F32) + dx

    return pl.pallas_call(
        kern, name="norm_bwd", grid=(ntl,),
        in_specs=[pl.BlockSpec((T, D), lambda i: (i, 0)), pl.BlockSpec((T, D), lambda i: (jnp.maximum(i - nct, 0), 0)),
                  *_ext_specs(nct),
                  pl.BlockSpec((T, D), lambda i: (jnp.maximum(i - nct, 0), 0)),
                  pl.BlockSpec((8, 3 * D), lambda i: (0, 0)), pl.BlockSpec((1, D), lambda i: (0, 0))],
        out_specs=[pl.BlockSpec((T, D), lambda i: (jnp.maximum(i - nct, 0), 0)),
                   pl.BlockSpec((8, D), lambda i: (0, 0)), pl.BlockSpec((8, 2 * D), lambda i: (0, 0))],
        out_shape=[jax.ShapeDtypeStruct((lext - nct * T, D), F32), jax.ShapeDtypeStruct((8, D), F32),
                   jax.ShapeDtypeStruct((8, 2 * D), F32)],
        compiler_params=_cp(("arbitrary",)),
    )(dha, dhb, ctx, xl, dx2, mod, nw)


CB = 1024


def _halo_specs(width_blk, col_off_blocks, ntl):
    t8 = T // 8
    main = pl.BlockSpec((T, width_blk), lambda j, i: (i, j + col_off_blocks))
    prev = pl.BlockSpec((8, width_blk), lambda j, i: (jnp.maximum(i * t8 - 1, 0), j + col_off_blocks))
    nxt = pl.BlockSpec((8, width_blk), lambda j, i: (jnp.minimum((i + 1) * t8, ntl * t8 - 1), j + col_off_blocks))
    return main, prev, nxt


def _seq_edges(i, nct, ntl):
    starts = jnp.logical_or(i == 0, i == nct)
    ends = jnp.logical_or(i == nct - 1, i == ntl - 1)
    return starts, ends


def _shifted(ext, off):
    n = ext.shape[0]
    return pltpu.roll(ext, (-off) % n, axis=0)[8:8 + T]


def _conv_fwd(proj_ssd, cw, cb, nct):
    lext = proj_ssd.shape[0]
    ntl = lext // T

    def kern(u_ref, up_ref, un_ref, w_ref, b_ref, o_ref, o16_ref):
        i = pl.program_id(1)
        starts, ends = _seq_edges(i, nct, ntl)
        up = jnp.where(starts, 0.0, up_ref[...])
        un = jnp.where(ends, 0.0, un_ref[...])
        ext = jnp.concatenate([up, u_ref[...], un], axis=0)
        w = w_ref[...]
        pre = b_ref[...] + w[0:1] * _shifted(ext, -2) + w[1:2] * _shifted(ext, -1) \
            + w[2:3] * u_ref[...] + w[3:4] * _shifted(ext, 1)
        act = _silu(pre)
        o_ref[...] = act
        o16_ref[...] = act.astype(BF16)

    cbf = 4 * CB
    main, prev, nxt = _halo_specs(cbf, 0, ntl)
    return pl.pallas_call(
        kern, name="conv_fwd", grid=(XBC // cbf, ntl),
        in_specs=[main, prev, nxt, pl.BlockSpec((8, cbf), lambda j, i: (0, j)), pl.BlockSpec((1, cbf), lambda j, i: (0, j))],
        out_specs=[pl.BlockSpec((T, cbf), lambda j, i: (i, j))] * 2,
        out_shape=[jax.ShapeDtypeStruct((lext, XBC), F32), jax.ShapeDtypeStruct((lext, XBC), BF16)],
        compiler_params=_cp(("parallel", "parallel")),
    )(proj_ssd, proj_ssd, proj_ssd, cw, cb)


def _conv_bwd(dpost, proj_ssd, cw, cb, col_off, width, nct, name):
    lext = proj_ssd.shape[0]
    ntl = lext // T
    bw = min(width, 2 * CB)
    assert col_off % bw == 0 and width % bw == 0
    cob = col_off // bw

    def kern(u_ref, up_ref, un_ref, d_ref, dp_ref, dn_ref, w_ref, b_ref, du_ref, gw_ref, gb_ref):
        i = pl.program_id(1)

        @pl.when(i == 0)
        def _():
            gw_ref[...] = jnp.zeros_like(gw_ref)
            gb_ref[...] = jnp.zeros_like(gb_ref)

        starts, ends = _seq_edges(i, nct, ntl)
        ext = jnp.concatenate([jnp.where(starts, 0.0, up_ref[...]), u_ref[...], jnp.where(ends, 0.0, un_ref[...])], axis=0)
        dext = jnp.concatenate([jnp.where(starts, 0.0, dp_ref[...]), d_ref[...], jnp.where(ends, 0.0, dn_ref[...])], axis=0)
        w = w_ref[...]
        n = ext.shape[0]
        pre = b_ref[...] + w[0:1] * pltpu.roll(ext, 2, axis=0) + w[1:2] * pltpu.roll(ext, 1, axis=0) \
            + w[2:3] * ext + w[3:4] * pltpu.roll(ext, n - 1, axis=0)
        dpre = dext * _dsilu(pre)
        dm = dpre[8:8 + T]
        du = w[0:1] * _shifted(dpre, 2) + w[1:2] * _shifted(dpre, 1) + w[2:3] * dm + w[3:4] * _shifted(dpre, -1)
        du_ref[...] = du.astype(BF16)
        g0 = jnp.sum(dm * _shifted(ext, -2), axis=0, keepdims=True)
        g1 = jnp.sum(dm * _shifted(ext, -1), axis=0, keepdims=True)
        g2 = jnp.sum(dm * u_ref[...], axis=0, keepdims=True)
        g3 = jnp.sum(dm * _shifted(ext, 1), axis=0, keepdims=True)
        rid = _iota((8, bw), 0)
        gw_ref[...] += jnp.where(rid == 0, g0, jnp.where(rid == 1, g1, jnp.where(rid == 2, g2, jnp.where(rid == 3, g3, 0.0))))
        gb_ref[...] += jnp.broadcast_to(jnp.sum(dm, axis=0, keepdims=True), (8, bw))

    main, prev, nxt = _halo_specs(bw, cob, ntl)
    dmain, dprev, dnxt = _halo_specs(bw, 0, ntl)
    return pl.pallas_call(
        kern, name=name, grid=(width // bw, ntl),
        in_specs=[main, prev, nxt, dmain, dprev, dnxt,
                  pl.BlockSpec((8, bw), lambda j, i: (0, j + cob)), pl.BlockSpec((1, bw), lambda j, i: (0, j + cob))],
        out_specs=[pl.BlockSpec((T, bw), lambda j, i: (i, j)), pl.BlockSpec((8, bw), lambda j, i: (0, j)),
                   pl.BlockSpec((8, bw), lambda j, i: (0, j))],
        out_shape=[jax.ShapeDtypeStruct((lext, width), BF16), jax.ShapeDtypeStruct((8, width), F32),
                   jax.ShapeDtypeStruct((8, width), F32)],
        compiler_params=_cp(("parallel", "arbitrary")),
    )(proj_ssd, proj_ssd, proj_ssd, dpost, dpost, dpost, cw, cb)


def _tri(lower):
    r, c = _iota((Q, Q), 0), _iota((Q, Q), 1)
    return jnp.where((c <= r) if lower else (c >= r), 1.0, 0.0).astype(BF16)


def _is_bdir_lane(shape):
    ln = _iota(shape, len(shape) - 1)
    return jnp.logical_and(((ln >> 2) & 1) == 1, ln < 64)


def _dt_fwd(proj_ssd, dtb, av):
    lext = proj_ssd.shape[0]

    def kern(p_ref, b_ref, a_ref, dtg_ref, lag_ref, dtt_ref, lat_ref):
        lane = _iota((T, 128), 1)
        raw = p_ref[...] + b_ref[...]
        dt = jnp.where(lane < 64, jnp.maximum(raw, 0.0) + jnp.log1p(jnp.exp(-jnp.abs(raw))), 0.0)
        dta = dt * a_ref[...]
        tl, tu = _tri(True), _tri(False)
        isb = _is_bdir_lane((Q, 128))
        las = []
        for qq in range(T // Q):
            blk = dta[qq * Q:(qq + 1) * Q]
            las.append(jnp.where(isb, _dot3_l(tu, blk), _dot3_l(tl, blk)))
        la = jnp.concatenate(las, axis=0)
        for g in range(NG):
            sh = (128 - 8 * g) % 128
            dtg_ref[g] = jnp.where(lane < 8, pltpu.roll(dt, sh, axis=1) if sh else dt, 0.0)
            lag_ref[g] = jnp.where(lane < 8, pltpu.roll(la, sh, axis=1) if sh else la, 0.0)
        dtt_ref[...] = dt.T[0:64]
        lat_ref[...] = la.T[0:64]

    return pl.pallas_call(
        kern, name="dt_fwd", grid=(lext // T,),
        in_specs=[pl.BlockSpec((T, 128), lambda i: (i, XBC // 128)), pl.BlockSpec((1, 128), lambda i: (0, 0)),
                  pl.BlockSpec((1, 128), lambda i: (0, 0))],
        out_specs=[pl.BlockSpec((NG, T, 128), lambda i: (0, i, 0)), pl.BlockSpec((NG, T, 128), lambda i: (0, i, 0)),
                   pl.BlockSpec((64, T), lambda i: (0, i)), pl.BlockSpec((64, T), lambda i: (0, i))],
        out_shape=[jax.ShapeDtypeStruct((NG, lext, 128), F32), jax.ShapeDtypeStruct((NG, lext, 128), F32),
                   jax.ShapeDtypeStruct((64, lext), F32), jax.ShapeDtypeStruct((64, lext), F32)],
        compiler_params=_cp(("parallel",)),
    )(proj_ssd, dtb, av)


def _dt_bwd(ddtg, proj_ssd, dtb):
    lext = proj_ssd.shape[0]

    def kern(d_ref, p_ref, b_ref, o_ref, gb_ref):
        @pl.when(pl.program_id(0) == 0)
        def _():
            gb_ref[...] = jnp.zeros_like(gb_ref)

        acc = d_ref[0]
        for g in range(1, NG):
            acc = acc + pltpu.roll(d_ref[g], 8 * g, axis=1)
        draw = acc * _sig(p_ref[...] + b_ref[...])
        o_ref[...] = draw.astype(BF16)
        gb_ref[...] += jnp.broadcast_to(jnp.sum(draw, axis=0, keepdims=True), (8, 128))

    return pl.pallas_call(
        kern, name="dt_bwd", grid=(lext // T,),
        in_specs=[pl.BlockSpec((NG, T, 128), lambda i: (0, i, 0)), pl.BlockSpec((T, 128), lambda i: (i, XBC // 128)),
                  pl.BlockSpec((1, 128), lambda i: (0, 0))],
        out_specs=[pl.BlockSpec((T, 128), lambda i: (i, 0)), pl.BlockSpec((8, 128), lambda i: (0, 0))],
        out_shape=[jax.ShapeDtypeStruct((lext, 128), BF16), jax.ShapeDtypeStruct((8, 128), F32)],
        compiler_params=_cp(("arbitrary",)),
    )(ddtg, proj_ssd, dtb)


def _expand_sel(d):
    r, c = _iota((128, 256), 0), _iota((128, 256), 1)
    return jnp.where(r == 4 * d + (c >> 6), 1.0, 0.0).astype(BF16)


def _reduce_sel(d):
    r, c = _iota((256, 128), 0), _iota((256, 128), 1)
    return jnp.where(c == 4 * d + (r >> 6), 1.0, 0.0).astype(BF16)


def _chunk_of_bwd_dir(j, ncc, nc):
    return jnp.where(j < ncc, ncc - 1 - j, nc + ncc - 1 - j)


def _dir_terms(la, dt, d):
    lane = _iota(la.shape, 1)
    mine = jnp.logical_and(lane >= 4 * d, lane < 4 * d + 4)
    la = jnp.where(mine, la, 0.0)
    tot = la[Q - 1:Q] if d == 0 else la[0:1]
    wnd = jnp.exp(tot - la)
    return tot, wnd * jnp.where(mine, dt, 0.0), wnd


def _ssd_state(xbc, dtg, lag, ncc):
    lext = xbc.shape[0]
    nc = lext // Q

    def kern(xf_ref, bf_ref, dtf_ref, laf_ref, xb_ref, bb_ref, dtb_ref, lab_ref, hf_ref, hb_ref, sf, sb):
        @pl.when(pl.program_id(0) == 0)
        def _():
            sf[...] = jnp.zeros_like(sf)
            sb[...] = jnp.zeros_like(sb)

        for d, (x_ref, b_ref, dt_ref, la_ref, h_ref, s) in enumerate(
                ((xf_ref, bf_ref, dtf_ref, laf_ref, hf_ref, sf), (xb_ref, bb_ref, dtb_ref, lab_ref, hb_ref, sb))):
            h_ref[...] = s[...].astype(BF16)
            ex = _expand_sel(d)
            for g in range(NG):
                cols = slice(256 * g, 256 * (g + 1))
                tot, w_end, _ = _dir_terms(la_ref[g], dt_ref[g], d)
                wexp = _dot2_r(w_end, ex)
                dexp = _dot2_r(jnp.broadcast_to(jnp.exp(tot), (8, 128)), ex)[0:1]
                xw = (x_ref[:, cols] * wexp).astype(BF16)
                s[:, cols] = s[:, cols] * dexp + _dot(b_ref[:, 128 * g:128 * (g + 1)].astype(F32).T.astype(BF16), xw)

    cb = functools.partial(_chunk_of_bwd_dir, ncc=ncc, nc=nc)
    sm = lambda f: pl.BlockSpec((NG, Q, 128), lambda j: (0, f(j), 0))
    one = lambda j: j
    return pl.pallas_call(
        kern, name="ssd_state", grid=(nc,),
        in_specs=[pl.BlockSpec((Q, DI), lambda j: (j, 0)), pl.BlockSpec((Q, NG * NS), lambda j: (j, 2)), sm(one), sm(one),
                  pl.BlockSpec((Q, DI), lambda j: (cb(j), 0)), pl.BlockSpec((Q, NG * NS), lambda j: (cb(j), 2)), sm(cb), sm(cb)],
        out_specs=[pl.BlockSpec((None, 128, DI), lambda j: (j, 0, 0)),
                   pl.BlockSpec((None, 128, DI), lambda j: (cb(j), 0, 0))],
        out_shape=[jax.ShapeDtypeStruct((nc, 128, DI), BF16), jax.ShapeDtypeStruct((nc, 128, DI), BF16)],
        scratch_shapes=[pltpu.VMEM((128, DI), F32), pltpu.VMEM((128, DI), F32)],
        compiler_params=_cp(("arbitrary",)),
    )(xbc, xbc, dtg, lag, xbc, xbc, dtg, lag)


def _ssd_out(xbc, dtg, lag, dtt, lat, htf, htb, ncc):
    lext = xbc.shape[0]
    nc = lext // Q
    ncx = nc - ncc

    gps = 8
    li, si = (lambda: _iota((Q, Q), 0)), (lambda: _iota((Q, Q), 1))

    def kern(x_ref, b_ref, c_ref, dtg_ref, lag_ref, dtt_ref, lat_ref, hf_ref, hb_ref, y_ref):
        lane = _iota((Q, 128), 1)
        masks = (li() >= si(), li() <= si())
        for gg in range(gps):
            cols = slice(256 * gg, 256 * (gg + 1))
            cm = c_ref[:, 128 * gg:128 * (gg + 1)]
            xb_ = x_ref[:, cols].astype(BF16)
            s_ = _dot_nt(cm.astype(BF16), b_ref[:, 128 * gg:128 * (gg + 1)].astype(BF16))
            la, dtt_, lat_ = lag_ref[gg], dtt_ref[8 * gg:8 * (gg + 1)], lat_ref[8 * gg:8 * (gg + 1)]
            elam = jnp.exp(la)
            yh = [jnp.zeros((Q, 128), F32), jnp.zeros((Q, 128), F32)]
            for d, h_ref in enumerate((hf_ref, hb_ref)):
                rhs = jnp.concatenate([xb_, h_ref[:, cols].astype(BF16)], axis=0)
                lhs = []
                for r in range(HPG):
                    j = 4 * d + r
                    lm = jnp.where(masks[d], jnp.exp(la[:, j:j + 1] - lat_[j:j + 1, :]), 0.0)
                    w = s_ * lm * dtt_[j:j + 1, :]
                    lhs.append(jnp.concatenate([w, cm * elam[:, j:j + 1]], axis=1).astype(BF16))
                for b in range(HPG // 2):
                    ypair = _dot(jnp.concatenate(lhs[2 * b:2 * b + 2], axis=0), rhs[:, 128 * b:128 * (b + 1)])
                    yh[b] = yh[b] + jnp.where(lane < 64, ypair[0:Q], ypair[Q:2 * Q])
            y_ref[:, cols] = jnp.concatenate(yh, axis=1).astype(BF16)

    nb = NG // gps
    sm = pl.BlockSpec((gps, Q, 128), lambda c, g: (g, c + ncc, 0))
    smt = pl.BlockSpec((8 * gps, Q), lambda c, g: (g, c + ncc))
    st3 = pl.BlockSpec((None, 128, 256 * gps), lambda c, g: (c + ncc, 0, g))
    return pl.pallas_call(
        kern, name="ssd_out", grid=(ncx, nb),
        in_specs=[pl.BlockSpec((Q, 256 * gps), lambda c, g: (c + ncc, g)),
                  pl.BlockSpec((Q, 128 * gps), lambda c, g: (c + ncc, 2 * nb + g)),
                  pl.BlockSpec((Q, 128 * gps), lambda c, g: (c + ncc, 3 * nb + g)), sm, sm, smt, smt, st3, st3],
        out_specs=pl.BlockSpec((Q, 256 * gps), lambda c, g: (c, g)),
        out_shape=jax.ShapeDtypeStruct((ncx * Q, DI), BF16),
        compiler_params=_cp(("parallel", "parallel")),
    )(xbc, xbc, xbc, dtg, lag, dtt, lat, htf, htb)


def _ssd_bwd_state(xbc, dy, lag, ncc):
    lext = xbc.shape[0]
    nc = lext // Q

    def kern(cf_ref, dyf_ref, laf_ref, cb_ref, dyb_ref, lab_ref, df_ref, db_ref, sf, sb):
        @pl.when(pl.program_id(0) == 0)
        def _():
            sf[...] = jnp.zeros_like(sf)
            sb[...] = jnp.zeros_like(sb)

        for d, (c_ref, dy_ref, la_ref, o_ref, s) in enumerate(
                ((cf_ref, dyf_ref, laf_ref, df_ref, sf), (cb_ref, dyb_ref, lab_ref, db_ref, sb))):
            o_ref[...] = s[...].astype(BF16)
            ex = _expand_sel(d)
            for g in range(NG):
                cols = slice(256 * g, 256 * (g + 1))
                la = la_ref[g]
                tot = la[Q - 1:Q] if d == 0 else la[0:1]
                eexp = _dot2_r(jnp.exp(la), ex)
                dexp = _dot2_r(jnp.broadcast_to(jnp.exp(tot), (8, 128)), ex)[0:1]
                dye = (dy_ref[:, cols] * eexp).astype(BF16)
                s[:, cols] = s[:, cols] * dexp + _dot(c_ref[:, 128 * g:128 * (g + 1)].astype(F32).T.astype(BF16), dye)

    cf = lambda j: nc - 1 - j
    cb = lambda j: _chunk_of_bwd_dir(nc - 1 - j, ncc, nc)
    sm = lambda f: pl.BlockSpec((NG, Q, 128), lambda j: (0, f(j), 0))
    return pl.pallas_call(
        kern, name="ssd_bwd_state", grid=(nc,),
        in_specs=[pl.BlockSpec((Q, NG * NS), lambda j: (cf(j), 3)), pl.BlockSpec((Q, DI), lambda j: (cf(j), 0)), sm(cf),
                  pl.BlockSpec((Q, NG * NS), lambda j: (cb(j), 3)), pl.BlockSpec((Q, DI), lambda j: (cb(j), 0)), sm(cb)],
        out_specs=[pl.BlockSpec((None, 128, DI), lambda j: (cf(j), 0, 0)),
                   pl.BlockSpec((None, 128, DI), lambda j: (cb(j), 0, 0))],
        out_shape=[jax.ShapeDtypeStruct((nc, 128, DI), BF16), jax.ShapeDtypeStruct((nc, 128, DI), BF16)],
        scratch_shapes=[pltpu.VMEM((128, DI), F32), pltpu.VMEM((128, DI), F32)],
        compiler_params=_cp(("arbitrary",)),
    )(xbc, dy, lag, xbc, dy, lag)


def _ssd_bwd_out(xbc, dy, dsk, dtg, lag, dtt, lat, htf, htb, dhf, dhb, a_rows):
    lext = xbc.shape[0]
    nc = lext // Q

    gps = 1

    def kern(x_ref, b_ref, c_ref, dy_ref, sk_ref, dtg_ref, lag_ref, dtt_ref, lat_ref, hf_ref, hb_ref, df_ref, db_ref,
             a_ref, dx_ref, dbo_ref, dco_ref, ddt_ref, ga_ref):
        @pl.when(pl.program_id(1) == 0)
        def _():
            ga_ref[...] = jnp.zeros_like(ga_ref)

        for gg in range(gps):
            one_group(gg, x_ref, b_ref, c_ref, dy_ref, sk_ref, dtg_ref, lag_ref, dtt_ref, lat_ref, hf_ref, hb_ref, df_ref,
                      db_ref, a_ref, dx_ref, dbo_ref, dco_ref, ddt_ref, ga_ref)

    def one_group(gg, x_ref, b_ref, c_ref, dy_ref, sk_ref, dtg_ref, lag_ref, dtt_ref, lat_ref, hf_ref, hb_ref, df_ref,
                  db_ref, a_ref, dx_ref, dbo_ref, dco_ref, ddt_ref, ga_ref):
        g = pl.program_id(0) * gps + gg
        cols, cols128 = slice(256 * gg, 256 * (gg + 1)), slice(128 * gg, 128 * (gg + 1))
        x, bm, cm, dy_ = x_ref[:, cols], b_ref[:, cols128], c_ref[:, cols128], dy_ref[:, cols].astype(F32)
        xb_, bb_, cb_, dyb_ = x.astype(BF16), bm.astype(BF16), cm.astype(BF16), dy_.astype(BF16)
        st = _dot_nt(bb_, cb_)
        si, li = _iota((Q, Q), 0), _iota((Q, Q), 1)
        lane = _iota((Q, 256), 1)
        lane128 = _iota((Q, 128), 1)
        row128 = _iota((Q, 128), 0)
        sub = _iota((128, Q), 0)
        la, dt = lag_ref[gg], dtg_ref[gg]
        dtt_, lat_ = dtt_ref[8 * gg:8 * (gg + 1)], lat_ref[8 * gg:8 * (gg + 1)]
        elam = jnp.exp(la)
        dst = jnp.zeros((Q, Q), F32)
        dxh = [jnp.zeros((Q, 128), F32), jnp.zeros((Q, 128), F32)]
        cdir, clam = jnp.zeros((Q, 128), F32), jnp.zeros((Q, 128), F32)
        dba = jnp.zeros((Q, 128), F32)
        dca = jnp.zeros((Q, 128), F32)
        dlam = jnp.zeros((Q, 128), F32)
        ddir = jnp.zeros((Q, 128), F32)
        rows = jnp.zeros((128, Q), F32)
        per_dir = []
        for d, (h_ref, dh_ref) in enumerate(((hf_ref, df_ref), (hb_ref, db_ref))):
            ht, dht = h_ref[:, cols].astype(F32), dh_ref[:, cols].astype(F32)
            htb_, dhtb_ = ht.astype(BF16), dht.astype(BF16)
            tot, w_end, wnd = _dir_terms(la, dt, d)
            ex, rs = _expand_sel(d), _reduce_sel(d)
            elx = _dot2_r(elam, ex)
            wex = _dot2_r(w_end, ex)
            dye = dy_ * elx
            ch = _dot(cb_, htb_)
            bd = _dot(bb_, dhtb_)
            dca = dca + _dot_nt(dye.astype(BF16), htb_)
            dba = dba + _dot_nt((x * wex).astype(BF16), dhtb_)
            dlam = dlam + _dot2_r(dye * ch, rs)
            xbd = _dot2_r(x * bd, rs)
            e_ = w_end * xbd
            dlam = dlam - e_
            ddir = ddir + wnd * xbd
            hh = _dot2_r(jnp.broadcast_to(jnp.sum(dht * ht, axis=0, keepdims=True), (8, 256)), rs)[0:1]
            tot_term = jnp.sum(e_, axis=0, keepdims=True) + jnp.exp(tot) * hh
            dlam = dlam + jnp.where(row128 == (Q - 1 if d == 0 else 0), tot_term, 0.0)
            per_dir.append((w_end, jnp.concatenate([dyb_, dhtb_], axis=0), (li >= si) if d == 0 else (li <= si)))
        for r in range(HPG):
            half = slice(128 * (r // 2), 128 * (r // 2 + 1))
            hm = (lane128 >> 6) == (r % 2)
            dwt = _dot_nt(jnp.where(hm, x[:, half], 0.0).astype(BF16), dyb_[:, half])
            q = dwt * st
            for d, (w_end, rhs, maskt) in enumerate(per_dir):
                j = 4 * d + r
                dc = dt[:, j:j + 1]
                lmt = jnp.where(maskt, jnp.exp(lat_[j:j + 1, :] - la[:, j:j + 1]), 0.0)
                ldc = lmt * jnp.broadcast_to(dc, (Q, Q))
                lhs = jnp.concatenate([st * ldc, bm * w_end[:, j:j + 1]], axis=1).astype(BF16)
                dxh[r // 2] = dxh[r // 2] + jnp.where(hm, _dot(lhs, rhs[:, half]), 0.0)
                cs = jnp.sum(q * lmt, axis=1, keepdims=True)
                cdir = jnp.where(lane128 == j, cs, cdir)
                clam = jnp.where(lane128 == j, cs * dc, clam)
                rows = rows + jnp.where(sub == j, jnp.sum(q * ldc, axis=0, keepdims=True), 0.0)
                dst = dst + dwt * ldc
        dxa = jnp.concatenate(dxh, axis=1)
        ddir = ddir + cdir
        dlam = dlam - clam + rows.T
        dba = dba + _dot(dst.astype(BF16), cb_)
        dca = dca + _dot(dst.T.astype(BF16), bb_)
        isb = jnp.logical_and(lane128 >= 4, lane128 < 8)
        ddel = jnp.where(isb, _dot2_l(_tri(True), dlam), _dot2_l(_tri(False), dlam))
        a_l = a_ref[pl.ds(g, 1), :]
        ddt_ref[gg] = ddir + a_l * ddel
        ga_ref[gg] += jnp.broadcast_to(a_l * jnp.sum(dt * ddel, axis=0, keepdims=True), (8, 128))
        dx_ref[:, cols] = dxa + dy_ * sk_ref[:, cols]
        dbo_ref[:, cols128] = dba
        dco_ref[:, cols128] = dca

    nb = NG // gps
    st3 = pl.BlockSpec((None, 128, 256 * gps), lambda g, c: (c, 0, g))
    sm = pl.BlockSpec((gps, Q, 128), lambda g, c: (g, c, 0))
    smt = pl.BlockSpec((8 * gps, Q), lambda g, c: (g, c))
    wide = pl.BlockSpec((Q, 256 * gps), lambda g, c: (c, g))
    return pl.pallas_call(
        kern, name="ssd_bwd_out", grid=(nb, nc),
        in_specs=[wide, pl.BlockSpec((Q, 128 * gps), lambda g, c: (c, 2 * nb + g)),
                  pl.BlockSpec((Q, 128 * gps), lambda g, c: (c, 3 * nb + g)), wide,
                  pl.BlockSpec((1, 256 * gps), lambda g, c: (0, g)), sm, sm, smt, smt, st3, st3, st3, st3,
                  pl.BlockSpec((8, 128), lambda g, c: (0, 0))],
        out_specs=[wide, pl.BlockSpec((Q, 128 * gps), lambda g, c: (c, g)),
                   pl.BlockSpec((Q, 128 * gps), lambda g, c: (c, g)), sm, pl.BlockSpec((gps, 8, 128), lambda g, c: (g, 0, 0))],
        out_shape=[jax.ShapeDtypeStruct((lext, DI), F32), jax.ShapeDtypeStruct((lext, NG * NS), F32),
                   jax.ShapeDtypeStruct((lext, NG * NS), F32), jax.ShapeDtypeStruct((NG, lext, 128), F32),
                   jax.ShapeDtypeStruct((NG, 8, 128), F32)],
        compiler_params=_cp(("parallel", "arbitrary")),
    )(xbc, xbc, xbc, dy, dsk, dtg, lag, dtt, lat, htf, htb, dhf, dhb, a_rows)


def _post_fwd(yssm, xbc, proj_rest, dsk, gnw, nct):
    l = yssm.shape[0]

    def kern(y_ref, x_ref, z_ref, dsk_ref, w_ref, o_ref):
        y = y_ref[...].astype(F32) + dsk_ref[...] * x_ref[...]
        yz = y * _silu(z_ref[...].astype(F32))
        for g in range(NG):
            sl = slice(256 * g, 256 * (g + 1))
            v = yz[:, sl]
            r = lax.rsqrt(jnp.mean(v * v, axis=-1, keepdims=True) + EPS)
            o_ref[:, sl] = (v * r * w_ref[:, sl]).astype(BF16)

    return pl.pallas_call(
        kern, name="post_fwd", grid=(l // T,),
        in_specs=[pl.BlockSpec((T, DI), lambda i: (i, 0)), pl.BlockSpec((T, DI), lambda i: (i + nct, 0)),
                  pl.BlockSpec((T, DI), lambda i: (i, 0)), pl.BlockSpec((1, DI), lambda i: (0, 0)),
                  pl.BlockSpec((1, DI), lambda i: (0, 0))],
        out_specs=pl.BlockSpec((T, DI), lambda i: (i, 0)),
        out_shape=jax.ShapeDtypeStruct((l, DI), BF16),
        compiler_params=_cp(("parallel",)),
    )(yssm, xbc, proj_rest, dsk, gnw)


def _post_bwd(dgn, yssm, xbc, proj_rest, dsk, gnw, dpr, nct):
    l = yssm.shape[0]
    lext = xbc.shape[0]
    xi = lambda i: (jnp.maximum(i - nct, 0), 0)

    def kern(dg_ref, y_ref, x_ref, z_ref, dsk_ref, w_ref, dpr_ref, dy_ref, dz_ref, gw_ref, gd_ref):
        i = pl.program_id(0)

        @pl.when(i == 0)
        def _():
            gw_ref[...] = jnp.zeros_like(gw_ref)
            gd_ref[...] = jnp.zeros_like(gd_ref)

        @pl.when(i < nct)
        def _():
            dy_ref[...] = jnp.zeros_like(dy_ref)

        @pl.when(i >= nct)
        def _():
            xs = x_ref[...]
            z = z_ref[...].astype(F32)
            y = y_ref[...].astype(F32) + dsk_ref[...] * xs
            sz = _silu(z)
            yz = y * sz
            dgn_ = dg_ref[...].astype(F32)
            dyz_parts = []
            gws = []
            for g in range(NG):
                sl = slice(256 * g, 256 * (g + 1))
                v = yz[:, sl]
                r = lax.rsqrt(jnp.mean(v * v, axis=-1, keepdims=True) + EPS)
                vn = v * r
                dn = dgn_[:, sl] * w_ref[:, sl]
                gws.append(jnp.sum(dgn_[:, sl] * vn, axis=0, keepdims=True))
                dyz_parts.append(r * (dn - vn * jnp.mean(dn * vn, axis=-1, keepdims=True)))
            dyz = jnp.concatenate(dyz_parts, axis=1)
            gw_ref[...] += jnp.broadcast_to(jnp.concatenate(gws, axis=1), (8, DI))
            dy = dyz * sz
            dz_ref[...] = (dyz * y * _dsilu(z)).astype(BF16)
            gd_ref[...] += jnp.broadcast_to(jnp.sum(dy * xs, axis=0, keepdims=True), (8, DI))
            dy_ref[...] = dy.astype(BF16)

    return pl.pallas_call(
        kern, name="post_bwd", grid=(lext // T,),
        in_specs=[pl.BlockSpec((T, DI), xi), pl.BlockSpec((T, DI), xi), pl.BlockSpec((T, DI), lambda i: (i, 0)),
                  pl.BlockSpec((T, DI), xi), pl.BlockSpec((1, DI), lambda i: (0, 0)), pl.BlockSpec((1, DI), lambda i: (0, 0)),
                  pl.BlockSpec(memory_space=pl.ANY)],
        out_specs=[pl.BlockSpec((T, DI), lambda i: (i, 0)),
                   pl.BlockSpec((T, DI), xi), pl.BlockSpec((8, DI), lambda i: (0, 0)), pl.BlockSpec((8, DI), lambda i: (0, 0))],
        out_shape=[jax.ShapeDtypeStruct((lext, DI), BF16),
                   jax.ShapeDtypeStruct((l, RESTW), BF16), jax.ShapeDtypeStruct((8, DI), F32), jax.ShapeDtypeStruct((8, DI), F32)],
        input_output_aliases={6: 1},
        compiler_params=_cp(("arbitrary",)),
    )(dgn, yssm, xbc, proj_rest, dsk, gnw, dpr)


C_G1, C_G2, C_GA, C_GB, C_CG = 2, 3, 4, 5, 6
PITCH = GW + 16
NROW = T // GW


GAP = PITCH - GW
PADR = GAP + NROW * PITCH
NSTRIP = D // 128


def _fill_padded(pad8, val):
    z = jnp.zeros((GAP, D), F32)
    parts = [z]
    for r in range(NROW):
        parts += [val[GW * r:GW * (r + 1)], z]
    p = jnp.concatenate(parts, axis=0)
    pad8[0] = p
    for j in range(1, pad8.shape[0]):
        pad8[j] = pltpu.roll(p, PADR - j, axis=0)


def _tap(pad8, base, off, ln):
    return pad8[off % 8, pl.ds(base + off - off % 8, GW), ln]


def _row_conv(out_ref, pad8, w_ref, transpose):
    def strip(s, carry):
        ln = pl.ds(pl.multiple_of(s * 128, 128), 128)
        for r in range(NROW):
            base = GAP + PITCH * r
            acc = jnp.zeros((GW, 128), F32)
            for k in range(CK):
                off = (k - 15) if not transpose else (15 - k)
                acc = acc + w_ref[pl.ds(k, 1), ln] * _tap(pad8, base, off, ln)
            out_ref[pl.ds(GW * r, GW), ln] = acc
        return carry

    lax.fori_loop(0, NSTRIP, strip, 0)


def _row_conv_wgrad(gcw_ref, padd8, pada8):
    def strip(s, carry):
        ln = pl.ds(pl.multiple_of(s * 128, 128), 128)
        rid = _iota((32, 128), 0)
        g = jnp.zeros((32, 128), F32)
        for k0 in range(0, CK, 8):
            taps = range(k0, min(k0 + 8, CK))
            accs = {k: jnp.zeros((8, 128), F32) for k in taps}
            for r in range(NROW):
                base = GAP + PITCH * r
                d = _tap(padd8, base, 0, ln)
                for k in taps:
                    p = d * pada8[0, pl.ds(base + k - 15, GW), ln]
                    part = p[0:8]
                    for q in range(1, GW // 8):
                        part = part + p[8 * q:8 * (q + 1)]
                    accs[k] = accs[k] + part
            for k in taps:
                g = jnp.where(rid == k, jnp.sum(accs[k], axis=0, keepdims=True), g)
        gcw_ref[:, ln] += g
        return carry

    lax.fori_loop(0, NSTRIP, strip, 0)


def _ln_stats(cv):
    mu = jnp.mean(cv, axis=-1, keepdims=True)
    xc = cv - mu
    rs = lax.rsqrt(jnp.mean(xc * xc, axis=-1, keepdims=True) + EPS)
    return xc * rs, rs


def _conf_fwd(proj_rest, cw, cb, lw, lb):
    l = proj_rest.shape[0]

    def kern(ga_ref, gb_ref, cg_ref, cw_ref, cb_ref, lw_ref, lb_ref, o_ref, cv_ref, pad8):
        _fill_padded(pad8, ga_ref[...].astype(F32) * _sig(gb_ref[...].astype(F32)))
        _row_conv(cv_ref, pad8, cw_ref, False)
        cv = cv_ref[...] + cb_ref[...]
        cv_ref[...] = cv
        xh, _ = _ln_stats(cv)
        ln = xh * lw_ref[...] + lb_ref[...]
        o_ref[...] = (_silu(ln) * _silu(cg_ref[...].astype(F32))).astype(BF16)

    vec = pl.BlockSpec((1, D), lambda i: (0, 0))
    blk = pl.BlockSpec((T, D), lambda i: (i, 0))
    return pl.pallas_call(
        kern, name="conf_fwd", grid=(l // T,),
        in_specs=[pl.BlockSpec((T, D), lambda i: (i, C_GA)), pl.BlockSpec((T, D), lambda i: (i, C_GB)),
                  pl.BlockSpec((T, D), lambda i: (i, C_CG)), pl.BlockSpec((32, D), lambda i: (0, 0)), vec, vec, vec],
        out_specs=[blk, blk],
        out_shape=[jax.ShapeDtypeStruct((l, D), BF16), jax.ShapeDtypeStruct((l, D), F32)],
        scratch_shapes=[pltpu.VMEM((8, PADR, D), F32)],
        compiler_params=_cp(("parallel",)),
    )(proj_rest, proj_rest, proj_rest, cw, cb, lw, lb)


def _conf_bwd(duc, cv, proj_rest, cw, lw, lb, dpr):
    l = proj_rest.shape[0]

    def kern(du_ref, cv_ref, ga_ref, gb_ref, cg_ref, cw_ref, lw_ref, lb_ref, dpr_ref, o_ref, gcw_ref, gv_ref, sc,
             pada, padd, da_ref):
        i, j = pl.program_id(0), pl.program_id(1)

        @pl.when(jnp.logical_and(i == 0, j == 0))
        def _():
            gcw_ref[...] = jnp.zeros_like(gcw_ref)
            gv_ref[...] = jnp.zeros_like(gv_ref)

        @pl.when(j == 0)
        def _():
            ga, gb, cg = ga_ref[...].astype(F32), gb_ref[...].astype(F32), cg_ref[...].astype(F32)
            sg = _sig(gb)
            xh, rs = _ln_stats(cv_ref[...])
            ln = xh * lw_ref[...] + lb_ref[...]
            du = du_ref[...].astype(F32)
            sc[:, 2 * D:3 * D] = (du * _silu(ln) * _dsilu(cg)).astype(BF16)
            dln = du * _silu(cg) * _dsilu(ln)
            g_lw = jnp.sum(dln * xh, axis=0, keepdims=True)
            g_lb = jnp.sum(dln, axis=0, keepdims=True)
            dxh = dln * lw_ref[...]
            dcv = rs * (dxh - jnp.mean(dxh, axis=-1, keepdims=True) - xh * jnp.mean(dxh * xh, axis=-1, keepdims=True))
            g_cb = jnp.sum(dcv, axis=0, keepdims=True)
            rid = _iota((8, D), 0)
            gv_ref[...] += jnp.where(rid == 0, g_cb, jnp.where(rid == 1, g_lw, jnp.where(rid == 2, g_lb, 0.0)))
            _fill_padded(padd, dcv)
            _fill_padded(pada, ga * sg)
            _row_conv(da_ref, padd, cw_ref, True)
            _row_conv_wgrad(gcw_ref, padd, pada)
            da = da_ref[...]
            sc[:, 0:D] = (da * sg).astype(BF16)
            sc[:, D:2 * D] = (da * ga * sg * (1.0 - sg)).astype(BF16)

        o_ref[...] = sc[:, pl.ds(pl.multiple_of(j * D, 128), D)]

    vec = pl.BlockSpec((1, D), lambda i, j: (0, 0))
    col = lambda c: pl.BlockSpec((T, D), lambda i, j: (i, c))
    return pl.pallas_call(
        kern, name="conf_bwd", grid=(l // T, 3),
        in_specs=[col(0), col(0), col(C_GA), col(C_GB), col(C_CG), pl.BlockSpec((32, D), lambda i, j: (0, 0)), vec, vec,
                  pl.BlockSpec(memory_space=pl.ANY)],
        out_specs=[pl.BlockSpec((T, D), lambda i, j: (i, C_GA + j)), pl.BlockSpec((32, D), lambda i, j: (0, 0)),
                   pl.BlockSpec((8, D), lambda i, j: (0, 0))],
        out_shape=[jax.ShapeDtypeStruct((l, RESTW), BF16), jax.ShapeDtypeStruct((32, D), F32),
                   jax.ShapeDtypeStruct((8, D), F32)],
        scratch_shapes=[pltpu.VMEM((T, 3 * D), BF16), pltpu.VMEM((1, PADR, D), F32), pltpu.VMEM((8, PADR, D), F32),
                        pltpu.VMEM((T, D), F32)],
        input_output_aliases={8: 0},
        compiler_params=_cp(("arbitrary", "arbitrary")),
    )(duc, cv, proj_rest, proj_rest, proj_rest, cw, lw, lb, dpr)


def _merge_fwd(bs, bc, proj_rest):
    l = bs.shape[0]

    def kern(bs_ref, bc_ref, g1_ref, g2_ref, o_ref):
        up = lambda r: r[...].astype(F32)
        o_ref[...] = (_sig(up(g1_ref)) * up(bs_ref) + _sig(up(g2_ref)) * up(bc_ref)).astype(BF16)

    blk = pl.BlockSpec((TX, D), lambda i: (i, 0))
    return pl.pallas_call(
        kern, name="merge_fwd", grid=(l // TX,),
        in_specs=[blk, blk, pl.BlockSpec((TX, D), lambda i: (i, C_G1)), pl.BlockSpec((TX, D), lambda i: (i, C_G2))],
        out_specs=blk, out_shape=jax.ShapeDtypeStruct((l, D), BF16),
        compiler_params=_cp(("parallel",)),
    )(bs, bc, proj_rest, proj_rest)


def _merge_bwd(dm, bs, bc, proj_rest):
    l = bs.shape[0]

    def kern(dm_ref, bs_ref, bc_ref, g1_ref, g2_ref, dbs_ref, dbc_ref, dg_ref):
        up = lambda r: r[...].astype(F32)
        dm_ = up(dm_ref)
        s1, s2 = _sig(up(g1_ref)), _sig(up(g2_ref))
        dbs_ref[...] = (dm_ * s1).astype(BF16)
        dbc_ref[...] = (dm_ * s2).astype(BF16)
        dg_ref[:, 0:D] = (dm_ * up(bs_ref) * s1 * (1.0 - s1)).astype(BF16)
        dg_ref[:, D:2 * D] = (dm_ * up(bc_ref) * s2 * (1.0 - s2)).astype(BF16)

    blk = pl.BlockSpec((TX, D), lambda i: (i, 0))
    return pl.pallas_call(
        kern, name="merge_bwd", grid=(l // TX,),
        in_specs=[blk, blk, blk, pl.BlockSpec((TX, D), lambda i: (i, C_G1)), pl.BlockSpec((TX, D), lambda i: (i, C_G2))],
        out_specs=[blk, blk, pl.BlockSpec((TX, 2 * D), lambda i: (i, 1))],
        out_shape=[jax.ShapeDtypeStruct((l, D), BF16), jax.ShapeDtypeStruct((l, D), BF16),
                   jax.ShapeDtypeStruct((l, RESTW), BF16)],
        compiler_params=_cp(("parallel",)),
    )(dm, bs, bc, proj_rest, proj_rest)


def _final(x, out, tgt, mod, fw):
    l = x.shape[0]

    def kern(x_ref, o_ref, t_ref, mod_ref, fw_ref, ls_ref, dx2_ref, do_ref, gv_ref):
        @pl.when(pl.program_id(0) == 0)
        def _():
            ls_ref[...] = jnp.zeros_like(ls_ref)
            gv_ref[...] = jnp.zeros_like(gv_ref)

        gate = mod_ref[0:1, 2 * D:3 * D]
        o = o_ref[...]
        x2 = x_ref[...] + gate * o
        r = lax.rsqrt(jnp.mean(x2 * x2, axis=-1, keepdims=True) + EPS)
        yn = x2 * r
        fw_ = fw_ref[...]
        e = yn * fw_ - t_ref[...]
        ls_ref[...] += jnp.full((8, 128), 1.0, F32) * (0.5 / D) * jnp.sum(e * e)
        dy = e * (1.0 / D)
        g_fw = jnp.sum(dy * yn, axis=0, keepdims=True)
        dyn = dy * fw_
        dx2 = r * (dyn - yn * jnp.mean(dyn * yn, axis=-1, keepdims=True))
        g_gate = jnp.sum(dx2 * o, axis=0, keepdims=True)
        rid = _iota((8, D), 0)
        gv_ref[...] += jnp.where(rid == 0, g_fw, jnp.where(rid == 1, g_gate, 0.0))
        dx2_ref[...] = dx2.astype(BF16)
        do_ref[...] = (dx2 * gate).astype(BF16)

    blk = pl.BlockSpec((TX, D), lambda i: (i, 0))
    return pl.pallas_call(
        kern, name="final", grid=(l // TX,),
        in_specs=[blk, blk, blk, pl.BlockSpec((8, 3 * D), lambda i: (0, 0)), pl.BlockSpec((1, D), lambda i: (0, 0))],
        out_specs=[pl.BlockSpec((8, 128), lambda i: (0, 0)), blk, blk, pl.BlockSpec((8, D), lambda i: (0, 0))],
        out_shape=[jax.ShapeDtypeStruct((8, 128), F32), jax.ShapeDtypeStruct((l, D), BF16),
                   jax.ShapeDtypeStruct((l, D), BF16), jax.ShapeDtypeStruct((8, D), F32)],
        compiler_params=_cp(("arbitrary",)),
    )(x, out, tgt, mod, fw)


def _perm_dt_cols(w):
    s = w.shape[:-1]
    return w.reshape(*s, 2, NG, HPG).swapaxes(-3, -2).reshape(*s, 64)


def _unperm_dt_cols(w):
    s = w.shape[:-1]
    return w.reshape(*s, NG, 2, HPG).swapaxes(-3, -2).reshape(*s, 64)


def _pad_lanes(v, width):
    return jnp.pad(v, ((0, 0), (0, width - v.shape[1])))


def _vcols(segs, a, b):
    parts, off = [], 0
    for s in segs:
        lo, hi = max(a, off), min(b, off + s.shape[1])
        if lo < hi:
            parts.append(s[:, lo - off:hi - off])
        off += s.shape[1]
    return parts[0] if len(parts) == 1 else jnp.concatenate(parts, axis=1)


def _local_step(x, c, ctx, tgt, w):
    l = x.shape[0]
    nct = CTX // T
    ncc = CTX // Q
    lext = l + CTX

    w_mod = w["w_mod"].astype(BF16)
    wsegs = [s.astype(BF16) for s in (w["w_in"] if isinstance(w["w_in"], (list, tuple)) else [w["w_in"]])]
    w_ssd = jnp.concatenate([_vcols(wsegs, 0, XBC), _perm_dt_cols(_vcols(wsegs, XBC, XBC + 64)), jnp.zeros((D, 64), BF16)], axis=1)
    r0 = XBC + 64
    w_rest = jnp.concatenate([_vcols(wsegs, r0, r0 + DI), _vcols(wsegs, r0 + DI + 3 * D, r0 + RESTW),
                              _vcols(wsegs, r0 + DI, r0 + DI + 3 * D)], axis=1)
    w_os, w_oc, w_o = w["w_out_ssm"].astype(BF16), w["w_out_conf"].astype(BF16), w["w_out"].astype(BF16)
    cw8 = jnp.pad(w["ssm_conv_w"], ((0, 4), (0, 0)))
    cb_s = w["ssm_conv_b"].reshape(1, XBC)
    dtb = _pad_lanes(_perm_dt_cols(w["dt_bias"].reshape(1, 64)), 128)
    a_all = -jnp.exp(w["a_log"].reshape(1, 64))
    a_perm = _pad_lanes(_perm_dt_cols(a_all), 128)
    a_rows = _pad_lanes(_perm_dt_cols(a_all).reshape(NG, 8), 128)
    dsk = jnp.repeat(w["d_skip"].reshape(NH), HP).reshape(1, DI)
    gnw = w["ssm_norm_w"].reshape(1, DI)
    ccw = jnp.pad(w["conf_conv_w"], ((0, 1), (0, 0)))
    ccb, clw, clb = w["conf_conv_b"].reshape(1, D), w["conf_ln_w"].reshape(1, D), w["conf_ln_b"].reshape(1, D)
    nw = w["norm_w"].reshape(1, D)
    fw = w["final_norm_w"].reshape(1, D)
    cc = jnp.concatenate([c.reshape(1, D), w["c_ctx"].reshape(1, D), jnp.zeros((6, D), F32)], axis=0)

    bx = min(1024, l)
    be = 768 if lext % 768 == 0 else 256
    tk = min(1024, l)
    mod = _mod_fwd(cc, w_mod, w["b_mod"].reshape(1, 3 * D))
    h = _norm_fwd(ctx, x, mod, nw, nct)
    hx = h[CTX:]
    proj_ssd = _mm(h, w_ssd, "nn", lext, SSDW, D, be, SSDW // 3, D, F32, "proj_ssd")
    proj_rest = _mm(hx, w_rest, "nn", l, RESTW, D, bx, 1024, D, BF16, "proj_rest")
    xbc, xbc16 = _conv_fwd(proj_ssd, cw8, cb_s, nct)
    dtg, lag, dtt, lat = _dt_fwd(proj_ssd, dtb, a_perm)
    htf, htb = _ssd_state(xbc16, dtg, lag, ncc)
    yssm = _ssd_out(xbc16, dtg, lag, dtt, lat, htf, htb, ncc)
    gn = _post_fwd(yssm, xbc, proj_rest, dsk, gnw, nct)
    bs = _mm(gn, w_os, "nn", l, D, DI, bx, D, DI, BF16, "out_ssm")
    uc, cv = _conf_fwd(proj_rest, ccw, ccb, clw, clb)
    bc = _mm(uc, w_oc, "nn", l, D, D, bx, D, D, BF16, "out_conf")
    merged = _merge_fwd(bs, bc, proj_rest)
    out = _mm(merged, w_o, "nn", l, D, D, bx, D, D, F32, "out_proj")
    lsum, dx2, dout, gv_fin = _final(x, out, tgt, mod, fw)

    g = {}
    g["final_norm_w"] = gv_fin[0]
    dmerged = _mm(dout, w_o, "nt", l, D, D, bx, D, D, BF16, "d_merged")
    g["w_out"] = _mm(merged, dout, "tn", D, D, l, D, D, tk, F32, "g_w_out")
    dbs, dbc, dpr = _merge_bwd(dmerged, bs, bc, proj_rest)
    dgn = _mm(dbs, w_os, "nt", l, DI, D, bx, DI, D, BF16, "d_gn")
    g["w_out_ssm"] = _mm(gn, dbs, "tn", DI, D, l, DI, D, tk, F32, "g_w_out_ssm")
    duc = _mm(dbc, w_oc, "nt", l, D, D, bx, D, D, BF16, "d_uc")
    g["w_out_conf"] = _mm(uc, dbc, "tn", D, D, l, D, D, tk, F32, "g_w_out_conf")
    dpr, gcw, gv_conf = _conf_bwd(duc, cv, proj_rest, ccw, clw, clb, dpr)
    g["conf_conv_w"] = gcw[:CK]
    g["conf_conv_b"], g["conf_ln_w"], g["conf_ln_b"] = gv_conf[0], gv_conf[1], gv_conf[2]
    dy, dproj_rest, ggnw, gdsk = _post_bwd(dgn, yssm, xbc, proj_rest, dsk, gnw, dpr, nct)
    g["ssm_norm_w"] = ggnw[0]
    g["d_skip"] = gdsk[0].reshape(NH, HP).sum(axis=1)
    dhf, dhb = _ssd_bwd_state(xbc16, dy, lag, ncc)
    dxs, dbm, dcm, ddtg, galog = _ssd_bwd_out(xbc, dy, dsk, dtg, lag, dtt, lat, htf, htb, dhf, dhb, a_rows)
    g["a_log"] = _unperm_dt_cols(galog[:, 0, 0:8].reshape(1, 64)).reshape(2, NH)
    dus, gws, gbs = [], [], []
    for dpost, off, width, nm in ((dxs, 0, DI, "conv_bwd_x"), (dbm, DI, NG * NS, "conv_bwd_b"), (dcm, DI + NG * NS, NG * NS, "conv_bwd_c")):
        du_, gw_, gb_ = _conv_bwd(dpost, proj_ssd, cw8, cb_s, off, width, nct, nm)
        dus.append(du_)
        gws.append(gw_[:SK])
        gbs.append(gb_[0])
    g["ssm_conv_w"] = jnp.concatenate(gws, axis=1)
    g["ssm_conv_b"] = jnp.concatenate(gbs, axis=0)
    ddt_raw, gdtb = _dt_bwd(ddtg, proj_ssd, dtb)
    g["dt_bias"] = _unperm_dt_cols(gdtb[0:1, 0:64]).reshape(2, NH)
    dproj_ssd = jnp.concatenate(dus + [ddt_raw], axis=1)
    gw_ssd = _mm(h, dproj_ssd, "tn", D, SSDW, lext, D, SSDW // 3, be, F32, "g_w_ssd")
    gw_rest = _mm(hx, dproj_rest, "tn", D, RESTW, l, D, 1024, tk, F32, "g_w_rest")
    gsegs = [gw_ssd[:, :XBC], _unperm_dt_cols(gw_ssd[:, XBC:XBC + 64]), gw_rest[:, :DI], gw_rest[:, 2 * DI:],
             gw_rest[:, DI:2 * DI]]
    g["w_in"] = jnp.concatenate(gsegs, axis=1)
    g["w_in_shards"] = jnp.stack([_vcols(gsegs, R_IN * s, R_IN * (s + 1)) for s in range(NSHARD)])
    dh_a = _mm(dproj_ssd, w_ssd, "nt", lext, D, SSDW, T, D, SSDW, BF16, "dh_ssd")
    dh_b = _mm(dproj_rest, w_rest, "nt", l, D, RESTW, T, D, RESTW, BF16, "dh_rest")
    grad_x, gnw_in, dss = _norm_bwd(dh_a, dh_b, ctx, x, dx2, mod, nw, nct)
    g["norm_w"] = gnw_in[0]
    dmod = jnp.concatenate([jnp.concatenate([dss[0:1], gv_fin[1:2]], axis=1),
                            jnp.concatenate([dss[1:2], jnp.zeros((1, D), F32)], axis=1),
                            jnp.zeros((6, 3 * D), F32)], axis=0)
    gwm, gbm, gcc = _mod_bwd(dmod, cc, cc.T, w_mod)
    g["w_mod"], g["b_mod"], g["c_ctx"] = gwm, gbm[0], gcc[1]
    return lsum[0, 0], grad_x, g


NSHARD = 4
R_MOD, R_IN, R_OS, R_OC, R_O, R_SC, R_CC = 768, 2832, 512, 256, 256, 8, 8
O_MOD = 0
O_OS = O_MOD + R_MOD
O_OC = O_OS + R_OS
O_O = O_OC + R_OC
O_SC = O_O + R_O
O_CC = O_SC + R_SC
PUSED = O_CC + R_CC
PROWS = 1824
HALF = PROWS // 2
RB = HALF // 3
WB = 128
SROWS = 16
SMALL = (("b_mod", 3 * D), ("norm_w", D), ("ssm_conv_b", XBC), ("dt_bias", 64), ("a_log", 64), ("d_skip", NH),
         ("ssm_norm_w", DI), ("conf_conv_b", D), ("conf_ln_w", D), ("conf_ln_b", D), ("final_norm_w", D), ("c_ctx", D))


def _pack_shard(s):
    return jnp.concatenate([s["w_mod"].reshape(R_MOD, D), _pack_rest(s), jnp.zeros((PROWS - PUSED, D), F32)], axis=0)


def _pack_rest(s):
    cc = jnp.pad(s["conf_conv_w"].reshape(1, CK * 256), ((0, 0), (0, R_CC * D - CK * 256))).reshape(R_CC, D)
    return jnp.concatenate([s["w_out_ssm"], s["w_out_conf"], s["w_out"],
                            jnp.pad(s["ssm_conv_w"], ((0, R_SC - SK), (0, 0))), cc], axis=0)


def _unpack_rest(p):
    o = lambda r: r - O_OS
    return {"w_out_ssm": p[o(O_OS):o(O_OC)][None], "w_out_conf": p[o(O_OC):o(O_O)][None], "w_out": p[o(O_O):o(O_SC)][None],
            "ssm_conv_w": p[o(O_SC):o(O_SC) + SK][None],
            "conf_conv_w": p[o(O_CC):o(O_CC) + R_CC].reshape(R_CC * D)[:CK * 256].reshape(1, CK, 256)}


def _shard_cols(a, n):
    return a.reshape(a.shape[0], NSHARD, n).transpose(1, 0, 2)


def _pack_full(g):
    cc = jnp.pad(_shard_cols(g["conf_conv_w"], 256).reshape(NSHARD, CK * 256), ((0, 0), (0, R_CC * D - CK * 256)))
    return jnp.concatenate([_shard_cols(g["w_mod"], R_MOD).reshape(NSHARD, R_MOD, D),
                            g["w_out_ssm"].reshape(NSHARD, R_OS, D), g["w_out_conf"].reshape(NSHARD, R_OC, D),
                            g["w_out"].reshape(NSHARD, R_O, D),
                            jnp.pad(_shard_cols(g["ssm_conv_w"], D), ((0, 0), (0, R_SC - SK), (0, 0))),
                            cc.reshape(NSHARD, R_CC, D), jnp.zeros((NSHARD, PROWS - PUSED, D), F32)], axis=1)


def _unpack_gathered(gm, gw, gs):
    def cols(a, r, n):
        return a.reshape(NSHARD, r, n).transpose(1, 0, 2).reshape(r, NSHARD * n)
    return {"w_mod": cols(gm[:, O_MOD:O_OS], D, R_MOD), "w_in": [gw[s] for s in range(NSHARD)],
            "w_out_ssm": gm[:, O_OS:O_OC].reshape(DI, D), "w_out_conf": gm[:, O_OC:O_O].reshape(D, D),
            "w_out": gm[:, O_O:O_SC].reshape(D, D), "ssm_conv_w": cols(gs[:, 0:SK], SK, D),
            "conf_conv_w": cols(gs[:, R_SC:R_SC + R_CC].reshape(NSHARD, R_CC * D)[:, :CK * 256], CK, 256)}


MESH_ID = pl.DeviceIdType.MESH
ANY = pl.BlockSpec(memory_space=pl.ANY)


def _place():
    x, y, c = lax.axis_index("x"), lax.axis_index("y"), lax.axis_index("c")
    return x, y, c, [(1 - x, y), (x, 1 - y), (1 - x, 1 - y)]


def _rcopy(src, dst, send, recv, dev):
    return pltpu.make_async_remote_copy(src_ref=src, dst_ref=dst, send_sem=send, recv_sem=recv,
                                        device_id=dev, device_id_type=MESH_ID)


def _gather_weights(mats, small):
    n = len(mats)

    def kern(*refs):
        m_refs, s_ref, g_refs, gs_ref, (send, recv) = refs[:n], refs[n], refs[n + 1:2 * n + 1], refs[2 * n + 1], refs[2 * n + 2:]
        x, y, c, chips = _place()
        me = 2 * x + y
        sib = (x, y, 1 - c)
        first, passed = [], []
        for k, (px, py) in enumerate(chips):
            first.append(_rcopy(s_ref, gs_ref.at[me], send.at[k], recv.at[k], (px, py, c)))
            for a, (m_ref, g_ref) in enumerate(zip(m_refs, g_refs)):
                mine = _half_rows(c, m_ref.shape[0])
                first.append(_rcopy(m_ref.at[mine], g_ref.at[me, mine], send.at[3 + 6 * a + k], recv.at[3 + 6 * a + k], (px, py, c)))
        for cp in first:
            cp.start()
        for k, (px, py) in enumerate(chips):
            s = 2 * px + py
            for a, (m_ref, g_ref) in enumerate(zip(m_refs, g_refs)):
                mine = _half_rows(c, m_ref.shape[0])
                _rcopy(m_ref.at[mine], g_ref.at[s, mine], send.at[3 + 6 * a + k], recv.at[3 + 6 * a + k], sib).wait_recv()
                f = _rcopy(g_ref.at[s, mine], g_ref.at[s, mine], send.at[6 + 6 * a + k], recv.at[6 + 6 * a + k], sib)
                f.start()
                passed.append(f)
        for k, (px, py) in enumerate(chips):
            s = 2 * px + py
            _rcopy(s_ref, gs_ref.at[s], send.at[k], recv.at[k], sib).wait_recv()
            for a, g_ref in enumerate(g_refs):
                other = _half_rows(1 - c, g_ref.shape[1])
                _rcopy(g_ref.at[s, other], g_ref.at[s, other], send.at[6 + 6 * a + k], recv.at[6 + 6 * a + k], sib).wait_recv()
        for cp in first + passed:
            cp.wait_send()

    nsem = 3 + 6 * n
    return pl.pallas_call(
        kern, name="gather_weights", in_specs=[ANY] * (n + 1), out_specs=[ANY] * (n + 1),
        out_shape=[jax.ShapeDtypeStruct((NSHARD,) + m.shape, m.dtype) for m in mats]
        + [jax.ShapeDtypeStruct((NSHARD, SROWS, D), F32)],
        scratch_shapes=[pltpu.SemaphoreType.DMA((nsem,)), pltpu.SemaphoreType.DMA((nsem,))],
    )(*mats, small)


def _half_rows(c, rows):
    return pl.ds(pl.multiple_of(c * (rows // 2), 16), rows // 2)


def _swap_halves(gs):
    n = len(gs)

    def kern(*refs):
        g_refs, o_refs, (send, recv) = refs[:n], refs[n:2 * n], refs[2 * n:]
        x, y, c, _ = _place()
        cps = [_rcopy(g_ref.at[s, _half_rows(1 - c, g_ref.shape[1])], o_ref.at[s], send.at[NSHARD * a + s],
                      recv.at[NSHARD * a + s], (x, y, 1 - c))
               for a, (g_ref, o_ref) in enumerate(zip(g_refs, o_refs)) for s in range(NSHARD)]
        for cp in cps:
            cp.start()
        for cp in cps:
            cp.wait()

    return pl.pallas_call(
        kern, name="swap_halves", in_specs=[ANY] * n, out_specs=[ANY] * n,
        out_shape=[jax.ShapeDtypeStruct((NSHARD, g.shape[1] // 2, g.shape[2]), F32) for g in gs],
        scratch_shapes=[pltpu.SemaphoreType.DMA((NSHARD * n,)), pltpu.SemaphoreType.DMA((NSHARD * n,))],
    )(*gs)


def _add_halves(cidx, g, ra, rb, name):
    _, half, cols = ra.shape
    nb = half // rb

    def kern(c_ref, g_ref, a_ref, o_ref):
        o_ref[...] = (g_ref[...] + a_ref[...]).astype(BF16)

    return pl.pallas_call(
        kern, name=name,
        grid_spec=pltpu.PrefetchScalarGridSpec(
            num_scalar_prefetch=1, grid=(NSHARD, nb),
            in_specs=[pl.BlockSpec((None, rb, cols), lambda s, i, c: (s, c[0] * nb + i, 0)),
                      pl.BlockSpec((None, rb, cols), lambda s, i, c: (s, i, 0))],
            out_specs=pl.BlockSpec((None, rb, cols), lambda s, i, c: (s, i, 0))),
        out_shape=jax.ShapeDtypeStruct((NSHARD, half, cols), BF16),
        compiler_params=_cp(("parallel", "parallel")),
    )(cidx, g, ra)


def _exchange_chips(ps):
    n = len(ps)

    def kern(*refs):
        p_refs, o_refs, (send, recv) = refs[:n], refs[n:2 * n], refs[2 * n:]
        x, y, c, chips = _place()
        cps = [_rcopy(p_ref.at[2 * px + py], o_ref.at[k], send.at[3 * a + k], recv.at[3 * a + k], (px, py, c))
               for a, (p_ref, o_ref) in enumerate(zip(p_refs, o_refs)) for k, (px, py) in enumerate(chips)]
        for cp in cps:
            cp.start()
        for cp in cps:
            cp.wait()

    return pl.pallas_call(
        kern, name="exchange_chips", in_specs=[ANY] * n, out_specs=[ANY] * n,
        out_shape=[jax.ShapeDtypeStruct((3,) + p.shape[1:], p.dtype) for p in ps],
        scratch_shapes=[pltpu.SemaphoreType.DMA((3 * n,)), pltpu.SemaphoreType.DMA((3 * n,))],
    )(*ps)


def _add_chips(mc, g, ra, rx, rb, name):
    _, half, cols = ra.shape
    nb = half // rb

    def kern(m_ref, g_ref, a_ref, r0_ref, r1_ref, r2_ref, o_ref):
        own = g_ref[...] + a_ref[...]
        o_ref[...] = ((own + r0_ref[...].astype(F32)) + r1_ref[...].astype(F32)) + r2_ref[...].astype(F32)

    return pl.pallas_call(
        kern, name=name,
        grid_spec=pltpu.PrefetchScalarGridSpec(
            num_scalar_prefetch=1, grid=(nb,),
            in_specs=[pl.BlockSpec((None, rb, cols), lambda i, m: (m[0], m[1] * nb + i, 0)),
                      pl.BlockSpec((None, rb, cols), lambda i, m: (m[0], i, 0))]
            + [pl.BlockSpec((None, rb, cols), functools.partial(lambda i, m, k: (k, i, 0), k=k)) for k in range(3)],
            out_specs=pl.BlockSpec((rb, cols), lambda i, m: (i, 0))),
        out_shape=jax.ShapeDtypeStruct((half, cols), F32),
        compiler_params=_cp(("parallel",)),
    )(mc, g, ra, rx, rx, rx)


def _share_halves(rs):
    n = len(rs)

    def kern(*refs):
        r_refs, o_refs, (send, recv) = refs[:n], refs[n:2 * n], refs[2 * n:]
        x, y, c, _ = _place()
        cps = [_rcopy(r_ref, o_ref, send.at[a], recv.at[a], (x, y, 1 - c))
               for a, (r_ref, o_ref) in enumerate(zip(r_refs, o_refs))]
        for cp in cps:
            cp.start()
        for cp in cps:
            cp.wait()

    return pl.pallas_call(
        kern, name="share_halves", in_specs=[ANY] * n, out_specs=[ANY] * n,
        out_shape=[jax.ShapeDtypeStruct(r.shape, F32) for r in rs],
        scratch_shapes=[pltpu.SemaphoreType.DMA((n,)), pltpu.SemaphoreType.DMA((n,))],
    )(*rs)


SMALL_W = XBC


def _small_update(gs, ws, ms, vs):
    n = len(gs)
    widths = [g.shape[1] for g in gs]
    assert n <= SROWS and max(widths) <= SMALL_W

    def kern(*refs):
        g_refs, w_refs, m_refs, v_refs = (refs[n * i:n * (i + 1)] for i in range(4))
        o_g, o_d, o_m, o_v = (refs[n * (4 + i):n * (5 + i)] for i in range(4))
        buf, send, recv = refs[8 * n:]
        x, y, c, _ = _place()
        me = 4 * x + 2 * y + c
        buf[me] = jnp.zeros((SROWS, SMALL_W), F32)
        for k, g_ref in enumerate(g_refs):
            buf[me, k:k + 1, 0:widths[k]] = g_ref[...]
        cps = []
        for r in range(1, 8):
            peer = (1 - x if r & 4 else x, 1 - y if r & 2 else y, 1 - c if r & 1 else c)
            cps.append(_rcopy(buf.at[me], buf.at[me], send.at[r - 1], recv.at[r - 1], peer))
        for cp in cps:
            cp.start()
        for cp in cps:
            cp.wait()
        acc = buf[0]
        for i in range(1, 8):
            acc = acc + buf[i]
        for k in range(n):
            g_ = acc[k:k + 1, 0:widths[k]]
            m_ = ADAM_B1 * m_refs[k][...] + (1.0 - ADAM_B1) * g_
            v_ = ADAM_B2 * v_refs[k][...] + (1.0 - ADAM_B2) * jnp.square(g_)
            m_hat = m_ / (1.0 - ADAM_B1 ** ADAM_STEP)
            v_hat = v_ / (1.0 - ADAM_B2 ** ADAM_STEP)
            o_g[k][...] = g_
            o_d[k][...] = -ADAM_LR * (m_hat / (jnp.sqrt(v_hat) + ADAM_EPS) + ADAM_WD * w_refs[k][...])
            o_m[k][...] = m_
            o_v[k][...] = v_

    vm = pl.BlockSpec(memory_space=pltpu.VMEM)
    outs = pl.pallas_call(
        kern, name="small_update", in_specs=[vm] * (4 * n), out_specs=[vm] * (4 * n),
        out_shape=[jax.ShapeDtypeStruct((1, wd), F32) for _ in range(4) for wd in widths],
        scratch_shapes=[pltpu.VMEM((8, SROWS, SMALL_W), F32), pltpu.SemaphoreType.DMA((7,)), pltpu.SemaphoreType.DMA((7,))],
    )(*gs, *ws, *ms, *vs)
    return [outs[n * i:n * (i + 1)] for i in range(4)]


def _adamw(g, w, m, v, rb, name):
    rows, cols = g.shape

    def kern(g_ref, w_ref, m_ref, v_ref, d_ref, nm_ref, nv_ref):
        g_ = g_ref[...]
        m_ = ADAM_B1 * m_ref[...] + (1.0 - ADAM_B1) * g_
        v_ = ADAM_B2 * v_ref[...] + (1.0 - ADAM_B2) * jnp.square(g_)
        m_hat = m_ / (1.0 - ADAM_B1 ** ADAM_STEP)
        v_hat = v_ / (1.0 - ADAM_B2 ** ADAM_STEP)
        d_ref[...] = -ADAM_LR * (m_hat / (jnp.sqrt(v_hat) + ADAM_EPS) + ADAM_WD * w_ref[...])
        nm_ref[...] = m_
        nv_ref[...] = v_

    assert rows % rb == 0
    blk = pl.BlockSpec((rb, cols), lambda i: (i, 0))
    return pl.pallas_call(
        kern, name=name, grid=(rows // rb,), in_specs=[blk] * 4, out_specs=[blk] * 3,
        out_shape=[jax.ShapeDtypeStruct((rows, cols), F32)] * 3,
        compiler_params=_cp(("parallel",)),
    )(g, w, m, v)


def _adamw_halves(cidx, mine, other, w, m, v, rb, name):
    rows, cols = w.shape
    nbh = rows // 2 // rb

    def kern(c_ref, a_ref, b_ref, w_ref, m_ref, v_ref, g_ref, d_ref, nm_ref, nv_ref):
        g_ = jnp.where(pl.program_id(0) // nbh == c_ref[0], a_ref[...], b_ref[...])
        m_ = ADAM_B1 * m_ref[...] + (1.0 - ADAM_B1) * g_
        v_ = ADAM_B2 * v_ref[...] + (1.0 - ADAM_B2) * jnp.square(g_)
        m_hat = m_ / (1.0 - ADAM_B1 ** ADAM_STEP)
        v_hat = v_ / (1.0 - ADAM_B2 ** ADAM_STEP)
        g_ref[...] = g_
        d_ref[...] = -ADAM_LR * (m_hat / (jnp.sqrt(v_hat) + ADAM_EPS) + ADAM_WD * w_ref[...])
        nm_ref[...] = m_
        nv_ref[...] = v_

    half = pl.BlockSpec((rb, cols), lambda i, c: (i % nbh, 0))
    blk = pl.BlockSpec((rb, cols), lambda i, c: (i, 0))
    return pl.pallas_call(
        kern, name=name,
        grid_spec=pltpu.PrefetchScalarGridSpec(num_scalar_prefetch=1, grid=(2 * nbh,), in_specs=[half, half, blk, blk, blk],
                                               out_specs=[blk] * 4),
        out_shape=[jax.ShapeDtypeStruct((rows, cols), F32)] * 4,
        compiler_params=_cp(("parallel",)),
    )(cidx, mine, other, w, m, v)


WEIGHTS = ("c_ctx", "w_mod", "b_mod", "norm_w", "w_in", "ssm_conv_w", "ssm_conv_b", "dt_bias", "a_log", "d_skip",
           "ssm_norm_w", "w_out_ssm", "conf_conv_w", "conf_conv_b", "conf_ln_w", "conf_ln_b", "w_out_conf", "w_out",
           "final_norm_w")


def kernel(x, c, ctx, c_ctx, w_mod, b_mod, norm_w, w_in, ssm_conv_w, ssm_conv_b, dt_bias, a_log, d_skip, ssm_norm_w, w_out_ssm, conf_conv_w, conf_conv_b, conf_ln_w, conf_ln_b, w_out_conf, w_out, final_norm_w, loss_target, m_c_ctx, m_w_mod, m_b_mod, m_norm_w, m_w_in, m_ssm_conv_w, m_ssm_conv_b, m_dt_bias, m_a_log, m_d_skip, m_ssm_norm_w, m_w_out_ssm, m_conf_conv_w, m_conf_conv_b, m_conf_ln_w, m_conf_ln_b, m_w_out_conf, m_w_out, m_final_norm_w, v_c_ctx, v_w_mod, v_b_mod, v_norm_w, v_w_in, v_ssm_conv_w, v_ssm_conv_b, v_dt_bias, v_a_log, v_d_skip, v_ssm_norm_w, v_w_out_ssm, v_conf_conv_w, v_conf_conv_b, v_conf_ln_w, v_conf_ln_b, v_w_out_conf, v_w_out, v_final_norm_w):
    wv = (c_ctx, w_mod, b_mod, norm_w, w_in, ssm_conv_w, ssm_conv_b, dt_bias, a_log, d_skip, ssm_norm_w, w_out_ssm,
          conf_conv_w, conf_conv_b, conf_ln_w, conf_ln_b, w_out_conf, w_out, final_norm_w)
    mv = (m_c_ctx, m_w_mod, m_b_mod, m_norm_w, m_w_in, m_ssm_conv_w, m_ssm_conv_b, m_dt_bias, m_a_log, m_d_skip,
          m_ssm_norm_w, m_w_out_ssm, m_conf_conv_w, m_conf_conv_b, m_conf_ln_w, m_conf_ln_b, m_w_out_conf, m_w_out,
          m_final_norm_w)
    vv = (v_c_ctx, v_w_mod, v_b_mod, v_norm_w, v_w_in, v_ssm_conv_w, v_ssm_conv_b, v_dt_bias, v_a_log, v_d_skip,
          v_ssm_norm_w, v_w_out_ssm, v_conf_conv_w, v_conf_conv_b, v_conf_ln_w, v_conf_ln_b, v_w_out_conf, v_w_out,
          v_final_norm_w)
    shapes = {n: a.shape for n, a in zip(WEIGHTS, wv)}

    def squeeze(d):
        return {n: (a if n in ("c_ctx", "final_norm_w") else a[0]) for n, a in d.items()}

    w, m, v = (squeeze(dict(zip(WEIGHTS, t))) for t in (wv, mv, vv))

    my_chip = 2 * lax.axis_index("x") + lax.axis_index("y")
    my_core = lax.axis_index("c")

    pw = _pack_shard(w)
    pwb, wib, psm = pw.astype(BF16), w["w_in"].astype(BF16), pw[O_SC:O_SC + SROWS]
    gm, gw, gs = _gather_weights([pwb, wib], psm)
    mine = (jnp.arange(NSHARD) == my_chip)[:, None, None]
    gm, gw, gs = jnp.where(mine, pwb[None], gm), jnp.where(mine, wib[None], gw), jnp.where(mine, psm[None], gs)
    full = dict(w)
    full.update(_unpack_gathered(gm, gw, gs))

    lsum, grad_x, g = _local_step(x[0], c, ctx[0], loss_target[0], full)
    loss = lax.psum(lsum, ("x", "y", "c"))

    cidx = my_core.astype(jnp.int32).reshape(1)
    mc = jnp.stack([my_chip, my_core]).astype(jnp.int32)
    gsrc = [_pack_full(g), g["w_in_shards"]]
    blocks = (RB, WB)
    sib = _swap_halves(gsrc)
    part = [_add_halves(cidx, a, b, rb, "add_halves_%d" % i) for i, (a, b, rb) in enumerate(zip(gsrc, sib, blocks))]
    far = _exchange_chips(part)
    red = [_add_chips(mc, a, b, f, rb, "add_chips_%d" % i) for i, (a, b, f, rb) in enumerate(zip(gsrc, sib, far, blocks))]
    got = _share_halves(red)
    g_pk = jnp.concatenate([jnp.where(my_core == 0, red[0], got[0]), jnp.where(my_core == 0, got[0], red[0])], axis=0)
    small = [name for name, _ in SMALL]
    as_row = lambda d: [d[name].reshape(1, -1) for name in small]
    res_sm = _small_update(as_row(g), as_row(w), as_row(m), as_row(v))

    gr = {"w_mod": g_pk[O_MOD:O_OS].reshape(D, R_MOD), "rest": g_pk[O_OS:PUSED]}
    wr, mr, vr = ({"w_mod": t["w_mod"], "rest": _pack_rest(t)} for t in (w, m, v))
    res = {k: _adamw(gr[k], wr[k], mr[k], vr[k], rb, "adamw_" + k)
           for k, rb in (("w_mod", 512), ("rest", (PUSED - O_OS) // 2))}
    gr["w_in"], *res["w_in"] = _adamw_halves(cidx, red[1], got[1], w["w_in"], m["w_in"], v["w_in"], WB, "adamw_w_in")

    outs = []
    for i in range(4):
        pick = (lambda k: gr[k]) if i == 0 else (lambda k: res[k][i - 1])
        d = {"w_mod": pick("w_mod")[None], "w_in": pick("w_in")[None]}
        d.update(_unpack_rest(pick("rest")))
        d.update({name: a.reshape(shapes[name]) for name, a in zip(small, res_sm[i])})
        outs.extend(d[n] for n in WEIGHTS)
    return (loss, grad_x[None], *outs)
```
